```python
import math
import jax
import jax.numpy as jnp
from jax import lax
import numpy as np

D_MODEL = 1024
BATCH = 4
SEQ = 4096
DEPTH = 2

CTX_LEN = 256
GRID_W = 64
N_MIXERS = 4
MIX_W = D_MODEL
GROUP_W = MIX_W // N_MIXERS
HEAD_DIM = 64
ROPE_FREQS = HEAD_DIM // 4
ROPE_BASE = 10000.0
Q_BLOCK = 128
EPS = 1e-6
S5_CH = 16
S5_GROUPS = GROUP_W // S5_CH
S5_STATE = 64
GA_HEADS = GROUP_W // HEAD_DIM
GA_KV = GA_HEADS // 2
SSD_HEADDIM = 64
SSD_HEADS = GROUP_W // SSD_HEADDIM
SSD_NGROUPS = 2
SSD_STATE = 128
SSD_CHUNK = 128
SSD_CONV = 3
SSD_XBC = GROUP_W + 2 * SSD_NGROUPS * SSD_STATE
WA_HEADS = GROUP_W // HEAD_DIM
WA_KV = WA_HEADS // 2
WINDOW = 128
MOE_GROUPS = 4
MOE_PER_GROUP = 8
N_EXPERTS = MOE_GROUPS * MOE_PER_GROUP
MOE_TOPK = 2
D_EXPERT = D_MODEL // 2
MOE_BLOCK = 128
SPLITS = (GROUP_W, GA_HEADS * HEAD_DIM, GA_KV * HEAD_DIM, GA_KV * HEAD_DIM, GROUP_W, SSD_XBC, 2 * SSD_HEADS, WA_HEADS * HEAD_DIM, WA_KV * HEAD_DIM, WA_KV * HEAD_DIM)
D_IN = sum(SPLITS)

kernel_name = 'hybrid_parallel_heads_dit_block'


def rmsnorm(x, g):
    xf = x.astype(jnp.float32)
    y = xf * lax.rsqrt(jnp.mean(xf * xf, axis=-1, keepdims=True) + EPS)
    return (y * g.astype(jnp.float32)).astype(x.dtype)


def _modulate(h, shift, scale):
    return h * (1.0 + scale) + shift


def _heads(t):
    return t.reshape(t.shape[0], t.shape[1], -1, HEAD_DIM)


def _flip(t, rev):
    return jnp.flip(t, axis=1) if rev else t


def rope_tables(n_tok, dtype):
    n_rows = n_tok // GRID_W
    rows = jnp.repeat(jnp.arange(n_rows), GRID_W)
    cols = jnp.tile(jnp.arange(GRID_W), n_rows)
    inv = jnp.power(ROPE_BASE, -jnp.arange(ROPE_FREQS, dtype=jnp.float32) / ROPE_FREQS)
    ang = jnp.stack([rows, cols], axis=-1).astype(jnp.float32)[..., None] * inv
    return jnp.cos(ang).astype(dtype), jnp.sin(ang).astype(dtype)


def apply_rope(x, cos, sin):
    shp = x.shape
    xv = x.reshape(*shp[:-1], 2, 2, ROPE_FREQS)
    rot = jnp.stack([-xv[..., 1, :], xv[..., 0, :]], axis=-2)
    c = cos[:, None, :, None, :]
    s = sin[:, None, :, None, :]
    return (xv * c + rot * s).reshape(shp)


def s5_discretize(lam_re, lam_im, log_dt, b_re, b_im):
    lr = lam_re.astype(jnp.float32)
    li = lam_im.astype(jnp.float32)
    dt = jnp.exp(log_dt.astype(jnp.float32))[:, None]
    mag = jnp.exp(lr * dt)
    a_re = mag * jnp.cos(li * dt)
    a_im = mag * jnp.sin(li * dt)
    den = lr * lr + li * li
    f_re = ((a_re - 1.0) * lr + a_im * li) / den
    f_im = (a_im * lr - (a_re - 1.0) * li) / den
    br = b_re.astype(jnp.float32)
    bi = b_im.astype(jnp.float32)
    bb_re = f_re[..., None] * br - f_im[..., None] * bi
    bb_im = f_re[..., None] * bi + f_im[..., None] * br
    return a_re, a_im, bb_re, bb_im


def _complex_affine_op(e1, e2):
    ar1, ai1, br1, bi1 = e1
    ar2, ai2, br2, bi2 = e2
    return (ar2 * ar1 - ai2 * ai1, ar2 * ai1 + ai2 * ar1,
            ar2 * br1 - ai2 * bi1 + br2, ar2 * bi1 + ai2 * br1 + bi2)


def s5_scan(bu_re, bu_im, a_re, a_im, h0, rev):
    if h0 is not None:
        first = -1 if rev else 0
        bu_re = bu_re.at[:, first].add(a_re * h0[0] - a_im * h0[1])
        bu_im = bu_im.at[:, first].add(a_re * h0[1] + a_im * h0[0])
    ar = jnp.broadcast_to(a_re, bu_re.shape)
    ai = jnp.broadcast_to(a_im, bu_re.shape)
    _, _, hr, hi = lax.associative_scan(_complex_affine_op, (ar, ai, bu_re, bu_im), reverse=rev, axis=1)
    return hr, hi


def _s5_readout(h, c_re, c_im):
    return jnp.einsum('blgp,gcp->blgc', h[0], c_re) - jnp.einsum('blgp,gcp->blgc', h[1], c_im)


def s5_mixer(u_l, u_c, lam_re, lam_im, log_dt, b_re, b_im, c_re, c_im, d_skip, glu_w, glu_b, ctx_out):
    def grp(u):
        return u.astype(jnp.float32).reshape(u.shape[0], u.shape[1], S5_GROUPS, S5_CH)
    ul, uc = grp(u_l), grp(u_c)
    dsk = d_skip.astype(jnp.float32).reshape(S5_GROUPS, S5_CH)
    y_l = ul * dsk
    y_c = uc * dsk
    for d in range(2):
        rev = d == 1
        a_re, a_im, bb_re, bb_im = s5_discretize(lam_re[d], lam_im[d], log_dt[d], b_re[d], b_im[d])
        cr = c_re[d].astype(jnp.float32)
        ci = c_im[d].astype(jnp.float32)
        hc = s5_scan(jnp.einsum('blgc,gpc->blgp', uc, bb_re), jnp.einsum('blgc,gpc->blgp', uc, bb_im), a_re, a_im, None, rev)
        last = 0 if rev else -1
        h_fin = (hc[0][:, last], hc[1][:, last])
        hl = s5_scan(jnp.einsum('blgc,gpc->blgp', ul, bb_re), jnp.einsum('blgc,gpc->blgp', ul, bb_im), a_re, a_im, h_fin, rev)
        y_l = y_l + _s5_readout(hl, cr, ci)
        if ctx_out:
            y_c = y_c + _s5_readout(hc, cr, ci)

    def glu(y, like):
        g = jax.nn.gelu(y.reshape(y.shape[0], y.shape[1], GROUP_W)).astype(like.dtype)
        return g * jax.nn.sigmoid(g @ glu_w + glu_b)
    return glu(y_l, u_l), (glu(y_c, u_c) if ctx_out else None)


def _attend(qb, k, v):
    s = jnp.einsum('bqkgd,bskd->bkgqs', qb, k).astype(jnp.float32) * (HEAD_DIM ** -0.5)
    p = jax.nn.softmax(s, axis=-1).astype(v.dtype)
    return jnp.einsum('bkgqs,bskd->bqkgd', p, v)


def gqa_global(q_l, k_l, v_l, q_c, k_c, v_c, qn_g, kn_g, cos, sin, ctx_out):
    bsz, L = q_l.shape[:2]
    grp = GA_HEADS // GA_KV
    q_l = apply_rope(rmsnorm(q_l, qn_g), cos, sin)
    k_l = apply_rope(rmsnorm(k_l, kn_g), cos, sin)
    q_c = rmsnorm(q_c, qn_g)
    k_c = rmsnorm(k_c, kn_g)
    k_all = jnp.concatenate([k_c, k_l], axis=1)
    v_all = jnp.concatenate([v_c, v_l], axis=1)
    nb = L // Q_BLOCK
    qb = q_l.reshape(bsz, nb, Q_BLOCK, GA_KV, grp, HEAD_DIM).swapaxes(0, 1)
    o = lax.map(lambda qi: _attend(qi, k_all, v_all), qb)
    y_l = o.swapaxes(0, 1).reshape(bsz, L, GROUP_W)
    y_c = None
    if ctx_out:
        lc = q_c.shape[1]
        y_c = _attend(q_c.reshape(bsz, lc, GA_KV, grp, HEAD_DIM), k_c, v_c).reshape(bsz, lc, GROUP_W)
    return y_l, y_c


def depthwise_conv(x, w, b):
    pad = (SSD_CONV - 1) // 2
    y = lax.conv_general_dilated(x, w[:, None, :].astype(x.dtype), window_strides=(1,),
                                 padding=[(pad, SSD_CONV - 1 - pad)],
                                 dimension_numbers=('NWC', 'WIO', 'NWC'),
                                 feature_group_count=x.shape[-1])
    return y + b


def _segsum(a):
    cs = jnp.cumsum(a, axis=-1)
    diff = cs[..., :, None] - cs[..., None, :]
    n = a.shape[-1]
    return jnp.where(jnp.tril(jnp.ones((n, n), dtype=bool)), diff, -jnp.inf)


def ssd_chunked(X, A, Bh, Ch, h0):
    b, l, h, p = X.shape
    n = Bh.shape[-1]
    nc = l // SSD_CHUNK
    X = X.astype(jnp.float32).reshape(b, nc, SSD_CHUNK, h, p)
    Bc = Bh.astype(jnp.float32).reshape(b, nc, SSD_CHUNK, h, n)
    Cc = Ch.astype(jnp.float32).reshape(b, nc, SSD_CHUNK, h, n)
    A = A.astype(jnp.float32).reshape(b, nc, SSD_CHUNK, h).transpose(0, 3, 1, 2)
    A_cum = jnp.cumsum(A, axis=-1)
    scores = jnp.einsum('bclhn,bcshn->bhcls', Cc, Bc) * jnp.exp(_segsum(A))
    y_diag = jnp.einsum('bhcls,bcshp->bclhp', scores, X)
    decay_states = jnp.exp(A_cum[..., -1:] - A_cum)
    states = jnp.einsum('bclhn,bhcl,bclhp->bchpn', Bc, decay_states, X)
    states = jnp.concatenate([h0[:, None], states], axis=1)
    decay_chunk = jnp.exp(_segsum(jnp.pad(A_cum[..., -1], ((0, 0), (0, 0), (1, 0)))))
    states = jnp.einsum('bhzc,bchpn->bzhpn', decay_chunk, states)
    y_off = jnp.einsum('bclhn,bchpn,bhcl->bclhp', Cc, states[:, :-1], jnp.exp(A_cum))
    return (y_diag + y_off).reshape(b, l, h, p), states[:, -1]


def _ssd_dir(x, bm, cm, dt, a, h0, rev):
    y, h_fin = ssd_chunked(_flip(x * dt[..., None], rev), _flip(dt * a, rev), _flip(bm, rev), _flip(cm, rev), h0)
    return _flip(y, rev), h_fin


def ssd_mixer(z_l, xbc_l, dt_l, z_c, xbc_c, dt_c, conv_w, conv_b, dt_bias, a_log, d_skip, norm_g, ctx_out):
    rep = SSD_HEADS // SSD_NGROUPS

    def prep(xbc, dtr):
        xbc = jax.nn.silu(depthwise_conv(xbc, conv_w, conv_b))
        bsz, L = xbc.shape[:2]
        x, bm, cm = jnp.split(xbc, [GROUP_W, GROUP_W + SSD_NGROUPS * SSD_STATE], axis=-1)
        x = x.reshape(bsz, L, SSD_HEADS, SSD_HEADDIM).astype(jnp.float32)
        bm = jnp.repeat(bm.reshape(bsz, L, SSD_NGROUPS, SSD_STATE), rep, axis=2)
        cm = jnp.repeat(cm.reshape(bsz, L, SSD_NGROUPS, SSD_STATE), rep, axis=2)
        dt = jax.nn.softplus(dtr.astype(jnp.float32).reshape(bsz, L, 2, SSD_HEADS) + dt_bias.astype(jnp.float32))
        return x, bm, cm, dt

    xl, bl, cl, dtl = prep(xbc_l, dt_l)
    xc, bc, cc, dtc = prep(xbc_c, dt_c)
    a = -jnp.exp(a_log.astype(jnp.float32))
    dsk = d_skip.astype(jnp.float32)[:, None]
    y_l = xl * dsk
    y_c = xc * dsk
    bsz = xl.shape[0]
    for d in range(2):
        rev = d == 1
        h0 = jnp.zeros((bsz, SSD_HEADS, SSD_HEADDIM, SSD_STATE), jnp.float32)
        yc_d, h_ctx = _ssd_dir(xc, bc, cc, dtc[:, :, d], a[d], h0, rev)
        yl_d, _ = _ssd_dir(xl, bl, cl, dtl[:, :, d], a[d], h_ctx, rev)
        y_l = y_l + yl_d
        if ctx_out:
            y_c = y_c + yc_d

    def gate(y, z):
        y = y.reshape(z.shape).astype(z.dtype)
        return rmsnorm(y * jax.nn.silu(z), norm_g)
    return gate(y_l, z_l), (gate(y_c, z_c) if ctx_out else None)


def _band(t, nb):
    bsz = t.shape[0]
    tp = jnp.pad(t.reshape(bsz, nb, Q_BLOCK, WA_KV, HEAD_DIM), ((0, 0), (1, 1), (0, 0), (0, 0), (0, 0)))
    return jnp.concatenate([tp[:, :-2], tp[:, 1:-1], tp[:, 2:]], axis=2)


def gqa_window(q_l, k_l, v_l, q_c, k_c, v_c, sink, cos, sin, ctx_out):
    bsz, L = q_l.shape[:2]
    lc = k_c.shape[1]
    grp = WA_HEADS // WA_KV
    scale = HEAD_DIM ** -0.5
    nb = L // Q_BLOCK
    q_l = apply_rope(q_l, cos, sin)
    k_l = apply_rope(k_l, cos, sin)
    qb = q_l.reshape(bsz, nb, Q_BLOCK, WA_KV, grp, HEAD_DIM)
    kb, vb = _band(k_l, nb), _band(v_l, nb)
    qpos = jnp.arange(L).reshape(nb, Q_BLOCK)
    kpos = (jnp.arange(nb)[:, None] - 1) * Q_BLOCK + jnp.arange(3 * Q_BLOCK)[None, :]
    valid = ((jnp.abs(qpos[:, :, None] - kpos[:, None, :]) <= WINDOW)
             & (kpos >= 0)[:, None, :] & (kpos < L)[:, None, :])
    s_band = jnp.einsum('bnqkgd,bnskd->bnkgqs', qb, kb).astype(jnp.float32) * scale
    s_band = jnp.where(valid[None, :, None, None], s_band, -jnp.inf)
    s_ctx = jnp.einsum('bnqkgd,bskd->bnkgqs', qb, k_c).astype(jnp.float32) * scale
    sink_f = sink.astype(jnp.float32).reshape(WA_KV, grp, 1, 1)
    sink_l = jnp.broadcast_to(sink_f, s_ctx.shape[:-1] + (1,))
    p = jax.nn.softmax(jnp.concatenate([sink_l, s_ctx, s_band], axis=-1), axis=-1).astype(v_l.dtype)
    o = (jnp.einsum('bnkgqs,bskd->bnqkgd', p[..., 1:1 + lc], v_c)
         + jnp.einsum('bnkgqs,bnskd->bnqkgd', p[..., 1 + lc:], vb))
    y_l = o.reshape(bsz, L, GROUP_W)
    y_c = None
    if ctx_out:
        qc = q_c.reshape(bsz, lc, WA_KV, grp, HEAD_DIM)
        s_cc = jnp.einsum('bqkgd,bskd->bkgqs', qc, k_c).astype(jnp.float32) * scale
        sink_c = jnp.broadcast_to(sink_f, s_cc.shape[:-1] + (1,))
        pc = jax.nn.softmax(jnp.concatenate([sink_c, s_cc], axis=-1), axis=-1).astype(v_c.dtype)
        y_c = jnp.einsum('bkgqs,bskd->bqkgd', pc[..., 1:], v_c).reshape(bsz, lc, GROUP_W)
    return y_l, y_c


def _swiglu(xb, wg, wu, wd):
    return (jax.nn.silu(xb @ wg) * (xb @ wu)) @ wd


def hier_moe(xf, coarse_w, coarse_b, fine_w, fine_b, w_gate, w_up, w_down):
    T, dm = xf.shape
    pc = jax.nn.softmax((xf @ coarse_w + coarse_b).astype(jnp.float32), axis=-1)
    pg, g = lax.top_k(pc, 1)
    fl = (xf @ fine_w + fine_b).astype(jnp.float32).reshape(T, MOE_GROUPS, MOE_PER_GROUP)
    fl = jnp.take_along_axis(fl, g[:, :, None], axis=1)[:, 0]
    tv, ti = lax.top_k(fl, MOE_TOPK)
    w = pg * jax.nn.softmax(tv, axis=-1)
    flat_e = (g * MOE_PER_GROUP + ti).reshape(-1)
    flat_w = w.reshape(-1)
    flat_tok = jnp.repeat(jnp.arange(T), MOE_TOPK)
    n_slots = T * MOE_TOPK
    order = jnp.argsort(flat_e)
    se, stok, sw = flat_e[order], flat_tok[order], flat_w[order]
    counts = jnp.zeros((N_EXPERTS,), jnp.int32).at[flat_e].add(1)
    starts = jnp.cumsum(counts) - counts
    pcounts = (counts + MOE_BLOCK - 1) // MOE_BLOCK * MOE_BLOCK
    pends = jnp.cumsum(pcounts)
    pstarts = pends - pcounts
    dest = pstarts[se] + jnp.arange(n_slots) - starts[se]
    n_blocks = -(-n_slots // MOE_BLOCK) + N_EXPERTS
    buf = jnp.zeros((n_blocks * MOE_BLOCK, dm), xf.dtype).at[dest].set(xf[stok])
    blk_e = jnp.minimum(jnp.searchsorted(pends, jnp.arange(n_blocks) * MOE_BLOCK, side='right'), N_EXPERTS - 1)
    out = lax.map(lambda a: _swiglu(a[0], w_gate[a[1]], w_up[a[1]], w_down[a[1]]),
                  (buf.reshape(n_blocks, MOE_BLOCK, dm), blk_e))
    out = out.reshape(-1, dm)[dest] * sw[:, None].astype(xf.dtype)
    return jax.ops.segment_sum(out, stok, num_segments=T)


def hybrid_layer(xl, xc, c, c_ctx, cos, sin, ada_w, ada_b, norm1_g, norm2_g, w_in, w_out,
                 s5_lam_re, s5_lam_im, s5_log_dt, s5_b_re, s5_b_im, s5_c_re, s5_c_im, s5_d, s5_glu_w, s5_glu_b,
                 ga_qn_g, ga_kn_g, ssd_conv_w, ssd_conv_b, ssd_dt_bias, ssd_a_log, ssd_d, ssd_norm_g, wa_sink,
                 moe_coarse_w, moe_coarse_b, moe_fine_w, moe_fine_b, moe_w_gate, moe_w_up, moe_w_down, ctx_out):
    dm = xl.shape[-1]
    mod_l = (jax.nn.silu(c) @ ada_w + ada_b)[:, None, :]
    mod_c = jax.nn.silu(c_ctx) @ ada_w + ada_b
    sh1, sc1, g1, sh2, sc2, g2 = jnp.split(mod_l, 6, axis=-1)
    csh1, csc1, cg1, csh2, csc2, cg2 = jnp.split(mod_c, 6, axis=-1)
    cuts = [int(v) for v in np.cumsum(SPLITS)[:-1]]
    pl = jnp.split(_modulate(rmsnorm(xl, norm1_g), sh1, sc1) @ w_in, cuts, axis=-1)
    pc = jnp.split(_modulate(rmsnorm(xc, norm1_g), csh1, csc1) @ w_in, cuts, axis=-1)

    a_l, a_c = s5_mixer(pl[0], pc[0], s5_lam_re, s5_lam_im, s5_log_dt, s5_b_re, s5_b_im, s5_c_re, s5_c_im,
                        s5_d, s5_glu_w, s5_glu_b, ctx_out)
    b_l, b_c = gqa_global(_heads(pl[1]), _heads(pl[2]), _heads(pl[3]), _heads(pc[1]), _heads(pc[2]), _heads(pc[3]),
                          ga_qn_g, ga_kn_g, cos, sin, ctx_out)
    m_l, m_c = ssd_mixer(pl[4], pl[5], pl[6], pc[4], pc[5], pc[6], ssd_conv_w, ssd_conv_b, ssd_dt_bias,
                         ssd_a_log, ssd_d, ssd_norm_g, ctx_out)
    w_l, w_c = gqa_window(_heads(pl[7]), _heads(pl[8]), _heads(pl[9]), _heads(pc[7]), _heads(pc[8]), _heads(pc[9]),
                          wa_sink, cos, sin, ctx_out)

    xl = xl + g1 * (jnp.concatenate([a_l, b_l, m_l, w_l], axis=-1) @ w_out)
    hl = _modulate(rmsnorm(xl, norm2_g), sh2, sc2)
    n_lat = hl.shape[0] * hl.shape[1]
    if ctx_out:
        xc = xc + cg1 * (jnp.concatenate([a_c, b_c, m_c, w_c], axis=-1) @ w_out)
        hc = _modulate(rmsnorm(xc, norm2_g), csh2, csc2)
        tokens = jnp.concatenate([hl.reshape(-1, dm), hc.reshape(-1, dm)], axis=0)
    else:
        tokens = hl.reshape(-1, dm)
    f = hier_moe(tokens, moe_coarse_w, moe_coarse_b, moe_fine_w, moe_fine_b, moe_w_gate, moe_w_up, moe_w_down)
    xl = xl + g2 * f[:n_lat].reshape(xl.shape)
    if ctx_out:
        xc = xc + cg2 * f[n_lat:].reshape(xc.shape)
        return xl, xc
    return xl, None


def setup_inputs(seed: int = 0) -> dict:
    key = jax.random.key(seed)
    ks = iter(jax.random.split(key, 64))
    f32 = jnp.float32
    Dm = D_MODEL

    def nrm(shape, s=1.0):
        return jax.random.normal(next(ks), shape, f32) * s

    def unif(shape, lo, hi):
        return jax.random.uniform(next(ks), shape, f32, lo, hi)

    inp = {}
    inp['x'] = nrm((BATCH, SEQ, Dm))
    inp['c'] = nrm((BATCH, Dm))
    inp['ctx'] = nrm((BATCH, CTX_LEN, Dm))
    inp['c_ctx'] = nrm((Dm,))
    inp['ada_w'] = nrm((DEPTH, Dm, 6 * Dm), 0.5 * Dm ** -0.5)
    inp['ada_b'] = nrm((DEPTH, 6 * Dm), 0.02)
    inp['norm1_g'] = 1.0 + nrm((DEPTH, Dm), 0.05)
    inp['norm2_g'] = 1.0 + nrm((DEPTH, Dm), 0.05)
    inp['w_in'] = nrm((DEPTH, Dm, D_IN), Dm ** -0.5)
    inp['w_out'] = nrm((DEPTH, MIX_W, Dm), MIX_W ** -0.5)
    sshape = (DEPTH, 2, S5_GROUPS, S5_STATE)
    inp['s5_lam_re'] = -0.5 + nrm(sshape, 0.01)
    inp['s5_lam_im'] = jnp.pi * jnp.arange(S5_STATE, dtype=f32) + nrm(sshape, 0.01)
    inp['s5_log_dt'] = unif((DEPTH, 2, S5_GROUPS), math.log(1e-3), math.log(1e-1))
    inp['s5_b_re'] = nrm((DEPTH, 2, S5_GROUPS, S5_STATE, S5_CH), (2 * S5_CH) ** -0.5)
    inp['s5_b_im'] = nrm((DEPTH, 2, S5_GROUPS, S5_STATE, S5_CH), (2 * S5_CH) ** -0.5)
    inp['s5_c_re'] = nrm((DEPTH, 2, S5_GROUPS, S5_CH, S5_STATE), S5_STATE ** -0.5)
    inp['s5_c_im'] = nrm((DEPTH, 2, S5_GROUPS, S5_CH, S5_STATE), S5_STATE ** -0.5)
    inp['s5_d'] = nrm((DEPTH, GROUP_W))
    inp['s5_glu_w'] = nrm((DEPTH, GROUP_W, GROUP_W), GROUP_W ** -0.5)
    inp['s5_glu_b'] = nrm((DEPTH, GROUP_W), 0.02)
    inp['ga_qn_g'] = 1.0 + nrm((DEPTH, HEAD_DIM), 0.05)
    inp['ga_kn_g'] = 1.0 + nrm((DEPTH, HEAD_DIM), 0.05)
    inp['ssd_conv_w'] = nrm((DEPTH, SSD_CONV, SSD_XBC), SSD_CONV ** -0.5)
    inp['ssd_conv_b'] = nrm((DEPTH, SSD_XBC), 0.02)
    dt0 = jnp.exp(unif((DEPTH, 2, SSD_HEADS), math.log(1e-3), math.log(1e-1)))
    inp['ssd_dt_bias'] = dt0 + jnp.log(-jnp.expm1(-dt0))
    inp['ssd_a_log'] = jnp.log(unif((DEPTH, 2, SSD_HEADS), 1.0, 16.0))
    inp['ssd_d'] = 1.0 + nrm((DEPTH, SSD_HEADS), 0.1)
    inp['ssd_norm_g'] = 1.0 + nrm((DEPTH, GROUP_W), 0.05)
    inp['wa_sink'] = nrm((DEPTH, WA_HEADS))
    inp['moe_coarse_w'] = nrm((DEPTH, Dm, MOE_GROUPS), Dm ** -0.5)
    inp['moe_coarse_b'] = nrm((DEPTH, MOE_GROUPS), 0.01)
    inp['moe_fine_w'] = nrm((DEPTH, Dm, N_EXPERTS), Dm ** -0.5)
    inp['moe_fine_b'] = nrm((DEPTH, N_EXPERTS), 0.01)
    inp['moe_w_gate'] = nrm((DEPTH, N_EXPERTS, Dm, D_EXPERT), Dm ** -0.5)
    inp['moe_w_up'] = nrm((DEPTH, N_EXPERTS, Dm, D_EXPERT), Dm ** -0.5)
    inp['moe_w_down'] = nrm((DEPTH, N_EXPERTS, D_EXPERT, Dm), D_EXPERT ** -0.5)
    inp['final_g'] = 1.0 + nrm((Dm,), 0.05)
    return inp


def reference(x, c, ctx, c_ctx, ada_w, ada_b, norm1_g, norm2_g, w_in, w_out,
              s5_lam_re, s5_lam_im, s5_log_dt, s5_b_re, s5_b_im, s5_c_re, s5_c_im, s5_d, s5_glu_w, s5_glu_b,
              ga_qn_g, ga_kn_g, ssd_conv_w, ssd_conv_b, ssd_dt_bias, ssd_a_log, ssd_d, ssd_norm_g, wa_sink,
              moe_coarse_w, moe_coarse_b, moe_fine_w, moe_fine_b, moe_w_gate, moe_w_up, moe_w_down, final_g):
    cos, sin = rope_tables(x.shape[1], x.dtype)
    xl, xc = x, ctx
    for i in range(DEPTH):
        xl, xc = hybrid_layer(
            xl, xc, c, c_ctx, cos, sin, ada_w[i], ada_b[i], norm1_g[i], norm2_g[i], w_in[i], w_out[i],
            s5_lam_re[i], s5_lam_im[i], s5_log_dt[i], s5_b_re[i], s5_b_im[i], s5_c_re[i], s5_c_im[i],
            s5_d[i], s5_glu_w[i], s5_glu_b[i], ga_qn_g[i], ga_kn_g[i], ssd_conv_w[i], ssd_conv_b[i],
            ssd_dt_bias[i], ssd_a_log[i], ssd_d[i], ssd_norm_g[i], wa_sink[i],
            moe_coarse_w[i], moe_coarse_b[i], moe_fine_w[i], moe_fine_b[i], moe_w_gate[i], moe_w_up[i],
            moe_w_down[i], i < DEPTH - 1)
    return rmsnorm(xl, final_g)
```

```python
import functools
import math

import jax
import jax.numpy as jnp
import numpy as np
from jax import lax
from jax.experimental import pallas as pl
from jax.experimental.pallas import tpu as pltpu

F32 = jnp.float32
BF16 = jnp.bfloat16
HI = lax.Precision.HIGHEST

D_MODEL = 1024
GRID_W = 64
GROUP_W = 256
HEAD_DIM = 64
ROPE_FREQS = HEAD_DIM // 4
ROPE_BASE = 10000.0
EPS = 1e-6
S5_CH = 16
S5_GROUPS = GROUP_W // S5_CH
S5_STATE = 64
N_HEADS = 4
SSD_HEADS = 4
SSD_NGROUPS = 2
SSD_STATE = 128
SSD_XBC = GROUP_W + 2 * SSD_NGROUPS * SSD_STATE
WINDOW = 128
MOE_GROUPS = 4
MOE_PER_GROUP = 8
N_EXPERTS = 32
D_EXPERT = D_MODEL // 2

LANES = 128
SUBLANES = 8
TM = 256
TQ = 128
S5_Q = 32
S5_BLK = S5_Q * S5_CH
MOE_TM = 256
GATHER_ROWS = 512
ROUTE_FINE0 = 32
VMEM_LIMIT = 56 * 1024 * 1024

NEG_INF = float("-inf")


def _cp(sem, vmem=VMEM_LIMIT):
    return pltpu.CompilerParams(dimension_semantics=sem, vmem_limit_bytes=vmem)


def _dot(a, b):
    return jnp.dot(a, b, preferred_element_type=F32)


def _dot_hi(a, b):
    return jnp.dot(a, b, preferred_element_type=F32, precision=HI)


def _dot_nt(a, b):
    return lax.dot_general(a, b, (((1,), (1,)), ((), ())), preferred_element_type=F32)


def _sigmoid(x):
    return 1.0 / (1.0 + jnp.exp(-x))


def _silu(x):
    return x * _sigmoid(x)


def _gelu_tanh(x):
    return 0.5 * x * (1.0 + jnp.tanh(math.sqrt(2.0 / math.pi) * (x + 0.044715 * (x * x * x))))


def _softplus(x):
    return jnp.maximum(x, 0.0) + jnp.log1p(jnp.exp(-jnp.abs(x)))


def _per_head_cols(v, base, n_heads, shape):
    lane = lax.broadcasted_iota(jnp.int32, shape, 1)
    out = jnp.broadcast_to(v[:, base + n_heads - 1:base + n_heads], shape)
    for h in range(n_heads - 2, -1, -1):
        out = jnp.where(lane < (h + 1) * HEAD_DIM, v[:, base + h:base + h + 1], out)
    return out


def _ada_kernel(c_ref, w_ref, b_ref, o_ref):
    c = c_ref[...]
    o_ref[0] = _dot_hi(_silu(c), w_ref[0]) + b_ref[0]


def _ada(cc, ada_w, ada_b):
    depth, d, n = ada_w.shape
    tn = 1536
    return pl.pallas_call(
        _ada_kernel,
        out_shape=jax.ShapeDtypeStruct((depth, SUBLANES, n), F32),
        grid=(depth, n // tn),
        in_specs=[pl.BlockSpec((SUBLANES, d), lambda l, j: (0, 0)),
                  pl.BlockSpec((1, d, tn), lambda l, j: (l, 0, j)),
                  pl.BlockSpec((1, 1, tn), lambda l, j: (l, 0, j))],
        out_specs=pl.BlockSpec((1, SUBLANES, tn), lambda l, j: (l, 0, j)),
        compiler_params=_cp(("parallel", "parallel")),
        name="ada_mod",
    )(cc, ada_w, ada_b.reshape(depth, 1, n))


_C_XBC = 0
_C_U = _C_XBC + SSD_XBC
_C_Z = _C_U + GROUP_W
_C_DT = _C_Z + GROUP_W
_C_GAQ = _C_DT + LANES
_C_WAQ = _C_GAQ + N_HEADS * LANES
_C_GAK = _C_WAQ + N_HEADS * LANES
_C_GAV = _C_GAK + LANES
_C_WAK = _C_GAV + LANES
_C_WAV = _C_WAK + LANES
_C_END = _C_WAV + LANES


def _expand_q_cols(wq):
    d = wq.shape[0]
    out = jnp.zeros((d, N_HEADS, LANES), wq.dtype)
    for h in range(N_HEADS):
        o = HEAD_DIM * (h // 2)
        out = out.at[:, h, o:o + HEAD_DIM].set(wq[:, h * HEAD_DIM:(h + 1) * HEAD_DIM])
    return out.reshape(d, N_HEADS * LANES)


def _pack_w_in(w_in):
    cuts = np.cumsum([256, 256, 128, 128, 256, SSD_XBC, 2 * SSD_HEADS, 256, 128, 128])[:-1]
    u, gaq, gak, gav, z, xbc, dt, waq, wak, wav = jnp.split(w_in, [int(c) for c in cuts], axis=1)
    dt = jnp.pad(dt, ((0, 0), (0, LANES - dt.shape[1])))
    w = jnp.concatenate([xbc, u, z, dt, _expand_q_cols(gaq), _expand_q_cols(waq), gak, gav, wak, wav], axis=1)
    return w.astype(BF16)


def _rope(x, cos, sins):
    w = x.shape[1]
    if w > LANES:
        cos = jnp.concatenate([cos] * (w // LANES), axis=1)
        sins = jnp.concatenate([sins] * (w // LANES), axis=1)
    lane = lax.broadcasted_iota(jnp.int32, x.shape, 1)
    up = pltpu.roll(x, w - ROPE_FREQS, 1)
    dn = pltpu.roll(x, ROPE_FREQS, 1)
    partner = jnp.where((lane & ROPE_FREQS) == 0, up, dn)
    return x * cos + partner * sins


def _inproj_kernel(x_ref, mod_ref, g_ref, w_ref, cos_ref, sin_ref, qn_ref, kn_ref,
                   xbc_o, u_o, z_o, dt_o, gaq_o, gak_o, gav_o, waq_o, wak_o, wav_o):
    x = x_ref[...]
    ms = jnp.mean(x * x, axis=-1, keepdims=True)
    xn = x * lax.rsqrt(ms + EPS) * g_ref[...]
    h = xn * (1.0 + mod_ref[0, 1:2, :]) + mod_ref[0, 0:1, :]
    p = _dot(h.astype(BF16), w_ref[...])
    xbc_o[...] = p[:, _C_XBC:_C_U]
    u_o[...] = p[:, _C_U:_C_Z]
    z_o[...] = p[:, _C_Z:_C_DT]
    dt_o[...] = p[:, _C_DT:_C_GAQ]
    cos = cos_ref[...]
    sins = sin_ref[...]
    scale = HEAD_DIM ** -0.5
    q = p[:, _C_GAQ:_C_WAQ]
    qs = q * q
    inv = jnp.concatenate(
        [jnp.broadcast_to(lax.rsqrt(jnp.sum(qs[:, s * LANES:(s + 1) * LANES], axis=1, keepdims=True)
                                    * (1.0 / HEAD_DIM) + EPS), (q.shape[0], LANES)) for s in range(N_HEADS)], axis=1)
    gaq_o[...] = (_rope(q * inv * qn_ref[...], cos, sins) * scale).astype(BF16)
    waq_o[...] = (_rope(p[:, _C_WAQ:_C_GAK], cos, sins) * scale).astype(BF16)
    k = p[:, _C_GAK:_C_GAV]
    ks = k * k
    lane = lax.broadcasted_iota(jnp.int32, k.shape, 1)
    lo = lane < HEAD_DIM
    ms0 = jnp.sum(jnp.where(lo, ks, 0.0), axis=1, keepdims=True)
    ms1 = jnp.sum(jnp.where(lo, 0.0, ks), axis=1, keepdims=True)
    kinv = lax.rsqrt(jnp.where(lo, ms0, ms1) * (1.0 / HEAD_DIM) + EPS)
    gak_o[...] = _rope(k * kinv * kn_ref[...], cos, sins).astype(BF16)
    gav_o[...] = p[:, _C_GAV:_C_WAK].astype(BF16)
    wak_o[...] = _rope(p[:, _C_WAK:_C_WAV], cos, sins).astype(BF16)
    wav_o[...] = p[:, _C_WAV:_C_END].astype(BF16)


def _mod_row(i, nblk, nb):
    return jnp.where(i % nblk == 0, nb, i // nblk)


def _inproj(x, mod, norm_g, w_packed, cos_t, sin_t, qn_g, kn_g, nb, nblk):
    t, d = x.shape
    row = lambda i: (i, 0)
    fix = lambda i: (0, 0)
    outs = [(SSD_XBC, F32), (GROUP_W, F32), (GROUP_W, F32), (LANES, F32),
            (N_HEADS * LANES, BF16), (LANES, BF16), (LANES, BF16),
            (N_HEADS * LANES, BF16), (LANES, BF16), (LANES, BF16)]
    return pl.pallas_call(
        _inproj_kernel,
        out_shape=[jax.ShapeDtypeStruct((t, w), dt) for w, dt in outs],
        grid=(t // TM,),
        in_specs=[pl.BlockSpec((TM, d), row),
                  pl.BlockSpec((1, 6, d), lambda i: (_mod_row(i, nblk, nb), 0, 0)),
                  pl.BlockSpec((1, d), fix),
                  pl.BlockSpec((d, _C_END), fix),
                  pl.BlockSpec((TM, LANES), lambda i: (i % nblk, 0)),
                  pl.BlockSpec((TM, LANES), lambda i: (i % nblk, 0)),
                  pl.BlockSpec((1, N_HEADS * LANES), fix),
                  pl.BlockSpec((1, LANES), fix)],
        out_specs=[pl.BlockSpec((TM, w), row) for w, _ in outs],
        compiler_params=_cp(("parallel",)),
        name="in_proj",
    )(x, mod, norm_g.reshape(1, d), w_packed, cos_t, sin_t,
      jnp.tile(qn_g, 2 * N_HEADS).reshape(1, -1), jnp.tile(kn_g, 2).reshape(1, -1))


def _rope_tables(lc, l):
    n_rows = l // GRID_W
    rows = jnp.repeat(jnp.arange(n_rows), GRID_W)
    cols = jnp.tile(jnp.arange(GRID_W), n_rows)
    inv = jnp.power(ROPE_BASE, -jnp.arange(ROPE_FREQS, dtype=F32) / ROPE_FREQS)
    ang = jnp.stack([rows, cols], axis=-1).astype(F32)[..., None] * inv
    cos = jnp.cos(ang)
    sin = jnp.sin(ang)
    cos64 = jnp.stack([cos, cos], axis=2).reshape(l, HEAD_DIM)
    sin64 = jnp.stack([-sin, sin], axis=2).reshape(l, HEAD_DIM)
    cos64 = jnp.concatenate([jnp.ones((lc, HEAD_DIM), F32), cos64], axis=0)
    sin64 = jnp.concatenate([jnp.zeros((lc, HEAD_DIM), F32), sin64], axis=0)
    return jnp.tile(cos64, (1, 2)), jnp.tile(sin64, (1, 2))


def _merge_heads(o2, kvh):
    oa, ob = o2[:TQ], o2[TQ:]
    lane = lax.broadcasted_iota(jnp.int32, oa.shape, 1)
    if kvh == 0:
        return jnp.where(lane < HEAD_DIM, oa, pltpu.roll(ob, HEAD_DIM, 1))
    return jnp.where(lane < HEAD_DIM, pltpu.roll(oa, HEAD_DIM, 1), ob)


def _stack_q(q_ref, kvh):
    return jnp.concatenate([q_ref[:, (2 * kvh) * LANES:(2 * kvh + 1) * LANES],
                            q_ref[:, (2 * kvh + 1) * LANES:(2 * kvh + 2) * LANES]], axis=0)


def _ga_kernel(q_ref, k_ref, v_ref, o_ref, *, lc):
    j = pl.program_id(1)
    k = k_ref[...]
    v = v_ref[...]
    s_len = k.shape[0]
    limit = jnp.where(j < lc // TQ, lc, s_len)
    col = lax.broadcasted_iota(jnp.int32, (2 * TQ, s_len), 1)
    outs = []
    for kvh in range(2):
        s = _dot_nt(_stack_q(q_ref, kvh), k)
        s = jnp.where(col < limit, s, NEG_INF)
        m = jnp.max(s, axis=1, keepdims=True)
        p = jnp.exp(s - m)
        denom = jnp.sum(p, axis=1, keepdims=True)
        outs.append(_merge_heads(_dot(p.astype(BF16), v) / denom, kvh))
    o_ref[...] = jnp.concatenate(outs, axis=1)


def _ga(q, k, v, nb, s_len, lc):
    t = q.shape[0]
    nq = s_len // TQ
    return pl.pallas_call(
        functools.partial(_ga_kernel, lc=lc),
        out_shape=jax.ShapeDtypeStruct((t, GROUP_W), F32),
        grid=(nb, nq),
        in_specs=[pl.BlockSpec((TQ, N_HEADS * LANES), lambda b, j: (b * nq + j, 0)),
                  pl.BlockSpec((s_len, LANES), lambda b, j: (b, 0)),
                  pl.BlockSpec((s_len, LANES), lambda b, j: (b, 0))],
        out_specs=pl.BlockSpec((TQ, GROUP_W), lambda b, j: (b * nq + j, 0)),
        compiler_params=_cp(("parallel", "arbitrary")),
        name="global_attn",
    )(q, k, v)


def _wa_kernel(sink_ref, q_ref, k_ref, v_ref, o_ref, *, lc):
    j = pl.program_id(1)
    s_len = k_ref.shape[0]
    n = j - lc // TQ
    start = pl.multiple_of(jnp.clip(lc + (n - 1) * TQ, lc, s_len - 3 * TQ), TQ)
    kc = k_ref[0:lc, :]
    vc = v_ref[0:lc, :]
    kb = k_ref[pl.ds(start, 3 * TQ), :]
    vb = v_ref[pl.ds(start, 3 * TQ), :]
    qpos = n * TQ + lax.broadcasted_iota(jnp.int32, (TQ, 3 * TQ), 0)
    kpos = (start - lc) + lax.broadcasted_iota(jnp.int32, (TQ, 3 * TQ), 1)
    reach = jnp.where(n >= 0, WINDOW, -1)
    valid = jnp.abs(qpos - kpos) <= reach
    valid = jnp.concatenate([valid, valid], axis=0)
    row = lax.broadcasted_iota(jnp.int32, (2 * TQ, 1), 0)
    outs = []
    for kvh in range(2):
        q2 = _stack_q(q_ref, kvh)
        sc = _dot_nt(q2, kc)
        sb = jnp.where(valid, _dot_nt(q2, kb), NEG_INF)
        sink = jnp.where(row < TQ, sink_ref[2 * kvh], sink_ref[2 * kvh + 1])
        m = jnp.maximum(jnp.maximum(jnp.max(sc, axis=1, keepdims=True), jnp.max(sb, axis=1, keepdims=True)), sink)
        pc = jnp.exp(sc - m)
        pb = jnp.exp(sb - m)
        denom = jnp.sum(pc, axis=1, keepdims=True) + jnp.sum(pb, axis=1, keepdims=True) + jnp.exp(sink - m)
        o2 = (_dot(pc.astype(BF16), vc) + _dot(pb.astype(BF16), vb)) / denom
        outs.append(_merge_heads(o2, kvh))
    o_ref[...] = jnp.concatenate(outs, axis=1)


def _wa(sink, q, k, v, nb, s_len, lc):
    t = q.shape[0]
    nq = s_len // TQ
    return pl.pallas_call(
        functools.partial(_wa_kernel, lc=lc),
        out_shape=jax.ShapeDtypeStruct((t, GROUP_W), F32),
        grid=(nb, nq),
        in_specs=[pl.BlockSpec(memory_space=pltpu.SMEM),
                  pl.BlockSpec((TQ, N_HEADS * LANES), lambda b, j: (b * nq + j, 0)),
                  pl.BlockSpec((s_len, LANES), lambda b, j: (b, 0)),
                  pl.BlockSpec((s_len, LANES), lambda b, j: (b, 0))],
        out_specs=pl.BlockSpec((TQ, GROUP_W), lambda b, j: (b * nq + j, 0)),
        compiler_params=_cp(("parallel", "arbitrary")),
        name="window_attn",
    )(sink, q, k, v)


def _s5_chunk_index(t, rev, nc_ctx, nc_tot):
    if not rev:
        return t
    return jnp.where(t < nc_ctx, nc_ctx - 1 - t, nc_tot - 1 - (t - nc_ctx))


def _s5_kernel(u_ref, m_ref, p_ref, g_ref, ar_ref, ai_ref, dsk_ref, y_ref, s_scr, h_scr, *, nc_ctx, nc_tot):
    u = u_ref[0]
    y = u.astype(F32) * dsk_ref[0]
    for d in range(2):
        s_scr[...] = _dot(u, p_ref[d, 0])
        ar = jnp.broadcast_to(ar_ref[d, 0], (SUBLANES, LANES))
        ai = jnp.broadcast_to(ai_ref[d, 0], (SUBLANES, LANES))

        def body(t, h, d=d, ar=ar, ai=ai):
            r0 = pl.multiple_of(_s5_chunk_index(t, d == 1, nc_ctx, nc_tot) * SUBLANES, SUBLANES)
            h_scr[pl.ds(r0, SUBLANES), :] = h
            return ar * h + ai * pltpu.roll(h, S5_STATE, 1) + s_scr[pl.ds(r0, SUBLANES), :]

        lax.fori_loop(0, nc_tot, body, jnp.zeros((SUBLANES, LANES), F32))
        y = y + _dot(u, m_ref[d, 0]) + _dot(h_scr[...].astype(BF16), g_ref[d, 0])
    y_ref[0] = y


def _s5_params(lam_re, lam_im, log_dt, b_re, b_im, c_re, c_im, d_skip):
    q = S5_Q
    dt = jnp.exp(log_dt)[..., None]
    lr, li = lam_re, lam_im
    mag = jnp.exp(lr * dt)
    a_re = mag * jnp.cos(li * dt)
    a_im = mag * jnp.sin(li * dt)
    den = lr * lr + li * li
    f_re = ((a_re - 1.0) * lr + a_im * li) / den
    f_im = (a_im * lr - (a_re - 1.0) * li) / den
    bb_re = f_re[..., None] * b_re - f_im[..., None] * b_im
    bb_im = f_re[..., None] * b_im + f_im[..., None] * b_re
    kk = jnp.arange(q + 1, dtype=F32)[:, None, None, None]
    pmag = jnp.exp(kk * (lr * dt))
    pw_re = pmag * jnp.cos(kk * (li * dt))
    pw_im = pmag * jnp.sin(kk * (li * dt))
    e_re = c_re[None] * pw_re[:, :, :, None, :] - c_im[None] * pw_im[:, :, :, None, :]
    e_im = c_re[None] * pw_im[:, :, :, None, :] + c_im[None] * pw_re[:, :, :, None, :]
    kern = (jnp.einsum("kdgop,dgpc->kdgoc", e_re, bb_re, precision=HI)
            - jnp.einsum("kdgop,dgpc->kdgoc", e_im, bb_im, precision=HI))
    s_i = jnp.arange(q)[:, None]
    t_i = jnp.arange(q)[None, :]
    ms, ps, gs = [], [], []
    for d in range(2):
        tau = (t_i - s_i) if d == 0 else (s_i - t_i)
        kd = kern[:, d][jnp.clip(tau, 0, q)]
        kd = jnp.where((tau >= 0)[:, :, None, None, None], kd, 0.0)
        ms.append(kd.transpose(2, 0, 4, 1, 3).reshape(S5_GROUPS, S5_BLK, S5_BLK))
        pidx = (q - 1 - jnp.arange(q)) if d == 0 else jnp.arange(q)
        pr = pw_re[pidx, d][..., None]
        pi = pw_im[pidx, d][..., None]
        p_re = pr * bb_re[d][None] - pi * bb_im[d][None]
        p_im = pr * bb_im[d][None] + pi * bb_re[d][None]
        pd = jnp.concatenate([p_re, p_im], axis=2)
        ps.append(pd.transpose(1, 0, 3, 2).reshape(S5_GROUPS, S5_BLK, 2 * S5_STATE))
        gidx = (jnp.arange(q) + 1) if d == 0 else (q - jnp.arange(q))
        gd = jnp.concatenate([e_re[gidx, d], -e_im[gidx, d]], axis=3)
        gs.append(gd.transpose(1, 3, 0, 2).reshape(S5_GROUPS, 2 * S5_STATE, S5_BLK))
    ar = jnp.concatenate([pw_re[q], pw_re[q]], axis=-1)[:, :, None, :]
    ai = jnp.concatenate([-pw_im[q], pw_im[q]], axis=-1)[:, :, None, :]
    dsk = jnp.tile(d_skip.reshape(S5_GROUPS, 1, S5_CH), (1, 1, q))
    return (jnp.stack(ms).astype(BF16), jnp.stack(ps).astype(BF16), jnp.stack(gs).astype(BF16),
            ar.astype(F32), ai.astype(F32), dsk.astype(F32))


def _s5(u, params, nb, s_len, lc):
    m, p, g, ar, ai, dsk = params
    nc_tot = s_len // S5_Q
    nc_ctx = lc // S5_Q
    r = nc_tot * SUBLANES
    ug = u.reshape(nb, nc_tot, S5_Q, S5_GROUPS, S5_CH).transpose(3, 1, 0, 2, 4)
    ug = jnp.pad(ug, ((0, 0), (0, 0), (0, SUBLANES - nb), (0, 0), (0, 0)))
    ug = ug.reshape(S5_GROUPS, r, S5_BLK).astype(BF16)
    y = pl.pallas_call(
        functools.partial(_s5_kernel, nc_ctx=nc_ctx, nc_tot=nc_tot),
        out_shape=jax.ShapeDtypeStruct((S5_GROUPS, r, S5_BLK), F32),
        grid=(S5_GROUPS,),
        in_specs=[pl.BlockSpec((1, r, S5_BLK), lambda gi: (gi, 0, 0)),
                  pl.BlockSpec((2, 1, S5_BLK, S5_BLK), lambda gi: (0, gi, 0, 0)),
                  pl.BlockSpec((2, 1, S5_BLK, 2 * S5_STATE), lambda gi: (0, gi, 0, 0)),
                  pl.BlockSpec((2, 1, 2 * S5_STATE, S5_BLK), lambda gi: (0, gi, 0, 0)),
                  pl.BlockSpec((2, 1, 1, 2 * S5_STATE), lambda gi: (0, gi, 0, 0)),
                  pl.BlockSpec((2, 1, 1, 2 * S5_STATE), lambda gi: (0, gi, 0, 0)),
                  pl.BlockSpec((1, 1, S5_BLK), lambda gi: (gi, 0, 0))],
        out_specs=pl.BlockSpec((1, r, S5_BLK), lambda gi: (gi, 0, 0)),
        scratch_shapes=[pltpu.VMEM((r, 2 * S5_STATE), F32), pltpu.VMEM((r, 2 * S5_STATE), F32)],
        compiler_params=_cp(("parallel",)),
        name="s5_scan",
    )(ug, m, p, g, ar, ai, dsk)
    y = y.reshape(S5_GROUPS, nc_tot, SUBLANES, S5_Q, S5_CH)[:, :, :nb]
    return y.transpose(2, 1, 3, 0, 4).reshape(nb * s_len, GROUP_W)


def _conv_kernel(x_ref, prev_ref, next_ref, w_ref, b_ref, o_ref, *, nblk):
    i = pl.program_id(0) % nblk
    x = x_ref[...]
    rows = x.shape[0]
    ridx = lax.broadcasted_iota(jnp.int32, x.shape, 0)
    prev_row = jnp.where(i <= 1, 0.0, prev_ref[SUBLANES - 1:SUBLANES, :])
    next_row = jnp.where(jnp.logical_or(i == 0, i == nblk - 1), 0.0, next_ref[0:1, :])
    xm = jnp.where(ridx == 0, prev_row, pltpu.roll(x, 1, 0))
    xp = jnp.where(ridx == rows - 1, next_row, pltpu.roll(x, rows - 1, 0))
    y = xm * w_ref[0:1, :] + x * w_ref[1:2, :] + xp * w_ref[2:3, :] + b_ref[...]
    o_ref[...] = _silu(y)


def _conv(xbc, w, b, nblk):
    t, c = xbc.shape
    per = TM // SUBLANES
    last = t // SUBLANES - 1
    return pl.pallas_call(
        functools.partial(_conv_kernel, nblk=nblk),
        out_shape=jax.ShapeDtypeStruct((t, c), F32),
        grid=(t // TM,),
        in_specs=[pl.BlockSpec((TM, c), lambda i: (i, 0)),
                  pl.BlockSpec((SUBLANES, c), lambda i: (jnp.maximum(i * per - 1, 0), 0)),
                  pl.BlockSpec((SUBLANES, c), lambda i: (jnp.minimum((i + 1) * per, last), 0)),
                  pl.BlockSpec((3, c), lambda i: (0, 0)),
                  pl.BlockSpec((1, c), lambda i: (0, 0))],
        out_specs=pl.BlockSpec((TM, c), lambda i: (i, 0)),
        compiler_params=_cp(("parallel",)),
        name="ssd_conv",
    )(xbc, xbc, xbc, w, b.reshape(1, c))


_X_B = GROUP_W
_X_C = GROUP_W + SSD_NGROUPS * SSD_STATE


def _ssd_kernel(xc_ref, dt_ref, dtt_ref, bias_ref, a_ref, biast_ref, at_ref, dsk_ref, y_ref, st_ref, *, rev):
    c = pl.program_id(1)

    @pl.when(c == 0)
    def _():
        st_ref[...] = jnp.zeros_like(st_ref)

    base = SSD_HEADS if rev else 0
    xc = xc_ref[...]
    x = xc[:, 0:GROUP_W]
    dt = _softplus(dt_ref[...] + bias_ref[...])
    a = dt * a_ref[...]
    dtt = _softplus(dtt_ref[0] + biast_ref[...])
    at = dtt * at_ref[...]
    ri = lax.broadcasted_iota(jnp.int32, (TQ, TQ), 0)
    ci = lax.broadcasted_iota(jnp.int32, (TQ, TQ), 1)
    causal = (ci >= ri) if rev else (ri >= ci)
    tri = jnp.where(causal, 1.0, 0.0)
    cum_c = _dot_hi(tri, a)
    cum_r = _dot_nt_hi(at, tri)
    edge = 0 if rev else TQ - 1
    tot = cum_c[edge:edge + 1, :]

    shape = (TQ, GROUP_W)
    xdt = x * _per_head_cols(dt, base, SSD_HEADS, shape)
    lane = lax.broadcasted_iota(jnp.int32, shape, 1)
    y = jnp.zeros(shape, F32)
    bmat = [xc[:, _X_B + g * SSD_STATE:_X_B + (g + 1) * SSD_STATE].astype(BF16) for g in range(SSD_NGROUPS)]
    cmat = [xc[:, _X_C + g * SSD_STATE:_X_C + (g + 1) * SSD_STATE].astype(BF16) for g in range(SSD_NGROUPS)]
    cb = [_dot_nt(cmat[g], bmat[g]) for g in range(SSD_NGROUPS)]
    for h in range(SSD_HEADS):
        col = base + h
        seg = jnp.where(causal, cum_c[:, col:col + 1] - cum_r[col:col + 1, :], NEG_INF)
        scores = cb[h // 2] * jnp.exp(seg)
        xh = jnp.where((lane >= h * HEAD_DIM) & (lane < (h + 1) * HEAD_DIM), xdt, 0.0)
        y = y + _dot(scores.astype(BF16), xh.astype(BF16))
    st = st_ref[...]
    yo = jnp.concatenate(
        [_dot_nt(cmat[g], st[g * SSD_STATE:(g + 1) * SSD_STATE].astype(BF16)) for g in range(SSD_NGROUPS)], axis=1)
    y = y + yo * _per_head_cols(jnp.exp(cum_c), base, SSD_HEADS, shape)
    if not rev:
        y = y + x * dsk_ref[...]
    y_ref[...] = y
    xd = xdt * _per_head_cols(jnp.exp(tot - cum_c), base, SSD_HEADS, shape)
    xdt_t = xd.T.astype(BF16)
    decay = jnp.exp(tot)
    for g in range(SSD_NGROUPS):
        new = _dot(xdt_t[g * SSD_STATE:(g + 1) * SSD_STATE], bmat[g])
        for hh in range(2):
            h = 2 * g + hh
            r0 = h * HEAD_DIM
            st_ref[r0:r0 + HEAD_DIM, :] = (decay[:, base + h:base + h + 1] * st[r0:r0 + HEAD_DIM]
                                           + new[hh * HEAD_DIM:(hh + 1) * HEAD_DIM])


def _dot_nt_hi(a, b):
    return lax.dot_general(a, b, (((1,), (1,)), ((), ())), preferred_element_type=F32, precision=HI)


def _ssd_chunk(c, rev, nc_ctx, nc_tot):
    if not rev:
        return c
    return jnp.where(c < nc_ctx, nc_ctx - 1 - c, nc_tot - 1 - (c - nc_ctx))


def _ssd_dir(xc, dt, dtt, bias, a, biast, at, dsk, rev, nb, s_len, lc):
    t = xc.shape[0]
    nc_tot = s_len // TQ
    nc_ctx = lc // TQ
    cidx = lambda c: _ssd_chunk(c, rev, nc_ctx, nc_tot)
    fix = lambda b, c: (0, 0)
    return pl.pallas_call(
        functools.partial(_ssd_kernel, rev=rev),
        out_shape=jax.ShapeDtypeStruct((t, GROUP_W), F32),
        grid=(nb, nc_tot),
        in_specs=[pl.BlockSpec((TQ, SSD_XBC), lambda b, c: (b * nc_tot + cidx(c), 0)),
                  pl.BlockSpec((TQ, LANES), lambda b, c: (b * nc_tot + cidx(c), 0)),
                  pl.BlockSpec((1, SUBLANES, TQ), lambda b, c: (b, 0, cidx(c))),
                  pl.BlockSpec((1, LANES), fix),
                  pl.BlockSpec((1, LANES), fix),
                  pl.BlockSpec((SUBLANES, TQ), fix),
                  pl.BlockSpec((SUBLANES, TQ), fix),
                  pl.BlockSpec((1, GROUP_W), fix)],
        out_specs=pl.BlockSpec((TQ, GROUP_W), lambda b, c: (b * nc_tot + cidx(c), 0)),
        scratch_shapes=[pltpu.VMEM((SSD_HEADS * HEAD_DIM, SSD_STATE), F32)],
        compiler_params=_cp(("parallel", "arbitrary")),
        name="ssd_rev" if rev else "ssd_fwd",
    )(xc, dt, dtt, bias, a, biast, at, dsk)


def _ssd(xbc, dt, conv_w, conv_b, dt_bias, a_log, d_skip, nb, s_len, lc):
    nblk = s_len // TM
    xc = _conv(xbc, conv_w, conv_b, nblk)
    nd = 2 * SSD_HEADS
    dtt = dt[:, :nd].reshape(nb, s_len, nd).transpose(0, 2, 1)
    bias = jnp.pad(dt_bias.reshape(1, nd), ((0, 0), (0, LANES - nd)))
    a = jnp.pad(-jnp.exp(a_log).reshape(1, nd), ((0, 0), (0, LANES - nd)))
    biast = jnp.broadcast_to(dt_bias.reshape(nd, 1), (nd, TQ))
    at = jnp.broadcast_to(-jnp.exp(a_log).reshape(nd, 1), (nd, TQ))
    dsk = jnp.repeat(d_skip, HEAD_DIM).reshape(1, GROUP_W)
    args = (xc, dt, dtt, bias, a, biast, at, dsk)
    return _ssd_dir(*args, False, nb, s_len, lc), _ssd_dir(*args, True, nb, s_len, lc)


def _outproj_kernel(x_ref, ys5_ref, oga_ref, y0_ref, y1_ref, z_ref, owa_ref, mod_ref, gluw_ref, glub_ref,
                    ng_ref, wout_ref, n2_ref, wr_ref, br_ref, xn_o, h2_o, route_o):
    gl = _gelu_tanh(ys5_ref[...])
    a = gl * _sigmoid(_dot(gl.astype(BF16), gluw_ref[...]) + glub_ref[...])
    m = (y0_ref[...] + y1_ref[...]) * _silu(z_ref[...])
    m = m * lax.rsqrt(jnp.mean(m * m, axis=-1, keepdims=True) + EPS) * ng_ref[...]
    w = wout_ref
    mix = (_dot(a.astype(BF16), w[0:GROUP_W, :]) + _dot(oga_ref[...].astype(BF16), w[GROUP_W:2 * GROUP_W, :])
           + _dot(m.astype(BF16), w[2 * GROUP_W:3 * GROUP_W, :]) + _dot(owa_ref[...].astype(BF16), w[3 * GROUP_W:, :]))
    xn = x_ref[...] + mod_ref[0, 2:3, :] * mix
    xn_o[...] = xn
    h2 = xn * lax.rsqrt(jnp.mean(xn * xn, axis=-1, keepdims=True) + EPS) * n2_ref[...]
    h2 = h2 * (1.0 + mod_ref[0, 4:5, :]) + mod_ref[0, 3:4, :]
    h2_o[...] = h2
    logits = _dot_hi(h2, wr_ref[...]) + br_ref[...]
    lane = lax.broadcasted_iota(jnp.int32, logits.shape, 1)
    big = 4 * LANES
    lcoarse = jnp.where(lane < MOE_GROUPS, logits, NEG_INF)
    mx = jnp.max(lcoarse, axis=1, keepdims=True)
    den = jnp.sum(jnp.exp(lcoarse - mx), axis=1, keepdims=True)
    grp = jnp.min(jnp.where(lcoarse == mx, lane, big), axis=1, keepdims=True)
    pg = 1.0 / den
    lo = ROUTE_FINE0 + grp * MOE_PER_GROUP
    lf = jnp.where(lane >= lo, jnp.where(lane < lo + MOE_PER_GROUP, logits, NEG_INF), NEG_INF)
    v1 = jnp.max(lf, axis=1, keepdims=True)
    i1 = jnp.min(jnp.where(lf == v1, lane, big), axis=1, keepdims=True)
    lf2 = jnp.where(lane == i1, NEG_INF, lf)
    v2 = jnp.max(lf2, axis=1, keepdims=True)
    i2 = jnp.min(jnp.where(lf2 == v2, lane, big), axis=1, keepdims=True)
    e2 = jnp.exp(v2 - v1)
    w1 = pg / (1.0 + e2)
    w2 = w1 * e2
    route = jnp.where(lane == 0, (i1 - ROUTE_FINE0).astype(F32),
                      jnp.where(lane == 1, (i2 - ROUTE_FINE0).astype(F32),
                                jnp.where(lane == 2, w1, jnp.where(lane == 3, w2, 0.0))))
    route_o[...] = route


def _outproj(x, ys5, oga, y0, y1, z, owa, mod, glu_w, glu_b, ssd_norm_g, w_out, norm2_g, wr, br, nb, nblk):
    t, d = x.shape
    row = lambda i: (i, 0)
    fix = lambda i: (0, 0)
    gw = pl.BlockSpec((TM, GROUP_W), row)
    return pl.pallas_call(
        _outproj_kernel,
        out_shape=[jax.ShapeDtypeStruct((t, d), F32), jax.ShapeDtypeStruct((t, d), F32),
                   jax.ShapeDtypeStruct((t, LANES), F32)],
        grid=(t // TM,),
        in_specs=[pl.BlockSpec((TM, d), row), gw, gw, gw, gw, gw, gw,
                  pl.BlockSpec((1, 6, d), lambda i: (_mod_row(i, nblk, nb), 0, 0)),
                  pl.BlockSpec((GROUP_W, GROUP_W), fix),
                  pl.BlockSpec((1, GROUP_W), fix),
                  pl.BlockSpec((1, GROUP_W), fix),
                  pl.BlockSpec((d, d), fix),
                  pl.BlockSpec((1, d), fix),
                  pl.BlockSpec((d, LANES), fix),
                  pl.BlockSpec((1, LANES), fix)],
        out_specs=[pl.BlockSpec((TM, d), row), pl.BlockSpec((TM, d), row), pl.BlockSpec((TM, LANES), row)],
        compiler_params=_cp(("parallel",)),
        name="out_proj_router",
    )(x, ys5, oga, y0, y1, z, owa, mod, glu_w.astype(BF16), glu_b.reshape(1, -1), ssd_norm_g.reshape(1, -1),
      w_out.astype(BF16), norm2_g.reshape(1, -1), wr, br)


def _pack_router(coarse_w, coarse_b, fine_w, fine_b):
    d = coarse_w.shape[0]
    wr = jnp.zeros((d, LANES), F32)
    wr = wr.at[:, 0:MOE_GROUPS].set(coarse_w).at[:, ROUTE_FINE0:ROUTE_FINE0 + N_EXPERTS].set(fine_w)
    br = jnp.zeros((1, LANES), F32)
    br = br.at[0, 0:MOE_GROUPS].set(coarse_b).at[0, ROUTE_FINE0:ROUTE_FINE0 + N_EXPERTS].set(fine_b)
    return wr, br


def _gather_kernel(idx_ref, src_ref, out_ref, sem):
    base = pl.program_id(0) * GATHER_ROWS

    def copy(i):
        return pltpu.make_async_copy(src_ref.at[pl.ds(idx_ref[0, 0, i], 1), :],
                                     out_ref.at[pl.ds(base + i, 1), :], sem)

    def start(i, carry):
        copy(i).start()
        return carry

    def wait(i, carry):
        copy(i).wait()
        return carry

    lax.fori_loop(0, GATHER_ROWS, start, 0)
    lax.fori_loop(0, GATHER_ROWS, wait, 0)


def _gather_rows(src, idx):
    m = idx.shape[0]
    nsteps = m // GATHER_ROWS
    return pl.pallas_call(
        _gather_kernel,
        out_shape=jax.ShapeDtypeStruct((m, src.shape[1]), src.dtype),
        grid=(nsteps,),
        in_specs=[pl.BlockSpec((1, 1, GATHER_ROWS), lambda i: (i, 0, 0), memory_space=pltpu.SMEM),
                  pl.BlockSpec(memory_space=pl.ANY)],
        out_specs=pl.BlockSpec(memory_space=pl.ANY),
        scratch_shapes=[pltpu.SemaphoreType.DMA],
        compiler_params=_cp(("arbitrary",)),
        name="row_gather",
    )(idx.reshape(nsteps, 1, GATHER_ROWS), src)


def _expert_kernel(be_ref, nused_ref, x_ref, wg_ref, wu_ref, wd_ref, o_ref):
    i = pl.program_id(0)

    @pl.when(i < nused_ref[0])
    def _():
        x = x_ref[...].astype(BF16)
        hid = _silu(_dot(x, wg_ref[0])) * _dot(x, wu_ref[0])
        o_ref[...] = _dot(hid.astype(BF16), wd_ref[0])

    @pl.when(i >= nused_ref[0])
    def _():
        o_ref[...] = jnp.zeros_like(o_ref)


def _experts(xs, blk_e, n_used, wg, wu, wd):
    rows, d = xs.shape
    nblocks = rows // MOE_TM
    de = wg.shape[2]
    grid_spec = pltpu.PrefetchScalarGridSpec(
        num_scalar_prefetch=2,
        grid=(nblocks,),
        in_specs=[pl.BlockSpec((MOE_TM, d), lambda i, be, nu: (i, 0)),
                  pl.BlockSpec((1, d, de), lambda i, be, nu: (be[i], 0, 0)),
                  pl.BlockSpec((1, d, de), lambda i, be, nu: (be[i], 0, 0)),
                  pl.BlockSpec((1, de, d), lambda i, be, nu: (be[i], 0, 0))],
        out_specs=pl.BlockSpec((MOE_TM, d), lambda i, be, nu: (i, 0)),
    )
    return pl.pallas_call(
        _expert_kernel,
        out_shape=jax.ShapeDtypeStruct((rows, d), F32),
        grid_spec=grid_spec,
        compiler_params=_cp(("arbitrary",)),
        name="moe_experts",
    )(blk_e, n_used, xs, wg, wu, wd)


def _combine_kernel(x_ref, r_ref, route_ref, mod_ref, fg_ref, o_ref, *, final):
    d = x_ref.shape[1]
    route = route_ref[...]
    f = route[:, 2:3] * r_ref[:, 0:d] + route[:, 3:4] * r_ref[:, d:2 * d]
    y = x_ref[...] + mod_ref[0, 5:6, :] * f
    if final:
        y = y * lax.rsqrt(jnp.mean(y * y, axis=-1, keepdims=True) + EPS) * fg_ref[...]
    o_ref[...] = y


def _combine(xn, rows2, route, mod, final_g, nb, nblk, final):
    t, d = xn.shape
    if final:
        nlat = nblk - 1
        grid = (nb * nlat,)
        src = lambda i: ((i // nlat) * nblk + 1 + i % nlat, 0)
        modi = lambda i: (i // nlat, 0, 0)
        out_rows = nb * nlat * TM
    else:
        grid = (t // TM,)
        src = lambda i: (i, 0)
        modi = lambda i: (_mod_row(i, nblk, nb), 0, 0)
        out_rows = t
    return pl.pallas_call(
        functools.partial(_combine_kernel, final=final),
        out_shape=jax.ShapeDtypeStruct((out_rows, d), F32),
        grid=grid,
        in_specs=[pl.BlockSpec((TM, d), src),
                  pl.BlockSpec((TM, 2 * d), src),
                  pl.BlockSpec((TM, LANES), src),
                  pl.BlockSpec((1, 6, d), modi),
                  pl.BlockSpec((1, d), lambda i: (0, 0))],
        out_specs=pl.BlockSpec((TM, d), lambda i: (i, 0)),
        compiler_params=_cp(("parallel",)),
        name="moe_combine_final" if final else "moe_combine",
    )(xn, rows2, route, mod, final_g.reshape(1, d))


def _moe(h2, route, wg, wu, wd):
    t, d = h2.shape
    n_slots = 2 * t
    flat_e = route[:, 0:2].astype(jnp.int32).reshape(-1)
    onehot = (flat_e[:, None] == jnp.arange(N_EXPERTS, dtype=jnp.int32)[None, :]).astype(jnp.int32)
    csum = jnp.cumsum(onehot, axis=0)
    counts = csum[-1]
    rank = jnp.sum(onehot * (csum - onehot), axis=1)
    pcounts = (counts + MOE_TM - 1) // MOE_TM * MOE_TM
    pends = jnp.cumsum(pcounts)
    pstarts = pends - pcounts
    dest = pstarts[flat_e] + rank
    nblocks = -(-n_slots // MOE_TM) + N_EXPERTS
    nrows = -(-nblocks * MOE_TM // GATHER_ROWS) * GATHER_ROWS
    nblocks = nrows // MOE_TM
    src_tok = jnp.zeros((nrows,), jnp.int32).at[dest].set(jnp.arange(n_slots, dtype=jnp.int32) // 2)
    blk_e = jnp.minimum(jnp.searchsorted(pends, jnp.arange(nblocks, dtype=jnp.int32) * MOE_TM, side="right"),
                        N_EXPERTS - 1).astype(jnp.int32)
    n_used = (pends[-1] // MOE_TM).astype(jnp.int32).reshape(1)
    xs = _gather_rows(h2, src_tok)
    ys = _experts(xs, blk_e, n_used, wg, wu, wd)
    return _gather_rows(ys, dest.astype(jnp.int32)).reshape(t, 2 * d)


def kernel(x, c, ctx, c_ctx, ada_w, ada_b, norm1_g, norm2_g, w_in, w_out, s5_lam_re, s5_lam_im, s5_log_dt, s5_b_re, s5_b_im, s5_c_re, s5_c_im, s5_d, s5_glu_w, s5_glu_b, ga_qn_g, ga_kn_g, ssd_conv_w, ssd_conv_b, ssd_dt_bias, ssd_a_log, ssd_d, ssd_norm_g, wa_sink, moe_coarse_w, moe_coarse_b, moe_fine_w, moe_fine_b, moe_w_gate, moe_w_up, moe_w_down, final_g):
    nb, l, d = x.shape
    lc = ctx.shape[1]
    depth = ada_w.shape[0]
    assert lc == TM and l % TM == 0 and nb <= SUBLANES - 1 and d == D_MODEL
    s_len = lc + l
    nblk = s_len // TM
    t = nb * s_len

    xm = jnp.concatenate([ctx, x], axis=1).reshape(t, d)
    cc = jnp.zeros((SUBLANES, d), F32).at[:nb].set(c).at[nb].set(c_ctx)
    mods = _ada(cc, ada_w, ada_b).reshape(depth, SUBLANES, 6, d)
    cos_t, sin_t = _rope_tables(lc, l)

    out = None
    for i in range(depth):
        mod = mods[i]
        (xbc, u, z, dt, gaq, gak, gav, waq, wak, wav) = _inproj(
            xm, mod, norm1_g[i], _pack_w_in(w_in[i]), cos_t, sin_t, ga_qn_g[i], ga_kn_g[i], nb, nblk)
        ys5 = _s5(u, _s5_params(s5_lam_re[i], s5_lam_im[i], s5_log_dt[i], s5_b_re[i], s5_b_im[i],
                                s5_c_re[i], s5_c_im[i], s5_d[i]), nb, s_len, lc)
        oga = _ga(gaq, gak, gav, nb, s_len, lc)
        y0, y1 = _ssd(xbc, dt, ssd_conv_w[i], ssd_conv_b[i], ssd_dt_bias[i], ssd_a_log[i], ssd_d[i], nb, s_len, lc)
        owa = _wa(wa_sink[i], waq, wak, wav, nb, s_len, lc)
        wr, br = _pack_router(moe_coarse_w[i], moe_coarse_b[i], moe_fine_w[i], moe_fine_b[i])
        xn, h2, route = _outproj(xm, ys5, oga, y0, y1, z, owa, mod, s5_glu_w[i], s5_glu_b[i], ssd_norm_g[i],
                                 w_out[i], norm2_g[i], wr, br, nb, nblk)
        rows2 = _moe(h2, route, moe_w_gate[i].astype(BF16), moe_w_up[i].astype(BF16), moe_w_down[i].astype(BF16))
        final = i == depth - 1
        xm = _combine(xn, rows2, route, mod, final_g, nb, nblk, final)
        if final:
            out = xm.reshape(nb, l, d)
    return out
```

```python
import functools
import math

import jax
import jax.numpy as jnp
import numpy as np
from jax import lax
from jax.experimental import pallas as pl
from jax.experimental.pallas import tpu as pltpu
from jax.experimental.pallas import tpu_sc as plsc

F32 = jnp.float32
BF16 = jnp.bfloat16
HI = lax.Precision.HIGHEST

D_MODEL = 1024
GRID_W = 64
GROUP_W = 256
HEAD_DIM = 64
ROPE_FREQS = HEAD_DIM // 4
ROPE_BASE = 10000.0
EPS = 1e-6
S5_CH = 16
S5_GROUPS = GROUP_W // S5_CH
S5_STATE = 64
N_HEADS = 4
SSD_HEADS = 4
SSD_NGROUPS = 2
SSD_STATE = 128
SSD_XBC = GROUP_W + 2 * SSD_NGROUPS * SSD_STATE
WINDOW = 128
MOE_GROUPS = 4
MOE_PER_GROUP = 8
N_EXPERTS = 32
D_EXPERT = D_MODEL // 2

LANES = 128
SUBLANES = 8
TM = 256
TQ = 128
S5_Q = 32
S5_BLK = S5_Q * S5_CH
MOE_TM = 256
SC_CORES = 2
SC_SUBCORES = 16
SC_GATHER_K = 32
GATHER_ROWS = SC_CORES * SC_SUBCORES * SC_GATHER_K
ROUTE_FINE0 = 32
VMEM_LIMIT = 56 * 1024 * 1024

NEG_INF = float("-inf")


def _cp(sem, vmem=VMEM_LIMIT):
    return pltpu.CompilerParams(dimension_semantics=sem, vmem_limit_bytes=vmem)


def _dot(a, b):
    return jnp.dot(a, b, preferred_element_type=F32)


def _dot_hi(a, b):
    return jnp.dot(a, b, preferred_element_type=F32, precision=HI)


def _dot_nt(a, b):
    return lax.dot_general(a, b, (((1,), (1,)), ((), ())), preferred_element_type=F32)


def _sigmoid(x):
    return 1.0 / (1.0 + jnp.exp(-x))


def _silu(x):
    return x * _sigmoid(x)


def _gelu_tanh(x):
    return 0.5 * x * (1.0 + jnp.tanh(math.sqrt(2.0 / math.pi) * (x + 0.044715 * (x * x * x))))


def _softplus(x):
    return jnp.maximum(x, 0.0) + jnp.log1p(jnp.exp(-jnp.abs(x)))


def _per_head_cols(v, base, n_heads, shape):
    lane = lax.broadcasted_iota(jnp.int32, shape, 1)
    out = jnp.broadcast_to(v[:, base + n_heads - 1:base + n_heads], shape)
    for h in range(n_heads - 2, -1, -1):
        out = jnp.where(lane < (h + 1) * HEAD_DIM, v[:, base + h:base + h + 1], out)
    return out


def _ada_kernel(c_ref, w_ref, b_ref, o_ref):
    c = c_ref[...]
    o_ref[0] = _dot_hi(_silu(c), w_ref[0]) + b_ref[0]


def _ada(cc, ada_w, ada_b):
    depth, d, n = ada_w.shape
    tn = 1536
    return pl.pallas_call(
        _ada_kernel,
        out_shape=jax.ShapeDtypeStruct((depth, SUBLANES, n), F32),
        grid=(depth, n // tn),
        in_specs=[pl.BlockSpec((SUBLANES, d), lambda l, j: (0, 0)),
                  pl.BlockSpec((1, d, tn), lambda l, j: (l, 0, j)),
                  pl.BlockSpec((1, 1, tn), lambda l, j: (l, 0, j))],
        out_specs=pl.BlockSpec((1, SUBLANES, tn), lambda l, j: (l, 0, j)),
        compiler_params=_cp(("parallel", "parallel")),
        name="ada_mod",
    )(cc, ada_w, ada_b.reshape(depth, 1, n))


_C_XBC = 0
_C_U = _C_XBC + SSD_XBC
_C_Z = _C_U + GROUP_W
_C_DT = _C_Z + GROUP_W
_C_GAQ = _C_DT + LANES
_C_WAQ = _C_GAQ + N_HEADS * LANES
_C_GAK = _C_WAQ + N_HEADS * LANES
_C_GAV = _C_GAK + LANES
_C_WAK = _C_GAV + LANES
_C_WAV = _C_WAK + LANES
_C_END = _C_WAV + LANES


def _expand_q_cols(wq):
    zero = jnp.zeros((wq.shape[0], HEAD_DIM), wq.dtype)
    parts = []
    for h in range(N_HEADS):
        head = wq[:, h * HEAD_DIM:(h + 1) * HEAD_DIM]
        parts += [head, zero] if h // 2 == 0 else [zero, head]
    return jnp.concatenate(parts, axis=1)


def _pack_w_in(w_in):
    cuts = np.cumsum([256, 256, 128, 128, 256, SSD_XBC, 2 * SSD_HEADS, 256, 128, 128])[:-1]
    u, gaq, gak, gav, z, xbc, dt, waq, wak, wav = jnp.split(w_in, [int(c) for c in cuts], axis=1)
    dt = jnp.pad(dt, ((0, 0), (0, LANES - dt.shape[1])))
    w = jnp.concatenate([xbc, u, z, dt, _expand_q_cols(gaq), _expand_q_cols(waq), gak, gav, wak, wav], axis=1)
    return w.astype(BF16)


def _rope(x, cos, sins):
    w = x.shape[1]
    if w > LANES:
        cos = jnp.concatenate([cos] * (w // LANES), axis=1)
        sins = jnp.concatenate([sins] * (w // LANES), axis=1)
    lane = lax.broadcasted_iota(jnp.int32, x.shape, 1)
    up = pltpu.roll(x, w - ROPE_FREQS, 1)
    dn = pltpu.roll(x, ROPE_FREQS, 1)
    partner = jnp.where((lane & ROPE_FREQS) == 0, up, dn)
    return x * cos + partner * sins


def _inproj_kernel(x_ref, mod_ref, g_ref, w_ref, cos_ref, sin_ref, qn_ref, kn_ref,
                   xbc_o, u_o, z_o, dt_o, gaq_o, gak_o, gav_o, waq_o, wak_o, wav_o):
    x = x_ref[...]
    ms = jnp.mean(x * x, axis=-1, keepdims=True)
    xn = x * lax.rsqrt(ms + EPS) * g_ref[...]
    h = xn * (1.0 + mod_ref[0, 1:2, :]) + mod_ref[0, 0:1, :]
    p = _dot(h.astype(BF16), w_ref[...])
    xbc_o[...] = p[:, _C_XBC:_C_U]
    u_o[...] = p[:, _C_U:_C_Z]
    z_o[...] = p[:, _C_Z:_C_DT]
    dt_o[...] = p[:, _C_DT:_C_GAQ]
    cos = cos_ref[...]
    sins = sin_ref[...]
    scale = HEAD_DIM ** -0.5
    q = p[:, _C_GAQ:_C_WAQ]
    qs = q * q
    inv = jnp.concatenate(
        [jnp.broadcast_to(lax.rsqrt(jnp.sum(qs[:, s * LANES:(s + 1) * LANES], axis=1, keepdims=True)
                                    * (1.0 / HEAD_DIM) + EPS), (q.shape[0], LANES)) for s in range(N_HEADS)], axis=1)
    gaq_o[...] = (_rope(q * inv * qn_ref[...], cos, sins) * scale).astype(BF16)
    waq_o[...] = (_rope(p[:, _C_WAQ:_C_GAK], cos, sins) * scale).astype(BF16)
    k = p[:, _C_GAK:_C_GAV]
    ks = k * k
    lane = lax.broadcasted_iota(jnp.int32, k.shape, 1)
    lo = lane < HEAD_DIM
    ms0 = jnp.sum(jnp.where(lo, ks, 0.0), axis=1, keepdims=True)
    ms1 = jnp.sum(jnp.where(lo, 0.0, ks), axis=1, keepdims=True)
    kinv = lax.rsqrt(jnp.where(lo, ms0, ms1) * (1.0 / HEAD_DIM) + EPS)
    gak_o[...] = _rope(k * kinv * kn_ref[...], cos, sins).astype(BF16)
    gav_o[...] = p[:, _C_GAV:_C_WAK].astype(BF16)
    wak_o[...] = _rope(p[:, _C_WAK:_C_WAV], cos, sins).astype(BF16)
    wav_o[...] = p[:, _C_WAV:_C_END].astype(BF16)


def _mod_row(i, nblk, nb):
    return jnp.where(i % nblk == 0, nb, i // nblk)


def _inproj(x, mod, norm_g, w_packed, cos_t, sin_t, qn_g, kn_g, nb, nblk):
    t, d = x.shape
    row = lambda i: (i, 0)
    fix = lambda i: (0, 0)
    outs = [(SSD_XBC, F32), (GROUP_W, F32), (GROUP_W, F32), (LANES, F32),
            (N_HEADS * LANES, BF16), (LANES, BF16), (LANES, BF16),
            (N_HEADS * LANES, BF16), (LANES, BF16), (LANES, BF16)]
    return pl.pallas_call(
        _inproj_kernel,
        out_shape=[jax.ShapeDtypeStruct((t, w), dt) for w, dt in outs],
        grid=(t // TM,),
        in_specs=[pl.BlockSpec((TM, d), row),
                  pl.BlockSpec((1, 6, d), lambda i: (_mod_row(i, nblk, nb), 0, 0)),
                  pl.BlockSpec((1, d), fix),
                  pl.BlockSpec((d, _C_END), fix),
                  pl.BlockSpec((TM, LANES), lambda i: (i % nblk, 0)),
                  pl.BlockSpec((TM, LANES), lambda i: (i % nblk, 0)),
                  pl.BlockSpec((1, N_HEADS * LANES), fix),
                  pl.BlockSpec((1, LANES), fix)],
        out_specs=[pl.BlockSpec((TM, w), row) for w, _ in outs],
        compiler_params=_cp(("parallel",)),
        name="in_proj",
    )(x, mod, norm_g.reshape(1, d), w_packed, cos_t, sin_t,
      jnp.tile(qn_g, 2 * N_HEADS).reshape(1, -1), jnp.tile(kn_g, 2).reshape(1, -1))


def _rope_tables(lc, l):
    n_rows = l // GRID_W
    rows = jnp.repeat(jnp.arange(n_rows), GRID_W)
    cols = jnp.tile(jnp.arange(GRID_W), n_rows)
    inv = jnp.power(ROPE_BASE, -jnp.arange(ROPE_FREQS, dtype=F32) / ROPE_FREQS)
    ang = jnp.stack([rows, cols], axis=-1).astype(F32)[..., None] * inv
    cos = jnp.cos(ang)
    sin = jnp.sin(ang)
    cos64 = jnp.stack([cos, cos], axis=2).reshape(l, HEAD_DIM)
    sin64 = jnp.stack([-sin, sin], axis=2).reshape(l, HEAD_DIM)
    cos64 = jnp.concatenate([jnp.ones((lc, HEAD_DIM), F32), cos64], axis=0)
    sin64 = jnp.concatenate([jnp.zeros((lc, HEAD_DIM), F32), sin64], axis=0)
    return jnp.tile(cos64, (1, 2)), jnp.tile(sin64, (1, 2))


def _merge_heads(o2, kvh):
    oa, ob = o2[:TQ], o2[TQ:]
    lane = lax.broadcasted_iota(jnp.int32, oa.shape, 1)
    if kvh == 0:
        return jnp.where(lane < HEAD_DIM, oa, pltpu.roll(ob, HEAD_DIM, 1))
    return jnp.where(lane < HEAD_DIM, pltpu.roll(oa, HEAD_DIM, 1), ob)


def _stack_q(q_ref, kvh):
    return jnp.concatenate([q_ref[:, (2 * kvh) * LANES:(2 * kvh + 1) * LANES],
                            q_ref[:, (2 * kvh + 1) * LANES:(2 * kvh + 2) * LANES]], axis=0)


def _ga_kernel(q_ref, k_ref, v_ref, o_ref, *, lc):
    j = pl.program_id(1)
    k = k_ref[...]
    v = v_ref[...]
    s_len = k.shape[0]
    limit = jnp.where(j < lc // TQ, lc, s_len)
    col = lax.broadcasted_iota(jnp.int32, (2 * TQ, s_len), 1)
    outs = []
    for kvh in range(2):
        s = _dot_nt(_stack_q(q_ref, kvh), k)
        s = jnp.where(col < limit, s, NEG_INF)
        m = jnp.max(s, axis=1, keepdims=True)
        p = jnp.exp(s - m)
        denom = jnp.sum(p, axis=1, keepdims=True)
        outs.append(_merge_heads(_dot(p.astype(BF16), v) / denom, kvh))
    o_ref[...] = jnp.concatenate(outs, axis=1)


def _ga(q, k, v, nb, s_len, lc):
    t = q.shape[0]
    nq = s_len // TQ
    return pl.pallas_call(
        functools.partial(_ga_kernel, lc=lc),
        out_shape=jax.ShapeDtypeStruct((t, GROUP_W), F32),
        grid=(nb, nq),
        in_specs=[pl.BlockSpec((TQ, N_HEADS * LANES), lambda b, j: (b * nq + j, 0)),
                  pl.BlockSpec((s_len, LANES), lambda b, j: (b, 0)),
                  pl.BlockSpec((s_len, LANES), lambda b, j: (b, 0))],
        out_specs=pl.BlockSpec((TQ, GROUP_W), lambda b, j: (b * nq + j, 0)),
        compiler_params=_cp(("parallel", "arbitrary")),
        name="global_attn",
    )(q, k, v)


def _wa_kernel(sink_ref, q_ref, k_ref, v_ref, o_ref, *, lc):
    j = pl.program_id(1)
    s_len = k_ref.shape[0]
    n = j - lc // TQ
    start = pl.multiple_of(jnp.clip(lc + (n - 1) * TQ, lc, s_len - 3 * TQ), TQ)
    kc = k_ref[0:lc, :]
    vc = v_ref[0:lc, :]
    kb = k_ref[pl.ds(start, 3 * TQ), :]
    vb = v_ref[pl.ds(start, 3 * TQ), :]
    qpos = n * TQ + lax.broadcasted_iota(jnp.int32, (TQ, 3 * TQ), 0)
    kpos = (start - lc) + lax.broadcasted_iota(jnp.int32, (TQ, 3 * TQ), 1)
    reach = jnp.where(n >= 0, WINDOW, -1)
    valid = jnp.abs(qpos - kpos) <= reach
    valid = jnp.concatenate([valid, valid], axis=0)
    row = lax.broadcasted_iota(jnp.int32, (2 * TQ, 1), 0)
    outs = []
    for kvh in range(2):
        q2 = _stack_q(q_ref, kvh)
        sc = _dot_nt(q2, kc)
        sb = jnp.where(valid, _dot_nt(q2, kb), NEG_INF)
        sink = jnp.where(row < TQ, sink_ref[2 * kvh], sink_ref[2 * kvh + 1])
        m = jnp.maximum(jnp.maximum(jnp.max(sc, axis=1, keepdims=True), jnp.max(sb, axis=1, keepdims=True)), sink)
        pc = jnp.exp(sc - m)
        pb = jnp.exp(sb - m)
        denom = jnp.sum(pc, axis=1, keepdims=True) + jnp.sum(pb, axis=1, keepdims=True) + jnp.exp(sink - m)
        o2 = (_dot(pc.astype(BF16), vc) + _dot(pb.astype(BF16), vb)) / denom
        outs.append(_merge_heads(o2, kvh))
    o_ref[...] = jnp.concatenate(outs, axis=1)


def _wa(sink, q, k, v, nb, s_len, lc):
    t = q.shape[0]
    nq = s_len // TQ
    return pl.pallas_call(
        functools.partial(_wa_kernel, lc=lc),
        out_shape=jax.ShapeDtypeStruct((t, GROUP_W), F32),
        grid=(nb, nq),
        in_specs=[pl.BlockSpec(memory_space=pltpu.SMEM),
                  pl.BlockSpec((TQ, N_HEADS * LANES), lambda b, j: (b * nq + j, 0)),
                  pl.BlockSpec((s_len, LANES), lambda b, j: (b, 0)),
                  pl.BlockSpec((s_len, LANES), lambda b, j: (b, 0))],
        out_specs=pl.BlockSpec((TQ, GROUP_W), lambda b, j: (b * nq + j, 0)),
        compiler_params=_cp(("parallel", "arbitrary")),
        name="window_attn",
    )(sink, q, k, v)


def _s5_chunk_index(t, rev, nc_ctx, nc_tot):
    if not rev:
        return t
    return jnp.where(t < nc_ctx, nc_ctx - 1 - t, nc_tot - 1 - (t - nc_ctx))


def _s5_kernel(u_ref, m_ref, p_ref, g_ref, ar_ref, ai_ref, dsk_ref, y_ref, s_scr, h_scr, *, nc_ctx, nc_tot):
    u = u_ref[0]
    for d in range(2):
        for k in range(2):
            s_scr[d, k] = _dot(u, p_ref[d, k, 0])
    ar = [jnp.broadcast_to(ar_ref[d, 0], (SUBLANES, LANES)) for d in range(2)]
    ai = [[jnp.broadcast_to(ai_ref[d, k, 0], (SUBLANES, LANES)) for k in range(2)] for d in range(2)]

    def body(t, carry):
        out = []
        for d in range(2):
            h, hs = carry[d]
            r0 = pl.multiple_of(_s5_chunk_index(t, d == 1, nc_ctx, nc_tot) * SUBLANES, SUBLANES)
            h_scr[d, pl.ds(r0, SUBLANES), :] = h
            out.append((ar[d] * h + ai[d][0] * hs + s_scr[d, 0, pl.ds(r0, SUBLANES), :],
                        ar[d] * hs + ai[d][1] * h + s_scr[d, 1, pl.ds(r0, SUBLANES), :]))
        return tuple(out)

    zero = jnp.zeros((SUBLANES, LANES), F32)
    lax.fori_loop(0, nc_tot, body, ((zero, zero), (zero, zero)), unroll=2)
    y = u.astype(F32) * dsk_ref[0]
    for d in range(2):
        y = y + _dot(u, m_ref[d, 0]) + _dot(h_scr[d].astype(BF16), g_ref[d, 0])
    y_ref[0] = y


def _s5_params(lam_re, lam_im, log_dt, b_re, b_im, c_re, c_im, d_skip):
    q = S5_Q
    dt = jnp.exp(log_dt)[..., None]
    lr, li = lam_re, lam_im
    mag = jnp.exp(lr * dt)
    a_re = mag * jnp.cos(li * dt)
    a_im = mag * jnp.sin(li * dt)
    den = lr * lr + li * li
    f_re = ((a_re - 1.0) * lr + a_im * li) / den
    f_im = (a_im * lr - (a_re - 1.0) * li) / den
    bb_re = f_re[..., None] * b_re - f_im[..., None] * b_im
    bb_im = f_re[..., None] * b_im + f_im[..., None] * b_re
    kk = jnp.arange(q + 1, dtype=F32)[:, None, None, None]
    pmag = jnp.exp(kk * (lr * dt))
    pw_re = pmag * jnp.cos(kk * (li * dt))
    pw_im = pmag * jnp.sin(kk * (li * dt))
    e_re = c_re[None] * pw_re[:, :, :, None, :] - c_im[None] * pw_im[:, :, :, None, :]
    e_im = c_re[None] * pw_im[:, :, :, None, :] + c_im[None] * pw_re[:, :, :, None, :]
    kern = (jnp.einsum("kdgop,dgpc->kdgoc", e_re, bb_re, precision=HI)
            - jnp.einsum("kdgop,dgpc->kdgoc", e_im, bb_im, precision=HI))
    s_i = jnp.arange(q)[:, None]
    t_i = jnp.arange(q)[None, :]
    ms, ps, gs = [], [], []
    for d in range(2):
        tau = (t_i - s_i) if d == 0 else (s_i - t_i)
        kd = kern[:, d][jnp.clip(tau, 0, q)]
        kd = jnp.where((tau >= 0)[:, :, None, None, None], kd, 0.0)
        ms.append(kd.transpose(2, 0, 4, 1, 3).reshape(S5_GROUPS, S5_BLK, S5_BLK))
        pidx = (q - 1 - jnp.arange(q)) if d == 0 else jnp.arange(q)
        pr = pw_re[pidx, d][..., None]
        pi = pw_im[pidx, d][..., None]
        p_re = pr * bb_re[d][None] - pi * bb_im[d][None]
        p_im = pr * bb_im[d][None] + pi * bb_re[d][None]
        pd = jnp.stack([jnp.concatenate([p_re, p_im], axis=2), jnp.concatenate([p_im, p_re], axis=2)])
        ps.append(pd.transpose(0, 2, 1, 4, 3).reshape(2, S5_GROUPS, S5_BLK, 2 * S5_STATE))
        gidx = (jnp.arange(q) + 1) if d == 0 else (q - jnp.arange(q))
        gd = jnp.concatenate([e_re[gidx, d], -e_im[gidx, d]], axis=3)
        gs.append(gd.transpose(1, 3, 0, 2).reshape(S5_GROUPS, 2 * S5_STATE, S5_BLK))
    ar = jnp.concatenate([pw_re[q], pw_re[q]], axis=-1)[:, :, None, :]
    ai = jnp.stack([jnp.concatenate([-pw_im[q], pw_im[q]], axis=-1),
                    jnp.concatenate([pw_im[q], -pw_im[q]], axis=-1)], axis=1)[:, :, :, None, :]
    dsk = jnp.tile(d_skip.reshape(S5_GROUPS, 1, S5_CH), (1, 1, q))
    return (jnp.stack(ms).astype(BF16), jnp.stack(ps).astype(BF16), jnp.stack(gs).astype(BF16),
            ar.astype(F32), ai.astype(F32), dsk.astype(F32))


def _s5(u, params, nb, s_len, lc):
    m, p, g, ar, ai, dsk = params
    nc_tot = s_len // S5_Q
    nc_ctx = lc // S5_Q
    r = nc_tot * SUBLANES
    ug = u.reshape(nb, nc_tot, S5_Q, S5_GROUPS, S5_CH).transpose(3, 1, 0, 2, 4)
    ug = jnp.pad(ug, ((0, 0), (0, 0), (0, SUBLANES - nb), (0, 0), (0, 0)))
    ug = ug.reshape(S5_GROUPS, r, S5_BLK).astype(BF16)
    y = pl.pallas_call(
        functools.partial(_s5_kernel, nc_ctx=nc_ctx, nc_tot=nc_tot),
        out_shape=jax.ShapeDtypeStruct((S5_GROUPS, r, S5_BLK), F32),
        grid=(S5_GROUPS,),
        in_specs=[pl.BlockSpec((1, r, S5_BLK), lambda gi: (gi, 0, 0)),
                  pl.BlockSpec((2, 1, S5_BLK, S5_BLK), lambda gi: (0, gi, 0, 0)),
                  pl.BlockSpec((2, 2, 1, S5_BLK, 2 * S5_STATE), lambda gi: (0, 0, gi, 0, 0)),
                  pl.BlockSpec((2, 1, 2 * S5_STATE, S5_BLK), lambda gi: (0, gi, 0, 0)),
                  pl.BlockSpec((2, 1, 1, 2 * S5_STATE), lambda gi: (0, gi, 0, 0)),
                  pl.BlockSpec((2, 2, 1, 1, 2 * S5_STATE), lambda gi: (0, 0, gi, 0, 0)),
                  pl.BlockSpec((1, 1, S5_BLK), lambda gi: (gi, 0, 0))],
        out_specs=pl.BlockSpec((1, r, S5_BLK), lambda gi: (gi, 0, 0)),
        scratch_shapes=[pltpu.VMEM((2, 2, r, 2 * S5_STATE), F32), pltpu.VMEM((2, r, 2 * S5_STATE), F32)],
        compiler_params=_cp(("parallel",)),
        name="s5_scan",
    )(ug, m, p, g, ar, ai, dsk)
    y = y.reshape(S5_GROUPS, nc_tot, SUBLANES, S5_Q, S5_CH)[:, :, :nb]
    return y.transpose(2, 1, 3, 0, 4).reshape(nb * s_len, GROUP_W)


def _conv_kernel(x_ref, prev_ref, next_ref, w_ref, b_ref, o_ref, *, nblk):
    i = pl.program_id(0) % nblk
    x = x_ref[...]
    rows = x.shape[0]
    ridx = lax.broadcasted_iota(jnp.int32, x.shape, 0)
    prev_row = jnp.where(i <= 1, 0.0, prev_ref[SUBLANES - 1:SUBLANES, :])
    next_row = jnp.where(jnp.logical_or(i == 0, i == nblk - 1), 0.0, next_ref[0:1, :])
    xm = jnp.where(ridx == 0, prev_row, pltpu.roll(x, 1, 0))
    xp = jnp.where(ridx == rows - 1, next_row, pltpu.roll(x, rows - 1, 0))
    y = xm * w_ref[0:1, :] + x * w_ref[1:2, :] + xp * w_ref[2:3, :] + b_ref[...]
    o_ref[...] = _silu(y)


def _conv(xbc, w, b, nblk):
    t, c = xbc.shape
    per = TM // SUBLANES
    last = t // SUBLANES - 1
    return pl.pallas_call(
        functools.partial(_conv_kernel, nblk=nblk),
        out_shape=jax.ShapeDtypeStruct((t, c), F32),
        grid=(t // TM,),
        in_specs=[pl.BlockSpec((TM, c), lambda i: (i, 0)),
                  pl.BlockSpec((SUBLANES, c), lambda i: (jnp.maximum(i * per - 1, 0), 0)),
                  pl.BlockSpec((SUBLANES, c), lambda i: (jnp.minimum((i + 1) * per, last), 0)),
                  pl.BlockSpec((3, c), lambda i: (0, 0)),
                  pl.BlockSpec((1, c), lambda i: (0, 0))],
        out_specs=pl.BlockSpec((TM, c), lambda i: (i, 0)),
        compiler_params=_cp(("parallel",)),
        name="ssd_conv",
    )(xbc, xbc, xbc, w, b.reshape(1, c))


_X_B = GROUP_W
_X_C = GROUP_W + SSD_NGROUPS * SSD_STATE


def _ssd_kernel(xc_ref, dt_ref, dtt_ref, bias_ref, a_ref, biast_ref, at_ref, dsk_ref, y_ref, st_ref, *, rev):
    c = pl.program_id(1)

    @pl.when(c == 0)
    def _():
        st_ref[...] = jnp.zeros_like(st_ref)

    base = SSD_HEADS if rev else 0
    xc = xc_ref[...]
    x = xc[:, 0:GROUP_W]
    dt = _softplus(dt_ref[...] + bias_ref[...])
    a = dt * a_ref[...]
    dtt = _softplus(dtt_ref[0] + biast_ref[...])
    at = dtt * at_ref[...]
    ri = lax.broadcasted_iota(jnp.int32, (TQ, TQ), 0)
    ci = lax.broadcasted_iota(jnp.int32, (TQ, TQ), 1)
    causal = (ci >= ri) if rev else (ri >= ci)
    tri = jnp.where(causal, 1.0, 0.0)
    cum_c = _dot_hi(tri, a)
    cum_r = _dot_nt_hi(at, tri)
    edge = 0 if rev else TQ - 1
    tot = cum_c[edge:edge + 1, :]

    shape = (TQ, GROUP_W)
    xdt = x * _per_head_cols(dt, base, SSD_HEADS, shape)
    lane = lax.broadcasted_iota(jnp.int32, shape, 1)
    y = jnp.zeros(shape, F32)
    bmat = [xc[:, _X_B + g * SSD_STATE:_X_B + (g + 1) * SSD_STATE].astype(BF16) for g in range(SSD_NGROUPS)]
    cmat = [xc[:, _X_C + g * SSD_STATE:_X_C + (g + 1) * SSD_STATE].astype(BF16) for g in range(SSD_NGROUPS)]
    cb = [_dot_nt(cmat[g], bmat[g]) for g in range(SSD_NGROUPS)]
    for h in range(SSD_HEADS):
        col = base + h
        seg = jnp.where(causal, cum_c[:, col:col + 1] - cum_r[col:col + 1, :], NEG_INF)
        scores = cb[h // 2] * jnp.exp(seg)
        xh = jnp.where((lane >= h * HEAD_DIM) & (lane < (h + 1) * HEAD_DIM), xdt, 0.0)
        y = y + _dot(scores.astype(BF16), xh.astype(BF16))
    st = st_ref[...]
    yo = jnp.concatenate(
        [_dot_nt(cmat[g], st[g * SSD_STATE:(g + 1) * SSD_STATE].astype(BF16)) for g in range(SSD_NGROUPS)], axis=1)
    y = y + yo * _per_head_cols(jnp.exp(cum_c), base, SSD_HEADS, shape)
    if not rev:
        y = y + x * dsk_ref[...]
    y_ref[...] = y
    xd = xdt * _per_head_cols(jnp.exp(tot - cum_c), base, SSD_HEADS, shape)
    xdt_t = xd.T.astype(BF16)
    decay = jnp.exp(tot)
    for g in range(SSD_NGROUPS):
        new = _dot(xdt_t[g * SSD_STATE:(g + 1) * SSD_STATE], bmat[g])
        for hh in range(2):
            h = 2 * g + hh
            r0 = h * HEAD_DIM
            st_ref[r0:r0 + HEAD_DIM, :] = (decay[:, base + h:base + h + 1] * st[r0:r0 + HEAD_DIM]
                                           + new[hh * HEAD_DIM:(hh + 1) * HEAD_DIM])


def _dot_nt_hi(a, b):
    return lax.dot_general(a, b, (((1,), (1,)), ((), ())), preferred_element_type=F32, precision=HI)


def _ssd_chunk(c, rev, nc_ctx, nc_tot):
    if not rev:
        return c
    return jnp.where(c < nc_ctx, nc_ctx - 1 - c, nc_tot - 1 - (c - nc_ctx))


def _ssd_dir(xc, dt, dtt, bias, a, biast, at, dsk, rev, nb, s_len, lc):
    t = xc.shape[0]
    nc_tot = s_len // TQ
    nc_ctx = lc // TQ
    cidx = lambda c: _ssd_chunk(c, rev, nc_ctx, nc_tot)
    fix = lambda b, c: (0, 0)
    return pl.pallas_call(
        functools.partial(_ssd_kernel, rev=rev),
        out_shape=jax.ShapeDtypeStruct((t, GROUP_W), F32),
        grid=(nb, nc_tot),
        in_specs=[pl.BlockSpec((TQ, SSD_XBC), lambda b, c: (b * nc_tot + cidx(c), 0)),
                  pl.BlockSpec((TQ, LANES), lambda b, c: (b * nc_tot + cidx(c), 0)),
                  pl.BlockSpec((1, SUBLANES, TQ), lambda b, c: (b, 0, cidx(c))),
                  pl.BlockSpec((1, LANES), fix),
                  pl.BlockSpec((1, LANES), fix),
                  pl.BlockSpec((SUBLANES, TQ), fix),
                  pl.BlockSpec((SUBLANES, TQ), fix),
                  pl.BlockSpec((1, GROUP_W), fix)],
        out_specs=pl.BlockSpec((TQ, GROUP_W), lambda b, c: (b * nc_tot + cidx(c), 0)),
        scratch_shapes=[pltpu.VMEM((SSD_HEADS * HEAD_DIM, SSD_STATE), F32)],
        compiler_params=_cp(("parallel", "arbitrary")),
        name="ssd_rev" if rev else "ssd_fwd",
    )(xc, dt, dtt, bias, a, biast, at, dsk)


def _ssd(xbc, dt, conv_w, conv_b, dt_bias, a_log, d_skip, nb, s_len, lc):
    nblk = s_len // TM
    xc = _conv(xbc, conv_w, conv_b, nblk)
    nd = 2 * SSD_HEADS
    dtt = dt[:, :nd].reshape(nb, s_len, nd).transpose(0, 2, 1)
    bias = jnp.pad(dt_bias.reshape(1, nd), ((0, 0), (0, LANES - nd)))
    a = jnp.pad(-jnp.exp(a_log).reshape(1, nd), ((0, 0), (0, LANES - nd)))
    biast = jnp.broadcast_to(dt_bias.reshape(nd, 1), (nd, TQ))
    at = jnp.broadcast_to(-jnp.exp(a_log).reshape(nd, 1), (nd, TQ))
    dsk = jnp.repeat(d_skip, HEAD_DIM).reshape(1, GROUP_W)
    args = (xc, dt, dtt, bias, a, biast, at, dsk)
    return _ssd_dir(*args, False, nb, s_len, lc), _ssd_dir(*args, True, nb, s_len, lc)


def _outproj_kernel(x_ref, ys5_ref, oga_ref, y0_ref, y1_ref, z_ref, owa_ref, mod_ref, gluw_ref, glub_ref,
                    ng_ref, wout_ref, n2_ref, wr_ref, br_ref, xn_o, h2_o, route_o):
    gl = _gelu_tanh(ys5_ref[...])
    a = gl * _sigmoid(_dot(gl.astype(BF16), gluw_ref[...]) + glub_ref[...])
    m = (y0_ref[...] + y1_ref[...]) * _silu(z_ref[...])
    m = m * lax.rsqrt(jnp.mean(m * m, axis=-1, keepdims=True) + EPS) * ng_ref[...]
    w = wout_ref
    mix = (_dot(a.astype(BF16), w[0:GROUP_W, :]) + _dot(oga_ref[...].astype(BF16), w[GROUP_W:2 * GROUP_W, :])
           + _dot(m.astype(BF16), w[2 * GROUP_W:3 * GROUP_W, :]) + _dot(owa_ref[...].astype(BF16), w[3 * GROUP_W:, :]))
    xn = x_ref[...] + mod_ref[0, 2:3, :] * mix
    xn_o[...] = xn
    h2 = xn * lax.rsqrt(jnp.mean(xn * xn, axis=-1, keepdims=True) + EPS) * n2_ref[...]
    h2 = h2 * (1.0 + mod_ref[0, 4:5, :]) + mod_ref[0, 3:4, :]
    h2_o[...] = h2
    logits = _dot_hi(h2, wr_ref[...]) + br_ref[...]
    lane = lax.broadcasted_iota(jnp.int32, logits.shape, 1)
    big = 4 * LANES
    lcoarse = jnp.where(lane < MOE_GROUPS, logits, NEG_INF)
    mx = jnp.max(lcoarse, axis=1, keepdims=True)
    den = jnp.sum(jnp.exp(lcoarse - mx), axis=1, keepdims=True)
    grp = jnp.min(jnp.where(lcoarse == mx, lane, big), axis=1, keepdims=True)
    pg = 1.0 / den
    lo = ROUTE_FINE0 + grp * MOE_PER_GROUP
    lf = jnp.where(lane >= lo, jnp.where(lane < lo + MOE_PER_GROUP, logits, NEG_INF), NEG_INF)
    v1 = jnp.max(lf, axis=1, keepdims=True)
    i1 = jnp.min(jnp.where(lf == v1, lane, big), axis=1, keepdims=True)
    lf2 = jnp.where(lane == i1, NEG_INF, lf)
    v2 = jnp.max(lf2, axis=1, keepdims=True)
    i2 = jnp.min(jnp.where(lf2 == v2, lane, big), axis=1, keepdims=True)
    e2 = jnp.exp(v2 - v1)
    w1 = pg / (1.0 + e2)
    w2 = w1 * e2
    route = jnp.where(lane == 0, (i1 - ROUTE_FINE0).astype(F32),
                      jnp.where(lane == 1, (i2 - ROUTE_FINE0).astype(F32),
                                jnp.where(lane == 2, w1, jnp.where(lane == 3, w2, 0.0))))
    route_o[...] = route


def _outproj(x, ys5, oga, y0, y1, z, owa, mod, glu_w, glu_b, ssd_norm_g, w_out, norm2_g, wr, br, nb, nblk):
    t, d = x.shape
    row = lambda i: (i, 0)
    fix = lambda i: (0, 0)
    gw = pl.BlockSpec((TM, GROUP_W), row)
    return pl.pallas_call(
        _outproj_kernel,
        out_shape=[jax.ShapeDtypeStruct((t, d), F32), jax.ShapeDtypeStruct((t, d), F32),
                   jax.ShapeDtypeStruct((t, LANES), F32)],
        grid=(t // TM,),
        in_specs=[pl.BlockSpec((TM, d), row), gw, gw, gw, gw, gw, gw,
                  pl.BlockSpec((1, 6, d), lambda i: (_mod_row(i, nblk, nb), 0, 0)),
                  pl.BlockSpec((GROUP_W, GROUP_W), fix),
                  pl.BlockSpec((1, GROUP_W), fix),
                  pl.BlockSpec((1, GROUP_W), fix),
                  pl.BlockSpec((d, d), fix),
                  pl.BlockSpec((1, d), fix),
                  pl.BlockSpec((d, LANES), fix),
                  pl.BlockSpec((1, LANES), fix)],
        out_specs=[pl.BlockSpec((TM, d), row), pl.BlockSpec((TM, d), row), pl.BlockSpec((TM, LANES), row)],
        compiler_params=_cp(("parallel",)),
        name="out_proj_router",
    )(x, ys5, oga, y0, y1, z, owa, mod, glu_w.astype(BF16), glu_b.reshape(1, -1), ssd_norm_g.reshape(1, -1),
      w_out.astype(BF16), norm2_g.reshape(1, -1), wr, br)


def _pack_router(coarse_w, coarse_b, fine_w, fine_b):
    d = coarse_w.shape[0]
    wr = jnp.zeros((d, LANES), F32)
    wr = wr.at[:, 0:MOE_GROUPS].set(coarse_w).at[:, ROUTE_FINE0:ROUTE_FINE0 + N_EXPERTS].set(fine_w)
    br = jnp.zeros((1, LANES), F32)
    br = br.at[0, 0:MOE_GROUPS].set(coarse_b).at[0, ROUTE_FINE0:ROUTE_FINE0 + N_EXPERTS].set(fine_b)
    return wr, br


def _gather_rows(src, idx):
    m = idx.shape[0]
    d = src.shape[1]
    workers = SC_CORES * SC_SUBCORES
    nch = m // (workers * SC_GATHER_K)
    assert nch * workers * SC_GATHER_K == m
    mesh = plsc.VectorSubcoreMesh(core_axis_name="c", subcore_axis_name="s")

    @functools.partial(
        pl.kernel, mesh=mesh,
        out_type=jax.ShapeDtypeStruct((m, d), src.dtype),
        scratch_types=[pltpu.VMEM((nch, SC_GATHER_K), jnp.int32),
                       pltpu.VMEM((SC_GATHER_K, d), src.dtype),
                       pltpu.SemaphoreType.DMA],
    )
    def gather(src_hbm, idx_hbm, out_hbm, idx_v, rows_v, sem):
        wid = lax.axis_index("s") * SC_CORES + lax.axis_index("c")
        pltpu.sync_copy(idx_hbm.at[wid], idx_v)

        @pl.loop(0, nch)
        def _(j):
            off = pl.multiple_of((wid * nch + j) * SC_GATHER_K, SC_GATHER_K)
            pltpu.async_copy(src_hbm.at[idx_v.at[j]], rows_v, sem).wait()
            pltpu.sync_copy(rows_v, out_hbm.at[pl.ds(off, SC_GATHER_K)])

    return gather(src, idx.reshape(workers, nch, SC_GATHER_K))


def _expert_kernel(be_ref, nused_ref, x_ref, wg_ref, wu_ref, wd_ref, o_ref, wg_s, wu_s, wd_s):
    i = pl.program_id(0)
    new_expert = jnp.logical_or(i == 0, be_ref[i] != be_ref[jnp.maximum(i - 1, 0)])

    @pl.when(jnp.logical_and(i < nused_ref[0], new_expert))
    def _():
        wg_s[...] = wg_ref[0].astype(BF16)
        wu_s[...] = wu_ref[0].astype(BF16)
        wd_s[...] = wd_ref[0].astype(BF16)

    @pl.when(i < nused_ref[0])
    def _():
        x = x_ref[...].astype(BF16)
        hid = _silu(_dot(x, wg_s[...])) * _dot(x, wu_s[...])
        o_ref[...] = _dot(hid.astype(BF16), wd_s[...])

    @pl.when(i >= nused_ref[0])
    def _():
        o_ref[...] = jnp.zeros_like(o_ref)


def _experts(xs, blk_e, n_used, wg, wu, wd):
    rows, d = xs.shape
    nblocks = rows // MOE_TM
    de = wg.shape[2]
    grid_spec = pltpu.PrefetchScalarGridSpec(
        num_scalar_prefetch=2,
        grid=(nblocks,),
        in_specs=[pl.BlockSpec((MOE_TM, d), lambda i, be, nu: (i, 0)),
                  pl.BlockSpec((1, d, de), lambda i, be, nu: (be[i], 0, 0)),
                  pl.BlockSpec((1, d, de), lambda i, be, nu: (be[i], 0, 0)),
                  pl.BlockSpec((1, de, d), lambda i, be, nu: (be[i], 0, 0))],
        out_specs=pl.BlockSpec((MOE_TM, d), lambda i, be, nu: (i, 0)),
        scratch_shapes=[pltpu.VMEM((d, de), BF16), pltpu.VMEM((d, de), BF16), pltpu.VMEM((de, d), BF16)],
    )
    return pl.pallas_call(
        _expert_kernel,
        out_shape=jax.ShapeDtypeStruct((rows, d), F32),
        grid_spec=grid_spec,
        compiler_params=_cp(("arbitrary",)),
        name="moe_experts",
    )(blk_e, n_used, xs, wg, wu, wd)


def _combine_kernel(x_ref, r_ref, route_ref, mod_ref, fg_ref, o_ref, *, final):
    d = x_ref.shape[1]
    route = route_ref[...]
    f = route[:, 2:3] * r_ref[:, 0:d] + route[:, 3:4] * r_ref[:, d:2 * d]
    y = x_ref[...] + mod_ref[0, 5:6, :] * f
    if final:
        y = y * lax.rsqrt(jnp.mean(y * y, axis=-1, keepdims=True) + EPS) * fg_ref[...]
    o_ref[...] = y


def _combine(xn, rows2, route, mod, final_g, nb, nblk, final):
    t, d = xn.shape
    if final:
        nlat = nblk - 1
        grid = (nb * nlat,)
        src = lambda i: ((i // nlat) * nblk + 1 + i % nlat, 0)
        modi = lambda i: (i // nlat, 0, 0)
        out_rows = nb * nlat * TM
    else:
        grid = (t // TM,)
        src = lambda i: (i, 0)
        modi = lambda i: (_mod_row(i, nblk, nb), 0, 0)
        out_rows = t
    return pl.pallas_call(
        functools.partial(_combine_kernel, final=final),
        out_shape=jax.ShapeDtypeStruct((out_rows, d), F32),
        grid=grid,
        in_specs=[pl.BlockSpec((TM, d), src),
                  pl.BlockSpec((TM, 2 * d), src),
                  pl.BlockSpec((TM, LANES), src),
                  pl.BlockSpec((1, 6, d), modi),
                  pl.BlockSpec((1, d), lambda i: (0, 0))],
        out_specs=pl.BlockSpec((TM, d), lambda i: (i, 0)),
        compiler_params=_cp(("parallel",)),
        name="moe_combine_final" if final else "moe_combine",
    )(xn, rows2, route, mod, final_g.reshape(1, d))


def _moe(h2, route, wg, wu, wd):
    t, d = h2.shape
    n_slots = 2 * t
    flat_e = route[:, 0:2].astype(jnp.int32).reshape(-1)
    onehot = (flat_e[:, None] == jnp.arange(N_EXPERTS, dtype=jnp.int32)[None, :]).astype(F32)
    oh3 = onehot.reshape(n_slots // LANES, LANES, N_EXPERTS)
    tri = jnp.tril(jnp.ones((LANES, LANES), F32))
    intra = jnp.einsum("ij,bjk->bik", tri, oh3)
    blk_tot = intra[:, -1, :]
    csum = (intra + (jnp.cumsum(blk_tot, axis=0) - blk_tot)[:, None, :]).reshape(n_slots, N_EXPERTS)
    counts = csum[-1].astype(jnp.int32)
    rank = jnp.sum(onehot * (csum - onehot), axis=1).astype(jnp.int32)
    pcounts = (counts + MOE_TM - 1) // MOE_TM * MOE_TM
    pends = jnp.cumsum(pcounts)
    pstarts = pends - pcounts
    dest = pstarts[flat_e] + rank
    nblocks = -(-n_slots // MOE_TM) + N_EXPERTS
    nrows = -(-nblocks * MOE_TM // GATHER_ROWS) * GATHER_ROWS
    nblocks = nrows // MOE_TM
    src_tok = jnp.zeros((nrows,), jnp.int32).at[dest].set(jnp.arange(n_slots, dtype=jnp.int32) // 2)
    blk_start = jnp.arange(nblocks, dtype=jnp.int32) * MOE_TM
    blk_e = jnp.minimum(jnp.sum((pends[None, :] <= blk_start[:, None]).astype(jnp.int32), axis=1), N_EXPERTS - 1)
    n_used = (pends[-1] // MOE_TM).astype(jnp.int32).reshape(1)
    xs = _gather_rows(h2, src_tok)
    ys = _experts(xs, blk_e, n_used, wg, wu, wd)
    return _gather_rows(ys, dest.astype(jnp.int32)).reshape(t, 2 * d)


def kernel(x, c, ctx, c_ctx, ada_w, ada_b, norm1_g, norm2_g, w_in, w_out, s5_lam_re, s5_lam_im, s5_log_dt, s5_b_re, s5_b_im, s5_c_re, s5_c_im, s5_d, s5_glu_w, s5_glu_b, ga_qn_g, ga_kn_g, ssd_conv_w, ssd_conv_b, ssd_dt_bias, ssd_a_log, ssd_d, ssd_norm_g, wa_sink, moe_coarse_w, moe_coarse_b, moe_fine_w, moe_fine_b, moe_w_gate, moe_w_up, moe_w_down, final_g):
    nb, l, d = x.shape
    lc = ctx.shape[1]
    depth = ada_w.shape[0]
    assert lc == TM and l % TM == 0 and nb <= SUBLANES - 1 and d == D_MODEL
    s_len = lc + l
    nblk = s_len // TM
    t = nb * s_len

    xm = jnp.concatenate([ctx, x], axis=1).reshape(t, d)
    cc = jnp.zeros((SUBLANES, d), F32).at[:nb].set(c).at[nb].set(c_ctx)
    mods = _ada(cc, ada_w, ada_b).reshape(depth, SUBLANES, 6, d)
    cos_t, sin_t = _rope_tables(lc, l)

    out = None
    for i in range(depth):
        mod = mods[i]
        (xbc, u, z, dt, gaq, gak, gav, waq, wak, wav) = _inproj(
            xm, mod, norm1_g[i], _pack_w_in(w_in[i]), cos_t, sin_t, ga_qn_g[i], ga_kn_g[i], nb, nblk)
        ys5 = _s5(u, _s5_params(s5_lam_re[i], s5_lam_im[i], s5_log_dt[i], s5_b_re[i], s5_b_im[i],
                                s5_c_re[i], s5_c_im[i], s5_d[i]), nb, s_len, lc)
        oga = _ga(gaq, gak, gav, nb, s_len, lc)
        y0, y1 = _ssd(xbc, dt, ssd_conv_w[i], ssd_conv_b[i], ssd_dt_bias[i], ssd_a_log[i], ssd_d[i], nb, s_len, lc)
        owa = _wa(wa_sink[i], waq, wak, wav, nb, s_len, lc)
        wr, br = _pack_router(moe_coarse_w[i], moe_coarse_b[i], moe_fine_w[i], moe_fine_b[i])
        xn, h2, route = _outproj(xm, ys5, oga, y0, y1, z, owa, mod, s5_glu_w[i], s5_glu_b[i], ssd_norm_g[i],
                                 w_out[i], norm2_g[i], wr, br, nb, nblk)
        rows2 = _moe(h2, route, moe_w_gate[i], moe_w_up[i], moe_w_down[i])
        final = i == depth - 1
        xm = _combine(xn, rows2, route, mod, final_g, nb, nblk, final)
        if final:
            out = xm.reshape(nb, l, d)
    return out
```

```python
import functools
import math

import jax
import jax.numpy as jnp
import numpy as np
from jax import lax
from jax.experimental import pallas as pl
from jax.experimental.pallas import tpu as pltpu
from jax.experimental.pallas import tpu_sc as plsc

F32 = jnp.float32
BF16 = jnp.bfloat16
HI = lax.Precision.HIGHEST

D_MODEL = 1024
GRID_W = 64
GROUP_W = 256
HEAD_DIM = 64
ROPE_FREQS = HEAD_DIM // 4
ROPE_BASE = 10000.0
EPS = 1e-6
S5_CH = 16
S5_GROUPS = GROUP_W // S5_CH
S5_STATE = 64
N_HEADS = 4
SSD_HEADS = 4
SSD_NGROUPS = 2
SSD_STATE = 128
SSD_XBC = GROUP_W + 2 * SSD_NGROUPS * SSD_STATE
WINDOW = 128
MOE_GROUPS = 4
MOE_PER_GROUP = 8
N_EXPERTS = 32
D_EXPERT = D_MODEL // 2

LANES = 128
SUBLANES = 8
TM = 256
TQ = 128
S5_Q = 32
S5_BLK = S5_Q * S5_CH
MOE_TM = 256
SC_CORES = 2
SC_SUBCORES = 16
SC_GATHER_K = 32
GATHER_ROWS = SC_CORES * SC_SUBCORES * SC_GATHER_K
ROUTE_FINE0 = 32
VMEM_LIMIT = 56 * 1024 * 1024

NEG_INF = float("-inf")


def _cp(sem, vmem=VMEM_LIMIT):
    return pltpu.CompilerParams(dimension_semantics=sem, vmem_limit_bytes=vmem)


def _dot(a, b):
    return jnp.dot(a, b, preferred_element_type=F32)


def _dot_hi(a, b):
    return jnp.dot(a, b, preferred_element_type=F32, precision=HI)


def _dot_nt(a, b):
    return lax.dot_general(a, b, (((1,), (1,)), ((), ())), preferred_element_type=F32)


def _sigmoid(x):
    return 1.0 / (1.0 + jnp.exp(-x))


def _silu(x):
    return x * _sigmoid(x)


def _gelu_tanh(x):
    return 0.5 * x * (1.0 + jnp.tanh(math.sqrt(2.0 / math.pi) * (x + 0.044715 * (x * x * x))))


def _softplus(x):
    return jnp.maximum(x, 0.0) + jnp.log1p(jnp.exp(-jnp.abs(x)))


def _per_head_cols(v, base, n_heads, shape):
    lane = lax.broadcasted_iota(jnp.int32, shape, 1)
    out = jnp.broadcast_to(v[:, base + n_heads - 1:base + n_heads], shape)
    for h in range(n_heads - 2, -1, -1):
        out = jnp.where(lane < (h + 1) * HEAD_DIM, v[:, base + h:base + h + 1], out)
    return out


def _ada_kernel(c_ref, w_ref, b_ref, o_ref):
    c = c_ref[...]
    o_ref[0] = _dot_hi(_silu(c), w_ref[0]) + b_ref[0]


def _ada(cc, ada_w, ada_b):
    depth, d, n = ada_w.shape
    tn = 1536
    return pl.pallas_call(
        _ada_kernel,
        out_shape=jax.ShapeDtypeStruct((depth, SUBLANES, n), F32),
        grid=(depth, n // tn),
        in_specs=[pl.BlockSpec((SUBLANES, d), lambda l, j: (0, 0)),
                  pl.BlockSpec((1, d, tn), lambda l, j: (l, 0, j)),
                  pl.BlockSpec((1, 1, tn), lambda l, j: (l, 0, j))],
        out_specs=pl.BlockSpec((1, SUBLANES, tn), lambda l, j: (l, 0, j)),
        compiler_params=_cp(("parallel", "parallel")),
        name="ada_mod",
    )(cc, ada_w, ada_b.reshape(depth, 1, n))


_C_XBC = 0
_C_U = _C_XBC + SSD_XBC
_C_Z = _C_U + GROUP_W
_C_DT = _C_Z + GROUP_W
_C_GAQ = _C_DT + LANES
_C_WAQ = _C_GAQ + N_HEADS * LANES
_C_GAK = _C_WAQ + N_HEADS * LANES
_C_GAV = _C_GAK + LANES
_C_WAK = _C_GAV + LANES
_C_WAV = _C_WAK + LANES
_C_END = _C_WAV + LANES


def _expand_q_cols(wq):
    zero = jnp.zeros((wq.shape[0], HEAD_DIM), wq.dtype)
    parts = []
    for h in range(N_HEADS):
        head = wq[:, h * HEAD_DIM:(h + 1) * HEAD_DIM]
        parts += [head, zero] if h // 2 == 0 else [zero, head]
    return jnp.concatenate(parts, axis=1)


def _pack_w_in(w_in):
    cuts = np.cumsum([256, 256, 128, 128, 256, SSD_XBC, 2 * SSD_HEADS, 256, 128, 128])[:-1]
    u, gaq, gak, gav, z, xbc, dt, waq, wak, wav = jnp.split(w_in, [int(c) for c in cuts], axis=1)
    dt = jnp.pad(dt, ((0, 0), (0, LANES - dt.shape[1])))
    w = jnp.concatenate([xbc, u, z, dt, _expand_q_cols(gaq), _expand_q_cols(waq), gak, gav, wak, wav], axis=1)
    return w.astype(BF16)


def _rope(x, cos, sins):
    w = x.shape[1]
    if w > LANES:
        cos = jnp.concatenate([cos] * (w // LANES), axis=1)
        sins = jnp.concatenate([sins] * (w // LANES), axis=1)
    lane = lax.broadcasted_iota(jnp.int32, x.shape, 1)
    up = pltpu.roll(x, w - ROPE_FREQS, 1)
    dn = pltpu.roll(x, ROPE_FREQS, 1)
    partner = jnp.where((lane & ROPE_FREQS) == 0, up, dn)
    return x * cos + partner * sins


def _inproj_kernel(x_ref, mod_ref, g_ref, w_ref, cos_ref, sin_ref, qn_ref, kn_ref,
                   xbc_o, u_o, z_o, dt_o, gaq_o, gak_o, gav_o, waq_o, wak_o, wav_o):
    x = x_ref[...]
    ms = jnp.mean(x * x, axis=-1, keepdims=True)
    xn = x * lax.rsqrt(ms + EPS) * g_ref[...]
    h = xn * (1.0 + mod_ref[0, 1:2, :]) + mod_ref[0, 0:1, :]
    p = _dot(h.astype(BF16), w_ref[...])
    xbc_o[...] = p[:, _C_XBC:_C_U]
    u_o[...] = p[:, _C_U:_C_Z]
    z_o[...] = p[:, _C_Z:_C_DT]
    dt_o[...] = p[:, _C_DT:_C_GAQ]
    cos = cos_ref[...]
    sins = sin_ref[...]
    scale = HEAD_DIM ** -0.5
    q = p[:, _C_GAQ:_C_WAQ]
    qs = q * q
    inv = jnp.concatenate(
        [jnp.broadcast_to(lax.rsqrt(jnp.sum(qs[:, s * LANES:(s + 1) * LANES], axis=1, keepdims=True)
                                    * (1.0 / HEAD_DIM) + EPS), (q.shape[0], LANES)) for s in range(N_HEADS)], axis=1)
    gaq_o[...] = (_rope(q * inv * qn_ref[...], cos, sins) * scale).astype(BF16)
    waq_o[...] = (_rope(p[:, _C_WAQ:_C_GAK], cos, sins) * scale).astype(BF16)
    k = p[:, _C_GAK:_C_GAV]
    ks = k * k
    lane = lax.broadcasted_iota(jnp.int32, k.shape, 1)
    lo = lane < HEAD_DIM
    ms0 = jnp.sum(jnp.where(lo, ks, 0.0), axis=1, keepdims=True)
    ms1 = jnp.sum(jnp.where(lo, 0.0, ks), axis=1, keepdims=True)
    kinv = lax.rsqrt(jnp.where(lo, ms0, ms1) * (1.0 / HEAD_DIM) + EPS)
    gak_o[...] = _rope(k * kinv * kn_ref[...], cos, sins).astype(BF16)
    gav_o[...] = p[:, _C_GAV:_C_WAK].astype(BF16)
    wak_o[...] = _rope(p[:, _C_WAK:_C_WAV], cos, sins).astype(BF16)
    wav_o[...] = p[:, _C_WAV:_C_END].astype(BF16)


def _mod_row(i, nblk, nb):
    return jnp.where(i % nblk == 0, nb, i // nblk)


def _inproj(x, mod, norm_g, w_packed, cos_t, sin_t, qn_g, kn_g, nb, nblk):
    t, d = x.shape
    row = lambda i: (i, 0)
    fix = lambda i: (0, 0)
    outs = [(SSD_XBC, F32), (GROUP_W, F32), (GROUP_W, F32), (LANES, F32),
            (N_HEADS * LANES, BF16), (LANES, BF16), (LANES, BF16),
            (N_HEADS * LANES, BF16), (LANES, BF16), (LANES, BF16)]
    return pl.pallas_call(
        _inproj_kernel,
        out_shape=[jax.ShapeDtypeStruct((t, w), dt) for w, dt in outs],
        grid=(t // TM,),
        in_specs=[pl.BlockSpec((TM, d), row),
                  pl.BlockSpec((1, 6, d), lambda i: (_mod_row(i, nblk, nb), 0, 0)),
                  pl.BlockSpec((1, d), fix),
                  pl.BlockSpec((d, _C_END), fix),
                  pl.BlockSpec((TM, LANES), lambda i: (i % nblk, 0)),
                  pl.BlockSpec((TM, LANES), lambda i: (i % nblk, 0)),
                  pl.BlockSpec((1, N_HEADS * LANES), fix),
                  pl.BlockSpec((1, LANES), fix)],
        out_specs=[pl.BlockSpec((TM, w), row) for w, _ in outs],
        compiler_params=_cp(("parallel",)),
        name="in_proj",
    )(x, mod, norm_g.reshape(1, d), w_packed, cos_t, sin_t,
      jnp.tile(qn_g, 2 * N_HEADS).reshape(1, -1), jnp.tile(kn_g, 2).reshape(1, -1))


def _rope_tables(lc, l):
    n_rows = l // GRID_W
    rows = jnp.repeat(jnp.arange(n_rows), GRID_W)
    cols = jnp.tile(jnp.arange(GRID_W), n_rows)
    inv = jnp.power(ROPE_BASE, -jnp.arange(ROPE_FREQS, dtype=F32) / ROPE_FREQS)
    ang = jnp.stack([rows, cols], axis=-1).astype(F32)[..., None] * inv
    cos = jnp.cos(ang)
    sin = jnp.sin(ang)
    cos64 = jnp.stack([cos, cos], axis=2).reshape(l, HEAD_DIM)
    sin64 = jnp.stack([-sin, sin], axis=2).reshape(l, HEAD_DIM)
    cos64 = jnp.concatenate([jnp.ones((lc, HEAD_DIM), F32), cos64], axis=0)
    sin64 = jnp.concatenate([jnp.zeros((lc, HEAD_DIM), F32), sin64], axis=0)
    return jnp.tile(cos64, (1, 2)), jnp.tile(sin64, (1, 2))


def _merge_heads(o2, kvh):
    oa, ob = o2[:TQ], o2[TQ:]
    lane = lax.broadcasted_iota(jnp.int32, oa.shape, 1)
    if kvh == 0:
        return jnp.where(lane < HEAD_DIM, oa, pltpu.roll(ob, HEAD_DIM, 1))
    return jnp.where(lane < HEAD_DIM, pltpu.roll(oa, HEAD_DIM, 1), ob)


def _stack_q(q_ref, kvh):
    return jnp.concatenate([q_ref[:, (2 * kvh) * LANES:(2 * kvh + 1) * LANES],
                            q_ref[:, (2 * kvh + 1) * LANES:(2 * kvh + 2) * LANES]], axis=0)


def _ga_kernel(q_ref, k_ref, v_ref, o_ref, *, lc):
    j = pl.program_id(1)
    k = k_ref[...]
    v = v_ref[...]
    s_len = k.shape[0]
    limit = jnp.where(j < lc // TQ, lc, s_len)
    col = lax.broadcasted_iota(jnp.int32, (2 * TQ, s_len), 1)
    outs = []
    for kvh in range(2):
        s = _dot_nt(_stack_q(q_ref, kvh), k)
        s = jnp.where(col < limit, s, NEG_INF)
        m = jnp.max(s, axis=1, keepdims=True)
        p = jnp.exp(s - m)
        denom = jnp.sum(p, axis=1, keepdims=True)
        outs.append(_merge_heads(_dot(p.astype(BF16), v) / denom, kvh))
    o_ref[...] = jnp.concatenate(outs, axis=1)


def _ga(q, k, v, nb, s_len, lc):
    t = q.shape[0]
    nq = s_len // TQ
    return pl.pallas_call(
        functools.partial(_ga_kernel, lc=lc),
        out_shape=jax.ShapeDtypeStruct((t, GROUP_W), F32),
        grid=(nb, nq),
        in_specs=[pl.BlockSpec((TQ, N_HEADS * LANES), lambda b, j: (b * nq + j, 0)),
                  pl.BlockSpec((s_len, LANES), lambda b, j: (b, 0)),
                  pl.BlockSpec((s_len, LANES), lambda b, j: (b, 0))],
        out_specs=pl.BlockSpec((TQ, GROUP_W), lambda b, j: (b * nq + j, 0)),
        compiler_params=_cp(("parallel", "arbitrary")),
        name="global_attn",
    )(q, k, v)


def _wa_kernel(sink_ref, q_ref, k_ref, v_ref, o_ref, *, lc):
    j = pl.program_id(1)
    s_len = k_ref.shape[0]
    n = j - lc // TQ
    start = pl.multiple_of(jnp.clip(lc + (n - 1) * TQ, lc, s_len - 3 * TQ), TQ)
    kc = k_ref[0:lc, :]
    vc = v_ref[0:lc, :]
    kb = k_ref[pl.ds(start, 3 * TQ), :]
    vb = v_ref[pl.ds(start, 3 * TQ), :]
    qpos = n * TQ + lax.broadcasted_iota(jnp.int32, (TQ, 3 * TQ), 0)
    kpos = (start - lc) + lax.broadcasted_iota(jnp.int32, (TQ, 3 * TQ), 1)
    reach = jnp.where(n >= 0, WINDOW, -1)
    valid = jnp.abs(qpos - kpos) <= reach
    valid = jnp.concatenate([valid, valid], axis=0)
    row = lax.broadcasted_iota(jnp.int32, (2 * TQ, 1), 0)
    outs = []
    for kvh in range(2):
        q2 = _stack_q(q_ref, kvh)
        sc = _dot_nt(q2, kc)
        sb = jnp.where(valid, _dot_nt(q2, kb), NEG_INF)
        sink = jnp.where(row < TQ, sink_ref[2 * kvh], sink_ref[2 * kvh + 1])
        m = jnp.maximum(jnp.maximum(jnp.max(sc, axis=1, keepdims=True), jnp.max(sb, axis=1, keepdims=True)), sink)
        pc = jnp.exp(sc - m)
        pb = jnp.exp(sb - m)
        denom = jnp.sum(pc, axis=1, keepdims=True) + jnp.sum(pb, axis=1, keepdims=True) + jnp.exp(sink - m)
        o2 = (_dot(pc.astype(BF16), vc) + _dot(pb.astype(BF16), vb)) / denom
        outs.append(_merge_heads(o2, kvh))
    o_ref[...] = jnp.concatenate(outs, axis=1)


def _wa(sink, q, k, v, nb, s_len, lc):
    t = q.shape[0]
    nq = s_len // TQ
    return pl.pallas_call(
        functools.partial(_wa_kernel, lc=lc),
        out_shape=jax.ShapeDtypeStruct((t, GROUP_W), F32),
        grid=(nb, nq),
        in_specs=[pl.BlockSpec(memory_space=pltpu.SMEM),
                  pl.BlockSpec((TQ, N_HEADS * LANES), lambda b, j: (b * nq + j, 0)),
                  pl.BlockSpec((s_len, LANES), lambda b, j: (b, 0)),
                  pl.BlockSpec((s_len, LANES), lambda b, j: (b, 0))],
        out_specs=pl.BlockSpec((TQ, GROUP_W), lambda b, j: (b * nq + j, 0)),
        compiler_params=_cp(("parallel", "arbitrary")),
        name="window_attn",
    )(sink, q, k, v)


def _s5_chunk_index(t, rev, nc_ctx, nc_tot):
    if not rev:
        return t
    return jnp.where(t < nc_ctx, nc_ctx - 1 - t, nc_tot - 1 - (t - nc_ctx))


def _s5_kernel(u_ref, m_ref, p_ref, g_ref, ar_ref, ai_ref, dsk_ref, y_ref, s_scr, h_scr, *, nb, nc_ctx, nc_tot):
    uf = u_ref[0]
    u = uf.astype(BF16)
    for d in range(2):
        for k in range(2):
            s_scr[d, k] = _dot(u, p_ref[d, k, 0])
    ar = [jnp.broadcast_to(ar_ref[d, 0], (nb, LANES)) for d in range(2)]
    ai = [[jnp.broadcast_to(ai_ref[d, k, 0], (nb, LANES)) for k in range(2)] for d in range(2)]

    def body(t, carry):
        out = []
        for d in range(2):
            h, hs = carry[d]
            rows = pl.ds(_s5_chunk_index(t, d == 1, nc_ctx, nc_tot), nb, stride=nc_tot)
            h_scr[d, rows, :] = h
            out.append((ar[d] * h + ai[d][0] * hs + s_scr[d, 0, rows, :],
                        ar[d] * hs + ai[d][1] * h + s_scr[d, 1, rows, :]))
        return tuple(out)

    zero = jnp.zeros((nb, LANES), F32)
    lax.fori_loop(0, nc_tot, body, ((zero, zero), (zero, zero)), unroll=2)
    y = uf * dsk_ref[0]
    for d in range(2):
        y = y + _dot(u, m_ref[d, 0]) + _dot(h_scr[d].astype(BF16), g_ref[d, 0])
    y_ref[0] = y


S5_TB = TM // S5_Q
S5_GPS = LANES // S5_CH


def _s5_pack_kernel(lo_ref, hi_ref, o_ref):
    for s in range(S5_Q):
        rows = pl.ds(s, S5_TB, stride=S5_Q)
        halves = (lo_ref[rows, :], hi_ref[rows, :])
        dst = S5_CH * (s % S5_GPS)
        for g in range(S5_GROUPS):
            slab = halves[g // S5_GPS]
            src = S5_CH * (g % S5_GPS)
            moved = slab if src == dst else pltpu.roll(slab, (dst - src) % LANES, 1)
            o_ref[g, :, s * S5_CH:(s + 1) * S5_CH] = moved[:, dst:dst + S5_CH]


def _s5_unpack_kernel(y_ref, o_ref):
    lane_grp = lax.broadcasted_iota(jnp.int32, (S5_TB, LANES), 1) // S5_CH
    for s in range(S5_Q):
        src = S5_CH * (s % S5_GPS)
        for half in range(S5_GROUPS // S5_GPS):
            acc = None
            for gl in range(S5_GPS):
                slab = y_ref[half * S5_GPS + gl, :, (s // S5_GPS) * LANES:(s // S5_GPS + 1) * LANES]
                dst = S5_CH * gl
                moved = slab if src == dst else pltpu.roll(slab, (dst - src) % LANES, 1)
                acc = moved if acc is None else jnp.where(lane_grp == gl, moved, acc)
            o_ref[half, pl.ds(s, S5_TB, stride=S5_Q), :] = acc


def _s5_params(lam_re, lam_im, log_dt, b_re, b_im, c_re, c_im, d_skip):
    q = S5_Q
    dt = jnp.exp(log_dt)[..., None]
    lr, li = lam_re, lam_im
    mag = jnp.exp(lr * dt)
    a_re = mag * jnp.cos(li * dt)
    a_im = mag * jnp.sin(li * dt)
    den = lr * lr + li * li
    f_re = ((a_re - 1.0) * lr + a_im * li) / den
    f_im = (a_im * lr - (a_re - 1.0) * li) / den
    bb_re = f_re[..., None] * b_re - f_im[..., None] * b_im
    bb_im = f_re[..., None] * b_im + f_im[..., None] * b_re
    kk = jnp.arange(q + 1, dtype=F32)[:, None, None, None]
    pmag = jnp.exp(kk * (lr * dt))
    pw_re = pmag * jnp.cos(kk * (li * dt))
    pw_im = pmag * jnp.sin(kk * (li * dt))
    e_re = c_re[None] * pw_re[:, :, :, None, :] - c_im[None] * pw_im[:, :, :, None, :]
    e_im = c_re[None] * pw_im[:, :, :, None, :] + c_im[None] * pw_re[:, :, :, None, :]
    kern = (jnp.einsum("kdgop,dgpc->kdgoc", e_re, bb_re, precision=HI)
            - jnp.einsum("kdgop,dgpc->kdgoc", e_im, bb_im, precision=HI))
    kern_t = jnp.pad(kern.transpose(1, 2, 4, 0, 3), ((0, 0), (0, 0), (0, 0), (0, 1), (0, 0)))
    bbt_re = bb_re.transpose(0, 1, 3, 2)
    bbt_im = bb_im.transpose(0, 1, 3, 2)
    ct_re = c_re.transpose(0, 1, 3, 2)[:, :, :, None, :]
    ct_im = c_im.transpose(0, 1, 3, 2)[:, :, :, None, :]
    s_i = jnp.arange(q)[:, None]
    t_i = jnp.arange(q)[None, :]
    ms, ps, gs = [], [], []
    for d in range(2):
        tau = (t_i - s_i) if d == 0 else (s_i - t_i)
        lag = jnp.where(tau >= 0, tau, q + 1)
        kd = kern_t[d][:, :, lag, :]
        ms.append(kd.transpose(0, 2, 1, 3, 4).reshape(S5_GROUPS, S5_BLK, S5_BLK))
        pidx = (q - 1 - jnp.arange(q)) if d == 0 else jnp.arange(q)
        pr = pw_re[pidx, d][:, :, None, :]
        pi = pw_im[pidx, d][:, :, None, :]
        p_re = pr * bbt_re[d][None] - pi * bbt_im[d][None]
        p_im = pr * bbt_im[d][None] + pi * bbt_re[d][None]
        pd = jnp.stack([jnp.concatenate([p_re, p_im], axis=3), jnp.concatenate([p_im, p_re], axis=3)])
        ps.append(pd.transpose(0, 2, 1, 3, 4).reshape(2, S5_GROUPS, S5_BLK, 2 * S5_STATE))
        gidx = (jnp.arange(q) + 1) if d == 0 else (q - jnp.arange(q))
        gw_re = pw_re[gidx, d].transpose(1, 2, 0)[..., None]
        gw_im = pw_im[gidx, d].transpose(1, 2, 0)[..., None]
        g_re = ct_re[d] * gw_re - ct_im[d] * gw_im
        g_im = ct_re[d] * gw_im + ct_im[d] * gw_re
        gs.append(jnp.concatenate([g_re, -g_im], axis=1).reshape(S5_GROUPS, 2 * S5_STATE, S5_BLK))
    ar = jnp.concatenate([pw_re[q], pw_re[q]], axis=-1)[:, :, None, :]
    ai = jnp.stack([jnp.concatenate([-pw_im[q], pw_im[q]], axis=-1),
                    jnp.concatenate([pw_im[q], -pw_im[q]], axis=-1)], axis=1)[:, :, :, None, :]
    dsk = jnp.tile(d_skip.reshape(S5_GROUPS, 1, S5_CH), (1, 1, q))
    return (jnp.stack(ms).astype(BF16), jnp.stack(ps).astype(BF16), jnp.stack(gs).astype(BF16),
            ar.astype(F32), ai.astype(F32), dsk.astype(F32))


def _s5(u, params, nb, s_len, lc):
    m, p, g, ar, ai, dsk = params
    nc_tot = s_len // S5_Q
    nc_ctx = lc // S5_Q
    t = u.shape[0]
    r = nb * nc_tot
    grp = pl.BlockSpec((S5_GROUPS, S5_TB, S5_BLK), lambda i: (0, i, 0))
    ug = pl.pallas_call(
        _s5_pack_kernel,
        out_shape=jax.ShapeDtypeStruct((S5_GROUPS, r, S5_BLK), F32),
        grid=(t // TM,),
        in_specs=[pl.BlockSpec((TM, LANES), lambda i: (i, 0)), pl.BlockSpec((TM, LANES), lambda i: (i, 1))],
        out_specs=grp,
        compiler_params=_cp(("parallel",)), name="s5_pack",
    )(u, u)
    y = pl.pallas_call(
        functools.partial(_s5_kernel, nb=nb, nc_ctx=nc_ctx, nc_tot=nc_tot),
        out_shape=jax.ShapeDtypeStruct((S5_GROUPS, r, S5_BLK), F32),
        grid=(S5_GROUPS,),
        in_specs=[pl.BlockSpec((1, r, S5_BLK), lambda gi: (gi, 0, 0)),
                  pl.BlockSpec((2, 1, S5_BLK, S5_BLK), lambda gi: (0, gi, 0, 0)),
                  pl.BlockSpec((2, 2, 1, S5_BLK, 2 * S5_STATE), lambda gi: (0, 0, gi, 0, 0)),
                  pl.BlockSpec((2, 1, 2 * S5_STATE, S5_BLK), lambda gi: (0, gi, 0, 0)),
                  pl.BlockSpec((2, 1, 1, 2 * S5_STATE), lambda gi: (0, gi, 0, 0)),
                  pl.BlockSpec((2, 2, 1, 1, 2 * S5_STATE), lambda gi: (0, 0, gi, 0, 0)),
                  pl.BlockSpec((1, 1, S5_BLK), lambda gi: (gi, 0, 0))],
        out_specs=pl.BlockSpec((1, r, S5_BLK), lambda gi: (gi, 0, 0)),
        scratch_shapes=[pltpu.VMEM((2, 2, r, 2 * S5_STATE), F32), pltpu.VMEM((2, r, 2 * S5_STATE), F32)],
        compiler_params=_cp(("parallel",)),
        name="s5_scan",
    )(ug, m, p, g, ar, ai, dsk)
    return pl.pallas_call(
        _s5_unpack_kernel,
        out_shape=jax.ShapeDtypeStruct((GROUP_W // LANES, t, LANES), F32),
        grid=(t // TM,), in_specs=[grp],
        out_specs=pl.BlockSpec((GROUP_W // LANES, TM, LANES), lambda i: (0, i, 0)),
        compiler_params=_cp(("parallel",)), name="s5_unpack",
    )(y)


def _conv_kernel(x_ref, prev_ref, next_ref, w_ref, b_ref, o_ref, *, nblk):
    i = pl.program_id(0) % nblk
    x = x_ref[...]
    rows = x.shape[0]
    ridx = lax.broadcasted_iota(jnp.int32, x.shape, 0)
    prev_row = jnp.where(i <= 1, 0.0, prev_ref[SUBLANES - 1:SUBLANES, :])
    next_row = jnp.where(jnp.logical_or(i == 0, i == nblk - 1), 0.0, next_ref[0:1, :])
    xm = jnp.where(ridx == 0, prev_row, pltpu.roll(x, 1, 0))
    xp = jnp.where(ridx == rows - 1, next_row, pltpu.roll(x, rows - 1, 0))
    y = xm * w_ref[0:1, :] + x * w_ref[1:2, :] + xp * w_ref[2:3, :] + b_ref[...]
    o_ref[...] = _silu(y)


def _conv(xbc, w, b, nblk):
    t, c = xbc.shape
    per = TM // SUBLANES
    last = t // SUBLANES - 1
    return pl.pallas_call(
        functools.partial(_conv_kernel, nblk=nblk),
        out_shape=jax.ShapeDtypeStruct((t, c), F32),
        grid=(t // TM,),
        in_specs=[pl.BlockSpec((TM, c), lambda i: (i, 0)),
                  pl.BlockSpec((SUBLANES, c), lambda i: (jnp.maximum(i * per - 1, 0), 0)),
                  pl.BlockSpec((SUBLANES, c), lambda i: (jnp.minimum((i + 1) * per, last), 0)),
                  pl.BlockSpec((3, c), lambda i: (0, 0)),
                  pl.BlockSpec((1, c), lambda i: (0, 0))],
        out_specs=pl.BlockSpec((TM, c), lambda i: (i, 0)),
        compiler_params=_cp(("parallel",)),
        name="ssd_conv",
    )(xbc, xbc, xbc, w, b.reshape(1, c))


_X_B = GROUP_W
_X_C = GROUP_W + SSD_NGROUPS * SSD_STATE


def _ssd_kernel(xc_ref, dt_ref, dtt_ref, bias_ref, a_ref, biast_ref, at_ref, dsk_ref, y_ref, st_ref, *, rev):
    c = pl.program_id(1)

    @pl.when(c == 0)
    def _():
        st_ref[...] = jnp.zeros_like(st_ref)

    base = SSD_HEADS if rev else 0
    xc = xc_ref[...]
    x = xc[:, 0:GROUP_W]
    dt = _softplus(dt_ref[...] + bias_ref[...])
    a = dt * a_ref[...]
    dtt = _softplus(dtt_ref[0] + biast_ref[...])
    at = dtt * at_ref[...]
    ri = lax.broadcasted_iota(jnp.int32, (TQ, TQ), 0)
    ci = lax.broadcasted_iota(jnp.int32, (TQ, TQ), 1)
    causal = (ci >= ri) if rev else (ri >= ci)
    tri = jnp.where(causal, 1.0, 0.0)
    cum_c = _dot_hi(tri, a)
    cum_r = _dot_nt_hi(at, tri)
    edge = 0 if rev else TQ - 1
    tot = cum_c[edge:edge + 1, :]

    shape = (TQ, GROUP_W)
    xdt = x * _per_head_cols(dt, base, SSD_HEADS, shape)
    lane = lax.broadcasted_iota(jnp.int32, shape, 1)
    y = jnp.zeros(shape, F32)
    bmat = [xc[:, _X_B + g * SSD_STATE:_X_B + (g + 1) * SSD_STATE].astype(BF16) for g in range(SSD_NGROUPS)]
    cmat = [xc[:, _X_C + g * SSD_STATE:_X_C + (g + 1) * SSD_STATE].astype(BF16) for g in range(SSD_NGROUPS)]
    cb = [_dot_nt(cmat[g], bmat[g]) for g in range(SSD_NGROUPS)]
    for h in range(SSD_HEADS):
        col = base + h
        seg = jnp.where(causal, cum_c[:, col:col + 1] - cum_r[col:col + 1, :], NEG_INF)
        scores = cb[h // 2] * jnp.exp(seg)
        xh = jnp.where((lane >= h * HEAD_DIM) & (lane < (h + 1) * HEAD_DIM), xdt, 0.0)
        y = y + _dot(scores.astype(BF16), xh.astype(BF16))
    st = st_ref[...]
    yo = jnp.concatenate(
        [_dot_nt(cmat[g], st[g * SSD_STATE:(g + 1) * SSD_STATE].astype(BF16)) for g in range(SSD_NGROUPS)], axis=1)
    y = y + yo * _per_head_cols(jnp.exp(cum_c), base, SSD_HEADS, shape)
    if not rev:
        y = y + x * dsk_ref[...]
    y_ref[...] = y
    xd = xdt * _per_head_cols(jnp.exp(tot - cum_c), base, SSD_HEADS, shape)
    xdt_t = xd.T.astype(BF16)
    decay = jnp.exp(tot)
    for g in range(SSD_NGROUPS):
        new = _dot(xdt_t[g * SSD_STATE:(g + 1) * SSD_STATE], bmat[g])
        for hh in range(2):
            h = 2 * g + hh
            r0 = h * HEAD_DIM
            st_ref[r0:r0 + HEAD_DIM, :] = (decay[:, base + h:base + h + 1] * st[r0:r0 + HEAD_DIM]
                                           + new[hh * HEAD_DIM:(hh + 1) * HEAD_DIM])


def _dot_nt_hi(a, b):
    return lax.dot_general(a, b, (((1,), (1,)), ((), ())), preferred_element_type=F32, precision=HI)


def _ssd_chunk(c, rev, nc_ctx, nc_tot):
    if not rev:
        return c
    return jnp.where(c < nc_ctx, nc_ctx - 1 - c, nc_tot - 1 - (c - nc_ctx))


def _ssd_dir(xc, dt, dtt, bias, a, biast, at, dsk, rev, nb, s_len, lc):
    t = xc.shape[0]
    nc_tot = s_len // TQ
    nc_ctx = lc // TQ
    cidx = lambda c: _ssd_chunk(c, rev, nc_ctx, nc_tot)
    fix = lambda b, c: (0, 0)
    return pl.pallas_call(
        functools.partial(_ssd_kernel, rev=rev),
        out_shape=jax.ShapeDtypeStruct((t, GROUP_W), F32),
        grid=(nb, nc_tot),
        in_specs=[pl.BlockSpec((TQ, SSD_XBC), lambda b, c: (b * nc_tot + cidx(c), 0)),
                  pl.BlockSpec((TQ, LANES), lambda b, c: (b * nc_tot + cidx(c), 0)),
                  pl.BlockSpec((1, SUBLANES, TQ), lambda b, c: (b, 0, cidx(c))),
                  pl.BlockSpec((1, LANES), fix),
                  pl.BlockSpec((1, LANES), fix),
                  pl.BlockSpec((SUBLANES, TQ), fix),
                  pl.BlockSpec((SUBLANES, TQ), fix),
                  pl.BlockSpec((1, GROUP_W), fix)],
        out_specs=pl.BlockSpec((TQ, GROUP_W), lambda b, c: (b * nc_tot + cidx(c), 0)),
        scratch_shapes=[pltpu.VMEM((SSD_HEADS * HEAD_DIM, SSD_STATE), F32)],
        compiler_params=_cp(("parallel", "arbitrary")),
        name="ssd_rev" if rev else "ssd_fwd",
    )(xc, dt, dtt, bias, a, biast, at, dsk)


def _ssd(xbc, dt, conv_w, conv_b, dt_bias, a_log, d_skip, nb, s_len, lc):
    nblk = s_len // TM
    xc = _conv(xbc, conv_w, conv_b, nblk)
    nd = 2 * SSD_HEADS
    dtt = dt[:, :nd].reshape(nb, s_len, nd).transpose(0, 2, 1)
    bias = jnp.pad(dt_bias.reshape(1, nd), ((0, 0), (0, LANES - nd)))
    a = jnp.pad(-jnp.exp(a_log).reshape(1, nd), ((0, 0), (0, LANES - nd)))
    biast = jnp.broadcast_to(dt_bias.reshape(nd, 1), (nd, TQ))
    at = jnp.broadcast_to(-jnp.exp(a_log).reshape(nd, 1), (nd, TQ))
    dsk = jnp.repeat(d_skip, HEAD_DIM).reshape(1, GROUP_W)
    args = (xc, dt, dtt, bias, a, biast, at, dsk)
    return _ssd_dir(*args, False, nb, s_len, lc), _ssd_dir(*args, True, nb, s_len, lc)


def _outproj_kernel(x_ref, ys5_ref, oga_ref, y0_ref, y1_ref, z_ref, owa_ref, mod_ref, gluw_ref, glub_ref,
                    ng_ref, wout_ref, n2_ref, wr_ref, br_ref, xn_o, h2_o, route_o):
    gl = _gelu_tanh(jnp.concatenate([ys5_ref[0], ys5_ref[1]], axis=1))
    a = gl * _sigmoid(_dot(gl.astype(BF16), gluw_ref[...]) + glub_ref[...])
    m = (y0_ref[...] + y1_ref[...]) * _silu(z_ref[...])
    m = m * lax.rsqrt(jnp.mean(m * m, axis=-1, keepdims=True) + EPS) * ng_ref[...]
    w = wout_ref
    mix = (_dot(a.astype(BF16), w[0:GROUP_W, :]) + _dot(oga_ref[...].astype(BF16), w[GROUP_W:2 * GROUP_W, :])
           + _dot(m.astype(BF16), w[2 * GROUP_W:3 * GROUP_W, :]) + _dot(owa_ref[...].astype(BF16), w[3 * GROUP_W:, :]))
    xn = x_ref[...] + mod_ref[0, 2:3, :] * mix
    xn_o[...] = xn
    h2 = xn * lax.rsqrt(jnp.mean(xn * xn, axis=-1, keepdims=True) + EPS) * n2_ref[...]
    h2 = h2 * (1.0 + mod_ref[0, 4:5, :]) + mod_ref[0, 3:4, :]
    h2_o[...] = h2
    logits = _dot_hi(h2, wr_ref[...]) + br_ref[...]
    lane = lax.broadcasted_iota(jnp.int32, logits.shape, 1)
    big = 4 * LANES
    lcoarse = jnp.where(lane < MOE_GROUPS, logits, NEG_INF)
    mx = jnp.max(lcoarse, axis=1, keepdims=True)
    den = jnp.sum(jnp.exp(lcoarse - mx), axis=1, keepdims=True)
    grp = jnp.min(jnp.where(lcoarse == mx, lane, big), axis=1, keepdims=True)
    pg = 1.0 / den
    lo = ROUTE_FINE0 + grp * MOE_PER_GROUP
    lf = jnp.where(lane >= lo, jnp.where(lane < lo + MOE_PER_GROUP, logits, NEG_INF), NEG_INF)
    v1 = jnp.max(lf, axis=1, keepdims=True)
    i1 = jnp.min(jnp.where(lf == v1, lane, big), axis=1, keepdims=True)
    lf2 = jnp.where(lane == i1, NEG_INF, lf)
    v2 = jnp.max(lf2, axis=1, keepdims=True)
    i2 = jnp.min(jnp.where(lf2 == v2, lane, big), axis=1, keepdims=True)
    e2 = jnp.exp(v2 - v1)
    w1 = pg / (1.0 + e2)
    w2 = w1 * e2
    route = jnp.where(lane == 0, (i1 - ROUTE_FINE0).astype(F32),
                      jnp.where(lane == 1, (i2 - ROUTE_FINE0).astype(F32),
                                jnp.where(lane == 2, w1, jnp.where(lane == 3, w2, 0.0))))
    route_o[...] = route


def _outproj(x, ys5, oga, y0, y1, z, owa, mod, glu_w, glu_b, ssd_norm_g, w_out, norm2_g, wr, br, nb, nblk):
    t, d = x.shape
    row = lambda i: (i, 0)
    fix = lambda i: (0, 0)
    gw = pl.BlockSpec((TM, GROUP_W), row)
    return pl.pallas_call(
        _outproj_kernel,
        out_shape=[jax.ShapeDtypeStruct((t, d), F32), jax.ShapeDtypeStruct((t, d), F32),
                   jax.ShapeDtypeStruct((t, LANES), F32)],
        grid=(t // TM,),
        in_specs=[pl.BlockSpec((TM, d), row), pl.BlockSpec((GROUP_W // LANES, TM, LANES), lambda i: (0, i, 0)),
                  gw, gw, gw, gw, gw,
                  pl.BlockSpec((1, 6, d), lambda i: (_mod_row(i, nblk, nb), 0, 0)),
                  pl.BlockSpec((GROUP_W, GROUP_W), fix),
                  pl.BlockSpec((1, GROUP_W), fix),
                  pl.BlockSpec((1, GROUP_W), fix),
                  pl.BlockSpec((d, d), fix),
                  pl.BlockSpec((1, d), fix),
                  pl.BlockSpec((d, LANES), fix),
                  pl.BlockSpec((1, LANES), fix)],
        out_specs=[pl.BlockSpec((TM, d), row), pl.BlockSpec((TM, d), row), pl.BlockSpec((TM, LANES), row)],
        compiler_params=_cp(("parallel",)),
        name="out_proj_router",
    )(x, ys5, oga, y0, y1, z, owa, mod, glu_w.astype(BF16), glu_b.reshape(1, -1), ssd_norm_g.reshape(1, -1),
      w_out.astype(BF16), norm2_g.reshape(1, -1), wr, br)


def _pack_router(coarse_w, coarse_b, fine_w, fine_b):
    d = coarse_w.shape[0]
    wr = jnp.zeros((d, LANES), F32)
    wr = wr.at[:, 0:MOE_GROUPS].set(coarse_w).at[:, ROUTE_FINE0:ROUTE_FINE0 + N_EXPERTS].set(fine_w)
    br = jnp.zeros((1, LANES), F32)
    br = br.at[0, 0:MOE_GROUPS].set(coarse_b).at[0, ROUTE_FINE0:ROUTE_FINE0 + N_EXPERTS].set(fine_b)
    return wr, br


def _gather_rows(src, idx):
    m = idx.shape[0]
    d = src.shape[1]
    workers = SC_CORES * SC_SUBCORES
    nch = m // (workers * SC_GATHER_K)
    assert nch * workers * SC_GATHER_K == m
    mesh = plsc.VectorSubcoreMesh(core_axis_name="c", subcore_axis_name="s")

    @functools.partial(
        pl.kernel, mesh=mesh,
        out_type=jax.ShapeDtypeStruct((m, d), src.dtype),
        scratch_types=[pltpu.VMEM((nch, SC_GATHER_K), jnp.int32),
                       pltpu.VMEM((SC_GATHER_K, d), src.dtype),
                       pltpu.SemaphoreType.DMA],
    )
    def gather(src_hbm, idx_hbm, out_hbm, idx_v, rows_v, sem):
        wid = lax.axis_index("s") * SC_CORES + lax.axis_index("c")
        pltpu.sync_copy(idx_hbm.at[wid], idx_v)

        @pl.loop(0, nch)
        def _(j):
            off = pl.multiple_of((wid * nch + j) * SC_GATHER_K, SC_GATHER_K)
            pltpu.async_copy(src_hbm.at[idx_v.at[j]], rows_v, sem).wait()
            pltpu.sync_copy(rows_v, out_hbm.at[pl.ds(off, SC_GATHER_K)])

    return gather(src, idx.reshape(workers, nch, SC_GATHER_K))


def _scatter_rows(src, dst0, dst1, nrows):
    t, d = src.shape
    workers = SC_CORES * SC_SUBCORES
    nch = t // (workers * SC_GATHER_K)
    assert nch * workers * SC_GATHER_K == t
    mesh = plsc.VectorSubcoreMesh(core_axis_name="c", subcore_axis_name="s")

    @functools.partial(
        pl.kernel, mesh=mesh,
        out_type=jax.ShapeDtypeStruct((nrows, d), src.dtype),
        scratch_types=[pltpu.VMEM((nch, SC_GATHER_K), jnp.int32),
                       pltpu.VMEM((nch, SC_GATHER_K), jnp.int32),
                       pltpu.VMEM((SC_GATHER_K, d), src.dtype)],
    )
    def scatter(src_hbm, d0_hbm, d1_hbm, out_hbm, i0_v, i1_v, rows_v):
        wid = lax.axis_index("s") * SC_CORES + lax.axis_index("c")
        pltpu.sync_copy(d0_hbm.at[wid], i0_v)
        pltpu.sync_copy(d1_hbm.at[wid], i1_v)

        @pl.loop(0, nch)
        def _(j):
            off = pl.multiple_of((wid * nch + j) * SC_GATHER_K, SC_GATHER_K)
            pltpu.sync_copy(src_hbm.at[pl.ds(off, SC_GATHER_K)], rows_v)
            pltpu.sync_copy(rows_v, out_hbm.at[i0_v.at[j]])
            pltpu.sync_copy(rows_v, out_hbm.at[i1_v.at[j]])

    return scatter(src, dst0.reshape(workers, nch, SC_GATHER_K), dst1.reshape(workers, nch, SC_GATHER_K))


def _expert_kernel(be_ref, nused_ref, nvalid_ref, x_ref, wg_ref, wu_ref, wd_ref, o_ref, wg_s, wu_s, wd_s):
    i = pl.program_id(0)
    new_expert = jnp.logical_or(i == 0, be_ref[i] != be_ref[jnp.maximum(i - 1, 0)])

    @pl.when(jnp.logical_and(i < nused_ref[0], new_expert))
    def _():
        wg_s[...] = wg_ref[0, 0].astype(BF16)
        wu_s[...] = wu_ref[0, 0].astype(BF16)
        wd_s[...] = wd_ref[0, 0].astype(BF16)

    @pl.when(i < nused_ref[0])
    def _():
        row = lax.broadcasted_iota(jnp.int32, x_ref.shape, 0)
        x = jnp.where(row < nvalid_ref[i], x_ref[...], 0.0).astype(BF16)
        hid = _silu(_dot(x, wg_s[...])) * _dot(x, wu_s[...])
        o_ref[...] = _dot(hid.astype(BF16), wd_s[...])

    @pl.when(i >= nused_ref[0])
    def _():
        o_ref[...] = jnp.zeros_like(o_ref)


def _experts(xs, blk_e, n_used, n_valid, wg, wu, wd, layer):
    rows, d = xs.shape
    nblocks = rows // MOE_TM
    de = wg.shape[3]
    wsel = lambda i, be, nu, nv: (layer, be[i], 0, 0)
    grid_spec = pltpu.PrefetchScalarGridSpec(
        num_scalar_prefetch=3,
        grid=(nblocks,),
        in_specs=[pl.BlockSpec((MOE_TM, d), lambda i, be, nu, nv: (i, 0)),
                  pl.BlockSpec((1, 1, d, de), wsel),
                  pl.BlockSpec((1, 1, d, de), wsel),
                  pl.BlockSpec((1, 1, de, d), wsel)],
        out_specs=pl.BlockSpec((MOE_TM, d), lambda i, be, nu, nv: (i, 0)),
        scratch_shapes=[pltpu.VMEM((d, de), BF16), pltpu.VMEM((d, de), BF16), pltpu.VMEM((de, d), BF16)],
    )
    return pl.pallas_call(
        _expert_kernel,
        out_shape=jax.ShapeDtypeStruct((rows, d), F32),
        grid_spec=grid_spec,
        compiler_params=_cp(("arbitrary",)),
        name="moe_experts",
    )(blk_e, n_used, n_valid, xs, wg, wu, wd)


def _combine_kernel(x_ref, r_ref, route_ref, mod_ref, fg_ref, o_ref, *, final):
    d = x_ref.shape[1]
    route = route_ref[...]
    f = route[:, 2:3] * r_ref[:, 0:d] + route[:, 3:4] * r_ref[:, d:2 * d]
    y = x_ref[...] + mod_ref[0, 5:6, :] * f
    if final:
        y = y * lax.rsqrt(jnp.mean(y * y, axis=-1, keepdims=True) + EPS) * fg_ref[...]
    o_ref[...] = y


def _combine(xn, rows2, route, mod, final_g, nb, nblk, final):
    t, d = xn.shape
    if final:
        nlat = nblk - 1
        grid = (nb * nlat,)
        src = lambda i: ((i // nlat) * nblk + 1 + i % nlat, 0)
        modi = lambda i: (i // nlat, 0, 0)
        out_rows = nb * nlat * TM
    else:
        grid = (t // TM,)
        src = lambda i: (i, 0)
        modi = lambda i: (_mod_row(i, nblk, nb), 0, 0)
        out_rows = t
    return pl.pallas_call(
        functools.partial(_combine_kernel, final=final),
        out_shape=jax.ShapeDtypeStruct((out_rows, d), F32),
        grid=grid,
        in_specs=[pl.BlockSpec((TM, d), src),
                  pl.BlockSpec((TM, 2 * d), src),
                  pl.BlockSpec((TM, LANES), src),
                  pl.BlockSpec((1, 6, d), modi),
                  pl.BlockSpec((1, d), lambda i: (0, 0))],
        out_specs=pl.BlockSpec((TM, d), lambda i: (i, 0)),
        compiler_params=_cp(("parallel",)),
        name="moe_combine_final" if final else "moe_combine",
    )(xn, rows2, route, mod, final_g.reshape(1, d))


def _moe(h2, route, wg, wu, wd, layer):
    t, d = h2.shape
    n_slots = 2 * t
    flat_e = route[:, 0:2].astype(jnp.int32).reshape(-1)
    onehot = (flat_e[:, None] == jnp.arange(N_EXPERTS, dtype=jnp.int32)[None, :]).astype(F32)
    oh3 = onehot.reshape(n_slots // LANES, LANES, N_EXPERTS)
    tri = jnp.tril(jnp.ones((LANES, LANES), F32))
    intra = jnp.einsum("ij,bjk->bik", tri, oh3)
    blk_tot = intra[:, -1, :]
    csum = (intra + (jnp.cumsum(blk_tot, axis=0) - blk_tot)[:, None, :]).reshape(n_slots, N_EXPERTS)
    counts = csum[-1].astype(jnp.int32)
    rank = jnp.sum(onehot * (csum - onehot), axis=1).astype(jnp.int32)
    pcounts = (counts + MOE_TM - 1) // MOE_TM * MOE_TM
    pends = jnp.cumsum(pcounts)
    pstarts = pends - pcounts
    dest = pstarts[flat_e] + rank
    nblocks = -(-n_slots // MOE_TM) + N_EXPERTS
    nrows = -(-nblocks * MOE_TM // GATHER_ROWS) * GATHER_ROWS
    nblocks = nrows // MOE_TM
    blk_start = jnp.arange(nblocks, dtype=jnp.int32) * MOE_TM
    blk_e = jnp.minimum(jnp.sum((pends[None, :] <= blk_start[:, None]).astype(jnp.int32), axis=1), N_EXPERTS - 1)
    n_used = (pends[-1] // MOE_TM).astype(jnp.int32).reshape(1)
    n_valid = jnp.clip((pstarts + counts)[blk_e] - blk_start, 0, MOE_TM).astype(jnp.int32)
    dest2 = dest.astype(jnp.int32).reshape(t, 2)
    xs = _scatter_rows(h2, dest2[:, 0], dest2[:, 1], nrows)
    ys = _experts(xs, blk_e, n_used, n_valid, wg, wu, wd, layer)
    return _gather_rows(ys, dest.astype(jnp.int32)).reshape(t, 2 * d)


def kernel(x, c, ctx, c_ctx, ada_w, ada_b, norm1_g, norm2_g, w_in, w_out, s5_lam_re, s5_lam_im, s5_log_dt, s5_b_re, s5_b_im, s5_c_re, s5_c_im, s5_d, s5_glu_w, s5_glu_b, ga_qn_g, ga_kn_g, ssd_conv_w, ssd_conv_b, ssd_dt_bias, ssd_a_log, ssd_d, ssd_norm_g, wa_sink, moe_coarse_w, moe_coarse_b, moe_fine_w, moe_fine_b, moe_w_gate, moe_w_up, moe_w_down, final_g):
    nb, l, d = x.shape
    lc = ctx.shape[1]
    depth = ada_w.shape[0]
    assert lc == TM and l % TM == 0 and nb <= SUBLANES - 1 and d == D_MODEL
    s_len = lc + l
    nblk = s_len // TM
    t = nb * s_len

    xm = jnp.concatenate([ctx, x], axis=1).reshape(t, d)
    cc = jnp.zeros((SUBLANES, d), F32).at[:nb].set(c).at[nb].set(c_ctx)
    mods = _ada(cc, ada_w, ada_b).reshape(depth, SUBLANES, 6, d)
    cos_t, sin_t = _rope_tables(lc, l)

    out = None
    for i in range(depth):
        mod = mods[i]
        (xbc, u, z, dt, gaq, gak, gav, waq, wak, wav) = _inproj(
            xm, mod, norm1_g[i], _pack_w_in(w_in[i]), cos_t, sin_t, ga_qn_g[i], ga_kn_g[i], nb, nblk)
        ys5 = _s5(u, _s5_params(s5_lam_re[i], s5_lam_im[i], s5_log_dt[i], s5_b_re[i], s5_b_im[i],
                                s5_c_re[i], s5_c_im[i], s5_d[i]), nb, s_len, lc)
        oga = _ga(gaq, gak, gav, nb, s_len, lc)
        y0, y1 = _ssd(xbc, dt, ssd_conv_w[i], ssd_conv_b[i], ssd_dt_bias[i], ssd_a_log[i], ssd_d[i], nb, s_len, lc)
        owa = _wa(wa_sink[i], waq, wak, wav, nb, s_len, lc)
        wr, br = _pack_router(moe_coarse_w[i], moe_coarse_b[i], moe_fine_w[i], moe_fine_b[i])
        xn, h2, route = _outproj(xm, ys5, oga, y0, y1, z, owa, mod, s5_glu_w[i], s5_glu_b[i], ssd_norm_g[i],
                                 w_out[i], norm2_g[i], wr, br, nb, nblk)
        rows2 = _moe(h2, route, moe_w_gate, moe_w_up, moe_w_down, i)
        final = i == depth - 1
        xm = _combine(xn, rows2, route, mod, final_g, nb, nblk, final)
        if final:
            out = xm.reshape(nb, l, d)
    return out
```

```python
import functools
import math

import jax
import jax.numpy as jnp
import numpy as np
from jax import lax
from jax.experimental import pallas as pl
from jax.experimental.pallas import tpu as pltpu
from jax.experimental.pallas import tpu_sc as plsc

F32 = jnp.float32
BF16 = jnp.bfloat16
HI = lax.Precision.HIGHEST

D_MODEL = 1024
GRID_W = 64
GROUP_W = 256
HEAD_DIM = 64
ROPE_FREQS = HEAD_DIM // 4
ROPE_BASE = 10000.0
EPS = 1e-6
S5_CH = 16
S5_GROUPS = GROUP_W // S5_CH
S5_STATE = 64
N_HEADS = 4
SSD_HEADS = 4
SSD_NGROUPS = 2
SSD_STATE = 128
SSD_XBC = GROUP_W + 2 * SSD_NGROUPS * SSD_STATE
WINDOW = 128
MOE_GROUPS = 4
MOE_PER_GROUP = 8
N_EXPERTS = 32
D_EXPERT = D_MODEL // 2

LANES = 128
SUBLANES = 8
TM = 256
TQ = 128
GA_TQ = 128
S5_Q = 32
S5_BLK = S5_Q * S5_CH
MOE_TM = 256
SC_CORES = 2
SC_SUBCORES = 16
SC_GATHER_K = 32
GATHER_ROWS = SC_CORES * SC_SUBCORES * SC_GATHER_K
ROUTE_FINE0 = 32
VMEM_LIMIT = 56 * 1024 * 1024

NEG_INF = float("-inf")
LOG2E = math.log2(math.e)


def _cp(sem, vmem=VMEM_LIMIT):
    return pltpu.CompilerParams(dimension_semantics=sem, vmem_limit_bytes=vmem)


def _dot(a, b):
    return jnp.dot(a, b, preferred_element_type=F32)


def _dot_hi(a, b):
    return jnp.dot(a, b, preferred_element_type=F32, precision=HI)


def _dot_nt(a, b):
    return lax.dot_general(a, b, (((1,), (1,)), ((), ())), preferred_element_type=F32)


def _sigmoid(x):
    return 1.0 / (1.0 + jnp.exp(-x))


def _silu(x):
    return x * _sigmoid(x)


def _gelu_tanh(x):
    return 0.5 * x * (1.0 + jnp.tanh(math.sqrt(2.0 / math.pi) * (x + 0.044715 * (x * x * x))))


def _softplus(x):
    return jnp.maximum(x, 0.0) + jnp.log1p(jnp.exp(-jnp.abs(x)))


def _per_head_cols(v, base, n_heads, shape):
    lane = lax.broadcasted_iota(jnp.int32, shape, 1)
    out = jnp.broadcast_to(v[:, base + n_heads - 1:base + n_heads], shape)
    for h in range(n_heads - 2, -1, -1):
        out = jnp.where(lane < (h + 1) * HEAD_DIM, v[:, base + h:base + h + 1], out)
    return out


def _ada_kernel(c_ref, w_ref, b_ref, o_ref):
    c = c_ref[...]
    o_ref[0] = _dot_hi(_silu(c), w_ref[0]) + b_ref[0]


def _ada(cc, ada_w, ada_b):
    depth, d, n = ada_w.shape
    tn = 1536
    return pl.pallas_call(
        _ada_kernel,
        out_shape=jax.ShapeDtypeStruct((depth, SUBLANES, n), F32),
        grid=(depth, n // tn),
        in_specs=[pl.BlockSpec((SUBLANES, d), lambda l, j: (0, 0)),
                  pl.BlockSpec((1, d, tn), lambda l, j: (l, 0, j)),
                  pl.BlockSpec((1, 1, tn), lambda l, j: (l, 0, j))],
        out_specs=pl.BlockSpec((1, SUBLANES, tn), lambda l, j: (l, 0, j)),
        compiler_params=_cp(("parallel", "parallel")),
        name="ada_mod",
    )(cc, ada_w, ada_b.reshape(depth, 1, n))


_C_XBC = 0
_C_U = _C_XBC + SSD_XBC
_C_Z = _C_U + GROUP_W
_C_DT = _C_Z + GROUP_W
_C_GAQ = _C_DT + LANES
_C_WAQ = _C_GAQ + N_HEADS * LANES
_C_GAK = _C_WAQ + N_HEADS * LANES
_C_GAV = _C_GAK + LANES
_C_WAK = _C_GAV + LANES
_C_WAV = _C_WAK + LANES
_C_END = _C_WAV + LANES


def _expand_q_cols(wq):
    zero = jnp.zeros((wq.shape[0], HEAD_DIM), wq.dtype)
    parts = []
    for h in range(N_HEADS):
        head = wq[:, h * HEAD_DIM:(h + 1) * HEAD_DIM]
        parts += [head, zero] if h // 2 == 0 else [zero, head]
    return jnp.concatenate(parts, axis=1)


def _pack_w_in(w_in):
    cuts = np.cumsum([256, 256, 128, 128, 256, SSD_XBC, 2 * SSD_HEADS, 256, 128, 128])[:-1]
    u, gaq, gak, gav, z, xbc, dt, waq, wak, wav = jnp.split(w_in, [int(c) for c in cuts], axis=1)
    dt = jnp.pad(dt, ((0, 0), (0, LANES - dt.shape[1])))
    w = jnp.concatenate([xbc, u, z, dt, _expand_q_cols(gaq), _expand_q_cols(waq), gak, gav, wak, wav], axis=1)
    return w.astype(BF16)


def _rope(x, cos, sins):
    w = x.shape[1]
    if w > LANES:
        cos = jnp.concatenate([cos] * (w // LANES), axis=1)
        sins = jnp.concatenate([sins] * (w // LANES), axis=1)
    lane = lax.broadcasted_iota(jnp.int32, x.shape, 1)
    up = pltpu.roll(x, w - ROPE_FREQS, 1)
    dn = pltpu.roll(x, ROPE_FREQS, 1)
    partner = jnp.where((lane & ROPE_FREQS) == 0, up, dn)
    return x * cos + partner * sins


def _inproj_kernel(x_ref, mod_ref, g_ref, w_ref, cos_ref, sin_ref, qn_ref, kn_ref,
                   xbc_o, u_o, z_o, dt_o, gaq_o, gak_o, gav_o, waq_o, wak_o, wav_o):
    x = x_ref[...]
    ms = jnp.mean(x * x, axis=-1, keepdims=True)
    xn = x * lax.rsqrt(ms + EPS) * g_ref[...]
    h = xn * (1.0 + mod_ref[0, 1:2, :]) + mod_ref[0, 0:1, :]
    p = _dot(h.astype(BF16), w_ref[...])
    xbc_o[...] = p[:, _C_XBC:_C_U]
    u_o[...] = p[:, _C_U:_C_Z]
    z_o[...] = p[:, _C_Z:_C_DT]
    dt_o[...] = p[:, _C_DT:_C_GAQ]
    cos = cos_ref[...]
    sins = sin_ref[...]
    scale = LOG2E * HEAD_DIM ** -0.5
    q = p[:, _C_GAQ:_C_WAQ]
    qs = q * q
    inv = jnp.concatenate(
        [jnp.broadcast_to(lax.rsqrt(jnp.sum(qs[:, s * LANES:(s + 1) * LANES], axis=1, keepdims=True)
                                    * (1.0 / HEAD_DIM) + EPS), (q.shape[0], LANES)) for s in range(N_HEADS)], axis=1)
    gaq_o[...] = (_rope(q * inv * qn_ref[...], cos, sins) * scale).astype(BF16)
    waq_o[...] = (_rope(p[:, _C_WAQ:_C_GAK], cos, sins) * scale).astype(BF16)
    k = p[:, _C_GAK:_C_GAV]
    ks = k * k
    lane = lax.broadcasted_iota(jnp.int32, k.shape, 1)
    lo = lane < HEAD_DIM
    ms0 = jnp.sum(jnp.where(lo, ks, 0.0), axis=1, keepdims=True)
    ms1 = jnp.sum(jnp.where(lo, 0.0, ks), axis=1, keepdims=True)
    kinv = lax.rsqrt(jnp.where(lo, ms0, ms1) * (1.0 / HEAD_DIM) + EPS)
    gak_o[...] = _rope(k * kinv * kn_ref[...], cos, sins).astype(BF16)
    gav_o[...] = p[:, _C_GAV:_C_WAK].astype(BF16)
    wak_o[...] = _rope(p[:, _C_WAK:_C_WAV], cos, sins).astype(BF16)
    wav_o[...] = p[:, _C_WAV:_C_END].astype(BF16)


def _mod_row(i, nblk, nb):
    return jnp.where(i % nblk == 0, nb, i // nblk)


def _inproj(x, mod, norm_g, w_packed, cos_t, sin_t, qn_g, kn_g, nb, nblk):
    t, d = x.shape
    row = lambda i: (i, 0)
    fix = lambda i: (0, 0)
    outs = [(SSD_XBC, F32), (GROUP_W, F32), (GROUP_W, F32), (LANES, F32),
            (N_HEADS * LANES, BF16), (LANES, BF16), (LANES, BF16),
            (N_HEADS * LANES, BF16), (LANES, BF16), (LANES, BF16)]
    return pl.pallas_call(
        _inproj_kernel,
        out_shape=[jax.ShapeDtypeStruct((t, w), dt) for w, dt in outs],
        grid=(t // TM,),
        in_specs=[pl.BlockSpec((TM, d), row),
                  pl.BlockSpec((1, 6, d), lambda i: (_mod_row(i, nblk, nb), 0, 0)),
                  pl.BlockSpec((1, d), fix),
                  pl.BlockSpec((d, _C_END), fix),
                  pl.BlockSpec((TM, LANES), lambda i: (i % nblk, 0)),
                  pl.BlockSpec((TM, LANES), lambda i: (i % nblk, 0)),
                  pl.BlockSpec((1, N_HEADS * LANES), fix),
                  pl.BlockSpec((1, LANES), fix)],
        out_specs=[pl.BlockSpec((TM, w), row) for w, _ in outs],
        compiler_params=_cp(("parallel",)),
        name="in_proj",
    )(x, mod, norm_g.reshape(1, d), w_packed, cos_t, sin_t,
      jnp.tile(qn_g, 2 * N_HEADS).reshape(1, -1), jnp.tile(kn_g, 2).reshape(1, -1))


def _rope_tables(lc, l):
    n_rows = l // GRID_W
    rows = jnp.repeat(jnp.arange(n_rows), GRID_W)
    cols = jnp.tile(jnp.arange(GRID_W), n_rows)
    inv = jnp.power(ROPE_BASE, -jnp.arange(ROPE_FREQS, dtype=F32) / ROPE_FREQS)
    ang = jnp.stack([rows, cols], axis=-1).astype(F32)[..., None] * inv
    cos = jnp.cos(ang)
    sin = jnp.sin(ang)
    cos64 = jnp.stack([cos, cos], axis=2).reshape(l, HEAD_DIM)
    sin64 = jnp.stack([-sin, sin], axis=2).reshape(l, HEAD_DIM)
    cos64 = jnp.concatenate([jnp.ones((lc, HEAD_DIM), F32), cos64], axis=0)
    sin64 = jnp.concatenate([jnp.zeros((lc, HEAD_DIM), F32), sin64], axis=0)
    return jnp.tile(cos64, (1, 2)), jnp.tile(sin64, (1, 2))


def _merge_heads(o2, kvh):
    tq = o2.shape[0] // 2
    oa, ob = o2[:tq], o2[tq:]
    lane = lax.broadcasted_iota(jnp.int32, oa.shape, 1)
    if kvh == 0:
        return jnp.where(lane < HEAD_DIM, oa, pltpu.roll(ob, HEAD_DIM, 1))
    return jnp.where(lane < HEAD_DIM, pltpu.roll(oa, HEAD_DIM, 1), ob)


def _stack_q(q_ref, kvh):
    return jnp.concatenate([q_ref[:, (2 * kvh) * LANES:(2 * kvh + 1) * LANES],
                            q_ref[:, (2 * kvh + 1) * LANES:(2 * kvh + 2) * LANES]], axis=0)


def _ga_kernel(q_ref, k_ref, v_ref, o_ref, *, lc):
    j = pl.program_id(1)

    def attend(nkeys):
        k = k_ref[0:nkeys, :]
        v = v_ref[0:nkeys, :]
        scores = [_dot_nt(_stack_q(q_ref, kvh), k) for kvh in range(2)]
        lane = lax.broadcasted_iota(jnp.int32, v.shape, 1)
        outs = []
        for kvh in range(2):
            s = scores[kvh]
            p = jnp.exp2((s - jnp.max(s, axis=1, keepdims=True)).astype(BF16))
            own = (lane < HEAD_DIM) if kvh == 0 else (lane >= HEAD_DIM)
            o2 = _dot(p, jnp.where(own, v, jnp.ones_like(v)))
            outs.append(_merge_heads(o2 / pltpu.roll(o2, HEAD_DIM, 1), kvh))
        o_ref[...] = jnp.concatenate(outs, axis=1)

    @pl.when(j < lc // GA_TQ)
    def _():
        attend(lc)

    @pl.when(j >= lc // GA_TQ)
    def _():
        attend(k_ref.shape[0])


def _ga(q, k, v, nb, s_len, lc):
    t = q.shape[0]
    nq = s_len // GA_TQ
    return pl.pallas_call(
        functools.partial(_ga_kernel, lc=lc),
        out_shape=jax.ShapeDtypeStruct((t, GROUP_W), F32),
        grid=(nb, nq),
        in_specs=[pl.BlockSpec((GA_TQ, N_HEADS * LANES), lambda b, j: (b * nq + j, 0)),
                  pl.BlockSpec((s_len, LANES), lambda b, j: (b, 0)),
                  pl.BlockSpec((s_len, LANES), lambda b, j: (b, 0))],
        out_specs=pl.BlockSpec((GA_TQ, GROUP_W), lambda b, j: (b * nq + j, 0)),
        compiler_params=_cp(("parallel", "arbitrary")),
        name="global_attn",
    )(q, k, v)


def _wa_kernel(sink_ref, q_ref, k_ref, v_ref, o_ref, *, lc):
    j = pl.program_id(1)
    s_len = k_ref.shape[0]
    n = j - lc // TQ
    start = pl.multiple_of(jnp.clip(lc + (n - 1) * TQ, lc, s_len - 3 * TQ), TQ)
    kc = k_ref[0:lc, :]
    vc = v_ref[0:lc, :]
    kb = k_ref[pl.ds(start, 3 * TQ), :]
    vb = v_ref[pl.ds(start, 3 * TQ), :]
    qpos = n * TQ + lax.broadcasted_iota(jnp.int32, (TQ, 3 * TQ), 0)
    kpos = (start - lc) + lax.broadcasted_iota(jnp.int32, (TQ, 3 * TQ), 1)
    reach = jnp.where(n >= 0, WINDOW, -1)
    valid = jnp.abs(qpos - kpos) <= reach
    valid = jnp.concatenate([valid, valid], axis=0)
    row = lax.broadcasted_iota(jnp.int32, (2 * TQ, 1), 0)
    outs = []
    for kvh in range(2):
        q2 = _stack_q(q_ref, kvh)
        sc = _dot_nt(q2, kc)
        sb = jnp.where(valid, _dot_nt(q2, kb), NEG_INF)
        sink = jnp.where(row < TQ, sink_ref[2 * kvh], sink_ref[2 * kvh + 1]) * LOG2E
        m = jnp.maximum(jnp.maximum(jnp.max(sc, axis=1, keepdims=True), jnp.max(sb, axis=1, keepdims=True)), sink)
        pc = jnp.exp2(sc - m)
        pb = jnp.exp2(sb - m)
        denom = jnp.sum(pc, axis=1, keepdims=True) + jnp.sum(pb, axis=1, keepdims=True) + jnp.exp2(sink - m)
        o2 = (_dot(pc.astype(BF16), vc) + _dot(pb.astype(BF16), vb)) / denom
        outs.append(_merge_heads(o2, kvh))
    o_ref[...] = jnp.concatenate(outs, axis=1)


def _wa(sink, q, k, v, nb, s_len, lc):
    t = q.shape[0]
    nq = s_len // TQ
    return pl.pallas_call(
        functools.partial(_wa_kernel, lc=lc),
        out_shape=jax.ShapeDtypeStruct((t, GROUP_W), F32),
        grid=(nb, nq),
        in_specs=[pl.BlockSpec(memory_space=pltpu.SMEM),
                  pl.BlockSpec((TQ, N_HEADS * LANES), lambda b, j: (b * nq + j, 0)),
                  pl.BlockSpec((s_len, LANES), lambda b, j: (b, 0)),
                  pl.BlockSpec((s_len, LANES), lambda b, j: (b, 0))],
        out_specs=pl.BlockSpec((TQ, GROUP_W), lambda b, j: (b * nq + j, 0)),
        compiler_params=_cp(("parallel", "arbitrary")),
        name="window_attn",
    )(sink, q, k, v)


def _s5_chunk_index(t, rev, nc_ctx, nc_tot):
    if not rev:
        return t
    return jnp.where(t < nc_ctx, nc_ctx - 1 - t, nc_tot - 1 - (t - nc_ctx))


def _s5_kernel(u_ref, m_ref, p_ref, g_ref, ar_ref, ai_ref, dsk_ref, y_ref, s_scr, h_scr, *, nb, nc_ctx, nc_tot):
    uf = u_ref[0]
    u = uf.astype(BF16)
    for d in range(2):
        for k in range(2):
            s_scr[d, k] = _dot(u, p_ref[d, k, 0])
    ar = [jnp.broadcast_to(ar_ref[d, 0], (nb, LANES)) for d in range(2)]
    ai = [[jnp.broadcast_to(ai_ref[d, k, 0], (nb, LANES)) for k in range(2)] for d in range(2)]

    def body(t, carry):
        out = []
        for d in range(2):
            h, hs = carry[d]
            rows = pl.ds(_s5_chunk_index(t, d == 1, nc_ctx, nc_tot), nb, stride=nc_tot)
            h_scr[d, rows, :] = h
            out.append((ar[d] * h + ai[d][0] * hs + s_scr[d, 0, rows, :],
                        ar[d] * hs + ai[d][1] * h + s_scr[d, 1, rows, :]))
        return tuple(out)

    zero = jnp.zeros((nb, LANES), F32)
    lax.fori_loop(0, nc_tot, body, ((zero, zero), (zero, zero)), unroll=2)
    y = uf * dsk_ref[0]
    for d in range(2):
        y = y + _dot(u, m_ref[d, 0]) + _dot(h_scr[d].astype(BF16), g_ref[d, 0])
    y_ref[0] = y


S5_TB = TM // S5_Q
S5_GPS = LANES // S5_CH


def _s5_pack_kernel(lo_ref, hi_ref, o_ref):
    for s in range(S5_Q):
        rows = pl.ds(s, S5_TB, stride=S5_Q)
        halves = (lo_ref[rows, :], hi_ref[rows, :])
        dst = S5_CH * (s % S5_GPS)
        for g in range(S5_GROUPS):
            slab = halves[g // S5_GPS]
            src = S5_CH * (g % S5_GPS)
            moved = slab if src == dst else pltpu.roll(slab, (dst - src) % LANES, 1)
            o_ref[g, :, s * S5_CH:(s + 1) * S5_CH] = moved[:, dst:dst + S5_CH]


def _s5_unpack_kernel(y_ref, o_ref):
    lane_grp = lax.broadcasted_iota(jnp.int32, (S5_TB, LANES), 1) // S5_CH
    for s in range(S5_Q):
        src = S5_CH * (s % S5_GPS)
        for half in range(S5_GROUPS // S5_GPS):
            acc = None
            for gl in range(S5_GPS):
                slab = y_ref[half * S5_GPS + gl, :, (s // S5_GPS) * LANES:(s // S5_GPS + 1) * LANES]
                dst = S5_CH * gl
                moved = slab if src == dst else pltpu.roll(slab, (dst - src) % LANES, 1)
                acc = moved if acc is None else jnp.where(lane_grp == gl, moved, acc)
            o_ref[half, pl.ds(s, S5_TB, stride=S5_Q), :] = acc


def _s5_params(lam_re, lam_im, log_dt, b_re, b_im, c_re, c_im, d_skip):
    q = S5_Q
    dt = jnp.exp(log_dt)[..., None]
    lr, li = lam_re, lam_im
    mag = jnp.exp(lr * dt)
    a_re = mag * jnp.cos(li * dt)
    a_im = mag * jnp.sin(li * dt)
    den = lr * lr + li * li
    f_re = ((a_re - 1.0) * lr + a_im * li) / den
    f_im = (a_im * lr - (a_re - 1.0) * li) / den
    bb_re = f_re[..., None] * b_re - f_im[..., None] * b_im
    bb_im = f_re[..., None] * b_im + f_im[..., None] * b_re
    kk = jnp.arange(q + 1, dtype=F32)[:, None, None, None]
    pmag = jnp.exp(kk * (lr * dt))
    pw_re = pmag * jnp.cos(kk * (li * dt))
    pw_im = pmag * jnp.sin(kk * (li * dt))
    e_re = c_re[None] * pw_re[:, :, :, None, :] - c_im[None] * pw_im[:, :, :, None, :]
    e_im = c_re[None] * pw_im[:, :, :, None, :] + c_im[None] * pw_re[:, :, :, None, :]
    kern = (jnp.einsum("kdgop,dgpc->kdgoc", e_re, bb_re, precision=HI)
            - jnp.einsum("kdgop,dgpc->kdgoc", e_im, bb_im, precision=HI))
    kern_t = kern[:q].transpose(1, 2, 4, 0, 3)
    zeros = jnp.zeros_like(kern_t)
    bbt_re = bb_re.transpose(0, 1, 3, 2)[:, :, None]
    bbt_im = bb_im.transpose(0, 1, 3, 2)[:, :, None]
    ct_re = c_re.transpose(0, 1, 3, 2)[:, :, :, None, :]
    ct_im = c_im.transpose(0, 1, 3, 2)[:, :, :, None, :]
    ms, ps, gs = [], [], []
    for d in range(2):
        ext = (jnp.concatenate([zeros[d], kern_t[d]], axis=2) if d == 0
               else jnp.concatenate([kern_t[d, :, :, ::-1], zeros[d]], axis=2))
        lo = [(q - s) if d == 0 else (q - 1 - s) for s in range(q)]
        md = jnp.stack([ext[:, :, a:a + q, :] for a in lo], axis=1)
        ms.append(md.reshape(S5_GROUPS, S5_BLK, S5_BLK))
        pidx = (q - 1 - jnp.arange(q)) if d == 0 else jnp.arange(q)
        pr = pw_re[pidx, d].transpose(1, 0, 2)[:, :, None, :]
        pi = pw_im[pidx, d].transpose(1, 0, 2)[:, :, None, :]
        p_re = pr * bbt_re[d] - pi * bbt_im[d]
        p_im = pr * bbt_im[d] + pi * bbt_re[d]
        pd = jnp.stack([jnp.concatenate([p_re, p_im], axis=3), jnp.concatenate([p_im, p_re], axis=3)])
        ps.append(pd.reshape(2, S5_GROUPS, S5_BLK, 2 * S5_STATE))
        gidx = (jnp.arange(q) + 1) if d == 0 else (q - jnp.arange(q))
        gw_re = pw_re[gidx, d].transpose(1, 2, 0)[..., None]
        gw_im = pw_im[gidx, d].transpose(1, 2, 0)[..., None]
        g_re = ct_re[d] * gw_re - ct_im[d] * gw_im
        g_im = ct_re[d] * gw_im + ct_im[d] * gw_re
        gs.append(jnp.concatenate([g_re, -g_im], axis=1).reshape(S5_GROUPS, 2 * S5_STATE, S5_BLK))
    ar = jnp.concatenate([pw_re[q], pw_re[q]], axis=-1)[:, :, None, :]
    ai = jnp.stack([jnp.concatenate([-pw_im[q], pw_im[q]], axis=-1),
                    jnp.concatenate([pw_im[q], -pw_im[q]], axis=-1)], axis=1)[:, :, :, None, :]
    dsk = jnp.tile(d_skip.reshape(S5_GROUPS, 1, S5_CH), (1, 1, q))
    return (jnp.stack(ms).astype(BF16), jnp.stack(ps).astype(BF16), jnp.stack(gs).astype(BF16),
            ar.astype(F32), ai.astype(F32), dsk.astype(F32))


def _s5(u, params, nb, s_len, lc):
    m, p, g, ar, ai, dsk = params
    nc_tot = s_len // S5_Q
    nc_ctx = lc // S5_Q
    t = u.shape[0]
    r = nb * nc_tot
    grp = pl.BlockSpec((S5_GROUPS, S5_TB, S5_BLK), lambda i: (0, i, 0))
    ug = pl.pallas_call(
        _s5_pack_kernel,
        out_shape=jax.ShapeDtypeStruct((S5_GROUPS, r, S5_BLK), F32),
        grid=(t // TM,),
        in_specs=[pl.BlockSpec((TM, LANES), lambda i: (i, 0)), pl.BlockSpec((TM, LANES), lambda i: (i, 1))],
        out_specs=grp,
        compiler_params=_cp(("parallel",)), name="s5_pack",
    )(u, u)
    y = pl.pallas_call(
        functools.partial(_s5_kernel, nb=nb, nc_ctx=nc_ctx, nc_tot=nc_tot),
        out_shape=jax.ShapeDtypeStruct((S5_GROUPS, r, S5_BLK), F32),
        grid=(S5_GROUPS,),
        in_specs=[pl.BlockSpec((1, r, S5_BLK), lambda gi: (gi, 0, 0)),
                  pl.BlockSpec((2, 1, S5_BLK, S5_BLK), lambda gi: (0, gi, 0, 0)),
                  pl.BlockSpec((2, 2, 1, S5_BLK, 2 * S5_STATE), lambda gi: (0, 0, gi, 0, 0)),
                  pl.BlockSpec((2, 1, 2 * S5_STATE, S5_BLK), lambda gi: (0, gi, 0, 0)),
                  pl.BlockSpec((2, 1, 1, 2 * S5_STATE), lambda gi: (0, gi, 0, 0)),
                  pl.BlockSpec((2, 2, 1, 1, 2 * S5_STATE), lambda gi: (0, 0, gi, 0, 0)),
                  pl.BlockSpec((1, 1, S5_BLK), lambda gi: (gi, 0, 0))],
        out_specs=pl.BlockSpec((1, r, S5_BLK), lambda gi: (gi, 0, 0)),
        scratch_shapes=[pltpu.VMEM((2, 2, r, 2 * S5_STATE), F32), pltpu.VMEM((2, r, 2 * S5_STATE), F32)],
        compiler_params=_cp(("parallel",)),
        name="s5_scan",
    )(ug, m, p, g, ar, ai, dsk)
    return pl.pallas_call(
        _s5_unpack_kernel,
        out_shape=jax.ShapeDtypeStruct((GROUP_W // LANES, t, LANES), F32),
        grid=(t // TM,), in_specs=[grp],
        out_specs=pl.BlockSpec((GROUP_W // LANES, TM, LANES), lambda i: (0, i, 0)),
        compiler_params=_cp(("parallel",)), name="s5_unpack",
    )(y)


def _conv_kernel(x_ref, prev_ref, next_ref, w_ref, b_ref, o_ref, *, nblk):
    i = pl.program_id(0) % nblk
    x = x_ref[...]
    rows = x.shape[0]
    ridx = lax.broadcasted_iota(jnp.int32, x.shape, 0)
    prev_row = jnp.where(i <= 1, 0.0, prev_ref[SUBLANES - 1:SUBLANES, :])
    next_row = jnp.where(jnp.logical_or(i == 0, i == nblk - 1), 0.0, next_ref[0:1, :])
    xm = jnp.where(ridx == 0, prev_row, pltpu.roll(x, 1, 0))
    xp = jnp.where(ridx == rows - 1, next_row, pltpu.roll(x, rows - 1, 0))
    y = xm * w_ref[0:1, :] + x * w_ref[1:2, :] + xp * w_ref[2:3, :] + b_ref[...]
    o_ref[...] = _silu(y)


def _conv(xbc, w, b, nblk):
    t, c = xbc.shape
    per = TM // SUBLANES
    last = t // SUBLANES - 1
    return pl.pallas_call(
        functools.partial(_conv_kernel, nblk=nblk),
        out_shape=jax.ShapeDtypeStruct((t, c), F32),
        grid=(t // TM,),
        in_specs=[pl.BlockSpec((TM, c), lambda i: (i, 0)),
                  pl.BlockSpec((SUBLANES, c), lambda i: (jnp.maximum(i * per - 1, 0), 0)),
                  pl.BlockSpec((SUBLANES, c), lambda i: (jnp.minimum((i + 1) * per, last), 0)),
                  pl.BlockSpec((3, c), lambda i: (0, 0)),
                  pl.BlockSpec((1, c), lambda i: (0, 0))],
        out_specs=pl.BlockSpec((TM, c), lambda i: (i, 0)),
        compiler_params=_cp(("parallel",)),
        name="ssd_conv",
    )(xbc, xbc, xbc, w, b.reshape(1, c))


_X_B = GROUP_W
_X_C = GROUP_W + SSD_NGROUPS * SSD_STATE


def _ssd_kernel(xc_ref, dt_ref, dtt_ref, bias_ref, a_ref, biast_ref, at_ref, dsk_ref, y_ref, st_ref, *, rev):
    c = pl.program_id(1)

    @pl.when(c == 0)
    def _():
        st_ref[...] = jnp.zeros_like(st_ref)

    base = SSD_HEADS if rev else 0
    xc = xc_ref[...]
    x = xc[:, 0:GROUP_W]
    dt = _softplus(dt_ref[...] + bias_ref[...])
    a = dt * a_ref[...]
    dtt = _softplus(dtt_ref[0] + biast_ref[...])
    at = dtt * at_ref[...]
    ri = lax.broadcasted_iota(jnp.int32, (TQ, TQ), 0)
    ci = lax.broadcasted_iota(jnp.int32, (TQ, TQ), 1)
    causal = (ci >= ri) if rev else (ri >= ci)
    tri = jnp.where(causal, 1.0, 0.0)
    cum_c = _dot_hi(tri, a)
    cum_r = _dot_nt_hi(at, tri)
    edge = 0 if rev else TQ - 1
    tot = cum_c[edge:edge + 1, :]

    shape = (TQ, GROUP_W)
    xdt = x * _per_head_cols(dt, base, SSD_HEADS, shape)
    lane = lax.broadcasted_iota(jnp.int32, shape, 1)
    y = jnp.zeros(shape, F32)
    bmat = [xc[:, _X_B + g * SSD_STATE:_X_B + (g + 1) * SSD_STATE].astype(BF16) for g in range(SSD_NGROUPS)]
    cmat = [xc[:, _X_C + g * SSD_STATE:_X_C + (g + 1) * SSD_STATE].astype(BF16) for g in range(SSD_NGROUPS)]
    cb = [_dot_nt(cmat[g], bmat[g]) for g in range(SSD_NGROUPS)]
    for h in range(SSD_HEADS):
        col = base + h
        seg = jnp.where(causal, cum_c[:, col:col + 1] - cum_r[col:col + 1, :], NEG_INF)
        scores = cb[h // 2] * jnp.exp(seg)
        xh = jnp.where((lane >= h * HEAD_DIM) & (lane < (h + 1) * HEAD_DIM), xdt, 0.0)
        y = y + _dot(scores.astype(BF16), xh.astype(BF16))
    st = st_ref[...]
    yo = jnp.concatenate(
        [_dot_nt(cmat[g], st[g * SSD_STATE:(g + 1) * SSD_STATE].astype(BF16)) for g in range(SSD_NGROUPS)], axis=1)
    y = y + yo * _per_head_cols(jnp.exp(cum_c), base, SSD_HEADS, shape)
    if not rev:
        y = y + x * dsk_ref[...]
    y_ref[...] = y
    xd = xdt * _per_head_cols(jnp.exp(tot - cum_c), base, SSD_HEADS, shape)
    xdt_t = xd.T.astype(BF16)
    decay = jnp.exp(tot)
    for g in range(SSD_NGROUPS):
        new = _dot(xdt_t[g * SSD_STATE:(g + 1) * SSD_STATE], bmat[g])
        for hh in range(2):
            h = 2 * g + hh
            r0 = h * HEAD_DIM
            st_ref[r0:r0 + HEAD_DIM, :] = (decay[:, base + h:base + h + 1] * st[r0:r0 + HEAD_DIM]
                                           + new[hh * HEAD_DIM:(hh + 1) * HEAD_DIM])


def _dot_nt_hi(a, b):
    return lax.dot_general(a, b, (((1,), (1,)), ((), ())), preferred_element_type=F32, precision=HI)


def _ssd_chunk(c, rev, nc_ctx, nc_tot):
    if not rev:
        return c
    return jnp.where(c < nc_ctx, nc_ctx - 1 - c, nc_tot - 1 - (c - nc_ctx))


def _ssd_dir(xc, dt, dtt, bias, a, biast, at, dsk, rev, nb, s_len, lc):
    t = xc.shape[0]
    nc_tot = s_len // TQ
    nc_ctx = lc // TQ
    cidx = lambda c: _ssd_chunk(c, rev, nc_ctx, nc_tot)
    fix = lambda b, c: (0, 0)
    return pl.pallas_call(
        functools.partial(_ssd_kernel, rev=rev),
        out_shape=jax.ShapeDtypeStruct((t, GROUP_W), F32),
        grid=(nb, nc_tot),
        in_specs=[pl.BlockSpec((TQ, SSD_XBC), lambda b, c: (b * nc_tot + cidx(c), 0)),
                  pl.BlockSpec((TQ, LANES), lambda b, c: (b * nc_tot + cidx(c), 0)),
                  pl.BlockSpec((1, SUBLANES, TQ), lambda b, c: (b, 0, cidx(c))),
                  pl.BlockSpec((1, LANES), fix),
                  pl.BlockSpec((1, LANES), fix),
                  pl.BlockSpec((SUBLANES, TQ), fix),
                  pl.BlockSpec((SUBLANES, TQ), fix),
                  pl.BlockSpec((1, GROUP_W), fix)],
        out_specs=pl.BlockSpec((TQ, GROUP_W), lambda b, c: (b * nc_tot + cidx(c), 0)),
        scratch_shapes=[pltpu.VMEM((SSD_HEADS * HEAD_DIM, SSD_STATE), F32)],
        compiler_params=_cp(("parallel", "arbitrary")),
        name="ssd_rev" if rev else "ssd_fwd",
    )(xc, dt, dtt, bias, a, biast, at, dsk)


def _ssd(xbc, dt, conv_w, conv_b, dt_bias, a_log, d_skip, nb, s_len, lc):
    nblk = s_len // TM
    xc = _conv(xbc, conv_w, conv_b, nblk)
    nd = 2 * SSD_HEADS
    dtt = dt[:, :nd].reshape(nb, s_len, nd).transpose(0, 2, 1)
    bias = jnp.pad(dt_bias.reshape(1, nd), ((0, 0), (0, LANES - nd)))
    a = jnp.pad(-jnp.exp(a_log).reshape(1, nd), ((0, 0), (0, LANES - nd)))
    biast = jnp.broadcast_to(dt_bias.reshape(nd, 1), (nd, TQ))
    at = jnp.broadcast_to(-jnp.exp(a_log).reshape(nd, 1), (nd, TQ))
    dsk = jnp.repeat(d_skip, HEAD_DIM).reshape(1, GROUP_W)
    args = (xc, dt, dtt, bias, a, biast, at, dsk)
    return _ssd_dir(*args, False, nb, s_len, lc), _ssd_dir(*args, True, nb, s_len, lc)


def _outproj_kernel(x_ref, ys5_ref, oga_ref, y0_ref, y1_ref, z_ref, owa_ref, mod_ref, gluw_ref, glub_ref,
                    ng_ref, wout_ref, n2_ref, wr_ref, br_ref, xn_o, h2_o, route_o):
    gl = _gelu_tanh(jnp.concatenate([ys5_ref[0], ys5_ref[1]], axis=1))
    a = gl * _sigmoid(_dot(gl.astype(BF16), gluw_ref[...]) + glub_ref[...])
    m = (y0_ref[...] + y1_ref[...]) * _silu(z_ref[...])
    m = m * lax.rsqrt(jnp.mean(m * m, axis=-1, keepdims=True) + EPS) * ng_ref[...]
    w = wout_ref
    mix = (_dot(a.astype(BF16), w[0:GROUP_W, :]) + _dot(oga_ref[...].astype(BF16), w[GROUP_W:2 * GROUP_W, :])
           + _dot(m.astype(BF16), w[2 * GROUP_W:3 * GROUP_W, :]) + _dot(owa_ref[...].astype(BF16), w[3 * GROUP_W:, :]))
    xn = x_ref[...] + mod_ref[0, 2:3, :] * mix
    xn_o[...] = xn
    h2 = xn * lax.rsqrt(jnp.mean(xn * xn, axis=-1, keepdims=True) + EPS) * n2_ref[...]
    h2 = h2 * (1.0 + mod_ref[0, 4:5, :]) + mod_ref[0, 3:4, :]
    h2_o[...] = h2
    logits = _dot_hi(h2, wr_ref[...]) + br_ref[...]
    lane = lax.broadcasted_iota(jnp.int32, logits.shape, 1)
    big = 4 * LANES
    lcoarse = jnp.where(lane < MOE_GROUPS, logits, NEG_INF)
    mx = jnp.max(lcoarse, axis=1, keepdims=True)
    den = jnp.sum(jnp.exp(lcoarse - mx), axis=1, keepdims=True)
    grp = jnp.min(jnp.where(lcoarse == mx, lane, big), axis=1, keepdims=True)
    pg = 1.0 / den
    lo = ROUTE_FINE0 + grp * MOE_PER_GROUP
    lf = jnp.where(lane >= lo, jnp.where(lane < lo + MOE_PER_GROUP, logits, NEG_INF), NEG_INF)
    v1 = jnp.max(lf, axis=1, keepdims=True)
    i1 = jnp.min(jnp.where(lf == v1, lane, big), axis=1, keepdims=True)
    lf2 = jnp.where(lane == i1, NEG_INF, lf)
    v2 = jnp.max(lf2, axis=1, keepdims=True)
    i2 = jnp.min(jnp.where(lf2 == v2, lane, big), axis=1, keepdims=True)
    e2 = jnp.exp(v2 - v1)
    w1 = pg / (1.0 + e2)
    w2 = w1 * e2
    route = jnp.where(lane == 0, (i1 - ROUTE_FINE0).astype(F32),
                      jnp.where(lane == 1, (i2 - ROUTE_FINE0).astype(F32),
                                jnp.where(lane == 2, w1, jnp.where(lane == 3, w2, 0.0))))
    route_o[...] = route


def _outproj(x, ys5, oga, y0, y1, z, owa, mod, glu_w, glu_b, ssd_norm_g, w_out, norm2_g, wr, br, nb, nblk):
    t, d = x.shape
    row = lambda i: (i, 0)
    fix = lambda i: (0, 0)
    gw = pl.BlockSpec((TM, GROUP_W), row)
    return pl.pallas_call(
        _outproj_kernel,
        out_shape=[jax.ShapeDtypeStruct((t, d), F32), jax.ShapeDtypeStruct((t, d), F32),
                   jax.ShapeDtypeStruct((t, LANES), F32)],
        grid=(t // TM,),
        in_specs=[pl.BlockSpec((TM, d), row), pl.BlockSpec((GROUP_W // LANES, TM, LANES), lambda i: (0, i, 0)),
                  gw, gw, gw, gw, gw,
                  pl.BlockSpec((1, 6, d), lambda i: (_mod_row(i, nblk, nb), 0, 0)),
                  pl.BlockSpec((GROUP_W, GROUP_W), fix),
                  pl.BlockSpec((1, GROUP_W), fix),
                  pl.BlockSpec((1, GROUP_W), fix),
                  pl.BlockSpec((d, d), fix),
                  pl.BlockSpec((1, d), fix),
                  pl.BlockSpec((d, LANES), fix),
                  pl.BlockSpec((1, LANES), fix)],
        out_specs=[pl.BlockSpec((TM, d), row), pl.BlockSpec((TM, d), row), pl.BlockSpec((TM, LANES), row)],
        compiler_params=_cp(("parallel",)),
        name="out_proj_router",
    )(x, ys5, oga, y0, y1, z, owa, mod, glu_w.astype(BF16), glu_b.reshape(1, -1), ssd_norm_g.reshape(1, -1),
      w_out.astype(BF16), norm2_g.reshape(1, -1), wr, br)


def _pack_router(coarse_w, coarse_b, fine_w, fine_b):
    d = coarse_w.shape[0]
    wr = jnp.zeros((d, LANES), F32)
    wr = wr.at[:, 0:MOE_GROUPS].set(coarse_w).at[:, ROUTE_FINE0:ROUTE_FINE0 + N_EXPERTS].set(fine_w)
    br = jnp.zeros((1, LANES), F32)
    br = br.at[0, 0:MOE_GROUPS].set(coarse_b).at[0, ROUTE_FINE0:ROUTE_FINE0 + N_EXPERTS].set(fine_b)
    return wr, br


def _gather_rows(src, idx):
    m = idx.shape[0]
    d = src.shape[1]
    workers = SC_CORES * SC_SUBCORES
    nch = m // (workers * SC_GATHER_K)
    assert nch * workers * SC_GATHER_K == m
    mesh = plsc.VectorSubcoreMesh(core_axis_name="c", subcore_axis_name="s")

    @functools.partial(
        pl.kernel, mesh=mesh,
        out_type=jax.ShapeDtypeStruct((m, d), src.dtype),
        scratch_types=[pltpu.VMEM((nch, SC_GATHER_K), jnp.int32),
                       pltpu.VMEM((SC_GATHER_K, d), src.dtype),
                       pltpu.SemaphoreType.DMA],
    )
    def gather(src_hbm, idx_hbm, out_hbm, idx_v, rows_v, sem):
        wid = lax.axis_index("s") * SC_CORES + lax.axis_index("c")
        pltpu.sync_copy(idx_hbm.at[wid], idx_v)

        @pl.loop(0, nch)
        def _(j):
            off = pl.multiple_of((wid * nch + j) * SC_GATHER_K, SC_GATHER_K)
            pltpu.async_copy(src_hbm.at[idx_v.at[j]], rows_v, sem).wait()
            pltpu.sync_copy(rows_v, out_hbm.at[pl.ds(off, SC_GATHER_K)])

    return gather(src, idx.reshape(workers, nch, SC_GATHER_K))


def _scatter_rows(src, dst0, dst1, nrows):
    t, d = src.shape
    workers = SC_CORES * SC_SUBCORES
    nch = t // (workers * SC_GATHER_K)
    assert nch * workers * SC_GATHER_K == t
    mesh = plsc.VectorSubcoreMesh(core_axis_name="c", subcore_axis_name="s")

    @functools.partial(
        pl.kernel, mesh=mesh,
        out_type=jax.ShapeDtypeStruct((nrows, d), src.dtype),
        scratch_types=[pltpu.VMEM((nch, SC_GATHER_K), jnp.int32),
                       pltpu.VMEM((nch, SC_GATHER_K), jnp.int32),
                       pltpu.VMEM((SC_GATHER_K, d), src.dtype)],
    )
    def scatter(src_hbm, d0_hbm, d1_hbm, out_hbm, i0_v, i1_v, rows_v):
        wid = lax.axis_index("s") * SC_CORES + lax.axis_index("c")
        pltpu.sync_copy(d0_hbm.at[wid], i0_v)
        pltpu.sync_copy(d1_hbm.at[wid], i1_v)

        @pl.loop(0, nch)
        def _(j):
            off = pl.multiple_of((wid * nch + j) * SC_GATHER_K, SC_GATHER_K)
            pltpu.sync_copy(src_hbm.at[pl.ds(off, SC_GATHER_K)], rows_v)
            pltpu.sync_copy(rows_v, out_hbm.at[i0_v.at[j]])
            pltpu.sync_copy(rows_v, out_hbm.at[i1_v.at[j]])

    return scatter(src, dst0.reshape(workers, nch, SC_GATHER_K), dst1.reshape(workers, nch, SC_GATHER_K))


def _expert_kernel(be_ref, nused_ref, nvalid_ref, x_ref, wg_ref, wu_ref, wd_ref, o_ref, wg_s, wu_s, wd_s):
    i = pl.program_id(0)
    new_expert = jnp.logical_or(i == 0, be_ref[i] != be_ref[jnp.maximum(i - 1, 0)])

    @pl.when(jnp.logical_and(i < nused_ref[0], new_expert))
    def _():
        wg_s[...] = wg_ref[0, 0].astype(BF16)
        wu_s[...] = wu_ref[0, 0].astype(BF16)
        wd_s[...] = wd_ref[0, 0].astype(BF16)

    @pl.when(i < nused_ref[0])
    def _():
        row = lax.broadcasted_iota(jnp.int32, x_ref.shape, 0)
        x = jnp.where(row < nvalid_ref[i], x_ref[...], 0.0).astype(BF16)
        hid = _silu(_dot(x, wg_s[...])) * _dot(x, wu_s[...])
        o_ref[...] = _dot(hid.astype(BF16), wd_s[...])

    @pl.when(i >= nused_ref[0])
    def _():
        o_ref[...] = jnp.zeros_like(o_ref)


def _experts(xs, blk_e, n_used, n_valid, wg, wu, wd, layer):
    rows, d = xs.shape
    nblocks = rows // MOE_TM
    de = wg.shape[3]
    wsel = lambda i, be, nu, nv: (layer, be[i], 0, 0)
    grid_spec = pltpu.PrefetchScalarGridSpec(
        num_scalar_prefetch=3,
        grid=(nblocks,),
        in_specs=[pl.BlockSpec((MOE_TM, d), lambda i, be, nu, nv: (i, 0)),
                  pl.BlockSpec((1, 1, d, de), wsel),
                  pl.BlockSpec((1, 1, d, de), wsel),
                  pl.BlockSpec((1, 1, de, d), wsel)],
        out_specs=pl.BlockSpec((MOE_TM, d), lambda i, be, nu, nv: (i, 0)),
        scratch_shapes=[pltpu.VMEM((d, de), BF16), pltpu.VMEM((d, de), BF16), pltpu.VMEM((de, d), BF16)],
    )
    return pl.pallas_call(
        _expert_kernel,
        out_shape=jax.ShapeDtypeStruct((rows, d), F32),
        grid_spec=grid_spec,
        compiler_params=_cp(("arbitrary",)),
        name="moe_experts",
    )(blk_e, n_used, n_valid, xs, wg, wu, wd)


def _combine_kernel(x_ref, r0_ref, r1_ref, route_ref, mod_ref, fg_ref, o_ref, *, final):
    route = route_ref[...]
    f = route[:, 2:3] * r0_ref[...] + route[:, 3:4] * r1_ref[...]
    y = x_ref[...] + mod_ref[0, 5:6, :] * f
    if final:
        y = y * lax.rsqrt(jnp.mean(y * y, axis=-1, keepdims=True) + EPS) * fg_ref[...]
    o_ref[...] = y


def _combine(xn, rows2, route, mod, final_g, nb, nblk, final):
    t, d = xn.shape
    if final:
        nlat = nblk - 1
        grid = (nb * nlat,)
        src = lambda i: ((i // nlat) * nblk + 1 + i % nlat, 0)
        modi = lambda i: (i // nlat, 0, 0)
        out_rows = nb * nlat * TM
    else:
        grid = (t // TM,)
        src = lambda i: (i, 0)
        modi = lambda i: (_mod_row(i, nblk, nb), 0, 0)
        out_rows = t
    return pl.pallas_call(
        functools.partial(_combine_kernel, final=final),
        out_shape=jax.ShapeDtypeStruct((out_rows, d), F32),
        grid=grid,
        in_specs=[pl.BlockSpec((TM, d), src),
                  pl.BlockSpec((TM, d), src),
                  pl.BlockSpec((TM, d), lambda i: (src(i)[0] + t // TM, 0)),
                  pl.BlockSpec((TM, LANES), src),
                  pl.BlockSpec((1, 6, d), modi),
                  pl.BlockSpec((1, d), lambda i: (0, 0))],
        out_specs=pl.BlockSpec((TM, d), lambda i: (i, 0)),
        compiler_params=_cp(("parallel",)),
        name="moe_combine_final" if final else "moe_combine",
    )(xn, rows2, rows2, route, mod, final_g.reshape(1, d))


def _moe(h2, route, wg, wu, wd, layer):
    t, d = h2.shape
    n_slots = 2 * t
    experts = jnp.arange(N_EXPERTS, dtype=F32)[None, :]
    oh0 = (route[:, 0:1] == experts).astype(F32)
    oh1 = (route[:, 1:2] == experts).astype(F32)
    both = (oh0 + oh1).reshape(t // LANES, LANES, N_EXPERTS)
    tri = jnp.tril(jnp.ones((LANES, LANES), F32))
    intra = jnp.einsum("ij,bjk->bik", tri, both)
    blk_tot = intra[:, -1, :]
    blk_cum = jnp.cumsum(blk_tot, axis=0)
    earlier = (intra - both + (blk_cum - blk_tot)[:, None, :]).reshape(t, N_EXPERTS)
    counts = blk_cum[-1].astype(jnp.int32)
    pcounts = (counts + MOE_TM - 1) // MOE_TM * MOE_TM
    pends = jnp.cumsum(pcounts)
    pstarts = pends - pcounts
    base = pstarts.astype(F32)[None, :] + earlier
    dest0 = jnp.sum(oh0 * base, axis=1).astype(jnp.int32)
    dest1 = jnp.sum(oh1 * base, axis=1).astype(jnp.int32)
    nblocks = -(-n_slots // MOE_TM) + N_EXPERTS
    nrows = -(-nblocks * MOE_TM // GATHER_ROWS) * GATHER_ROWS
    nblocks = nrows // MOE_TM
    blk_start = jnp.arange(nblocks, dtype=jnp.int32) * MOE_TM
    blk_e = jnp.minimum(jnp.sum((pends[None, :] <= blk_start[:, None]).astype(jnp.int32), axis=1), N_EXPERTS - 1)
    n_used = (pends[-1] // MOE_TM).astype(jnp.int32).reshape(1)
    n_valid = jnp.clip((pstarts + counts)[blk_e] - blk_start, 0, MOE_TM).astype(jnp.int32)
    xs = _scatter_rows(h2, dest0, dest1, nrows)
    ys = _experts(xs, blk_e, n_used, n_valid, wg, wu, wd, layer)
    return _gather_rows(ys, jnp.concatenate([dest0, dest1]))


def kernel(x, c, ctx, c_ctx, ada_w, ada_b, norm1_g, norm2_g, w_in, w_out, s5_lam_re, s5_lam_im, s5_log_dt, s5_b_re, s5_b_im, s5_c_re, s5_c_im, s5_d, s5_glu_w, s5_glu_b, ga_qn_g, ga_kn_g, ssd_conv_w, ssd_conv_b, ssd_dt_bias, ssd_a_log, ssd_d, ssd_norm_g, wa_sink, moe_coarse_w, moe_coarse_b, moe_fine_w, moe_fine_b, moe_w_gate, moe_w_up, moe_w_down, final_g):
    nb, l, d = x.shape
    lc = ctx.shape[1]
    depth = ada_w.shape[0]
    assert lc == TM and l % TM == 0 and nb <= SUBLANES - 1 and d == D_MODEL
    s_len = lc + l
    nblk = s_len // TM
    t = nb * s_len

    xm = jnp.concatenate([ctx, x], axis=1).reshape(t, d)
    cc = jnp.zeros((SUBLANES, d), F32).at[:nb].set(c).at[nb].set(c_ctx)
    mods = _ada(cc, ada_w, ada_b).reshape(depth, SUBLANES, 6, d)
    cos_t, sin_t = _rope_tables(lc, l)

    out = None
    for i in range(depth):
        mod = mods[i]
        (xbc, u, z, dt, gaq, gak, gav, waq, wak, wav) = _inproj(
            xm, mod, norm1_g[i], _pack_w_in(w_in[i]), cos_t, sin_t, ga_qn_g[i], ga_kn_g[i], nb, nblk)
        ys5 = _s5(u, _s5_params(s5_lam_re[i], s5_lam_im[i], s5_log_dt[i], s5_b_re[i], s5_b_im[i],
                                s5_c_re[i], s5_c_im[i], s5_d[i]), nb, s_len, lc)
        oga = _ga(gaq, gak, gav, nb, s_len, lc)
        y0, y1 = _ssd(xbc, dt, ssd_conv_w[i], ssd_conv_b[i], ssd_dt_bias[i], ssd_a_log[i], ssd_d[i], nb, s_len, lc)
        owa = _wa(wa_sink[i], waq, wak, wav, nb, s_len, lc)
        wr, br = _pack_router(moe_coarse_w[i], moe_coarse_b[i], moe_fine_w[i], moe_fine_b[i])
        xn, h2, route = _outproj(xm, ys5, oga, y0, y1, z, owa, mod, s5_glu_w[i], s5_glu_b[i], ssd_norm_g[i],
                                 w_out[i], norm2_g[i], wr, br, nb, nblk)
        rows2 = _moe(h2, route, moe_w_gate, moe_w_up, moe_w_down, i)
        final = i == depth - 1
        xm = _combine(xn, rows2, route, mod, final_g, nb, nblk, final)
        if final:
            out = xm.reshape(nb, l, d)
    return out
```

```python
import functools
import math

import jax
import jax.numpy as jnp
import numpy as np
from jax import lax
from jax.experimental import pallas as pl
from jax.experimental.pallas import tpu as pltpu
from jax.experimental.pallas import tpu_sc as plsc

F32 = jnp.float32
BF16 = jnp.bfloat16
HI = lax.Precision.HIGHEST

D_MODEL = 1024
GRID_W = 64
GROUP_W = 256
HEAD_DIM = 64
ROPE_FREQS = HEAD_DIM // 4
ROPE_BASE = 10000.0
EPS = 1e-6
S5_CH = 16
S5_GROUPS = GROUP_W // S5_CH
S5_STATE = 64
N_HEADS = 4
SSD_HEADS = 4
SSD_NGROUPS = 2
SSD_STATE = 128
SSD_XBC = GROUP_W + 2 * SSD_NGROUPS * SSD_STATE
WINDOW = 128
MOE_GROUPS = 4
MOE_PER_GROUP = 8
N_EXPERTS = 32
D_EXPERT = D_MODEL // 2

LANES = 128
SUBLANES = 8
TM = 256
TQ = 128
GA_TQ = 128
S5_Q = 32
S5_BLK = S5_Q * S5_CH
MOE_TM = 256
SC_CORES = 2
SC_SUBCORES = 16
SC_GATHER_K = 32
GATHER_ROWS = SC_CORES * SC_SUBCORES * SC_GATHER_K
ROUTE_FINE0 = 32
VMEM_LIMIT = 56 * 1024 * 1024

NEG_INF = float("-inf")
LOG2E = math.log2(math.e)


def _cp(sem, vmem=VMEM_LIMIT):
    return pltpu.CompilerParams(dimension_semantics=sem, vmem_limit_bytes=vmem)


def _dot(a, b):
    return jnp.dot(a, b, preferred_element_type=F32)


def _dot_hi(a, b):
    return jnp.dot(a, b, preferred_element_type=F32, precision=HI)


def _dot_nt(a, b):
    return lax.dot_general(a, b, (((1,), (1,)), ((), ())), preferred_element_type=F32)


def _sigmoid(x):
    return 1.0 / (1.0 + jnp.exp(-x))


def _silu(x):
    return x * _sigmoid(x)


def _gelu_tanh(x):
    return 0.5 * x * (1.0 + jnp.tanh(math.sqrt(2.0 / math.pi) * (x + 0.044715 * (x * x * x))))


def _softplus(x):
    return jnp.maximum(x, 0.0) + jnp.log1p(jnp.exp(-jnp.abs(x)))


_HI16 = 0xFFFF0000


def _pack_bf16_pair(x):
    n = x.shape[1] // 2
    bits = pltpu.bitcast(x.astype(BF16).astype(F32), jnp.uint32)
    return (bits[:, n:] & jnp.uint32(_HI16)) | (bits[:, :n] >> 16)


def _unpack_bf16_pair(w):
    return pltpu.bitcast(w << 16, F32), pltpu.bitcast(w & jnp.uint32(_HI16), F32)


def _per_head_cols(v, base, n_heads, shape):
    lane = lax.broadcasted_iota(jnp.int32, shape, 1)
    out = jnp.broadcast_to(v[:, base + n_heads - 1:base + n_heads], shape)
    for h in range(n_heads - 2, -1, -1):
        out = jnp.where(lane < (h + 1) * HEAD_DIM, v[:, base + h:base + h + 1], out)
    return out


def _ada_kernel(c_ref, w_ref, b_ref, o_ref):
    c = c_ref[...]
    o_ref[0] = _dot_hi(_silu(c), w_ref[0]) + b_ref[0]


def _ada(cc, ada_w, ada_b):
    depth, d, n = ada_w.shape
    tn = 1536
    return pl.pallas_call(
        _ada_kernel,
        out_shape=jax.ShapeDtypeStruct((depth, SUBLANES, n), F32),
        grid=(depth, n // tn),
        in_specs=[pl.BlockSpec((SUBLANES, d), lambda l, j: (0, 0)),
                  pl.BlockSpec((1, d, tn), lambda l, j: (l, 0, j)),
                  pl.BlockSpec((1, 1, tn), lambda l, j: (l, 0, j))],
        out_specs=pl.BlockSpec((1, SUBLANES, tn), lambda l, j: (l, 0, j)),
        compiler_params=_cp(("parallel", "parallel")),
        name="ada_mod",
    )(cc, ada_w, ada_b.reshape(depth, 1, n))


_C_XBC = 0
_C_U = _C_XBC + SSD_XBC
_C_Z = _C_U + GROUP_W
_C_DT = _C_Z + GROUP_W
_C_GAQ = _C_DT + LANES
_C_WAQ = _C_GAQ + N_HEADS * LANES
_C_GAK = _C_WAQ + N_HEADS * LANES
_C_GAV = _C_GAK + LANES
_C_WAK = _C_GAV + LANES
_C_WAV = _C_WAK + LANES
_C_END = _C_WAV + LANES


def _expand_q_cols(wq):
    zero = jnp.zeros((wq.shape[0], HEAD_DIM), wq.dtype)
    parts = []
    for h in range(N_HEADS):
        head = wq[:, h * HEAD_DIM:(h + 1) * HEAD_DIM]
        parts += [head, zero] if h // 2 == 0 else [zero, head]
    return jnp.concatenate(parts, axis=1)


def _pack_w_in(w_in):
    cuts = np.cumsum([256, 256, 128, 128, 256, SSD_XBC, 2 * SSD_HEADS, 256, 128, 128])[:-1]
    u, gaq, gak, gav, z, xbc, dt, waq, wak, wav = jnp.split(w_in, [int(c) for c in cuts], axis=1)
    dt = jnp.pad(dt, ((0, 0), (0, LANES - dt.shape[1])))
    w = jnp.concatenate([xbc, u, z, dt, _expand_q_cols(gaq), _expand_q_cols(waq), gak, gav, wak, wav], axis=1)
    return w.astype(BF16)


def _rope(x, cos, sins):
    w = x.shape[1]
    if w > LANES:
        cos = jnp.concatenate([cos] * (w // LANES), axis=1)
        sins = jnp.concatenate([sins] * (w // LANES), axis=1)
    lane = lax.broadcasted_iota(jnp.int32, x.shape, 1)
    up = pltpu.roll(x, w - ROPE_FREQS, 1)
    dn = pltpu.roll(x, ROPE_FREQS, 1)
    partner = jnp.where((lane & ROPE_FREQS) == 0, up, dn)
    return x * cos + partner * sins


def _inproj_kernel(x_ref, mod_ref, g_ref, w_ref, cos_ref, sin_ref, qn_ref, kn_ref,
                   xbc_o, u_o, z_o, dt_o, gaq_o, gak_o, gav_o, waq_o, wak_o, wav_o):
    x = x_ref[...]
    ms = jnp.mean(x * x, axis=-1, keepdims=True)
    xn = x * lax.rsqrt(ms + EPS) * g_ref[...]
    h = xn * (1.0 + mod_ref[0, 1:2, :]) + mod_ref[0, 0:1, :]
    hb = h.astype(BF16)

    def proj(lo, hi):
        return _dot(hb, w_ref[:, lo:hi])

    cos = cos_ref[...]
    sins = sin_ref[...]
    scale = LOG2E * HEAD_DIM ** -0.5
    q = proj(_C_GAQ, _C_WAQ)
    qs = q * q
    inv = jnp.concatenate(
        [jnp.broadcast_to(lax.rsqrt(jnp.sum(qs[:, s * LANES:(s + 1) * LANES], axis=1, keepdims=True)
                                    * (1.0 / HEAD_DIM) + EPS), (q.shape[0], LANES)) for s in range(N_HEADS)], axis=1)
    gaq_o[...] = (_rope(q * inv * qn_ref[...], cos, sins) * scale).astype(BF16)
    waq_o[...] = (_rope(proj(_C_WAQ, _C_GAK), cos, sins) * scale).astype(BF16)
    k = proj(_C_GAK, _C_GAV)
    ks = k * k
    lane = lax.broadcasted_iota(jnp.int32, k.shape, 1)
    lo = lane < HEAD_DIM
    ms0 = jnp.sum(jnp.where(lo, ks, 0.0), axis=1, keepdims=True)
    ms1 = jnp.sum(jnp.where(lo, 0.0, ks), axis=1, keepdims=True)
    kinv = lax.rsqrt(jnp.where(lo, ms0, ms1) * (1.0 / HEAD_DIM) + EPS)
    gak_o[...] = _rope(k * kinv * kn_ref[...], cos, sins).astype(BF16)
    gav_o[...] = proj(_C_GAV, _C_WAK).astype(BF16)
    wak_o[...] = _rope(proj(_C_WAK, _C_WAV), cos, sins).astype(BF16)
    wav_o[...] = proj(_C_WAV, _C_END).astype(BF16)
    xbc_o[...] = proj(_C_XBC, _C_U)
    u_o[...] = proj(_C_U, _C_Z)
    z_o[...] = proj(_C_Z, _C_DT)
    dt_o[...] = proj(_C_DT, _C_GAQ)


def _mod_row(i, nblk, nb):
    return jnp.where(i % nblk == 0, nb, i // nblk)


def _inproj(x, mod, norm_g, w_packed, cos_t, sin_t, qn_g, kn_g, nb, nblk):
    t, d = x.shape
    row = lambda i: (i, 0)
    fix = lambda i: (0, 0)
    outs = [(SSD_XBC, F32), (GROUP_W, F32), (GROUP_W, F32), (LANES, F32),
            (N_HEADS * LANES, BF16), (LANES, BF16), (LANES, BF16),
            (N_HEADS * LANES, BF16), (LANES, BF16), (LANES, BF16)]
    return pl.pallas_call(
        _inproj_kernel,
        out_shape=[jax.ShapeDtypeStruct((t, w), dt) for w, dt in outs],
        grid=(t // TM,),
        in_specs=[pl.BlockSpec((TM, d), row),
                  pl.BlockSpec((1, 6, d), lambda i: (_mod_row(i, nblk, nb), 0, 0)),
                  pl.BlockSpec((1, d), fix),
                  pl.BlockSpec((d, _C_END), fix),
                  pl.BlockSpec((TM, LANES), lambda i: (i % nblk, 0)),
                  pl.BlockSpec((TM, LANES), lambda i: (i % nblk, 0)),
                  pl.BlockSpec((1, N_HEADS * LANES), fix),
                  pl.BlockSpec((1, LANES), fix)],
        out_specs=[pl.BlockSpec((TM, w), row) for w, _ in outs],
        compiler_params=_cp(("parallel",)),
        name="in_proj",
    )(x, mod, norm_g.reshape(1, d), w_packed, cos_t, sin_t,
      jnp.tile(qn_g, 2 * N_HEADS).reshape(1, -1), jnp.tile(kn_g, 2).reshape(1, -1))


def _rope_tables(lc, l):
    n_rows = l // GRID_W
    rows = jnp.repeat(jnp.arange(n_rows), GRID_W)
    cols = jnp.tile(jnp.arange(GRID_W), n_rows)
    inv = jnp.power(ROPE_BASE, -jnp.arange(ROPE_FREQS, dtype=F32) / ROPE_FREQS)
    ang = jnp.stack([rows, cols], axis=-1).astype(F32)[..., None] * inv
    cos = jnp.cos(ang)
    sin = jnp.sin(ang)
    cos64 = jnp.stack([cos, cos], axis=2).reshape(l, HEAD_DIM)
    sin64 = jnp.stack([-sin, sin], axis=2).reshape(l, HEAD_DIM)
    cos64 = jnp.concatenate([jnp.ones((lc, HEAD_DIM), F32), cos64], axis=0)
    sin64 = jnp.concatenate([jnp.zeros((lc, HEAD_DIM), F32), sin64], axis=0)
    return jnp.tile(cos64, (1, 2)), jnp.tile(sin64, (1, 2))


def _merge_heads(o2, kvh):
    tq = o2.shape[0] // 2
    oa, ob = o2[:tq], o2[tq:]
    lane = lax.broadcasted_iota(jnp.int32, oa.shape, 1)
    if kvh == 0:
        return jnp.where(lane < HEAD_DIM, oa, pltpu.roll(ob, HEAD_DIM, 1))
    return jnp.where(lane < HEAD_DIM, pltpu.roll(oa, HEAD_DIM, 1), ob)


def _stack_q(q_ref, kvh):
    return jnp.concatenate([q_ref[:, (2 * kvh) * LANES:(2 * kvh + 1) * LANES],
                            q_ref[:, (2 * kvh + 1) * LANES:(2 * kvh + 2) * LANES]], axis=0)


def _ga_kernel(q_ref, k_ref, v_ref, o_ref, *, lc):
    j = pl.program_id(1)

    def attend(nkeys):
        k = k_ref[0:nkeys, :]
        v = v_ref[0:nkeys, :]
        scores = [_dot_nt(_stack_q(q_ref, kvh), k) for kvh in range(2)]
        lane = lax.broadcasted_iota(jnp.int32, v.shape, 1)
        outs = []
        for kvh in range(2):
            s = scores[kvh]
            p = jnp.exp2((s - jnp.max(s, axis=1, keepdims=True)).astype(BF16))
            own = (lane < HEAD_DIM) if kvh == 0 else (lane >= HEAD_DIM)
            o2 = _dot(p, jnp.where(own, v, jnp.ones_like(v)))
            outs.append(_merge_heads(o2 / pltpu.roll(o2, HEAD_DIM, 1), kvh))
        o_ref[...] = jnp.concatenate(outs, axis=1)

    @pl.when(j < lc // GA_TQ)
    def _():
        attend(lc)

    @pl.when(j >= lc // GA_TQ)
    def _():
        attend(k_ref.shape[0])


def _ga(q, k, v, nb, s_len, lc):
    t = q.shape[0]
    nq = s_len // GA_TQ
    return pl.pallas_call(
        functools.partial(_ga_kernel, lc=lc),
        out_shape=jax.ShapeDtypeStruct((t, GROUP_W), F32),
        grid=(nb, nq),
        in_specs=[pl.BlockSpec((GA_TQ, N_HEADS * LANES), lambda b, j: (b * nq + j, 0)),
                  pl.BlockSpec((s_len, LANES), lambda b, j: (b, 0)),
                  pl.BlockSpec((s_len, LANES), lambda b, j: (b, 0))],
        out_specs=pl.BlockSpec((GA_TQ, GROUP_W), lambda b, j: (b * nq + j, 0)),
        compiler_params=_cp(("parallel", "arbitrary")),
        name="global_attn",
    )(q, k, v)


def _wa_kernel(sink_ref, q_ref, k_ref, v_ref, o_ref, *, lc):
    j = pl.program_id(1)
    s_len = k_ref.shape[0]
    n = j - lc // TQ
    start = pl.multiple_of(jnp.clip(lc + (n - 1) * TQ, lc, s_len - 3 * TQ), TQ)
    kc = k_ref[0:lc, :]
    vc = v_ref[0:lc, :]
    kb = k_ref[pl.ds(start, 3 * TQ), :]
    vb = v_ref[pl.ds(start, 3 * TQ), :]
    qpos = n * TQ + lax.broadcasted_iota(jnp.int32, (TQ, 3 * TQ), 0)
    kpos = (start - lc) + lax.broadcasted_iota(jnp.int32, (TQ, 3 * TQ), 1)
    reach = jnp.where(n >= 0, WINDOW, -1)
    valid = jnp.abs(qpos - kpos) <= reach
    valid = jnp.concatenate([valid, valid], axis=0)
    row = lax.broadcasted_iota(jnp.int32, (2 * TQ, 1), 0)
    outs = []
    for kvh in range(2):
        q2 = _stack_q(q_ref, kvh)
        sc = _dot_nt(q2, kc)
        sb = jnp.where(valid, _dot_nt(q2, kb), NEG_INF)
        sink = jnp.where(row < TQ, sink_ref[2 * kvh], sink_ref[2 * kvh + 1]) * LOG2E
        m = jnp.maximum(jnp.maximum(jnp.max(sc, axis=1, keepdims=True), jnp.max(sb, axis=1, keepdims=True)), sink)
        pc = jnp.exp2(sc - m)
        pb = jnp.exp2(sb - m)
        denom = jnp.sum(pc, axis=1, keepdims=True) + jnp.sum(pb, axis=1, keepdims=True) + jnp.exp2(sink - m)
        o2 = (_dot(pc.astype(BF16), vc) + _dot(pb.astype(BF16), vb)) / denom
        outs.append(_merge_heads(o2, kvh))
    o_ref[...] = jnp.concatenate(outs, axis=1)


def _wa(sink, q, k, v, nb, s_len, lc):
    t = q.shape[0]
    nq = s_len // TQ
    return pl.pallas_call(
        functools.partial(_wa_kernel, lc=lc),
        out_shape=jax.ShapeDtypeStruct((t, GROUP_W), F32),
        grid=(nb, nq),
        in_specs=[pl.BlockSpec(memory_space=pltpu.SMEM),
                  pl.BlockSpec((TQ, N_HEADS * LANES), lambda b, j: (b * nq + j, 0)),
                  pl.BlockSpec((s_len, LANES), lambda b, j: (b, 0)),
                  pl.BlockSpec((s_len, LANES), lambda b, j: (b, 0))],
        out_specs=pl.BlockSpec((TQ, GROUP_W), lambda b, j: (b * nq + j, 0)),
        compiler_params=_cp(("parallel", "arbitrary")),
        name="window_attn",
    )(sink, q, k, v)


def _s5_chunk_index(t, rev, nc_ctx, nc_tot):
    if not rev:
        return t
    return jnp.where(t < nc_ctx, nc_ctx - 1 - t, nc_tot - 1 - (t - nc_ctx))


def _s5_kernel(u_ref, m_ref, p_ref, g_ref, ar_ref, ai_ref, dsk_ref, y_ref, s_scr, h_scr, *, nb, nc_ctx, nc_tot):
    uf = u_ref[0]
    u = uf.astype(BF16)
    for d in range(2):
        for k in range(2):
            s_scr[d, k] = _dot(u, p_ref[d, k, 0])
    ar = [jnp.broadcast_to(ar_ref[d, 0], (nb, LANES)) for d in range(2)]
    ai = [[jnp.broadcast_to(ai_ref[d, k, 0], (nb, LANES)) for k in range(2)] for d in range(2)]

    def body(t, carry):
        out = []
        for d in range(2):
            h, hs = carry[d]
            rows = pl.ds(_s5_chunk_index(t, d == 1, nc_ctx, nc_tot), nb, stride=nc_tot)
            h_scr[d, rows, :] = h
            out.append((ar[d] * h + ai[d][0] * hs + s_scr[d, 0, rows, :],
                        ar[d] * hs + ai[d][1] * h + s_scr[d, 1, rows, :]))
        return tuple(out)

    zero = jnp.zeros((nb, LANES), F32)
    lax.fori_loop(0, nc_tot, body, ((zero, zero), (zero, zero)), unroll=2)
    y = uf * dsk_ref[0]
    for d in range(2):
        y = y + _dot(u, m_ref[d, 0]) + _dot(h_scr[d].astype(BF16), g_ref[d, 0])
    y_ref[0] = y


S5_TB = TM // S5_Q
S5_GPS = LANES // S5_CH


def _s5_pack_kernel(lo_ref, hi_ref, o_ref):
    for s in range(S5_Q):
        rows = pl.ds(s, S5_TB, stride=S5_Q)
        halves = (lo_ref[rows, :], hi_ref[rows, :])
        dst = S5_CH * (s % S5_GPS)
        for g in range(S5_GROUPS):
            slab = halves[g // S5_GPS]
            src = S5_CH * (g % S5_GPS)
            moved = slab if src == dst else pltpu.roll(slab, (dst - src) % LANES, 1)
            o_ref[g, :, s * S5_CH:(s + 1) * S5_CH] = moved[:, dst:dst + S5_CH]


def _s5_unpack_kernel(y_ref, o_ref):
    lane_grp = lax.broadcasted_iota(jnp.int32, (S5_TB, LANES), 1) // S5_CH
    for s in range(S5_Q):
        src = S5_CH * (s % S5_GPS)
        for half in range(S5_GROUPS // S5_GPS):
            acc = None
            for gl in range(S5_GPS):
                slab = y_ref[half * S5_GPS + gl, :, (s // S5_GPS) * LANES:(s // S5_GPS + 1) * LANES]
                dst = S5_CH * gl
                moved = slab if src == dst else pltpu.roll(slab, (dst - src) % LANES, 1)
                acc = moved if acc is None else jnp.where(lane_grp == gl, moved, acc)
            o_ref[half, pl.ds(s, S5_TB, stride=S5_Q), :] = acc


def _s5_params(lam_re, lam_im, log_dt, b_re, b_im, c_re, c_im, d_skip):
    q = S5_Q
    dt = jnp.exp(log_dt)[..., None]
    lr, li = lam_re, lam_im
    mag = jnp.exp(lr * dt)
    a_re = mag * jnp.cos(li * dt)
    a_im = mag * jnp.sin(li * dt)
    den = lr * lr + li * li
    f_re = ((a_re - 1.0) * lr + a_im * li) / den
    f_im = (a_im * lr - (a_re - 1.0) * li) / den
    bb_re = f_re[..., None] * b_re - f_im[..., None] * b_im
    bb_im = f_re[..., None] * b_im + f_im[..., None] * b_re
    kk = jnp.arange(q + 1, dtype=F32)[:, None, None, None]
    pmag = jnp.exp(kk * (lr * dt))
    pw_re = pmag * jnp.cos(kk * (li * dt))
    pw_im = pmag * jnp.sin(kk * (li * dt))
    e_re = c_re[None] * pw_re[:, :, :, None, :] - c_im[None] * pw_im[:, :, :, None, :]
    e_im = c_re[None] * pw_im[:, :, :, None, :] + c_im[None] * pw_re[:, :, :, None, :]
    kern = (jnp.einsum("kdgop,dgpc->kdgoc", e_re, bb_re, precision=HI)
            - jnp.einsum("kdgop,dgpc->kdgoc", e_im, bb_im, precision=HI))
    kern_t = kern[:q].transpose(1, 2, 4, 0, 3)
    zeros = jnp.zeros_like(kern_t)
    bbt_re = bb_re.transpose(0, 1, 3, 2)[:, :, None]
    bbt_im = bb_im.transpose(0, 1, 3, 2)[:, :, None]
    ct_re = c_re.transpose(0, 1, 3, 2)[:, :, :, None, :]
    ct_im = c_im.transpose(0, 1, 3, 2)[:, :, :, None, :]
    ms, ps, gs = [], [], []
    for d in range(2):
        ext = (jnp.concatenate([zeros[d], kern_t[d]], axis=2) if d == 0
               else jnp.concatenate([kern_t[d, :, :, ::-1], zeros[d]], axis=2))
        ext = ext.reshape(S5_GROUPS, S5_CH, 2 * S5_BLK)
        lo = [(q - s) if d == 0 else (q - 1 - s) for s in range(q)]
        md = jnp.stack([ext[:, :, a * S5_CH:a * S5_CH + S5_BLK] for a in lo], axis=1)
        ms.append(md.reshape(S5_GROUPS, S5_BLK, S5_BLK))
        pidx = (q - 1 - jnp.arange(q)) if d == 0 else jnp.arange(q)
        pr = pw_re[pidx, d].transpose(1, 0, 2)[:, :, None, :]
        pi = pw_im[pidx, d].transpose(1, 0, 2)[:, :, None, :]
        p_re = pr * bbt_re[d] - pi * bbt_im[d]
        p_im = pr * bbt_im[d] + pi * bbt_re[d]
        pd = jnp.stack([jnp.concatenate([p_re, p_im], axis=3), jnp.concatenate([p_im, p_re], axis=3)])
        ps.append(pd.reshape(2, S5_GROUPS, S5_BLK, 2 * S5_STATE))
        gidx = (jnp.arange(q) + 1) if d == 0 else (q - jnp.arange(q))
        gw_re = pw_re[gidx, d].transpose(1, 2, 0)[..., None]
        gw_im = pw_im[gidx, d].transpose(1, 2, 0)[..., None]
        g_re = ct_re[d] * gw_re - ct_im[d] * gw_im
        g_im = ct_re[d] * gw_im + ct_im[d] * gw_re
        gs.append(jnp.concatenate([g_re, -g_im], axis=1).reshape(S5_GROUPS, 2 * S5_STATE, S5_BLK))
    ar = jnp.concatenate([pw_re[q], pw_re[q]], axis=-1)[:, :, None, :]
    ai = jnp.stack([jnp.concatenate([-pw_im[q], pw_im[q]], axis=-1),
                    jnp.concatenate([pw_im[q], -pw_im[q]], axis=-1)], axis=1)[:, :, :, None, :]
    dsk = jnp.tile(d_skip.reshape(S5_GROUPS, 1, S5_CH), (1, 1, q))
    return (jnp.stack(ms).astype(BF16), jnp.stack(ps).astype(BF16), jnp.stack(gs).astype(BF16),
            ar.astype(F32), ai.astype(F32), dsk.astype(F32))


def _s5(u, params, nb, s_len, lc):
    m, p, g, ar, ai, dsk = params
    nc_tot = s_len // S5_Q
    nc_ctx = lc // S5_Q
    t = u.shape[0]
    r = nb * nc_tot
    grp = pl.BlockSpec((S5_GROUPS, S5_TB, S5_BLK), lambda i: (0, i, 0))
    ug = pl.pallas_call(
        _s5_pack_kernel,
        out_shape=jax.ShapeDtypeStruct((S5_GROUPS, r, S5_BLK), F32),
        grid=(t // TM,),
        in_specs=[pl.BlockSpec((TM, LANES), lambda i: (i, 0)), pl.BlockSpec((TM, LANES), lambda i: (i, 1))],
        out_specs=grp,
        compiler_params=_cp(("parallel",)), name="s5_pack",
    )(u, u)
    y = pl.pallas_call(
        functools.partial(_s5_kernel, nb=nb, nc_ctx=nc_ctx, nc_tot=nc_tot),
        out_shape=jax.ShapeDtypeStruct((S5_GROUPS, r, S5_BLK), F32),
        grid=(S5_GROUPS,),
        in_specs=[pl.BlockSpec((1, r, S5_BLK), lambda gi: (gi, 0, 0)),
                  pl.BlockSpec((2, 1, S5_BLK, S5_BLK), lambda gi: (0, gi, 0, 0)),
                  pl.BlockSpec((2, 2, 1, S5_BLK, 2 * S5_STATE), lambda gi: (0, 0, gi, 0, 0)),
                  pl.BlockSpec((2, 1, 2 * S5_STATE, S5_BLK), lambda gi: (0, gi, 0, 0)),
                  pl.BlockSpec((2, 1, 1, 2 * S5_STATE), lambda gi: (0, gi, 0, 0)),
                  pl.BlockSpec((2, 2, 1, 1, 2 * S5_STATE), lambda gi: (0, 0, gi, 0, 0)),
                  pl.BlockSpec((1, 1, S5_BLK), lambda gi: (gi, 0, 0))],
        out_specs=pl.BlockSpec((1, r, S5_BLK), lambda gi: (gi, 0, 0)),
        scratch_shapes=[pltpu.VMEM((2, 2, r, 2 * S5_STATE), F32), pltpu.VMEM((2, r, 2 * S5_STATE), F32)],
        compiler_params=_cp(("parallel",)),
        name="s5_scan",
    )(ug, m, p, g, ar, ai, dsk)
    return pl.pallas_call(
        _s5_unpack_kernel,
        out_shape=jax.ShapeDtypeStruct((GROUP_W // LANES, t, LANES), F32),
        grid=(t // TM,), in_specs=[grp],
        out_specs=pl.BlockSpec((GROUP_W // LANES, TM, LANES), lambda i: (0, i, 0)),
        compiler_params=_cp(("parallel",)), name="s5_unpack",
    )(y)


def _conv_kernel(x_ref, prev_ref, next_ref, w_ref, b_ref, o_ref, *, nblk):
    i = pl.program_id(0) % nblk
    x = x_ref[...]
    rows = x.shape[0]
    ridx = lax.broadcasted_iota(jnp.int32, x.shape, 0)
    prev_row = jnp.where(i <= 1, 0.0, prev_ref[SUBLANES - 1:SUBLANES, :])
    next_row = jnp.where(jnp.logical_or(i == 0, i == nblk - 1), 0.0, next_ref[0:1, :])
    xm = jnp.where(ridx == 0, prev_row, pltpu.roll(x, 1, 0))
    xp = jnp.where(ridx == rows - 1, next_row, pltpu.roll(x, rows - 1, 0))
    y = xm * w_ref[0:1, :] + x * w_ref[1:2, :] + xp * w_ref[2:3, :] + b_ref[...]
    o_ref[...] = _silu(y)


def _conv(xbc, w, b, nblk):
    t, c = xbc.shape
    per = TM // SUBLANES
    last = t // SUBLANES - 1
    return pl.pallas_call(
        functools.partial(_conv_kernel, nblk=nblk),
        out_shape=jax.ShapeDtypeStruct((t, c), F32),
        grid=(t // TM,),
        in_specs=[pl.BlockSpec((TM, c), lambda i: (i, 0)),
                  pl.BlockSpec((SUBLANES, c), lambda i: (jnp.maximum(i * per - 1, 0), 0)),
                  pl.BlockSpec((SUBLANES, c), lambda i: (jnp.minimum((i + 1) * per, last), 0)),
                  pl.BlockSpec((3, c), lambda i: (0, 0)),
                  pl.BlockSpec((1, c), lambda i: (0, 0))],
        out_specs=pl.BlockSpec((TM, c), lambda i: (i, 0)),
        compiler_params=_cp(("parallel",)),
        name="ssd_conv",
    )(xbc, xbc, xbc, w, b.reshape(1, c))


_X_B = GROUP_W
_X_C = GROUP_W + SSD_NGROUPS * SSD_STATE


def _ssd_kernel(xc_ref, dt_ref, dtt_ref, bias_ref, a_ref, biast_ref, at_ref, dsk_ref, y_ref, st_ref, *, rev):
    c = pl.program_id(1)

    @pl.when(c == 0)
    def _():
        st_ref[...] = jnp.zeros_like(st_ref)

    base = SSD_HEADS if rev else 0
    xc = xc_ref[...]
    x = xc[:, 0:GROUP_W]
    dt = _softplus(dt_ref[...] + bias_ref[...])
    a = dt * a_ref[...]
    dtt = _softplus(dtt_ref[0] + biast_ref[...])
    at = dtt * at_ref[...]
    ri = lax.broadcasted_iota(jnp.int32, (TQ, TQ), 0)
    ci = lax.broadcasted_iota(jnp.int32, (TQ, TQ), 1)
    causal = (ci >= ri) if rev else (ri >= ci)
    tri = jnp.where(causal, 1.0, 0.0)
    cum_c = _dot_hi(tri, a)
    cum_r = _dot_nt_hi(at, tri)
    edge = 0 if rev else TQ - 1
    tot = cum_c[edge:edge + 1, :]

    shape = (TQ, GROUP_W)
    xdt = x * _per_head_cols(dt, base, SSD_HEADS, shape)
    lane = lax.broadcasted_iota(jnp.int32, shape, 1)
    y = jnp.zeros(shape, F32)
    bmat = [xc[:, _X_B + g * SSD_STATE:_X_B + (g + 1) * SSD_STATE].astype(BF16) for g in range(SSD_NGROUPS)]
    cmat = [xc[:, _X_C + g * SSD_STATE:_X_C + (g + 1) * SSD_STATE].astype(BF16) for g in range(SSD_NGROUPS)]
    cb = [_dot_nt(cmat[g], bmat[g]) for g in range(SSD_NGROUPS)]
    for h in range(SSD_HEADS):
        col = base + h
        seg = jnp.where(causal, cum_c[:, col:col + 1] - cum_r[col:col + 1, :], NEG_INF)
        scores = cb[h // 2] * jnp.exp(seg)
        xh = jnp.where((lane >= h * HEAD_DIM) & (lane < (h + 1) * HEAD_DIM), xdt, 0.0)
        y = y + _dot(scores.astype(BF16), xh.astype(BF16))
    st = st_ref[...]
    yo = jnp.concatenate(
        [_dot_nt(cmat[g], st[g * SSD_STATE:(g + 1) * SSD_STATE].astype(BF16)) for g in range(SSD_NGROUPS)], axis=1)
    y = y + yo * _per_head_cols(jnp.exp(cum_c), base, SSD_HEADS, shape)
    if not rev:
        y = y + x * dsk_ref[...]
    y_ref[...] = y
    xd = xdt * _per_head_cols(jnp.exp(tot - cum_c), base, SSD_HEADS, shape)
    xdt_t = xd.T.astype(BF16)
    decay = jnp.exp(tot)
    for g in range(SSD_NGROUPS):
        new = _dot(xdt_t[g * SSD_STATE:(g + 1) * SSD_STATE], bmat[g])
        for hh in range(2):
            h = 2 * g + hh
            r0 = h * HEAD_DIM
            st_ref[r0:r0 + HEAD_DIM, :] = (decay[:, base + h:base + h + 1] * st[r0:r0 + HEAD_DIM]
                                           + new[hh * HEAD_DIM:(hh + 1) * HEAD_DIM])


def _dot_nt_hi(a, b):
    return lax.dot_general(a, b, (((1,), (1,)), ((), ())), preferred_element_type=F32, precision=HI)


def _ssd_chunk(c, rev, nc_ctx, nc_tot):
    if not rev:
        return c
    return jnp.where(c < nc_ctx, nc_ctx - 1 - c, nc_tot - 1 - (c - nc_ctx))


def _ssd_dir(xc, dt, dtt, bias, a, biast, at, dsk, rev, nb, s_len, lc):
    t = xc.shape[0]
    nc_tot = s_len // TQ
    nc_ctx = lc // TQ
    cidx = lambda c: _ssd_chunk(c, rev, nc_ctx, nc_tot)
    fix = lambda b, c: (0, 0)
    return pl.pallas_call(
        functools.partial(_ssd_kernel, rev=rev),
        out_shape=jax.ShapeDtypeStruct((t, GROUP_W), F32),
        grid=(nb, nc_tot),
        in_specs=[pl.BlockSpec((TQ, SSD_XBC), lambda b, c: (b * nc_tot + cidx(c), 0)),
                  pl.BlockSpec((TQ, LANES), lambda b, c: (b * nc_tot + cidx(c), 0)),
                  pl.BlockSpec((1, SUBLANES, TQ), lambda b, c: (b, 0, cidx(c))),
                  pl.BlockSpec((1, LANES), fix),
                  pl.BlockSpec((1, LANES), fix),
                  pl.BlockSpec((SUBLANES, TQ), fix),
                  pl.BlockSpec((SUBLANES, TQ), fix),
                  pl.BlockSpec((1, GROUP_W), fix)],
        out_specs=pl.BlockSpec((TQ, GROUP_W), lambda b, c: (b * nc_tot + cidx(c), 0)),
        scratch_shapes=[pltpu.VMEM((SSD_HEADS * HEAD_DIM, SSD_STATE), F32)],
        compiler_params=_cp(("parallel", "arbitrary")),
        name="ssd_rev" if rev else "ssd_fwd",
    )(xc, dt, dtt, bias, a, biast, at, dsk)


def _ssd(xbc, dt, conv_w, conv_b, dt_bias, a_log, d_skip, nb, s_len, lc):
    nblk = s_len // TM
    xc = _conv(xbc, conv_w, conv_b, nblk)
    nd = 2 * SSD_HEADS
    dtt = dt[:, :nd].reshape(nb, s_len, nd).transpose(0, 2, 1)
    bias = jnp.pad(dt_bias.reshape(1, nd), ((0, 0), (0, LANES - nd)))
    a = jnp.pad(-jnp.exp(a_log).reshape(1, nd), ((0, 0), (0, LANES - nd)))
    biast = jnp.broadcast_to(dt_bias.reshape(nd, 1), (nd, TQ))
    at = jnp.broadcast_to(-jnp.exp(a_log).reshape(nd, 1), (nd, TQ))
    dsk = jnp.repeat(d_skip, HEAD_DIM).reshape(1, GROUP_W)
    args = (xc, dt, dtt, bias, a, biast, at, dsk)
    return _ssd_dir(*args, False, nb, s_len, lc), _ssd_dir(*args, True, nb, s_len, lc)


def _outproj_kernel(x_ref, ys5_ref, oga_ref, y0_ref, y1_ref, z_ref, owa_ref, mod_ref, gluw_ref, glub_ref,
                    ng_ref, wout_ref, n2_ref, wr_ref, br_ref, xn_o, h2_o, route_o):
    gl = _gelu_tanh(jnp.concatenate([ys5_ref[0], ys5_ref[1]], axis=1))
    a = gl * _sigmoid(_dot(gl.astype(BF16), gluw_ref[...]) + glub_ref[...])
    m = (y0_ref[...] + y1_ref[...]) * _silu(z_ref[...])
    m = m * lax.rsqrt(jnp.mean(m * m, axis=-1, keepdims=True) + EPS) * ng_ref[...]
    w = wout_ref
    mix = (_dot(a.astype(BF16), w[0:GROUP_W, :]) + _dot(oga_ref[...].astype(BF16), w[GROUP_W:2 * GROUP_W, :])
           + _dot(m.astype(BF16), w[2 * GROUP_W:3 * GROUP_W, :]) + _dot(owa_ref[...].astype(BF16), w[3 * GROUP_W:, :]))
    xn = x_ref[...] + mod_ref[0, 2:3, :] * mix
    xn_o[...] = xn
    h2 = xn * lax.rsqrt(jnp.mean(xn * xn, axis=-1, keepdims=True) + EPS) * n2_ref[...]
    h2 = h2 * (1.0 + mod_ref[0, 4:5, :]) + mod_ref[0, 3:4, :]
    h2_o[...] = _pack_bf16_pair(h2)
    h_hi = h2.astype(BF16)
    h_lo = (h2 - h_hi.astype(F32)).astype(BF16)
    logits = _dot(h_hi, wr_ref[0]) + (_dot(h_lo, wr_ref[0]) + _dot(h_hi, wr_ref[1])) + br_ref[...]
    lane = lax.broadcasted_iota(jnp.int32, logits.shape, 1)
    big = 4 * LANES
    lcoarse = jnp.where(lane < MOE_GROUPS, logits, NEG_INF)
    mx = jnp.max(lcoarse, axis=1, keepdims=True)
    den = jnp.sum(jnp.exp(lcoarse - mx), axis=1, keepdims=True)
    grp = jnp.min(jnp.where(lcoarse == mx, lane, big), axis=1, keepdims=True)
    pg = 1.0 / den
    lo = ROUTE_FINE0 + grp * MOE_PER_GROUP
    lf = jnp.where(lane >= lo, jnp.where(lane < lo + MOE_PER_GROUP, logits, NEG_INF), NEG_INF)
    v1 = jnp.max(lf, axis=1, keepdims=True)
    i1 = jnp.min(jnp.where(lf == v1, lane, big), axis=1, keepdims=True)
    lf2 = jnp.where(lane == i1, NEG_INF, lf)
    v2 = jnp.max(lf2, axis=1, keepdims=True)
    i2 = jnp.min(jnp.where(lf2 == v2, lane, big), axis=1, keepdims=True)
    e2 = jnp.exp(v2 - v1)
    w1 = pg / (1.0 + e2)
    w2 = w1 * e2
    route = jnp.where(lane == 0, (i1 - ROUTE_FINE0).astype(F32),
                      jnp.where(lane == 1, (i2 - ROUTE_FINE0).astype(F32),
                                jnp.where(lane == 2, w1, jnp.where(lane == 3, w2, 0.0))))
    route_o[...] = route


def _outproj(x, ys5, oga, y0, y1, z, owa, mod, glu_w, glu_b, ssd_norm_g, w_out, norm2_g, wr, br, nb, nblk):
    t, d = x.shape
    row = lambda i: (i, 0)
    fix = lambda i: (0, 0)
    gw = pl.BlockSpec((TM, GROUP_W), row)
    wr_hi = wr.astype(BF16)
    return pl.pallas_call(
        _outproj_kernel,
        out_shape=[jax.ShapeDtypeStruct((t, d), F32), jax.ShapeDtypeStruct((t, d // 2), jnp.uint32),
                   jax.ShapeDtypeStruct((t, LANES), F32)],
        grid=(t // TM,),
        in_specs=[pl.BlockSpec((TM, d), row), pl.BlockSpec((GROUP_W // LANES, TM, LANES), lambda i: (0, i, 0)),
                  gw, gw, gw, gw, gw,
                  pl.BlockSpec((1, 6, d), lambda i: (_mod_row(i, nblk, nb), 0, 0)),
                  pl.BlockSpec((GROUP_W, GROUP_W), fix),
                  pl.BlockSpec((1, GROUP_W), fix),
                  pl.BlockSpec((1, GROUP_W), fix),
                  pl.BlockSpec((d, d), fix),
                  pl.BlockSpec((1, d), fix),
                  pl.BlockSpec((2, d, LANES), lambda i: (0, 0, 0)),
                  pl.BlockSpec((1, LANES), fix)],
        out_specs=[pl.BlockSpec((TM, d), row), pl.BlockSpec((TM, d // 2), row), pl.BlockSpec((TM, LANES), row)],
        compiler_params=_cp(("parallel",)),
        name="out_proj_router",
    )(x, ys5, oga, y0, y1, z, owa, mod, glu_w.astype(BF16), glu_b.reshape(1, -1), ssd_norm_g.reshape(1, -1),
      w_out.astype(BF16), norm2_g.reshape(1, -1), jnp.stack([wr_hi, (wr - wr_hi.astype(F32)).astype(BF16)]), br)


def _pack_router(coarse_w, coarse_b, fine_w, fine_b):
    def lanes(coarse, fine):
        gap = jnp.zeros(coarse.shape[:-1] + (ROUTE_FINE0 - MOE_GROUPS,), F32)
        tail = jnp.zeros(coarse.shape[:-1] + (LANES - ROUTE_FINE0 - N_EXPERTS,), F32)
        return jnp.concatenate([coarse, gap, fine, tail], axis=-1)

    return lanes(coarse_w, fine_w), lanes(coarse_b[None, :], fine_b[None, :])


def _gather_rows(src, idx):
    m = idx.shape[0]
    d = src.shape[1]
    workers = SC_CORES * SC_SUBCORES
    nch = m // (workers * SC_GATHER_K)
    assert nch * workers * SC_GATHER_K == m
    mesh = plsc.VectorSubcoreMesh(core_axis_name="c", subcore_axis_name="s")

    @functools.partial(
        pl.kernel, mesh=mesh,
        out_type=jax.ShapeDtypeStruct((m, d), src.dtype),
        scratch_types=[pltpu.VMEM((nch, SC_GATHER_K), jnp.int32),
                       pltpu.VMEM((SC_GATHER_K, d), src.dtype),
                       pltpu.SemaphoreType.DMA],
    )
    def gather(src_hbm, idx_hbm, out_hbm, idx_v, rows_v, sem):
        wid = lax.axis_index("s") * SC_CORES + lax.axis_index("c")
        pltpu.sync_copy(idx_hbm.at[wid], idx_v)

        @pl.loop(0, nch)
        def _(j):
            off = pl.multiple_of((wid * nch + j) * SC_GATHER_K, SC_GATHER_K)
            pltpu.async_copy(src_hbm.at[idx_v.at[j]], rows_v, sem).wait()
            pltpu.sync_copy(rows_v, out_hbm.at[pl.ds(off, SC_GATHER_K)])

    return gather(src, idx.reshape(workers, nch, SC_GATHER_K))


def _scatter_rows(src, dst0, dst1, nrows):
    t, d = src.shape
    workers = SC_CORES * SC_SUBCORES
    nch = t // (workers * SC_GATHER_K)
    assert nch * workers * SC_GATHER_K == t
    mesh = plsc.VectorSubcoreMesh(core_axis_name="c", subcore_axis_name="s")

    @functools.partial(
        pl.kernel, mesh=mesh,
        out_type=jax.ShapeDtypeStruct((nrows, d), src.dtype),
        scratch_types=[pltpu.VMEM((nch, SC_GATHER_K), jnp.int32),
                       pltpu.VMEM((nch, SC_GATHER_K), jnp.int32),
                       pltpu.VMEM((SC_GATHER_K, d), src.dtype)],
    )
    def scatter(src_hbm, d0_hbm, d1_hbm, out_hbm, i0_v, i1_v, rows_v):
        wid = lax.axis_index("s") * SC_CORES + lax.axis_index("c")
        pltpu.sync_copy(d0_hbm.at[wid], i0_v)
        pltpu.sync_copy(d1_hbm.at[wid], i1_v)

        @pl.loop(0, nch)
        def _(j):
            off = pl.multiple_of((wid * nch + j) * SC_GATHER_K, SC_GATHER_K)
            pltpu.sync_copy(src_hbm.at[pl.ds(off, SC_GATHER_K)], rows_v)
            pltpu.sync_copy(rows_v, out_hbm.at[i0_v.at[j]])
            pltpu.sync_copy(rows_v, out_hbm.at[i1_v.at[j]])

    return scatter(src, dst0.reshape(workers, nch, SC_GATHER_K), dst1.reshape(workers, nch, SC_GATHER_K))


def _expert_kernel(be_ref, nused_ref, nvalid_ref, x_ref, wg_ref, wu_ref, wd_ref, o_ref, wg_s, wu_s, wd_s):
    i = pl.program_id(0)
    new_expert = jnp.logical_or(i == 0, be_ref[i] != be_ref[jnp.maximum(i - 1, 0)])

    @pl.when(jnp.logical_and(i < nused_ref[0], new_expert))
    def _():
        wg_s[...] = wg_ref[0, 0].astype(BF16)
        wu_s[...] = wu_ref[0, 0].astype(BF16)
        wd_s[...] = wd_ref[0, 0].astype(BF16)

    @pl.when(i < nused_ref[0])
    def _():
        row = lax.broadcasted_iota(jnp.int32, x_ref.shape, 0)
        lo, hi = _unpack_bf16_pair(jnp.where(row < nvalid_ref[i], x_ref[...], jnp.uint32(0)))
        lo = lo.astype(BF16)
        hi = hi.astype(BF16)
        half = lo.shape[1]
        gate = _dot(lo, wg_s[0:half, :]) + _dot(hi, wg_s[half:, :])
        up = _dot(lo, wu_s[0:half, :]) + _dot(hi, wu_s[half:, :])
        o_ref[...] = _pack_bf16_pair(_dot((_silu(gate) * up).astype(BF16), wd_s[...]))

    @pl.when(i >= nused_ref[0])
    def _():
        o_ref[...] = jnp.zeros_like(o_ref)


def _experts(xs, blk_e, n_used, n_valid, wg, wu, wd, layer):
    rows, dp = xs.shape
    d = 2 * dp
    nblocks = rows // MOE_TM
    de = wg.shape[3]
    wsel = lambda i, be, nu, nv: (layer, be[i], 0, 0)
    grid_spec = pltpu.PrefetchScalarGridSpec(
        num_scalar_prefetch=3,
        grid=(nblocks,),
        in_specs=[pl.BlockSpec((MOE_TM, dp), lambda i, be, nu, nv: (i, 0)),
                  pl.BlockSpec((1, 1, d, de), wsel),
                  pl.BlockSpec((1, 1, d, de), wsel),
                  pl.BlockSpec((1, 1, de, d), wsel)],
        out_specs=pl.BlockSpec((MOE_TM, dp), lambda i, be, nu, nv: (i, 0)),
        scratch_shapes=[pltpu.VMEM((d, de), BF16), pltpu.VMEM((d, de), BF16), pltpu.VMEM((de, d), BF16)],
    )
    return pl.pallas_call(
        _expert_kernel,
        out_shape=jax.ShapeDtypeStruct((rows, dp), jnp.uint32),
        grid_spec=grid_spec,
        compiler_params=_cp(("arbitrary",)),
        name="moe_experts",
    )(blk_e, n_used, n_valid, xs, wg, wu, wd)


def _combine_kernel(x_ref, r0_ref, r1_ref, route_ref, mod_ref, fg_ref, o_ref, *, final):
    route = route_ref[...]
    r0 = jnp.concatenate(_unpack_bf16_pair(r0_ref[...]), axis=1)
    r1 = jnp.concatenate(_unpack_bf16_pair(r1_ref[...]), axis=1)
    f = route[:, 2:3] * r0 + route[:, 3:4] * r1
    y = x_ref[...] + mod_ref[0, 5:6, :] * f
    if final:
        y = y * lax.rsqrt(jnp.mean(y * y, axis=-1, keepdims=True) + EPS) * fg_ref[...]
    o_ref[...] = y


def _combine(xn, rows2, route, mod, final_g, nb, nblk, final):
    t, d = xn.shape
    if final:
        nlat = nblk - 1
        grid = (nb * nlat,)
        src = lambda i: ((i // nlat) * nblk + 1 + i % nlat, 0)
        modi = lambda i: (i // nlat, 0, 0)
        out_rows = nb * nlat * TM
    else:
        grid = (t // TM,)
        src = lambda i: (i, 0)
        modi = lambda i: (_mod_row(i, nblk, nb), 0, 0)
        out_rows = t
    return pl.pallas_call(
        functools.partial(_combine_kernel, final=final),
        out_shape=jax.ShapeDtypeStruct((out_rows, d), F32),
        grid=grid,
        in_specs=[pl.BlockSpec((TM, d), src),
                  pl.BlockSpec((TM, d // 2), src),
                  pl.BlockSpec((TM, d // 2), lambda i: (src(i)[0] + t // TM, 0)),
                  pl.BlockSpec((TM, LANES), src),
                  pl.BlockSpec((1, 6, d), modi),
                  pl.BlockSpec((1, d), lambda i: (0, 0))],
        out_specs=pl.BlockSpec((TM, d), lambda i: (i, 0)),
        compiler_params=_cp(("parallel",)),
        name="moe_combine_final" if final else "moe_combine",
    )(xn, rows2, rows2, route, mod, final_g.reshape(1, d))


def _moe(h2, route, wg, wu, wd, layer):
    t, d = h2.shape
    n_slots = 2 * t
    experts = jnp.arange(N_EXPERTS, dtype=F32)[None, :]
    oh0 = (route[:, 0:1] == experts).astype(F32)
    oh1 = (route[:, 1:2] == experts).astype(F32)
    both = (oh0 + oh1).reshape(t // LANES, LANES, N_EXPERTS)
    tri = jnp.tril(jnp.ones((LANES, LANES), F32))
    intra = jnp.einsum("ij,bjk->bik", tri, both)
    blk_tot = intra[:, -1, :]
    blk_cum = jnp.cumsum(blk_tot, axis=0)
    earlier = (intra - both + (blk_cum - blk_tot)[:, None, :]).reshape(t, N_EXPERTS)
    counts = blk_cum[-1].astype(jnp.int32)
    pcounts = (counts + MOE_TM - 1) // MOE_TM * MOE_TM
    pends = jnp.cumsum(pcounts)
    pstarts = pends - pcounts
    base = pstarts.astype(F32)[None, :] + earlier
    dest0 = jnp.sum(oh0 * base, axis=1).astype(jnp.int32)
    dest1 = jnp.sum(oh1 * base, axis=1).astype(jnp.int32)
    nblocks = -(-n_slots // MOE_TM) + N_EXPERTS
    nrows = -(-nblocks * MOE_TM // GATHER_ROWS) * GATHER_ROWS
    nblocks = nrows // MOE_TM
    blk_start = jnp.arange(nblocks, dtype=jnp.int32) * MOE_TM
    blk_e = jnp.minimum(jnp.sum((pends[None, :] <= blk_start[:, None]).astype(jnp.int32), axis=1), N_EXPERTS - 1)
    n_used = (pends[-1] // MOE_TM).astype(jnp.int32).reshape(1)
    n_valid = jnp.clip((pstarts + counts)[blk_e] - blk_start, 0, MOE_TM).astype(jnp.int32)
    xs = _scatter_rows(h2, dest0, dest1, nrows)
    ys = _experts(xs, blk_e, n_used, n_valid, wg, wu, wd, layer)
    return _gather_rows(ys, jnp.concatenate([dest0, dest1]))


def kernel(x, c, ctx, c_ctx, ada_w, ada_b, norm1_g, norm2_g, w_in, w_out, s5_lam_re, s5_lam_im, s5_log_dt, s5_b_re, s5_b_im, s5_c_re, s5_c_im, s5_d, s5_glu_w, s5_glu_b, ga_qn_g, ga_kn_g, ssd_conv_w, ssd_conv_b, ssd_dt_bias, ssd_a_log, ssd_d, ssd_norm_g, wa_sink, moe_coarse_w, moe_coarse_b, moe_fine_w, moe_fine_b, moe_w_gate, moe_w_up, moe_w_down, final_g):
    nb, l, d = x.shape
    lc = ctx.shape[1]
    depth = ada_w.shape[0]
    assert lc == TM and l % TM == 0 and nb <= SUBLANES - 1 and d == D_MODEL
    s_len = lc + l
    nblk = s_len // TM
    t = nb * s_len

    xm = jnp.concatenate([ctx, x], axis=1).reshape(t, d)
    cc = jnp.zeros((SUBLANES, d), F32).at[:nb].set(c).at[nb].set(c_ctx)
    mods = _ada(cc, ada_w, ada_b).reshape(depth, SUBLANES, 6, d)
    cos_t, sin_t = _rope_tables(lc, l)
    w_packed = jax.vmap(_pack_w_in)(w_in)
    s5_tabs = jax.vmap(_s5_params)(s5_lam_re, s5_lam_im, s5_log_dt, s5_b_re, s5_b_im, s5_c_re, s5_c_im, s5_d)
    wrs, brs = jax.vmap(_pack_router)(moe_coarse_w, moe_coarse_b, moe_fine_w, moe_fine_b)

    out = None
    for i in range(depth):
        mod = mods[i]
        (xbc, u, z, dt, gaq, gak, gav, waq, wak, wav) = _inproj(
            xm, mod, norm1_g[i], w_packed[i], cos_t, sin_t, ga_qn_g[i], ga_kn_g[i], nb, nblk)
        ys5 = _s5(u, tuple(tab[i] for tab in s5_tabs), nb, s_len, lc)
        oga = _ga(gaq, gak, gav, nb, s_len, lc)
        y0, y1 = _ssd(xbc, dt, ssd_conv_w[i], ssd_conv_b[i], ssd_dt_bias[i], ssd_a_log[i], ssd_d[i], nb, s_len, lc)
        owa = _wa(wa_sink[i], waq, wak, wav, nb, s_len, lc)
        wr, br = wrs[i], brs[i]
        xn, h2, route = _outproj(xm, ys5, oga, y0, y1, z, owa, mod, s5_glu_w[i], s5_glu_b[i], ssd_norm_g[i],
                                 w_out[i], norm2_g[i], wr, br, nb, nblk)
        rows2 = _moe(h2, route, moe_w_gate, moe_w_up, moe_w_down, i)
        final = i == depth - 1
        xm = _combine(xn, rows2, route, mod, final_g, nb, nblk, final)
        if final:
            out = xm.reshape(nb, l, d)
    return out
```

```python
import functools
import math

import jax
import jax.numpy as jnp
import numpy as np
from jax import lax
from jax.experimental import pallas as pl
from jax.experimental.pallas import tpu as pltpu
from jax.experimental.pallas import tpu_sc as plsc

F32 = jnp.float32
BF16 = jnp.bfloat16
HI = lax.Precision.HIGHEST

D_MODEL = 1024
GRID_W = 64
GROUP_W = 256
HEAD_DIM = 64
ROPE_FREQS = HEAD_DIM // 4
ROPE_BASE = 10000.0
EPS = 1e-6
S5_CH = 16
S5_GROUPS = GROUP_W // S5_CH
S5_STATE = 64
N_HEADS = 4
SSD_HEADS = 4
SSD_NGROUPS = 2
SSD_STATE = 128
SSD_XBC = GROUP_W + 2 * SSD_NGROUPS * SSD_STATE
WINDOW = 128
MOE_GROUPS = 4
MOE_PER_GROUP = 8
N_EXPERTS = 32
D_EXPERT = D_MODEL // 2

LANES = 128
SUBLANES = 8
TM = 256
TQ = 128
GA_TQ = 128
S5_Q = 32
S5_BLK = S5_Q * S5_CH
MOE_TM = 256
SC_CORES = 2
SC_SUBCORES = 16
SC_GATHER_K = 32
GATHER_ROWS = SC_CORES * SC_SUBCORES * SC_GATHER_K
ROUTE_FINE0 = 32
VMEM_LIMIT = 56 * 1024 * 1024

NEG_INF = float("-inf")
LOG2E = math.log2(math.e)


def _cp(sem, vmem=VMEM_LIMIT):
    return pltpu.CompilerParams(dimension_semantics=sem, vmem_limit_bytes=vmem)


def _dot(a, b):
    return jnp.dot(a, b, preferred_element_type=F32)


def _dot_hi(a, b):
    return jnp.dot(a, b, preferred_element_type=F32, precision=HI)


def _dot_nt(a, b):
    return lax.dot_general(a, b, (((1,), (1,)), ((), ())), preferred_element_type=F32)


def _sigmoid(x):
    return 1.0 / (1.0 + jnp.exp(-x))


def _silu(x):
    return x * _sigmoid(x)


def _gelu_tanh(x):
    return 0.5 * x * (1.0 + jnp.tanh(math.sqrt(2.0 / math.pi) * (x + 0.044715 * (x * x * x))))


def _softplus(x):
    return jnp.maximum(x, 0.0) + jnp.log1p(jnp.exp(-jnp.abs(x)))


_HI16 = 0xFFFF0000


def _pack_bf16_pair(x):
    n = x.shape[1] // 2
    bits = pltpu.bitcast(x.astype(BF16).astype(F32), jnp.uint32)
    return (bits[:, n:] & jnp.uint32(_HI16)) | (bits[:, :n] >> 16)


def _unpack_bf16_pair(w):
    return pltpu.bitcast(w << 16, F32), pltpu.bitcast(w & jnp.uint32(_HI16), F32)


def _per_head_cols(v, base, n_heads, shape):
    lane = lax.broadcasted_iota(jnp.int32, shape, 1)
    out = jnp.broadcast_to(v[:, base + n_heads - 1:base + n_heads], shape)
    for h in range(n_heads - 2, -1, -1):
        out = jnp.where(lane < (h + 1) * HEAD_DIM, v[:, base + h:base + h + 1], out)
    return out


def _ada_kernel(c_ref, w_ref, b_ref, o_ref):
    c = c_ref[...]
    o_ref[0] = _dot_hi(_silu(c), w_ref[0]) + b_ref[0]


def _ada(cc, ada_w, ada_b):
    depth, d, n = ada_w.shape
    tn = 1536
    return pl.pallas_call(
        _ada_kernel,
        out_shape=jax.ShapeDtypeStruct((depth, SUBLANES, n), F32),
        grid=(depth, n // tn),
        in_specs=[pl.BlockSpec((SUBLANES, d), lambda l, j: (0, 0)),
                  pl.BlockSpec((1, d, tn), lambda l, j: (l, 0, j)),
                  pl.BlockSpec((1, 1, tn), lambda l, j: (l, 0, j))],
        out_specs=pl.BlockSpec((1, SUBLANES, tn), lambda l, j: (l, 0, j)),
        compiler_params=_cp(("parallel", "parallel")),
        name="ada_mod",
    )(cc, ada_w, ada_b.reshape(depth, 1, n))


_C_XBC = 0
_C_U = _C_XBC + SSD_XBC
_C_Z = _C_U + GROUP_W
_C_DT = _C_Z + GROUP_W
_C_GAQ = _C_DT + LANES
_C_WAQ = _C_GAQ + N_HEADS * LANES
_C_GAK = _C_WAQ + N_HEADS * LANES
_C_GAV = _C_GAK + LANES
_C_WAK = _C_GAV + LANES
_C_WAV = _C_WAK + LANES
_C_END = _C_WAV + LANES


def _expand_q_cols(wq):
    zero = jnp.zeros((wq.shape[0], HEAD_DIM), wq.dtype)
    parts = []
    for h in range(N_HEADS):
        head = wq[:, h * HEAD_DIM:(h + 1) * HEAD_DIM]
        parts += [head, zero] if h // 2 == 0 else [zero, head]
    return jnp.concatenate(parts, axis=1)


def _pack_w_in(w_in):
    cuts = np.cumsum([256, 256, 128, 128, 256, SSD_XBC, 2 * SSD_HEADS, 256, 128, 128])[:-1]
    u, gaq, gak, gav, z, xbc, dt, waq, wak, wav = jnp.split(w_in, [int(c) for c in cuts], axis=1)
    dt = jnp.pad(dt, ((0, 0), (0, LANES - dt.shape[1])))
    w = jnp.concatenate([xbc, u, z, dt, _expand_q_cols(gaq), _expand_q_cols(waq), gak, gav, wak, wav], axis=1)
    return w.astype(BF16)


def _rope(x, cos, sins):
    w = x.shape[1]
    if w > LANES:
        cos = jnp.concatenate([cos] * (w // LANES), axis=1)
        sins = jnp.concatenate([sins] * (w // LANES), axis=1)
    lane = lax.broadcasted_iota(jnp.int32, x.shape, 1)
    up = pltpu.roll(x, w - ROPE_FREQS, 1)
    dn = pltpu.roll(x, ROPE_FREQS, 1)
    partner = jnp.where((lane & ROPE_FREQS) == 0, up, dn)
    return x * cos + partner * sins


def _inproj_kernel(x_ref, mod_ref, g_ref, w_ref, cos_ref, sin_ref, qn_ref, kn_ref,
                   xbc_o, ug_o, z_o, dt_o, gaq_o, gak_o, gav_o, waq_o, wak_o, wav_o, u_scr):
    x = x_ref[...]
    ms = jnp.mean(x * x, axis=-1, keepdims=True)
    xn = x * lax.rsqrt(ms + EPS) * g_ref[...]
    h = xn * (1.0 + mod_ref[0, 1:2, :]) + mod_ref[0, 0:1, :]
    hb = h.astype(BF16)

    def proj(lo, hi):
        return _dot(hb, w_ref[:, lo:hi])

    cos = cos_ref[...]
    sins = sin_ref[...]
    scale = LOG2E * HEAD_DIM ** -0.5
    q = proj(_C_GAQ, _C_WAQ)
    qs = q * q
    inv = jnp.concatenate(
        [jnp.broadcast_to(lax.rsqrt(jnp.sum(qs[:, s * LANES:(s + 1) * LANES], axis=1, keepdims=True)
                                    * (1.0 / HEAD_DIM) + EPS), (q.shape[0], LANES)) for s in range(N_HEADS)], axis=1)
    gaq_o[...] = (_rope(q * inv * qn_ref[...], cos, sins) * scale).astype(BF16)
    waq_o[...] = (_rope(proj(_C_WAQ, _C_GAK), cos, sins) * scale).astype(BF16)
    k = proj(_C_GAK, _C_GAV)
    ks = k * k
    lane = lax.broadcasted_iota(jnp.int32, k.shape, 1)
    lo = lane < HEAD_DIM
    ms0 = jnp.sum(jnp.where(lo, ks, 0.0), axis=1, keepdims=True)
    ms1 = jnp.sum(jnp.where(lo, 0.0, ks), axis=1, keepdims=True)
    kinv = lax.rsqrt(jnp.where(lo, ms0, ms1) * (1.0 / HEAD_DIM) + EPS)
    gak_o[...] = _rope(k * kinv * kn_ref[...], cos, sins).astype(BF16)
    gav_o[...] = proj(_C_GAV, _C_WAK).astype(BF16)
    wak_o[...] = _rope(proj(_C_WAK, _C_WAV), cos, sins).astype(BF16)
    wav_o[...] = proj(_C_WAV, _C_END).astype(BF16)
    xbc_o[...] = proj(_C_XBC, _C_U)
    u = proj(_C_U, _C_Z)
    u_scr[0] = u[:, :LANES]
    u_scr[1] = u[:, LANES:]
    _s5_pack_kernel(u_scr.at[0], u_scr.at[1], ug_o)
    z_o[...] = proj(_C_Z, _C_DT)
    dt_o[...] = proj(_C_DT, _C_GAQ)


def _mod_row(i, nblk, nb):
    return jnp.where(i % nblk == 0, nb, i // nblk)


def _inproj(x, mod, norm_g, w_packed, cos_t, sin_t, qn_g, kn_g, nb, nblk):
    t, d = x.shape
    row = lambda i: (i, 0)
    fix = lambda i: (0, 0)
    outs = [(SSD_XBC, F32), None, (GROUP_W, F32), (LANES, F32),
            (N_HEADS * LANES, BF16), (LANES, BF16), (LANES, BF16),
            (N_HEADS * LANES, BF16), (LANES, BF16), (LANES, BF16)]
    shapes = [jax.ShapeDtypeStruct((t, o[0]), o[1]) if o else
              jax.ShapeDtypeStruct((S5_GROUPS, t // S5_Q, S5_BLK), F32) for o in outs]
    specs = [pl.BlockSpec((TM, o[0]), row) if o else
             pl.BlockSpec((S5_GROUPS, S5_TB, S5_BLK), lambda i: (0, i, 0)) for o in outs]
    return pl.pallas_call(
        _inproj_kernel,
        out_shape=shapes,
        grid=(t // TM,),
        in_specs=[pl.BlockSpec((TM, d), row),
                  pl.BlockSpec((1, 6, d), lambda i: (_mod_row(i, nblk, nb), 0, 0)),
                  pl.BlockSpec((1, d), fix),
                  pl.BlockSpec((d, _C_END), fix),
                  pl.BlockSpec((TM, LANES), lambda i: (i % nblk, 0)),
                  pl.BlockSpec((TM, LANES), lambda i: (i % nblk, 0)),
                  pl.BlockSpec((1, N_HEADS * LANES), fix),
                  pl.BlockSpec((1, LANES), fix)],
        out_specs=specs,
        scratch_shapes=[pltpu.VMEM((GROUP_W // LANES, TM, LANES), F32)],
        compiler_params=_cp(("parallel",)),
        name="in_proj",
    )(x, mod, norm_g.reshape(1, d), w_packed, cos_t, sin_t,
      jnp.tile(qn_g, 2 * N_HEADS).reshape(1, -1), jnp.tile(kn_g, 2).reshape(1, -1))


def _rope_tables(lc, l):
    n_rows = l // GRID_W
    rows = np.repeat(np.arange(n_rows), GRID_W)
    cols = np.tile(np.arange(GRID_W), n_rows)
    inv = np.power(np.float32(ROPE_BASE), -np.arange(ROPE_FREQS, dtype=np.float32) / ROPE_FREQS)
    ang = np.stack([rows, cols], axis=-1).astype(np.float32)[..., None] * inv
    cos = np.cos(ang)
    sin = np.sin(ang)
    cos64 = np.stack([cos, cos], axis=2).reshape(l, HEAD_DIM)
    sin64 = np.stack([-sin, sin], axis=2).reshape(l, HEAD_DIM)
    cos64 = np.concatenate([np.ones((lc, HEAD_DIM), np.float32), cos64], axis=0)
    sin64 = np.concatenate([np.zeros((lc, HEAD_DIM), np.float32), sin64], axis=0)
    return (jnp.asarray(np.tile(cos64, (1, 2)), dtype=F32), jnp.asarray(np.tile(sin64, (1, 2)), dtype=F32))


def _merge_heads(o2, kvh):
    tq = o2.shape[0] // 2
    oa, ob = o2[:tq], o2[tq:]
    lane = lax.broadcasted_iota(jnp.int32, oa.shape, 1)
    if kvh == 0:
        return jnp.where(lane < HEAD_DIM, oa, pltpu.roll(ob, HEAD_DIM, 1))
    return jnp.where(lane < HEAD_DIM, pltpu.roll(oa, HEAD_DIM, 1), ob)


def _stack_q(q_ref, kvh):
    return jnp.concatenate([q_ref[:, (2 * kvh) * LANES:(2 * kvh + 1) * LANES],
                            q_ref[:, (2 * kvh + 1) * LANES:(2 * kvh + 2) * LANES]], axis=0)


def _ga_kernel(q_ref, k_ref, v_ref, o_ref, *, lc):
    j = pl.program_id(1)

    def attend(nkeys):
        k = k_ref[0:nkeys, :]
        v = v_ref[0:nkeys, :]
        scores = [_dot_nt(_stack_q(q_ref, kvh), k) for kvh in range(2)]
        lane = lax.broadcasted_iota(jnp.int32, v.shape, 1)
        outs = []
        for kvh in range(2):
            s = scores[kvh]
            p = jnp.exp2((s - jnp.max(s, axis=1, keepdims=True)).astype(BF16))
            own = (lane < HEAD_DIM) if kvh == 0 else (lane >= HEAD_DIM)
            o2 = _dot(p, jnp.where(own, v, jnp.ones_like(v)))
            outs.append(_merge_heads(o2 / pltpu.roll(o2, HEAD_DIM, 1), kvh))
        o_ref[...] = jnp.concatenate(outs, axis=1)

    @pl.when(j < lc // GA_TQ)
    def _():
        attend(lc)

    @pl.when(j >= lc // GA_TQ)
    def _():
        attend(k_ref.shape[0])


def _ga(q, k, v, nb, s_len, lc):
    t = q.shape[0]
    nq = s_len // GA_TQ
    return pl.pallas_call(
        functools.partial(_ga_kernel, lc=lc),
        out_shape=jax.ShapeDtypeStruct((t, GROUP_W), F32),
        grid=(nb, nq),
        in_specs=[pl.BlockSpec((GA_TQ, N_HEADS * LANES), lambda b, j: (b * nq + j, 0)),
                  pl.BlockSpec((s_len, LANES), lambda b, j: (b, 0)),
                  pl.BlockSpec((s_len, LANES), lambda b, j: (b, 0))],
        out_specs=pl.BlockSpec((GA_TQ, GROUP_W), lambda b, j: (b * nq + j, 0)),
        compiler_params=_cp(("parallel", "arbitrary")),
        name="global_attn",
    )(q, k, v)


WA_SUB = TM // TQ


def _wa_kernel(sink_ref, q_ref, k_ref, v_ref, o_ref, *, lc):
    s_len = k_ref.shape[0]
    kc = k_ref[0:lc, :]
    vc = v_ref[0:lc, :]
    lane_c = lax.broadcasted_iota(jnp.int32, vc.shape, 1)
    lane_b = lax.broadcasted_iota(jnp.int32, (3 * TQ, LANES), 1)
    row = lax.broadcasted_iota(jnp.int32, (2 * TQ, 1), 0)
    for sub in range(WA_SUB):
        rows = slice(sub * TQ, (sub + 1) * TQ)
        n = pl.program_id(1) * WA_SUB + sub - lc // TQ
        start = pl.multiple_of(jnp.clip(lc + (n - 1) * TQ, lc, s_len - 3 * TQ), TQ)
        kb = k_ref[pl.ds(start, 3 * TQ), :]
        vb = v_ref[pl.ds(start, 3 * TQ), :]
        qpos = n * TQ + lax.broadcasted_iota(jnp.int32, (TQ, 3 * TQ), 0)
        kpos = (start - lc) + lax.broadcasted_iota(jnp.int32, (TQ, 3 * TQ), 1)
        reach = jnp.where(n >= 0, WINDOW, -1)
        valid = jnp.abs(qpos - kpos) <= reach
        valid = jnp.concatenate([valid, valid], axis=0)
        outs = []
        for kvh in range(2):
            q2 = jnp.concatenate([q_ref[rows, (2 * kvh) * LANES:(2 * kvh + 1) * LANES],
                                  q_ref[rows, (2 * kvh + 1) * LANES:(2 * kvh + 2) * LANES]], axis=0)
            sc = _dot_nt(q2, kc)
            sb = jnp.where(valid, _dot_nt(q2, kb), NEG_INF)
            sink = jnp.where(row < TQ, sink_ref[2 * kvh], sink_ref[2 * kvh + 1]) * LOG2E
            m = jnp.maximum(jnp.maximum(jnp.max(sc, axis=1, keepdims=True), jnp.max(sb, axis=1, keepdims=True)), sink)
            pc = jnp.exp2((sc - m).astype(BF16))
            pb = jnp.exp2((sb - m).astype(BF16))
            own_c = (lane_c < HEAD_DIM) if kvh == 0 else (lane_c >= HEAD_DIM)
            own_b = (lane_b < HEAD_DIM) if kvh == 0 else (lane_b >= HEAD_DIM)
            o2 = _dot(pc, jnp.where(own_c, vc, jnp.ones_like(vc))) + _dot(pb, jnp.where(own_b, vb, jnp.ones_like(vb)))
            denom = pltpu.roll(o2, HEAD_DIM, 1) + jnp.exp2(sink - m)
            outs.append(_merge_heads(o2 / denom, kvh))
        o_ref[rows, :] = jnp.concatenate(outs, axis=1)


def _wa(sink, q, k, v, nb, s_len, lc):
    t = q.shape[0]
    nq = s_len // TM
    return pl.pallas_call(
        functools.partial(_wa_kernel, lc=lc),
        out_shape=jax.ShapeDtypeStruct((t, GROUP_W), F32),
        grid=(nb, nq),
        in_specs=[pl.BlockSpec(memory_space=pltpu.SMEM),
                  pl.BlockSpec((TM, N_HEADS * LANES), lambda b, j: (b * nq + j, 0)),
                  pl.BlockSpec((s_len, LANES), lambda b, j: (b, 0)),
                  pl.BlockSpec((s_len, LANES), lambda b, j: (b, 0))],
        out_specs=pl.BlockSpec((TM, GROUP_W), lambda b, j: (b * nq + j, 0)),
        compiler_params=_cp(("parallel", "arbitrary")),
        name="window_attn",
    )(sink, q, k, v)


def _s5_chunk_index(t, rev, nc_ctx, nc_tot):
    if not rev:
        return t
    return jnp.where(t < nc_ctx, nc_ctx - 1 - t, nc_tot - 1 - (t - nc_ctx))


def _s5_kernel(u_ref, m_ref, p_ref, g_ref, ar_ref, ai_ref, dsk_ref, y_ref, s_scr, h_scr, *, nb, nc_ctx, nc_tot):
    uf = u_ref[0]
    u = uf.astype(BF16)
    for d in range(2):
        for k in range(2):
            s_scr[d, k] = _dot(u, p_ref[d, k, 0])
    ar = [jnp.broadcast_to(ar_ref[d, 0], (nb, LANES)) for d in range(2)]
    ai = [[jnp.broadcast_to(ai_ref[d, k, 0], (nb, LANES)) for k in range(2)] for d in range(2)]

    def body(t, carry):
        out = []
        for d in range(2):
            h, hs = carry[d]
            rows = pl.ds(_s5_chunk_index(t, d == 1, nc_ctx, nc_tot), nb, stride=nc_tot)
            h_scr[d, rows, :] = h
            out.append((ar[d] * h + ai[d][0] * hs + s_scr[d, 0, rows, :],
                        ar[d] * hs + ai[d][1] * h + s_scr[d, 1, rows, :]))
        return tuple(out)

    zero = jnp.zeros((nb, LANES), F32)
    lax.fori_loop(0, nc_tot, body, ((zero, zero), (zero, zero)), unroll=2)
    y = uf * dsk_ref[0]
    for d in range(2):
        y = y + _dot(u, m_ref[d, 0]) + _dot(h_scr[d].astype(BF16), g_ref[d, 0])
    y_ref[0] = y


S5_TB = TM // S5_Q
S5_GPS = LANES // S5_CH


def _s5_pack_kernel(lo_ref, hi_ref, o_ref):
    for s in range(S5_Q):
        rows = pl.ds(s, S5_TB, stride=S5_Q)
        halves = (lo_ref[rows, :], hi_ref[rows, :])
        dst = S5_CH * (s % S5_GPS)
        for g in range(S5_GROUPS):
            slab = halves[g // S5_GPS]
            src = S5_CH * (g % S5_GPS)
            moved = slab if src == dst else pltpu.roll(slab, (dst - src) % LANES, 1)
            o_ref[g, :, s * S5_CH:(s + 1) * S5_CH] = moved[:, dst:dst + S5_CH]


def _s5_unpack_kernel(y_ref, o_ref):
    lane_grp = lax.broadcasted_iota(jnp.int32, (S5_TB, LANES), 1) // S5_CH
    for s in range(S5_Q):
        src = S5_CH * (s % S5_GPS)
        for half in range(S5_GROUPS // S5_GPS):
            acc = None
            for gl in range(S5_GPS):
                slab = y_ref[half * S5_GPS + gl, :, (s // S5_GPS) * LANES:(s // S5_GPS + 1) * LANES]
                dst = S5_CH * gl
                moved = slab if src == dst else pltpu.roll(slab, (dst - src) % LANES, 1)
                acc = moved if acc is None else jnp.where(lane_grp == gl, moved, acc)
            o_ref[half, pl.ds(s, S5_TB, stride=S5_Q), :] = acc


def _s5_params(lam_re, lam_im, log_dt, b_re, b_im, c_re, c_im, d_skip):
    q = S5_Q
    dt = jnp.exp(log_dt)[..., None]
    lr, li = lam_re, lam_im
    mag = jnp.exp(lr * dt)
    a_re = mag * jnp.cos(li * dt)
    a_im = mag * jnp.sin(li * dt)
    den = lr * lr + li * li
    f_re = ((a_re - 1.0) * lr + a_im * li) / den
    f_im = (a_im * lr - (a_re - 1.0) * li) / den
    bb_re = f_re[..., None] * b_re - f_im[..., None] * b_im
    bb_im = f_re[..., None] * b_im + f_im[..., None] * b_re
    kk = jnp.arange(q + 1, dtype=F32)[:, None, None, None]
    pmag = jnp.exp(kk * (lr * dt))
    pw_re = pmag * jnp.cos(kk * (li * dt))
    pw_im = pmag * jnp.sin(kk * (li * dt))
    e_re = c_re[None] * pw_re[:, :, :, None, :] - c_im[None] * pw_im[:, :, :, None, :]
    e_im = c_re[None] * pw_im[:, :, :, None, :] + c_im[None] * pw_re[:, :, :, None, :]
    kern = (jnp.einsum("kdgop,dgpc->kdgoc", e_re, bb_re, precision=HI)
            - jnp.einsum("kdgop,dgpc->kdgoc", e_im, bb_im, precision=HI))
    kern_t = kern[:q].transpose(1, 2, 4, 0, 3)
    zeros = jnp.zeros_like(kern_t)
    bbt_re = bb_re.transpose(0, 1, 3, 2)[:, :, None]
    bbt_im = bb_im.transpose(0, 1, 3, 2)[:, :, None]
    ct_re = c_re.transpose(0, 1, 3, 2)[:, :, :, None, :]
    ct_im = c_im.transpose(0, 1, 3, 2)[:, :, :, None, :]
    ms, ps, gs = [], [], []
    for d in range(2):
        ext = (jnp.concatenate([zeros[d], kern_t[d]], axis=2) if d == 0
               else jnp.concatenate([kern_t[d, :, :, ::-1], zeros[d]], axis=2))
        ext = ext.reshape(S5_GROUPS, S5_CH, 2 * S5_BLK)
        lo = [(q - s) if d == 0 else (q - 1 - s) for s in range(q)]
        md = jnp.stack([ext[:, :, a * S5_CH:a * S5_CH + S5_BLK] for a in lo], axis=1)
        ms.append(md.reshape(S5_GROUPS, S5_BLK, S5_BLK))
        pidx = (q - 1 - jnp.arange(q)) if d == 0 else jnp.arange(q)
        pr = pw_re[pidx, d].transpose(1, 0, 2)[:, :, None, :]
        pi = pw_im[pidx, d].transpose(1, 0, 2)[:, :, None, :]
        p_re = pr * bbt_re[d] - pi * bbt_im[d]
        p_im = pr * bbt_im[d] + pi * bbt_re[d]
        pd = jnp.stack([jnp.concatenate([p_re, p_im], axis=3), jnp.concatenate([p_im, p_re], axis=3)])
        ps.append(pd.reshape(2, S5_GROUPS, S5_BLK, 2 * S5_STATE))
        gidx = (jnp.arange(q) + 1) if d == 0 else (q - jnp.arange(q))
        gw_re = pw_re[gidx, d].transpose(1, 2, 0)[..., None]
        gw_im = pw_im[gidx, d].transpose(1, 2, 0)[..., None]
        g_re = ct_re[d] * gw_re - ct_im[d] * gw_im
        g_im = ct_re[d] * gw_im + ct_im[d] * gw_re
        gs.append(jnp.concatenate([g_re, -g_im], axis=1).reshape(S5_GROUPS, 2 * S5_STATE, S5_BLK))
    ar = jnp.concatenate([pw_re[q], pw_re[q]], axis=-1)[:, :, None, :]
    ai = jnp.stack([jnp.concatenate([-pw_im[q], pw_im[q]], axis=-1),
                    jnp.concatenate([pw_im[q], -pw_im[q]], axis=-1)], axis=1)[:, :, :, None, :]
    dsk = jnp.tile(d_skip.reshape(S5_GROUPS, 1, S5_CH), (1, 1, q))
    return (jnp.stack(ms).astype(BF16), jnp.stack(ps).astype(BF16), jnp.stack(gs).astype(BF16),
            ar.astype(F32), ai.astype(F32), dsk.astype(F32))


def _s5(ug, params, nb, s_len, lc):
    m, p, g, ar, ai, dsk = params
    nc_tot = s_len // S5_Q
    nc_ctx = lc // S5_Q
    r = nb * nc_tot
    return pl.pallas_call(
        functools.partial(_s5_kernel, nb=nb, nc_ctx=nc_ctx, nc_tot=nc_tot),
        out_shape=jax.ShapeDtypeStruct((S5_GROUPS, r, S5_BLK), F32),
        grid=(S5_GROUPS,),
        in_specs=[pl.BlockSpec((1, r, S5_BLK), lambda gi: (gi, 0, 0)),
                  pl.BlockSpec((2, 1, S5_BLK, S5_BLK), lambda gi: (0, gi, 0, 0)),
                  pl.BlockSpec((2, 2, 1, S5_BLK, 2 * S5_STATE), lambda gi: (0, 0, gi, 0, 0)),
                  pl.BlockSpec((2, 1, 2 * S5_STATE, S5_BLK), lambda gi: (0, gi, 0, 0)),
                  pl.BlockSpec((2, 1, 1, 2 * S5_STATE), lambda gi: (0, gi, 0, 0)),
                  pl.BlockSpec((2, 2, 1, 1, 2 * S5_STATE), lambda gi: (0, 0, gi, 0, 0)),
                  pl.BlockSpec((1, 1, S5_BLK), lambda gi: (gi, 0, 0))],
        out_specs=pl.BlockSpec((1, r, S5_BLK), lambda gi: (gi, 0, 0)),
        scratch_shapes=[pltpu.VMEM((2, 2, r, 2 * S5_STATE), F32), pltpu.VMEM((2, r, 2 * S5_STATE), F32)],
        compiler_params=_cp(("parallel",)),
        name="s5_scan",
    )(ug, m, p, g, ar, ai, dsk)


def _conv_kernel(x_ref, prev_ref, next_ref, w_ref, b_ref, o_ref, *, nblk):
    i = pl.program_id(0) % nblk
    x = x_ref[...]
    rows = x.shape[0]
    ridx = lax.broadcasted_iota(jnp.int32, x.shape, 0)
    prev_row = jnp.where(i <= 1, 0.0, prev_ref[SUBLANES - 1:SUBLANES, :])
    next_row = jnp.where(jnp.logical_or(i == 0, i == nblk - 1), 0.0, next_ref[0:1, :])
    xm = jnp.where(ridx == 0, prev_row, pltpu.roll(x, 1, 0))
    xp = jnp.where(ridx == rows - 1, next_row, pltpu.roll(x, rows - 1, 0))
    y = xm * w_ref[0:1, :] + x * w_ref[1:2, :] + xp * w_ref[2:3, :] + b_ref[...]
    o_ref[...] = _silu(y)


def _conv(xbc, w, b, nblk):
    t, c = xbc.shape
    per = TM // SUBLANES
    last = t // SUBLANES - 1
    return pl.pallas_call(
        functools.partial(_conv_kernel, nblk=nblk),
        out_shape=jax.ShapeDtypeStruct((t, c), F32),
        grid=(t // TM,),
        in_specs=[pl.BlockSpec((TM, c), lambda i: (i, 0)),
                  pl.BlockSpec((SUBLANES, c), lambda i: (jnp.maximum(i * per - 1, 0), 0)),
                  pl.BlockSpec((SUBLANES, c), lambda i: (jnp.minimum((i + 1) * per, last), 0)),
                  pl.BlockSpec((3, c), lambda i: (0, 0)),
                  pl.BlockSpec((1, c), lambda i: (0, 0))],
        out_specs=pl.BlockSpec((TM, c), lambda i: (i, 0)),
        compiler_params=_cp(("parallel",)),
        name="ssd_conv",
    )(xbc, xbc, xbc, w, b.reshape(1, c))


_X_B = GROUP_W
_X_C = GROUP_W + SSD_NGROUPS * SSD_STATE


def _ssd_kernel(xf_ref, dtf_ref, dttf_ref, xr_ref, dtr_ref, dttr_ref, bias_ref, a_ref, biast_ref, at_ref, dsk_ref,
                yf_ref, yr_ref, stf_ref, str_ref):
    @pl.when(pl.program_id(1) == 0)
    def _():
        stf_ref[...] = jnp.zeros_like(stf_ref)
        str_ref[...] = jnp.zeros_like(str_ref)

    par = (bias_ref[...], a_ref[...], biast_ref[...], at_ref[...], dsk_ref[...])
    for j in range(SSD_SUB):
        rf = slice(j * TQ, (j + 1) * TQ)
        yf_ref[rf, :] = _ssd_chunk_step(xf_ref[rf, :], dtf_ref[rf, :], dttf_ref[0, :, rf], par, stf_ref, False)
        rr = slice((SSD_SUB - 1 - j) * TQ, (SSD_SUB - j) * TQ)
        yr_ref[rr, :] = _ssd_chunk_step(xr_ref[rr, :], dtr_ref[rr, :], dttr_ref[0, :, rr], par, str_ref, True)


def _ssd_chunk_step(xc, dt_raw, dtt_raw, par, st_ref, rev):
    bias, a_vec, biast, at_vec, dsk = par
    base = SSD_HEADS if rev else 0
    x = xc[:, 0:GROUP_W]
    dt = _softplus(dt_raw + bias)
    a = dt * a_vec
    dtt = _softplus(dtt_raw + biast)
    at = dtt * at_vec
    ri = lax.broadcasted_iota(jnp.int32, (TQ, TQ), 0)
    ci = lax.broadcasted_iota(jnp.int32, (TQ, TQ), 1)
    causal = (ci >= ri) if rev else (ri >= ci)
    tri = jnp.where(causal, 1.0, 0.0)
    cum_c = _dot_hi(tri, a)
    cum_r = _dot_nt_hi(at, tri)
    edge = 0 if rev else TQ - 1
    tot = cum_c[edge:edge + 1, :]

    shape = (TQ, GROUP_W)
    xdt = x * _per_head_cols(dt, base, SSD_HEADS, shape)
    lane = lax.broadcasted_iota(jnp.int32, shape, 1)
    y = jnp.zeros(shape, F32)
    bmat = [xc[:, _X_B + g * SSD_STATE:_X_B + (g + 1) * SSD_STATE].astype(BF16) for g in range(SSD_NGROUPS)]
    cmat = [xc[:, _X_C + g * SSD_STATE:_X_C + (g + 1) * SSD_STATE].astype(BF16) for g in range(SSD_NGROUPS)]
    cb = [_dot_nt(cmat[g], bmat[g]) for g in range(SSD_NGROUPS)]
    for h in range(SSD_HEADS):
        col = base + h
        seg = jnp.where(causal, cum_c[:, col:col + 1] - cum_r[col:col + 1, :], NEG_INF)
        scores = cb[h // 2] * jnp.exp(seg)
        xh = jnp.where((lane >= h * HEAD_DIM) & (lane < (h + 1) * HEAD_DIM), xdt, 0.0)
        y = y + _dot(scores.astype(BF16), xh.astype(BF16))
    st = st_ref[...]
    yo = jnp.concatenate(
        [_dot_nt(cmat[g], st[g * SSD_STATE:(g + 1) * SSD_STATE].astype(BF16)) for g in range(SSD_NGROUPS)], axis=1)
    y = y + yo * _per_head_cols(jnp.exp(cum_c), base, SSD_HEADS, shape)
    if not rev:
        y = y + x * dsk
    xd = xdt * _per_head_cols(jnp.exp(tot - cum_c), base, SSD_HEADS, shape)
    xdt_t = xd.T.astype(BF16)
    decay = jnp.exp(tot)
    for g in range(SSD_NGROUPS):
        new = _dot(xdt_t[g * SSD_STATE:(g + 1) * SSD_STATE], bmat[g])
        for hh in range(2):
            h = 2 * g + hh
            r0 = h * HEAD_DIM
            st_ref[r0:r0 + HEAD_DIM, :] = (decay[:, base + h:base + h + 1] * st[r0:r0 + HEAD_DIM]
                                           + new[hh * HEAD_DIM:(hh + 1) * HEAD_DIM])
    return y


def _dot_nt_hi(a, b):
    return lax.dot_general(a, b, (((1,), (1,)), ((), ())), preferred_element_type=F32, precision=HI)


def _ssd_chunk(c, rev, nc_ctx, nc_tot):
    if not rev:
        return c
    return jnp.where(c < nc_ctx, nc_ctx - 1 - c, nc_tot - 1 - (c - nc_ctx))


SSD_SUB = TM // TQ


def _ssd_scan(xc, dt, dtt, bias, a, biast, at, dsk, nb, s_len, lc):
    t = xc.shape[0]
    nblk = s_len // TM
    nctx = lc // TM
    fix = lambda b, c: (0, 0)

    def rows(rev):
        return lambda b, c: (b * nblk + _ssd_chunk(c, rev, nctx, nblk), 0)

    def lanes(rev):
        return lambda b, c: (b, 0, _ssd_chunk(c, rev, nctx, nblk))

    def data_specs(rev):
        return [pl.BlockSpec((TM, SSD_XBC), rows(rev)), pl.BlockSpec((TM, LANES), rows(rev)),
                pl.BlockSpec((1, SUBLANES, TM), lanes(rev))]

    state = pltpu.VMEM((SSD_HEADS * HEAD_DIM, SSD_STATE), F32)
    return pl.pallas_call(
        _ssd_kernel,
        out_shape=[jax.ShapeDtypeStruct((t, GROUP_W), F32)] * 2,
        grid=(nb, nblk),
        in_specs=data_specs(False) + data_specs(True) + [
            pl.BlockSpec((1, LANES), fix), pl.BlockSpec((1, LANES), fix),
            pl.BlockSpec((SUBLANES, TQ), fix), pl.BlockSpec((SUBLANES, TQ), fix),
            pl.BlockSpec((1, GROUP_W), fix)],
        out_specs=[pl.BlockSpec((TM, GROUP_W), rows(False)), pl.BlockSpec((TM, GROUP_W), rows(True))],
        scratch_shapes=[state, state],
        compiler_params=_cp(("parallel", "arbitrary")),
        name="ssd_scan",
    )(xc, dt, dtt, xc, dt, dtt, bias, a, biast, at, dsk)


def _ssd(xbc, dt, conv_w, conv_b, dt_bias, a_log, d_skip, nb, s_len, lc):
    nblk = s_len // TM
    xc = _conv(xbc, conv_w, conv_b, nblk)
    nd = 2 * SSD_HEADS
    dtt = dt[:, :nd].reshape(nb, s_len, nd).transpose(0, 2, 1)
    bias = jnp.pad(dt_bias.reshape(1, nd), ((0, 0), (0, LANES - nd)))
    a = jnp.pad(-jnp.exp(a_log).reshape(1, nd), ((0, 0), (0, LANES - nd)))
    biast = jnp.broadcast_to(dt_bias.reshape(nd, 1), (nd, TQ))
    at = jnp.broadcast_to(-jnp.exp(a_log).reshape(nd, 1), (nd, TQ))
    dsk = jnp.repeat(d_skip, HEAD_DIM).reshape(1, GROUP_W)
    return _ssd_scan(xc, dt, dtt, bias, a, biast, at, dsk, nb, s_len, lc)


def _outproj_kernel(x_ref, ys5_ref, oga_ref, y0_ref, y1_ref, z_ref, owa_ref, mod_ref, gluw_ref, glub_ref,
                    ng_ref, wout_ref, n2_ref, wr_ref, br_ref, xn_o, h2_o, route_o, y_scr):
    _s5_unpack_kernel(ys5_ref, y_scr)
    gl = _gelu_tanh(jnp.concatenate([y_scr[0], y_scr[1]], axis=1))
    a = gl * _sigmoid(_dot(gl.astype(BF16), gluw_ref[...]) + glub_ref[...])
    m = (y0_ref[...] + y1_ref[...]) * _silu(z_ref[...])
    m = m * lax.rsqrt(jnp.mean(m * m, axis=-1, keepdims=True) + EPS) * ng_ref[...]
    w = wout_ref
    mix = (_dot(a.astype(BF16), w[0:GROUP_W, :]) + _dot(oga_ref[...].astype(BF16), w[GROUP_W:2 * GROUP_W, :])
           + _dot(m.astype(BF16), w[2 * GROUP_W:3 * GROUP_W, :]) + _dot(owa_ref[...].astype(BF16), w[3 * GROUP_W:, :]))
    xn = x_ref[...] + mod_ref[0, 2:3, :] * mix
    xn_o[...] = xn
    h2 = xn * lax.rsqrt(jnp.mean(xn * xn, axis=-1, keepdims=True) + EPS) * n2_ref[...]
    h2 = h2 * (1.0 + mod_ref[0, 4:5, :]) + mod_ref[0, 3:4, :]
    h2_o[...] = _pack_bf16_pair(h2)
    h_hi = h2.astype(BF16)
    h_lo = (h2 - h_hi.astype(F32)).astype(BF16)
    logits = _dot(h_hi, wr_ref[0]) + (_dot(h_lo, wr_ref[0]) + _dot(h_hi, wr_ref[1])) + br_ref[...]
    lane = lax.broadcasted_iota(jnp.int32, logits.shape, 1)
    big = 4 * LANES
    lcoarse = jnp.where(lane < MOE_GROUPS, logits, NEG_INF)
    mx = jnp.max(lcoarse, axis=1, keepdims=True)
    den = jnp.sum(jnp.exp(lcoarse - mx), axis=1, keepdims=True)
    grp = jnp.min(jnp.where(lcoarse == mx, lane, big), axis=1, keepdims=True)
    pg = 1.0 / den
    lo = ROUTE_FINE0 + grp * MOE_PER_GROUP
    lf = jnp.where(lane >= lo, jnp.where(lane < lo + MOE_PER_GROUP, logits, NEG_INF), NEG_INF)
    v1 = jnp.max(lf, axis=1, keepdims=True)
    i1 = jnp.min(jnp.where(lf == v1, lane, big), axis=1, keepdims=True)
    lf2 = jnp.where(lane == i1, NEG_INF, lf)
    v2 = jnp.max(lf2, axis=1, keepdims=True)
    i2 = jnp.min(jnp.where(lf2 == v2, lane, big), axis=1, keepdims=True)
    e2 = jnp.exp(v2 - v1)
    w1 = pg / (1.0 + e2)
    w2 = w1 * e2
    route = jnp.where(lane == 0, (i1 - ROUTE_FINE0).astype(F32),
                      jnp.where(lane == 1, (i2 - ROUTE_FINE0).astype(F32),
                                jnp.where(lane == 2, w1, jnp.where(lane == 3, w2, 0.0))))
    route_o[...] = route


def _outproj(x, ys5, oga, y0, y1, z, owa, mod, glu_w, glu_b, ssd_norm_g, w_out, norm2_g, wr, br, nb, nblk):
    t, d = x.shape
    row = lambda i: (i, 0)
    fix = lambda i: (0, 0)
    gw = pl.BlockSpec((TM, GROUP_W), row)
    wr_hi = wr.astype(BF16)
    return pl.pallas_call(
        _outproj_kernel,
        out_shape=[jax.ShapeDtypeStruct((t, d), F32), jax.ShapeDtypeStruct((t, d // 2), jnp.uint32),
                   jax.ShapeDtypeStruct((t, LANES), F32)],
        grid=(t // TM,),
        in_specs=[pl.BlockSpec((TM, d), row), pl.BlockSpec((S5_GROUPS, S5_TB, S5_BLK), lambda i: (0, i, 0)),
                  gw, gw, gw, gw, gw,
                  pl.BlockSpec((1, 6, d), lambda i: (_mod_row(i, nblk, nb), 0, 0)),
                  pl.BlockSpec((GROUP_W, GROUP_W), fix),
                  pl.BlockSpec((1, GROUP_W), fix),
                  pl.BlockSpec((1, GROUP_W), fix),
                  pl.BlockSpec((d, d), fix),
                  pl.BlockSpec((1, d), fix),
                  pl.BlockSpec((2, d, LANES), lambda i: (0, 0, 0)),
                  pl.BlockSpec((1, LANES), fix)],
        out_specs=[pl.BlockSpec((TM, d), row), pl.BlockSpec((TM, d // 2), row), pl.BlockSpec((TM, LANES), row)],
        scratch_shapes=[pltpu.VMEM((GROUP_W // LANES, TM, LANES), F32)],
        compiler_params=_cp(("parallel",)),
        name="out_proj_router",
    )(x, ys5, oga, y0, y1, z, owa, mod, glu_w.astype(BF16), glu_b.reshape(1, -1), ssd_norm_g.reshape(1, -1),
      w_out.astype(BF16), norm2_g.reshape(1, -1), jnp.stack([wr_hi, (wr - wr_hi.astype(F32)).astype(BF16)]), br)


def _pack_router(coarse_w, coarse_b, fine_w, fine_b):
    def lanes(coarse, fine):
        gap = jnp.zeros(coarse.shape[:-1] + (ROUTE_FINE0 - MOE_GROUPS,), F32)
        tail = jnp.zeros(coarse.shape[:-1] + (LANES - ROUTE_FINE0 - N_EXPERTS,), F32)
        return jnp.concatenate([coarse, gap, fine, tail], axis=-1)

    return lanes(coarse_w, fine_w), lanes(coarse_b[None, :], fine_b[None, :])


def _gather_rows(src, idx):
    m = idx.shape[0]
    d = src.shape[1]
    workers = SC_CORES * SC_SUBCORES
    nch = m // (workers * SC_GATHER_K)
    assert nch * workers * SC_GATHER_K == m
    mesh = plsc.VectorSubcoreMesh(core_axis_name="c", subcore_axis_name="s")

    @functools.partial(
        pl.kernel, mesh=mesh,
        out_type=jax.ShapeDtypeStruct((m, d), src.dtype),
        scratch_types=[pltpu.VMEM((nch, SC_GATHER_K), jnp.int32),
                       pltpu.VMEM((SC_GATHER_K, d), src.dtype),
                       pltpu.SemaphoreType.DMA],
    )
    def gather(src_hbm, idx_hbm, out_hbm, idx_v, rows_v, sem):
        wid = lax.axis_index("s") * SC_CORES + lax.axis_index("c")
        pltpu.sync_copy(idx_hbm.at[wid], idx_v)

        @pl.loop(0, nch)
        def _(j):
            off = pl.multiple_of((wid * nch + j) * SC_GATHER_K, SC_GATHER_K)
            pltpu.async_copy(src_hbm.at[idx_v.at[j]], rows_v, sem).wait()
            pltpu.sync_copy(rows_v, out_hbm.at[pl.ds(off, SC_GATHER_K)])

    return gather(src, idx.reshape(workers, nch, SC_GATHER_K))


def _scatter_rows(src, dst0, dst1, nrows):
    t, d = src.shape
    workers = SC_CORES * SC_SUBCORES
    nch = t // (workers * SC_GATHER_K)
    assert nch * workers * SC_GATHER_K == t
    mesh = plsc.VectorSubcoreMesh(core_axis_name="c", subcore_axis_name="s")

    @functools.partial(
        pl.kernel, mesh=mesh,
        out_type=jax.ShapeDtypeStruct((nrows, d), src.dtype),
        scratch_types=[pltpu.VMEM((nch, SC_GATHER_K), jnp.int32),
                       pltpu.VMEM((nch, SC_GATHER_K), jnp.int32),
                       pltpu.VMEM((SC_GATHER_K, d), src.dtype)],
    )
    def scatter(src_hbm, d0_hbm, d1_hbm, out_hbm, i0_v, i1_v, rows_v):
        wid = lax.axis_index("s") * SC_CORES + lax.axis_index("c")
        pltpu.sync_copy(d0_hbm.at[wid], i0_v)
        pltpu.sync_copy(d1_hbm.at[wid], i1_v)

        @pl.loop(0, nch)
        def _(j):
            off = pl.multiple_of((wid * nch + j) * SC_GATHER_K, SC_GATHER_K)
            pltpu.sync_copy(src_hbm.at[pl.ds(off, SC_GATHER_K)], rows_v)
            pltpu.sync_copy(rows_v, out_hbm.at[i0_v.at[j]])
            pltpu.sync_copy(rows_v, out_hbm.at[i1_v.at[j]])

    return scatter(src, dst0.reshape(workers, nch, SC_GATHER_K), dst1.reshape(workers, nch, SC_GATHER_K))


def _expert_kernel(be_ref, nused_ref, nvalid_ref, x_ref, wg_ref, wu_ref, wd_ref, o_ref, wg_s, wu_s, wd_s):
    i = pl.program_id(0)
    new_expert = jnp.logical_or(i == 0, be_ref[i] != be_ref[jnp.maximum(i - 1, 0)])

    @pl.when(jnp.logical_and(i < nused_ref[0], new_expert))
    def _():
        wg_s[...] = wg_ref[0, 0].astype(BF16)
        wu_s[...] = wu_ref[0, 0].astype(BF16)
        wd_s[...] = wd_ref[0, 0].astype(BF16)

    @pl.when(i < nused_ref[0])
    def _():
        row = lax.broadcasted_iota(jnp.int32, x_ref.shape, 0)
        lo, hi = _unpack_bf16_pair(jnp.where(row < nvalid_ref[i], x_ref[...], jnp.uint32(0)))
        lo = lo.astype(BF16)
        hi = hi.astype(BF16)
        half = lo.shape[1]
        gate = _dot(lo, wg_s[0:half, :]) + _dot(hi, wg_s[half:, :])
        up = _dot(lo, wu_s[0:half, :]) + _dot(hi, wu_s[half:, :])
        o_ref[...] = _pack_bf16_pair(_dot((_silu(gate) * up).astype(BF16), wd_s[...]))

    @pl.when(i >= nused_ref[0])
    def _():
        o_ref[...] = jnp.zeros_like(o_ref)


def _experts(xs, blk_e, n_used, n_valid, wg, wu, wd, layer):
    rows, dp = xs.shape
    d = 2 * dp
    nblocks = rows // MOE_TM
    de = wg.shape[3]
    wsel = lambda i, be, nu, nv: (layer, be[i], 0, 0)
    grid_spec = pltpu.PrefetchScalarGridSpec(
        num_scalar_prefetch=3,
        grid=(nblocks,),
        in_specs=[pl.BlockSpec((MOE_TM, dp), lambda i, be, nu, nv: (i, 0)),
                  pl.BlockSpec((1, 1, d, de), wsel),
                  pl.BlockSpec((1, 1, d, de), wsel),
                  pl.BlockSpec((1, 1, de, d), wsel)],
        out_specs=pl.BlockSpec((MOE_TM, dp), lambda i, be, nu, nv: (i, 0)),
        scratch_shapes=[pltpu.VMEM((d, de), BF16), pltpu.VMEM((d, de), BF16), pltpu.VMEM((de, d), BF16)],
    )
    return pl.pallas_call(
        _expert_kernel,
        out_shape=jax.ShapeDtypeStruct((rows, dp), jnp.uint32),
        grid_spec=grid_spec,
        compiler_params=_cp(("arbitrary",)),
        name="moe_experts",
    )(blk_e, n_used, n_valid, xs, wg, wu, wd)


def _combine_kernel(x_ref, r0_ref, r1_ref, route_ref, mod_ref, fg_ref, o_ref, *, final):
    route = route_ref[...]
    r0 = jnp.concatenate(_unpack_bf16_pair(r0_ref[...]), axis=1)
    r1 = jnp.concatenate(_unpack_bf16_pair(r1_ref[...]), axis=1)
    f = route[:, 2:3] * r0 + route[:, 3:4] * r1
    y = x_ref[...] + mod_ref[0, 5:6, :] * f
    if final:
        y = y * lax.rsqrt(jnp.mean(y * y, axis=-1, keepdims=True) + EPS) * fg_ref[...]
    o_ref[...] = y


def _combine(xn, rows2, route, mod, final_g, nb, nblk, final):
    t, d = xn.shape
    if final:
        nlat = nblk - 1
        grid = (nb * nlat,)
        src = lambda i: ((i // nlat) * nblk + 1 + i % nlat, 0)
        modi = lambda i: (i // nlat, 0, 0)
        out_rows = nb * nlat * TM
    else:
        grid = (t // TM,)
        src = lambda i: (i, 0)
        modi = lambda i: (_mod_row(i, nblk, nb), 0, 0)
        out_rows = t
    return pl.pallas_call(
        functools.partial(_combine_kernel, final=final),
        out_shape=jax.ShapeDtypeStruct((out_rows, d), F32),
        grid=grid,
        in_specs=[pl.BlockSpec((TM, d), src),
                  pl.BlockSpec((TM, d // 2), src),
                  pl.BlockSpec((TM, d // 2), lambda i: (src(i)[0] + t // TM, 0)),
                  pl.BlockSpec((TM, LANES), src),
                  pl.BlockSpec((1, 6, d), modi),
                  pl.BlockSpec((1, d), lambda i: (0, 0))],
        out_specs=pl.BlockSpec((TM, d), lambda i: (i, 0)),
        compiler_params=_cp(("parallel",)),
        name="moe_combine_final" if final else "moe_combine",
    )(xn, rows2, rows2, route, mod, final_g.reshape(1, d))


def _moe(h2, route, wg, wu, wd, layer):
    t, d = h2.shape
    n_slots = 2 * t
    experts = jnp.arange(N_EXPERTS, dtype=F32)[None, :]
    oh0 = (route[:, 0:1] == experts).astype(F32)
    oh1 = (route[:, 1:2] == experts).astype(F32)
    both = (oh0 + oh1).reshape(t // LANES, LANES, N_EXPERTS)
    tri = jnp.tril(jnp.ones((LANES, LANES), F32))
    intra = jnp.einsum("ij,bjk->bik", tri, both)
    blk_tot = intra[:, -1, :]
    blk_cum = jnp.cumsum(blk_tot, axis=0)
    earlier = (intra - both + (blk_cum - blk_tot)[:, None, :]).reshape(t, N_EXPERTS)
    counts = blk_cum[-1].astype(jnp.int32)
    pcounts = (counts + MOE_TM - 1) // MOE_TM * MOE_TM
    pends = jnp.cumsum(pcounts)
    pstarts = pends - pcounts
    base = pstarts.astype(F32)[None, :] + earlier
    dest0 = jnp.sum(oh0 * base, axis=1).astype(jnp.int32)
    dest1 = jnp.sum(oh1 * base, axis=1).astype(jnp.int32)
    nblocks = -(-n_slots // MOE_TM) + N_EXPERTS
    nrows = -(-nblocks * MOE_TM // GATHER_ROWS) * GATHER_ROWS
    nblocks = nrows // MOE_TM
    blk_start = jnp.arange(nblocks, dtype=jnp.int32) * MOE_TM
    blk_e = jnp.minimum(jnp.sum((pends[None, :] <= blk_start[:, None]).astype(jnp.int32), axis=1), N_EXPERTS - 1)
    n_used = (pends[-1] // MOE_TM).astype(jnp.int32).reshape(1)
    n_valid = jnp.clip((pstarts + counts)[blk_e] - blk_start, 0, MOE_TM).astype(jnp.int32)
    xs = _scatter_rows(h2, dest0, dest1, nrows)
    ys = _experts(xs, blk_e, n_used, n_valid, wg, wu, wd, layer)
    return _gather_rows(ys, jnp.concatenate([dest0, dest1]))


def kernel(x, c, ctx, c_ctx, ada_w, ada_b, norm1_g, norm2_g, w_in, w_out, s5_lam_re, s5_lam_im, s5_log_dt, s5_b_re, s5_b_im, s5_c_re, s5_c_im, s5_d, s5_glu_w, s5_glu_b, ga_qn_g, ga_kn_g, ssd_conv_w, ssd_conv_b, ssd_dt_bias, ssd_a_log, ssd_d, ssd_norm_g, wa_sink, moe_coarse_w, moe_coarse_b, moe_fine_w, moe_fine_b, moe_w_gate, moe_w_up, moe_w_down, final_g):
    nb, l, d = x.shape
    lc = ctx.shape[1]
    depth = ada_w.shape[0]
    assert lc == TM and l % TM == 0 and nb <= SUBLANES - 1 and d == D_MODEL
    s_len = lc + l
    nblk = s_len // TM
    t = nb * s_len

    xm = jnp.concatenate([ctx, x], axis=1).reshape(t, d)
    cc = jnp.zeros((SUBLANES, d), F32).at[:nb].set(c).at[nb].set(c_ctx)
    mods = _ada(cc, ada_w, ada_b).reshape(depth, SUBLANES, 6, d)
    cos_t, sin_t = _rope_tables(lc, l)
    w_packed = jax.vmap(_pack_w_in)(w_in)
    s5_tabs = jax.vmap(_s5_params)(s5_lam_re, s5_lam_im, s5_log_dt, s5_b_re, s5_b_im, s5_c_re, s5_c_im, s5_d)
    wrs, brs = jax.vmap(_pack_router)(moe_coarse_w, moe_coarse_b, moe_fine_w, moe_fine_b)

    out = None
    for i in range(depth):
        mod = mods[i]
        (xbc, ug, z, dt, gaq, gak, gav, waq, wak, wav) = _inproj(
            xm, mod, norm1_g[i], w_packed[i], cos_t, sin_t, ga_qn_g[i], ga_kn_g[i], nb, nblk)
        ys5 = _s5(ug, tuple(tab[i] for tab in s5_tabs), nb, s_len, lc)
        oga = _ga(gaq, gak, gav, nb, s_len, lc)
        y0, y1 = _ssd(xbc, dt, ssd_conv_w[i], ssd_conv_b[i], ssd_dt_bias[i], ssd_a_log[i], ssd_d[i], nb, s_len, lc)
        owa = _wa(wa_sink[i], waq, wak, wav, nb, s_len, lc)
        wr, br = wrs[i], brs[i]
        xn, h2, route = _outproj(xm, ys5, oga, y0, y1, z, owa, mod, s5_glu_w[i], s5_glu_b[i], ssd_norm_g[i],
                                 w_out[i], norm2_g[i], wr, br, nb, nblk)
        rows2 = _moe(h2, route, moe_w_gate, moe_w_up, moe_w_down, i)
        final = i == depth - 1
        xm = _combine(xn, rows2, route, mod, final_g, nb, nblk, final)
        if final:
            out = xm.reshape(nb, l, d)
    return out
```

```python
import functools
import math

import jax
import jax.numpy as jnp
import numpy as np
from jax import lax
from jax.experimental import pallas as pl
from jax.experimental.pallas import tpu as pltpu
from jax.experimental.pallas import tpu_sc as plsc

F32 = jnp.float32
BF16 = jnp.bfloat16
HI = lax.Precision.HIGHEST

D_MODEL = 1024
GRID_W = 64
GROUP_W = 256
HEAD_DIM = 64
ROPE_FREQS = HEAD_DIM // 4
ROPE_BASE = 10000.0
EPS = 1e-6
S5_CH = 16
S5_GROUPS = GROUP_W // S5_CH
S5_STATE = 64
N_HEADS = 4
SSD_HEADS = 4
SSD_NGROUPS = 2
SSD_STATE = 128
SSD_XBC = GROUP_W + 2 * SSD_NGROUPS * SSD_STATE
WINDOW = 128
MOE_GROUPS = 4
MOE_PER_GROUP = 8
N_EXPERTS = 32
D_EXPERT = D_MODEL // 2

LANES = 128
SUBLANES = 8
TM = 256
TQ = 128
GA_TQ = 128
GA_SUB = 2
S5_Q = 32
S5_BLK = S5_Q * S5_CH
MOE_TM = 512
SC_CORES = 2
SC_SUBCORES = 16
SC_GATHER_K = 32
GATHER_ROWS = SC_CORES * SC_SUBCORES * SC_GATHER_K
ROUTE_FINE0 = 32
VMEM_LIMIT = 56 * 1024 * 1024

NEG_INF = float("-inf")
LOG2E = math.log2(math.e)


def _cp(sem, vmem=VMEM_LIMIT):
    return pltpu.CompilerParams(dimension_semantics=sem, vmem_limit_bytes=vmem)


def _dot(a, b):
    return jnp.dot(a, b, preferred_element_type=F32)


def _dot_hi(a, b):
    return jnp.dot(a, b, preferred_element_type=F32, precision=HI)


def _dot_nt(a, b):
    return lax.dot_general(a, b, (((1,), (1,)), ((), ())), preferred_element_type=F32)


def _sigmoid(x):
    return 1.0 / (1.0 + jnp.exp(-x))


def _silu(x):
    return x * _sigmoid(x)


def _gelu_tanh(x):
    return 0.5 * x * (1.0 + jnp.tanh(math.sqrt(2.0 / math.pi) * (x + 0.044715 * (x * x * x))))


def _softplus(x):
    return jnp.maximum(x, 0.0) + jnp.log1p(jnp.exp(-jnp.abs(x)))


_HI16 = 0xFFFF0000


def _pack_bf16_pair(x):
    n = x.shape[1] // 2
    bits = pltpu.bitcast(x.astype(BF16).astype(F32), jnp.uint32)
    return (bits[:, n:] & jnp.uint32(_HI16)) | (bits[:, :n] >> 16)


def _unpack_bf16_pair(w):
    return pltpu.bitcast(w << 16, F32), pltpu.bitcast(w & jnp.uint32(_HI16), F32)


def _per_head_cols(v, base, n_heads, shape):
    lane = lax.broadcasted_iota(jnp.int32, shape, 1)
    out = jnp.broadcast_to(v[:, base + n_heads - 1:base + n_heads], shape)
    for h in range(n_heads - 2, -1, -1):
        out = jnp.where(lane < (h + 1) * HEAD_DIM, v[:, base + h:base + h + 1], out)
    return out


def _ada_kernel(c_ref, w_ref, b_ref, o_ref):
    c = c_ref[...]
    o_ref[0] = _dot_hi(_silu(c), w_ref[0]) + b_ref[0]


def _ada(cc, ada_w, ada_b):
    depth, d, n = ada_w.shape
    tn = 1536
    return pl.pallas_call(
        _ada_kernel,
        out_shape=jax.ShapeDtypeStruct((depth, SUBLANES, n), F32),
        grid=(depth, n // tn),
        in_specs=[pl.BlockSpec((SUBLANES, d), lambda l, j: (0, 0)),
                  pl.BlockSpec((1, d, tn), lambda l, j: (l, 0, j)),
                  pl.BlockSpec((1, 1, tn), lambda l, j: (l, 0, j))],
        out_specs=pl.BlockSpec((1, SUBLANES, tn), lambda l, j: (l, 0, j)),
        compiler_params=_cp(("parallel", "parallel")),
        name="ada_mod",
    )(cc, ada_w, ada_b.reshape(depth, 1, n))


_C_XBC = 0
_C_U = _C_XBC + SSD_XBC
_C_Z = _C_U + GROUP_W
_C_DT = _C_Z + GROUP_W
_C_GAQ = _C_DT + LANES
_C_WAQ = _C_GAQ + N_HEADS * LANES
_C_GAK = _C_WAQ + N_HEADS * LANES
_C_GAV = _C_GAK + LANES
_C_WAK = _C_GAV + LANES
_C_WAV = _C_WAK + LANES
_C_END = _C_WAV + LANES


def _expand_q_cols(wq):
    zero = jnp.zeros((wq.shape[0], HEAD_DIM), wq.dtype)
    parts = []
    for h in range(N_HEADS):
        head = wq[:, h * HEAD_DIM:(h + 1) * HEAD_DIM]
        parts += [head, zero] if h // 2 == 0 else [zero, head]
    return jnp.concatenate(parts, axis=1)


def _pack_w_in(w_in):
    cuts = np.cumsum([256, 256, 128, 128, 256, SSD_XBC, 2 * SSD_HEADS, 256, 128, 128])[:-1]
    u, gaq, gak, gav, z, xbc, dt, waq, wak, wav = jnp.split(w_in, [int(c) for c in cuts], axis=1)
    dt = jnp.pad(dt, ((0, 0), (0, LANES - dt.shape[1])))
    w = jnp.concatenate([xbc, u, z, dt, _expand_q_cols(gaq), _expand_q_cols(waq), gak, gav, wak, wav], axis=1)
    return w.astype(BF16)


def _rope(x, cos, sins):
    w = x.shape[1]
    if w > LANES:
        cos = jnp.concatenate([cos] * (w // LANES), axis=1)
        sins = jnp.concatenate([sins] * (w // LANES), axis=1)
    lane = lax.broadcasted_iota(jnp.int32, x.shape, 1)
    up = pltpu.roll(x, w - ROPE_FREQS, 1)
    dn = pltpu.roll(x, ROPE_FREQS, 1)
    partner = jnp.where((lane & ROPE_FREQS) == 0, up, dn)
    return x * cos + partner * sins


def _inproj_kernel(x_ref, mod_ref, g_ref, w_ref, cos_ref, sin_ref, qn_ref, kn_ref,
                   xbc_o, ug_o, z_o, dt_o, gaq_o, gak_o, gav_o, waq_o, wak_o, wav_o, u_scr):
    x = x_ref[...]
    ms = jnp.mean(x * x, axis=-1, keepdims=True)
    xn = x * lax.rsqrt(ms + EPS) * g_ref[...]
    h = xn * (1.0 + mod_ref[0, 1:2, :]) + mod_ref[0, 0:1, :]
    hb = h.astype(BF16)

    def proj(lo, hi):
        return _dot(hb, w_ref[:, lo:hi])

    cos = cos_ref[...]
    sins = sin_ref[...]
    scale = LOG2E * HEAD_DIM ** -0.5
    q = proj(_C_GAQ, _C_WAQ)
    qs = q * q
    inv = jnp.concatenate(
        [jnp.broadcast_to(lax.rsqrt(jnp.sum(qs[:, s * LANES:(s + 1) * LANES], axis=1, keepdims=True)
                                    * (1.0 / HEAD_DIM) + EPS), (q.shape[0], LANES)) for s in range(N_HEADS)], axis=1)
    gaq_o[...] = (_rope(q * inv * qn_ref[...], cos, sins) * scale).astype(BF16)
    waq_o[...] = (_rope(proj(_C_WAQ, _C_GAK), cos, sins) * scale).astype(BF16)
    k = proj(_C_GAK, _C_GAV)
    ks = k * k
    lane = lax.broadcasted_iota(jnp.int32, k.shape, 1)
    lo = lane < HEAD_DIM
    ms0 = jnp.sum(jnp.where(lo, ks, 0.0), axis=1, keepdims=True)
    ms1 = jnp.sum(jnp.where(lo, 0.0, ks), axis=1, keepdims=True)
    kinv = lax.rsqrt(jnp.where(lo, ms0, ms1) * (1.0 / HEAD_DIM) + EPS)
    gak_o[...] = _rope(k * kinv * kn_ref[...], cos, sins).astype(BF16)
    gav_o[...] = proj(_C_GAV, _C_WAK).astype(BF16)
    wak_o[...] = _rope(proj(_C_WAK, _C_WAV), cos, sins).astype(BF16)
    wav_o[...] = proj(_C_WAV, _C_END).astype(BF16)
    xbc_o[...] = proj(_C_XBC, _C_U)
    u = proj(_C_U, _C_Z)
    u_scr[0] = u[:, :LANES]
    u_scr[1] = u[:, LANES:]
    _s5_pack_kernel(u_scr.at[0], u_scr.at[1], ug_o)
    z_o[...] = proj(_C_Z, _C_DT)
    dt_o[...] = proj(_C_DT, _C_GAQ)


def _mod_row(i, nblk, nb):
    return jnp.where(i % nblk == 0, nb, i // nblk)


def _inproj(x, mod, norm_g, w_packed, cos_t, sin_t, qn_g, kn_g, nb, nblk):
    t, d = x.shape
    row = lambda i: (i, 0)
    fix = lambda i: (0, 0)
    outs = [(SSD_XBC, F32), None, (GROUP_W, F32), (LANES, F32),
            (N_HEADS * LANES, BF16), (LANES, BF16), (LANES, BF16),
            (N_HEADS * LANES, BF16), (LANES, BF16), (LANES, BF16)]
    shapes = [jax.ShapeDtypeStruct((t, o[0]), o[1]) if o else
              jax.ShapeDtypeStruct((S5_GROUPS, t // S5_Q, S5_BLK), F32) for o in outs]
    specs = [pl.BlockSpec((TM, o[0]), row) if o else
             pl.BlockSpec((S5_GROUPS, S5_TB, S5_BLK), lambda i: (0, i, 0)) for o in outs]
    return pl.pallas_call(
        _inproj_kernel,
        out_shape=shapes,
        grid=(t // TM,),
        in_specs=[pl.BlockSpec((TM, d), row),
                  pl.BlockSpec((1, 6, d), lambda i: (_mod_row(i, nblk, nb), 0, 0)),
                  pl.BlockSpec((1, d), fix),
                  pl.BlockSpec((d, _C_END), fix),
                  pl.BlockSpec((TM, LANES), lambda i: (i % nblk, 0)),
                  pl.BlockSpec((TM, LANES), lambda i: (i % nblk, 0)),
                  pl.BlockSpec((1, N_HEADS * LANES), fix),
                  pl.BlockSpec((1, LANES), fix)],
        out_specs=specs,
        scratch_shapes=[pltpu.VMEM((GROUP_W // LANES, TM, LANES), F32)],
        compiler_params=_cp(("parallel",)),
        name="in_proj",
    )(x, mod, norm_g.reshape(1, d), w_packed, cos_t, sin_t,
      jnp.tile(qn_g, 2 * N_HEADS).reshape(1, -1), jnp.tile(kn_g, 2).reshape(1, -1))


def _rope_tables(lc, l):
    n_rows = l // GRID_W
    rows = np.repeat(np.arange(n_rows), GRID_W)
    cols = np.tile(np.arange(GRID_W), n_rows)
    inv = np.power(np.float32(ROPE_BASE), -np.arange(ROPE_FREQS, dtype=np.float32) / ROPE_FREQS)
    ang = np.stack([rows, cols], axis=-1).astype(np.float32)[..., None] * inv
    cos = np.cos(ang)
    sin = np.sin(ang)
    cos64 = np.stack([cos, cos], axis=2).reshape(l, HEAD_DIM)
    sin64 = np.stack([-sin, sin], axis=2).reshape(l, HEAD_DIM)
    cos64 = np.concatenate([np.ones((lc, HEAD_DIM), np.float32), cos64], axis=0)
    sin64 = np.concatenate([np.zeros((lc, HEAD_DIM), np.float32), sin64], axis=0)
    return (jnp.asarray(np.tile(cos64, (1, 2)), dtype=F32), jnp.asarray(np.tile(sin64, (1, 2)), dtype=F32))


def _merge_heads(o2, kvh):
    tq = o2.shape[0] // 2
    oa, ob = o2[:tq], o2[tq:]
    lane = lax.broadcasted_iota(jnp.int32, oa.shape, 1)
    if kvh == 0:
        return jnp.where(lane < HEAD_DIM, oa, pltpu.roll(ob, HEAD_DIM, 1))
    return jnp.where(lane < HEAD_DIM, pltpu.roll(oa, HEAD_DIM, 1), ob)


def _stack_q(q_ref, rows, kvh):
    return jnp.concatenate([q_ref[rows, (2 * kvh) * LANES:(2 * kvh + 1) * LANES],
                            q_ref[rows, (2 * kvh + 1) * LANES:(2 * kvh + 2) * LANES]], axis=0)


def _ga_kernel(q_ref, k_ref, v_ref, o_ref, *, lc):
    j = pl.program_id(1)
    blk = GA_SUB * GA_TQ

    def attend(nkeys):
        k = k_ref[0:nkeys, :]
        v = v_ref[0:nkeys, :]
        lane = lax.broadcasted_iota(jnp.int32, v.shape, 1)
        vaug = [jnp.where((lane < HEAD_DIM) if kvh == 0 else (lane >= HEAD_DIM), v, jnp.ones_like(v))
                for kvh in range(2)]
        for sub in range(GA_SUB):
            rows = slice(sub * GA_TQ, (sub + 1) * GA_TQ)
            scores = [_dot_nt(_stack_q(q_ref, rows, kvh), k) for kvh in range(2)]
            outs = []
            for kvh in range(2):
                s = scores[kvh]
                p = jnp.exp2((s - jnp.max(s, axis=1, keepdims=True)).astype(BF16))
                o2 = _dot(p, vaug[kvh])
                outs.append(_merge_heads(o2 / pltpu.roll(o2, HEAD_DIM, 1), kvh))
            o_ref[rows, :] = jnp.concatenate(outs, axis=1)

    @pl.when(j < lc // blk)
    def _():
        attend(lc)

    @pl.when(j >= lc // blk)
    def _():
        attend(k_ref.shape[0])


def _ga(q, k, v, nb, s_len, lc):
    t = q.shape[0]
    blk = GA_SUB * GA_TQ
    nq = s_len // blk
    return pl.pallas_call(
        functools.partial(_ga_kernel, lc=lc),
        out_shape=jax.ShapeDtypeStruct((t, GROUP_W), F32),
        grid=(nb, nq),
        in_specs=[pl.BlockSpec((blk, N_HEADS * LANES), lambda b, j: (b * nq + j, 0)),
                  pl.BlockSpec((s_len, LANES), lambda b, j: (b, 0)),
                  pl.BlockSpec((s_len, LANES), lambda b, j: (b, 0))],
        out_specs=pl.BlockSpec((blk, GROUP_W), lambda b, j: (b * nq + j, 0)),
        compiler_params=_cp(("parallel", "arbitrary")),
        name="global_attn",
    )(q, k, v)


WA_SUB = TM // TQ


def _wa_kernel(sink_ref, q_ref, k_ref, v_ref, o_ref, *, lc):
    s_len = k_ref.shape[0]
    kc = k_ref[0:lc, :]
    vc = v_ref[0:lc, :]
    lane_c = lax.broadcasted_iota(jnp.int32, vc.shape, 1)
    lane_b = lax.broadcasted_iota(jnp.int32, (3 * TQ, LANES), 1)
    row = lax.broadcasted_iota(jnp.int32, (2 * TQ, 1), 0)
    for sub in range(WA_SUB):
        rows = slice(sub * TQ, (sub + 1) * TQ)
        n = pl.program_id(1) * WA_SUB + sub - lc // TQ
        start = pl.multiple_of(jnp.clip(lc + (n - 1) * TQ, lc, s_len - 3 * TQ), TQ)
        kb = k_ref[pl.ds(start, 3 * TQ), :]
        vb = v_ref[pl.ds(start, 3 * TQ), :]
        qpos = n * TQ + lax.broadcasted_iota(jnp.int32, (TQ, 3 * TQ), 0)
        kpos = (start - lc) + lax.broadcasted_iota(jnp.int32, (TQ, 3 * TQ), 1)
        reach = jnp.where(n >= 0, WINDOW, -1)
        valid = jnp.abs(qpos - kpos) <= reach
        valid = jnp.concatenate([valid, valid], axis=0)
        outs = []
        for kvh in range(2):
            q2 = jnp.concatenate([q_ref[rows, (2 * kvh) * LANES:(2 * kvh + 1) * LANES],
                                  q_ref[rows, (2 * kvh + 1) * LANES:(2 * kvh + 2) * LANES]], axis=0)
            sc = _dot_nt(q2, kc)
            sb = jnp.where(valid, _dot_nt(q2, kb), NEG_INF)
            sink = jnp.where(row < TQ, sink_ref[2 * kvh], sink_ref[2 * kvh + 1]) * LOG2E
            m = jnp.maximum(jnp.maximum(jnp.max(sc, axis=1, keepdims=True), jnp.max(sb, axis=1, keepdims=True)), sink)
            pc = jnp.exp2((sc - m).astype(BF16))
            pb = jnp.exp2((sb - m).astype(BF16))
            own_c = (lane_c < HEAD_DIM) if kvh == 0 else (lane_c >= HEAD_DIM)
            own_b = (lane_b < HEAD_DIM) if kvh == 0 else (lane_b >= HEAD_DIM)
            o2 = _dot(pc, jnp.where(own_c, vc, jnp.ones_like(vc))) + _dot(pb, jnp.where(own_b, vb, jnp.ones_like(vb)))
            denom = pltpu.roll(o2, HEAD_DIM, 1) + jnp.exp2(sink - m)
            outs.append(_merge_heads(o2 / denom, kvh))
        o_ref[rows, :] = jnp.concatenate(outs, axis=1)


def _wa(sink, q, k, v, nb, s_len, lc):
    t = q.shape[0]
    nq = s_len // TM
    return pl.pallas_call(
        functools.partial(_wa_kernel, lc=lc),
        out_shape=jax.ShapeDtypeStruct((t, GROUP_W), F32),
        grid=(nb, nq),
        in_specs=[pl.BlockSpec(memory_space=pltpu.SMEM),
                  pl.BlockSpec((TM, N_HEADS * LANES), lambda b, j: (b * nq + j, 0)),
                  pl.BlockSpec((s_len, LANES), lambda b, j: (b, 0)),
                  pl.BlockSpec((s_len, LANES), lambda b, j: (b, 0))],
        out_specs=pl.BlockSpec((TM, GROUP_W), lambda b, j: (b * nq + j, 0)),
        compiler_params=_cp(("parallel", "arbitrary")),
        name="window_attn",
    )(sink, q, k, v)


def _s5_chunk_index(t, rev, nc_ctx, nc_tot):
    if not rev:
        return t
    return jnp.where(t < nc_ctx, nc_ctx - 1 - t, nc_tot - 1 - (t - nc_ctx))


def _s5_kernel(u_ref, m_ref, p_ref, g_ref, ar_ref, ai_ref, dsk_ref, y_ref, s_scr, h_scr, *, nb, nc_ctx, nc_tot):
    uf = u_ref[0]
    u = uf.astype(BF16)
    for d in range(2):
        for k in range(2):
            s_scr[d, k] = _dot(u, p_ref[d, k, 0])
    ar = [jnp.broadcast_to(ar_ref[d, 0], (nb, LANES)) for d in range(2)]
    ai = [[jnp.broadcast_to(ai_ref[d, k, 0], (nb, LANES)) for k in range(2)] for d in range(2)]

    def body(t, carry):
        out = []
        for d in range(2):
            h, hs = carry[d]
            rows = pl.ds(_s5_chunk_index(t, d == 1, nc_ctx, nc_tot), nb, stride=nc_tot)
            h_scr[d, rows, :] = h
            out.append((ar[d] * h + ai[d][0] * hs + s_scr[d, 0, rows, :],
                        ar[d] * hs + ai[d][1] * h + s_scr[d, 1, rows, :]))
        return tuple(out)

    zero = jnp.zeros((nb, LANES), F32)
    lax.fori_loop(0, nc_tot, body, ((zero, zero), (zero, zero)), unroll=2)
    y = uf * dsk_ref[0]
    for d in range(2):
        y = y + _dot(u, m_ref[d, 0]) + _dot(h_scr[d].astype(BF16), g_ref[d, 0])
    y_ref[0] = y


S5_TB = TM // S5_Q
S5_GPS = LANES // S5_CH


def _s5_pack_kernel(lo_ref, hi_ref, o_ref):
    for s in range(S5_Q):
        rows = pl.ds(s, S5_TB, stride=S5_Q)
        halves = (lo_ref[rows, :], hi_ref[rows, :])
        dst = S5_CH * (s % S5_GPS)
        for g in range(S5_GROUPS):
            slab = halves[g // S5_GPS]
            src = S5_CH * (g % S5_GPS)
            moved = slab if src == dst else pltpu.roll(slab, (dst - src) % LANES, 1)
            o_ref[g, :, s * S5_CH:(s + 1) * S5_CH] = moved[:, dst:dst + S5_CH]


def _s5_unpack_kernel(y_ref, o_ref):
    lane_grp = lax.broadcasted_iota(jnp.int32, (S5_TB, LANES), 1) // S5_CH
    for s in range(S5_Q):
        src = S5_CH * (s % S5_GPS)
        for half in range(S5_GROUPS // S5_GPS):
            acc = None
            for gl in range(S5_GPS):
                slab = y_ref[half * S5_GPS + gl, :, (s // S5_GPS) * LANES:(s // S5_GPS + 1) * LANES]
                dst = S5_CH * gl
                moved = slab if src == dst else pltpu.roll(slab, (dst - src) % LANES, 1)
                acc = moved if acc is None else jnp.where(lane_grp == gl, moved, acc)
            o_ref[half, pl.ds(s, S5_TB, stride=S5_Q), :] = acc


def _s5_params(lam_re, lam_im, log_dt, b_re, b_im, c_re, c_im, d_skip):
    q = S5_Q
    dt = jnp.exp(log_dt)[..., None]
    lr, li = lam_re, lam_im
    mag = jnp.exp(lr * dt)
    a_re = mag * jnp.cos(li * dt)
    a_im = mag * jnp.sin(li * dt)
    den = lr * lr + li * li
    f_re = ((a_re - 1.0) * lr + a_im * li) / den
    f_im = (a_im * lr - (a_re - 1.0) * li) / den
    bb_re = f_re[..., None] * b_re - f_im[..., None] * b_im
    bb_im = f_re[..., None] * b_im + f_im[..., None] * b_re
    kk = jnp.arange(q + 1, dtype=F32)[:, None, None, None]
    pmag = jnp.exp(kk * (lr * dt))
    pw_re = pmag * jnp.cos(kk * (li * dt))
    pw_im = pmag * jnp.sin(kk * (li * dt))
    lw_re = pw_re[:q].transpose(1, 2, 0, 3)[:, :, :, None, :]
    lw_im = pw_im[:q].transpose(1, 2, 0, 3)[:, :, :, None, :]
    ck_re = c_re[:, :, None] * lw_re - c_im[:, :, None] * lw_im
    ck_im = c_re[:, :, None] * lw_im + c_im[:, :, None] * lw_re
    ck = jnp.concatenate([ck_re, -ck_im], axis=-1).reshape(2, S5_GROUPS, S5_BLK, 2 * S5_STATE)
    kern_t = jnp.einsum("dgmp,dgpc->dgcm", ck, jnp.concatenate([bb_re, bb_im], axis=2), precision=HI)
    kern_t = kern_t.reshape(2, S5_GROUPS, S5_CH, q, S5_CH)
    zeros = jnp.zeros_like(kern_t)
    bbt_re = bb_re.transpose(0, 1, 3, 2)[:, :, None]
    bbt_im = bb_im.transpose(0, 1, 3, 2)[:, :, None]
    ct_re = c_re.transpose(0, 1, 3, 2)[:, :, :, None, :]
    ct_im = c_im.transpose(0, 1, 3, 2)[:, :, :, None, :]
    ms, ps, gs = [], [], []
    for d in range(2):
        ext = (jnp.concatenate([zeros[d], kern_t[d]], axis=2) if d == 0
               else jnp.concatenate([kern_t[d, :, :, ::-1], zeros[d]], axis=2))
        ext = ext.reshape(S5_GROUPS, S5_CH, 2 * S5_BLK)
        lo = [(q - s) if d == 0 else (q - 1 - s) for s in range(q)]
        md = jnp.stack([ext[:, :, a * S5_CH:a * S5_CH + S5_BLK] for a in lo], axis=1)
        ms.append(md.reshape(S5_GROUPS, S5_BLK, S5_BLK))
        pidx = (q - 1 - jnp.arange(q)) if d == 0 else jnp.arange(q)
        pr = pw_re[pidx, d].transpose(1, 0, 2)[:, :, None, :]
        pi = pw_im[pidx, d].transpose(1, 0, 2)[:, :, None, :]
        p_re = pr * bbt_re[d] - pi * bbt_im[d]
        p_im = pr * bbt_im[d] + pi * bbt_re[d]
        pd = jnp.stack([jnp.concatenate([p_re, p_im], axis=3), jnp.concatenate([p_im, p_re], axis=3)])
        ps.append(pd.reshape(2, S5_GROUPS, S5_BLK, 2 * S5_STATE))
        gidx = (jnp.arange(q) + 1) if d == 0 else (q - jnp.arange(q))
        gw_re = pw_re[gidx, d].transpose(1, 2, 0)[..., None]
        gw_im = pw_im[gidx, d].transpose(1, 2, 0)[..., None]
        g_re = ct_re[d] * gw_re - ct_im[d] * gw_im
        g_im = ct_re[d] * gw_im + ct_im[d] * gw_re
        gs.append(jnp.concatenate([g_re, -g_im], axis=1).reshape(S5_GROUPS, 2 * S5_STATE, S5_BLK))
    ar = jnp.concatenate([pw_re[q], pw_re[q]], axis=-1)[:, :, None, :]
    ai = jnp.stack([jnp.concatenate([-pw_im[q], pw_im[q]], axis=-1),
                    jnp.concatenate([pw_im[q], -pw_im[q]], axis=-1)], axis=1)[:, :, :, None, :]
    dsk = jnp.tile(d_skip.reshape(S5_GROUPS, 1, S5_CH), (1, 1, q))
    return (jnp.stack(ms).astype(BF16), jnp.stack(ps).astype(BF16), jnp.stack(gs).astype(BF16),
            ar.astype(F32), ai.astype(F32), dsk.astype(F32))


def _s5(ug, params, nb, s_len, lc):
    m, p, g, ar, ai, dsk = params
    nc_tot = s_len // S5_Q
    nc_ctx = lc // S5_Q
    r = nb * nc_tot
    return pl.pallas_call(
        functools.partial(_s5_kernel, nb=nb, nc_ctx=nc_ctx, nc_tot=nc_tot),
        out_shape=jax.ShapeDtypeStruct((S5_GROUPS, r, S5_BLK), F32),
        grid=(S5_GROUPS,),
        in_specs=[pl.BlockSpec((1, r, S5_BLK), lambda gi: (gi, 0, 0)),
                  pl.BlockSpec((2, 1, S5_BLK, S5_BLK), lambda gi: (0, gi, 0, 0)),
                  pl.BlockSpec((2, 2, 1, S5_BLK, 2 * S5_STATE), lambda gi: (0, 0, gi, 0, 0)),
                  pl.BlockSpec((2, 1, 2 * S5_STATE, S5_BLK), lambda gi: (0, gi, 0, 0)),
                  pl.BlockSpec((2, 1, 1, 2 * S5_STATE), lambda gi: (0, gi, 0, 0)),
                  pl.BlockSpec((2, 2, 1, 1, 2 * S5_STATE), lambda gi: (0, 0, gi, 0, 0)),
                  pl.BlockSpec((1, 1, S5_BLK), lambda gi: (gi, 0, 0))],
        out_specs=pl.BlockSpec((1, r, S5_BLK), lambda gi: (gi, 0, 0)),
        scratch_shapes=[pltpu.VMEM((2, 2, r, 2 * S5_STATE), F32), pltpu.VMEM((2, r, 2 * S5_STATE), F32)],
        compiler_params=_cp(("parallel",)),
        name="s5_scan",
    )(ug, m, p, g, ar, ai, dsk)


def _conv_kernel(x_ref, prev_ref, next_ref, w_ref, b_ref, o_ref, *, nblk):
    i = pl.program_id(0) % nblk
    x = x_ref[...]
    rows = x.shape[0]
    ridx = lax.broadcasted_iota(jnp.int32, x.shape, 0)
    prev_row = jnp.where(i <= 1, 0.0, prev_ref[SUBLANES - 1:SUBLANES, :])
    next_row = jnp.where(jnp.logical_or(i == 0, i == nblk - 1), 0.0, next_ref[0:1, :])
    xm = jnp.where(ridx == 0, prev_row, pltpu.roll(x, 1, 0))
    xp = jnp.where(ridx == rows - 1, next_row, pltpu.roll(x, rows - 1, 0))
    y = xm * w_ref[0:1, :] + x * w_ref[1:2, :] + xp * w_ref[2:3, :] + b_ref[...]
    o_ref[...] = _silu(y)


def _conv(xbc, w, b, nblk):
    t, c = xbc.shape
    per = TM // SUBLANES
    last = t // SUBLANES - 1
    return pl.pallas_call(
        functools.partial(_conv_kernel, nblk=nblk),
        out_shape=jax.ShapeDtypeStruct((t, c), F32),
        grid=(t // TM,),
        in_specs=[pl.BlockSpec((TM, c), lambda i: (i, 0)),
                  pl.BlockSpec((SUBLANES, c), lambda i: (jnp.maximum(i * per - 1, 0), 0)),
                  pl.BlockSpec((SUBLANES, c), lambda i: (jnp.minimum((i + 1) * per, last), 0)),
                  pl.BlockSpec((3, c), lambda i: (0, 0)),
                  pl.BlockSpec((1, c), lambda i: (0, 0))],
        out_specs=pl.BlockSpec((TM, c), lambda i: (i, 0)),
        compiler_params=_cp(("parallel",)),
        name="ssd_conv",
    )(xbc, xbc, xbc, w, b.reshape(1, c))


_X_B = GROUP_W
_X_C = GROUP_W + SSD_NGROUPS * SSD_STATE


def _ssd_kernel(xf_ref, dtf_ref, dttf_ref, xr_ref, dtr_ref, dttr_ref, bias_ref, a_ref, biast_ref, at_ref, dsk_ref,
                yf_ref, yr_ref, stf_ref, str_ref):
    @pl.when(pl.program_id(1) == 0)
    def _():
        stf_ref[...] = jnp.zeros_like(stf_ref)
        str_ref[...] = jnp.zeros_like(str_ref)

    par = (bias_ref[...], a_ref[...], biast_ref[...], at_ref[...], dsk_ref[...])
    for j in range(SSD_SUB):
        rf = slice(j * TQ, (j + 1) * TQ)
        yf_ref[rf, :] = _ssd_chunk_step(xf_ref[rf, :], dtf_ref[rf, :], dttf_ref[0, :, rf], par, stf_ref, False)
        rr = slice((SSD_SUB - 1 - j) * TQ, (SSD_SUB - j) * TQ)
        yr_ref[rr, :] = _ssd_chunk_step(xr_ref[rr, :], dtr_ref[rr, :], dttr_ref[0, :, rr], par, str_ref, True)


def _ssd_chunk_step(xc, dt_raw, dtt_raw, par, st_ref, rev):
    bias, a_vec, biast, at_vec, dsk = par
    base = SSD_HEADS if rev else 0
    x = xc[:, 0:GROUP_W]
    dt = _softplus(dt_raw + bias)
    a = dt * a_vec
    dtt = _softplus(dtt_raw + biast)
    at = dtt * at_vec
    ri = lax.broadcasted_iota(jnp.int32, (TQ, TQ), 0)
    ci = lax.broadcasted_iota(jnp.int32, (TQ, TQ), 1)
    causal = (ci >= ri) if rev else (ri >= ci)
    tri = jnp.where(causal, 1.0, 0.0)
    cum_c = _dot_hi(tri, a)
    cum_r = _dot_nt_hi(at, tri)
    edge = 0 if rev else TQ - 1
    tot = cum_c[edge:edge + 1, :]

    shape = (TQ, GROUP_W)
    xdt = x * _per_head_cols(dt, base, SSD_HEADS, shape)
    lane = lax.broadcasted_iota(jnp.int32, shape, 1)
    y = jnp.zeros(shape, F32)
    bmat = [xc[:, _X_B + g * SSD_STATE:_X_B + (g + 1) * SSD_STATE].astype(BF16) for g in range(SSD_NGROUPS)]
    cmat = [xc[:, _X_C + g * SSD_STATE:_X_C + (g + 1) * SSD_STATE].astype(BF16) for g in range(SSD_NGROUPS)]
    cb = [_dot_nt(cmat[g], bmat[g]) for g in range(SSD_NGROUPS)]
    for h in range(SSD_HEADS):
        col = base + h
        seg = jnp.where(causal, cum_c[:, col:col + 1] - cum_r[col:col + 1, :], NEG_INF)
        scores = cb[h // 2] * jnp.exp(seg)
        xh = jnp.where((lane >= h * HEAD_DIM) & (lane < (h + 1) * HEAD_DIM), xdt, 0.0)
        y = y + _dot(scores.astype(BF16), xh.astype(BF16))
    st = st_ref[...]
    yo = jnp.concatenate(
        [_dot_nt(cmat[g], st[g * SSD_STATE:(g + 1) * SSD_STATE].astype(BF16)) for g in range(SSD_NGROUPS)], axis=1)
    y = y + yo * _per_head_cols(jnp.exp(cum_c), base, SSD_HEADS, shape)
    if not rev:
        y = y + x * dsk
    xd = xdt * _per_head_cols(jnp.exp(tot - cum_c), base, SSD_HEADS, shape)
    xdt_t = xd.T.astype(BF16)
    decay = jnp.exp(tot)
    for g in range(SSD_NGROUPS):
        new = _dot(xdt_t[g * SSD_STATE:(g + 1) * SSD_STATE], bmat[g])
        for hh in range(2):
            h = 2 * g + hh
            r0 = h * HEAD_DIM
            st_ref[r0:r0 + HEAD_DIM, :] = (decay[:, base + h:base + h + 1] * st[r0:r0 + HEAD_DIM]
                                           + new[hh * HEAD_DIM:(hh + 1) * HEAD_DIM])
    return y


def _dot_nt_hi(a, b):
    return lax.dot_general(a, b, (((1,), (1,)), ((), ())), preferred_element_type=F32, precision=HI)


def _ssd_chunk(c, rev, nc_ctx, nc_tot):
    if not rev:
        return c
    return jnp.where(c < nc_ctx, nc_ctx - 1 - c, nc_tot - 1 - (c - nc_ctx))


SSD_SUB = TM // TQ


def _ssd_scan(xc, dt, dtt, bias, a, biast, at, dsk, nb, s_len, lc):
    t = xc.shape[0]
    nblk = s_len // TM
    nctx = lc // TM
    fix = lambda b, c: (0, 0)

    def rows(rev):
        return lambda b, c: (b * nblk + _ssd_chunk(c, rev, nctx, nblk), 0)

    def lanes(rev):
        return lambda b, c: (b, 0, _ssd_chunk(c, rev, nctx, nblk))

    def data_specs(rev):
        return [pl.BlockSpec((TM, SSD_XBC), rows(rev)), pl.BlockSpec((TM, LANES), rows(rev)),
                pl.BlockSpec((1, SUBLANES, TM), lanes(rev))]

    state = pltpu.VMEM((SSD_HEADS * HEAD_DIM, SSD_STATE), F32)
    return pl.pallas_call(
        _ssd_kernel,
        out_shape=[jax.ShapeDtypeStruct((t, GROUP_W), F32)] * 2,
        grid=(nb, nblk),
        in_specs=data_specs(False) + data_specs(True) + [
            pl.BlockSpec((1, LANES), fix), pl.BlockSpec((1, LANES), fix),
            pl.BlockSpec((SUBLANES, TQ), fix), pl.BlockSpec((SUBLANES, TQ), fix),
            pl.BlockSpec((1, GROUP_W), fix)],
        out_specs=[pl.BlockSpec((TM, GROUP_W), rows(False)), pl.BlockSpec((TM, GROUP_W), rows(True))],
        scratch_shapes=[state, state],
        compiler_params=_cp(("parallel", "arbitrary")),
        name="ssd_scan",
    )(xc, dt, dtt, xc, dt, dtt, bias, a, biast, at, dsk)


def _ssd(xbc, dt, conv_w, conv_b, dt_bias, a_log, d_skip, nb, s_len, lc):
    nblk = s_len // TM
    xc = _conv(xbc, conv_w, conv_b, nblk)
    nd = 2 * SSD_HEADS
    dtt = dt[:, :nd].reshape(nb, s_len, nd).transpose(0, 2, 1)
    bias = jnp.pad(dt_bias.reshape(1, nd), ((0, 0), (0, LANES - nd)))
    a = jnp.pad(-jnp.exp(a_log).reshape(1, nd), ((0, 0), (0, LANES - nd)))
    biast = jnp.broadcast_to(dt_bias.reshape(nd, 1), (nd, TQ))
    at = jnp.broadcast_to(-jnp.exp(a_log).reshape(nd, 1), (nd, TQ))
    dsk = jnp.repeat(d_skip, HEAD_DIM).reshape(1, GROUP_W)
    return _ssd_scan(xc, dt, dtt, bias, a, biast, at, dsk, nb, s_len, lc)


def _outproj_kernel(x_ref, ys5_ref, oga_ref, y0_ref, y1_ref, z_ref, owa_ref, mod_ref, gluw_ref, glub_ref,
                    ng_ref, wout_ref, n2_ref, wr_ref, br_ref, xn_o, h2_o, route_o, y_scr):
    _s5_unpack_kernel(ys5_ref, y_scr)
    gl = _gelu_tanh(jnp.concatenate([y_scr[0], y_scr[1]], axis=1))
    a = gl * _sigmoid(_dot(gl.astype(BF16), gluw_ref[...]) + glub_ref[...])
    m = (y0_ref[...] + y1_ref[...]) * _silu(z_ref[...])
    m = m * lax.rsqrt(jnp.mean(m * m, axis=-1, keepdims=True) + EPS) * ng_ref[...]
    w = wout_ref
    mix = (_dot(a.astype(BF16), w[0:GROUP_W, :]) + _dot(oga_ref[...].astype(BF16), w[GROUP_W:2 * GROUP_W, :])
           + _dot(m.astype(BF16), w[2 * GROUP_W:3 * GROUP_W, :]) + _dot(owa_ref[...].astype(BF16), w[3 * GROUP_W:, :]))
    xn = x_ref[...] + mod_ref[0, 2:3, :] * mix
    xn_o[...] = xn
    h2 = xn * lax.rsqrt(jnp.mean(xn * xn, axis=-1, keepdims=True) + EPS) * n2_ref[...]
    h2 = h2 * (1.0 + mod_ref[0, 4:5, :]) + mod_ref[0, 3:4, :]
    h2_o[...] = _pack_bf16_pair(h2)
    h_hi = h2.astype(BF16)
    h_lo = (h2 - h_hi.astype(F32)).astype(BF16)
    logits = _dot(h_hi, wr_ref[0]) + (_dot(h_lo, wr_ref[0]) + _dot(h_hi, wr_ref[1])) + br_ref[...]
    lane = lax.broadcasted_iota(jnp.int32, logits.shape, 1)
    big = 4 * LANES
    lcoarse = jnp.where(lane < MOE_GROUPS, logits, NEG_INF)
    mx = jnp.max(lcoarse, axis=1, keepdims=True)
    den = jnp.sum(jnp.exp(lcoarse - mx), axis=1, keepdims=True)
    grp = jnp.min(jnp.where(lcoarse == mx, lane, big), axis=1, keepdims=True)
    pg = 1.0 / den
    lo = ROUTE_FINE0 + grp * MOE_PER_GROUP
    lf = jnp.where(lane >= lo, jnp.where(lane < lo + MOE_PER_GROUP, logits, NEG_INF), NEG_INF)
    v1 = jnp.max(lf, axis=1, keepdims=True)
    i1 = jnp.min(jnp.where(lf == v1, lane, big), axis=1, keepdims=True)
    lf2 = jnp.where(lane == i1, NEG_INF, lf)
    v2 = jnp.max(lf2, axis=1, keepdims=True)
    i2 = jnp.min(jnp.where(lf2 == v2, lane, big), axis=1, keepdims=True)
    e2 = jnp.exp(v2 - v1)
    w1 = pg / (1.0 + e2)
    w2 = w1 * e2
    route = jnp.where(lane == 0, (i1 - ROUTE_FINE0).astype(F32),
                      jnp.where(lane == 1, (i2 - ROUTE_FINE0).astype(F32),
                                jnp.where(lane == 2, w1, jnp.where(lane == 3, w2, 0.0))))
    route_o[...] = route


def _outproj(x, ys5, oga, y0, y1, z, owa, mod, glu_w, glu_b, ssd_norm_g, w_out, norm2_g, wr, br, nb, nblk):
    t, d = x.shape
    row = lambda i: (i, 0)
    fix = lambda i: (0, 0)
    gw = pl.BlockSpec((TM, GROUP_W), row)
    wr_hi = wr.astype(BF16)
    return pl.pallas_call(
        _outproj_kernel,
        out_shape=[jax.ShapeDtypeStruct((t, d), F32), jax.ShapeDtypeStruct((t, d // 2), jnp.uint32),
                   jax.ShapeDtypeStruct((t, LANES), F32)],
        grid=(t // TM,),
        in_specs=[pl.BlockSpec((TM, d), row), pl.BlockSpec((S5_GROUPS, S5_TB, S5_BLK), lambda i: (0, i, 0)),
                  gw, gw, gw, gw, gw,
                  pl.BlockSpec((1, 6, d), lambda i: (_mod_row(i, nblk, nb), 0, 0)),
                  pl.BlockSpec((GROUP_W, GROUP_W), fix),
                  pl.BlockSpec((1, GROUP_W), fix),
                  pl.BlockSpec((1, GROUP_W), fix),
                  pl.BlockSpec((d, d), fix),
                  pl.BlockSpec((1, d), fix),
                  pl.BlockSpec((2, d, LANES), lambda i: (0, 0, 0)),
                  pl.BlockSpec((1, LANES), fix)],
        out_specs=[pl.BlockSpec((TM, d), row), pl.BlockSpec((TM, d // 2), row), pl.BlockSpec((TM, LANES), row)],
        scratch_shapes=[pltpu.VMEM((GROUP_W // LANES, TM, LANES), F32)],
        compiler_params=_cp(("parallel",)),
        name="out_proj_router",
    )(x, ys5, oga, y0, y1, z, owa, mod, glu_w.astype(BF16), glu_b.reshape(1, -1), ssd_norm_g.reshape(1, -1),
      w_out.astype(BF16), norm2_g.reshape(1, -1), jnp.stack([wr_hi, (wr - wr_hi.astype(F32)).astype(BF16)]), br)


def _pack_router(coarse_w, coarse_b, fine_w, fine_b):
    def lanes(coarse, fine):
        gap = jnp.zeros(coarse.shape[:-1] + (ROUTE_FINE0 - MOE_GROUPS,), F32)
        tail = jnp.zeros(coarse.shape[:-1] + (LANES - ROUTE_FINE0 - N_EXPERTS,), F32)
        return jnp.concatenate([coarse, gap, fine, tail], axis=-1)

    return lanes(coarse_w, fine_w), lanes(coarse_b[None, :], fine_b[None, :])


def _gather_rows(src, idx):
    m = idx.shape[0]
    d = src.shape[1]
    workers = SC_CORES * SC_SUBCORES
    nch = m // (workers * SC_GATHER_K)
    assert nch * workers * SC_GATHER_K == m
    mesh = plsc.VectorSubcoreMesh(core_axis_name="c", subcore_axis_name="s")

    @functools.partial(
        pl.kernel, mesh=mesh,
        out_type=jax.ShapeDtypeStruct((m, d), src.dtype),
        scratch_types=[pltpu.VMEM((nch, SC_GATHER_K), jnp.int32),
                       pltpu.VMEM((SC_GATHER_K, d), src.dtype),
                       pltpu.SemaphoreType.DMA],
    )
    def gather(src_hbm, idx_hbm, out_hbm, idx_v, rows_v, sem):
        wid = lax.axis_index("s") * SC_CORES + lax.axis_index("c")
        pltpu.sync_copy(idx_hbm.at[wid], idx_v)

        @pl.loop(0, nch)
        def _(j):
            off = pl.multiple_of((wid * nch + j) * SC_GATHER_K, SC_GATHER_K)
            pltpu.async_copy(src_hbm.at[idx_v.at[j]], rows_v, sem).wait()
            pltpu.sync_copy(rows_v, out_hbm.at[pl.ds(off, SC_GATHER_K)])

    return gather(src, idx.reshape(workers, nch, SC_GATHER_K))


def _scatter_rows(src, dst0, dst1, nrows):
    t, d = src.shape
    workers = SC_CORES * SC_SUBCORES
    nch = t // (workers * SC_GATHER_K)
    assert nch * workers * SC_GATHER_K == t
    mesh = plsc.VectorSubcoreMesh(core_axis_name="c", subcore_axis_name="s")

    @functools.partial(
        pl.kernel, mesh=mesh,
        out_type=jax.ShapeDtypeStruct((nrows, d), src.dtype),
        scratch_types=[pltpu.VMEM((nch, SC_GATHER_K), jnp.int32),
                       pltpu.VMEM((nch, SC_GATHER_K), jnp.int32),
                       pltpu.VMEM((SC_GATHER_K, d), src.dtype)],
    )
    def scatter(src_hbm, d0_hbm, d1_hbm, out_hbm, i0_v, i1_v, rows_v):
        wid = lax.axis_index("s") * SC_CORES + lax.axis_index("c")
        pltpu.sync_copy(d0_hbm.at[wid], i0_v)
        pltpu.sync_copy(d1_hbm.at[wid], i1_v)

        @pl.loop(0, nch)
        def _(j):
            off = pl.multiple_of((wid * nch + j) * SC_GATHER_K, SC_GATHER_K)
            pltpu.sync_copy(src_hbm.at[pl.ds(off, SC_GATHER_K)], rows_v)
            pltpu.sync_copy(rows_v, out_hbm.at[i0_v.at[j]])
            pltpu.sync_copy(rows_v, out_hbm.at[i1_v.at[j]])

    return scatter(src, dst0.reshape(workers, nch, SC_GATHER_K), dst1.reshape(workers, nch, SC_GATHER_K))


def _expert_kernel(be_ref, nused_ref, nvalid_ref, x_ref, wg_ref, wu_ref, wd_ref, o_ref, wg_s, wu_s, wd_s):
    i = pl.program_id(0)
    new_expert = jnp.logical_or(i == 0, be_ref[i] != be_ref[jnp.maximum(i - 1, 0)])

    @pl.when(jnp.logical_and(i < nused_ref[0], new_expert))
    def _():
        wg_s[...] = wg_ref[0, 0].astype(BF16)
        wu_s[...] = wu_ref[0, 0].astype(BF16)
        wd_s[...] = wd_ref[0, 0].astype(BF16)

    @pl.when(i < nused_ref[0])
    def _():
        row = lax.broadcasted_iota(jnp.int32, x_ref.shape, 0)
        lo, hi = _unpack_bf16_pair(jnp.where(row < nvalid_ref[i], x_ref[...], jnp.uint32(0)))
        lo = lo.astype(BF16)
        hi = hi.astype(BF16)
        half = lo.shape[1]
        gate = _dot(lo, wg_s[0:half, :]) + _dot(hi, wg_s[half:, :])
        up = _dot(lo, wu_s[0:half, :]) + _dot(hi, wu_s[half:, :])
        o_ref[...] = _pack_bf16_pair(_dot((_silu(gate) * up).astype(BF16), wd_s[...]))

    @pl.when(i >= nused_ref[0])
    def _():
        o_ref[...] = jnp.zeros_like(o_ref)


def _experts(xs, blk_e, n_used, n_valid, wg, wu, wd, layer):
    rows, dp = xs.shape
    d = 2 * dp
    nblocks = rows // MOE_TM
    de = wg.shape[3]
    wsel = lambda i, be, nu, nv: (layer, be[i], 0, 0)
    grid_spec = pltpu.PrefetchScalarGridSpec(
        num_scalar_prefetch=3,
        grid=(nblocks,),
        in_specs=[pl.BlockSpec((MOE_TM, dp), lambda i, be, nu, nv: (i, 0)),
                  pl.BlockSpec((1, 1, d, de), wsel),
                  pl.BlockSpec((1, 1, d, de), wsel),
                  pl.BlockSpec((1, 1, de, d), wsel)],
        out_specs=pl.BlockSpec((MOE_TM, dp), lambda i, be, nu, nv: (i, 0)),
        scratch_shapes=[pltpu.VMEM((d, de), BF16), pltpu.VMEM((d, de), BF16), pltpu.VMEM((de, d), BF16)],
    )
    return pl.pallas_call(
        _expert_kernel,
        out_shape=jax.ShapeDtypeStruct((rows, dp), jnp.uint32),
        grid_spec=grid_spec,
        compiler_params=_cp(("arbitrary",)),
        name="moe_experts",
    )(blk_e, n_used, n_valid, xs, wg, wu, wd)


def _combine_kernel(x_ref, r0_ref, r1_ref, route_ref, mod_ref, fg_ref, o_ref, *, final):
    route = route_ref[...]
    r0 = jnp.concatenate(_unpack_bf16_pair(r0_ref[...]), axis=1)
    r1 = jnp.concatenate(_unpack_bf16_pair(r1_ref[...]), axis=1)
    f = route[:, 2:3] * r0 + route[:, 3:4] * r1
    y = x_ref[...] + mod_ref[0, 5:6, :] * f
    if final:
        y = y * lax.rsqrt(jnp.mean(y * y, axis=-1, keepdims=True) + EPS) * fg_ref[...]
    o_ref[...] = y


def _combine(xn, rows2, route, mod, final_g, nb, nblk, final):
    t, d = xn.shape
    if final:
        nlat = nblk - 1
        grid = (nb * nlat,)
        src = lambda i: ((i // nlat) * nblk + 1 + i % nlat, 0)
        modi = lambda i: (i // nlat, 0, 0)
        out_rows = nb * nlat * TM
    else:
        grid = (t // TM,)
        src = lambda i: (i, 0)
        modi = lambda i: (_mod_row(i, nblk, nb), 0, 0)
        out_rows = t
    return pl.pallas_call(
        functools.partial(_combine_kernel, final=final),
        out_shape=jax.ShapeDtypeStruct((out_rows, d), F32),
        grid=grid,
        in_specs=[pl.BlockSpec((TM, d), src),
                  pl.BlockSpec((TM, d // 2), src),
                  pl.BlockSpec((TM, d // 2), lambda i: (src(i)[0] + t // TM, 0)),
                  pl.BlockSpec((TM, LANES), src),
                  pl.BlockSpec((1, 6, d), modi),
                  pl.BlockSpec((1, d), lambda i: (0, 0))],
        out_specs=pl.BlockSpec((TM, d), lambda i: (i, 0)),
        compiler_params=_cp(("parallel",)),
        name="moe_combine_final" if final else "moe_combine",
    )(xn, rows2, rows2, route, mod, final_g.reshape(1, d))


def _moe(h2, route, wg, wu, wd, layer):
    t, d = h2.shape
    n_slots = 2 * t
    experts = jnp.arange(N_EXPERTS, dtype=F32)[None, :]
    oh0 = (route[:, 0:1] == experts).astype(F32)
    oh1 = (route[:, 1:2] == experts).astype(F32)
    both = (oh0 + oh1).reshape(t // LANES, LANES, N_EXPERTS)
    tri = jnp.tril(jnp.ones((LANES, LANES), F32))
    intra = jnp.einsum("ij,bjk->bik", tri, both)
    blk_tot = intra[:, -1, :]
    blk_cum = jnp.cumsum(blk_tot, axis=0)
    earlier = (intra - both + (blk_cum - blk_tot)[:, None, :]).reshape(t, N_EXPERTS)
    counts = blk_cum[-1].astype(jnp.int32)
    pcounts = (counts + MOE_TM - 1) // MOE_TM * MOE_TM
    pends = jnp.cumsum(pcounts)
    pstarts = pends - pcounts
    base = pstarts.astype(F32)[None, :] + earlier
    dest0 = jnp.sum(oh0 * base, axis=1).astype(jnp.int32)
    dest1 = jnp.sum(oh1 * base, axis=1).astype(jnp.int32)
    nblocks = -(-n_slots // MOE_TM) + N_EXPERTS
    nrows = -(-nblocks * MOE_TM // GATHER_ROWS) * GATHER_ROWS
    nblocks = nrows // MOE_TM
    blk_start = jnp.arange(nblocks, dtype=jnp.int32) * MOE_TM
    blk_e = jnp.minimum(jnp.sum((pends[None, :] <= blk_start[:, None]).astype(jnp.int32), axis=1), N_EXPERTS - 1)
    n_used = (pends[-1] // MOE_TM).astype(jnp.int32).reshape(1)
    n_valid = jnp.clip((pstarts + counts)[blk_e] - blk_start, 0, MOE_TM).astype(jnp.int32)
    xs = _scatter_rows(h2, dest0, dest1, nrows)
    ys = _experts(xs, blk_e, n_used, n_valid, wg, wu, wd, layer)
    return _gather_rows(ys, jnp.concatenate([dest0, dest1]))


def kernel(x, c, ctx, c_ctx, ada_w, ada_b, norm1_g, norm2_g, w_in, w_out, s5_lam_re, s5_lam_im, s5_log_dt, s5_b_re, s5_b_im, s5_c_re, s5_c_im, s5_d, s5_glu_w, s5_glu_b, ga_qn_g, ga_kn_g, ssd_conv_w, ssd_conv_b, ssd_dt_bias, ssd_a_log, ssd_d, ssd_norm_g, wa_sink, moe_coarse_w, moe_coarse_b, moe_fine_w, moe_fine_b, moe_w_gate, moe_w_up, moe_w_down, final_g):
    nb, l, d = x.shape
    lc = ctx.shape[1]
    depth = ada_w.shape[0]
    assert lc == TM and l % TM == 0 and nb <= SUBLANES - 1 and d == D_MODEL
    s_len = lc + l
    nblk = s_len // TM
    t = nb * s_len

    xm = jnp.concatenate([ctx, x], axis=1).reshape(t, d)
    cc = jnp.zeros((SUBLANES, d), F32).at[:nb].set(c).at[nb].set(c_ctx)
    mods = _ada(cc, ada_w, ada_b).reshape(depth, SUBLANES, 6, d)
    cos_t, sin_t = _rope_tables(lc, l)
    w_packed = jax.vmap(_pack_w_in)(w_in)
    s5_tabs = jax.vmap(_s5_params)(s5_lam_re, s5_lam_im, s5_log_dt, s5_b_re, s5_b_im, s5_c_re, s5_c_im, s5_d)
    wrs, brs = jax.vmap(_pack_router)(moe_coarse_w, moe_coarse_b, moe_fine_w, moe_fine_b)

    out = None
    for i in range(depth):
        mod = mods[i]
        (xbc, ug, z, dt, gaq, gak, gav, waq, wak, wav) = _inproj(
            xm, mod, norm1_g[i], w_packed[i], cos_t, sin_t, ga_qn_g[i], ga_kn_g[i], nb, nblk)
        ys5 = _s5(ug, tuple(tab[i] for tab in s5_tabs), nb, s_len, lc)
        oga = _ga(gaq, gak, gav, nb, s_len, lc)
        y0, y1 = _ssd(xbc, dt, ssd_conv_w[i], ssd_conv_b[i], ssd_dt_bias[i], ssd_a_log[i], ssd_d[i], nb, s_len, lc)
        owa = _wa(wa_sink[i], waq, wak, wav, nb, s_len, lc)
        wr, br = wrs[i], brs[i]
        xn, h2, route = _outproj(xm, ys5, oga, y0, y1, z, owa, mod, s5_glu_w[i], s5_glu_b[i], ssd_norm_g[i],
                                 w_out[i], norm2_g[i], wr, br, nb, nblk)
        rows2 = _moe(h2, route, moe_w_gate, moe_w_up, moe_w_down, i)
        final = i == depth - 1
        xm = _combine(xn, rows2, route, mod, final_g, nb, nblk, final)
        if final:
            out = xm.reshape(nb, l, d)
    return out
```

```python
import functools
import math

import jax
import jax.numpy as jnp
import numpy as np
from jax import lax
from jax.experimental import pallas as pl
from jax.experimental.pallas import tpu as pltpu
from jax.experimental.pallas import tpu_sc as plsc

F32 = jnp.float32
BF16 = jnp.bfloat16
HI = lax.Precision.HIGHEST

D_MODEL = 1024
GRID_W = 64
GROUP_W = 256
HEAD_DIM = 64
ROPE_FREQS = HEAD_DIM // 4
ROPE_BASE = 10000.0
EPS = 1e-6
S5_CH = 16
S5_GROUPS = GROUP_W // S5_CH
S5_STATE = 64
N_HEADS = 4
SSD_HEADS = 4
SSD_NGROUPS = 2
SSD_STATE = 128
SSD_XBC = GROUP_W + 2 * SSD_NGROUPS * SSD_STATE
WINDOW = 128
MOE_GROUPS = 4
MOE_PER_GROUP = 8
N_EXPERTS = 32
D_EXPERT = D_MODEL // 2

LANES = 128
SUBLANES = 8
TM = 256
TQ = 128
GA_TQ = 128
GA_SUB = 2
S5_Q = 32
S5_BLK = S5_Q * S5_CH
MOE_TM = 512
SC_CORES = 2
SC_SUBCORES = 16
SC_GATHER_K = 32
GATHER_ROWS = SC_CORES * SC_SUBCORES * SC_GATHER_K
ROUTE_FINE0 = 32
VMEM_LIMIT = 56 * 1024 * 1024

NEG_INF = float("-inf")
LOG2E = math.log2(math.e)


def _cp(sem, vmem=VMEM_LIMIT):
    return pltpu.CompilerParams(dimension_semantics=sem, vmem_limit_bytes=vmem)


def _dot(a, b):
    return jnp.dot(a, b, preferred_element_type=F32)


def _dot_hi(a, b):
    return jnp.dot(a, b, preferred_element_type=F32, precision=HI)


def _dot_nt(a, b):
    return lax.dot_general(a, b, (((1,), (1,)), ((), ())), preferred_element_type=F32)


def _sigmoid(x):
    return 1.0 / (1.0 + jnp.exp(-x))


def _silu(x):
    return x * _sigmoid(x)


def _gelu_tanh(x):
    return 0.5 * x * (1.0 + jnp.tanh(math.sqrt(2.0 / math.pi) * (x + 0.044715 * (x * x * x))))


def _softplus(x):
    return jnp.maximum(x, 0.0) + jnp.log1p(jnp.exp(-jnp.abs(x)))


_HI16 = 0xFFFF0000


def _pack_bf16_pair(x):
    n = x.shape[1] // 2
    bits = pltpu.bitcast(x.astype(BF16).astype(F32), jnp.uint32)
    return (bits[:, n:] & jnp.uint32(_HI16)) | (bits[:, :n] >> 16)


def _unpack_bf16_pair(w):
    return pltpu.bitcast(w << 16, F32), pltpu.bitcast(w & jnp.uint32(_HI16), F32)


def _per_head_cols(v, base, n_heads, shape):
    lane = lax.broadcasted_iota(jnp.int32, shape, 1)
    out = jnp.broadcast_to(v[:, base + n_heads - 1:base + n_heads], shape)
    for h in range(n_heads - 2, -1, -1):
        out = jnp.where(lane < (h + 1) * HEAD_DIM, v[:, base + h:base + h + 1], out)
    return out


def _ada_kernel(c_ref, w_ref, b_ref, o_ref):
    c = c_ref[...]
    o_ref[0] = _dot_hi(_silu(c), w_ref[0]) + b_ref[0]


def _ada(cc, ada_w, ada_b):
    depth, d, n = ada_w.shape
    tn = 1536
    return pl.pallas_call(
        _ada_kernel,
        out_shape=jax.ShapeDtypeStruct((depth, SUBLANES, n), F32),
        grid=(depth, n // tn),
        in_specs=[pl.BlockSpec((SUBLANES, d), lambda l, j: (0, 0)),
                  pl.BlockSpec((1, d, tn), lambda l, j: (l, 0, j)),
                  pl.BlockSpec((1, 1, tn), lambda l, j: (l, 0, j))],
        out_specs=pl.BlockSpec((1, SUBLANES, tn), lambda l, j: (l, 0, j)),
        compiler_params=_cp(("parallel", "parallel")),
        name="ada_mod",
    )(cc, ada_w, ada_b.reshape(depth, 1, n))


_C_XBC = 0
_C_U = _C_XBC + SSD_XBC
_C_Z = _C_U + GROUP_W
_C_DT = _C_Z + GROUP_W
_C_GAQ = _C_DT + LANES
_C_WAQ = _C_GAQ + N_HEADS * LANES
_C_GAK = _C_WAQ + N_HEADS * LANES
_C_GAV = _C_GAK + LANES
_C_WAK = _C_GAV + LANES
_C_WAV = _C_WAK + LANES
_C_END = _C_WAV + LANES


def _expand_q_cols(wq):
    zero = jnp.zeros((wq.shape[0], HEAD_DIM), wq.dtype)
    parts = []
    for h in range(N_HEADS):
        head = wq[:, h * HEAD_DIM:(h + 1) * HEAD_DIM]
        parts += [head, zero] if h // 2 == 0 else [zero, head]
    return jnp.concatenate(parts, axis=1)


def _pack_w_in(w_in):
    cuts = np.cumsum([256, 256, 128, 128, 256, SSD_XBC, 2 * SSD_HEADS, 256, 128, 128])[:-1]
    u, gaq, gak, gav, z, xbc, dt, waq, wak, wav = jnp.split(w_in, [int(c) for c in cuts], axis=1)
    dt = jnp.pad(dt, ((0, 0), (0, LANES - dt.shape[1])))
    w = jnp.concatenate([xbc, u, z, dt, _expand_q_cols(gaq), _expand_q_cols(waq), gak, gav, wak, wav], axis=1)
    return w.astype(BF16)


def _rope(x, cos, sins):
    w = x.shape[1]
    if w > LANES:
        cos = jnp.concatenate([cos] * (w // LANES), axis=1)
        sins = jnp.concatenate([sins] * (w // LANES), axis=1)
    lane = lax.broadcasted_iota(jnp.int32, x.shape, 1)
    up = pltpu.roll(x, w - ROPE_FREQS, 1)
    dn = pltpu.roll(x, ROPE_FREQS, 1)
    partner = jnp.where((lane & ROPE_FREQS) == 0, up, dn)
    return x * cos + partner * sins


def _moe_residual(xn_ref, r0_ref, r1_ref, route_ref, mod_ref):
    route = route_ref[...]
    r0 = jnp.concatenate(_unpack_bf16_pair(r0_ref[...]), axis=1)
    r1 = jnp.concatenate(_unpack_bf16_pair(r1_ref[...]), axis=1)
    return xn_ref[...] + mod_ref[0, 5:6, :] * (route[:, 2:3] * r0 + route[:, 3:4] * r1)


def _inproj_kernel(*refs, first, nblk):
    if first:
        lat_ref, ctx_ref = refs[:2]
        x = jnp.where(pl.program_id(0) % nblk == 0, ctx_ref[...], lat_ref[...])
        refs = refs[2:]
    else:
        x = _moe_residual(*refs[:5])
        refs = refs[5:]
    (mod_ref, g_ref, w_ref, cos_ref, sin_ref, qn_ref, kn_ref,
     xm_o, xbc_o, ug_o, z_o, dt_o, gaq_o, gak_o, gav_o, waq_o, wak_o, wav_o, u_scr) = refs
    xm_o[...] = x
    ms = jnp.mean(x * x, axis=-1, keepdims=True)
    xn = x * lax.rsqrt(ms + EPS) * g_ref[...]
    h = xn * (1.0 + mod_ref[0, 1:2, :]) + mod_ref[0, 0:1, :]
    hb = h.astype(BF16)

    def proj(lo, hi):
        return _dot(hb, w_ref[:, lo:hi])

    cos = cos_ref[...]
    sins = sin_ref[...]
    scale = LOG2E * HEAD_DIM ** -0.5
    q = proj(_C_GAQ, _C_WAQ)
    qs = q * q
    inv = jnp.concatenate(
        [jnp.broadcast_to(lax.rsqrt(jnp.sum(qs[:, s * LANES:(s + 1) * LANES], axis=1, keepdims=True)
                                    * (1.0 / HEAD_DIM) + EPS), (q.shape[0], LANES)) for s in range(N_HEADS)], axis=1)
    gaq_o[...] = (_rope(q * inv * qn_ref[...], cos, sins) * scale).astype(BF16)
    waq_o[...] = (_rope(proj(_C_WAQ, _C_GAK), cos, sins) * scale).astype(BF16)
    k = proj(_C_GAK, _C_GAV)
    ks = k * k
    lane = lax.broadcasted_iota(jnp.int32, k.shape, 1)
    lo = lane < HEAD_DIM
    ms0 = jnp.sum(jnp.where(lo, ks, 0.0), axis=1, keepdims=True)
    ms1 = jnp.sum(jnp.where(lo, 0.0, ks), axis=1, keepdims=True)
    kinv = lax.rsqrt(jnp.where(lo, ms0, ms1) * (1.0 / HEAD_DIM) + EPS)
    gak_o[...] = _rope(k * kinv * kn_ref[...], cos, sins).astype(BF16)
    gav_o[...] = proj(_C_GAV, _C_WAK).astype(BF16)
    wak_o[...] = _rope(proj(_C_WAK, _C_WAV), cos, sins).astype(BF16)
    wav_o[...] = proj(_C_WAV, _C_END).astype(BF16)
    xbc_o[...] = proj(_C_XBC, _C_U)
    u = proj(_C_U, _C_Z)
    u_scr[0] = u[:, :LANES]
    u_scr[1] = u[:, LANES:]
    _s5_pack_kernel(u_scr.at[0], u_scr.at[1], ug_o)
    z_o[...] = proj(_C_Z, _C_DT)
    dt_o[...] = proj(_C_DT, _C_GAQ)


def _mod_row(i, nblk, nb):
    return jnp.where(i % nblk == 0, nb, i // nblk)


def _inproj(src, mod, norm_g, w_packed, cos_t, sin_t, qn_g, kn_g, nb, nblk):
    first = src[0] == "first"
    d = src[1].shape[1]
    t = nb * nblk * TM
    row = lambda i: (i, 0)
    fix = lambda i: (0, 0)
    modspec = pl.BlockSpec((1, 6, d), lambda i: (_mod_row(i, nblk, nb), 0, 0))
    if first:
        src_specs = [pl.BlockSpec((TM, d), lambda i: ((i // nblk) * (nblk - 1) + jnp.maximum(i % nblk - 1, 0), 0)),
                     pl.BlockSpec((TM, d), lambda i: (i // nblk, 0))]
        src_args = src[1:]
    else:
        src_specs = [pl.BlockSpec((TM, d), row), pl.BlockSpec((TM, d // 2), row),
                     pl.BlockSpec((TM, d // 2), lambda i: (i + t // TM, 0)), pl.BlockSpec((TM, LANES), row), modspec]
        src_args = (src[1], src[2], src[2], src[3], src[4])
    outs = [(d, F32), (SSD_XBC, F32), None, (GROUP_W, F32), (LANES, F32),
            (N_HEADS * LANES, BF16), (LANES, BF16), (LANES, BF16),
            (N_HEADS * LANES, BF16), (LANES, BF16), (LANES, BF16)]
    shapes = [jax.ShapeDtypeStruct((t, o[0]), o[1]) if o else
              jax.ShapeDtypeStruct((S5_GROUPS, t // S5_Q, S5_BLK), F32) for o in outs]
    specs = [pl.BlockSpec((TM, o[0]), row) if o else
             pl.BlockSpec((S5_GROUPS, S5_TB, S5_BLK), lambda i: (0, i, 0)) for o in outs]
    return pl.pallas_call(
        functools.partial(_inproj_kernel, first=first, nblk=nblk),
        out_shape=shapes,
        grid=(t // TM,),
        in_specs=src_specs + [
                  modspec,
                  pl.BlockSpec((1, d), fix),
                  pl.BlockSpec((d, _C_END), fix),
                  pl.BlockSpec((TM, LANES), lambda i: (i % nblk, 0)),
                  pl.BlockSpec((TM, LANES), lambda i: (i % nblk, 0)),
                  pl.BlockSpec((1, N_HEADS * LANES), fix),
                  pl.BlockSpec((1, LANES), fix)],
        out_specs=specs,
        scratch_shapes=[pltpu.VMEM((GROUP_W // LANES, TM, LANES), F32)],
        compiler_params=_cp(("parallel",)),
        name="in_proj",
    )(*src_args, mod, norm_g.reshape(1, d), w_packed, cos_t, sin_t,
      jnp.tile(qn_g, 2 * N_HEADS).reshape(1, -1), jnp.tile(kn_g, 2).reshape(1, -1))


def _rope_tables(lc, l):
    n_rows = l // GRID_W
    rows = np.repeat(np.arange(n_rows), GRID_W)
    cols = np.tile(np.arange(GRID_W), n_rows)
    inv = np.power(np.float32(ROPE_BASE), -np.arange(ROPE_FREQS, dtype=np.float32) / ROPE_FREQS)
    ang = np.stack([rows, cols], axis=-1).astype(np.float32)[..., None] * inv
    cos = np.cos(ang)
    sin = np.sin(ang)
    cos64 = np.stack([cos, cos], axis=2).reshape(l, HEAD_DIM)
    sin64 = np.stack([-sin, sin], axis=2).reshape(l, HEAD_DIM)
    cos64 = np.concatenate([np.ones((lc, HEAD_DIM), np.float32), cos64], axis=0)
    sin64 = np.concatenate([np.zeros((lc, HEAD_DIM), np.float32), sin64], axis=0)
    return (jnp.asarray(np.tile(cos64, (1, 2)), dtype=F32), jnp.asarray(np.tile(sin64, (1, 2)), dtype=F32))


def _merge_heads(o2, kvh):
    tq = o2.shape[0] // 2
    oa, ob = o2[:tq], o2[tq:]
    lane = lax.broadcasted_iota(jnp.int32, oa.shape, 1)
    if kvh == 0:
        return jnp.where(lane < HEAD_DIM, oa, pltpu.roll(ob, HEAD_DIM, 1))
    return jnp.where(lane < HEAD_DIM, pltpu.roll(oa, HEAD_DIM, 1), ob)


def _stack_q(q_ref, rows, kvh):
    return jnp.concatenate([q_ref[rows, (2 * kvh) * LANES:(2 * kvh + 1) * LANES],
                            q_ref[rows, (2 * kvh + 1) * LANES:(2 * kvh + 2) * LANES]], axis=0)


def _ga_kernel(q_ref, k_ref, v_ref, o_ref, *, lc):
    j = pl.program_id(1)
    blk = GA_SUB * GA_TQ

    def attend(nkeys):
        k = k_ref[0:nkeys, :]
        v = v_ref[0:nkeys, :]
        lane = lax.broadcasted_iota(jnp.int32, v.shape, 1)
        vaug = [jnp.where((lane < HEAD_DIM) if kvh == 0 else (lane >= HEAD_DIM), v, jnp.ones_like(v))
                for kvh in range(2)]
        for sub in range(GA_SUB):
            rows = slice(sub * GA_TQ, (sub + 1) * GA_TQ)
            scores = [_dot_nt(_stack_q(q_ref, rows, kvh), k) for kvh in range(2)]
            outs = []
            for kvh in range(2):
                s = scores[kvh]
                p = jnp.exp2((s - jnp.max(s, axis=1, keepdims=True)).astype(BF16))
                o2 = _dot(p, vaug[kvh])
                outs.append(_merge_heads(o2 / pltpu.roll(o2, HEAD_DIM, 1), kvh))
            o_ref[rows, :] = jnp.concatenate(outs, axis=1)

    @pl.when(j < lc // blk)
    def _():
        attend(lc)

    @pl.when(j >= lc // blk)
    def _():
        attend(k_ref.shape[0])


def _ga(q, k, v, nb, s_len, lc):
    t = q.shape[0]
    blk = GA_SUB * GA_TQ
    nq = s_len // blk
    return pl.pallas_call(
        functools.partial(_ga_kernel, lc=lc),
        out_shape=jax.ShapeDtypeStruct((t, GROUP_W), F32),
        grid=(nb, nq),
        in_specs=[pl.BlockSpec((blk, N_HEADS * LANES), lambda b, j: (b * nq + j, 0)),
                  pl.BlockSpec((s_len, LANES), lambda b, j: (b, 0)),
                  pl.BlockSpec((s_len, LANES), lambda b, j: (b, 0))],
        out_specs=pl.BlockSpec((blk, GROUP_W), lambda b, j: (b * nq + j, 0)),
        compiler_params=_cp(("parallel", "arbitrary")),
        name="global_attn",
    )(q, k, v)


WA_SUB = TM // TQ


def _wa_kernel(sink_ref, q_ref, k_ref, v_ref, o_ref, *, lc):
    s_len = k_ref.shape[0]
    kc = k_ref[0:lc, :]
    vc = v_ref[0:lc, :]
    lane_c = lax.broadcasted_iota(jnp.int32, vc.shape, 1)
    lane_b = lax.broadcasted_iota(jnp.int32, (3 * TQ, LANES), 1)
    row = lax.broadcasted_iota(jnp.int32, (2 * TQ, 1), 0)
    for sub in range(WA_SUB):
        rows = slice(sub * TQ, (sub + 1) * TQ)
        n = pl.program_id(1) * WA_SUB + sub - lc // TQ
        start = pl.multiple_of(jnp.clip(lc + (n - 1) * TQ, lc, s_len - 3 * TQ), TQ)
        kb = k_ref[pl.ds(start, 3 * TQ), :]
        vb = v_ref[pl.ds(start, 3 * TQ), :]
        qpos = n * TQ + lax.broadcasted_iota(jnp.int32, (TQ, 3 * TQ), 0)
        kpos = (start - lc) + lax.broadcasted_iota(jnp.int32, (TQ, 3 * TQ), 1)
        reach = jnp.where(n >= 0, WINDOW, -1)
        valid = jnp.abs(qpos - kpos) <= reach
        valid = jnp.concatenate([valid, valid], axis=0)
        outs = []
        for kvh in range(2):
            q2 = jnp.concatenate([q_ref[rows, (2 * kvh) * LANES:(2 * kvh + 1) * LANES],
                                  q_ref[rows, (2 * kvh + 1) * LANES:(2 * kvh + 2) * LANES]], axis=0)
            sc = _dot_nt(q2, kc)
            sb = jnp.where(valid, _dot_nt(q2, kb), NEG_INF)
            sink = jnp.where(row < TQ, sink_ref[2 * kvh], sink_ref[2 * kvh + 1]) * LOG2E
            m = jnp.maximum(jnp.maximum(jnp.max(sc, axis=1, keepdims=True), jnp.max(sb, axis=1, keepdims=True)), sink)
            pc = jnp.exp2((sc - m).astype(BF16))
            pb = jnp.exp2((sb - m).astype(BF16))
            own_c = (lane_c < HEAD_DIM) if kvh == 0 else (lane_c >= HEAD_DIM)
            own_b = (lane_b < HEAD_DIM) if kvh == 0 else (lane_b >= HEAD_DIM)
            o2 = _dot(pc, jnp.where(own_c, vc, jnp.ones_like(vc))) + _dot(pb, jnp.where(own_b, vb, jnp.ones_like(vb)))
            denom = pltpu.roll(o2, HEAD_DIM, 1) + jnp.exp2(sink - m)
            outs.append(_merge_heads(o2 / denom, kvh))
        o_ref[rows, :] = jnp.concatenate(outs, axis=1)


def _wa(sink, q, k, v, nb, s_len, lc):
    t = q.shape[0]
    nq = s_len // TM
    return pl.pallas_call(
        functools.partial(_wa_kernel, lc=lc),
        out_shape=jax.ShapeDtypeStruct((t, GROUP_W), F32),
        grid=(nb, nq),
        in_specs=[pl.BlockSpec(memory_space=pltpu.SMEM),
                  pl.BlockSpec((TM, N_HEADS * LANES), lambda b, j: (b * nq + j, 0)),
                  pl.BlockSpec((s_len, LANES), lambda b, j: (b, 0)),
                  pl.BlockSpec((s_len, LANES), lambda b, j: (b, 0))],
        out_specs=pl.BlockSpec((TM, GROUP_W), lambda b, j: (b * nq + j, 0)),
        compiler_params=_cp(("parallel", "arbitrary")),
        name="window_attn",
    )(sink, q, k, v)


def _s5_chunk_index(t, rev, nc_ctx, nc_tot):
    if not rev:
        return t
    return jnp.where(t < nc_ctx, nc_ctx - 1 - t, nc_tot - 1 - (t - nc_ctx))


def _s5_kernel(u_ref, m_ref, p_ref, g_ref, ar_ref, ai_ref, dsk_ref, y_ref, s_scr, h_scr, *, nb, nc_ctx, nc_tot):
    uf = u_ref[0]
    u = uf.astype(BF16)
    for d in range(2):
        for k in range(2):
            s_scr[d, k] = _dot(u, p_ref[d, k, 0])
    ar = [jnp.broadcast_to(ar_ref[d, 0], (nb, LANES)) for d in range(2)]
    ai = [[jnp.broadcast_to(ai_ref[d, k, 0], (nb, LANES)) for k in range(2)] for d in range(2)]

    def body(t, carry):
        out = []
        for d in range(2):
            h, hs = carry[d]
            rows = pl.ds(_s5_chunk_index(t, d == 1, nc_ctx, nc_tot), nb, stride=nc_tot)
            h_scr[d, rows, :] = h
            out.append((ar[d] * h + ai[d][0] * hs + s_scr[d, 0, rows, :],
                        ar[d] * hs + ai[d][1] * h + s_scr[d, 1, rows, :]))
        return tuple(out)

    zero = jnp.zeros((nb, LANES), F32)
    lax.fori_loop(0, nc_tot, body, ((zero, zero), (zero, zero)), unroll=2)
    y = uf * dsk_ref[0]
    for d in range(2):
        y = y + _dot(u, m_ref[d, 0]) + _dot(h_scr[d].astype(BF16), g_ref[d, 0])
    y_ref[0] = y


S5_TB = TM // S5_Q
S5_GPS = LANES // S5_CH


def _s5_pack_kernel(lo_ref, hi_ref, o_ref):
    for s in range(S5_Q):
        rows = pl.ds(s, S5_TB, stride=S5_Q)
        halves = (lo_ref[rows, :], hi_ref[rows, :])
        dst = S5_CH * (s % S5_GPS)
        for g in range(S5_GROUPS):
            slab = halves[g // S5_GPS]
            src = S5_CH * (g % S5_GPS)
            moved = slab if src == dst else pltpu.roll(slab, (dst - src) % LANES, 1)
            o_ref[g, :, s * S5_CH:(s + 1) * S5_CH] = moved[:, dst:dst + S5_CH]


def _s5_unpack_kernel(y_ref, o_ref):
    lane_grp = lax.broadcasted_iota(jnp.int32, (S5_TB, LANES), 1) // S5_CH
    for s in range(S5_Q):
        src = S5_CH * (s % S5_GPS)
        for half in range(S5_GROUPS // S5_GPS):
            acc = None
            for gl in range(S5_GPS):
                slab = y_ref[half * S5_GPS + gl, :, (s // S5_GPS) * LANES:(s // S5_GPS + 1) * LANES]
                dst = S5_CH * gl
                moved = slab if src == dst else pltpu.roll(slab, (dst - src) % LANES, 1)
                acc = moved if acc is None else jnp.where(lane_grp == gl, moved, acc)
            o_ref[half, pl.ds(s, S5_TB, stride=S5_Q), :] = acc


def _s5_params(lam_re, lam_im, log_dt, b_re, b_im, c_re, c_im, d_skip):
    q = S5_Q
    dt = jnp.exp(log_dt)[..., None]
    lr, li = lam_re, lam_im
    mag = jnp.exp(lr * dt)
    a_re = mag * jnp.cos(li * dt)
    a_im = mag * jnp.sin(li * dt)
    den = lr * lr + li * li
    f_re = ((a_re - 1.0) * lr + a_im * li) / den
    f_im = (a_im * lr - (a_re - 1.0) * li) / den
    bb_re = f_re[..., None] * b_re - f_im[..., None] * b_im
    bb_im = f_re[..., None] * b_im + f_im[..., None] * b_re
    kk = jnp.arange(q + 1, dtype=F32)[:, None, None, None]
    pmag = jnp.exp(kk * (lr * dt))
    pw_re = pmag * jnp.cos(kk * (li * dt))
    pw_im = pmag * jnp.sin(kk * (li * dt))
    lw_re = pw_re[:q].transpose(1, 2, 0, 3)[:, :, :, None, :]
    lw_im = pw_im[:q].transpose(1, 2, 0, 3)[:, :, :, None, :]
    ck_re = c_re[:, :, None] * lw_re - c_im[:, :, None] * lw_im
    ck_im = c_re[:, :, None] * lw_im + c_im[:, :, None] * lw_re
    ck = jnp.concatenate([ck_re, -ck_im], axis=-1).reshape(2, S5_GROUPS, S5_BLK, 2 * S5_STATE)
    kern_t = jnp.einsum("dgmp,dgpc->dgcm", ck, jnp.concatenate([bb_re, bb_im], axis=2), precision=HI)
    kern_t = kern_t.reshape(2, S5_GROUPS, S5_CH, q, S5_CH)
    zeros = jnp.zeros_like(kern_t)
    bbt_re = bb_re.transpose(0, 1, 3, 2)[:, :, None]
    bbt_im = bb_im.transpose(0, 1, 3, 2)[:, :, None]
    ct_re = c_re.transpose(0, 1, 3, 2)[:, :, :, None, :]
    ct_im = c_im.transpose(0, 1, 3, 2)[:, :, :, None, :]
    ms, ps, gs = [], [], []
    for d in range(2):
        ext = (jnp.concatenate([zeros[d], kern_t[d]], axis=2) if d == 0
               else jnp.concatenate([kern_t[d, :, :, ::-1], zeros[d]], axis=2))
        ext = ext.reshape(S5_GROUPS, S5_CH, 2 * S5_BLK)
        lo = [(q - s) if d == 0 else (q - 1 - s) for s in range(q)]
        md = jnp.stack([ext[:, :, a * S5_CH:a * S5_CH + S5_BLK] for a in lo], axis=1)
        ms.append(md.reshape(S5_GROUPS, S5_BLK, S5_BLK))
        pidx = (q - 1 - jnp.arange(q)) if d == 0 else jnp.arange(q)
        pr = pw_re[pidx, d].transpose(1, 0, 2)[:, :, None, :]
        pi = pw_im[pidx, d].transpose(1, 0, 2)[:, :, None, :]
        p_re = pr * bbt_re[d] - pi * bbt_im[d]
        p_im = pr * bbt_im[d] + pi * bbt_re[d]
        pd = jnp.stack([jnp.concatenate([p_re, p_im], axis=3), jnp.concatenate([p_im, p_re], axis=3)])
        ps.append(pd.reshape(2, S5_GROUPS, S5_BLK, 2 * S5_STATE))
        gidx = (jnp.arange(q) + 1) if d == 0 else (q - jnp.arange(q))
        gw_re = pw_re[gidx, d].transpose(1, 2, 0)[..., None]
        gw_im = pw_im[gidx, d].transpose(1, 2, 0)[..., None]
        g_re = ct_re[d] * gw_re - ct_im[d] * gw_im
        g_im = ct_re[d] * gw_im + ct_im[d] * gw_re
        gs.append(jnp.concatenate([g_re, -g_im], axis=1).reshape(S5_GROUPS, 2 * S5_STATE, S5_BLK))
    ar = jnp.concatenate([pw_re[q], pw_re[q]], axis=-1)[:, :, None, :]
    ai = jnp.stack([jnp.concatenate([-pw_im[q], pw_im[q]], axis=-1),
                    jnp.concatenate([pw_im[q], -pw_im[q]], axis=-1)], axis=1)[:, :, :, None, :]
    dsk = jnp.tile(d_skip.reshape(S5_GROUPS, 1, S5_CH), (1, 1, q))
    return (jnp.stack(ms).astype(BF16), jnp.stack(ps).astype(BF16), jnp.stack(gs).astype(BF16),
            ar.astype(F32), ai.astype(F32), dsk.astype(F32))


def _s5(ug, params, nb, s_len, lc):
    m, p, g, ar, ai, dsk = params
    nc_tot = s_len // S5_Q
    nc_ctx = lc // S5_Q
    r = nb * nc_tot
    return pl.pallas_call(
        functools.partial(_s5_kernel, nb=nb, nc_ctx=nc_ctx, nc_tot=nc_tot),
        out_shape=jax.ShapeDtypeStruct((S5_GROUPS, r, S5_BLK), F32),
        grid=(S5_GROUPS,),
        in_specs=[pl.BlockSpec((1, r, S5_BLK), lambda gi: (gi, 0, 0)),
                  pl.BlockSpec((2, 1, S5_BLK, S5_BLK), lambda gi: (0, gi, 0, 0)),
                  pl.BlockSpec((2, 2, 1, S5_BLK, 2 * S5_STATE), lambda gi: (0, 0, gi, 0, 0)),
                  pl.BlockSpec((2, 1, 2 * S5_STATE, S5_BLK), lambda gi: (0, gi, 0, 0)),
                  pl.BlockSpec((2, 1, 1, 2 * S5_STATE), lambda gi: (0, gi, 0, 0)),
                  pl.BlockSpec((2, 2, 1, 1, 2 * S5_STATE), lambda gi: (0, 0, gi, 0, 0)),
                  pl.BlockSpec((1, 1, S5_BLK), lambda gi: (gi, 0, 0))],
        out_specs=pl.BlockSpec((1, r, S5_BLK), lambda gi: (gi, 0, 0)),
        scratch_shapes=[pltpu.VMEM((2, 2, r, 2 * S5_STATE), F32), pltpu.VMEM((2, r, 2 * S5_STATE), F32)],
        compiler_params=_cp(("parallel",)),
        name="s5_scan",
    )(ug, m, p, g, ar, ai, dsk)


def _conv_kernel(x_ref, prev_ref, next_ref, w_ref, b_ref, o_ref, *, nblk):
    i = pl.program_id(0) % nblk
    x = x_ref[...]
    rows = x.shape[0]
    ridx = lax.broadcasted_iota(jnp.int32, x.shape, 0)
    prev_row = jnp.where(i <= 1, 0.0, prev_ref[SUBLANES - 1:SUBLANES, :])
    next_row = jnp.where(jnp.logical_or(i == 0, i == nblk - 1), 0.0, next_ref[0:1, :])
    xm = jnp.where(ridx == 0, prev_row, pltpu.roll(x, 1, 0))
    xp = jnp.where(ridx == rows - 1, next_row, pltpu.roll(x, rows - 1, 0))
    y = xm * w_ref[0:1, :] + x * w_ref[1:2, :] + xp * w_ref[2:3, :] + b_ref[...]
    o_ref[...] = _silu(y)


def _conv(xbc, w, b, nblk):
    t, c = xbc.shape
    per = TM // SUBLANES
    last = t // SUBLANES - 1
    return pl.pallas_call(
        functools.partial(_conv_kernel, nblk=nblk),
        out_shape=jax.ShapeDtypeStruct((t, c), F32),
        grid=(t // TM,),
        in_specs=[pl.BlockSpec((TM, c), lambda i: (i, 0)),
                  pl.BlockSpec((SUBLANES, c), lambda i: (jnp.maximum(i * per - 1, 0), 0)),
                  pl.BlockSpec((SUBLANES, c), lambda i: (jnp.minimum((i + 1) * per, last), 0)),
                  pl.BlockSpec((3, c), lambda i: (0, 0)),
                  pl.BlockSpec((1, c), lambda i: (0, 0))],
        out_specs=pl.BlockSpec((TM, c), lambda i: (i, 0)),
        compiler_params=_cp(("parallel",)),
        name="ssd_conv",
    )(xbc, xbc, xbc, w, b.reshape(1, c))


_X_B = GROUP_W
_X_C = GROUP_W + SSD_NGROUPS * SSD_STATE


def _ssd_kernel(xf_ref, dtf_ref, dttf_ref, xr_ref, dtr_ref, dttr_ref, bias_ref, a_ref, biast_ref, at_ref, dsk_ref,
                yf_ref, yr_ref, stf_ref, str_ref):
    @pl.when(pl.program_id(1) == 0)
    def _():
        stf_ref[...] = jnp.zeros_like(stf_ref)
        str_ref[...] = jnp.zeros_like(str_ref)

    par = (bias_ref[...], a_ref[...], biast_ref[...], at_ref[...], dsk_ref[...])
    for j in range(SSD_SUB):
        rf = slice(j * TQ, (j + 1) * TQ)
        yf_ref[rf, :] = _ssd_chunk_step(xf_ref[rf, :], dtf_ref[rf, :], dttf_ref[0, :, rf], par, stf_ref, False)
        rr = slice((SSD_SUB - 1 - j) * TQ, (SSD_SUB - j) * TQ)
        yr_ref[rr, :] = _ssd_chunk_step(xr_ref[rr, :], dtr_ref[rr, :], dttr_ref[0, :, rr], par, str_ref, True)


def _ssd_chunk_step(xc, dt_raw, dtt_raw, par, st_ref, rev):
    bias, a_vec, biast, at_vec, dsk = par
    base = SSD_HEADS if rev else 0
    x = xc[:, 0:GROUP_W]
    dt = _softplus(dt_raw + bias)
    a = dt * a_vec
    dtt = _softplus(dtt_raw + biast)
    at = dtt * at_vec
    ri = lax.broadcasted_iota(jnp.int32, (TQ, TQ), 0)
    ci = lax.broadcasted_iota(jnp.int32, (TQ, TQ), 1)
    causal = (ci >= ri) if rev else (ri >= ci)
    tri = jnp.where(causal, 1.0, 0.0)
    cum_c = _dot_hi(tri, a)
    cum_r = _dot_nt_hi(at, tri)
    edge = 0 if rev else TQ - 1
    tot = cum_c[edge:edge + 1, :]

    shape = (TQ, GROUP_W)
    xdt = x * _per_head_cols(dt, base, SSD_HEADS, shape)
    lane = lax.broadcasted_iota(jnp.int32, shape, 1)
    y = jnp.zeros(shape, F32)
    bmat = [xc[:, _X_B + g * SSD_STATE:_X_B + (g + 1) * SSD_STATE].astype(BF16) for g in range(SSD_NGROUPS)]
    cmat = [xc[:, _X_C + g * SSD_STATE:_X_C + (g + 1) * SSD_STATE].astype(BF16) for g in range(SSD_NGROUPS)]
    cb = [_dot_nt(cmat[g], bmat[g]) for g in range(SSD_NGROUPS)]
    for h in range(SSD_HEADS):
        col = base + h
        seg = jnp.where(causal, cum_c[:, col:col + 1] - cum_r[col:col + 1, :], NEG_INF)
        scores = cb[h // 2] * jnp.exp(seg)
        xh = jnp.where((lane >= h * HEAD_DIM) & (lane < (h + 1) * HEAD_DIM), xdt, 0.0)
        y = y + _dot(scores.astype(BF16), xh.astype(BF16))
    st = st_ref[...]
    yo = jnp.concatenate(
        [_dot_nt(cmat[g], st[g * SSD_STATE:(g + 1) * SSD_STATE].astype(BF16)) for g in range(SSD_NGROUPS)], axis=1)
    y = y + yo * _per_head_cols(jnp.exp(cum_c), base, SSD_HEADS, shape)
    if not rev:
        y = y + x * dsk
    xd = xdt * _per_head_cols(jnp.exp(tot - cum_c), base, SSD_HEADS, shape)
    xdt_t = xd.T.astype(BF16)
    decay = jnp.exp(tot)
    for g in range(SSD_NGROUPS):
        new = _dot(xdt_t[g * SSD_STATE:(g + 1) * SSD_STATE], bmat[g])
        for hh in range(2):
            h = 2 * g + hh
            r0 = h * HEAD_DIM
            st_ref[r0:r0 + HEAD_DIM, :] = (decay[:, base + h:base + h + 1] * st[r0:r0 + HEAD_DIM]
                                           + new[hh * HEAD_DIM:(hh + 1) * HEAD_DIM])
    return y


def _dot_nt_hi(a, b):
    return lax.dot_general(a, b, (((1,), (1,)), ((), ())), preferred_element_type=F32, precision=HI)


def _ssd_chunk(c, rev, nc_ctx, nc_tot):
    if not rev:
        return c
    return jnp.where(c < nc_ctx, nc_ctx - 1 - c, nc_tot - 1 - (c - nc_ctx))


SSD_SUB = TM // TQ


def _ssd_scan(xc, dt, dtt, bias, a, biast, at, dsk, nb, s_len, lc):
    t = xc.shape[0]
    nblk = s_len // TM
    nctx = lc // TM
    fix = lambda b, c: (0, 0)

    def rows(rev):
        return lambda b, c: (b * nblk + _ssd_chunk(c, rev, nctx, nblk), 0)

    def lanes(rev):
        return lambda b, c: (b, 0, _ssd_chunk(c, rev, nctx, nblk))

    def data_specs(rev):
        return [pl.BlockSpec((TM, SSD_XBC), rows(rev)), pl.BlockSpec((TM, LANES), rows(rev)),
                pl.BlockSpec((1, SUBLANES, TM), lanes(rev))]

    state = pltpu.VMEM((SSD_HEADS * HEAD_DIM, SSD_STATE), F32)
    return pl.pallas_call(
        _ssd_kernel,
        out_shape=[jax.ShapeDtypeStruct((t, GROUP_W), F32)] * 2,
        grid=(nb, nblk),
        in_specs=data_specs(False) + data_specs(True) + [
            pl.BlockSpec((1, LANES), fix), pl.BlockSpec((1, LANES), fix),
            pl.BlockSpec((SUBLANES, TQ), fix), pl.BlockSpec((SUBLANES, TQ), fix),
            pl.BlockSpec((1, GROUP_W), fix)],
        out_specs=[pl.BlockSpec((TM, GROUP_W), rows(False)), pl.BlockSpec((TM, GROUP_W), rows(True))],
        scratch_shapes=[state, state],
        compiler_params=_cp(("parallel", "arbitrary")),
        name="ssd_scan",
    )(xc, dt, dtt, xc, dt, dtt, bias, a, biast, at, dsk)


def _ssd(xbc, dt, conv_w, conv_b, dt_bias, a_log, d_skip, nb, s_len, lc):
    nblk = s_len // TM
    xc = _conv(xbc, conv_w, conv_b, nblk)
    nd = 2 * SSD_HEADS
    dtt = dt[:, :nd].reshape(nb, s_len, nd).transpose(0, 2, 1)
    bias = jnp.pad(dt_bias.reshape(1, nd), ((0, 0), (0, LANES - nd)))
    a = jnp.pad(-jnp.exp(a_log).reshape(1, nd), ((0, 0), (0, LANES - nd)))
    biast = jnp.broadcast_to(dt_bias.reshape(nd, 1), (nd, TQ))
    at = jnp.broadcast_to(-jnp.exp(a_log).reshape(nd, 1), (nd, TQ))
    dsk = jnp.repeat(d_skip, HEAD_DIM).reshape(1, GROUP_W)
    return _ssd_scan(xc, dt, dtt, bias, a, biast, at, dsk, nb, s_len, lc)


def _outproj_kernel(x_ref, ys5_ref, oga_ref, y0_ref, y1_ref, z_ref, owa_ref, mod_ref, gluw_ref, glub_ref,
                    ng_ref, wout_ref, n2_ref, wr_ref, br_ref, xn_o, h2_o, route_o, y_scr):
    _s5_unpack_kernel(ys5_ref, y_scr)
    gl = _gelu_tanh(jnp.concatenate([y_scr[0], y_scr[1]], axis=1))
    a = gl * _sigmoid(_dot(gl.astype(BF16), gluw_ref[...]) + glub_ref[...])
    m = (y0_ref[...] + y1_ref[...]) * _silu(z_ref[...])
    m = m * lax.rsqrt(jnp.mean(m * m, axis=-1, keepdims=True) + EPS) * ng_ref[...]
    w = wout_ref
    mix = (_dot(a.astype(BF16), w[0:GROUP_W, :]) + _dot(oga_ref[...].astype(BF16), w[GROUP_W:2 * GROUP_W, :])
           + _dot(m.astype(BF16), w[2 * GROUP_W:3 * GROUP_W, :]) + _dot(owa_ref[...].astype(BF16), w[3 * GROUP_W:, :]))
    xn = x_ref[...] + mod_ref[0, 2:3, :] * mix
    xn_o[...] = xn
    h2 = xn * lax.rsqrt(jnp.mean(xn * xn, axis=-1, keepdims=True) + EPS) * n2_ref[...]
    h2 = h2 * (1.0 + mod_ref[0, 4:5, :]) + mod_ref[0, 3:4, :]
    h2_o[...] = _pack_bf16_pair(h2)
    h_hi = h2.astype(BF16)
    h_lo = (h2 - h_hi.astype(F32)).astype(BF16)
    logits = _dot(h_hi, wr_ref[0]) + (_dot(h_lo, wr_ref[0]) + _dot(h_hi, wr_ref[1])) + br_ref[...]
    lane = lax.broadcasted_iota(jnp.int32, logits.shape, 1).astype(F32)
    big = float(4 * LANES)
    lcoarse = jnp.where(lane < MOE_GROUPS, logits, NEG_INF)
    mx = jnp.max(lcoarse, axis=1, keepdims=True)
    den = jnp.sum(jnp.exp(lcoarse - mx), axis=1, keepdims=True)
    grp = jnp.min(jnp.where(lcoarse == mx, lane, big), axis=1, keepdims=True)
    pg = 1.0 / den
    lo = ROUTE_FINE0 + grp * MOE_PER_GROUP
    lf = jnp.where(lane >= lo, jnp.where(lane < lo + MOE_PER_GROUP, logits, NEG_INF), NEG_INF)
    v1 = jnp.max(lf, axis=1, keepdims=True)
    i1 = jnp.min(jnp.where(lf == v1, lane, big), axis=1, keepdims=True)
    lf2 = jnp.where(lane == i1, NEG_INF, lf)
    v2 = jnp.max(lf2, axis=1, keepdims=True)
    i2 = jnp.min(jnp.where(lf2 == v2, lane, big), axis=1, keepdims=True)
    e2 = jnp.exp(v2 - v1)
    w1 = pg / (1.0 + e2)
    w2 = w1 * e2
    route = jnp.where(lane == 0, i1 - ROUTE_FINE0,
                      jnp.where(lane == 1, i2 - ROUTE_FINE0,
                                jnp.where(lane == 2, w1, jnp.where(lane == 3, w2, 0.0))))
    route_o[...] = route


def _outproj(x, ys5, oga, y0, y1, z, owa, mod, glu_w, glu_b, ssd_norm_g, w_out, norm2_g, wr, br, nb, nblk):
    t, d = x.shape
    row = lambda i: (i, 0)
    fix = lambda i: (0, 0)
    gw = pl.BlockSpec((TM, GROUP_W), row)
    wr_hi = wr.astype(BF16)
    return pl.pallas_call(
        _outproj_kernel,
        out_shape=[jax.ShapeDtypeStruct((t, d), F32), jax.ShapeDtypeStruct((t, d // 2), jnp.uint32),
                   jax.ShapeDtypeStruct((t, LANES), F32)],
        grid=(t // TM,),
        in_specs=[pl.BlockSpec((TM, d), row), pl.BlockSpec((S5_GROUPS, S5_TB, S5_BLK), lambda i: (0, i, 0)),
                  gw, gw, gw, gw, gw,
                  pl.BlockSpec((1, 6, d), lambda i: (_mod_row(i, nblk, nb), 0, 0)),
                  pl.BlockSpec((GROUP_W, GROUP_W), fix),
                  pl.BlockSpec((1, GROUP_W), fix),
                  pl.BlockSpec((1, GROUP_W), fix),
                  pl.BlockSpec((d, d), fix),
                  pl.BlockSpec((1, d), fix),
                  pl.BlockSpec((2, d, LANES), lambda i: (0, 0, 0)),
                  pl.BlockSpec((1, LANES), fix)],
        out_specs=[pl.BlockSpec((TM, d), row), pl.BlockSpec((TM, d // 2), row), pl.BlockSpec((TM, LANES), row)],
        scratch_shapes=[pltpu.VMEM((GROUP_W // LANES, TM, LANES), F32)],
        compiler_params=_cp(("parallel",)),
        name="out_proj_router",
    )(x, ys5, oga, y0, y1, z, owa, mod, glu_w.astype(BF16), glu_b.reshape(1, -1), ssd_norm_g.reshape(1, -1),
      w_out.astype(BF16), norm2_g.reshape(1, -1), jnp.stack([wr_hi, (wr - wr_hi.astype(F32)).astype(BF16)]), br)


def _pack_router(coarse_w, coarse_b, fine_w, fine_b):
    def lanes(coarse, fine):
        gap = jnp.zeros(coarse.shape[:-1] + (ROUTE_FINE0 - MOE_GROUPS,), F32)
        tail = jnp.zeros(coarse.shape[:-1] + (LANES - ROUTE_FINE0 - N_EXPERTS,), F32)
        return jnp.concatenate([coarse, gap, fine, tail], axis=-1)

    return lanes(coarse_w, fine_w), lanes(coarse_b[None, :], fine_b[None, :])


def _gather_rows(src, idx):
    m = idx.shape[0]
    d = src.shape[1]
    workers = SC_CORES * SC_SUBCORES
    nch = m // (workers * SC_GATHER_K)
    assert nch * workers * SC_GATHER_K == m
    mesh = plsc.VectorSubcoreMesh(core_axis_name="c", subcore_axis_name="s")

    @functools.partial(
        pl.kernel, mesh=mesh,
        out_type=jax.ShapeDtypeStruct((m, d), src.dtype),
        scratch_types=[pltpu.VMEM((nch, SC_GATHER_K), jnp.int32),
                       pltpu.VMEM((SC_GATHER_K, d), src.dtype),
                       pltpu.SemaphoreType.DMA],
    )
    def gather(src_hbm, idx_hbm, out_hbm, idx_v, rows_v, sem):
        wid = lax.axis_index("s") * SC_CORES + lax.axis_index("c")
        pltpu.sync_copy(idx_hbm.at[wid], idx_v)

        @pl.loop(0, nch)
        def _(j):
            off = pl.multiple_of((wid * nch + j) * SC_GATHER_K, SC_GATHER_K)
            pltpu.async_copy(src_hbm.at[idx_v.at[j]], rows_v, sem).wait()
            pltpu.sync_copy(rows_v, out_hbm.at[pl.ds(off, SC_GATHER_K)])

    return gather(src, idx.reshape(workers, nch, SC_GATHER_K))


def _scatter_rows(src, dst0, dst1, nrows):
    t, d = src.shape
    workers = SC_CORES * SC_SUBCORES
    nch = t // (workers * SC_GATHER_K)
    assert nch * workers * SC_GATHER_K == t
    mesh = plsc.VectorSubcoreMesh(core_axis_name="c", subcore_axis_name="s")

    @functools.partial(
        pl.kernel, mesh=mesh,
        out_type=jax.ShapeDtypeStruct((nrows, d), src.dtype),
        scratch_types=[pltpu.VMEM((nch, SC_GATHER_K), jnp.int32),
                       pltpu.VMEM((nch, SC_GATHER_K), jnp.int32),
                       pltpu.VMEM((SC_GATHER_K, d), src.dtype)],
    )
    def scatter(src_hbm, d0_hbm, d1_hbm, out_hbm, i0_v, i1_v, rows_v):
        wid = lax.axis_index("s") * SC_CORES + lax.axis_index("c")
        pltpu.sync_copy(d0_hbm.at[wid], i0_v)
        pltpu.sync_copy(d1_hbm.at[wid], i1_v)

        @pl.loop(0, nch)
        def _(j):
            off = pl.multiple_of((wid * nch + j) * SC_GATHER_K, SC_GATHER_K)
            pltpu.sync_copy(src_hbm.at[pl.ds(off, SC_GATHER_K)], rows_v)
            pltpu.sync_copy(rows_v, out_hbm.at[i0_v.at[j]])
            pltpu.sync_copy(rows_v, out_hbm.at[i1_v.at[j]])

    return scatter(src, dst0.reshape(workers, nch, SC_GATHER_K), dst1.reshape(workers, nch, SC_GATHER_K))


def _expert_kernel(be_ref, nused_ref, nvalid_ref, x_ref, wg_ref, wu_ref, wd_ref, o_ref, wg_s, wu_s, wd_s):
    i = pl.program_id(0)
    new_expert = jnp.logical_or(i == 0, be_ref[i] != be_ref[jnp.maximum(i - 1, 0)])

    @pl.when(jnp.logical_and(i < nused_ref[0], new_expert))
    def _():
        wg_s[...] = wg_ref[0, 0].astype(BF16)
        wu_s[...] = wu_ref[0, 0].astype(BF16)
        wd_s[...] = wd_ref[0, 0].astype(BF16)

    @pl.when(i < nused_ref[0])
    def _():
        row = lax.broadcasted_iota(jnp.int32, x_ref.shape, 0)
        lo, hi = _unpack_bf16_pair(jnp.where(row < nvalid_ref[i], x_ref[...], jnp.uint32(0)))
        lo = lo.astype(BF16)
        hi = hi.astype(BF16)
        half = lo.shape[1]
        gate = _dot(lo, wg_s[0:half, :]) + _dot(hi, wg_s[half:, :])
        up = _dot(lo, wu_s[0:half, :]) + _dot(hi, wu_s[half:, :])
        o_ref[...] = _pack_bf16_pair(_dot((_silu(gate) * up).astype(BF16), wd_s[...]))

    @pl.when(i >= nused_ref[0])
    def _():
        o_ref[...] = jnp.zeros_like(o_ref)


def _experts(xs, blk_e, n_used, n_valid, wg, wu, wd, layer):
    rows, dp = xs.shape
    d = 2 * dp
    nblocks = rows // MOE_TM
    de = wg.shape[3]
    wsel = lambda i, be, nu, nv: (layer, be[i], 0, 0)
    grid_spec = pltpu.PrefetchScalarGridSpec(
        num_scalar_prefetch=3,
        grid=(nblocks,),
        in_specs=[pl.BlockSpec((MOE_TM, dp), lambda i, be, nu, nv: (i, 0)),
                  pl.BlockSpec((1, 1, d, de), wsel),
                  pl.BlockSpec((1, 1, d, de), wsel),
                  pl.BlockSpec((1, 1, de, d), wsel)],
        out_specs=pl.BlockSpec((MOE_TM, dp), lambda i, be, nu, nv: (i, 0)),
        scratch_shapes=[pltpu.VMEM((d, de), BF16), pltpu.VMEM((d, de), BF16), pltpu.VMEM((de, d), BF16)],
    )
    return pl.pallas_call(
        _expert_kernel,
        out_shape=jax.ShapeDtypeStruct((rows, dp), jnp.uint32),
        grid_spec=grid_spec,
        compiler_params=_cp(("arbitrary",)),
        name="moe_experts",
    )(blk_e, n_used, n_valid, xs, wg, wu, wd)


def _final_kernel(x_ref, r0_ref, r1_ref, route_ref, mod_ref, fg_ref, o_ref):
    y = _moe_residual(x_ref, r0_ref, r1_ref, route_ref, mod_ref)
    o_ref[...] = y * lax.rsqrt(jnp.mean(y * y, axis=-1, keepdims=True) + EPS) * fg_ref[...]


def _final(xn, rows2, route, mod, final_g, nb, nblk):
    t, d = xn.shape
    nlat = nblk - 1
    grid = (nb * nlat,)
    src = lambda i: ((i // nlat) * nblk + 1 + i % nlat, 0)
    modi = lambda i: (i // nlat, 0, 0)
    out_rows = nb * nlat * TM
    return pl.pallas_call(
        _final_kernel,
        out_shape=jax.ShapeDtypeStruct((out_rows, d), F32),
        grid=grid,
        in_specs=[pl.BlockSpec((TM, d), src),
                  pl.BlockSpec((TM, d // 2), src),
                  pl.BlockSpec((TM, d // 2), lambda i: (src(i)[0] + t // TM, 0)),
                  pl.BlockSpec((TM, LANES), src),
                  pl.BlockSpec((1, 6, d), modi),
                  pl.BlockSpec((1, d), lambda i: (0, 0))],
        out_specs=pl.BlockSpec((TM, d), lambda i: (i, 0)),
        compiler_params=_cp(("parallel",)),
        name="moe_combine_final",
    )(xn, rows2, rows2, route, mod, final_g.reshape(1, d))


def _moe(h2, route, wg, wu, wd, layer):
    t, d = h2.shape
    n_slots = 2 * t
    experts = jnp.arange(N_EXPERTS, dtype=F32)[None, :]
    oh0 = (route[:, 0:1] == experts).astype(F32)
    oh1 = (route[:, 1:2] == experts).astype(F32)
    both = (oh0 + oh1).reshape(t // LANES, LANES, N_EXPERTS)
    tri = jnp.tril(jnp.ones((LANES, LANES), F32))
    intra = jnp.einsum("ij,bjk->bik", tri, both)
    blk_tot = intra[:, -1, :]
    blk_cum = jnp.cumsum(blk_tot, axis=0)
    earlier = (intra - both + (blk_cum - blk_tot)[:, None, :]).reshape(t, N_EXPERTS)
    counts = blk_cum[-1].astype(jnp.int32)
    pcounts = (counts + MOE_TM - 1) // MOE_TM * MOE_TM
    pends = jnp.cumsum(pcounts)
    pstarts = pends - pcounts
    base = pstarts.astype(F32)[None, :] + earlier
    dest0 = jnp.sum(oh0 * base, axis=1).astype(jnp.int32)
    dest1 = jnp.sum(oh1 * base, axis=1).astype(jnp.int32)
    nblocks = -(-n_slots // MOE_TM) + N_EXPERTS
    nrows = -(-nblocks * MOE_TM // GATHER_ROWS) * GATHER_ROWS
    nblocks = nrows // MOE_TM
    blk_start = jnp.arange(nblocks, dtype=jnp.int32) * MOE_TM
    blk_e = jnp.minimum(jnp.sum((pends[None, :] <= blk_start[:, None]).astype(jnp.int32), axis=1), N_EXPERTS - 1)
    n_used = (pends[-1] // MOE_TM).astype(jnp.int32).reshape(1)
    n_valid = jnp.clip((pstarts + counts)[blk_e] - blk_start, 0, MOE_TM).astype(jnp.int32)
    xs = _scatter_rows(h2, dest0, dest1, nrows)
    ys = _experts(xs, blk_e, n_used, n_valid, wg, wu, wd, layer)
    return _gather_rows(ys, jnp.concatenate([dest0, dest1]))


def kernel(x, c, ctx, c_ctx, ada_w, ada_b, norm1_g, norm2_g, w_in, w_out, s5_lam_re, s5_lam_im, s5_log_dt, s5_b_re, s5_b_im, s5_c_re, s5_c_im, s5_d, s5_glu_w, s5_glu_b, ga_qn_g, ga_kn_g, ssd_conv_w, ssd_conv_b, ssd_dt_bias, ssd_a_log, ssd_d, ssd_norm_g, wa_sink, moe_coarse_w, moe_coarse_b, moe_fine_w, moe_fine_b, moe_w_gate, moe_w_up, moe_w_down, final_g):
    nb, l, d = x.shape
    lc = ctx.shape[1]
    depth = ada_w.shape[0]
    assert lc == TM and l % TM == 0 and nb <= SUBLANES - 1 and d == D_MODEL
    s_len = lc + l
    nblk = s_len // TM
    t = nb * s_len

    cc = jnp.zeros((SUBLANES, d), F32).at[:nb].set(c).at[nb].set(c_ctx)
    mods = _ada(cc, ada_w, ada_b).reshape(depth, SUBLANES, 6, d)
    cos_t, sin_t = _rope_tables(lc, l)
    w_packed = jax.vmap(_pack_w_in)(w_in)
    s5_tabs = jax.vmap(_s5_params)(s5_lam_re, s5_lam_im, s5_log_dt, s5_b_re, s5_b_im, s5_c_re, s5_c_im, s5_d)
    wrs, brs = jax.vmap(_pack_router)(moe_coarse_w, moe_coarse_b, moe_fine_w, moe_fine_b)

    src = ("first", x.reshape(nb * l, d), ctx.reshape(nb * lc, d))
    for i in range(depth):
        mod = mods[i]
        (xm, xbc, ug, z, dt, gaq, gak, gav, waq, wak, wav) = _inproj(
            src, mod, norm1_g[i], w_packed[i], cos_t, sin_t, ga_qn_g[i], ga_kn_g[i], nb, nblk)
        ys5 = _s5(ug, tuple(tab[i] for tab in s5_tabs), nb, s_len, lc)
        oga = _ga(gaq, gak, gav, nb, s_len, lc)
        y0, y1 = _ssd(xbc, dt, ssd_conv_w[i], ssd_conv_b[i], ssd_dt_bias[i], ssd_a_log[i], ssd_d[i], nb, s_len, lc)
        owa = _wa(wa_sink[i], waq, wak, wav, nb, s_len, lc)
        wr, br = wrs[i], brs[i]
        xn, h2, route = _outproj(xm, ys5, oga, y0, y1, z, owa, mod, s5_glu_w[i], s5_glu_b[i], ssd_norm_g[i],
                                 w_out[i], norm2_g[i], wr, br, nb, nblk)
        rows2 = _moe(h2, route, moe_w_gate, moe_w_up, moe_w_down, i)
        src = ("moe", xn, rows2, route, mod)
    return _final(xn, rows2, route, mod, final_g, nb, nblk).reshape(nb, l, d)
```

```python
import functools
import math

import jax
import jax.numpy as jnp
import numpy as np
from jax import lax
from jax.experimental import pallas as pl
from jax.experimental.pallas import tpu as pltpu
from jax.experimental.pallas import tpu_sc as plsc

F32 = jnp.float32
BF16 = jnp.bfloat16
HI = lax.Precision.HIGHEST

D_MODEL = 1024
GRID_W = 64
GROUP_W = 256
HEAD_DIM = 64
ROPE_FREQS = HEAD_DIM // 4
ROPE_BASE = 10000.0
EPS = 1e-6
S5_CH = 16
S5_GROUPS = GROUP_W // S5_CH
S5_STATE = 64
N_HEADS = 4
SSD_HEADS = 4
SSD_NGROUPS = 2
SSD_STATE = 128
SSD_XBC = GROUP_W + 2 * SSD_NGROUPS * SSD_STATE
WINDOW = 128
MOE_GROUPS = 4
MOE_PER_GROUP = 8
N_EXPERTS = 32
D_EXPERT = D_MODEL // 2

LANES = 128
SUBLANES = 8
TM = 256
TQ = 128
GA_TQ = 128
GA_SUB = 2
S5_Q = 32
S5_BLK = S5_Q * S5_CH
MOE_TM = 512
SC_CORES = 2
SC_SUBCORES = 16
SC_GATHER_K = 32
GATHER_ROWS = SC_CORES * SC_SUBCORES * SC_GATHER_K
ROUTE_FINE0 = 32
VMEM_LIMIT = 56 * 1024 * 1024

NEG_INF = float("-inf")
LOG2E = math.log2(math.e)


def _cp(sem, vmem=VMEM_LIMIT):
    return pltpu.CompilerParams(dimension_semantics=sem, vmem_limit_bytes=vmem)


def _dot(a, b):
    return jnp.dot(a, b, preferred_element_type=F32)


def _dot_hi(a, b):
    return jnp.dot(a, b, preferred_element_type=F32, precision=HI)


def _dot_nt(a, b):
    return lax.dot_general(a, b, (((1,), (1,)), ((), ())), preferred_element_type=F32)


def _sigmoid(x):
    return 1.0 / (1.0 + jnp.exp(-x))


def _silu(x):
    return x * _sigmoid(x)


def _gelu_tanh(x):
    return 0.5 * x * (1.0 + jnp.tanh(math.sqrt(2.0 / math.pi) * (x + 0.044715 * (x * x * x))))


def _softplus(x):
    return jnp.maximum(x, 0.0) + jnp.log1p(jnp.exp(-jnp.abs(x)))


_HI16 = 0xFFFF0000


def _pack_bf16_pair(x):
    n = x.shape[1] // 2
    bits = pltpu.bitcast(x.astype(BF16).astype(F32), jnp.uint32)
    return (bits[:, n:] & jnp.uint32(_HI16)) | (bits[:, :n] >> 16)


def _unpack_bf16_pair(w):
    return pltpu.bitcast(w << 16, F32), pltpu.bitcast(w & jnp.uint32(_HI16), F32)


def _per_head_cols(v, base, n_heads, shape):
    lane = lax.broadcasted_iota(jnp.int32, shape, 1)
    out = jnp.broadcast_to(v[:, base + n_heads - 1:base + n_heads], shape)
    for h in range(n_heads - 2, -1, -1):
        out = jnp.where(lane < (h + 1) * HEAD_DIM, v[:, base + h:base + h + 1], out)
    return out


def _ada_kernel(c_ref, w_ref, b_ref, o_ref):
    c = c_ref[...]
    o_ref[0] = _dot_hi(_silu(c), w_ref[0]) + b_ref[0]


def _ada(cc, ada_w, ada_b):
    depth, d, n = ada_w.shape
    tn = 1536
    return pl.pallas_call(
        _ada_kernel,
        out_shape=jax.ShapeDtypeStruct((depth, SUBLANES, n), F32),
        grid=(depth, n // tn),
        in_specs=[pl.BlockSpec((SUBLANES, d), lambda l, j: (0, 0)),
                  pl.BlockSpec((1, d, tn), lambda l, j: (l, 0, j)),
                  pl.BlockSpec((1, 1, tn), lambda l, j: (l, 0, j))],
        out_specs=pl.BlockSpec((1, SUBLANES, tn), lambda l, j: (l, 0, j)),
        compiler_params=_cp(("parallel", "parallel")),
        name="ada_mod",
    )(cc, ada_w, ada_b.reshape(depth, 1, n))


_C_XBC = 0
_C_U = _C_XBC + SSD_XBC
_C_Z = _C_U + GROUP_W
_C_DT = _C_Z + GROUP_W
_C_GAQ = _C_DT + LANES
_C_WAQ = _C_GAQ + N_HEADS * LANES
_C_GAK = _C_WAQ + N_HEADS * LANES
_C_GAV = _C_GAK + LANES
_C_WAK = _C_GAV + LANES
_C_WAV = _C_WAK + LANES
_C_END = _C_WAV + LANES


def _expand_q_cols(wq):
    zero = jnp.zeros((wq.shape[0], HEAD_DIM), wq.dtype)
    parts = []
    for h in range(N_HEADS):
        head = wq[:, h * HEAD_DIM:(h + 1) * HEAD_DIM]
        parts += [head, zero] if h // 2 == 0 else [zero, head]
    return jnp.concatenate(parts, axis=1)


def _pack_w_in(w_in):
    cuts = np.cumsum([256, 256, 128, 128, 256, SSD_XBC, 2 * SSD_HEADS, 256, 128, 128])[:-1]
    u, gaq, gak, gav, z, xbc, dt, waq, wak, wav = jnp.split(w_in, [int(c) for c in cuts], axis=1)
    dt = jnp.pad(dt, ((0, 0), (0, LANES - dt.shape[1])))
    w = jnp.concatenate([xbc, u, z, dt, _expand_q_cols(gaq), _expand_q_cols(waq), gak, gav, wak, wav], axis=1)
    return w.astype(BF16)


def _rope(x, cos, sins):
    w = x.shape[1]
    if w > LANES:
        cos = jnp.concatenate([cos] * (w // LANES), axis=1)
        sins = jnp.concatenate([sins] * (w // LANES), axis=1)
    lane = lax.broadcasted_iota(jnp.int32, x.shape, 1)
    up = pltpu.roll(x, w - ROPE_FREQS, 1)
    dn = pltpu.roll(x, ROPE_FREQS, 1)
    partner = jnp.where((lane & ROPE_FREQS) == 0, up, dn)
    return x * cos + partner * sins


def _moe_residual(xn_ref, r0_ref, r1_ref, route_ref, mod_ref):
    route = route_ref[...]
    r0 = jnp.concatenate(_unpack_bf16_pair(r0_ref[...]), axis=1)
    r1 = jnp.concatenate(_unpack_bf16_pair(r1_ref[...]), axis=1)
    return xn_ref[...] + mod_ref[0, 5:6, :] * (route[:, 2:3] * r0 + route[:, 3:4] * r1)


def _inproj_kernel(*refs, first, nblk):
    if first:
        lat_ref, ctx_ref = refs[:2]
        x = jnp.where(pl.program_id(0) % nblk == 0, ctx_ref[...], lat_ref[...])
        refs = refs[2:]
    else:
        x = _moe_residual(*refs[:5])
        refs = refs[5:]
    (mod_ref, g_ref, w_ref, cos_ref, sin_ref, qn_ref, kn_ref,
     xm_o, xbc_o, ug_o, z_o, dt_o, gaq_o, gak_o, gav_o, waq_o, wak_o, wav_o, u_scr) = refs
    xm_o[...] = x
    ms = jnp.mean(x * x, axis=-1, keepdims=True)
    xn = x * lax.rsqrt(ms + EPS) * g_ref[...]
    h = xn * (1.0 + mod_ref[0, 1:2, :]) + mod_ref[0, 0:1, :]
    hb = h.astype(BF16)

    def proj(lo, hi):
        return _dot(hb, w_ref[:, lo:hi])

    cos = cos_ref[...]
    sins = sin_ref[...]
    scale = LOG2E * HEAD_DIM ** -0.5
    q = proj(_C_GAQ, _C_WAQ)
    qs = q * q
    inv = jnp.concatenate(
        [jnp.broadcast_to(lax.rsqrt(jnp.sum(qs[:, s * LANES:(s + 1) * LANES], axis=1, keepdims=True)
                                    * (1.0 / HEAD_DIM) + EPS), (q.shape[0], LANES)) for s in range(N_HEADS)], axis=1)
    gaq_o[...] = (_rope(q * inv * qn_ref[...], cos, sins) * scale).astype(BF16)
    waq_o[...] = (_rope(proj(_C_WAQ, _C_GAK), cos, sins) * scale).astype(BF16)
    k = proj(_C_GAK, _C_GAV)
    ks = k * k
    lane = lax.broadcasted_iota(jnp.int32, k.shape, 1)
    lo = lane < HEAD_DIM
    ms0 = jnp.sum(jnp.where(lo, ks, 0.0), axis=1, keepdims=True)
    ms1 = jnp.sum(jnp.where(lo, 0.0, ks), axis=1, keepdims=True)
    kinv = lax.rsqrt(jnp.where(lo, ms0, ms1) * (1.0 / HEAD_DIM) + EPS)
    gak_o[...] = _rope(k * kinv * kn_ref[...], cos, sins).astype(BF16)
    gav_o[...] = proj(_C_GAV, _C_WAK).astype(BF16)
    wak_o[...] = _rope(proj(_C_WAK, _C_WAV), cos, sins).astype(BF16)
    wav_o[...] = proj(_C_WAV, _C_END).astype(BF16)
    xbc_o[...] = proj(_C_XBC, _C_U)
    u = proj(_C_U, _C_Z)
    u_scr[0] = u[:, :LANES]
    u_scr[1] = u[:, LANES:]
    _s5_pack_kernel(u_scr.at[0], u_scr.at[1], ug_o)
    z_o[...] = proj(_C_Z, _C_DT)
    dt_o[...] = proj(_C_DT, _C_GAQ)


def _mod_row(i, nblk, nb):
    return jnp.where(i % nblk == 0, nb, i // nblk)


def _inproj(src, mod, norm_g, w_packed, cos_t, sin_t, qn_g, kn_g, nb, nblk):
    first = src[0] == "first"
    d = src[1].shape[1]
    t = nb * nblk * TM
    row = lambda i: (i, 0)
    fix = lambda i: (0, 0)
    modspec = pl.BlockSpec((1, 6, d), lambda i: (_mod_row(i, nblk, nb), 0, 0))
    if first:
        src_specs = [pl.BlockSpec((TM, d), lambda i: ((i // nblk) * (nblk - 1) + jnp.maximum(i % nblk - 1, 0), 0)),
                     pl.BlockSpec((TM, d), lambda i: (i // nblk, 0))]
        src_args = src[1:]
    else:
        src_specs = [pl.BlockSpec((TM, d), row), pl.BlockSpec((TM, d // 2), row),
                     pl.BlockSpec((TM, d // 2), lambda i: (i + t // TM, 0)), pl.BlockSpec((TM, LANES), row), modspec]
        src_args = (src[1], src[2], src[2], src[3], src[4])
    outs = [(d, F32), (SSD_XBC, F32), None, (GROUP_W, F32), (LANES, F32),
            (N_HEADS * LANES, BF16), (LANES, BF16), (LANES, BF16),
            (N_HEADS * LANES, BF16), (LANES, BF16), (LANES, BF16)]
    shapes = [jax.ShapeDtypeStruct((t, o[0]), o[1]) if o else
              jax.ShapeDtypeStruct((S5_GROUPS, t // S5_Q, S5_BLK), F32) for o in outs]
    specs = [pl.BlockSpec((TM, o[0]), row) if o else
             pl.BlockSpec((S5_GROUPS, S5_TB, S5_BLK), lambda i: (0, i, 0)) for o in outs]
    return pl.pallas_call(
        functools.partial(_inproj_kernel, first=first, nblk=nblk),
        out_shape=shapes,
        grid=(t // TM,),
        in_specs=src_specs + [
                  modspec,
                  pl.BlockSpec((1, d), fix),
                  pl.BlockSpec((d, _C_END), fix),
                  pl.BlockSpec((TM, LANES), lambda i: (i % nblk, 0)),
                  pl.BlockSpec((TM, LANES), lambda i: (i % nblk, 0)),
                  pl.BlockSpec((1, N_HEADS * LANES), fix),
                  pl.BlockSpec((1, LANES), fix)],
        out_specs=specs,
        scratch_shapes=[pltpu.VMEM((GROUP_W // LANES, TM, LANES), F32)],
        compiler_params=_cp(("parallel",)),
        name="in_proj",
    )(*src_args, mod, norm_g.reshape(1, d), w_packed, cos_t, sin_t,
      jnp.tile(qn_g, 2 * N_HEADS).reshape(1, -1), jnp.tile(kn_g, 2).reshape(1, -1))


def _rope_tables(lc, l):
    n_rows = l // GRID_W
    rows = np.repeat(np.arange(n_rows), GRID_W)
    cols = np.tile(np.arange(GRID_W), n_rows)
    inv = np.power(np.float32(ROPE_BASE), -np.arange(ROPE_FREQS, dtype=np.float32) / ROPE_FREQS)
    ang = np.stack([rows, cols], axis=-1).astype(np.float32)[..., None] * inv
    cos = np.cos(ang)
    sin = np.sin(ang)
    cos64 = np.stack([cos, cos], axis=2).reshape(l, HEAD_DIM)
    sin64 = np.stack([-sin, sin], axis=2).reshape(l, HEAD_DIM)
    cos64 = np.concatenate([np.ones((lc, HEAD_DIM), np.float32), cos64], axis=0)
    sin64 = np.concatenate([np.zeros((lc, HEAD_DIM), np.float32), sin64], axis=0)
    return (jnp.asarray(np.tile(cos64, (1, 2)), dtype=F32), jnp.asarray(np.tile(sin64, (1, 2)), dtype=F32))


def _merge_heads(o2, kvh):
    tq = o2.shape[0] // 2
    oa, ob = o2[:tq], o2[tq:]
    lane = lax.broadcasted_iota(jnp.int32, oa.shape, 1)
    if kvh == 0:
        return jnp.where(lane < HEAD_DIM, oa, pltpu.roll(ob, HEAD_DIM, 1))
    return jnp.where(lane < HEAD_DIM, pltpu.roll(oa, HEAD_DIM, 1), ob)


def _stack_q(q_ref, rows, kvh):
    return jnp.concatenate([q_ref[rows, (2 * kvh) * LANES:(2 * kvh + 1) * LANES],
                            q_ref[rows, (2 * kvh + 1) * LANES:(2 * kvh + 2) * LANES]], axis=0)


def _ga_kernel(q_ref, k_ref, v_ref, o_ref, *, lc):
    j = pl.program_id(1)
    blk = GA_SUB * GA_TQ

    def attend(nkeys):
        k = k_ref[0:nkeys, :]
        v = v_ref[0:nkeys, :]
        lane = lax.broadcasted_iota(jnp.int32, v.shape, 1)
        vaug = [jnp.where((lane < HEAD_DIM) if kvh == 0 else (lane >= HEAD_DIM), v, jnp.ones_like(v))
                for kvh in range(2)]
        for sub in range(GA_SUB):
            rows = slice(sub * GA_TQ, (sub + 1) * GA_TQ)
            scores = [_dot_nt(_stack_q(q_ref, rows, kvh), k) for kvh in range(2)]
            outs = []
            for kvh in range(2):
                s = scores[kvh]
                p = jnp.exp2((s - jnp.max(s, axis=1, keepdims=True)).astype(BF16))
                o2 = _dot(p, vaug[kvh])
                outs.append(_merge_heads(o2 / pltpu.roll(o2, HEAD_DIM, 1), kvh))
            o_ref[rows, :] = jnp.concatenate(outs, axis=1)

    @pl.when(j < lc // blk)
    def _():
        attend(lc)

    @pl.when(j >= lc // blk)
    def _():
        attend(k_ref.shape[0])


def _ga(q, k, v, nb, s_len, lc):
    t = q.shape[0]
    blk = GA_SUB * GA_TQ
    nq = s_len // blk
    return pl.pallas_call(
        functools.partial(_ga_kernel, lc=lc),
        out_shape=jax.ShapeDtypeStruct((t, GROUP_W), F32),
        grid=(nb, nq),
        in_specs=[pl.BlockSpec((blk, N_HEADS * LANES), lambda b, j: (b * nq + j, 0)),
                  pl.BlockSpec((s_len, LANES), lambda b, j: (b, 0)),
                  pl.BlockSpec((s_len, LANES), lambda b, j: (b, 0))],
        out_specs=pl.BlockSpec((blk, GROUP_W), lambda b, j: (b * nq + j, 0)),
        compiler_params=_cp(("parallel", "arbitrary")),
        name="global_attn",
    )(q, k, v)


WA_SUB = TM // TQ


def _wa_kernel(sink_ref, q_ref, k_ref, v_ref, o_ref, *, lc):
    s_len = k_ref.shape[0]
    kc = k_ref[0:lc, :]
    vc = v_ref[0:lc, :]
    lane_c = lax.broadcasted_iota(jnp.int32, vc.shape, 1)
    lane_b = lax.broadcasted_iota(jnp.int32, (3 * TQ, LANES), 1)
    row = lax.broadcasted_iota(jnp.int32, (2 * TQ, 1), 0)
    for sub in range(WA_SUB):
        rows = slice(sub * TQ, (sub + 1) * TQ)
        n = pl.program_id(1) * WA_SUB + sub - lc // TQ
        start = pl.multiple_of(jnp.clip(lc + (n - 1) * TQ, lc, s_len - 3 * TQ), TQ)
        kb = k_ref[pl.ds(start, 3 * TQ), :]
        vb = v_ref[pl.ds(start, 3 * TQ), :]
        qpos = n * TQ + lax.broadcasted_iota(jnp.int32, (TQ, 3 * TQ), 0)
        kpos = (start - lc) + lax.broadcasted_iota(jnp.int32, (TQ, 3 * TQ), 1)
        reach = jnp.where(n >= 0, WINDOW, -1)
        valid = jnp.abs(qpos - kpos) <= reach
        valid = jnp.concatenate([valid, valid], axis=0)
        outs = []
        for kvh in range(2):
            q2 = jnp.concatenate([q_ref[rows, (2 * kvh) * LANES:(2 * kvh + 1) * LANES],
                                  q_ref[rows, (2 * kvh + 1) * LANES:(2 * kvh + 2) * LANES]], axis=0)
            sc = _dot_nt(q2, kc)
            sb = jnp.where(valid, _dot_nt(q2, kb), NEG_INF)
            sink = jnp.where(row < TQ, sink_ref[2 * kvh], sink_ref[2 * kvh + 1]) * LOG2E
            m = jnp.maximum(jnp.maximum(jnp.max(sc, axis=1, keepdims=True), jnp.max(sb, axis=1, keepdims=True)), sink)
            pc = jnp.exp2((sc - m).astype(BF16))
            pb = jnp.exp2((sb - m).astype(BF16))
            own_c = (lane_c < HEAD_DIM) if kvh == 0 else (lane_c >= HEAD_DIM)
            own_b = (lane_b < HEAD_DIM) if kvh == 0 else (lane_b >= HEAD_DIM)
            o2 = _dot(pc, jnp.where(own_c, vc, jnp.ones_like(vc))) + _dot(pb, jnp.where(own_b, vb, jnp.ones_like(vb)))
            denom = pltpu.roll(o2, HEAD_DIM, 1) + jnp.exp2(sink - m)
            outs.append(_merge_heads(o2 / denom, kvh))
        o_ref[rows, :] = jnp.concatenate(outs, axis=1)


def _wa(sink, q, k, v, nb, s_len, lc):
    t = q.shape[0]
    nq = s_len // TM
    return pl.pallas_call(
        functools.partial(_wa_kernel, lc=lc),
        out_shape=jax.ShapeDtypeStruct((t, GROUP_W), F32),
        grid=(nb, nq),
        in_specs=[pl.BlockSpec(memory_space=pltpu.SMEM),
                  pl.BlockSpec((TM, N_HEADS * LANES), lambda b, j: (b * nq + j, 0)),
                  pl.BlockSpec((s_len, LANES), lambda b, j: (b, 0)),
                  pl.BlockSpec((s_len, LANES), lambda b, j: (b, 0))],
        out_specs=pl.BlockSpec((TM, GROUP_W), lambda b, j: (b * nq + j, 0)),
        compiler_params=_cp(("parallel", "arbitrary")),
        name="window_attn",
    )(sink, q, k, v)


def _s5_chunk_index(t, rev, nc_ctx, nc_tot):
    if not rev:
        return t
    return jnp.where(t < nc_ctx, nc_ctx - 1 - t, nc_tot - 1 - (t - nc_ctx))


def _s5_kernel(u_ref, k_ref, p_ref, g_ref, ar_ref, ai_ref, dsk_ref, y_ref, s_scr, h_scr, m_scr, *, nb, nc_ctx, nc_tot):
    for d in range(2):
        ext = k_ref[d, 0]
        for s in range(S5_Q):
            lo = ((S5_Q - s) if d == 0 else (S5_Q - 1 - s)) * S5_CH
            win = pltpu.roll(ext, (2 * S5_BLK - lo) % (2 * S5_BLK), 1)[:, :S5_BLK]
            m_scr[d, s * S5_CH:(s + 1) * S5_CH, :] = win.astype(BF16)
    uf = u_ref[0]
    u = uf.astype(BF16)
    for d in range(2):
        for k in range(2):
            s_scr[d, k] = _dot(u, p_ref[d, k, 0])
    ar = [jnp.broadcast_to(ar_ref[d, 0], (nb, LANES)) for d in range(2)]
    ai = [[jnp.broadcast_to(ai_ref[d, k, 0], (nb, LANES)) for k in range(2)] for d in range(2)]

    def body(t, carry):
        out = []
        for d in range(2):
            h, hs = carry[d]
            rows = pl.ds(_s5_chunk_index(t, d == 1, nc_ctx, nc_tot), nb, stride=nc_tot)
            h_scr[d, rows, :] = h
            out.append((ar[d] * h + ai[d][0] * hs + s_scr[d, 0, rows, :],
                        ar[d] * hs + ai[d][1] * h + s_scr[d, 1, rows, :]))
        return tuple(out)

    zero = jnp.zeros((nb, LANES), F32)
    lax.fori_loop(0, nc_tot, body, ((zero, zero), (zero, zero)), unroll=2)
    y = uf * dsk_ref[0]
    for d in range(2):
        y = y + _dot(u, m_scr[d]) + _dot(h_scr[d].astype(BF16), g_ref[d, 0])
    y_ref[0] = y


S5_TB = TM // S5_Q
S5_GPS = LANES // S5_CH


def _s5_pack_kernel(lo_ref, hi_ref, o_ref):
    for s in range(S5_Q):
        rows = pl.ds(s, S5_TB, stride=S5_Q)
        halves = (lo_ref[rows, :], hi_ref[rows, :])
        dst = S5_CH * (s % S5_GPS)
        for g in range(S5_GROUPS):
            slab = halves[g // S5_GPS]
            src = S5_CH * (g % S5_GPS)
            moved = slab if src == dst else pltpu.roll(slab, (dst - src) % LANES, 1)
            o_ref[g, :, s * S5_CH:(s + 1) * S5_CH] = moved[:, dst:dst + S5_CH]


def _s5_unpack_kernel(y_ref, o_ref):
    lane_grp = lax.broadcasted_iota(jnp.int32, (S5_TB, LANES), 1) // S5_CH
    for s in range(S5_Q):
        src = S5_CH * (s % S5_GPS)
        for half in range(S5_GROUPS // S5_GPS):
            acc = None
            for gl in range(S5_GPS):
                slab = y_ref[half * S5_GPS + gl, :, (s // S5_GPS) * LANES:(s // S5_GPS + 1) * LANES]
                dst = S5_CH * gl
                moved = slab if src == dst else pltpu.roll(slab, (dst - src) % LANES, 1)
                acc = moved if acc is None else jnp.where(lane_grp == gl, moved, acc)
            o_ref[half, pl.ds(s, S5_TB, stride=S5_Q), :] = acc


def _s5_params(lam_re, lam_im, log_dt, b_re, b_im, c_re, c_im, d_skip):
    q = S5_Q
    dt = jnp.exp(log_dt)[..., None]
    lr, li = lam_re, lam_im
    mag = jnp.exp(lr * dt)
    a_re = mag * jnp.cos(li * dt)
    a_im = mag * jnp.sin(li * dt)
    den = lr * lr + li * li
    f_re = ((a_re - 1.0) * lr + a_im * li) / den
    f_im = (a_im * lr - (a_re - 1.0) * li) / den
    bb_re = f_re[..., None] * b_re - f_im[..., None] * b_im
    bb_im = f_re[..., None] * b_im + f_im[..., None] * b_re
    kk = jnp.arange(q + 1, dtype=F32)[:, None, None, None]
    pmag = jnp.exp(kk * (lr * dt))
    pw_re = pmag * jnp.cos(kk * (li * dt))
    pw_im = pmag * jnp.sin(kk * (li * dt))
    lw_re = pw_re[:q].transpose(1, 2, 0, 3)[:, :, :, None, :]
    lw_im = pw_im[:q].transpose(1, 2, 0, 3)[:, :, :, None, :]
    ck_re = c_re[:, :, None] * lw_re - c_im[:, :, None] * lw_im
    ck_im = c_re[:, :, None] * lw_im + c_im[:, :, None] * lw_re
    ck = jnp.concatenate([ck_re, -ck_im], axis=-1).reshape(2, S5_GROUPS, S5_BLK, 2 * S5_STATE)
    kern_t = jnp.einsum("dgmp,dgpc->dgcm", ck, jnp.concatenate([bb_re, bb_im], axis=2), precision=HI)
    kern_t = kern_t.reshape(2, S5_GROUPS, S5_CH, q, S5_CH)
    zeros = jnp.zeros_like(kern_t)
    bbt_re = bb_re.transpose(0, 1, 3, 2)[:, :, None]
    bbt_im = bb_im.transpose(0, 1, 3, 2)[:, :, None]
    ct_re = c_re.transpose(0, 1, 3, 2)[:, :, :, None, :]
    ct_im = c_im.transpose(0, 1, 3, 2)[:, :, :, None, :]
    ms, ps, gs = [], [], []
    for d in range(2):
        ext = (jnp.concatenate([zeros[d], kern_t[d]], axis=2) if d == 0
               else jnp.concatenate([kern_t[d, :, :, ::-1], zeros[d]], axis=2))
        ext = ext.reshape(S5_GROUPS, S5_CH, 2 * S5_BLK)
        ms.append(ext)
        pidx = (q - 1 - jnp.arange(q)) if d == 0 else jnp.arange(q)
        pr = pw_re[pidx, d].transpose(1, 0, 2)[:, :, None, :]
        pi = pw_im[pidx, d].transpose(1, 0, 2)[:, :, None, :]
        p_re = pr * bbt_re[d] - pi * bbt_im[d]
        p_im = pr * bbt_im[d] + pi * bbt_re[d]
        pd = jnp.stack([jnp.concatenate([p_re, p_im], axis=3), jnp.concatenate([p_im, p_re], axis=3)])
        ps.append(pd.reshape(2, S5_GROUPS, S5_BLK, 2 * S5_STATE))
        gidx = (jnp.arange(q) + 1) if d == 0 else (q - jnp.arange(q))
        gw_re = pw_re[gidx, d].transpose(1, 2, 0)[..., None]
        gw_im = pw_im[gidx, d].transpose(1, 2, 0)[..., None]
        g_re = ct_re[d] * gw_re - ct_im[d] * gw_im
        g_im = ct_re[d] * gw_im + ct_im[d] * gw_re
        gs.append(jnp.concatenate([g_re, -g_im], axis=1).reshape(S5_GROUPS, 2 * S5_STATE, S5_BLK))
    ar = jnp.concatenate([pw_re[q], pw_re[q]], axis=-1)[:, :, None, :]
    ai = jnp.stack([jnp.concatenate([-pw_im[q], pw_im[q]], axis=-1),
                    jnp.concatenate([pw_im[q], -pw_im[q]], axis=-1)], axis=1)[:, :, :, None, :]
    dsk = jnp.tile(d_skip.reshape(S5_GROUPS, 1, S5_CH), (1, 1, q))
    return (jnp.stack(ms), jnp.stack(ps).astype(BF16), jnp.stack(gs).astype(BF16),
            ar.astype(F32), ai.astype(F32), dsk.astype(F32))


def _s5(ug, params, nb, s_len, lc):
    m, p, g, ar, ai, dsk = params
    nc_tot = s_len // S5_Q
    nc_ctx = lc // S5_Q
    r = nb * nc_tot
    return pl.pallas_call(
        functools.partial(_s5_kernel, nb=nb, nc_ctx=nc_ctx, nc_tot=nc_tot),
        out_shape=jax.ShapeDtypeStruct((S5_GROUPS, r, S5_BLK), F32),
        grid=(S5_GROUPS,),
        in_specs=[pl.BlockSpec((1, r, S5_BLK), lambda gi: (gi, 0, 0)),
                  pl.BlockSpec((2, 1, S5_CH, 2 * S5_BLK), lambda gi: (0, gi, 0, 0)),
                  pl.BlockSpec((2, 2, 1, S5_BLK, 2 * S5_STATE), lambda gi: (0, 0, gi, 0, 0)),
                  pl.BlockSpec((2, 1, 2 * S5_STATE, S5_BLK), lambda gi: (0, gi, 0, 0)),
                  pl.BlockSpec((2, 1, 1, 2 * S5_STATE), lambda gi: (0, gi, 0, 0)),
                  pl.BlockSpec((2, 2, 1, 1, 2 * S5_STATE), lambda gi: (0, 0, gi, 0, 0)),
                  pl.BlockSpec((1, 1, S5_BLK), lambda gi: (gi, 0, 0))],
        out_specs=pl.BlockSpec((1, r, S5_BLK), lambda gi: (gi, 0, 0)),
        scratch_shapes=[pltpu.VMEM((2, 2, r, 2 * S5_STATE), F32), pltpu.VMEM((2, r, 2 * S5_STATE), F32),
                        pltpu.VMEM((2, S5_BLK, S5_BLK), BF16)],
        compiler_params=_cp(("parallel",)),
        name="s5_scan",
    )(ug, m, p, g, ar, ai, dsk)


def _conv_kernel(x_ref, prev_ref, next_ref, w_ref, b_ref, o_ref, *, nblk):
    i = pl.program_id(0) % nblk
    x = x_ref[...]
    rows = x.shape[0]
    ridx = lax.broadcasted_iota(jnp.int32, x.shape, 0)
    prev_row = jnp.where(i <= 1, 0.0, prev_ref[SUBLANES - 1:SUBLANES, :])
    next_row = jnp.where(jnp.logical_or(i == 0, i == nblk - 1), 0.0, next_ref[0:1, :])
    xm = jnp.where(ridx == 0, prev_row, pltpu.roll(x, 1, 0))
    xp = jnp.where(ridx == rows - 1, next_row, pltpu.roll(x, rows - 1, 0))
    y = xm * w_ref[0:1, :] + x * w_ref[1:2, :] + xp * w_ref[2:3, :] + b_ref[...]
    o_ref[...] = _silu(y)


def _conv(xbc, w, b, nblk):
    t, c = xbc.shape
    per = TM // SUBLANES
    last = t // SUBLANES - 1
    return pl.pallas_call(
        functools.partial(_conv_kernel, nblk=nblk),
        out_shape=jax.ShapeDtypeStruct((t, c), F32),
        grid=(t // TM,),
        in_specs=[pl.BlockSpec((TM, c), lambda i: (i, 0)),
                  pl.BlockSpec((SUBLANES, c), lambda i: (jnp.maximum(i * per - 1, 0), 0)),
                  pl.BlockSpec((SUBLANES, c), lambda i: (jnp.minimum((i + 1) * per, last), 0)),
                  pl.BlockSpec((3, c), lambda i: (0, 0)),
                  pl.BlockSpec((1, c), lambda i: (0, 0))],
        out_specs=pl.BlockSpec((TM, c), lambda i: (i, 0)),
        compiler_params=_cp(("parallel",)),
        name="ssd_conv",
    )(xbc, xbc, xbc, w, b.reshape(1, c))


_X_B = GROUP_W
_X_C = GROUP_W + SSD_NGROUPS * SSD_STATE


def _ssd_kernel(xf_ref, dtf_ref, dttf_ref, xr_ref, dtr_ref, dttr_ref, bias_ref, a_ref, biast_ref, at_ref, dsk_ref,
                yf_ref, yr_ref, stf_ref, str_ref):
    @pl.when(pl.program_id(1) == 0)
    def _():
        stf_ref[...] = jnp.zeros_like(stf_ref)
        str_ref[...] = jnp.zeros_like(str_ref)

    par = (bias_ref[...], a_ref[...], biast_ref[...], at_ref[...], dsk_ref[...])
    for j in range(SSD_SUB):
        rf = slice(j * TQ, (j + 1) * TQ)
        yf_ref[rf, :] = _ssd_chunk_step(xf_ref[rf, :], dtf_ref[rf, :], dttf_ref[0, :, rf], par, stf_ref, False)
        rr = slice((SSD_SUB - 1 - j) * TQ, (SSD_SUB - j) * TQ)
        yr_ref[rr, :] = _ssd_chunk_step(xr_ref[rr, :], dtr_ref[rr, :], dttr_ref[0, :, rr], par, str_ref, True)


def _ssd_chunk_step(xc, dt_raw, dtt_raw, par, st_ref, rev):
    bias, a_vec, biast, at_vec, dsk = par
    base = SSD_HEADS if rev else 0
    x = xc[:, 0:GROUP_W]
    dt = _softplus(dt_raw + bias)
    a = dt * a_vec
    dtt = _softplus(dtt_raw + biast)
    at = dtt * at_vec
    ri = lax.broadcasted_iota(jnp.int32, (TQ, TQ), 0)
    ci = lax.broadcasted_iota(jnp.int32, (TQ, TQ), 1)
    causal = (ci >= ri) if rev else (ri >= ci)
    tri = jnp.where(causal, 1.0, 0.0)
    cum_c = _dot_hi(tri, a)
    cum_r = _dot_nt_hi(at, tri)
    edge = 0 if rev else TQ - 1
    tot = cum_c[edge:edge + 1, :]

    shape = (TQ, GROUP_W)
    xdt = x * _per_head_cols(dt, base, SSD_HEADS, shape)
    lane = lax.broadcasted_iota(jnp.int32, shape, 1)
    y = jnp.zeros(shape, F32)
    bmat = [xc[:, _X_B + g * SSD_STATE:_X_B + (g + 1) * SSD_STATE].astype(BF16) for g in range(SSD_NGROUPS)]
    cmat = [xc[:, _X_C + g * SSD_STATE:_X_C + (g + 1) * SSD_STATE].astype(BF16) for g in range(SSD_NGROUPS)]
    cb = [_dot_nt(cmat[g], bmat[g]) for g in range(SSD_NGROUPS)]
    for h in range(SSD_HEADS):
        col = base + h
        seg = jnp.where(causal, cum_c[:, col:col + 1] - cum_r[col:col + 1, :], NEG_INF)
        scores = cb[h // 2] * jnp.exp(seg)
        xh = jnp.where((lane >= h * HEAD_DIM) & (lane < (h + 1) * HEAD_DIM), xdt, 0.0)
        y = y + _dot(scores.astype(BF16), xh.astype(BF16))
    st = st_ref[...]
    yo = jnp.concatenate(
        [_dot_nt(cmat[g], st[g * SSD_STATE:(g + 1) * SSD_STATE].astype(BF16)) for g in range(SSD_NGROUPS)], axis=1)
    y = y + yo * _per_head_cols(jnp.exp(cum_c), base, SSD_HEADS, shape)
    if not rev:
        y = y + x * dsk
    xd = xdt * _per_head_cols(jnp.exp(tot - cum_c), base, SSD_HEADS, shape)
    xdt_t = xd.T.astype(BF16)
    decay = jnp.exp(tot)
    for g in range(SSD_NGROUPS):
        new = _dot(xdt_t[g * SSD_STATE:(g + 1) * SSD_STATE], bmat[g])
        for hh in range(2):
            h = 2 * g + hh
            r0 = h * HEAD_DIM
            st_ref[r0:r0 + HEAD_DIM, :] = (decay[:, base + h:base + h + 1] * st[r0:r0 + HEAD_DIM]
                                           + new[hh * HEAD_DIM:(hh + 1) * HEAD_DIM])
    return y


def _dot_nt_hi(a, b):
    return lax.dot_general(a, b, (((1,), (1,)), ((), ())), preferred_element_type=F32, precision=HI)


def _ssd_chunk(c, rev, nc_ctx, nc_tot):
    if not rev:
        return c
    return jnp.where(c < nc_ctx, nc_ctx - 1 - c, nc_tot - 1 - (c - nc_ctx))


SSD_SUB = TM // TQ


def _ssd_scan(xc, dt, dtt, bias, a, biast, at, dsk, nb, s_len, lc):
    t = xc.shape[0]
    nblk = s_len // TM
    nctx = lc // TM
    fix = lambda b, c: (0, 0)

    def rows(rev):
        return lambda b, c: (b * nblk + _ssd_chunk(c, rev, nctx, nblk), 0)

    def lanes(rev):
        return lambda b, c: (b, 0, _ssd_chunk(c, rev, nctx, nblk))

    def data_specs(rev):
        return [pl.BlockSpec((TM, SSD_XBC), rows(rev)), pl.BlockSpec((TM, LANES), rows(rev)),
                pl.BlockSpec((1, SUBLANES, TM), lanes(rev))]

    state = pltpu.VMEM((SSD_HEADS * HEAD_DIM, SSD_STATE), F32)
    return pl.pallas_call(
        _ssd_kernel,
        out_shape=[jax.ShapeDtypeStruct((t, GROUP_W), F32)] * 2,
        grid=(nb, nblk),
        in_specs=data_specs(False) + data_specs(True) + [
            pl.BlockSpec((1, LANES), fix), pl.BlockSpec((1, LANES), fix),
            pl.BlockSpec((SUBLANES, TQ), fix), pl.BlockSpec((SUBLANES, TQ), fix),
            pl.BlockSpec((1, GROUP_W), fix)],
        out_specs=[pl.BlockSpec((TM, GROUP_W), rows(False)), pl.BlockSpec((TM, GROUP_W), rows(True))],
        scratch_shapes=[state, state],
        compiler_params=_cp(("parallel", "arbitrary")),
        name="ssd_scan",
    )(xc, dt, dtt, xc, dt, dtt, bias, a, biast, at, dsk)


def _ssd(xbc, dt, conv_w, conv_b, dt_bias, a_log, d_skip, nb, s_len, lc):
    nblk = s_len // TM
    xc = _conv(xbc, conv_w, conv_b, nblk)
    nd = 2 * SSD_HEADS
    dtt = dt[:, :nd].reshape(nb, s_len, nd).transpose(0, 2, 1)
    bias = jnp.pad(dt_bias.reshape(1, nd), ((0, 0), (0, LANES - nd)))
    a = jnp.pad(-jnp.exp(a_log).reshape(1, nd), ((0, 0), (0, LANES - nd)))
    biast = jnp.broadcast_to(dt_bias.reshape(nd, 1), (nd, TQ))
    at = jnp.broadcast_to(-jnp.exp(a_log).reshape(nd, 1), (nd, TQ))
    dsk = jnp.repeat(d_skip, HEAD_DIM).reshape(1, GROUP_W)
    return _ssd_scan(xc, dt, dtt, bias, a, biast, at, dsk, nb, s_len, lc)


def _outproj_kernel(x_ref, ys5_ref, oga_ref, y0_ref, y1_ref, z_ref, owa_ref, mod_ref, gluw_ref, glub_ref,
                    ng_ref, wout_ref, n2_ref, wr_ref, br_ref, xn_o, h2_o, route_o, y_scr):
    _s5_unpack_kernel(ys5_ref, y_scr)
    gl = _gelu_tanh(jnp.concatenate([y_scr[0], y_scr[1]], axis=1))
    a = gl * _sigmoid(_dot(gl.astype(BF16), gluw_ref[...]) + glub_ref[...])
    m = (y0_ref[...] + y1_ref[...]) * _silu(z_ref[...])
    m = m * lax.rsqrt(jnp.mean(m * m, axis=-1, keepdims=True) + EPS) * ng_ref[...]
    w = wout_ref
    mix = (_dot(a.astype(BF16), w[0:GROUP_W, :]) + _dot(oga_ref[...].astype(BF16), w[GROUP_W:2 * GROUP_W, :])
           + _dot(m.astype(BF16), w[2 * GROUP_W:3 * GROUP_W, :]) + _dot(owa_ref[...].astype(BF16), w[3 * GROUP_W:, :]))
    xn = x_ref[...] + mod_ref[0, 2:3, :] * mix
    xn_o[...] = xn
    h2 = xn * lax.rsqrt(jnp.mean(xn * xn, axis=-1, keepdims=True) + EPS) * n2_ref[...]
    h2 = h2 * (1.0 + mod_ref[0, 4:5, :]) + mod_ref[0, 3:4, :]
    h2_o[...] = _pack_bf16_pair(h2)
    h_hi = h2.astype(BF16)
    h_lo = (h2 - h_hi.astype(F32)).astype(BF16)
    logits = _dot(h_hi, wr_ref[0]) + (_dot(h_lo, wr_ref[0]) + _dot(h_hi, wr_ref[1])) + br_ref[...]
    lane = lax.broadcasted_iota(jnp.int32, logits.shape, 1).astype(F32)
    big = float(4 * LANES)
    lcoarse = jnp.where(lane < MOE_GROUPS, logits, NEG_INF)
    mx = jnp.max(lcoarse, axis=1, keepdims=True)
    den = jnp.sum(jnp.exp(lcoarse - mx), axis=1, keepdims=True)
    grp = jnp.min(jnp.where(lcoarse == mx, lane, big), axis=1, keepdims=True)
    pg = 1.0 / den
    lo = ROUTE_FINE0 + grp * MOE_PER_GROUP
    lf = jnp.where(lane >= lo, jnp.where(lane < lo + MOE_PER_GROUP, logits, NEG_INF), NEG_INF)
    v1 = jnp.max(lf, axis=1, keepdims=True)
    i1 = jnp.min(jnp.where(lf == v1, lane, big), axis=1, keepdims=True)
    lf2 = jnp.where(lane == i1, NEG_INF, lf)
    v2 = jnp.max(lf2, axis=1, keepdims=True)
    i2 = jnp.min(jnp.where(lf2 == v2, lane, big), axis=1, keepdims=True)
    e2 = jnp.exp(v2 - v1)
    w1 = pg / (1.0 + e2)
    w2 = w1 * e2
    route = jnp.where(lane == 0, i1 - ROUTE_FINE0,
                      jnp.where(lane == 1, i2 - ROUTE_FINE0,
                                jnp.where(lane == 2, w1, jnp.where(lane == 3, w2, 0.0))))
    route_o[...] = route


def _outproj(x, ys5, oga, y0, y1, z, owa, mod, glu_w, glu_b, ssd_norm_g, w_out, norm2_g, wr, br, nb, nblk):
    t, d = x.shape
    row = lambda i: (i, 0)
    fix = lambda i: (0, 0)
    gw = pl.BlockSpec((TM, GROUP_W), row)
    wr_hi = wr.astype(BF16)
    return pl.pallas_call(
        _outproj_kernel,
        out_shape=[jax.ShapeDtypeStruct((t, d), F32), jax.ShapeDtypeStruct((t, d // 2), jnp.uint32),
                   jax.ShapeDtypeStruct((t, LANES), F32)],
        grid=(t // TM,),
        in_specs=[pl.BlockSpec((TM, d), row), pl.BlockSpec((S5_GROUPS, S5_TB, S5_BLK), lambda i: (0, i, 0)),
                  gw, gw, gw, gw, gw,
                  pl.BlockSpec((1, 6, d), lambda i: (_mod_row(i, nblk, nb), 0, 0)),
                  pl.BlockSpec((GROUP_W, GROUP_W), fix),
                  pl.BlockSpec((1, GROUP_W), fix),
                  pl.BlockSpec((1, GROUP_W), fix),
                  pl.BlockSpec((d, d), fix),
                  pl.BlockSpec((1, d), fix),
                  pl.BlockSpec((2, d, LANES), lambda i: (0, 0, 0)),
                  pl.BlockSpec((1, LANES), fix)],
        out_specs=[pl.BlockSpec((TM, d), row), pl.BlockSpec((TM, d // 2), row), pl.BlockSpec((TM, LANES), row)],
        scratch_shapes=[pltpu.VMEM((GROUP_W // LANES, TM, LANES), F32)],
        compiler_params=_cp(("parallel",)),
        name="out_proj_router",
    )(x, ys5, oga, y0, y1, z, owa, mod, glu_w.astype(BF16), glu_b.reshape(1, -1), ssd_norm_g.reshape(1, -1),
      w_out.astype(BF16), norm2_g.reshape(1, -1), jnp.stack([wr_hi, (wr - wr_hi.astype(F32)).astype(BF16)]), br)


def _pack_router(coarse_w, coarse_b, fine_w, fine_b):
    def lanes(coarse, fine):
        gap = jnp.zeros(coarse.shape[:-1] + (ROUTE_FINE0 - MOE_GROUPS,), F32)
        tail = jnp.zeros(coarse.shape[:-1] + (LANES - ROUTE_FINE0 - N_EXPERTS,), F32)
        return jnp.concatenate([coarse, gap, fine, tail], axis=-1)

    return lanes(coarse_w, fine_w), lanes(coarse_b[None, :], fine_b[None, :])


def _gather_rows(src, idx):
    m = idx.shape[0]
    d = src.shape[1]
    workers = SC_CORES * SC_SUBCORES
    nch = m // (workers * SC_GATHER_K)
    assert nch * workers * SC_GATHER_K == m
    mesh = plsc.VectorSubcoreMesh(core_axis_name="c", subcore_axis_name="s")

    @functools.partial(
        pl.kernel, mesh=mesh,
        out_type=jax.ShapeDtypeStruct((m, d), src.dtype),
        scratch_types=[pltpu.VMEM((nch, SC_GATHER_K), jnp.int32),
                       pltpu.VMEM((SC_GATHER_K, d), src.dtype),
                       pltpu.SemaphoreType.DMA],
    )
    def gather(src_hbm, idx_hbm, out_hbm, idx_v, rows_v, sem):
        wid = lax.axis_index("s") * SC_CORES + lax.axis_index("c")
        pltpu.sync_copy(idx_hbm.at[wid], idx_v)

        @pl.loop(0, nch)
        def _(j):
            off = pl.multiple_of((wid * nch + j) * SC_GATHER_K, SC_GATHER_K)
            pltpu.async_copy(src_hbm.at[idx_v.at[j]], rows_v, sem).wait()
            pltpu.sync_copy(rows_v, out_hbm.at[pl.ds(off, SC_GATHER_K)])

    return gather(src, idx.reshape(workers, nch, SC_GATHER_K))


def _scatter_rows(src, dst0, dst1, nrows):
    t, d = src.shape
    workers = SC_CORES * SC_SUBCORES
    nch = t // (workers * SC_GATHER_K)
    assert nch * workers * SC_GATHER_K == t
    mesh = plsc.VectorSubcoreMesh(core_axis_name="c", subcore_axis_name="s")

    @functools.partial(
        pl.kernel, mesh=mesh,
        out_type=jax.ShapeDtypeStruct((nrows, d), src.dtype),
        scratch_types=[pltpu.VMEM((nch, SC_GATHER_K), jnp.int32),
                       pltpu.VMEM((nch, SC_GATHER_K), jnp.int32),
                       pltpu.VMEM((SC_GATHER_K, d), src.dtype)],
    )
    def scatter(src_hbm, d0_hbm, d1_hbm, out_hbm, i0_v, i1_v, rows_v):
        wid = lax.axis_index("s") * SC_CORES + lax.axis_index("c")
        pltpu.sync_copy(d0_hbm.at[wid], i0_v)
        pltpu.sync_copy(d1_hbm.at[wid], i1_v)

        @pl.loop(0, nch)
        def _(j):
            off = pl.multiple_of((wid * nch + j) * SC_GATHER_K, SC_GATHER_K)
            pltpu.sync_copy(src_hbm.at[pl.ds(off, SC_GATHER_K)], rows_v)
            pltpu.sync_copy(rows_v, out_hbm.at[i0_v.at[j]])
            pltpu.sync_copy(rows_v, out_hbm.at[i1_v.at[j]])

    return scatter(src, dst0.reshape(workers, nch, SC_GATHER_K), dst1.reshape(workers, nch, SC_GATHER_K))


def _expert_kernel(be_ref, nused_ref, nvalid_ref, x_ref, wg_ref, wu_ref, wd_ref, o_ref, wg_s, wu_s, wd_s):
    i = pl.program_id(0)
    new_expert = jnp.logical_or(i == 0, be_ref[i] != be_ref[jnp.maximum(i - 1, 0)])

    @pl.when(jnp.logical_and(i < nused_ref[0], new_expert))
    def _():
        wg_s[...] = wg_ref[0, 0].astype(BF16)
        wu_s[...] = wu_ref[0, 0].astype(BF16)
        wd_s[...] = wd_ref[0, 0].astype(BF16)

    def swiglu(rows):
        row = rows.start + lax.broadcasted_iota(jnp.int32, (rows.stop - rows.start, x_ref.shape[1]), 0)
        lo, hi = _unpack_bf16_pair(jnp.where(row < nvalid_ref[i], x_ref[rows, :], jnp.uint32(0)))
        lo = lo.astype(BF16)
        hi = hi.astype(BF16)
        half = lo.shape[1]
        gate = _dot(lo, wg_s[0:half, :]) + _dot(hi, wg_s[half:, :])
        up = _dot(lo, wu_s[0:half, :]) + _dot(hi, wu_s[half:, :])
        o_ref[rows, :] = _pack_bf16_pair(_dot((_silu(gate) * up).astype(BF16), wd_s[...]))

    used = i < nused_ref[0]
    half_rows = MOE_TM // 2

    @pl.when(jnp.logical_and(used, nvalid_ref[i] > half_rows))
    def _():
        swiglu(slice(0, MOE_TM))

    @pl.when(jnp.logical_and(used, nvalid_ref[i] <= half_rows))
    def _():
        swiglu(slice(0, half_rows))
        o_ref[half_rows:, :] = jnp.zeros((MOE_TM - half_rows, o_ref.shape[1]), o_ref.dtype)

    @pl.when(jnp.logical_not(used))
    def _():
        o_ref[...] = jnp.zeros_like(o_ref)


def _experts(xs, blk_e, n_used, n_valid, wg, wu, wd, layer):
    rows, dp = xs.shape
    d = 2 * dp
    nblocks = rows // MOE_TM
    de = wg.shape[3]
    wsel = lambda i, be, nu, nv: (layer, be[i], 0, 0)
    grid_spec = pltpu.PrefetchScalarGridSpec(
        num_scalar_prefetch=3,
        grid=(nblocks,),
        in_specs=[pl.BlockSpec((MOE_TM, dp), lambda i, be, nu, nv: (i, 0)),
                  pl.BlockSpec((1, 1, d, de), wsel),
                  pl.BlockSpec((1, 1, d, de), wsel),
                  pl.BlockSpec((1, 1, de, d), wsel)],
        out_specs=pl.BlockSpec((MOE_TM, dp), lambda i, be, nu, nv: (i, 0)),
        scratch_shapes=[pltpu.VMEM((d, de), BF16), pltpu.VMEM((d, de), BF16), pltpu.VMEM((de, d), BF16)],
    )
    return pl.pallas_call(
        _expert_kernel,
        out_shape=jax.ShapeDtypeStruct((rows, dp), jnp.uint32),
        grid_spec=grid_spec,
        compiler_params=_cp(("arbitrary",)),
        name="moe_experts",
    )(blk_e, n_used, n_valid, xs, wg, wu, wd)


def _final_kernel(x_ref, r0_ref, r1_ref, route_ref, mod_ref, fg_ref, o_ref):
    y = _moe_residual(x_ref, r0_ref, r1_ref, route_ref, mod_ref)
    o_ref[...] = y * lax.rsqrt(jnp.mean(y * y, axis=-1, keepdims=True) + EPS) * fg_ref[...]


def _final(xn, rows2, route, mod, final_g, nb, nblk):
    t, d = xn.shape
    nlat = nblk - 1
    grid = (nb * nlat,)
    src = lambda i: ((i // nlat) * nblk + 1 + i % nlat, 0)
    modi = lambda i: (i // nlat, 0, 0)
    out_rows = nb * nlat * TM
    return pl.pallas_call(
        _final_kernel,
        out_shape=jax.ShapeDtypeStruct((out_rows, d), F32),
        grid=grid,
        in_specs=[pl.BlockSpec((TM, d), src),
                  pl.BlockSpec((TM, d // 2), src),
                  pl.BlockSpec((TM, d // 2), lambda i: (src(i)[0] + t // TM, 0)),
                  pl.BlockSpec((TM, LANES), src),
                  pl.BlockSpec((1, 6, d), modi),
                  pl.BlockSpec((1, d), lambda i: (0, 0))],
        out_specs=pl.BlockSpec((TM, d), lambda i: (i, 0)),
        compiler_params=_cp(("parallel",)),
        name="moe_combine_final",
    )(xn, rows2, rows2, route, mod, final_g.reshape(1, d))


def _moe(h2, route, wg, wu, wd, layer):
    t, d = h2.shape
    n_slots = 2 * t
    experts = jnp.arange(N_EXPERTS, dtype=F32)[None, :]
    oh0 = (route[:, 0:1] == experts).astype(F32)
    oh1 = (route[:, 1:2] == experts).astype(F32)
    both = (oh0 + oh1).reshape(t // LANES, LANES, N_EXPERTS)
    tri = jnp.tril(jnp.ones((LANES, LANES), F32))
    intra = jnp.einsum("ij,bjk->bik", tri, both)
    blk_tot = intra[:, -1, :]
    blk_cum = jnp.cumsum(blk_tot, axis=0)
    earlier = (intra - both + (blk_cum - blk_tot)[:, None, :]).reshape(t, N_EXPERTS)
    counts = blk_cum[-1].astype(jnp.int32)
    pcounts = (counts + MOE_TM - 1) // MOE_TM * MOE_TM
    pends = jnp.cumsum(pcounts)
    pstarts = pends - pcounts
    base = pstarts.astype(F32)[None, :] + earlier
    dest0 = jnp.sum(oh0 * base, axis=1).astype(jnp.int32)
    dest1 = jnp.sum(oh1 * base, axis=1).astype(jnp.int32)
    nblocks = -(-n_slots // MOE_TM) + N_EXPERTS
    nrows = -(-nblocks * MOE_TM // GATHER_ROWS) * GATHER_ROWS
    nblocks = nrows // MOE_TM
    blk_start = jnp.arange(nblocks, dtype=jnp.int32) * MOE_TM
    blk_e = jnp.minimum(jnp.sum((pends[None, :] <= blk_start[:, None]).astype(jnp.int32), axis=1), N_EXPERTS - 1)
    n_used = (pends[-1] // MOE_TM).astype(jnp.int32).reshape(1)
    n_valid = jnp.clip((pstarts + counts)[blk_e] - blk_start, 0, MOE_TM).astype(jnp.int32)
    xs = _scatter_rows(h2, dest0, dest1, nrows)
    ys = _experts(xs, blk_e, n_used, n_valid, wg, wu, wd, layer)
    return _gather_rows(ys, jnp.concatenate([dest0, dest1]))


def kernel(x, c, ctx, c_ctx, ada_w, ada_b, norm1_g, norm2_g, w_in, w_out, s5_lam_re, s5_lam_im, s5_log_dt, s5_b_re, s5_b_im, s5_c_re, s5_c_im, s5_d, s5_glu_w, s5_glu_b, ga_qn_g, ga_kn_g, ssd_conv_w, ssd_conv_b, ssd_dt_bias, ssd_a_log, ssd_d, ssd_norm_g, wa_sink, moe_coarse_w, moe_coarse_b, moe_fine_w, moe_fine_b, moe_w_gate, moe_w_up, moe_w_down, final_g):
    nb, l, d = x.shape
    lc = ctx.shape[1]
    depth = ada_w.shape[0]
    assert lc == TM and l % TM == 0 and nb <= SUBLANES - 1 and d == D_MODEL
    s_len = lc + l
    nblk = s_len // TM
    t = nb * s_len

    cc = jnp.zeros((SUBLANES, d), F32).at[:nb].set(c).at[nb].set(c_ctx)
    mods = _ada(cc, ada_w, ada_b).reshape(depth, SUBLANES, 6, d)
    cos_t, sin_t = _rope_tables(lc, l)
    w_packed = jax.vmap(_pack_w_in)(w_in)
    s5_tabs = jax.vmap(_s5_params)(s5_lam_re, s5_lam_im, s5_log_dt, s5_b_re, s5_b_im, s5_c_re, s5_c_im, s5_d)
    wrs, brs = jax.vmap(_pack_router)(moe_coarse_w, moe_coarse_b, moe_fine_w, moe_fine_b)

    src = ("first", x.reshape(nb * l, d), ctx.reshape(nb * lc, d))
    for i in range(depth):
        mod = mods[i]
        (xm, xbc, ug, z, dt, gaq, gak, gav, waq, wak, wav) = _inproj(
            src, mod, norm1_g[i], w_packed[i], cos_t, sin_t, ga_qn_g[i], ga_kn_g[i], nb, nblk)
        ys5 = _s5(ug, tuple(tab[i] for tab in s5_tabs), nb, s_len, lc)
        oga = _ga(gaq, gak, gav, nb, s_len, lc)
        y0, y1 = _ssd(xbc, dt, ssd_conv_w[i], ssd_conv_b[i], ssd_dt_bias[i], ssd_a_log[i], ssd_d[i], nb, s_len, lc)
        owa = _wa(wa_sink[i], waq, wak, wav, nb, s_len, lc)
        wr, br = wrs[i], brs[i]
        xn, h2, route = _outproj(xm, ys5, oga, y0, y1, z, owa, mod, s5_glu_w[i], s5_glu_b[i], ssd_norm_g[i],
                                 w_out[i], norm2_g[i], wr, br, nb, nblk)
        rows2 = _moe(h2, route, moe_w_gate, moe_w_up, moe_w_down, i)
        src = ("moe", xn, rows2, route, mod)
    return _final(xn, rows2, route, mod, final_g, nb, nblk).reshape(nb, l, d)
```

```python
import functools
import math

import jax
import jax.numpy as jnp
import numpy as np
from jax import lax
from jax.experimental import pallas as pl
from jax.experimental.pallas import tpu as pltpu
from jax.experimental.pallas import tpu_sc as plsc

F32 = jnp.float32
BF16 = jnp.bfloat16
HI = lax.Precision.HIGHEST

D_MODEL = 1024
GRID_W = 64
GROUP_W = 256
HEAD_DIM = 64
ROPE_FREQS = HEAD_DIM // 4
ROPE_BASE = 10000.0
EPS = 1e-6
S5_CH = 16
S5_GROUPS = GROUP_W // S5_CH
S5_STATE = 64
N_HEADS = 4
SSD_HEADS = 4
SSD_NGROUPS = 2
SSD_STATE = 128
SSD_XBC = GROUP_W + 2 * SSD_NGROUPS * SSD_STATE
WINDOW = 128
MOE_GROUPS = 4
MOE_PER_GROUP = 8
N_EXPERTS = 32
D_EXPERT = D_MODEL // 2

LANES = 128
SUBLANES = 8
TM = 256
TQ = 128
GA_TQ = 128
GA_SUB = 2
S5_Q = 32
S5_BLK = S5_Q * S5_CH
MOE_TM = 512
SC_CORES = 2
SC_SUBCORES = 16
SC_GATHER_K = 32
GATHER_ROWS = SC_CORES * SC_SUBCORES * SC_GATHER_K
ROUTE_FINE0 = 32
VMEM_LIMIT = 56 * 1024 * 1024

NEG_INF = float("-inf")
LOG2E = math.log2(math.e)


def _cp(sem, vmem=VMEM_LIMIT):
    return pltpu.CompilerParams(dimension_semantics=sem, vmem_limit_bytes=vmem)


def _dot(a, b):
    return jnp.dot(a, b, preferred_element_type=F32)


def _dot_hi(a, b):
    return jnp.dot(a, b, preferred_element_type=F32, precision=HI)


def _dot_nt(a, b):
    return lax.dot_general(a, b, (((1,), (1,)), ((), ())), preferred_element_type=F32)


def _sigmoid(x):
    return 1.0 / (1.0 + jnp.exp(-x))


def _silu(x):
    return x * _sigmoid(x)


def _gelu_tanh(x):
    return 0.5 * x * (1.0 + jnp.tanh(math.sqrt(2.0 / math.pi) * (x + 0.044715 * (x * x * x))))


def _softplus(x):
    return jnp.maximum(x, 0.0) + jnp.log(1.0 + jnp.exp(-jnp.abs(x)))


_HI16 = 0xFFFF0000


def _pack_bf16_pair(x):
    n = x.shape[1] // 2
    bits = pltpu.bitcast(x.astype(BF16).astype(F32), jnp.uint32)
    return (bits[:, n:] & jnp.uint32(_HI16)) | (bits[:, :n] >> 16)


def _unpack_bf16_pair(w):
    return pltpu.bitcast(w << 16, F32), pltpu.bitcast(w & jnp.uint32(_HI16), F32)


def _per_head_cols(v, base, n_heads, shape):
    lane = lax.broadcasted_iota(jnp.int32, shape, 1)
    out = jnp.broadcast_to(v[:, base + n_heads - 1:base + n_heads], shape)
    for h in range(n_heads - 2, -1, -1):
        out = jnp.where(lane < (h + 1) * HEAD_DIM, v[:, base + h:base + h + 1], out)
    return out


def _ada_kernel(c_ref, w_ref, b_ref, o_ref):
    c = c_ref[...]
    o_ref[0] = _dot_hi(_silu(c), w_ref[0]) + b_ref[0]


def _ada(cc, ada_w, ada_b):
    depth, d, n = ada_w.shape
    tn = 1536
    return pl.pallas_call(
        _ada_kernel,
        out_shape=jax.ShapeDtypeStruct((depth, SUBLANES, n), F32),
        grid=(depth, n // tn),
        in_specs=[pl.BlockSpec((SUBLANES, d), lambda l, j: (0, 0)),
                  pl.BlockSpec((1, d, tn), lambda l, j: (l, 0, j)),
                  pl.BlockSpec((1, 1, tn), lambda l, j: (l, 0, j))],
        out_specs=pl.BlockSpec((1, SUBLANES, tn), lambda l, j: (l, 0, j)),
        compiler_params=_cp(("parallel", "parallel")),
        name="ada_mod",
    )(cc, ada_w, ada_b.reshape(depth, 1, n))


_C_XBC = 0
_C_U = _C_XBC + SSD_XBC
_C_Z = _C_U + GROUP_W
_C_DT = _C_Z + GROUP_W
_C_GAQ = _C_DT + LANES
_C_WAQ = _C_GAQ + N_HEADS * LANES
_C_GAK = _C_WAQ + N_HEADS * LANES
_C_GAV = _C_GAK + LANES
_C_WAK = _C_GAV + LANES
_C_WAV = _C_WAK + LANES
_C_END = _C_WAV + LANES


def _expand_q_cols(wq):
    zero = jnp.zeros((wq.shape[0], HEAD_DIM), wq.dtype)
    parts = []
    for h in range(N_HEADS):
        head = wq[:, h * HEAD_DIM:(h + 1) * HEAD_DIM]
        parts += [head, zero] if h // 2 == 0 else [zero, head]
    return jnp.concatenate(parts, axis=1)


def _pack_w_in(w_in):
    cuts = np.cumsum([256, 256, 128, 128, 256, SSD_XBC, 2 * SSD_HEADS, 256, 128, 128])[:-1]
    u, gaq, gak, gav, z, xbc, dt, waq, wak, wav = jnp.split(w_in, [int(c) for c in cuts], axis=1)
    dt = jnp.pad(dt, ((0, 0), (0, LANES - dt.shape[1])))
    w = jnp.concatenate([xbc, u, z, dt, _expand_q_cols(gaq), _expand_q_cols(waq), gak, gav, wak, wav], axis=1)
    return w.astype(BF16)


def _rope(x, cos, sins):
    w = x.shape[1]
    if w > LANES:
        cos = jnp.concatenate([cos] * (w // LANES), axis=1)
        sins = jnp.concatenate([sins] * (w // LANES), axis=1)
    lane = lax.broadcasted_iota(jnp.int32, x.shape, 1)
    up = pltpu.roll(x, w - ROPE_FREQS, 1)
    dn = pltpu.roll(x, ROPE_FREQS, 1)
    partner = jnp.where((lane & ROPE_FREQS) == 0, up, dn)
    return x * cos + partner * sins


def _moe_residual(xn_ref, r0_ref, r1_ref, route_ref, mod_ref):
    route = route_ref[...]
    r0 = jnp.concatenate(_unpack_bf16_pair(r0_ref[...]), axis=1)
    r1 = jnp.concatenate(_unpack_bf16_pair(r1_ref[...]), axis=1)
    return xn_ref[...] + mod_ref[0, 5:6, :] * (route[:, 2:3] * r0 + route[:, 3:4] * r1)


def _inproj_kernel(*refs, first, nblk):
    if first:
        lat_ref, ctx_ref = refs[:2]
        x = jnp.where(pl.program_id(0) % nblk == 0, ctx_ref[...], lat_ref[...])
        refs = refs[2:]
    else:
        x = _moe_residual(*refs[:5])
        refs = refs[5:]
    (mod_ref, g_ref, w_ref, cos_ref, sin_ref, qn_ref, kn_ref,
     xm_o, xbc_o, ug_o, z_o, dt_o, gaq_o, gak_o, gav_o, waq_o, wak_o, wav_o, u_scr) = refs
    xm_o[...] = x
    ms = jnp.mean(x * x, axis=-1, keepdims=True)
    xn = x * lax.rsqrt(ms + EPS) * g_ref[...]
    h = xn * (1.0 + mod_ref[0, 1:2, :]) + mod_ref[0, 0:1, :]
    hb = h.astype(BF16)

    def proj(lo, hi):
        return _dot(hb, w_ref[:, lo:hi])

    cos = cos_ref[...]
    sins = sin_ref[...]
    scale = LOG2E * HEAD_DIM ** -0.5
    q = proj(_C_GAQ, _C_WAQ)
    qs = q * q
    inv = jnp.concatenate(
        [jnp.broadcast_to(lax.rsqrt(jnp.sum(qs[:, s * LANES:(s + 1) * LANES], axis=1, keepdims=True)
                                    * (1.0 / HEAD_DIM) + EPS), (q.shape[0], LANES)) for s in range(N_HEADS)], axis=1)
    gaq_o[...] = (_rope(q * inv * qn_ref[...], cos, sins) * scale).astype(BF16)
    waq_o[...] = (_rope(proj(_C_WAQ, _C_GAK), cos, sins) * scale).astype(BF16)
    k = proj(_C_GAK, _C_GAV)
    ks = k * k
    lane = lax.broadcasted_iota(jnp.int32, k.shape, 1)
    lo = lane < HEAD_DIM
    ms0 = jnp.sum(jnp.where(lo, ks, 0.0), axis=1, keepdims=True)
    ms1 = jnp.sum(jnp.where(lo, 0.0, ks), axis=1, keepdims=True)
    kinv = lax.rsqrt(jnp.where(lo, ms0, ms1) * (1.0 / HEAD_DIM) + EPS)
    gak_o[...] = _rope(k * kinv * kn_ref[...], cos, sins).astype(BF16)
    gav_o[...] = proj(_C_GAV, _C_WAK).astype(BF16)
    wak_o[...] = _rope(proj(_C_WAK, _C_WAV), cos, sins).astype(BF16)
    wav_o[...] = proj(_C_WAV, _C_END).astype(BF16)
    xbc_o[...] = proj(_C_XBC, _C_U)
    u = proj(_C_U, _C_Z)
    u_scr[0] = u[:, :LANES]
    u_scr[1] = u[:, LANES:]
    _s5_pack_kernel(u_scr.at[0], u_scr.at[1], ug_o)
    z_o[...] = proj(_C_Z, _C_DT)
    dt_o[...] = proj(_C_DT, _C_GAQ)


def _mod_row(i, nblk, nb):
    return jnp.where(i % nblk == 0, nb, i // nblk)


def _inproj(src, mod, norm_g, w_packed, cos_t, sin_t, qn_g, kn_g, nb, nblk):
    first = src[0] == "first"
    d = src[1].shape[1]
    t = nb * nblk * TM
    row = lambda i: (i, 0)
    fix = lambda i: (0, 0)
    modspec = pl.BlockSpec((1, 6, d), lambda i: (_mod_row(i, nblk, nb), 0, 0))
    if first:
        src_specs = [pl.BlockSpec((TM, d), lambda i: ((i // nblk) * (nblk - 1) + jnp.maximum(i % nblk - 1, 0), 0)),
                     pl.BlockSpec((TM, d), lambda i: (i // nblk, 0))]
        src_args = src[1:]
    else:
        src_specs = [pl.BlockSpec((TM, d), row), pl.BlockSpec((TM, d // 2), row),
                     pl.BlockSpec((TM, d // 2), lambda i: (i + t // TM, 0)), pl.BlockSpec((TM, LANES), row), modspec]
        src_args = (src[1], src[2], src[2], src[3], src[4])
    outs = [(d, F32), (SSD_XBC, F32), None, (GROUP_W, F32), (LANES, F32),
            (N_HEADS * LANES, BF16), (LANES, BF16), (LANES, BF16),
            (N_HEADS * LANES, BF16), (LANES, BF16), (LANES, BF16)]
    shapes = [jax.ShapeDtypeStruct((t, o[0]), o[1]) if o else
              jax.ShapeDtypeStruct((S5_GROUPS, t // S5_Q, S5_BLK), F32) for o in outs]
    specs = [pl.BlockSpec((TM, o[0]), row) if o else
             pl.BlockSpec((S5_GROUPS, S5_TB, S5_BLK), lambda i: (0, i, 0)) for o in outs]
    return pl.pallas_call(
        functools.partial(_inproj_kernel, first=first, nblk=nblk),
        out_shape=shapes,
        grid=(t // TM,),
        in_specs=src_specs + [
                  modspec,
                  pl.BlockSpec((1, d), fix),
                  pl.BlockSpec((d, _C_END), fix),
                  pl.BlockSpec((TM, LANES), lambda i: (i % nblk, 0)),
                  pl.BlockSpec((TM, LANES), lambda i: (i % nblk, 0)),
                  pl.BlockSpec((1, N_HEADS * LANES), fix),
                  pl.BlockSpec((1, LANES), fix)],
        out_specs=specs,
        scratch_shapes=[pltpu.VMEM((GROUP_W // LANES, TM, LANES), F32)],
        compiler_params=_cp(("parallel",)),
        name="in_proj",
    )(*src_args, mod, norm_g.reshape(1, d), w_packed, cos_t, sin_t,
      jnp.tile(qn_g, 2 * N_HEADS).reshape(1, -1), jnp.tile(kn_g, 2).reshape(1, -1))


def _rope_tables(lc, l):
    n_rows = l // GRID_W
    rows = np.repeat(np.arange(n_rows), GRID_W)
    cols = np.tile(np.arange(GRID_W), n_rows)
    inv = np.power(np.float32(ROPE_BASE), -np.arange(ROPE_FREQS, dtype=np.float32) / ROPE_FREQS)
    ang = np.stack([rows, cols], axis=-1).astype(np.float32)[..., None] * inv
    cos = np.cos(ang)
    sin = np.sin(ang)
    cos64 = np.stack([cos, cos], axis=2).reshape(l, HEAD_DIM)
    sin64 = np.stack([-sin, sin], axis=2).reshape(l, HEAD_DIM)
    cos64 = np.concatenate([np.ones((lc, HEAD_DIM), np.float32), cos64], axis=0)
    sin64 = np.concatenate([np.zeros((lc, HEAD_DIM), np.float32), sin64], axis=0)
    return (jnp.asarray(np.tile(cos64, (1, 2)), dtype=F32), jnp.asarray(np.tile(sin64, (1, 2)), dtype=F32))


def _merge_heads(o2, kvh):
    tq = o2.shape[0] // 2
    oa, ob = o2[:tq], o2[tq:]
    lane = lax.broadcasted_iota(jnp.int32, oa.shape, 1)
    if kvh == 0:
        return jnp.where(lane < HEAD_DIM, oa, pltpu.roll(ob, HEAD_DIM, 1))
    return jnp.where(lane < HEAD_DIM, pltpu.roll(oa, HEAD_DIM, 1), ob)


def _stack_q(q_ref, rows, kvh):
    return jnp.concatenate([q_ref[rows, (2 * kvh) * LANES:(2 * kvh + 1) * LANES],
                            q_ref[rows, (2 * kvh + 1) * LANES:(2 * kvh + 2) * LANES]], axis=0)


def _ga_attend(q_ref, k_ref, v_ref, o_ref, nkeys):
    k = k_ref[0:nkeys, :]
    v = v_ref[0:nkeys, :]
    lane = lax.broadcasted_iota(jnp.int32, v.shape, 1)
    vaug = [jnp.where((lane < HEAD_DIM) if kvh == 0 else (lane >= HEAD_DIM), v, jnp.ones_like(v))
            for kvh in range(2)]
    for sub in range(GA_SUB):
        rows = slice(sub * GA_TQ, (sub + 1) * GA_TQ)
        scores = [_dot_nt(_stack_q(q_ref, rows, kvh), k) for kvh in range(2)]
        outs = []
        for kvh in range(2):
            s = scores[kvh]
            p = jnp.exp2((s - jnp.max(s, axis=1, keepdims=True)).astype(BF16))
            o2 = _dot(p, vaug[kvh])
            outs.append(_merge_heads(o2 / pltpu.roll(o2, HEAD_DIM, 1), kvh))
        o_ref[rows, :] = jnp.concatenate(outs, axis=1)


def _attn_kernel(sink_ref, gq_ref, gk_ref, gv_ref, wq_ref, wk_ref, wv_ref, og_ref, ow_ref, *, lc):
    is_ctx = pl.program_id(1) < lc // TM

    @pl.when(is_ctx)
    def _():
        _ga_attend(gq_ref, gk_ref, gv_ref, og_ref, lc)
        _wa_attend(sink_ref, wq_ref, wk_ref, wv_ref, ow_ref, lc)

    @pl.when(jnp.logical_not(is_ctx))
    def _():
        _ga_attend(gq_ref, gk_ref, gv_ref, og_ref, gk_ref.shape[0])
        _wa_attend(sink_ref, wq_ref, wk_ref, wv_ref, ow_ref, lc)


def _attn(sink, gq, gk, gv, wq, wk, wv, nb, s_len, lc):
    t = gq.shape[0]
    nq = s_len // TM
    assert GA_SUB * GA_TQ == TM and WA_SUB * TQ == TM
    qspec = pl.BlockSpec((TM, N_HEADS * LANES), lambda b, j: (b * nq + j, 0))
    kvspec = pl.BlockSpec((s_len, LANES), lambda b, j: (b, 0))
    ospec = pl.BlockSpec((TM, GROUP_W), lambda b, j: (b * nq + j, 0))
    return pl.pallas_call(
        functools.partial(_attn_kernel, lc=lc),
        out_shape=[jax.ShapeDtypeStruct((t, GROUP_W), F32)] * 2,
        grid=(nb, nq),
        in_specs=[pl.BlockSpec(memory_space=pltpu.SMEM), qspec, kvspec, kvspec, qspec, kvspec, kvspec],
        out_specs=[ospec, ospec],
        compiler_params=_cp(("parallel", "arbitrary")),
        name="attention",
    )(sink, gq, gk, gv, wq, wk, wv)


WA_SUB = TM // TQ


def _wa_attend(sink_ref, q_ref, k_ref, v_ref, o_ref, lc):
    s_len = k_ref.shape[0]
    kc = k_ref[0:lc, :]
    vc = v_ref[0:lc, :]
    lane_c = lax.broadcasted_iota(jnp.int32, vc.shape, 1)
    lane_b = lax.broadcasted_iota(jnp.int32, (3 * TQ, LANES), 1)
    row = lax.broadcasted_iota(jnp.int32, (2 * TQ, 1), 0)
    for sub in range(WA_SUB):
        rows = slice(sub * TQ, (sub + 1) * TQ)
        n = pl.program_id(1) * WA_SUB + sub - lc // TQ
        start = pl.multiple_of(jnp.clip(lc + (n - 1) * TQ, lc, s_len - 3 * TQ), TQ)
        kb = k_ref[pl.ds(start, 3 * TQ), :]
        vb = v_ref[pl.ds(start, 3 * TQ), :]
        qpos = n * TQ + lax.broadcasted_iota(jnp.int32, (TQ, 3 * TQ), 0)
        kpos = (start - lc) + lax.broadcasted_iota(jnp.int32, (TQ, 3 * TQ), 1)
        reach = jnp.where(n >= 0, WINDOW, -1)
        valid = jnp.abs(qpos - kpos) <= reach
        valid = jnp.concatenate([valid, valid], axis=0)
        outs = []
        for kvh in range(2):
            q2 = jnp.concatenate([q_ref[rows, (2 * kvh) * LANES:(2 * kvh + 1) * LANES],
                                  q_ref[rows, (2 * kvh + 1) * LANES:(2 * kvh + 2) * LANES]], axis=0)
            sc = _dot_nt(q2, kc)
            sb = jnp.where(valid, _dot_nt(q2, kb), NEG_INF)
            sink = jnp.where(row < TQ, sink_ref[2 * kvh], sink_ref[2 * kvh + 1]) * LOG2E
            m = jnp.maximum(jnp.maximum(jnp.max(sc, axis=1, keepdims=True), jnp.max(sb, axis=1, keepdims=True)), sink)
            pc = jnp.exp2((sc - m).astype(BF16))
            pb = jnp.exp2((sb - m).astype(BF16))
            own_c = (lane_c < HEAD_DIM) if kvh == 0 else (lane_c >= HEAD_DIM)
            own_b = (lane_b < HEAD_DIM) if kvh == 0 else (lane_b >= HEAD_DIM)
            o2 = _dot(pc, jnp.where(own_c, vc, jnp.ones_like(vc))) + _dot(pb, jnp.where(own_b, vb, jnp.ones_like(vb)))
            denom = pltpu.roll(o2, HEAD_DIM, 1) + jnp.exp2(sink - m)
            outs.append(_merge_heads(o2 / denom, kvh))
        o_ref[rows, :] = jnp.concatenate(outs, axis=1)


def _s5_chunk_index(t, rev, nc_ctx, nc_tot):
    if not rev:
        return t
    return jnp.where(t < nc_ctx, nc_ctx - 1 - t, nc_tot - 1 - (t - nc_ctx))


def _s5_kernel(u_ref, k_ref, p_ref, g_ref, ar_ref, ai_ref, dsk_ref, y_ref, s_scr, h_scr, m_scr, *, nb, nc_ctx, nc_tot):
    for d in range(2):
        ext = k_ref[d, 0]
        for s in range(S5_Q):
            lo = ((S5_Q - s) if d == 0 else (S5_Q - 1 - s)) * S5_CH
            win = pltpu.roll(ext, (2 * S5_BLK - lo) % (2 * S5_BLK), 1)[:, :S5_BLK]
            m_scr[d, s * S5_CH:(s + 1) * S5_CH, :] = win.astype(BF16)
    uf = u_ref[0]
    u = uf.astype(BF16)
    for d in range(2):
        for k in range(2):
            s_scr[d, k] = _dot(u, p_ref[d, k, 0])
    ar = [jnp.broadcast_to(ar_ref[d, 0], (nb, LANES)) for d in range(2)]
    ai = [[jnp.broadcast_to(ai_ref[d, k, 0], (nb, LANES)) for k in range(2)] for d in range(2)]

    def body(t, carry):
        out = []
        for d in range(2):
            h, hs = carry[d]
            rows = pl.ds(_s5_chunk_index(t, d == 1, nc_ctx, nc_tot), nb, stride=nc_tot)
            h_scr[d, rows, :] = h
            out.append((ar[d] * h + ai[d][0] * hs + s_scr[d, 0, rows, :],
                        ar[d] * hs + ai[d][1] * h + s_scr[d, 1, rows, :]))
        return tuple(out)

    zero = jnp.zeros((nb, LANES), F32)
    lax.fori_loop(0, nc_tot, body, ((zero, zero), (zero, zero)), unroll=2)
    y = uf * dsk_ref[0]
    for d in range(2):
        y = y + _dot(u, m_scr[d]) + _dot(h_scr[d].astype(BF16), g_ref[d, 0])
    y_ref[0] = y


S5_TB = TM // S5_Q
S5_GPS = LANES // S5_CH


def _s5_pack_kernel(lo_ref, hi_ref, o_ref):
    for s in range(S5_Q):
        rows = pl.ds(s, S5_TB, stride=S5_Q)
        halves = (lo_ref[rows, :], hi_ref[rows, :])
        dst = S5_CH * (s % S5_GPS)
        for g in range(S5_GROUPS):
            slab = halves[g // S5_GPS]
            src = S5_CH * (g % S5_GPS)
            moved = slab if src == dst else pltpu.roll(slab, (dst - src) % LANES, 1)
            o_ref[g, :, s * S5_CH:(s + 1) * S5_CH] = moved[:, dst:dst + S5_CH]


def _s5_unpack_kernel(y_ref, o_ref):
    lane_grp = lax.broadcasted_iota(jnp.int32, (S5_TB, LANES), 1) // S5_CH
    for s in range(S5_Q):
        src = S5_CH * (s % S5_GPS)
        for half in range(S5_GROUPS // S5_GPS):
            acc = None
            for gl in range(S5_GPS):
                slab = y_ref[half * S5_GPS + gl, :, (s // S5_GPS) * LANES:(s // S5_GPS + 1) * LANES]
                dst = S5_CH * gl
                moved = slab if src == dst else pltpu.roll(slab, (dst - src) % LANES, 1)
                acc = moved if acc is None else jnp.where(lane_grp == gl, moved, acc)
            o_ref[half, pl.ds(s, S5_TB, stride=S5_Q), :] = acc


def _s5_params(lam_re, lam_im, log_dt, b_re, b_im, c_re, c_im, d_skip):
    q = S5_Q
    dt = jnp.exp(log_dt)[..., None]
    lr, li = lam_re, lam_im
    mag = jnp.exp(lr * dt)
    a_re = mag * jnp.cos(li * dt)
    a_im = mag * jnp.sin(li * dt)
    den = lr * lr + li * li
    f_re = ((a_re - 1.0) * lr + a_im * li) / den
    f_im = (a_im * lr - (a_re - 1.0) * li) / den
    bb_re = f_re[..., None] * b_re - f_im[..., None] * b_im
    bb_im = f_re[..., None] * b_im + f_im[..., None] * b_re
    kk = jnp.arange(q + 1, dtype=F32)[:, None, None, None]
    pmag = jnp.exp(kk * (lr * dt))
    pw_re = pmag * jnp.cos(kk * (li * dt))
    pw_im = pmag * jnp.sin(kk * (li * dt))
    lw_re = pw_re[:q].transpose(1, 2, 0, 3)[:, :, :, None, :]
    lw_im = pw_im[:q].transpose(1, 2, 0, 3)[:, :, :, None, :]
    ck_re = c_re[:, :, None] * lw_re - c_im[:, :, None] * lw_im
    ck_im = c_re[:, :, None] * lw_im + c_im[:, :, None] * lw_re
    ck = jnp.concatenate([ck_re, -ck_im], axis=-1).reshape(2, S5_GROUPS, S5_BLK, 2 * S5_STATE)
    kern_t = jnp.einsum("dgmp,dgpc->dgcm", ck, jnp.concatenate([bb_re, bb_im], axis=2), precision=HI)
    kern_t = kern_t.reshape(2, S5_GROUPS, S5_CH, q, S5_CH)
    zeros = jnp.zeros_like(kern_t)
    bbt_re = bb_re.transpose(0, 1, 3, 2)[:, :, None]
    bbt_im = bb_im.transpose(0, 1, 3, 2)[:, :, None]
    ct_re = c_re.transpose(0, 1, 3, 2)[:, :, :, None, :]
    ct_im = c_im.transpose(0, 1, 3, 2)[:, :, :, None, :]
    ms, ps, gs = [], [], []
    for d in range(2):
        ext = (jnp.concatenate([zeros[d], kern_t[d]], axis=2) if d == 0
               else jnp.concatenate([kern_t[d, :, :, ::-1], zeros[d]], axis=2))
        ext = ext.reshape(S5_GROUPS, S5_CH, 2 * S5_BLK)
        ms.append(ext)
        pidx = (q - 1 - jnp.arange(q)) if d == 0 else jnp.arange(q)
        pr = pw_re[pidx, d].transpose(1, 0, 2)[:, :, None, :]
        pi = pw_im[pidx, d].transpose(1, 0, 2)[:, :, None, :]
        p_re = pr * bbt_re[d] - pi * bbt_im[d]
        p_im = pr * bbt_im[d] + pi * bbt_re[d]
        pd = jnp.stack([jnp.concatenate([p_re, p_im], axis=3), jnp.concatenate([p_im, p_re], axis=3)])
        ps.append(pd.reshape(2, S5_GROUPS, S5_BLK, 2 * S5_STATE))
        gidx = (jnp.arange(q) + 1) if d == 0 else (q - jnp.arange(q))
        gw_re = pw_re[gidx, d].transpose(1, 2, 0)[..., None]
        gw_im = pw_im[gidx, d].transpose(1, 2, 0)[..., None]
        g_re = ct_re[d] * gw_re - ct_im[d] * gw_im
        g_im = ct_re[d] * gw_im + ct_im[d] * gw_re
        gs.append(jnp.concatenate([g_re, -g_im], axis=1).reshape(S5_GROUPS, 2 * S5_STATE, S5_BLK))
    ar = jnp.concatenate([pw_re[q], pw_re[q]], axis=-1)[:, :, None, :]
    ai = jnp.stack([jnp.concatenate([-pw_im[q], pw_im[q]], axis=-1),
                    jnp.concatenate([pw_im[q], -pw_im[q]], axis=-1)], axis=1)[:, :, :, None, :]
    dsk = jnp.tile(d_skip.reshape(S5_GROUPS, 1, S5_CH), (1, 1, q))
    return (jnp.stack(ms), jnp.stack(ps).astype(BF16), jnp.stack(gs).astype(BF16),
            ar.astype(F32), ai.astype(F32), dsk.astype(F32))


def _s5(ug, params, nb, s_len, lc):
    m, p, g, ar, ai, dsk = params
    nc_tot = s_len // S5_Q
    nc_ctx = lc // S5_Q
    r = nb * nc_tot
    return pl.pallas_call(
        functools.partial(_s5_kernel, nb=nb, nc_ctx=nc_ctx, nc_tot=nc_tot),
        out_shape=jax.ShapeDtypeStruct((S5_GROUPS, r, S5_BLK), F32),
        grid=(S5_GROUPS,),
        in_specs=[pl.BlockSpec((1, r, S5_BLK), lambda gi: (gi, 0, 0)),
                  pl.BlockSpec((2, 1, S5_CH, 2 * S5_BLK), lambda gi: (0, gi, 0, 0)),
                  pl.BlockSpec((2, 2, 1, S5_BLK, 2 * S5_STATE), lambda gi: (0, 0, gi, 0, 0)),
                  pl.BlockSpec((2, 1, 2 * S5_STATE, S5_BLK), lambda gi: (0, gi, 0, 0)),
                  pl.BlockSpec((2, 1, 1, 2 * S5_STATE), lambda gi: (0, gi, 0, 0)),
                  pl.BlockSpec((2, 2, 1, 1, 2 * S5_STATE), lambda gi: (0, 0, gi, 0, 0)),
                  pl.BlockSpec((1, 1, S5_BLK), lambda gi: (gi, 0, 0))],
        out_specs=pl.BlockSpec((1, r, S5_BLK), lambda gi: (gi, 0, 0)),
        scratch_shapes=[pltpu.VMEM((2, 2, r, 2 * S5_STATE), F32), pltpu.VMEM((2, r, 2 * S5_STATE), F32),
                        pltpu.VMEM((2, S5_BLK, S5_BLK), BF16)],
        compiler_params=_cp(("parallel",)),
        name="s5_scan",
    )(ug, m, p, g, ar, ai, dsk)


CONV_ROWS = 2 * TM


def _conv_kernel(x_ref, prev_ref, next_ref, w_ref, b_ref, o_ref, *, s_len, lc):
    x = x_ref[...]
    rows = x.shape[0]
    ridx = lax.broadcasted_iota(jnp.int32, x.shape, 0)
    pos = (pl.program_id(0) * rows) % s_len + ridx
    pos = jnp.where(pos >= s_len, pos - s_len, pos)
    seg_first = jnp.logical_or(pos == 0, pos == lc)
    seg_last = jnp.logical_or(pos == lc - 1, pos == s_len - 1)
    xm = jnp.where(ridx == 0, prev_ref[SUBLANES - 1:SUBLANES, :], pltpu.roll(x, 1, 0))
    xp = jnp.where(ridx == rows - 1, next_ref[0:1, :], pltpu.roll(x, rows - 1, 0))
    xm = jnp.where(seg_first, 0.0, xm)
    xp = jnp.where(seg_last, 0.0, xp)
    y = xm * w_ref[0:1, :] + x * w_ref[1:2, :] + xp * w_ref[2:3, :] + b_ref[...]
    o_ref[...] = _silu(y)


def _conv(xbc, w, b, s_len, lc):
    t, c = xbc.shape
    per = CONV_ROWS // SUBLANES
    last = t // SUBLANES - 1
    return pl.pallas_call(
        functools.partial(_conv_kernel, s_len=s_len, lc=lc),
        out_shape=jax.ShapeDtypeStruct((t, c), F32),
        grid=(t // CONV_ROWS,),
        in_specs=[pl.BlockSpec((CONV_ROWS, c), lambda i: (i, 0)),
                  pl.BlockSpec((SUBLANES, c), lambda i: (jnp.maximum(i * per - 1, 0), 0)),
                  pl.BlockSpec((SUBLANES, c), lambda i: (jnp.minimum((i + 1) * per, last), 0)),
                  pl.BlockSpec((3, c), lambda i: (0, 0)),
                  pl.BlockSpec((1, c), lambda i: (0, 0))],
        out_specs=pl.BlockSpec((CONV_ROWS, c), lambda i: (i, 0)),
        compiler_params=_cp(("parallel",)),
        name="ssd_conv",
    )(xbc, xbc, xbc, w, b.reshape(1, c))


_X_B = GROUP_W
_X_C = GROUP_W + SSD_NGROUPS * SSD_STATE


def _ssd_kernel(xf_ref, dtf_ref, dttf_ref, xr_ref, dtr_ref, dttr_ref, bias_ref, a_ref, biast_ref, at_ref, dsk_ref,
                yf_ref, yr_ref, stf_ref, str_ref):
    @pl.when(pl.program_id(1) == 0)
    def _():
        stf_ref[...] = jnp.zeros_like(stf_ref)
        str_ref[...] = jnp.zeros_like(str_ref)

    par = (bias_ref[...], a_ref[...], biast_ref[...], at_ref[...], dsk_ref[...])
    for j in range(SSD_SUB):
        rf = slice(j * TQ, (j + 1) * TQ)
        yf_ref[rf, :] = _ssd_chunk_step(xf_ref[rf, :], dtf_ref[rf, :], dttf_ref[0, :, rf], par, stf_ref, False)
        rr = slice((SSD_SUB - 1 - j) * TQ, (SSD_SUB - j) * TQ)
        yr_ref[rr, :] = _ssd_chunk_step(xr_ref[rr, :], dtr_ref[rr, :], dttr_ref[0, :, rr], par, str_ref, True)


def _ssd_chunk_step(xc, dt_raw, dtt_raw, par, st_ref, rev):
    bias, a_vec, biast, at_vec, dsk = par
    base = SSD_HEADS if rev else 0
    x = xc[:, 0:GROUP_W]
    dt = _softplus(dt_raw + bias)
    a = dt * a_vec
    dtt = _softplus(dtt_raw + biast)
    at = dtt * at_vec
    ri = lax.broadcasted_iota(jnp.int32, (TQ, TQ), 0)
    ci = lax.broadcasted_iota(jnp.int32, (TQ, TQ), 1)
    causal = (ci >= ri) if rev else (ri >= ci)
    tri = jnp.where(causal, 1.0, 0.0)
    cum_c = _dot_hi(tri, a)
    cum_r = _dot_nt_hi(at, tri)
    edge = 0 if rev else TQ - 1
    tot = cum_c[edge:edge + 1, :]

    shape = (TQ, GROUP_W)
    xdt = x * _per_head_cols(dt, base, SSD_HEADS, shape)
    lane = lax.broadcasted_iota(jnp.int32, shape, 1)
    y = jnp.zeros(shape, F32)
    bmat = [xc[:, _X_B + g * SSD_STATE:_X_B + (g + 1) * SSD_STATE].astype(BF16) for g in range(SSD_NGROUPS)]
    cmat = [xc[:, _X_C + g * SSD_STATE:_X_C + (g + 1) * SSD_STATE].astype(BF16) for g in range(SSD_NGROUPS)]
    cb = [_dot_nt(cmat[g], bmat[g]) for g in range(SSD_NGROUPS)]
    for h in range(SSD_HEADS):
        col = base + h
        seg = jnp.where(causal, cum_c[:, col:col + 1] - cum_r[col:col + 1, :], NEG_INF)
        scores = cb[h // 2] * jnp.exp(seg)
        xh = jnp.where((lane >= h * HEAD_DIM) & (lane < (h + 1) * HEAD_DIM), xdt, 0.0)
        y = y + _dot(scores.astype(BF16), xh.astype(BF16))
    st = st_ref[...]
    yo = jnp.concatenate(
        [_dot_nt(cmat[g], st[g * SSD_STATE:(g + 1) * SSD_STATE].astype(BF16)) for g in range(SSD_NGROUPS)], axis=1)
    y = y + yo * _per_head_cols(jnp.exp(cum_c), base, SSD_HEADS, shape)
    if not rev:
        y = y + x * dsk
    xd = xdt * _per_head_cols(jnp.exp(tot - cum_c), base, SSD_HEADS, shape)
    xdt_t = xd.T.astype(BF16)
    decay = jnp.exp(tot)
    for g in range(SSD_NGROUPS):
        new = _dot(xdt_t[g * SSD_STATE:(g + 1) * SSD_STATE], bmat[g])
        for hh in range(2):
            h = 2 * g + hh
            r0 = h * HEAD_DIM
            st_ref[r0:r0 + HEAD_DIM, :] = (decay[:, base + h:base + h + 1] * st[r0:r0 + HEAD_DIM]
                                           + new[hh * HEAD_DIM:(hh + 1) * HEAD_DIM])
    return y


def _dot_nt_hi(a, b):
    return lax.dot_general(a, b, (((1,), (1,)), ((), ())), preferred_element_type=F32, precision=HI)


def _ssd_chunk(c, rev, nc_ctx, nc_tot):
    if not rev:
        return c
    return jnp.where(c < nc_ctx, nc_ctx - 1 - c, nc_tot - 1 - (c - nc_ctx))


SSD_SUB = TM // TQ


def _ssd_scan(xc, dt, dtt, bias, a, biast, at, dsk, nb, s_len, lc):
    t = xc.shape[0]
    nblk = s_len // TM
    nctx = lc // TM
    fix = lambda b, c: (0, 0)

    def rows(rev):
        return lambda b, c: (b * nblk + _ssd_chunk(c, rev, nctx, nblk), 0)

    def lanes(rev):
        return lambda b, c: (b, 0, _ssd_chunk(c, rev, nctx, nblk))

    def data_specs(rev):
        return [pl.BlockSpec((TM, SSD_XBC), rows(rev)), pl.BlockSpec((TM, LANES), rows(rev)),
                pl.BlockSpec((1, SUBLANES, TM), lanes(rev))]

    state = pltpu.VMEM((SSD_HEADS * HEAD_DIM, SSD_STATE), F32)
    return pl.pallas_call(
        _ssd_kernel,
        out_shape=[jax.ShapeDtypeStruct((t, GROUP_W), F32)] * 2,
        grid=(nb, nblk),
        in_specs=data_specs(False) + data_specs(True) + [
            pl.BlockSpec((1, LANES), fix), pl.BlockSpec((1, LANES), fix),
            pl.BlockSpec((SUBLANES, TQ), fix), pl.BlockSpec((SUBLANES, TQ), fix),
            pl.BlockSpec((1, GROUP_W), fix)],
        out_specs=[pl.BlockSpec((TM, GROUP_W), rows(False)), pl.BlockSpec((TM, GROUP_W), rows(True))],
        scratch_shapes=[state, state],
        compiler_params=_cp(("parallel", "arbitrary")),
        name="ssd_scan",
    )(xc, dt, dtt, xc, dt, dtt, bias, a, biast, at, dsk)


def _ssd(xbc, dt, conv_w, conv_b, dt_bias, a_log, d_skip, nb, s_len, lc):
    xc = _conv(xbc, conv_w, conv_b, s_len, lc)
    nd = 2 * SSD_HEADS
    dtt = dt[:, :nd].reshape(nb, s_len, nd).transpose(0, 2, 1)
    bias = jnp.pad(dt_bias.reshape(1, nd), ((0, 0), (0, LANES - nd)))
    a = jnp.pad(-jnp.exp(a_log).reshape(1, nd), ((0, 0), (0, LANES - nd)))
    biast = jnp.broadcast_to(dt_bias.reshape(nd, 1), (nd, TQ))
    at = jnp.broadcast_to(-jnp.exp(a_log).reshape(nd, 1), (nd, TQ))
    dsk = jnp.repeat(d_skip, HEAD_DIM).reshape(1, GROUP_W)
    return _ssd_scan(xc, dt, dtt, bias, a, biast, at, dsk, nb, s_len, lc)


def _outproj_kernel(x_ref, ys5_ref, oga_ref, y0_ref, y1_ref, z_ref, owa_ref, mod_ref, gluw_ref, glub_ref,
                    ng_ref, wout_ref, n2_ref, wr_ref, br_ref, xn_o, h2_o, route_o, y_scr):
    _s5_unpack_kernel(ys5_ref, y_scr)
    gl = _gelu_tanh(jnp.concatenate([y_scr[0], y_scr[1]], axis=1))
    a = gl * _sigmoid(_dot(gl.astype(BF16), gluw_ref[...]) + glub_ref[...])
    m = (y0_ref[...] + y1_ref[...]) * _silu(z_ref[...])
    m = m * lax.rsqrt(jnp.mean(m * m, axis=-1, keepdims=True) + EPS) * ng_ref[...]
    w = wout_ref
    mix = (_dot(a.astype(BF16), w[0:GROUP_W, :]) + _dot(oga_ref[...].astype(BF16), w[GROUP_W:2 * GROUP_W, :])
           + _dot(m.astype(BF16), w[2 * GROUP_W:3 * GROUP_W, :]) + _dot(owa_ref[...].astype(BF16), w[3 * GROUP_W:, :]))
    xn = x_ref[...] + mod_ref[0, 2:3, :] * mix
    xn_o[...] = xn
    h2 = xn * lax.rsqrt(jnp.mean(xn * xn, axis=-1, keepdims=True) + EPS) * n2_ref[...]
    h2 = h2 * (1.0 + mod_ref[0, 4:5, :]) + mod_ref[0, 3:4, :]
    h2_o[...] = _pack_bf16_pair(h2)
    h_hi = h2.astype(BF16)
    h_lo = (h2 - h_hi.astype(F32)).astype(BF16)
    logits = _dot(h_hi, wr_ref[0]) + (_dot(h_lo, wr_ref[0]) + _dot(h_hi, wr_ref[1])) + br_ref[...]
    lane = lax.broadcasted_iota(jnp.int32, logits.shape, 1).astype(F32)
    big = float(4 * LANES)
    lcoarse = jnp.where(lane < MOE_GROUPS, logits, NEG_INF)
    mx = jnp.max(lcoarse, axis=1, keepdims=True)
    den = jnp.sum(jnp.exp(lcoarse - mx), axis=1, keepdims=True)
    grp = jnp.min(jnp.where(lcoarse == mx, lane, big), axis=1, keepdims=True)
    pg = 1.0 / den
    lo = ROUTE_FINE0 + grp * MOE_PER_GROUP
    lf = jnp.where(lane >= lo, jnp.where(lane < lo + MOE_PER_GROUP, logits, NEG_INF), NEG_INF)
    v1 = jnp.max(lf, axis=1, keepdims=True)
    i1 = jnp.min(jnp.where(lf == v1, lane, big), axis=1, keepdims=True)
    lf2 = jnp.where(lane == i1, NEG_INF, lf)
    v2 = jnp.max(lf2, axis=1, keepdims=True)
    i2 = jnp.min(jnp.where(lf2 == v2, lane, big), axis=1, keepdims=True)
    e2 = jnp.exp(v2 - v1)
    w1 = pg / (1.0 + e2)
    w2 = w1 * e2
    route = jnp.where(lane == 0, i1 - ROUTE_FINE0,
                      jnp.where(lane == 1, i2 - ROUTE_FINE0,
                                jnp.where(lane == 2, w1, jnp.where(lane == 3, w2, 0.0))))
    route_o[...] = route


def _outproj(x, ys5, oga, y0, y1, z, owa, mod, glu_w, glu_b, ssd_norm_g, w_out, norm2_g, wr, br, nb, nblk):
    t, d = x.shape
    row = lambda i: (i, 0)
    fix = lambda i: (0, 0)
    gw = pl.BlockSpec((TM, GROUP_W), row)
    wr_hi = wr.astype(BF16)
    return pl.pallas_call(
        _outproj_kernel,
        out_shape=[jax.ShapeDtypeStruct((t, d), F32), jax.ShapeDtypeStruct((t, d // 2), jnp.uint32),
                   jax.ShapeDtypeStruct((t, LANES), F32)],
        grid=(t // TM,),
        in_specs=[pl.BlockSpec((TM, d), row), pl.BlockSpec((S5_GROUPS, S5_TB, S5_BLK), lambda i: (0, i, 0)),
                  gw, gw, gw, gw, gw,
                  pl.BlockSpec((1, 6, d), lambda i: (_mod_row(i, nblk, nb), 0, 0)),
                  pl.BlockSpec((GROUP_W, GROUP_W), fix),
                  pl.BlockSpec((1, GROUP_W), fix),
                  pl.BlockSpec((1, GROUP_W), fix),
                  pl.BlockSpec((d, d), fix),
                  pl.BlockSpec((1, d), fix),
                  pl.BlockSpec((2, d, LANES), lambda i: (0, 0, 0)),
                  pl.BlockSpec((1, LANES), fix)],
        out_specs=[pl.BlockSpec((TM, d), row), pl.BlockSpec((TM, d // 2), row), pl.BlockSpec((TM, LANES), row)],
        scratch_shapes=[pltpu.VMEM((GROUP_W // LANES, TM, LANES), F32)],
        compiler_params=_cp(("parallel",)),
        name="out_proj_router",
    )(x, ys5, oga, y0, y1, z, owa, mod, glu_w.astype(BF16), glu_b.reshape(1, -1), ssd_norm_g.reshape(1, -1),
      w_out.astype(BF16), norm2_g.reshape(1, -1), jnp.stack([wr_hi, (wr - wr_hi.astype(F32)).astype(BF16)]), br)


def _pack_router(coarse_w, coarse_b, fine_w, fine_b):
    def lanes(coarse, fine):
        gap = jnp.zeros(coarse.shape[:-1] + (ROUTE_FINE0 - MOE_GROUPS,), F32)
        tail = jnp.zeros(coarse.shape[:-1] + (LANES - ROUTE_FINE0 - N_EXPERTS,), F32)
        return jnp.concatenate([coarse, gap, fine, tail], axis=-1)

    return lanes(coarse_w, fine_w), lanes(coarse_b[None, :], fine_b[None, :])


def _gather_rows(src, idx):
    m = idx.shape[0]
    d = src.shape[1]
    workers = SC_CORES * SC_SUBCORES
    nch = m // (workers * SC_GATHER_K)
    assert nch * workers * SC_GATHER_K == m
    mesh = plsc.VectorSubcoreMesh(core_axis_name="c", subcore_axis_name="s")

    @functools.partial(
        pl.kernel, mesh=mesh,
        out_type=jax.ShapeDtypeStruct((m, d), src.dtype),
        scratch_types=[pltpu.VMEM((nch, SC_GATHER_K), jnp.int32),
                       pltpu.VMEM((SC_GATHER_K, d), src.dtype),
                       pltpu.SemaphoreType.DMA],
    )
    def gather(src_hbm, idx_hbm, out_hbm, idx_v, rows_v, sem):
        wid = lax.axis_index("s") * SC_CORES + lax.axis_index("c")
        pltpu.sync_copy(idx_hbm.at[wid], idx_v)

        @pl.loop(0, nch)
        def _(j):
            off = pl.multiple_of((wid * nch + j) * SC_GATHER_K, SC_GATHER_K)
            pltpu.async_copy(src_hbm.at[idx_v.at[j]], rows_v, sem).wait()
            pltpu.sync_copy(rows_v, out_hbm.at[pl.ds(off, SC_GATHER_K)])

    return gather(src, idx.reshape(workers, nch, SC_GATHER_K))


def _scatter_rows(src, dst0, dst1, nrows):
    t, d = src.shape
    workers = SC_CORES * SC_SUBCORES
    nch = t // (workers * SC_GATHER_K)
    assert nch * workers * SC_GATHER_K == t
    mesh = plsc.VectorSubcoreMesh(core_axis_name="c", subcore_axis_name="s")

    @functools.partial(
        pl.kernel, mesh=mesh,
        out_type=jax.ShapeDtypeStruct((nrows, d), src.dtype),
        scratch_types=[pltpu.VMEM((nch, SC_GATHER_K), jnp.int32),
                       pltpu.VMEM((nch, SC_GATHER_K), jnp.int32),
                       pltpu.VMEM((SC_GATHER_K, d), src.dtype)],
    )
    def scatter(src_hbm, d0_hbm, d1_hbm, out_hbm, i0_v, i1_v, rows_v):
        wid = lax.axis_index("s") * SC_CORES + lax.axis_index("c")
        pltpu.sync_copy(d0_hbm.at[wid], i0_v)
        pltpu.sync_copy(d1_hbm.at[wid], i1_v)

        @pl.loop(0, nch)
        def _(j):
            off = pl.multiple_of((wid * nch + j) * SC_GATHER_K, SC_GATHER_K)
            pltpu.sync_copy(src_hbm.at[pl.ds(off, SC_GATHER_K)], rows_v)
            pltpu.sync_copy(rows_v, out_hbm.at[i0_v.at[j]])
            pltpu.sync_copy(rows_v, out_hbm.at[i1_v.at[j]])

    return scatter(src, dst0.reshape(workers, nch, SC_GATHER_K), dst1.reshape(workers, nch, SC_GATHER_K))


def _expert_kernel(be_ref, nused_ref, nvalid_ref, x_ref, wg_ref, wu_ref, wd_ref, o_ref, wg_s, wu_s, wd_s):
    i = pl.program_id(0)
    new_expert = jnp.logical_or(i == 0, be_ref[i] != be_ref[jnp.maximum(i - 1, 0)])

    @pl.when(jnp.logical_and(i < nused_ref[0], new_expert))
    def _():
        wg_s[...] = wg_ref[0, 0].astype(BF16)
        wu_s[...] = wu_ref[0, 0].astype(BF16)
        wd_s[...] = wd_ref[0, 0].astype(BF16)

    def swiglu(rows):
        row = rows.start + lax.broadcasted_iota(jnp.int32, (rows.stop - rows.start, x_ref.shape[1]), 0)
        lo, hi = _unpack_bf16_pair(jnp.where(row < nvalid_ref[i], x_ref[rows, :], jnp.uint32(0)))
        lo = lo.astype(BF16)
        hi = hi.astype(BF16)
        half = lo.shape[1]
        gate = _dot(lo, wg_s[0:half, :]) + _dot(hi, wg_s[half:, :])
        up = _dot(lo, wu_s[0:half, :]) + _dot(hi, wu_s[half:, :])
        o_ref[rows, :] = _pack_bf16_pair(_dot((_silu(gate) * up).astype(BF16), wd_s[...]))

    used = i < nused_ref[0]
    half_rows = MOE_TM // 2

    @pl.when(jnp.logical_and(used, nvalid_ref[i] > half_rows))
    def _():
        swiglu(slice(0, MOE_TM))

    @pl.when(jnp.logical_and(used, nvalid_ref[i] <= half_rows))
    def _():
        swiglu(slice(0, half_rows))
        o_ref[half_rows:, :] = jnp.zeros((MOE_TM - half_rows, o_ref.shape[1]), o_ref.dtype)

    @pl.when(jnp.logical_not(used))
    def _():
        o_ref[...] = jnp.zeros_like(o_ref)


def _experts(xs, blk_e, n_used, n_valid, wg, wu, wd, layer):
    rows, dp = xs.shape
    d = 2 * dp
    nblocks = rows // MOE_TM
    de = wg.shape[3]
    wsel = lambda i, be, nu, nv: (layer, be[i], 0, 0)
    grid_spec = pltpu.PrefetchScalarGridSpec(
        num_scalar_prefetch=3,
        grid=(nblocks,),
        in_specs=[pl.BlockSpec((MOE_TM, dp), lambda i, be, nu, nv: (i, 0)),
                  pl.BlockSpec((1, 1, d, de), wsel),
                  pl.BlockSpec((1, 1, d, de), wsel),
                  pl.BlockSpec((1, 1, de, d), wsel)],
        out_specs=pl.BlockSpec((MOE_TM, dp), lambda i, be, nu, nv: (i, 0)),
        scratch_shapes=[pltpu.VMEM((d, de), BF16), pltpu.VMEM((d, de), BF16), pltpu.VMEM((de, d), BF16)],
    )
    return pl.pallas_call(
        _expert_kernel,
        out_shape=jax.ShapeDtypeStruct((rows, dp), jnp.uint32),
        grid_spec=grid_spec,
        compiler_params=_cp(("arbitrary",)),
        name="moe_experts",
    )(blk_e, n_used, n_valid, xs, wg, wu, wd)


def _final_kernel(x_ref, r0_ref, r1_ref, route_ref, mod_ref, fg_ref, o_ref):
    y = _moe_residual(x_ref, r0_ref, r1_ref, route_ref, mod_ref)
    o_ref[...] = y * lax.rsqrt(jnp.mean(y * y, axis=-1, keepdims=True) + EPS) * fg_ref[...]


def _final(xn, rows2, route, mod, final_g, nb, nblk):
    t, d = xn.shape
    nlat = nblk - 1
    grid = (nb * nlat,)
    src = lambda i: ((i // nlat) * nblk + 1 + i % nlat, 0)
    modi = lambda i: (i // nlat, 0, 0)
    out_rows = nb * nlat * TM
    return pl.pallas_call(
        _final_kernel,
        out_shape=jax.ShapeDtypeStruct((out_rows, d), F32),
        grid=grid,
        in_specs=[pl.BlockSpec((TM, d), src),
                  pl.BlockSpec((TM, d // 2), src),
                  pl.BlockSpec((TM, d // 2), lambda i: (src(i)[0] + t // TM, 0)),
                  pl.BlockSpec((TM, LANES), src),
                  pl.BlockSpec((1, 6, d), modi),
                  pl.BlockSpec((1, d), lambda i: (0, 0))],
        out_specs=pl.BlockSpec((TM, d), lambda i: (i, 0)),
        compiler_params=_cp(("parallel",)),
        name="moe_combine_final",
    )(xn, rows2, rows2, route, mod, final_g.reshape(1, d))


def _moe(h2, route, wg, wu, wd, layer):
    t, d = h2.shape
    n_slots = 2 * t
    experts = jnp.arange(N_EXPERTS, dtype=F32)[None, :]
    oh0 = (route[:, 0:1] == experts).astype(F32)
    oh1 = (route[:, 1:2] == experts).astype(F32)
    both = (oh0 + oh1).reshape(t // LANES, LANES, N_EXPERTS)
    tri = jnp.tril(jnp.ones((LANES, LANES), F32))
    intra = jnp.einsum("ij,bjk->bik", tri, both)
    blk_tot = intra[:, -1, :]
    blk_cum = jnp.cumsum(blk_tot, axis=0)
    earlier = (intra - both + (blk_cum - blk_tot)[:, None, :]).reshape(t, N_EXPERTS)
    counts = blk_cum[-1].astype(jnp.int32)
    pcounts = (counts + MOE_TM - 1) // MOE_TM * MOE_TM
    pends = jnp.cumsum(pcounts)
    pstarts = pends - pcounts
    base = pstarts.astype(F32)[None, :] + earlier
    dest0 = jnp.sum(oh0 * base, axis=1).astype(jnp.int32)
    dest1 = jnp.sum(oh1 * base, axis=1).astype(jnp.int32)
    nblocks = -(-n_slots // MOE_TM) + N_EXPERTS
    nrows = -(-nblocks * MOE_TM // GATHER_ROWS) * GATHER_ROWS
    nblocks = nrows // MOE_TM
    blk_start = jnp.arange(nblocks, dtype=jnp.int32) * MOE_TM
    blk_e = jnp.minimum(jnp.sum((pends[None, :] <= blk_start[:, None]).astype(jnp.int32), axis=1), N_EXPERTS - 1)
    n_used = (pends[-1] // MOE_TM).astype(jnp.int32).reshape(1)
    n_valid = jnp.clip((pstarts + counts)[blk_e] - blk_start, 0, MOE_TM).astype(jnp.int32)
    xs = _scatter_rows(h2, dest0, dest1, nrows)
    ys = _experts(xs, blk_e, n_used, n_valid, wg, wu, wd, layer)
    return _gather_rows(ys, jnp.concatenate([dest0, dest1]))


def kernel(x, c, ctx, c_ctx, ada_w, ada_b, norm1_g, norm2_g, w_in, w_out, s5_lam_re, s5_lam_im, s5_log_dt, s5_b_re, s5_b_im, s5_c_re, s5_c_im, s5_d, s5_glu_w, s5_glu_b, ga_qn_g, ga_kn_g, ssd_conv_w, ssd_conv_b, ssd_dt_bias, ssd_a_log, ssd_d, ssd_norm_g, wa_sink, moe_coarse_w, moe_coarse_b, moe_fine_w, moe_fine_b, moe_w_gate, moe_w_up, moe_w_down, final_g):
    nb, l, d = x.shape
    lc = ctx.shape[1]
    depth = ada_w.shape[0]
    assert lc == TM and l % TM == 0 and nb <= SUBLANES - 1 and d == D_MODEL
    s_len = lc + l
    nblk = s_len // TM
    t = nb * s_len

    cc = jnp.zeros((SUBLANES, d), F32).at[:nb].set(c).at[nb].set(c_ctx)
    mods = _ada(cc, ada_w, ada_b).reshape(depth, SUBLANES, 6, d)
    cos_t, sin_t = _rope_tables(lc, l)
    w_packed = jax.vmap(_pack_w_in)(w_in)
    s5_tabs = jax.vmap(_s5_params)(s5_lam_re, s5_lam_im, s5_log_dt, s5_b_re, s5_b_im, s5_c_re, s5_c_im, s5_d)
    wrs, brs = jax.vmap(_pack_router)(moe_coarse_w, moe_coarse_b, moe_fine_w, moe_fine_b)

    src = ("first", x.reshape(nb * l, d), ctx.reshape(nb * lc, d))
    for i in range(depth):
        mod = mods[i]
        (xm, xbc, ug, z, dt, gaq, gak, gav, waq, wak, wav) = _inproj(
            src, mod, norm1_g[i], w_packed[i], cos_t, sin_t, ga_qn_g[i], ga_kn_g[i], nb, nblk)
        ys5 = _s5(ug, tuple(tab[i] for tab in s5_tabs), nb, s_len, lc)
        oga, owa = _attn(wa_sink[i], gaq, gak, gav, waq, wak, wav, nb, s_len, lc)
        y0, y1 = _ssd(xbc, dt, ssd_conv_w[i], ssd_conv_b[i], ssd_dt_bias[i], ssd_a_log[i], ssd_d[i], nb, s_len, lc)
        wr, br = wrs[i], brs[i]
        xn, h2, route = _outproj(xm, ys5, oga, y0, y1, z, owa, mod, s5_glu_w[i], s5_glu_b[i], ssd_norm_g[i],
                                 w_out[i], norm2_g[i], wr, br, nb, nblk)
        rows2 = _moe(h2, route, moe_w_gate, moe_w_up, moe_w_down, i)
        src = ("moe", xn, rows2, route, mod)
    return _final(xn, rows2, route, mod, final_g, nb, nblk).reshape(nb, l, d)
```

```python
import functools
import math

import jax
import jax.numpy as jnp
import numpy as np
from jax import lax
from jax.experimental import pallas as pl
from jax.experimental.pallas import tpu as pltpu
from jax.experimental.pallas import tpu_sc as plsc

F32 = jnp.float32
BF16 = jnp.bfloat16
HI = lax.Precision.HIGHEST

D_MODEL = 1024
GRID_W = 64
GROUP_W = 256
HEAD_DIM = 64
ROPE_FREQS = HEAD_DIM // 4
ROPE_BASE = 10000.0
EPS = 1e-6
S5_CH = 16
S5_GROUPS = GROUP_W // S5_CH
S5_STATE = 64
N_HEADS = 4
SSD_HEADS = 4
SSD_NGROUPS = 2
SSD_STATE = 128
SSD_XBC = GROUP_W + 2 * SSD_NGROUPS * SSD_STATE
WINDOW = 128
MOE_GROUPS = 4
MOE_PER_GROUP = 8
N_EXPERTS = 32
D_EXPERT = D_MODEL // 2

LANES = 128
SUBLANES = 8
TM = 256
TQ = 128
GA_TQ = 128
GA_SUB = 2
S5_Q = 32
S5_BLK = S5_Q * S5_CH
MOE_TM = 512
SC_CORES = 2
SC_SUBCORES = 16
SC_GATHER_K = 32
SC_FETCH_K = 64
GATHER_ROWS = SC_CORES * SC_SUBCORES * SC_GATHER_K
ROUTE_FINE0 = 32
VMEM_LIMIT = 56 * 1024 * 1024

NEG_INF = float("-inf")
LOG2E = math.log2(math.e)


def _cp(sem, vmem=VMEM_LIMIT):
    return pltpu.CompilerParams(dimension_semantics=sem, vmem_limit_bytes=vmem)


def _dot(a, b):
    return jnp.dot(a, b, preferred_element_type=F32)


def _dot_hi(a, b):
    return jnp.dot(a, b, preferred_element_type=F32, precision=HI)


def _dot_nt(a, b):
    return lax.dot_general(a, b, (((1,), (1,)), ((), ())), preferred_element_type=F32)


def _sigmoid(x):
    return 1.0 / (1.0 + jnp.exp(-x))


def _silu(x):
    return x * _sigmoid(x)


def _gelu_tanh(x):
    return 0.5 * x * (1.0 + jnp.tanh(math.sqrt(2.0 / math.pi) * (x + 0.044715 * (x * x * x))))


def _softplus(x):
    return jnp.maximum(x, 0.0) + jnp.log(1.0 + jnp.exp(-jnp.abs(x)))


_HI16 = 0xFFFF0000


def _pack_bf16_pair(x):
    n = x.shape[1] // 2
    bits = pltpu.bitcast(x.astype(BF16).astype(F32), jnp.uint32)
    return (bits[:, n:] & jnp.uint32(_HI16)) | (bits[:, :n] >> 16)


def _unpack_bf16_pair(w):
    return pltpu.bitcast(w << 16, F32), pltpu.bitcast(w & jnp.uint32(_HI16), F32)


def _per_head_cols(v, base, n_heads, shape):
    lane = lax.broadcasted_iota(jnp.int32, shape, 1)
    out = jnp.broadcast_to(v[:, base + n_heads - 1:base + n_heads], shape)
    for h in range(n_heads - 2, -1, -1):
        out = jnp.where(lane < (h + 1) * HEAD_DIM, v[:, base + h:base + h + 1], out)
    return out


def _ada_kernel(c_ref, w_ref, b_ref, o_ref):
    c = c_ref[...]
    o_ref[0] = _dot_hi(_silu(c), w_ref[0]) + b_ref[0]


def _ada(cc, ada_w, ada_b):
    depth, d, n = ada_w.shape
    tn = 1536
    return pl.pallas_call(
        _ada_kernel,
        out_shape=jax.ShapeDtypeStruct((depth, SUBLANES, n), F32),
        grid=(depth, n // tn),
        in_specs=[pl.BlockSpec((SUBLANES, d), lambda l, j: (0, 0)),
                  pl.BlockSpec((1, d, tn), lambda l, j: (l, 0, j)),
                  pl.BlockSpec((1, 1, tn), lambda l, j: (l, 0, j))],
        out_specs=pl.BlockSpec((1, SUBLANES, tn), lambda l, j: (l, 0, j)),
        compiler_params=_cp(("parallel", "parallel")),
        name="ada_mod",
    )(cc, ada_w, ada_b.reshape(depth, 1, n))


_C_XBC = 0
_C_U = _C_XBC + SSD_XBC
_C_Z = _C_U + GROUP_W
_C_DT = _C_Z + GROUP_W
_C_GAQ = _C_DT + LANES
_C_WAQ = _C_GAQ + N_HEADS * LANES
_C_GAK = _C_WAQ + N_HEADS * LANES
_C_GAV = _C_GAK + LANES
_C_WAK = _C_GAV + LANES
_C_WAV = _C_WAK + LANES
_C_END = _C_WAV + LANES


def _expand_q_cols(wq):
    zero = jnp.zeros((wq.shape[0], HEAD_DIM), wq.dtype)
    parts = []
    for h in range(N_HEADS):
        head = wq[:, h * HEAD_DIM:(h + 1) * HEAD_DIM]
        parts += [head, zero] if h // 2 == 0 else [zero, head]
    return jnp.concatenate(parts, axis=1)


def _pack_w_in(w_in):
    cuts = np.cumsum([256, 256, 128, 128, 256, SSD_XBC, 2 * SSD_HEADS, 256, 128, 128])[:-1]
    u, gaq, gak, gav, z, xbc, dt, waq, wak, wav = jnp.split(w_in, [int(c) for c in cuts], axis=1)
    dt = jnp.pad(dt, ((0, 0), (0, LANES - dt.shape[1])))
    w = jnp.concatenate([xbc, u, z, dt, _expand_q_cols(gaq), _expand_q_cols(waq), gak, gav, wak, wav], axis=1)
    return w.astype(BF16)


def _rope(x, cos, sins):
    w = x.shape[1]
    if w > LANES:
        cos = jnp.concatenate([cos] * (w // LANES), axis=1)
        sins = jnp.concatenate([sins] * (w // LANES), axis=1)
    lane = lax.broadcasted_iota(jnp.int32, x.shape, 1)
    up = pltpu.roll(x, w - ROPE_FREQS, 1)
    dn = pltpu.roll(x, ROPE_FREQS, 1)
    partner = jnp.where((lane & ROPE_FREQS) == 0, up, dn)
    return x * cos + partner * sins


def _moe_residual(xn_ref, r0_ref, r1_ref, route_ref, mod_ref):
    route = route_ref[...]
    r0 = jnp.concatenate(_unpack_bf16_pair(r0_ref[...]), axis=1)
    r1 = jnp.concatenate(_unpack_bf16_pair(r1_ref[...]), axis=1)
    return xn_ref[...] + mod_ref[0, 5:6, :] * (route[:, 2:3] * r0 + route[:, 3:4] * r1)


ROW_SUB = 2


def _row_views(refs, s):
    return [r.at[pl.ds(s * TM, TM), :] for r in refs]


def _inproj_kernel(*refs, first, nblk):
    n_blk_in = (2 if first else 5) + 3
    shared = refs[ROW_SUB * n_blk_in:]
    g_ref, w_ref, qn_ref, kn_ref = shared[:4]
    xm_o, xbc_o, ug_o = shared[4:7]
    rest_o = shared[7:-1]
    u_scr = shared[-1]
    for s in range(ROW_SUB):
        blk_refs = refs[s * n_blk_in:(s + 1) * n_blk_in]
        xm_v, xbc_v = _row_views((xm_o, xbc_o), s)
        ug_v = ug_o.at[:, pl.ds(s * S5_TB, S5_TB), :]
        _inproj_block(blk_refs, g_ref, w_ref, qn_ref, kn_ref, xm_v, xbc_v, ug_v, _row_views(rest_o, s), u_scr,
                      first, (pl.program_id(0) * ROW_SUB + s) % nblk == 0)


def _inproj_block(blk_refs, g_ref, w_ref, qn_ref, kn_ref, xm_o, xbc_o, ug_o, rest_o, u_scr, first, is_ctx):
    if first:
        lat_ref, ctx_ref = blk_refs[:2]
        x = jnp.where(is_ctx, ctx_ref[...], lat_ref[...])
    else:
        x = _moe_residual(*blk_refs[:5])
    mod_ref, cos_ref, sin_ref = blk_refs[-3:]
    z_o, dt_o, gaq_o, gak_o, gav_o, waq_o, wak_o, wav_o = rest_o
    xm_o[...] = x
    ms = jnp.mean(x * x, axis=-1, keepdims=True)
    xn = x * lax.rsqrt(ms + EPS) * g_ref[...]
    h = xn * (1.0 + mod_ref[0, 1:2, :]) + mod_ref[0, 0:1, :]
    hb = h.astype(BF16)

    def proj(lo, hi):
        return _dot(hb, w_ref[:, lo:hi])

    cos = cos_ref[...]
    sins = sin_ref[...]
    scale = LOG2E * HEAD_DIM ** -0.5
    q = proj(_C_GAQ, _C_WAQ)
    qs = q * q
    inv = jnp.concatenate(
        [jnp.broadcast_to(lax.rsqrt(jnp.sum(qs[:, s * LANES:(s + 1) * LANES], axis=1, keepdims=True)
                                    * (1.0 / HEAD_DIM) + EPS), (q.shape[0], LANES)) for s in range(N_HEADS)], axis=1)
    gaq_o[...] = (_rope(q * inv * qn_ref[...], cos, sins) * scale).astype(BF16)
    waq_o[...] = (_rope(proj(_C_WAQ, _C_GAK), cos, sins) * scale).astype(BF16)
    k = proj(_C_GAK, _C_GAV)
    ks = k * k
    lane = lax.broadcasted_iota(jnp.int32, k.shape, 1)
    lo = lane < HEAD_DIM
    ms0 = jnp.sum(jnp.where(lo, ks, 0.0), axis=1, keepdims=True)
    ms1 = jnp.sum(jnp.where(lo, 0.0, ks), axis=1, keepdims=True)
    kinv = lax.rsqrt(jnp.where(lo, ms0, ms1) * (1.0 / HEAD_DIM) + EPS)
    gak_o[...] = _rope(k * kinv * kn_ref[...], cos, sins).astype(BF16)
    gav_o[...] = proj(_C_GAV, _C_WAK).astype(BF16)
    wak_o[...] = _rope(proj(_C_WAK, _C_WAV), cos, sins).astype(BF16)
    wav_o[...] = proj(_C_WAV, _C_END).astype(BF16)
    xbc_o[...] = proj(_C_XBC, _C_U)
    u = proj(_C_U, _C_Z)
    u_scr[0] = u[:, :LANES]
    u_scr[1] = u[:, LANES:]
    _s5_pack_kernel(u_scr.at[0], u_scr.at[1], ug_o)
    z_o[...] = proj(_C_Z, _C_DT)
    dt_o[...] = proj(_C_DT, _C_GAQ)


def _mod_row(i, nblk, nb):
    return jnp.where(i % nblk == 0, nb, i // nblk)


def _inproj(src, mod, norm_g, w_packed, cos_t, sin_t, qn_g, kn_g, nb, nblk):
    first = src[0] == "first"
    d = src[1].shape[1]
    t = nb * nblk * TM
    row = lambda i: (i, 0)
    fix = lambda i: (0, 0)
    nsteps = t // (ROW_SUB * TM)
    assert nsteps * ROW_SUB * TM == t

    def blk_specs(s):
        bid = lambda i: ROW_SUB * i + s
        modspec = pl.BlockSpec((1, 6, d), lambda i: (_mod_row(bid(i), nblk, nb), 0, 0))
        table = pl.BlockSpec((TM, LANES), lambda i: (bid(i) % nblk, 0))
        if first:
            srcs = [pl.BlockSpec((TM, d), lambda i: ((bid(i) // nblk) * (nblk - 1) + jnp.maximum(bid(i) % nblk - 1, 0), 0)),
                    pl.BlockSpec((TM, d), lambda i: (bid(i) // nblk, 0))]
        else:
            srcs = [pl.BlockSpec((TM, d), lambda i: (bid(i), 0)), pl.BlockSpec((TM, d // 2), lambda i: (bid(i), 0)),
                    pl.BlockSpec((TM, d // 2), lambda i: (bid(i) + t // TM, 0)),
                    pl.BlockSpec((TM, LANES), lambda i: (bid(i), 0)), modspec]
        return srcs + [modspec, table, table]

    if first:
        blk_args = tuple(src[1:]) + (mod, cos_t, sin_t)
    else:
        blk_args = (src[1], src[2], src[2], src[3], src[4], mod, cos_t, sin_t)
    outs = [(d, F32), (SSD_XBC, F32), None, (GROUP_W, F32), (LANES, F32),
            (N_HEADS * LANES, BF16), (LANES, BF16), (LANES, BF16),
            (N_HEADS * LANES, BF16), (LANES, BF16), (LANES, BF16)]
    shapes = [jax.ShapeDtypeStruct((t, o[0]), o[1]) if o else
              jax.ShapeDtypeStruct((S5_GROUPS, t // S5_Q, S5_BLK), F32) for o in outs]
    specs = [pl.BlockSpec((ROW_SUB * TM, o[0]), row) if o else
             pl.BlockSpec((S5_GROUPS, ROW_SUB * S5_TB, S5_BLK), lambda i: (0, i, 0)) for o in outs]
    return pl.pallas_call(
        functools.partial(_inproj_kernel, first=first, nblk=nblk),
        out_shape=shapes,
        grid=(nsteps,),
        in_specs=[sp for s in range(ROW_SUB) for sp in blk_specs(s)] + [
                  pl.BlockSpec((1, d), fix),
                  pl.BlockSpec((d, _C_END), fix),
                  pl.BlockSpec((1, N_HEADS * LANES), fix),
                  pl.BlockSpec((1, LANES), fix)],
        out_specs=specs,
        scratch_shapes=[pltpu.VMEM((GROUP_W // LANES, TM, LANES), F32)],
        compiler_params=_cp(("parallel",)),
        name="in_proj",
    )(*(blk_args * ROW_SUB), norm_g.reshape(1, d), w_packed,
      jnp.tile(qn_g, 2 * N_HEADS).reshape(1, -1), jnp.tile(kn_g, 2).reshape(1, -1))


def _rope_tables(lc, l):
    n_rows = l // GRID_W
    rows = np.repeat(np.arange(n_rows), GRID_W)
    cols = np.tile(np.arange(GRID_W), n_rows)
    inv = np.power(np.float32(ROPE_BASE), -np.arange(ROPE_FREQS, dtype=np.float32) / ROPE_FREQS)
    ang = np.stack([rows, cols], axis=-1).astype(np.float32)[..., None] * inv
    cos = np.cos(ang)
    sin = np.sin(ang)
    cos64 = np.stack([cos, cos], axis=2).reshape(l, HEAD_DIM)
    sin64 = np.stack([-sin, sin], axis=2).reshape(l, HEAD_DIM)
    cos64 = np.concatenate([np.ones((lc, HEAD_DIM), np.float32), cos64], axis=0)
    sin64 = np.concatenate([np.zeros((lc, HEAD_DIM), np.float32), sin64], axis=0)
    return (jnp.asarray(np.tile(cos64, (1, 2)), dtype=F32), jnp.asarray(np.tile(sin64, (1, 2)), dtype=F32))


def _merge_heads(o2, kvh):
    tq = o2.shape[0] // 2
    oa, ob = o2[:tq], o2[tq:]
    lane = lax.broadcasted_iota(jnp.int32, oa.shape, 1)
    if kvh == 0:
        return jnp.where(lane < HEAD_DIM, oa, pltpu.roll(ob, HEAD_DIM, 1))
    return jnp.where(lane < HEAD_DIM, pltpu.roll(oa, HEAD_DIM, 1), ob)


def _stack_q(q_ref, rows, kvh):
    return jnp.concatenate([q_ref[rows, (2 * kvh) * LANES:(2 * kvh + 1) * LANES],
                            q_ref[rows, (2 * kvh + 1) * LANES:(2 * kvh + 2) * LANES]], axis=0)


def _ga_attend(q_ref, k_ref, v_ref, o_ref, nkeys):
    k = k_ref[0:nkeys, :]
    v = v_ref[0:nkeys, :]
    lane = lax.broadcasted_iota(jnp.int32, v.shape, 1)
    vaug = [jnp.where((lane < HEAD_DIM) if kvh == 0 else (lane >= HEAD_DIM), v, jnp.ones_like(v))
            for kvh in range(2)]
    for sub in range(GA_SUB):
        rows = slice(sub * GA_TQ, (sub + 1) * GA_TQ)
        scores = [_dot_nt(_stack_q(q_ref, rows, kvh), k) for kvh in range(2)]
        outs = []
        for kvh in range(2):
            s = scores[kvh]
            p = jnp.exp2((s - jnp.max(s, axis=1, keepdims=True)).astype(BF16))
            o2 = _dot(p, vaug[kvh])
            outs.append(_merge_heads(o2 / pltpu.roll(o2, HEAD_DIM, 1), kvh))
        o_ref[rows, :] = jnp.concatenate(outs, axis=1)


def _attn_kernel(sink_ref, gq_ref, gk_ref, gv_ref, wq_ref, wk_ref, wv_ref, og_ref, ow_ref, *, lc):
    is_ctx = pl.program_id(1) < lc // TM

    @pl.when(is_ctx)
    def _():
        _ga_attend(gq_ref, gk_ref, gv_ref, og_ref, lc)
        _wa_attend(sink_ref, wq_ref, wk_ref, wv_ref, ow_ref, lc)

    @pl.when(jnp.logical_not(is_ctx))
    def _():
        _ga_attend(gq_ref, gk_ref, gv_ref, og_ref, gk_ref.shape[0])
        _wa_attend(sink_ref, wq_ref, wk_ref, wv_ref, ow_ref, lc)


def _attn(sink, gq, gk, gv, wq, wk, wv, nb, s_len, lc):
    t = gq.shape[0]
    nq = s_len // TM
    assert GA_SUB * GA_TQ == TM and WA_SUB * TQ == TM
    qspec = pl.BlockSpec((TM, N_HEADS * LANES), lambda b, j: (b * nq + j, 0))
    kvspec = pl.BlockSpec((s_len, LANES), lambda b, j: (b, 0))
    ospec = pl.BlockSpec((TM, GROUP_W), lambda b, j: (b * nq + j, 0))
    return pl.pallas_call(
        functools.partial(_attn_kernel, lc=lc),
        out_shape=[jax.ShapeDtypeStruct((t, GROUP_W), F32)] * 2,
        grid=(nb, nq),
        in_specs=[pl.BlockSpec(memory_space=pltpu.SMEM), qspec, kvspec, kvspec, qspec, kvspec, kvspec],
        out_specs=[ospec, ospec],
        compiler_params=_cp(("parallel", "arbitrary")),
        name="attention",
    )(sink, gq, gk, gv, wq, wk, wv)


WA_SUB = TM // TQ


def _wa_attend(sink_ref, q_ref, k_ref, v_ref, o_ref, lc):
    s_len = k_ref.shape[0]
    kc = k_ref[0:lc, :]
    vc = v_ref[0:lc, :]
    lane_c = lax.broadcasted_iota(jnp.int32, vc.shape, 1)
    lane_b = lax.broadcasted_iota(jnp.int32, (3 * TQ, LANES), 1)
    row = lax.broadcasted_iota(jnp.int32, (2 * TQ, 1), 0)
    for sub in range(WA_SUB):
        rows = slice(sub * TQ, (sub + 1) * TQ)
        n = pl.program_id(1) * WA_SUB + sub - lc // TQ
        start = pl.multiple_of(jnp.clip(lc + (n - 1) * TQ, lc, s_len - 3 * TQ), TQ)
        kb = k_ref[pl.ds(start, 3 * TQ), :]
        vb = v_ref[pl.ds(start, 3 * TQ), :]
        qpos = n * TQ + lax.broadcasted_iota(jnp.int32, (TQ, 3 * TQ), 0)
        kpos = (start - lc) + lax.broadcasted_iota(jnp.int32, (TQ, 3 * TQ), 1)
        reach = jnp.where(n >= 0, WINDOW, -1)
        valid = jnp.abs(qpos - kpos) <= reach
        valid = jnp.concatenate([valid, valid], axis=0)
        outs = []
        for kvh in range(2):
            q2 = jnp.concatenate([q_ref[rows, (2 * kvh) * LANES:(2 * kvh + 1) * LANES],
                                  q_ref[rows, (2 * kvh + 1) * LANES:(2 * kvh + 2) * LANES]], axis=0)
            sc = _dot_nt(q2, kc)
            sb = jnp.where(valid, _dot_nt(q2, kb), NEG_INF)
            sink = jnp.where(row < TQ, sink_ref[2 * kvh], sink_ref[2 * kvh + 1]) * LOG2E
            m = jnp.maximum(jnp.maximum(jnp.max(sc, axis=1, keepdims=True), jnp.max(sb, axis=1, keepdims=True)), sink)
            pc = jnp.exp2((sc - m).astype(BF16))
            pb = jnp.exp2((sb - m).astype(BF16))
            own_c = (lane_c < HEAD_DIM) if kvh == 0 else (lane_c >= HEAD_DIM)
            own_b = (lane_b < HEAD_DIM) if kvh == 0 else (lane_b >= HEAD_DIM)
            o2 = _dot(pc, jnp.where(own_c, vc, jnp.ones_like(vc))) + _dot(pb, jnp.where(own_b, vb, jnp.ones_like(vb)))
            denom = pltpu.roll(o2, HEAD_DIM, 1) + jnp.exp2(sink - m)
            outs.append(_merge_heads(o2 / denom, kvh))
        o_ref[rows, :] = jnp.concatenate(outs, axis=1)


def _s5_chunk_index(t, rev, nc_ctx, nc_tot):
    if not rev:
        return t
    return jnp.where(t < nc_ctx, nc_ctx - 1 - t, nc_tot - 1 - (t - nc_ctx))


def _s5_kernel(u_ref, k_ref, p_ref, g_ref, ar_ref, ai_ref, dsk_ref, y_ref, s_scr, h_scr, m_scr, *, nb, nc_ctx, nc_tot):
    for d in range(2):
        ext = k_ref[d, 0]
        for s in range(S5_Q):
            lo = ((S5_Q - s) if d == 0 else (S5_Q - 1 - s)) * S5_CH
            win = pltpu.roll(ext, (2 * S5_BLK - lo) % (2 * S5_BLK), 1)[:, :S5_BLK]
            m_scr[d, s * S5_CH:(s + 1) * S5_CH, :] = win.astype(BF16)
    uf = u_ref[0]
    u = uf.astype(BF16)
    for d in range(2):
        for k in range(2):
            s_scr[d, k] = _dot(u, p_ref[d, k, 0])
    ar = [jnp.broadcast_to(ar_ref[d, 0], (nb, LANES)) for d in range(2)]
    ai = [[jnp.broadcast_to(ai_ref[d, k, 0], (nb, LANES)) for k in range(2)] for d in range(2)]

    def body(t, carry):
        out = []
        for d in range(2):
            h, hs = carry[d]
            rows = pl.ds(_s5_chunk_index(t, d == 1, nc_ctx, nc_tot), nb, stride=nc_tot)
            h_scr[d, rows, :] = h
            out.append((ar[d] * h + ai[d][0] * hs + s_scr[d, 0, rows, :],
                        ar[d] * hs + ai[d][1] * h + s_scr[d, 1, rows, :]))
        return tuple(out)

    zero = jnp.zeros((nb, LANES), F32)
    lax.fori_loop(0, nc_tot, body, ((zero, zero), (zero, zero)), unroll=2)
    y = uf * dsk_ref[0]
    for d in range(2):
        y = y + _dot(u, m_scr[d]) + _dot(h_scr[d].astype(BF16), g_ref[d, 0])
    y_ref[0] = y


S5_TB = TM // S5_Q
S5_GPS = LANES // S5_CH


def _s5_pack_kernel(lo_ref, hi_ref, o_ref):
    for s in range(S5_Q):
        rows = pl.ds(s, S5_TB, stride=S5_Q)
        halves = (lo_ref[rows, :], hi_ref[rows, :])
        dst = S5_CH * (s % S5_GPS)
        for g in range(S5_GROUPS):
            slab = halves[g // S5_GPS]
            src = S5_CH * (g % S5_GPS)
            moved = slab if src == dst else pltpu.roll(slab, (dst - src) % LANES, 1)
            o_ref[g, :, s * S5_CH:(s + 1) * S5_CH] = moved[:, dst:dst + S5_CH]


def _s5_unpack_kernel(y_ref, o_ref):
    lane_grp = lax.broadcasted_iota(jnp.int32, (S5_TB, LANES), 1) // S5_CH
    for s in range(S5_Q):
        src = S5_CH * (s % S5_GPS)
        for half in range(S5_GROUPS // S5_GPS):
            acc = None
            for gl in range(S5_GPS):
                slab = y_ref[half * S5_GPS + gl, :, (s // S5_GPS) * LANES:(s // S5_GPS + 1) * LANES]
                dst = S5_CH * gl
                moved = slab if src == dst else pltpu.roll(slab, (dst - src) % LANES, 1)
                acc = moved if acc is None else jnp.where(lane_grp == gl, moved, acc)
            o_ref[half, pl.ds(s, S5_TB, stride=S5_Q), :] = acc


def _s5_params(lam_re, lam_im, log_dt, b_re, b_im, c_re, c_im, d_skip):
    q = S5_Q
    dt = jnp.exp(log_dt)[..., None]
    lr, li = lam_re, lam_im
    mag = jnp.exp(lr * dt)
    a_re = mag * jnp.cos(li * dt)
    a_im = mag * jnp.sin(li * dt)
    den = lr * lr + li * li
    f_re = ((a_re - 1.0) * lr + a_im * li) / den
    f_im = (a_im * lr - (a_re - 1.0) * li) / den
    bb_re = f_re[..., None] * b_re - f_im[..., None] * b_im
    bb_im = f_re[..., None] * b_im + f_im[..., None] * b_re
    kk = jnp.arange(q + 1, dtype=F32)[:, None, None, None]
    pmag = jnp.exp(kk * (lr * dt))
    pw_re = pmag * jnp.cos(kk * (li * dt))
    pw_im = pmag * jnp.sin(kk * (li * dt))
    lw_re = pw_re[:q].transpose(1, 2, 0, 3)[:, :, :, None, :]
    lw_im = pw_im[:q].transpose(1, 2, 0, 3)[:, :, :, None, :]
    ck_re = c_re[:, :, None] * lw_re - c_im[:, :, None] * lw_im
    ck_im = c_re[:, :, None] * lw_im + c_im[:, :, None] * lw_re
    ck = jnp.concatenate([ck_re, -ck_im], axis=-1).reshape(2, S5_GROUPS, S5_BLK, 2 * S5_STATE)
    kern_t = jnp.einsum("dgmp,dgpc->dgcm", ck, jnp.concatenate([bb_re, bb_im], axis=2), precision=HI)
    kern_t = kern_t.reshape(2, S5_GROUPS, S5_CH, q, S5_CH)
    zeros = jnp.zeros_like(kern_t)
    bbt_re = bb_re.transpose(0, 1, 3, 2)[:, :, None]
    bbt_im = bb_im.transpose(0, 1, 3, 2)[:, :, None]
    ct_re = c_re.transpose(0, 1, 3, 2)[:, :, :, None, :]
    ct_im = c_im.transpose(0, 1, 3, 2)[:, :, :, None, :]
    ms, ps, gs = [], [], []
    for d in range(2):
        ext = (jnp.concatenate([zeros[d], kern_t[d]], axis=2) if d == 0
               else jnp.concatenate([kern_t[d, :, :, ::-1], zeros[d]], axis=2))
        ext = ext.reshape(S5_GROUPS, S5_CH, 2 * S5_BLK)
        ms.append(ext)
        pidx = (q - 1 - jnp.arange(q)) if d == 0 else jnp.arange(q)
        pr = pw_re[pidx, d].transpose(1, 0, 2)[:, :, None, :]
        pi = pw_im[pidx, d].transpose(1, 0, 2)[:, :, None, :]
        p_re = pr * bbt_re[d] - pi * bbt_im[d]
        p_im = pr * bbt_im[d] + pi * bbt_re[d]
        pd = jnp.stack([jnp.concatenate([p_re, p_im], axis=3), jnp.concatenate([p_im, p_re], axis=3)])
        ps.append(pd.reshape(2, S5_GROUPS, S5_BLK, 2 * S5_STATE))
        gidx = (jnp.arange(q) + 1) if d == 0 else (q - jnp.arange(q))
        gw_re = pw_re[gidx, d].transpose(1, 2, 0)[..., None]
        gw_im = pw_im[gidx, d].transpose(1, 2, 0)[..., None]
        g_re = ct_re[d] * gw_re - ct_im[d] * gw_im
        g_im = ct_re[d] * gw_im + ct_im[d] * gw_re
        gs.append(jnp.concatenate([g_re, -g_im], axis=1).reshape(S5_GROUPS, 2 * S5_STATE, S5_BLK))
    ar = jnp.concatenate([pw_re[q], pw_re[q]], axis=-1)[:, :, None, :]
    ai = jnp.stack([jnp.concatenate([-pw_im[q], pw_im[q]], axis=-1),
                    jnp.concatenate([pw_im[q], -pw_im[q]], axis=-1)], axis=1)[:, :, :, None, :]
    dsk = jnp.tile(d_skip.reshape(S5_GROUPS, 1, S5_CH), (1, 1, q))
    return (jnp.stack(ms), jnp.stack(ps).astype(BF16), jnp.stack(gs).astype(BF16),
            ar.astype(F32), ai.astype(F32), dsk.astype(F32))


def _s5(ug, params, nb, s_len, lc):
    m, p, g, ar, ai, dsk = params
    nc_tot = s_len // S5_Q
    nc_ctx = lc // S5_Q
    r = nb * nc_tot
    return pl.pallas_call(
        functools.partial(_s5_kernel, nb=nb, nc_ctx=nc_ctx, nc_tot=nc_tot),
        out_shape=jax.ShapeDtypeStruct((S5_GROUPS, r, S5_BLK), F32),
        grid=(S5_GROUPS,),
        in_specs=[pl.BlockSpec((1, r, S5_BLK), lambda gi: (gi, 0, 0)),
                  pl.BlockSpec((2, 1, S5_CH, 2 * S5_BLK), lambda gi: (0, gi, 0, 0)),
                  pl.BlockSpec((2, 2, 1, S5_BLK, 2 * S5_STATE), lambda gi: (0, 0, gi, 0, 0)),
                  pl.BlockSpec((2, 1, 2 * S5_STATE, S5_BLK), lambda gi: (0, gi, 0, 0)),
                  pl.BlockSpec((2, 1, 1, 2 * S5_STATE), lambda gi: (0, gi, 0, 0)),
                  pl.BlockSpec((2, 2, 1, 1, 2 * S5_STATE), lambda gi: (0, 0, gi, 0, 0)),
                  pl.BlockSpec((1, 1, S5_BLK), lambda gi: (gi, 0, 0))],
        out_specs=pl.BlockSpec((1, r, S5_BLK), lambda gi: (gi, 0, 0)),
        scratch_shapes=[pltpu.VMEM((2, 2, r, 2 * S5_STATE), F32), pltpu.VMEM((2, r, 2 * S5_STATE), F32),
                        pltpu.VMEM((2, S5_BLK, S5_BLK), BF16)],
        compiler_params=_cp(("parallel",)),
        name="s5_scan",
    )(ug, m, p, g, ar, ai, dsk)


CONV_ROWS = 2 * TM


def _conv_kernel(x_ref, prev_ref, next_ref, w_ref, b_ref, o_ref, *, s_len, lc):
    x = x_ref[...]
    rows = x.shape[0]
    ridx = lax.broadcasted_iota(jnp.int32, x.shape, 0)
    pos = (pl.program_id(0) * rows) % s_len + ridx
    pos = jnp.where(pos >= s_len, pos - s_len, pos)
    seg_first = jnp.logical_or(pos == 0, pos == lc)
    seg_last = jnp.logical_or(pos == lc - 1, pos == s_len - 1)
    xm = jnp.where(ridx == 0, prev_ref[SUBLANES - 1:SUBLANES, :], pltpu.roll(x, 1, 0))
    xp = jnp.where(ridx == rows - 1, next_ref[0:1, :], pltpu.roll(x, rows - 1, 0))
    xm = jnp.where(seg_first, 0.0, xm)
    xp = jnp.where(seg_last, 0.0, xp)
    y = xm * w_ref[0:1, :] + x * w_ref[1:2, :] + xp * w_ref[2:3, :] + b_ref[...]
    o_ref[...] = _silu(y)


def _conv(xbc, w, b, s_len, lc):
    t, c = xbc.shape
    per = CONV_ROWS // SUBLANES
    last = t // SUBLANES - 1
    return pl.pallas_call(
        functools.partial(_conv_kernel, s_len=s_len, lc=lc),
        out_shape=jax.ShapeDtypeStruct((t, c), F32),
        grid=(t // CONV_ROWS,),
        in_specs=[pl.BlockSpec((CONV_ROWS, c), lambda i: (i, 0)),
                  pl.BlockSpec((SUBLANES, c), lambda i: (jnp.maximum(i * per - 1, 0), 0)),
                  pl.BlockSpec((SUBLANES, c), lambda i: (jnp.minimum((i + 1) * per, last), 0)),
                  pl.BlockSpec((3, c), lambda i: (0, 0)),
                  pl.BlockSpec((1, c), lambda i: (0, 0))],
        out_specs=pl.BlockSpec((CONV_ROWS, c), lambda i: (i, 0)),
        compiler_params=_cp(("parallel",)),
        name="ssd_conv",
    )(xbc, xbc, xbc, w, b.reshape(1, c))


_X_B = GROUP_W
_X_C = GROUP_W + SSD_NGROUPS * SSD_STATE


def _ssd_kernel(xf_ref, dtf_ref, dttf_ref, xr_ref, dtr_ref, dttr_ref, bias_ref, a_ref, biast_ref, at_ref, dsk_ref,
                yf_ref, yr_ref, stf_ref, str_ref):
    @pl.when(pl.program_id(1) == 0)
    def _():
        stf_ref[...] = jnp.zeros_like(stf_ref)
        str_ref[...] = jnp.zeros_like(str_ref)

    par = (bias_ref[...], a_ref[...], biast_ref[...], at_ref[...], dsk_ref[...])
    for j in range(SSD_SUB):
        rf = slice(j * TQ, (j + 1) * TQ)
        yf_ref[rf, :] = _ssd_chunk_step(xf_ref[rf, :], dtf_ref[rf, :], dttf_ref[0, :, rf], par, stf_ref, False)
        rr = slice((SSD_SUB - 1 - j) * TQ, (SSD_SUB - j) * TQ)
        yr_ref[rr, :] = _ssd_chunk_step(xr_ref[rr, :], dtr_ref[rr, :], dttr_ref[0, :, rr], par, str_ref, True)


def _ssd_chunk_step(xc, dt_raw, dtt_raw, par, st_ref, rev):
    bias, a_vec, biast, at_vec, dsk = par
    base = SSD_HEADS if rev else 0
    x = xc[:, 0:GROUP_W]
    dt = _softplus(dt_raw + bias)
    a = dt * a_vec
    dtt = _softplus(dtt_raw + biast)
    at = dtt * at_vec
    ri = lax.broadcasted_iota(jnp.int32, (TQ, TQ), 0)
    ci = lax.broadcasted_iota(jnp.int32, (TQ, TQ), 1)
    causal = (ci >= ri) if rev else (ri >= ci)
    tri = jnp.where(causal, 1.0, 0.0)
    cum_c = _dot_hi(tri, a)
    cum_r = _dot_nt_hi(at, tri)
    edge = 0 if rev else TQ - 1
    tot = cum_c[edge:edge + 1, :]

    shape = (TQ, GROUP_W)
    xdt = x * _per_head_cols(dt, base, SSD_HEADS, shape)
    lane = lax.broadcasted_iota(jnp.int32, shape, 1)
    y = jnp.zeros(shape, F32)
    bmat = [xc[:, _X_B + g * SSD_STATE:_X_B + (g + 1) * SSD_STATE].astype(BF16) for g in range(SSD_NGROUPS)]
    cmat = [xc[:, _X_C + g * SSD_STATE:_X_C + (g + 1) * SSD_STATE].astype(BF16) for g in range(SSD_NGROUPS)]
    cb = [_dot_nt(cmat[g], bmat[g]) for g in range(SSD_NGROUPS)]
    for h in range(SSD_HEADS):
        col = base + h
        seg = jnp.where(causal, cum_c[:, col:col + 1] - cum_r[col:col + 1, :], NEG_INF)
        scores = cb[h // 2] * jnp.exp(seg)
        xh = jnp.where((lane >= h * HEAD_DIM) & (lane < (h + 1) * HEAD_DIM), xdt, 0.0)
        y = y + _dot(scores.astype(BF16), xh.astype(BF16))
    st = st_ref[...]
    yo = jnp.concatenate(
        [_dot_nt(cmat[g], st[g * SSD_STATE:(g + 1) * SSD_STATE].astype(BF16)) for g in range(SSD_NGROUPS)], axis=1)
    y = y + yo * _per_head_cols(jnp.exp(cum_c), base, SSD_HEADS, shape)
    if not rev:
        y = y + x * dsk
    xd = xdt * _per_head_cols(jnp.exp(tot - cum_c), base, SSD_HEADS, shape)
    xdt_t = xd.T.astype(BF16)
    decay = jnp.exp(tot)
    for g in range(SSD_NGROUPS):
        new = _dot(xdt_t[g * SSD_STATE:(g + 1) * SSD_STATE], bmat[g])
        for hh in range(2):
            h = 2 * g + hh
            r0 = h * HEAD_DIM
            st_ref[r0:r0 + HEAD_DIM, :] = (decay[:, base + h:base + h + 1] * st[r0:r0 + HEAD_DIM]
                                           + new[hh * HEAD_DIM:(hh + 1) * HEAD_DIM])
    return y


def _dot_nt_hi(a, b):
    return lax.dot_general(a, b, (((1,), (1,)), ((), ())), preferred_element_type=F32, precision=HI)


def _ssd_chunk(c, rev, nc_ctx, nc_tot):
    if not rev:
        return c
    return jnp.where(c < nc_ctx, nc_ctx - 1 - c, nc_tot - 1 - (c - nc_ctx))


SSD_SUB = TM // TQ


def _ssd_scan(xc, dt, dtt, bias, a, biast, at, dsk, nb, s_len, lc):
    t = xc.shape[0]
    nblk = s_len // TM
    nctx = lc // TM
    fix = lambda b, c: (0, 0)

    def rows(rev):
        return lambda b, c: (b * nblk + _ssd_chunk(c, rev, nctx, nblk), 0)

    def lanes(rev):
        return lambda b, c: (b, 0, _ssd_chunk(c, rev, nctx, nblk))

    def data_specs(rev):
        return [pl.BlockSpec((TM, SSD_XBC), rows(rev)), pl.BlockSpec((TM, LANES), rows(rev)),
                pl.BlockSpec((1, SUBLANES, TM), lanes(rev))]

    state = pltpu.VMEM((SSD_HEADS * HEAD_DIM, SSD_STATE), F32)
    return pl.pallas_call(
        _ssd_kernel,
        out_shape=[jax.ShapeDtypeStruct((t, GROUP_W), F32)] * 2,
        grid=(nb, nblk),
        in_specs=data_specs(False) + data_specs(True) + [
            pl.BlockSpec((1, LANES), fix), pl.BlockSpec((1, LANES), fix),
            pl.BlockSpec((SUBLANES, TQ), fix), pl.BlockSpec((SUBLANES, TQ), fix),
            pl.BlockSpec((1, GROUP_W), fix)],
        out_specs=[pl.BlockSpec((TM, GROUP_W), rows(False)), pl.BlockSpec((TM, GROUP_W), rows(True))],
        scratch_shapes=[state, state],
        compiler_params=_cp(("parallel", "arbitrary")),
        name="ssd_scan",
    )(xc, dt, dtt, xc, dt, dtt, bias, a, biast, at, dsk)


def _ssd(xbc, dt, conv_w, conv_b, dt_bias, a_log, d_skip, nb, s_len, lc):
    xc = _conv(xbc, conv_w, conv_b, s_len, lc)
    nd = 2 * SSD_HEADS
    dtt = dt[:, :nd].reshape(nb, s_len, nd).transpose(0, 2, 1)
    bias = jnp.pad(dt_bias.reshape(1, nd), ((0, 0), (0, LANES - nd)))
    a = jnp.pad(-jnp.exp(a_log).reshape(1, nd), ((0, 0), (0, LANES - nd)))
    biast = jnp.broadcast_to(dt_bias.reshape(nd, 1), (nd, TQ))
    at = jnp.broadcast_to(-jnp.exp(a_log).reshape(nd, 1), (nd, TQ))
    dsk = jnp.repeat(d_skip, HEAD_DIM).reshape(1, GROUP_W)
    return _ssd_scan(xc, dt, dtt, bias, a, biast, at, dsk, nb, s_len, lc)


def _outproj_kernel(x_ref, ys5_ref, oga_ref, y0_ref, y1_ref, z_ref, owa_ref, *refs):
    mods, shared = refs[:ROW_SUB], refs[ROW_SUB:]
    for s in range(ROW_SUB):
        rows = _row_views((x_ref, oga_ref, y0_ref, y1_ref, z_ref, owa_ref) + tuple(shared[-4:-1]), s)
        _outproj_block(rows[0], ys5_ref.at[:, pl.ds(s * S5_TB, S5_TB), :], *rows[1:6], mods[s], *shared[:-4],
                       *rows[6:], shared[-1])


def _outproj_block(x_ref, ys5_ref, oga_ref, y0_ref, y1_ref, z_ref, owa_ref, mod_ref, gluw_ref, glub_ref,
                   ng_ref, wout_ref, n2_ref, wr_ref, br_ref, xn_o, h2_o, route_o, y_scr):
    _s5_unpack_kernel(ys5_ref, y_scr)
    gl = _gelu_tanh(jnp.concatenate([y_scr[0], y_scr[1]], axis=1))
    a = gl * _sigmoid(_dot(gl.astype(BF16), gluw_ref[...]) + glub_ref[...])
    m = (y0_ref[...] + y1_ref[...]) * _silu(z_ref[...])
    m = m * lax.rsqrt(jnp.mean(m * m, axis=-1, keepdims=True) + EPS) * ng_ref[...]
    w = wout_ref
    mix = (_dot(a.astype(BF16), w[0:GROUP_W, :]) + _dot(oga_ref[...].astype(BF16), w[GROUP_W:2 * GROUP_W, :])
           + _dot(m.astype(BF16), w[2 * GROUP_W:3 * GROUP_W, :]) + _dot(owa_ref[...].astype(BF16), w[3 * GROUP_W:, :]))
    xn = x_ref[...] + mod_ref[0, 2:3, :] * mix
    xn_o[...] = xn
    h2 = xn * lax.rsqrt(jnp.mean(xn * xn, axis=-1, keepdims=True) + EPS) * n2_ref[...]
    h2 = h2 * (1.0 + mod_ref[0, 4:5, :]) + mod_ref[0, 3:4, :]
    h2_o[...] = _pack_bf16_pair(h2)
    h_hi = h2.astype(BF16)
    h_lo = (h2 - h_hi.astype(F32)).astype(BF16)
    logits = _dot(h_hi, wr_ref[0]) + (_dot(h_lo, wr_ref[0]) + _dot(h_hi, wr_ref[1])) + br_ref[...]
    lane = lax.broadcasted_iota(jnp.int32, logits.shape, 1).astype(F32)
    big = float(4 * LANES)
    lcoarse = jnp.where(lane < MOE_GROUPS, logits, NEG_INF)
    mx = jnp.max(lcoarse, axis=1, keepdims=True)
    den = jnp.sum(jnp.exp(lcoarse - mx), axis=1, keepdims=True)
    grp = jnp.min(jnp.where(lcoarse == mx, lane, big), axis=1, keepdims=True)
    pg = 1.0 / den
    lo = ROUTE_FINE0 + grp * MOE_PER_GROUP
    lf = jnp.where(lane >= lo, jnp.where(lane < lo + MOE_PER_GROUP, logits, NEG_INF), NEG_INF)
    v1 = jnp.max(lf, axis=1, keepdims=True)
    i1 = jnp.min(jnp.where(lf == v1, lane, big), axis=1, keepdims=True)
    lf2 = jnp.where(lane == i1, NEG_INF, lf)
    v2 = jnp.max(lf2, axis=1, keepdims=True)
    i2 = jnp.min(jnp.where(lf2 == v2, lane, big), axis=1, keepdims=True)
    e2 = jnp.exp(v2 - v1)
    w1 = pg / (1.0 + e2)
    w2 = w1 * e2
    route = jnp.where(lane == 0, i1 - ROUTE_FINE0,
                      jnp.where(lane == 1, i2 - ROUTE_FINE0,
                                jnp.where(lane == 2, w1, jnp.where(lane == 3, w2, 0.0))))
    route_o[...] = route


def _outproj(x, ys5, oga, y0, y1, z, owa, mod, glu_w, glu_b, ssd_norm_g, w_out, norm2_g, wr, br, nb, nblk):
    t, d = x.shape
    row = lambda i: (i, 0)
    fix = lambda i: (0, 0)
    step = ROW_SUB * TM
    gw = pl.BlockSpec((step, GROUP_W), row)
    wr_hi = wr.astype(BF16)
    mod_specs = [pl.BlockSpec((1, 6, d), lambda i, s=s: (_mod_row(ROW_SUB * i + s, nblk, nb), 0, 0))
                 for s in range(ROW_SUB)]
    return pl.pallas_call(
        _outproj_kernel,
        out_shape=[jax.ShapeDtypeStruct((t, d), F32), jax.ShapeDtypeStruct((t, d // 2), jnp.uint32),
                   jax.ShapeDtypeStruct((t, LANES), F32)],
        grid=(t // step,),
        in_specs=[pl.BlockSpec((step, d), row),
                  pl.BlockSpec((S5_GROUPS, ROW_SUB * S5_TB, S5_BLK), lambda i: (0, i, 0)),
                  gw, gw, gw, gw, gw] + mod_specs + [
                  pl.BlockSpec((GROUP_W, GROUP_W), fix),
                  pl.BlockSpec((1, GROUP_W), fix),
                  pl.BlockSpec((1, GROUP_W), fix),
                  pl.BlockSpec((d, d), fix),
                  pl.BlockSpec((1, d), fix),
                  pl.BlockSpec((2, d, LANES), lambda i: (0, 0, 0)),
                  pl.BlockSpec((1, LANES), fix)],
        out_specs=[pl.BlockSpec((step, d), row), pl.BlockSpec((step, d // 2), row), pl.BlockSpec((step, LANES), row)],
        scratch_shapes=[pltpu.VMEM((GROUP_W // LANES, TM, LANES), F32)],
        compiler_params=_cp(("parallel",)),
        name="out_proj_router",
    )(x, ys5, oga, y0, y1, z, owa, *([mod] * ROW_SUB), glu_w.astype(BF16), glu_b.reshape(1, -1), ssd_norm_g.reshape(1, -1),
      w_out.astype(BF16), norm2_g.reshape(1, -1), jnp.stack([wr_hi, (wr - wr_hi.astype(F32)).astype(BF16)]), br)


def _pack_router(coarse_w, coarse_b, fine_w, fine_b):
    def lanes(coarse, fine):
        gap = jnp.zeros(coarse.shape[:-1] + (ROUTE_FINE0 - MOE_GROUPS,), F32)
        tail = jnp.zeros(coarse.shape[:-1] + (LANES - ROUTE_FINE0 - N_EXPERTS,), F32)
        return jnp.concatenate([coarse, gap, fine, tail], axis=-1)

    return lanes(coarse_w, fine_w), lanes(coarse_b[None, :], fine_b[None, :])


def _gather_rows(src, idx):
    m = idx.shape[0]
    d = src.shape[1]
    workers = SC_CORES * SC_SUBCORES
    k = SC_FETCH_K
    nch = m // (workers * k)
    assert nch * workers * k == m
    mesh = plsc.VectorSubcoreMesh(core_axis_name="c", subcore_axis_name="s")

    @functools.partial(
        pl.kernel, mesh=mesh,
        out_type=jax.ShapeDtypeStruct((m, d), src.dtype),
        scratch_types=[pltpu.VMEM((nch, k), jnp.int32),
                       pltpu.VMEM((k, d), src.dtype),
                       pltpu.SemaphoreType.DMA],
    )
    def gather(src_hbm, idx_hbm, out_hbm, idx_v, rows_v, sem):
        wid = lax.axis_index("s") * SC_CORES + lax.axis_index("c")
        pltpu.sync_copy(idx_hbm.at[wid], idx_v)

        @pl.loop(0, nch)
        def _(j):
            off = pl.multiple_of((wid * nch + j) * k, k)
            pltpu.async_copy(src_hbm.at[idx_v.at[j]], rows_v, sem).wait()
            pltpu.sync_copy(rows_v, out_hbm.at[pl.ds(off, k)])

    return gather(src, idx.reshape(workers, nch, k))


def _scatter_rows(src, dst0, dst1, nrows):
    t, d = src.shape
    workers = SC_CORES * SC_SUBCORES
    nch = t // (workers * SC_GATHER_K)
    assert nch * workers * SC_GATHER_K == t
    mesh = plsc.VectorSubcoreMesh(core_axis_name="c", subcore_axis_name="s")

    @functools.partial(
        pl.kernel, mesh=mesh,
        out_type=jax.ShapeDtypeStruct((nrows, d), src.dtype),
        scratch_types=[pltpu.VMEM((nch, SC_GATHER_K), jnp.int32),
                       pltpu.VMEM((nch, SC_GATHER_K), jnp.int32),
                       pltpu.VMEM((SC_GATHER_K, d), src.dtype)],
    )
    def scatter(src_hbm, d0_hbm, d1_hbm, out_hbm, i0_v, i1_v, rows_v):
        wid = lax.axis_index("s") * SC_CORES + lax.axis_index("c")
        pltpu.sync_copy(d0_hbm.at[wid], i0_v)
        pltpu.sync_copy(d1_hbm.at[wid], i1_v)

        @pl.loop(0, nch)
        def _(j):
            off = pl.multiple_of((wid * nch + j) * SC_GATHER_K, SC_GATHER_K)
            pltpu.sync_copy(src_hbm.at[pl.ds(off, SC_GATHER_K)], rows_v)
            pltpu.sync_copy(rows_v, out_hbm.at[i0_v.at[j]])
            pltpu.sync_copy(rows_v, out_hbm.at[i1_v.at[j]])

    return scatter(src, dst0.reshape(workers, nch, SC_GATHER_K), dst1.reshape(workers, nch, SC_GATHER_K))


def _expert_kernel(be_ref, nused_ref, nvalid_ref, x_ref, wg_ref, wu_ref, wd_ref, o_ref, wg_s, wu_s, wd_s):
    i = pl.program_id(0)
    new_expert = jnp.logical_or(i == 0, be_ref[i] != be_ref[jnp.maximum(i - 1, 0)])

    @pl.when(jnp.logical_and(i < nused_ref[0], new_expert))
    def _():
        wg_s[...] = wg_ref[0, 0].astype(BF16)
        wu_s[...] = wu_ref[0, 0].astype(BF16)
        wd_s[...] = wd_ref[0, 0].astype(BF16)

    def swiglu(rows):
        row = rows.start + lax.broadcasted_iota(jnp.int32, (rows.stop - rows.start, x_ref.shape[1]), 0)
        lo, hi = _unpack_bf16_pair(jnp.where(row < nvalid_ref[i], x_ref[rows, :], jnp.uint32(0)))
        lo = lo.astype(BF16)
        hi = hi.astype(BF16)
        half = lo.shape[1]
        gate = _dot(lo, wg_s[0:half, :]) + _dot(hi, wg_s[half:, :])
        up = _dot(lo, wu_s[0:half, :]) + _dot(hi, wu_s[half:, :])
        o_ref[rows, :] = _pack_bf16_pair(_dot((_silu(gate) * up).astype(BF16), wd_s[...]))

    used = i < nused_ref[0]
    half_rows = MOE_TM // 2

    @pl.when(jnp.logical_and(used, nvalid_ref[i] > half_rows))
    def _():
        swiglu(slice(0, MOE_TM))

    @pl.when(jnp.logical_and(used, nvalid_ref[i] <= half_rows))
    def _():
        swiglu(slice(0, half_rows))
        o_ref[half_rows:, :] = jnp.zeros((MOE_TM - half_rows, o_ref.shape[1]), o_ref.dtype)

    @pl.when(jnp.logical_not(used))
    def _():
        o_ref[...] = jnp.zeros_like(o_ref)


def _experts(xs, blk_e, n_used, n_valid, wg, wu, wd, layer):
    rows, dp = xs.shape
    d = 2 * dp
    nblocks = rows // MOE_TM
    de = wg.shape[3]
    wsel = lambda i, be, nu, nv: (layer, be[i], 0, 0)
    grid_spec = pltpu.PrefetchScalarGridSpec(
        num_scalar_prefetch=3,
        grid=(nblocks,),
        in_specs=[pl.BlockSpec((MOE_TM, dp), lambda i, be, nu, nv: (i, 0)),
                  pl.BlockSpec((1, 1, d, de), wsel),
                  pl.BlockSpec((1, 1, d, de), wsel),
                  pl.BlockSpec((1, 1, de, d), wsel)],
        out_specs=pl.BlockSpec((MOE_TM, dp), lambda i, be, nu, nv: (i, 0)),
        scratch_shapes=[pltpu.VMEM((d, de), BF16), pltpu.VMEM((d, de), BF16), pltpu.VMEM((de, d), BF16)],
    )
    return pl.pallas_call(
        _expert_kernel,
        out_shape=jax.ShapeDtypeStruct((rows, dp), jnp.uint32),
        grid_spec=grid_spec,
        compiler_params=_cp(("arbitrary",)),
        name="moe_experts",
    )(blk_e, n_used, n_valid, xs, wg, wu, wd)


def _final_kernel(x_ref, r0_ref, r1_ref, route_ref, mod_ref, fg_ref, o_ref):
    y = _moe_residual(x_ref, r0_ref, r1_ref, route_ref, mod_ref)
    o_ref[...] = y * lax.rsqrt(jnp.mean(y * y, axis=-1, keepdims=True) + EPS) * fg_ref[...]


def _final(xn, rows2, route, mod, final_g, nb, nblk):
    t, d = xn.shape
    nlat = nblk - 1
    grid = (nb * nlat,)
    src = lambda i: ((i // nlat) * nblk + 1 + i % nlat, 0)
    modi = lambda i: (i // nlat, 0, 0)
    out_rows = nb * nlat * TM
    return pl.pallas_call(
        _final_kernel,
        out_shape=jax.ShapeDtypeStruct((out_rows, d), F32),
        grid=grid,
        in_specs=[pl.BlockSpec((TM, d), src),
                  pl.BlockSpec((TM, d // 2), src),
                  pl.BlockSpec((TM, d // 2), lambda i: (src(i)[0] + t // TM, 0)),
                  pl.BlockSpec((TM, LANES), src),
                  pl.BlockSpec((1, 6, d), modi),
                  pl.BlockSpec((1, d), lambda i: (0, 0))],
        out_specs=pl.BlockSpec((TM, d), lambda i: (i, 0)),
        compiler_params=_cp(("parallel",)),
        name="moe_combine_final",
    )(xn, rows2, rows2, route, mod, final_g.reshape(1, d))


def _moe(h2, route, wg, wu, wd, layer):
    t, d = h2.shape
    n_slots = 2 * t
    experts = jnp.arange(N_EXPERTS, dtype=F32)[None, :]
    oh0 = (route[:, 0:1] == experts).astype(F32)
    oh1 = (route[:, 1:2] == experts).astype(F32)
    both = (oh0 + oh1).reshape(t // LANES, LANES, N_EXPERTS)
    tri = jnp.tril(jnp.ones((LANES, LANES), F32))
    intra = jnp.einsum("ij,bjk->bik", tri, both)
    blk_tot = intra[:, -1, :]
    blk_cum = jnp.cumsum(blk_tot, axis=0)
    earlier = (intra - both + (blk_cum - blk_tot)[:, None, :]).reshape(t, N_EXPERTS)
    counts = blk_cum[-1].astype(jnp.int32)
    pcounts = (counts + MOE_TM - 1) // MOE_TM * MOE_TM
    pends = jnp.cumsum(pcounts)
    pstarts = pends - pcounts
    base = pstarts.astype(F32)[None, :] + earlier
    dest0 = jnp.sum(oh0 * base, axis=1).astype(jnp.int32)
    dest1 = jnp.sum(oh1 * base, axis=1).astype(jnp.int32)
    nblocks = -(-n_slots // MOE_TM) + N_EXPERTS
    nrows = -(-nblocks * MOE_TM // GATHER_ROWS) * GATHER_ROWS
    nblocks = nrows // MOE_TM
    blk_start = jnp.arange(nblocks, dtype=jnp.int32) * MOE_TM
    blk_e = jnp.minimum(jnp.sum((pends[None, :] <= blk_start[:, None]).astype(jnp.int32), axis=1), N_EXPERTS - 1)
    n_used = (pends[-1] // MOE_TM).astype(jnp.int32).reshape(1)
    n_valid = jnp.clip((pstarts + counts)[blk_e] - blk_start, 0, MOE_TM).astype(jnp.int32)
    xs = _scatter_rows(h2, dest0, dest1, nrows)
    ys = _experts(xs, blk_e, n_used, n_valid, wg, wu, wd, layer)
    return _gather_rows(ys, jnp.concatenate([dest0, dest1]))


def kernel(x, c, ctx, c_ctx, ada_w, ada_b, norm1_g, norm2_g, w_in, w_out, s5_lam_re, s5_lam_im, s5_log_dt, s5_b_re, s5_b_im, s5_c_re, s5_c_im, s5_d, s5_glu_w, s5_glu_b, ga_qn_g, ga_kn_g, ssd_conv_w, ssd_conv_b, ssd_dt_bias, ssd_a_log, ssd_d, ssd_norm_g, wa_sink, moe_coarse_w, moe_coarse_b, moe_fine_w, moe_fine_b, moe_w_gate, moe_w_up, moe_w_down, final_g):
    nb, l, d = x.shape
    lc = ctx.shape[1]
    depth = ada_w.shape[0]
    assert lc == TM and l % TM == 0 and nb <= SUBLANES - 1 and d == D_MODEL
    s_len = lc + l
    nblk = s_len // TM
    t = nb * s_len

    cc = jnp.zeros((SUBLANES, d), F32).at[:nb].set(c).at[nb].set(c_ctx)
    mods = _ada(cc, ada_w, ada_b).reshape(depth, SUBLANES, 6, d)
    cos_t, sin_t = _rope_tables(lc, l)
    w_packed = jax.vmap(_pack_w_in)(w_in)
    s5_tabs = jax.vmap(_s5_params)(s5_lam_re, s5_lam_im, s5_log_dt, s5_b_re, s5_b_im, s5_c_re, s5_c_im, s5_d)
    wrs, brs = jax.vmap(_pack_router)(moe_coarse_w, moe_coarse_b, moe_fine_w, moe_fine_b)

    src = ("first", x.reshape(nb * l, d), ctx.reshape(nb * lc, d))
    for i in range(depth):
        mod = mods[i]
        (xm, xbc, ug, z, dt, gaq, gak, gav, waq, wak, wav) = _inproj(
            src, mod, norm1_g[i], w_packed[i], cos_t, sin_t, ga_qn_g[i], ga_kn_g[i], nb, nblk)
        ys5 = _s5(ug, tuple(tab[i] for tab in s5_tabs), nb, s_len, lc)
        oga, owa = _attn(wa_sink[i], gaq, gak, gav, waq, wak, wav, nb, s_len, lc)
        y0, y1 = _ssd(xbc, dt, ssd_conv_w[i], ssd_conv_b[i], ssd_dt_bias[i], ssd_a_log[i], ssd_d[i], nb, s_len, lc)
        wr, br = wrs[i], brs[i]
        xn, h2, route = _outproj(xm, ys5, oga, y0, y1, z, owa, mod, s5_glu_w[i], s5_glu_b[i], ssd_norm_g[i],
                                 w_out[i], norm2_g[i], wr, br, nb, nblk)
        rows2 = _moe(h2, route, moe_w_gate, moe_w_up, moe_w_down, i)
        src = ("moe", xn, rows2, route, mod)
    return _final(xn, rows2, route, mod, final_g, nb, nblk).reshape(nb, l, d)
```

```python
import functools
import math

import jax
import jax.numpy as jnp
import numpy as np
from jax import lax
from jax.experimental import pallas as pl
from jax.experimental.pallas import tpu as pltpu
from jax.experimental.pallas import tpu_sc as plsc

F32 = jnp.float32
BF16 = jnp.bfloat16
HI = lax.Precision.HIGHEST

D_MODEL = 1024
GRID_W = 64
GROUP_W = 256
HEAD_DIM = 64
ROPE_FREQS = HEAD_DIM // 4
ROPE_BASE = 10000.0
EPS = 1e-6
S5_CH = 16
S5_GROUPS = GROUP_W // S5_CH
S5_STATE = 64
N_HEADS = 4
SSD_HEADS = 4
SSD_NGROUPS = 2
SSD_STATE = 128
SSD_XBC = GROUP_W + 2 * SSD_NGROUPS * SSD_STATE
WINDOW = 128
MOE_GROUPS = 4
MOE_PER_GROUP = 8
N_EXPERTS = 32
D_EXPERT = D_MODEL // 2

LANES = 128
SUBLANES = 8
TM = 256
TQ = 128
GA_TQ = 128
GA_SUB = 2
S5_Q = 32
S5_BLK = S5_Q * S5_CH
MOE_TM = 512
SC_CORES = 2
SC_SUBCORES = 16
SC_GATHER_K = 32
SC_FETCH_K = 64
GATHER_ROWS = SC_CORES * SC_SUBCORES * SC_GATHER_K
ROUTE_FINE0 = 32
VMEM_LIMIT = 56 * 1024 * 1024

NEG_INF = float("-inf")
LOG2E = math.log2(math.e)


def _cp(sem, vmem=VMEM_LIMIT):
    return pltpu.CompilerParams(dimension_semantics=sem, vmem_limit_bytes=vmem)


def _dot(a, b):
    return jnp.dot(a, b, preferred_element_type=F32)


def _dot_hi(a, b):
    return jnp.dot(a, b, preferred_element_type=F32, precision=HI)


def _dot_nt(a, b):
    return lax.dot_general(a, b, (((1,), (1,)), ((), ())), preferred_element_type=F32)


def _sigmoid(x):
    return 1.0 / (1.0 + jnp.exp(-x))


def _silu(x):
    return x * _sigmoid(x)


def _gelu_tanh(x):
    return 0.5 * x * (1.0 + jnp.tanh(math.sqrt(2.0 / math.pi) * (x + 0.044715 * (x * x * x))))


def _softplus(x):
    return jnp.maximum(x, 0.0) + jnp.log(1.0 + jnp.exp(-jnp.abs(x)))


_HI16 = 0xFFFF0000


def _pack_bf16_pair(x):
    n = x.shape[1] // 2
    bits = pltpu.bitcast(x.astype(BF16).astype(F32), jnp.uint32)
    return (bits[:, n:] & jnp.uint32(_HI16)) | (bits[:, :n] >> 16)


def _unpack_bf16_pair(w):
    return pltpu.bitcast(w << 16, F32), pltpu.bitcast(w & jnp.uint32(_HI16), F32)


def _per_head_cols(v, base, n_heads, shape):
    lane = lax.broadcasted_iota(jnp.int32, shape, 1)
    out = jnp.broadcast_to(v[:, base + n_heads - 1:base + n_heads], shape)
    for h in range(n_heads - 2, -1, -1):
        out = jnp.where(lane < (h + 1) * HEAD_DIM, v[:, base + h:base + h + 1], out)
    return out


def _ada_kernel(c_ref, w_ref, b_ref, o_ref):
    c = c_ref[...]
    o_ref[0] = _dot_hi(_silu(c), w_ref[0]) + b_ref[0]


def _ada(cc, ada_w, ada_b):
    depth, d, n = ada_w.shape
    tn = 1536
    return pl.pallas_call(
        _ada_kernel,
        out_shape=jax.ShapeDtypeStruct((depth, SUBLANES, n), F32),
        grid=(depth, n // tn),
        in_specs=[pl.BlockSpec((SUBLANES, d), lambda l, j: (0, 0)),
                  pl.BlockSpec((1, d, tn), lambda l, j: (l, 0, j)),
                  pl.BlockSpec((1, 1, tn), lambda l, j: (l, 0, j))],
        out_specs=pl.BlockSpec((1, SUBLANES, tn), lambda l, j: (l, 0, j)),
        compiler_params=_cp(("parallel", "parallel")),
        name="ada_mod",
    )(cc, ada_w, ada_b.reshape(depth, 1, n))


_C_XBC = 0
_C_U = _C_XBC + SSD_XBC
_C_Z = _C_U + GROUP_W
_C_DT = _C_Z + GROUP_W
_C_GAQ = _C_DT + LANES
_C_WAQ = _C_GAQ + N_HEADS * LANES
_C_GAK = _C_WAQ + N_HEADS * LANES
_C_GAV = _C_GAK + LANES
_C_WAK = _C_GAV + LANES
_C_WAV = _C_WAK + LANES
_C_END = _C_WAV + LANES


def _expand_q_cols(wq):
    zero = jnp.zeros((wq.shape[0], HEAD_DIM), wq.dtype)
    parts = []
    for h in range(N_HEADS):
        head = wq[:, h * HEAD_DIM:(h + 1) * HEAD_DIM]
        parts += [head, zero] if h // 2 == 0 else [zero, head]
    return jnp.concatenate(parts, axis=1)


def _pack_w_in(w_in):
    cuts = np.cumsum([256, 256, 128, 128, 256, SSD_XBC, 2 * SSD_HEADS, 256, 128, 128])[:-1]
    u, gaq, gak, gav, z, xbc, dt, waq, wak, wav = jnp.split(w_in, [int(c) for c in cuts], axis=1)
    dt = jnp.pad(dt, ((0, 0), (0, LANES - dt.shape[1])))
    w = jnp.concatenate([xbc, u, z, dt, _expand_q_cols(gaq), _expand_q_cols(waq), gak, gav, wak, wav], axis=1)
    return w.astype(BF16)


def _rope(x, cos, sins):
    w = x.shape[1]
    if w > LANES:
        cos = jnp.concatenate([cos] * (w // LANES), axis=1)
        sins = jnp.concatenate([sins] * (w // LANES), axis=1)
    lane = lax.broadcasted_iota(jnp.int32, x.shape, 1)
    up = pltpu.roll(x, w - ROPE_FREQS, 1)
    dn = pltpu.roll(x, ROPE_FREQS, 1)
    partner = jnp.where((lane & ROPE_FREQS) == 0, up, dn)
    return x * cos + partner * sins


def _moe_residual(xn_ref, r0_ref, r1_ref, route_ref, mod_ref):
    route = route_ref[...]
    r0 = jnp.concatenate(_unpack_bf16_pair(r0_ref[...]), axis=1)
    r1 = jnp.concatenate(_unpack_bf16_pair(r1_ref[...]), axis=1)
    return xn_ref[...] + mod_ref[0, 5:6, :] * (route[:, 2:3] * r0 + route[:, 3:4] * r1)


ROW_SUB = 2


def _row_views(refs, s):
    return [r.at[pl.ds(s * TM, TM), :] for r in refs]


def _inproj_kernel(*refs, first, nblk):
    n_blk_in = (2 if first else 5) + 3
    shared = refs[ROW_SUB * n_blk_in:]
    g_ref, w_ref, qn_ref, kn_ref = shared[:4]
    xm_o, xbc_o, ug_o = shared[4:7]
    rest_o = shared[7:-1]
    u_scr = shared[-1]
    for s in range(ROW_SUB):
        blk_refs = refs[s * n_blk_in:(s + 1) * n_blk_in]
        xm_v, xbc_v = _row_views((xm_o, xbc_o), s)
        ug_v = ug_o.at[:, pl.ds(s * S5_TB, S5_TB), :]
        _inproj_block(blk_refs, g_ref, w_ref, qn_ref, kn_ref, xm_v, xbc_v, ug_v, _row_views(rest_o, s), u_scr,
                      first, (pl.program_id(0) * ROW_SUB + s) % nblk == 0)


def _inproj_block(blk_refs, g_ref, w_ref, qn_ref, kn_ref, xm_o, xbc_o, ug_o, rest_o, u_scr, first, is_ctx):
    if first:
        lat_ref, ctx_ref = blk_refs[:2]
        x = jnp.where(is_ctx, ctx_ref[...], lat_ref[...])
    else:
        x = _moe_residual(*blk_refs[:5])
    mod_ref, cos_ref, sin_ref = blk_refs[-3:]
    z_o, dt_o, gaq_o, gak_o, gav_o, waq_o, wak_o, wav_o = rest_o
    xm_o[...] = x
    ms = jnp.mean(x * x, axis=-1, keepdims=True)
    xn = x * lax.rsqrt(ms + EPS) * g_ref[...]
    h = xn * (1.0 + mod_ref[0, 1:2, :]) + mod_ref[0, 0:1, :]
    hb = h.astype(BF16)

    def proj(lo, hi):
        return _dot(hb, w_ref[:, lo:hi])

    cos = cos_ref[...]
    sins = sin_ref[...]
    scale = LOG2E * HEAD_DIM ** -0.5
    q = proj(_C_GAQ, _C_WAQ)
    qs = q * q
    inv = jnp.concatenate(
        [jnp.broadcast_to(lax.rsqrt(jnp.sum(qs[:, s * LANES:(s + 1) * LANES], axis=1, keepdims=True)
                                    * (1.0 / HEAD_DIM) + EPS), (q.shape[0], LANES)) for s in range(N_HEADS)], axis=1)
    gaq_o[...] = (_rope(q * inv * qn_ref[...], cos, sins) * scale).astype(BF16)
    waq_o[...] = (_rope(proj(_C_WAQ, _C_GAK), cos, sins) * scale).astype(BF16)
    k = proj(_C_GAK, _C_GAV)
    ks = k * k
    lane = lax.broadcasted_iota(jnp.int32, k.shape, 1)
    lo = lane < HEAD_DIM
    ms0 = jnp.sum(jnp.where(lo, ks, 0.0), axis=1, keepdims=True)
    ms1 = jnp.sum(jnp.where(lo, 0.0, ks), axis=1, keepdims=True)
    kinv = lax.rsqrt(jnp.where(lo, ms0, ms1) * (1.0 / HEAD_DIM) + EPS)
    gak_o[...] = _rope(k * kinv * kn_ref[...], cos, sins).astype(BF16)
    gav_o[...] = proj(_C_GAV, _C_WAK).astype(BF16)
    wak_o[...] = _rope(proj(_C_WAK, _C_WAV), cos, sins).astype(BF16)
    wav_o[...] = proj(_C_WAV, _C_END).astype(BF16)
    xbc_o[...] = proj(_C_XBC, _C_U)
    u = proj(_C_U, _C_Z)
    u_scr[0] = u[:, :LANES]
    u_scr[1] = u[:, LANES:]
    _s5_pack_kernel(u_scr.at[0], u_scr.at[1], ug_o)
    z_o[...] = proj(_C_Z, _C_DT)
    dt_o[...] = proj(_C_DT, _C_GAQ)


def _mod_row(i, nblk, nb):
    return jnp.where(i % nblk == 0, nb, i // nblk)


def _inproj(src, mod, norm_g, w_packed, cos_t, sin_t, qn_g, kn_g, nb, nblk):
    first = src[0] == "first"
    d = src[1].shape[1]
    t = nb * nblk * TM
    row = lambda i: (i, 0)
    fix = lambda i: (0, 0)
    nsteps = t // (ROW_SUB * TM)
    assert nsteps * ROW_SUB * TM == t

    def blk_specs(s):
        bid = lambda i: ROW_SUB * i + s
        modspec = pl.BlockSpec((1, 6, d), lambda i: (_mod_row(bid(i), nblk, nb), 0, 0))
        table = pl.BlockSpec((TM, LANES), lambda i: (bid(i) % nblk, 0))
        if first:
            srcs = [pl.BlockSpec((TM, d), lambda i: ((bid(i) // nblk) * (nblk - 1) + jnp.maximum(bid(i) % nblk - 1, 0), 0)),
                    pl.BlockSpec((TM, d), lambda i: (bid(i) // nblk, 0))]
        else:
            srcs = [pl.BlockSpec((TM, d), lambda i: (bid(i), 0)), pl.BlockSpec((TM, d // 2), lambda i: (bid(i), 0)),
                    pl.BlockSpec((TM, d // 2), lambda i: (bid(i) + t // TM, 0)),
                    pl.BlockSpec((TM, LANES), lambda i: (bid(i), 0)), modspec]
        return srcs + [modspec, table, table]

    if first:
        blk_args = tuple(src[1:]) + (mod, cos_t, sin_t)
    else:
        blk_args = (src[1], src[2], src[2], src[3], src[4], mod, cos_t, sin_t)
    outs = [(d, F32), (SSD_XBC, F32), None, (GROUP_W, F32), (LANES, F32),
            (N_HEADS * LANES, BF16), (LANES, BF16), (LANES, BF16),
            (N_HEADS * LANES, BF16), (LANES, BF16), (LANES, BF16)]
    shapes = [jax.ShapeDtypeStruct((t, o[0]), o[1]) if o else
              jax.ShapeDtypeStruct((S5_GROUPS, t // S5_Q, S5_BLK), F32) for o in outs]
    specs = [pl.BlockSpec((ROW_SUB * TM, o[0]), row) if o else
             pl.BlockSpec((S5_GROUPS, ROW_SUB * S5_TB, S5_BLK), lambda i: (0, i, 0)) for o in outs]
    return pl.pallas_call(
        functools.partial(_inproj_kernel, first=first, nblk=nblk),
        out_shape=shapes,
        grid=(nsteps,),
        in_specs=[sp for s in range(ROW_SUB) for sp in blk_specs(s)] + [
                  pl.BlockSpec((1, d), fix),
                  pl.BlockSpec((d, _C_END), fix),
                  pl.BlockSpec((1, N_HEADS * LANES), fix),
                  pl.BlockSpec((1, LANES), fix)],
        out_specs=specs,
        scratch_shapes=[pltpu.VMEM((GROUP_W // LANES, TM, LANES), F32)],
        compiler_params=_cp(("parallel",)),
        name="in_proj",
    )(*(blk_args * ROW_SUB), norm_g.reshape(1, d), w_packed,
      jnp.tile(qn_g, 2 * N_HEADS).reshape(1, -1), jnp.tile(kn_g, 2).reshape(1, -1))


def _rope_tables(lc, l):
    n_rows = l // GRID_W
    rows = np.repeat(np.arange(n_rows), GRID_W)
    cols = np.tile(np.arange(GRID_W), n_rows)
    inv = np.power(np.float32(ROPE_BASE), -np.arange(ROPE_FREQS, dtype=np.float32) / ROPE_FREQS)
    ang = np.stack([rows, cols], axis=-1).astype(np.float32)[..., None] * inv
    cos = np.cos(ang)
    sin = np.sin(ang)
    cos64 = np.stack([cos, cos], axis=2).reshape(l, HEAD_DIM)
    sin64 = np.stack([-sin, sin], axis=2).reshape(l, HEAD_DIM)
    cos64 = np.concatenate([np.ones((lc, HEAD_DIM), np.float32), cos64], axis=0)
    sin64 = np.concatenate([np.zeros((lc, HEAD_DIM), np.float32), sin64], axis=0)
    return (jnp.asarray(np.tile(cos64, (1, 2)), dtype=F32), jnp.asarray(np.tile(sin64, (1, 2)), dtype=F32))


def _merge_heads(o2, kvh):
    tq = o2.shape[0] // 2
    oa, ob = o2[:tq], o2[tq:]
    lane = lax.broadcasted_iota(jnp.int32, oa.shape, 1)
    if kvh == 0:
        return jnp.where(lane < HEAD_DIM, oa, pltpu.roll(ob, HEAD_DIM, 1))
    return jnp.where(lane < HEAD_DIM, pltpu.roll(oa, HEAD_DIM, 1), ob)


def _stack_q(q_ref, rows, kvh):
    return jnp.concatenate([q_ref[rows, (2 * kvh) * LANES:(2 * kvh + 1) * LANES],
                            q_ref[rows, (2 * kvh + 1) * LANES:(2 * kvh + 2) * LANES]], axis=0)


def _ga_attend(q_ref, k_ref, v_ref, o_ref, nkeys):
    k = k_ref[0:nkeys, :]
    v = v_ref[0:nkeys, :]
    lane = lax.broadcasted_iota(jnp.int32, v.shape, 1)
    vaug = [jnp.where((lane < HEAD_DIM) if kvh == 0 else (lane >= HEAD_DIM), v, jnp.ones_like(v))
            for kvh in range(2)]
    for sub in range(GA_SUB):
        rows = slice(sub * GA_TQ, (sub + 1) * GA_TQ)
        scores = [_dot_nt(_stack_q(q_ref, rows, kvh), k) for kvh in range(2)]
        outs = []
        for kvh in range(2):
            s = scores[kvh]
            p = jnp.exp2((s - jnp.max(s, axis=1, keepdims=True)).astype(BF16))
            o2 = _dot(p, vaug[kvh])
            outs.append(_merge_heads(o2 / pltpu.roll(o2, HEAD_DIM, 1), kvh))
        o_ref[rows, :] = jnp.concatenate(outs, axis=1)


def _attn_kernel(sink_ref, gq_ref, gk_ref, gv_ref, wq_ref, wk_ref, wv_ref, og_ref, ow_ref, *, lc):
    is_ctx = pl.program_id(1) < lc // TM

    @pl.when(is_ctx)
    def _():
        _ga_attend(gq_ref, gk_ref, gv_ref, og_ref, lc)
        _wa_attend(sink_ref, wq_ref, wk_ref, wv_ref, ow_ref, lc)

    @pl.when(jnp.logical_not(is_ctx))
    def _():
        _ga_attend(gq_ref, gk_ref, gv_ref, og_ref, gk_ref.shape[0])
        _wa_attend(sink_ref, wq_ref, wk_ref, wv_ref, ow_ref, lc)


def _attn(sink, gq, gk, gv, wq, wk, wv, nb, s_len, lc):
    t = gq.shape[0]
    nq = s_len // TM
    assert GA_SUB * GA_TQ == TM and WA_SUB * TQ == TM
    qspec = pl.BlockSpec((TM, N_HEADS * LANES), lambda b, j: (b * nq + j, 0))
    kvspec = pl.BlockSpec((s_len, LANES), lambda b, j: (b, 0))
    ospec = pl.BlockSpec((TM, GROUP_W), lambda b, j: (b * nq + j, 0))
    return pl.pallas_call(
        functools.partial(_attn_kernel, lc=lc),
        out_shape=[jax.ShapeDtypeStruct((t, GROUP_W), F32)] * 2,
        grid=(nb, nq),
        in_specs=[pl.BlockSpec(memory_space=pltpu.SMEM), qspec, kvspec, kvspec, qspec, kvspec, kvspec],
        out_specs=[ospec, ospec],
        compiler_params=_cp(("parallel", "arbitrary")),
        name="attention",
    )(sink, gq, gk, gv, wq, wk, wv)


WA_SUB = TM // TQ


def _wa_attend(sink_ref, q_ref, k_ref, v_ref, o_ref, lc):
    s_len = k_ref.shape[0]
    kc = k_ref[0:lc, :]
    vc = v_ref[0:lc, :]
    lane_c = lax.broadcasted_iota(jnp.int32, vc.shape, 1)
    lane_b = lax.broadcasted_iota(jnp.int32, (3 * TQ, LANES), 1)
    row = lax.broadcasted_iota(jnp.int32, (2 * TQ, 1), 0)
    for sub in range(WA_SUB):
        rows = slice(sub * TQ, (sub + 1) * TQ)
        n = pl.program_id(1) * WA_SUB + sub - lc // TQ
        start = pl.multiple_of(jnp.clip(lc + (n - 1) * TQ, lc, s_len - 3 * TQ), TQ)
        kb = k_ref[pl.ds(start, 3 * TQ), :]
        vb = v_ref[pl.ds(start, 3 * TQ), :]
        qpos = n * TQ + lax.broadcasted_iota(jnp.int32, (TQ, 3 * TQ), 0)
        kpos = (start - lc) + lax.broadcasted_iota(jnp.int32, (TQ, 3 * TQ), 1)
        reach = jnp.where(n >= 0, WINDOW, -1)
        valid = jnp.abs(qpos - kpos) <= reach
        valid = jnp.concatenate([valid, valid], axis=0)
        outs = []
        for kvh in range(2):
            q2 = jnp.concatenate([q_ref[rows, (2 * kvh) * LANES:(2 * kvh + 1) * LANES],
                                  q_ref[rows, (2 * kvh + 1) * LANES:(2 * kvh + 2) * LANES]], axis=0)
            sc = _dot_nt(q2, kc)
            sb = jnp.where(valid, _dot_nt(q2, kb), NEG_INF)
            sink = jnp.where(row < TQ, sink_ref[2 * kvh], sink_ref[2 * kvh + 1]) * LOG2E
            m = jnp.maximum(jnp.maximum(jnp.max(sc, axis=1, keepdims=True), jnp.max(sb, axis=1, keepdims=True)), sink)
            pc = jnp.exp2((sc - m).astype(BF16))
            pb = jnp.exp2((sb - m).astype(BF16))
            own_c = (lane_c < HEAD_DIM) if kvh == 0 else (lane_c >= HEAD_DIM)
            own_b = (lane_b < HEAD_DIM) if kvh == 0 else (lane_b >= HEAD_DIM)
            o2 = _dot(pc, jnp.where(own_c, vc, jnp.ones_like(vc))) + _dot(pb, jnp.where(own_b, vb, jnp.ones_like(vb)))
            denom = pltpu.roll(o2, HEAD_DIM, 1) + jnp.exp2(sink - m)
            outs.append(_merge_heads(o2 / denom, kvh))
        o_ref[rows, :] = jnp.concatenate(outs, axis=1)


def _s5_chunk_index(t, rev, nc_ctx, nc_tot):
    if not rev:
        return t
    return jnp.where(t < nc_ctx, nc_ctx - 1 - t, nc_tot - 1 - (t - nc_ctx))


def _s5_kernel(u_ref, k_ref, p_ref, g_ref, ar_ref, ai_ref, dsk_ref, y_ref, s_scr, h_scr, m_scr, *, nb, nc_ctx, nc_tot):
    for d in range(2):
        ext = k_ref[d, 0]
        for s in range(S5_Q):
            lo = ((S5_Q - s) if d == 0 else (S5_Q - 1 - s)) * S5_CH
            win = pltpu.roll(ext, (2 * S5_BLK - lo) % (2 * S5_BLK), 1)[:, :S5_BLK]
            m_scr[d, s * S5_CH:(s + 1) * S5_CH, :] = win.astype(BF16)
    uf = u_ref[0]
    u = uf.astype(BF16)
    for d in range(2):
        for k in range(2):
            s_scr[d, k] = _dot(u, p_ref[d, k, 0])
    ar = [jnp.broadcast_to(ar_ref[d, 0], (nb, LANES)) for d in range(2)]
    ai = [[jnp.broadcast_to(ai_ref[d, k, 0], (nb, LANES)) for k in range(2)] for d in range(2)]

    def body(t, carry):
        out = []
        for d in range(2):
            h, hs = carry[d]
            rows = pl.ds(_s5_chunk_index(t, d == 1, nc_ctx, nc_tot), nb, stride=nc_tot)
            h_scr[d, rows, :] = h
            out.append((ar[d] * h + ai[d][0] * hs + s_scr[d, 0, rows, :],
                        ar[d] * hs + ai[d][1] * h + s_scr[d, 1, rows, :]))
        return tuple(out)

    zero = jnp.zeros((nb, LANES), F32)
    lax.fori_loop(0, nc_tot, body, ((zero, zero), (zero, zero)), unroll=2)
    y = uf * dsk_ref[0]
    for d in range(2):
        y = y + _dot(u, m_scr[d]) + _dot(h_scr[d].astype(BF16), g_ref[d, 0])
    y_ref[0] = y


S5_TB = TM // S5_Q
S5_GPS = LANES // S5_CH


def _s5_pack_kernel(lo_ref, hi_ref, o_ref):
    for s in range(S5_Q):
        rows = pl.ds(s, S5_TB, stride=S5_Q)
        halves = (lo_ref[rows, :], hi_ref[rows, :])
        dst = S5_CH * (s % S5_GPS)
        for g in range(S5_GROUPS):
            slab = halves[g // S5_GPS]
            src = S5_CH * (g % S5_GPS)
            moved = slab if src == dst else pltpu.roll(slab, (dst - src) % LANES, 1)
            o_ref[g, :, s * S5_CH:(s + 1) * S5_CH] = moved[:, dst:dst + S5_CH]


def _s5_unpack_kernel(y_ref, o_ref):
    lane_grp = lax.broadcasted_iota(jnp.int32, (S5_TB, LANES), 1) // S5_CH
    for s in range(S5_Q):
        src = S5_CH * (s % S5_GPS)
        for half in range(S5_GROUPS // S5_GPS):
            acc = None
            for gl in range(S5_GPS):
                slab = y_ref[half * S5_GPS + gl, :, (s // S5_GPS) * LANES:(s // S5_GPS + 1) * LANES]
                dst = S5_CH * gl
                moved = slab if src == dst else pltpu.roll(slab, (dst - src) % LANES, 1)
                acc = moved if acc is None else jnp.where(lane_grp == gl, moved, acc)
            o_ref[half, pl.ds(s, S5_TB, stride=S5_Q), :] = acc


def _s5_params(lam_re, lam_im, log_dt, b_re, b_im, c_re, c_im, d_skip):
    q = S5_Q
    dt = jnp.exp(log_dt)[..., None]
    lr, li = lam_re, lam_im
    mag = jnp.exp(lr * dt)
    a_re = mag * jnp.cos(li * dt)
    a_im = mag * jnp.sin(li * dt)
    den = lr * lr + li * li
    f_re = ((a_re - 1.0) * lr + a_im * li) / den
    f_im = (a_im * lr - (a_re - 1.0) * li) / den
    bb_re = f_re[..., None] * b_re - f_im[..., None] * b_im
    bb_im = f_re[..., None] * b_im + f_im[..., None] * b_re
    kk = jnp.arange(q + 1, dtype=F32)[:, None, None, None]
    pmag = jnp.exp(kk * (lr * dt))
    pw_re = pmag * jnp.cos(kk * (li * dt))
    pw_im = pmag * jnp.sin(kk * (li * dt))
    lw_re = pw_re[:q].transpose(1, 2, 0, 3)[:, :, :, None, :]
    lw_im = pw_im[:q].transpose(1, 2, 0, 3)[:, :, :, None, :]
    ck_re = c_re[:, :, None] * lw_re - c_im[:, :, None] * lw_im
    ck_im = c_re[:, :, None] * lw_im + c_im[:, :, None] * lw_re
    ck = jnp.concatenate([ck_re, -ck_im], axis=-1).reshape(2, S5_GROUPS, S5_BLK, 2 * S5_STATE)
    kern_t = jnp.einsum("dgmp,dgpc->dgcm", ck, jnp.concatenate([bb_re, bb_im], axis=2), precision=HI)
    kern_t = kern_t.reshape(2, S5_GROUPS, S5_CH, q, S5_CH)
    zeros = jnp.zeros_like(kern_t)
    bbt_re = bb_re.transpose(0, 1, 3, 2)[:, :, None]
    bbt_im = bb_im.transpose(0, 1, 3, 2)[:, :, None]
    ct_re = c_re.transpose(0, 1, 3, 2)[:, :, :, None, :]
    ct_im = c_im.transpose(0, 1, 3, 2)[:, :, :, None, :]
    ms, ps, gs = [], [], []
    for d in range(2):
        ext = (jnp.concatenate([zeros[d], kern_t[d]], axis=2) if d == 0
               else jnp.concatenate([kern_t[d, :, :, ::-1], zeros[d]], axis=2))
        ext = ext.reshape(S5_GROUPS, S5_CH, 2 * S5_BLK)
        ms.append(ext)
        pidx = (q - 1 - jnp.arange(q)) if d == 0 else jnp.arange(q)
        pr = pw_re[pidx, d].transpose(1, 0, 2)[:, :, None, :]
        pi = pw_im[pidx, d].transpose(1, 0, 2)[:, :, None, :]
        p_re = pr * bbt_re[d] - pi * bbt_im[d]
        p_im = pr * bbt_im[d] + pi * bbt_re[d]
        pd = jnp.stack([jnp.concatenate([p_re, p_im], axis=3), jnp.concatenate([p_im, p_re], axis=3)])
        ps.append(pd.reshape(2, S5_GROUPS, S5_BLK, 2 * S5_STATE))
        gidx = (jnp.arange(q) + 1) if d == 0 else (q - jnp.arange(q))
        gw_re = pw_re[gidx, d].transpose(1, 2, 0)[..., None]
        gw_im = pw_im[gidx, d].transpose(1, 2, 0)[..., None]
        g_re = ct_re[d] * gw_re - ct_im[d] * gw_im
        g_im = ct_re[d] * gw_im + ct_im[d] * gw_re
        gs.append(jnp.concatenate([g_re, -g_im], axis=1).reshape(S5_GROUPS, 2 * S5_STATE, S5_BLK))
    ar = jnp.concatenate([pw_re[q], pw_re[q]], axis=-1)[:, :, None, :]
    ai = jnp.stack([jnp.concatenate([-pw_im[q], pw_im[q]], axis=-1),
                    jnp.concatenate([pw_im[q], -pw_im[q]], axis=-1)], axis=1)[:, :, :, None, :]
    dsk = jnp.tile(d_skip.reshape(S5_GROUPS, 1, S5_CH), (1, 1, q))
    return (jnp.stack(ms), jnp.stack(ps).astype(BF16), jnp.stack(gs).astype(BF16),
            ar.astype(F32), ai.astype(F32), dsk.astype(F32))


def _s5(ug, params, nb, s_len, lc):
    m, p, g, ar, ai, dsk = params
    nc_tot = s_len // S5_Q
    nc_ctx = lc // S5_Q
    r = nb * nc_tot
    return pl.pallas_call(
        functools.partial(_s5_kernel, nb=nb, nc_ctx=nc_ctx, nc_tot=nc_tot),
        out_shape=jax.ShapeDtypeStruct((S5_GROUPS, r, S5_BLK), F32),
        grid=(S5_GROUPS,),
        in_specs=[pl.BlockSpec((1, r, S5_BLK), lambda gi: (gi, 0, 0)),
                  pl.BlockSpec((2, 1, S5_CH, 2 * S5_BLK), lambda gi: (0, gi, 0, 0)),
                  pl.BlockSpec((2, 2, 1, S5_BLK, 2 * S5_STATE), lambda gi: (0, 0, gi, 0, 0)),
                  pl.BlockSpec((2, 1, 2 * S5_STATE, S5_BLK), lambda gi: (0, gi, 0, 0)),
                  pl.BlockSpec((2, 1, 1, 2 * S5_STATE), lambda gi: (0, gi, 0, 0)),
                  pl.BlockSpec((2, 2, 1, 1, 2 * S5_STATE), lambda gi: (0, 0, gi, 0, 0)),
                  pl.BlockSpec((1, 1, S5_BLK), lambda gi: (gi, 0, 0))],
        out_specs=pl.BlockSpec((1, r, S5_BLK), lambda gi: (gi, 0, 0)),
        scratch_shapes=[pltpu.VMEM((2, 2, r, 2 * S5_STATE), F32), pltpu.VMEM((2, r, 2 * S5_STATE), F32),
                        pltpu.VMEM((2, S5_BLK, S5_BLK), BF16)],
        compiler_params=_cp(("parallel",)),
        name="s5_scan",
    )(ug, m, p, g, ar, ai, dsk)


CONV_ROWS = 2 * TM


def _conv_kernel(x_ref, prev_ref, next_ref, w_ref, b_ref, o_ref, *, s_len, lc):
    x = x_ref[...]
    rows = x.shape[0]
    ridx = lax.broadcasted_iota(jnp.int32, x.shape, 0)
    pos = (pl.program_id(0) * rows) % s_len + ridx
    pos = jnp.where(pos >= s_len, pos - s_len, pos)
    seg_first = jnp.logical_or(pos == 0, pos == lc)
    seg_last = jnp.logical_or(pos == lc - 1, pos == s_len - 1)
    xm = jnp.where(ridx == 0, prev_ref[SUBLANES - 1:SUBLANES, :], pltpu.roll(x, 1, 0))
    xp = jnp.where(ridx == rows - 1, next_ref[0:1, :], pltpu.roll(x, rows - 1, 0))
    xm = jnp.where(seg_first, 0.0, xm)
    xp = jnp.where(seg_last, 0.0, xp)
    y = xm * w_ref[0:1, :] + x * w_ref[1:2, :] + xp * w_ref[2:3, :] + b_ref[...]
    o_ref[...] = _silu(y)


def _conv(xbc, w, b, s_len, lc):
    t, c = xbc.shape
    per = CONV_ROWS // SUBLANES
    last = t // SUBLANES - 1
    return pl.pallas_call(
        functools.partial(_conv_kernel, s_len=s_len, lc=lc),
        out_shape=jax.ShapeDtypeStruct((t, c), F32),
        grid=(t // CONV_ROWS,),
        in_specs=[pl.BlockSpec((CONV_ROWS, c), lambda i: (i, 0)),
                  pl.BlockSpec((SUBLANES, c), lambda i: (jnp.maximum(i * per - 1, 0), 0)),
                  pl.BlockSpec((SUBLANES, c), lambda i: (jnp.minimum((i + 1) * per, last), 0)),
                  pl.BlockSpec((3, c), lambda i: (0, 0)),
                  pl.BlockSpec((1, c), lambda i: (0, 0))],
        out_specs=pl.BlockSpec((CONV_ROWS, c), lambda i: (i, 0)),
        compiler_params=_cp(("parallel",)),
        name="ssd_conv",
    )(xbc, xbc, xbc, w, b.reshape(1, c))


_X_B = GROUP_W
_X_C = GROUP_W + SSD_NGROUPS * SSD_STATE


def _ssd_kernel(xf_ref, dtf_ref, dttf_ref, xr_ref, dtr_ref, dttr_ref, bias_ref, a_ref, biast_ref, at_ref, dsk_ref,
                yf_ref, yr_ref, stf_ref, str_ref):
    @pl.when(pl.program_id(1) == 0)
    def _():
        stf_ref[...] = jnp.zeros_like(stf_ref)
        str_ref[...] = jnp.zeros_like(str_ref)

    par = (bias_ref[...], a_ref[...], biast_ref[...], at_ref[...], dsk_ref[...])
    for j in range(SSD_SUB):
        rf = slice(j * TQ, (j + 1) * TQ)
        yf_ref[rf, :] = _ssd_chunk_step(xf_ref[rf, :], dtf_ref[rf, :], dttf_ref[0, :, rf], par, stf_ref, False)
        rr = slice((SSD_SUB - 1 - j) * TQ, (SSD_SUB - j) * TQ)
        yr_ref[rr, :] = _ssd_chunk_step(xr_ref[rr, :], dtr_ref[rr, :], dttr_ref[0, :, rr], par, str_ref, True)


def _ssd_chunk_step(xc, dt_raw, dtt_raw, par, st_ref, rev):
    bias, a_vec, biast, at_vec, dsk = par
    base = SSD_HEADS if rev else 0
    x = xc[:, 0:GROUP_W]
    dt = _softplus(dt_raw + bias)
    a = dt * a_vec
    dtt = _softplus(dtt_raw + biast)
    at = dtt * at_vec
    ri = lax.broadcasted_iota(jnp.int32, (TQ, TQ), 0)
    ci = lax.broadcasted_iota(jnp.int32, (TQ, TQ), 1)
    causal = (ci >= ri) if rev else (ri >= ci)
    tri = jnp.where(causal, 1.0, 0.0)
    cum_c = _dot_hi(tri, a)
    cum_r = _dot_nt_hi(at, tri)
    edge = 0 if rev else TQ - 1
    tot = cum_c[edge:edge + 1, :]

    shape = (TQ, GROUP_W)
    xdt = x * _per_head_cols(dt, base, SSD_HEADS, shape)
    lane = lax.broadcasted_iota(jnp.int32, shape, 1)
    y = jnp.zeros(shape, F32)
    bmat = [xc[:, _X_B + g * SSD_STATE:_X_B + (g + 1) * SSD_STATE].astype(BF16) for g in range(SSD_NGROUPS)]
    cmat = [xc[:, _X_C + g * SSD_STATE:_X_C + (g + 1) * SSD_STATE].astype(BF16) for g in range(SSD_NGROUPS)]
    cb = [_dot_nt(cmat[g], bmat[g]) for g in range(SSD_NGROUPS)]
    for h in range(SSD_HEADS):
        col = base + h
        seg = jnp.where(causal, cum_c[:, col:col + 1] - cum_r[col:col + 1, :], NEG_INF)
        scores = cb[h // 2] * jnp.exp(seg)
        xh = jnp.where((lane >= h * HEAD_DIM) & (lane < (h + 1) * HEAD_DIM), xdt, 0.0)
        y = y + _dot(scores.astype(BF16), xh.astype(BF16))
    st = st_ref[...]
    yo = jnp.concatenate(
        [_dot_nt(cmat[g], st[g * SSD_STATE:(g + 1) * SSD_STATE].astype(BF16)) for g in range(SSD_NGROUPS)], axis=1)
    y = y + yo * _per_head_cols(jnp.exp(cum_c), base, SSD_HEADS, shape)
    if not rev:
        y = y + x * dsk
    xd = xdt * _per_head_cols(jnp.exp(tot - cum_c), base, SSD_HEADS, shape)
    xdt_t = xd.T.astype(BF16)
    decay = jnp.exp(tot)
    for g in range(SSD_NGROUPS):
        new = _dot(xdt_t[g * SSD_STATE:(g + 1) * SSD_STATE], bmat[g])
        for hh in range(2):
            h = 2 * g + hh
            r0 = h * HEAD_DIM
            st_ref[r0:r0 + HEAD_DIM, :] = (decay[:, base + h:base + h + 1] * st[r0:r0 + HEAD_DIM]
                                           + new[hh * HEAD_DIM:(hh + 1) * HEAD_DIM])
    return y


def _dot_nt_hi(a, b):
    return lax.dot_general(a, b, (((1,), (1,)), ((), ())), preferred_element_type=F32, precision=HI)


def _ssd_chunk(c, rev, nc_ctx, nc_tot):
    if not rev:
        return c
    return jnp.where(c < nc_ctx, nc_ctx - 1 - c, nc_tot - 1 - (c - nc_ctx))


SSD_SUB = TM // TQ


def _ssd_scan(xc, dt, dtt, bias, a, biast, at, dsk, nb, s_len, lc):
    t = xc.shape[0]
    nblk = s_len // TM
    nctx = lc // TM
    fix = lambda b, c: (0, 0)

    def rows(rev):
        return lambda b, c: (b * nblk + _ssd_chunk(c, rev, nctx, nblk), 0)

    def lanes(rev):
        return lambda b, c: (b, 0, _ssd_chunk(c, rev, nctx, nblk))

    def data_specs(rev):
        return [pl.BlockSpec((TM, SSD_XBC), rows(rev)), pl.BlockSpec((TM, LANES), rows(rev)),
                pl.BlockSpec((1, SUBLANES, TM), lanes(rev))]

    state = pltpu.VMEM((SSD_HEADS * HEAD_DIM, SSD_STATE), F32)
    return pl.pallas_call(
        _ssd_kernel,
        out_shape=[jax.ShapeDtypeStruct((t, GROUP_W), F32)] * 2,
        grid=(nb, nblk),
        in_specs=data_specs(False) + data_specs(True) + [
            pl.BlockSpec((1, LANES), fix), pl.BlockSpec((1, LANES), fix),
            pl.BlockSpec((SUBLANES, TQ), fix), pl.BlockSpec((SUBLANES, TQ), fix),
            pl.BlockSpec((1, GROUP_W), fix)],
        out_specs=[pl.BlockSpec((TM, GROUP_W), rows(False)), pl.BlockSpec((TM, GROUP_W), rows(True))],
        scratch_shapes=[state, state],
        compiler_params=_cp(("parallel", "arbitrary")),
        name="ssd_scan",
    )(xc, dt, dtt, xc, dt, dtt, bias, a, biast, at, dsk)


def _ssd(xbc, dt, conv_w, conv_b, dt_bias, a_log, d_skip, nb, s_len, lc):
    xc = _conv(xbc, conv_w, conv_b, s_len, lc)
    nd = 2 * SSD_HEADS
    dtt = dt[:, :nd].reshape(nb, s_len, nd).transpose(0, 2, 1)
    bias = jnp.pad(dt_bias.reshape(1, nd), ((0, 0), (0, LANES - nd)))
    a = jnp.pad(-jnp.exp(a_log).reshape(1, nd), ((0, 0), (0, LANES - nd)))
    biast = jnp.broadcast_to(dt_bias.reshape(nd, 1), (nd, TQ))
    at = jnp.broadcast_to(-jnp.exp(a_log).reshape(nd, 1), (nd, TQ))
    dsk = jnp.repeat(d_skip, HEAD_DIM).reshape(1, GROUP_W)
    return _ssd_scan(xc, dt, dtt, bias, a, biast, at, dsk, nb, s_len, lc)


def _outproj_kernel(x_ref, ys5_ref, oga_ref, y0_ref, y1_ref, z_ref, owa_ref, *refs):
    mods, shared = refs[:ROW_SUB], refs[ROW_SUB:]
    for s in range(ROW_SUB):
        rows = _row_views((x_ref, oga_ref, y0_ref, y1_ref, z_ref, owa_ref) + tuple(shared[-4:-1]), s)
        _outproj_block(rows[0], ys5_ref.at[:, pl.ds(s * S5_TB, S5_TB), :], *rows[1:6], mods[s], *shared[:-4],
                       *rows[6:], shared[-1])


def _outproj_block(x_ref, ys5_ref, oga_ref, y0_ref, y1_ref, z_ref, owa_ref, mod_ref, gluw_ref, glub_ref,
                   ng_ref, wout_ref, n2_ref, wr_ref, br_ref, xn_o, h2_o, route_o, y_scr):
    _s5_unpack_kernel(ys5_ref, y_scr)
    gl = _gelu_tanh(jnp.concatenate([y_scr[0], y_scr[1]], axis=1))
    a = gl * _sigmoid(_dot(gl.astype(BF16), gluw_ref[...]) + glub_ref[...])
    m = (y0_ref[...] + y1_ref[...]) * _silu(z_ref[...])
    m = m * lax.rsqrt(jnp.mean(m * m, axis=-1, keepdims=True) + EPS) * ng_ref[...]
    w = wout_ref
    mix = (_dot(a.astype(BF16), w[0:GROUP_W, :]) + _dot(oga_ref[...].astype(BF16), w[GROUP_W:2 * GROUP_W, :])
           + _dot(m.astype(BF16), w[2 * GROUP_W:3 * GROUP_W, :]) + _dot(owa_ref[...].astype(BF16), w[3 * GROUP_W:, :]))
    xn = x_ref[...] + mod_ref[0, 2:3, :] * mix
    xn_o[...] = xn
    h2 = xn * lax.rsqrt(jnp.mean(xn * xn, axis=-1, keepdims=True) + EPS) * n2_ref[...]
    h2 = h2 * (1.0 + mod_ref[0, 4:5, :]) + mod_ref[0, 3:4, :]
    h2_o[...] = _pack_bf16_pair(h2)
    h_hi = h2.astype(BF16)
    h_lo = (h2 - h_hi.astype(F32)).astype(BF16)
    logits = _dot(h_hi, wr_ref[0]) + (_dot(h_lo, wr_ref[0]) + _dot(h_hi, wr_ref[1])) + br_ref[...]
    lane = lax.broadcasted_iota(jnp.int32, logits.shape, 1).astype(F32)
    big = float(4 * LANES)
    lcoarse = jnp.where(lane < MOE_GROUPS, logits, NEG_INF)
    mx = jnp.max(lcoarse, axis=1, keepdims=True)
    den = jnp.sum(jnp.exp(lcoarse - mx), axis=1, keepdims=True)
    grp = jnp.min(jnp.where(lcoarse == mx, lane, big), axis=1, keepdims=True)
    pg = 1.0 / den
    lo = ROUTE_FINE0 + grp * MOE_PER_GROUP
    lf = jnp.where(lane >= lo, jnp.where(lane < lo + MOE_PER_GROUP, logits, NEG_INF), NEG_INF)
    v1 = jnp.max(lf, axis=1, keepdims=True)
    i1 = jnp.min(jnp.where(lf == v1, lane, big), axis=1, keepdims=True)
    lf2 = jnp.where(lane == i1, NEG_INF, lf)
    v2 = jnp.max(lf2, axis=1, keepdims=True)
    i2 = jnp.min(jnp.where(lf2 == v2, lane, big), axis=1, keepdims=True)
    e2 = jnp.exp(v2 - v1)
    w1 = pg / (1.0 + e2)
    w2 = w1 * e2
    route = jnp.where(lane == 0, i1 - ROUTE_FINE0,
                      jnp.where(lane == 1, i2 - ROUTE_FINE0,
                                jnp.where(lane == 2, w1, jnp.where(lane == 3, w2, 0.0))))
    route_o[...] = route


def _outproj(x, ys5, oga, y0, y1, z, owa, mod, glu_w, glu_b, ssd_norm_g, w_out, norm2_g, wr, br, nb, nblk):
    t, d = x.shape
    row = lambda i: (i, 0)
    fix = lambda i: (0, 0)
    step = ROW_SUB * TM
    gw = pl.BlockSpec((step, GROUP_W), row)
    wr_hi = wr.astype(BF16)
    mod_specs = [pl.BlockSpec((1, 6, d), lambda i, s=s: (_mod_row(ROW_SUB * i + s, nblk, nb), 0, 0))
                 for s in range(ROW_SUB)]
    return pl.pallas_call(
        _outproj_kernel,
        out_shape=[jax.ShapeDtypeStruct((t, d), F32), jax.ShapeDtypeStruct((t, d // 2), jnp.uint32),
                   jax.ShapeDtypeStruct((t, LANES), F32)],
        grid=(t // step,),
        in_specs=[pl.BlockSpec((step, d), row),
                  pl.BlockSpec((S5_GROUPS, ROW_SUB * S5_TB, S5_BLK), lambda i: (0, i, 0)),
                  gw, gw, gw, gw, gw] + mod_specs + [
                  pl.BlockSpec((GROUP_W, GROUP_W), fix),
                  pl.BlockSpec((1, GROUP_W), fix),
                  pl.BlockSpec((1, GROUP_W), fix),
                  pl.BlockSpec((d, d), fix),
                  pl.BlockSpec((1, d), fix),
                  pl.BlockSpec((2, d, LANES), lambda i: (0, 0, 0)),
                  pl.BlockSpec((1, LANES), fix)],
        out_specs=[pl.BlockSpec((step, d), row), pl.BlockSpec((step, d // 2), row), pl.BlockSpec((step, LANES), row)],
        scratch_shapes=[pltpu.VMEM((GROUP_W // LANES, TM, LANES), F32)],
        compiler_params=_cp(("parallel",)),
        name="out_proj_router",
    )(x, ys5, oga, y0, y1, z, owa, *([mod] * ROW_SUB), glu_w.astype(BF16), glu_b.reshape(1, -1), ssd_norm_g.reshape(1, -1),
      w_out.astype(BF16), norm2_g.reshape(1, -1), jnp.stack([wr_hi, (wr - wr_hi.astype(F32)).astype(BF16)]), br)


def _pack_router(coarse_w, coarse_b, fine_w, fine_b):
    def lanes(coarse, fine):
        gap = jnp.zeros(coarse.shape[:-1] + (ROUTE_FINE0 - MOE_GROUPS,), F32)
        tail = jnp.zeros(coarse.shape[:-1] + (LANES - ROUTE_FINE0 - N_EXPERTS,), F32)
        return jnp.concatenate([coarse, gap, fine, tail], axis=-1)

    return lanes(coarse_w, fine_w), lanes(coarse_b[None, :], fine_b[None, :])


def _gather_rows(src, idx):
    m = idx.shape[0]
    d = src.shape[1]
    workers = SC_CORES * SC_SUBCORES
    k = SC_FETCH_K
    nch = m // (workers * k)
    assert nch * workers * k == m
    mesh = plsc.VectorSubcoreMesh(core_axis_name="c", subcore_axis_name="s")

    @functools.partial(
        pl.kernel, mesh=mesh,
        out_type=jax.ShapeDtypeStruct((m, d), src.dtype),
        scratch_types=[pltpu.VMEM((nch, k), jnp.int32),
                       pltpu.VMEM((k, d), src.dtype),
                       pltpu.SemaphoreType.DMA],
    )
    def gather(src_hbm, idx_hbm, out_hbm, idx_v, rows_v, sem):
        wid = lax.axis_index("s") * SC_CORES + lax.axis_index("c")
        pltpu.sync_copy(idx_hbm.at[wid], idx_v)

        @pl.loop(0, nch)
        def _(j):
            off = pl.multiple_of((wid * nch + j) * k, k)
            pltpu.async_copy(src_hbm.at[idx_v.at[j]], rows_v, sem).wait()
            pltpu.sync_copy(rows_v, out_hbm.at[pl.ds(off, k)])

    return gather(src, idx.reshape(workers, nch, k))


def _scatter_rows(src, dst0, dst1, nrows):
    t, d = src.shape
    workers = SC_CORES * SC_SUBCORES
    nch = t // (workers * SC_GATHER_K)
    assert nch * workers * SC_GATHER_K == t
    mesh = plsc.VectorSubcoreMesh(core_axis_name="c", subcore_axis_name="s")

    @functools.partial(
        pl.kernel, mesh=mesh,
        out_type=jax.ShapeDtypeStruct((nrows, d), src.dtype),
        scratch_types=[pltpu.VMEM((nch, SC_GATHER_K), jnp.int32),
                       pltpu.VMEM((nch, SC_GATHER_K), jnp.int32),
                       pltpu.VMEM((SC_GATHER_K, d), src.dtype),
                       pltpu.SemaphoreType.DMA((2,))],
    )
    def scatter(src_hbm, d0_hbm, d1_hbm, out_hbm, i0_v, i1_v, rows_v, sem):
        wid = lax.axis_index("s") * SC_CORES + lax.axis_index("c")
        pltpu.sync_copy(d0_hbm.at[wid], i0_v)
        pltpu.sync_copy(d1_hbm.at[wid], i1_v)

        @pl.loop(0, nch)
        def _(j):
            off = pl.multiple_of((wid * nch + j) * SC_GATHER_K, SC_GATHER_K)
            pltpu.sync_copy(src_hbm.at[pl.ds(off, SC_GATHER_K)], rows_v)
            first = pltpu.async_copy(rows_v, out_hbm.at[i0_v.at[j]], sem.at[0])
            second = pltpu.async_copy(rows_v, out_hbm.at[i1_v.at[j]], sem.at[1])
            first.wait()
            second.wait()

    return scatter(src, dst0.reshape(workers, nch, SC_GATHER_K), dst1.reshape(workers, nch, SC_GATHER_K))


def _expert_kernel(be_ref, nused_ref, nvalid_ref, nxt_ref, slot_ref, x_ref, wg_hbm, wu_hbm, wd_hbm, o_ref,
                   wg_f, wu_f, wd_f, wg_s, wu_s, wd_s, sem, *, layer):
    i = pl.program_id(0)
    used = i < nused_ref[0]
    new_expert = jnp.logical_or(i == 0, be_ref[i] != be_ref[jnp.maximum(i - 1, 0)])

    def weight_copies(expert, slot):
        return [pltpu.make_async_copy(w.at[layer, expert], f.at[slot], sem.at[slot, j])
                for j, (w, f) in enumerate(((wg_hbm, wg_f), (wu_hbm, wu_f), (wd_hbm, wd_f)))]

    @pl.when(jnp.logical_and(used, new_expert))
    def _():
        slot = slot_ref[i]

        @pl.when(i == 0)
        def _():
            for c in weight_copies(be_ref[i], slot):
                c.start()

        for c in weight_copies(be_ref[i], slot):
            c.wait()
        wg_s[...] = wg_f[slot].astype(BF16)
        wu_s[...] = wu_f[slot].astype(BF16)
        wd_s[...] = wd_f[slot].astype(BF16)

        @pl.when(nxt_ref[i] >= 0)
        def _():
            for c in weight_copies(nxt_ref[i], 1 - slot):
                c.start()

    def swiglu(rows):
        row = rows.start + lax.broadcasted_iota(jnp.int32, (rows.stop - rows.start, x_ref.shape[1]), 0)
        lo, hi = _unpack_bf16_pair(jnp.where(row < nvalid_ref[i], x_ref[rows, :], jnp.uint32(0)))
        lo = lo.astype(BF16)
        hi = hi.astype(BF16)
        half = lo.shape[1]
        gate = _dot(lo, wg_s[0:half, :]) + _dot(hi, wg_s[half:, :])
        up = _dot(lo, wu_s[0:half, :]) + _dot(hi, wu_s[half:, :])
        o_ref[rows, :] = _pack_bf16_pair(_dot((_silu(gate) * up).astype(BF16), wd_s[...]))

    used = i < nused_ref[0]
    half_rows = MOE_TM // 2

    @pl.when(jnp.logical_and(used, nvalid_ref[i] > half_rows))
    def _():
        swiglu(slice(0, MOE_TM))

    @pl.when(jnp.logical_and(used, nvalid_ref[i] <= half_rows))
    def _():
        swiglu(slice(0, half_rows))
        o_ref[half_rows:, :] = jnp.zeros((MOE_TM - half_rows, o_ref.shape[1]), o_ref.dtype)

    @pl.when(jnp.logical_not(used))
    def _():
        o_ref[...] = jnp.zeros_like(o_ref)


def _experts(xs, blk_e, n_used, n_valid, nxt_e, slot, wg, wu, wd, layer):
    rows, dp = xs.shape
    d = 2 * dp
    nblocks = rows // MOE_TM
    de = wg.shape[3]
    blk = lambda i, *_: (i, 0)
    hbm = pl.BlockSpec(memory_space=pl.ANY)
    grid_spec = pltpu.PrefetchScalarGridSpec(
        num_scalar_prefetch=5,
        grid=(nblocks,),
        in_specs=[pl.BlockSpec((MOE_TM, dp), blk), hbm, hbm, hbm],
        out_specs=pl.BlockSpec((MOE_TM, dp), blk),
        scratch_shapes=[pltpu.VMEM((2, d, de), F32), pltpu.VMEM((2, d, de), F32), pltpu.VMEM((2, de, d), F32),
                        pltpu.VMEM((d, de), BF16), pltpu.VMEM((d, de), BF16), pltpu.VMEM((de, d), BF16),
                        pltpu.SemaphoreType.DMA((2, 3))],
    )
    return pl.pallas_call(
        functools.partial(_expert_kernel, layer=layer),
        out_shape=jax.ShapeDtypeStruct((rows, dp), jnp.uint32),
        grid_spec=grid_spec,
        compiler_params=_cp(("arbitrary",)),
        name="moe_experts",
    )(blk_e, n_used, n_valid, nxt_e, slot, xs, wg, wu, wd)


def _final_kernel(x_ref, r0_ref, r1_ref, route_ref, mod_ref, fg_ref, o_ref):
    y = _moe_residual(x_ref, r0_ref, r1_ref, route_ref, mod_ref)
    o_ref[...] = y * lax.rsqrt(jnp.mean(y * y, axis=-1, keepdims=True) + EPS) * fg_ref[...]


def _final(xn, rows2, route, mod, final_g, nb, nblk):
    t, d = xn.shape
    nlat = nblk - 1
    grid = (nb * nlat,)
    src = lambda i: ((i // nlat) * nblk + 1 + i % nlat, 0)
    modi = lambda i: (i // nlat, 0, 0)
    out_rows = nb * nlat * TM
    return pl.pallas_call(
        _final_kernel,
        out_shape=jax.ShapeDtypeStruct((out_rows, d), F32),
        grid=grid,
        in_specs=[pl.BlockSpec((TM, d), src),
                  pl.BlockSpec((TM, d // 2), src),
                  pl.BlockSpec((TM, d // 2), lambda i: (src(i)[0] + t // TM, 0)),
                  pl.BlockSpec((TM, LANES), src),
                  pl.BlockSpec((1, 6, d), modi),
                  pl.BlockSpec((1, d), lambda i: (0, 0))],
        out_specs=pl.BlockSpec((TM, d), lambda i: (i, 0)),
        compiler_params=_cp(("parallel",)),
        name="moe_combine_final",
    )(xn, rows2, rows2, route, mod, final_g.reshape(1, d))


def _moe(h2, route, wg, wu, wd, layer):
    t, d = h2.shape
    n_slots = 2 * t
    experts = jnp.arange(N_EXPERTS, dtype=F32)[None, :]
    oh0 = (route[:, 0:1] == experts).astype(F32)
    oh1 = (route[:, 1:2] == experts).astype(F32)
    both = (oh0 + oh1).reshape(t // LANES, LANES, N_EXPERTS)
    tri = jnp.tril(jnp.ones((LANES, LANES), F32))
    intra = jnp.einsum("ij,bjk->bik", tri, both)
    blk_tot = intra[:, -1, :]
    blk_cum = jnp.cumsum(blk_tot, axis=0)
    earlier = (intra - both + (blk_cum - blk_tot)[:, None, :]).reshape(t, N_EXPERTS)
    counts = blk_cum[-1].astype(jnp.int32)
    pcounts = (counts + MOE_TM - 1) // MOE_TM * MOE_TM
    pends = jnp.cumsum(pcounts)
    pstarts = pends - pcounts
    base = pstarts.astype(F32)[None, :] + earlier
    dest0 = jnp.sum(oh0 * base, axis=1).astype(jnp.int32)
    dest1 = jnp.sum(oh1 * base, axis=1).astype(jnp.int32)
    nblocks = -(-n_slots // MOE_TM) + N_EXPERTS
    nrows = -(-nblocks * MOE_TM // GATHER_ROWS) * GATHER_ROWS
    nblocks = nrows // MOE_TM
    blk_start = jnp.arange(nblocks, dtype=jnp.int32) * MOE_TM
    blk_e = jnp.minimum(jnp.sum((pends[None, :] <= blk_start[:, None]).astype(jnp.int32), axis=1), N_EXPERTS - 1)
    n_used = (pends[-1] // MOE_TM).astype(jnp.int32).reshape(1)
    n_valid = jnp.clip((pstarts + counts)[blk_e] - blk_start, 0, MOE_TM).astype(jnp.int32)
    ids = jnp.arange(N_EXPERTS, dtype=jnp.int32)
    has = counts > 0
    later = lax.cummin(jnp.where(has, ids, N_EXPERTS)[::-1])[::-1]
    nxt = jnp.concatenate([later[1:], jnp.full((1,), N_EXPERTS, jnp.int32)])
    nxt = jnp.where(nxt >= N_EXPERTS, -1, nxt)
    slot = (jnp.cumsum(has.astype(jnp.int32)) - 1) % 2
    xs = _scatter_rows(h2, dest0, dest1, nrows)
    ys = _experts(xs, blk_e, n_used, n_valid, nxt[blk_e], slot[blk_e], wg, wu, wd, layer)
    return _gather_rows(ys, jnp.concatenate([dest0, dest1]))


def kernel(x, c, ctx, c_ctx, ada_w, ada_b, norm1_g, norm2_g, w_in, w_out, s5_lam_re, s5_lam_im, s5_log_dt, s5_b_re, s5_b_im, s5_c_re, s5_c_im, s5_d, s5_glu_w, s5_glu_b, ga_qn_g, ga_kn_g, ssd_conv_w, ssd_conv_b, ssd_dt_bias, ssd_a_log, ssd_d, ssd_norm_g, wa_sink, moe_coarse_w, moe_coarse_b, moe_fine_w, moe_fine_b, moe_w_gate, moe_w_up, moe_w_down, final_g):
    nb, l, d = x.shape
    lc = ctx.shape[1]
    depth = ada_w.shape[0]
    assert lc == TM and l % TM == 0 and nb <= SUBLANES - 1 and d == D_MODEL
    s_len = lc + l
    nblk = s_len // TM
    t = nb * s_len

    cc = jnp.zeros((SUBLANES, d), F32).at[:nb].set(c).at[nb].set(c_ctx)
    mods = _ada(cc, ada_w, ada_b).reshape(depth, SUBLANES, 6, d)
    cos_t, sin_t = _rope_tables(lc, l)
    w_packed = jax.vmap(_pack_w_in)(w_in)
    s5_tabs = jax.vmap(_s5_params)(s5_lam_re, s5_lam_im, s5_log_dt, s5_b_re, s5_b_im, s5_c_re, s5_c_im, s5_d)
    wrs, brs = jax.vmap(_pack_router)(moe_coarse_w, moe_coarse_b, moe_fine_w, moe_fine_b)

    src = ("first", x.reshape(nb * l, d), ctx.reshape(nb * lc, d))
    for i in range(depth):
        mod = mods[i]
        (xm, xbc, ug, z, dt, gaq, gak, gav, waq, wak, wav) = _inproj(
            src, mod, norm1_g[i], w_packed[i], cos_t, sin_t, ga_qn_g[i], ga_kn_g[i], nb, nblk)
        ys5 = _s5(ug, tuple(tab[i] for tab in s5_tabs), nb, s_len, lc)
        oga, owa = _attn(wa_sink[i], gaq, gak, gav, waq, wak, wav, nb, s_len, lc)
        y0, y1 = _ssd(xbc, dt, ssd_conv_w[i], ssd_conv_b[i], ssd_dt_bias[i], ssd_a_log[i], ssd_d[i], nb, s_len, lc)
        wr, br = wrs[i], brs[i]
        xn, h2, route = _outproj(xm, ys5, oga, y0, y1, z, owa, mod, s5_glu_w[i], s5_glu_b[i], ssd_norm_g[i],
                                 w_out[i], norm2_g[i], wr, br, nb, nblk)
        rows2 = _moe(h2, route, moe_w_gate, moe_w_up, moe_w_down, i)
        src = ("moe", xn, rows2, route, mod)
    return _final(xn, rows2, route, mod, final_g, nb, nblk).reshape(nb, l, d)
```

```python
import functools
import math

import jax
import jax.numpy as jnp
import numpy as np
from jax import lax
from jax.experimental import pallas as pl
from jax.experimental.pallas import tpu as pltpu
from jax.experimental.pallas import tpu_sc as plsc

F32 = jnp.float32
BF16 = jnp.bfloat16
HI = lax.Precision.HIGHEST

D_MODEL = 1024
GRID_W = 64
GROUP_W = 256
HEAD_DIM = 64
ROPE_FREQS = HEAD_DIM // 4
ROPE_BASE = 10000.0
EPS = 1e-6
S5_CH = 16
S5_GROUPS = GROUP_W // S5_CH
S5_STATE = 64
N_HEADS = 4
SSD_HEADS = 4
SSD_NGROUPS = 2
SSD_STATE = 128
SSD_XBC = GROUP_W + 2 * SSD_NGROUPS * SSD_STATE
WINDOW = 128
MOE_GROUPS = 4
MOE_PER_GROUP = 8
N_EXPERTS = 32
D_EXPERT = D_MODEL // 2

LANES = 128
SUBLANES = 8
TM = 256
TQ = 128
GA_TQ = 128
GA_SUB = 2
S5_Q = 32
S5_BLK = S5_Q * S5_CH
MOE_TM = 512
SC_CORES = 2
SC_SUBCORES = 16
SC_GATHER_K = 32
SC_FETCH_K = 64
GATHER_ROWS = SC_CORES * SC_SUBCORES * SC_GATHER_K
ROUTE_FINE0 = 32
VMEM_LIMIT = 56 * 1024 * 1024

NEG_INF = float("-inf")
LOG2E = math.log2(math.e)


def _cp(sem, vmem=VMEM_LIMIT):
    return pltpu.CompilerParams(dimension_semantics=sem, vmem_limit_bytes=vmem)


def _dot(a, b):
    return jnp.dot(a, b, preferred_element_type=F32)


def _dot_hi(a, b):
    return jnp.dot(a, b, preferred_element_type=F32, precision=HI)


def _dot_nt(a, b):
    return lax.dot_general(a, b, (((1,), (1,)), ((), ())), preferred_element_type=F32)


def _sigmoid(x):
    return 1.0 / (1.0 + jnp.exp(-x))


def _silu(x):
    return x * _sigmoid(x)


def _gelu_tanh(x):
    return 0.5 * x * (1.0 + jnp.tanh(math.sqrt(2.0 / math.pi) * (x + 0.044715 * (x * x * x))))


def _softplus(x):
    return jnp.maximum(x, 0.0) + jnp.log(1.0 + jnp.exp(-jnp.abs(x)))


_HI16 = 0xFFFF0000


def _pack_bf16_pair(x):
    n = x.shape[1] // 2
    bits = pltpu.bitcast(x.astype(BF16).astype(F32), jnp.uint32)
    return (bits[:, n:] & jnp.uint32(_HI16)) | (bits[:, :n] >> 16)


def _unpack_bf16_pair(w):
    return pltpu.bitcast(w << 16, F32), pltpu.bitcast(w & jnp.uint32(_HI16), F32)


def _per_head_cols(v, base, n_heads, shape):
    lane = lax.broadcasted_iota(jnp.int32, shape, 1)
    out = jnp.broadcast_to(v[:, base + n_heads - 1:base + n_heads], shape)
    for h in range(n_heads - 2, -1, -1):
        out = jnp.where(lane < (h + 1) * HEAD_DIM, v[:, base + h:base + h + 1], out)
    return out


def _ada_kernel(c_ref, w_ref, b_ref, o_ref):
    c = c_ref[...]
    o_ref[0] = _dot_hi(_silu(c), w_ref[0]) + b_ref[0]


def _ada(cc, ada_w, ada_b):
    depth, d, n = ada_w.shape
    tn = 1536
    return pl.pallas_call(
        _ada_kernel,
        out_shape=jax.ShapeDtypeStruct((depth, SUBLANES, n), F32),
        grid=(depth, n // tn),
        in_specs=[pl.BlockSpec((SUBLANES, d), lambda l, j: (0, 0)),
                  pl.BlockSpec((1, d, tn), lambda l, j: (l, 0, j)),
                  pl.BlockSpec((1, 1, tn), lambda l, j: (l, 0, j))],
        out_specs=pl.BlockSpec((1, SUBLANES, tn), lambda l, j: (l, 0, j)),
        compiler_params=_cp(("parallel", "parallel")),
        name="ada_mod",
    )(cc, ada_w, ada_b.reshape(depth, 1, n))


_C_XBC = 0
_C_U = _C_XBC + SSD_XBC
_C_Z = _C_U + GROUP_W
_C_DT = _C_Z + GROUP_W
_C_GAQ = _C_DT + LANES
_C_WAQ = _C_GAQ + N_HEADS * LANES
_C_GAK = _C_WAQ + N_HEADS * LANES
_C_GAV = _C_GAK + LANES
_C_WAK = _C_GAV + LANES
_C_WAV = _C_WAK + LANES
_C_END = _C_WAV + LANES


def _expand_q_cols(wq):
    zero = jnp.zeros((wq.shape[0], HEAD_DIM), wq.dtype)
    parts = []
    for h in range(N_HEADS):
        head = wq[:, h * HEAD_DIM:(h + 1) * HEAD_DIM]
        parts += [head, zero] if h // 2 == 0 else [zero, head]
    return jnp.concatenate(parts, axis=1)


def _pack_w_in(w_in):
    cuts = np.cumsum([256, 256, 128, 128, 256, SSD_XBC, 2 * SSD_HEADS, 256, 128, 128])[:-1]
    u, gaq, gak, gav, z, xbc, dt, waq, wak, wav = jnp.split(w_in, [int(c) for c in cuts], axis=1)
    dt = jnp.pad(dt, ((0, 0), (0, LANES - dt.shape[1])))
    w = jnp.concatenate([xbc, u, z, dt, _expand_q_cols(gaq), _expand_q_cols(waq), gak, gav, wak, wav], axis=1)
    return w.astype(BF16)


def _rope(x, cos, sins):
    w = x.shape[1]
    if w > LANES:
        cos = jnp.concatenate([cos] * (w // LANES), axis=1)
        sins = jnp.concatenate([sins] * (w // LANES), axis=1)
    lane = lax.broadcasted_iota(jnp.int32, x.shape, 1)
    up = pltpu.roll(x, w - ROPE_FREQS, 1)
    dn = pltpu.roll(x, ROPE_FREQS, 1)
    partner = jnp.where((lane & ROPE_FREQS) == 0, up, dn)
    return x * cos + partner * sins


def _v_with_ones(v):
    lo = lax.broadcasted_iota(jnp.int32, v.shape, 1) < HEAD_DIM
    return jnp.concatenate([jnp.where(lo, v, 1.0), jnp.where(lo, 1.0, v)], axis=1).astype(BF16)


def _moe_residual(xn_ref, r0_ref, r1_ref, route_ref, mod_ref):
    route = route_ref[...]
    r0 = jnp.concatenate(_unpack_bf16_pair(r0_ref[...]), axis=1)
    r1 = jnp.concatenate(_unpack_bf16_pair(r1_ref[...]), axis=1)
    return xn_ref[...] + mod_ref[0, 5:6, :] * (route[:, 2:3] * r0 + route[:, 3:4] * r1)


ROW_SUB = 2


def _row_views(refs, s):
    return [r.at[pl.ds(s * TM, TM), :] for r in refs]


def _inproj_kernel(*refs, first, nblk):
    n_blk_in = (2 if first else 5) + 3
    shared = refs[ROW_SUB * n_blk_in:]
    g_ref, w_ref, qn_ref, kn_ref = shared[:4]
    xm_o, xbc_o, ug_o = shared[4:7]
    rest_o = shared[7:-1]
    u_scr = shared[-1]
    for s in range(ROW_SUB):
        blk_refs = refs[s * n_blk_in:(s + 1) * n_blk_in]
        xm_v, xbc_v = _row_views((xm_o, xbc_o), s)
        ug_v = ug_o.at[:, pl.ds(s * S5_TB, S5_TB), :]
        _inproj_block(blk_refs, g_ref, w_ref, qn_ref, kn_ref, xm_v, xbc_v, ug_v, _row_views(rest_o, s), u_scr,
                      first, (pl.program_id(0) * ROW_SUB + s) % nblk == 0)


def _inproj_block(blk_refs, g_ref, w_ref, qn_ref, kn_ref, xm_o, xbc_o, ug_o, rest_o, u_scr, first, is_ctx):
    if first:
        lat_ref, ctx_ref = blk_refs[:2]
        x = jnp.where(is_ctx, ctx_ref[...], lat_ref[...])
    else:
        x = _moe_residual(*blk_refs[:5])
    mod_ref, cos_ref, sin_ref = blk_refs[-3:]
    z_o, dt_o, gaq_o, gak_o, gav_o, waq_o, wak_o, wav_o = rest_o
    xm_o[...] = x
    ms = jnp.mean(x * x, axis=-1, keepdims=True)
    xn = x * lax.rsqrt(ms + EPS) * g_ref[...]
    h = xn * (1.0 + mod_ref[0, 1:2, :]) + mod_ref[0, 0:1, :]
    hb = h.astype(BF16)

    def proj(lo, hi):
        return _dot(hb, w_ref[:, lo:hi])

    cos = cos_ref[...]
    sins = sin_ref[...]
    scale = LOG2E * HEAD_DIM ** -0.5
    q = proj(_C_GAQ, _C_WAQ)
    qs = q * q
    inv = jnp.concatenate(
        [jnp.broadcast_to(lax.rsqrt(jnp.sum(qs[:, s * LANES:(s + 1) * LANES], axis=1, keepdims=True)
                                    * (1.0 / HEAD_DIM) + EPS), (q.shape[0], LANES)) for s in range(N_HEADS)], axis=1)
    gaq_o[...] = (_rope(q * inv * qn_ref[...], cos, sins) * scale).astype(BF16)
    waq_o[...] = (_rope(proj(_C_WAQ, _C_GAK), cos, sins) * scale).astype(BF16)
    k = proj(_C_GAK, _C_GAV)
    ks = k * k
    lane = lax.broadcasted_iota(jnp.int32, k.shape, 1)
    lo = lane < HEAD_DIM
    ms0 = jnp.sum(jnp.where(lo, ks, 0.0), axis=1, keepdims=True)
    ms1 = jnp.sum(jnp.where(lo, 0.0, ks), axis=1, keepdims=True)
    kinv = lax.rsqrt(jnp.where(lo, ms0, ms1) * (1.0 / HEAD_DIM) + EPS)
    gak_o[...] = _rope(k * kinv * kn_ref[...], cos, sins).astype(BF16)
    gav_o[...] = _v_with_ones(proj(_C_GAV, _C_WAK))
    wak_o[...] = _rope(proj(_C_WAK, _C_WAV), cos, sins).astype(BF16)
    wav_o[...] = _v_with_ones(proj(_C_WAV, _C_END))
    xbc_o[...] = proj(_C_XBC, _C_U)
    u = proj(_C_U, _C_Z)
    u_scr[0] = u[:, :LANES]
    u_scr[1] = u[:, LANES:]
    _s5_pack_kernel(u_scr.at[0], u_scr.at[1], ug_o)
    z_o[...] = proj(_C_Z, _C_DT)
    dt_o[...] = proj(_C_DT, _C_GAQ)


def _mod_row(i, nblk, nb):
    return jnp.where(i % nblk == 0, nb, i // nblk)


def _inproj(src, mod, norm_g, w_packed, cos_t, sin_t, qn_g, kn_g, nb, nblk):
    first = src[0] == "first"
    d = src[1].shape[1]
    t = nb * nblk * TM
    row = lambda i: (i, 0)
    fix = lambda i: (0, 0)
    nsteps = t // (ROW_SUB * TM)
    assert nsteps * ROW_SUB * TM == t

    def blk_specs(s):
        bid = lambda i: ROW_SUB * i + s
        modspec = pl.BlockSpec((1, 6, d), lambda i: (_mod_row(bid(i), nblk, nb), 0, 0))
        table = pl.BlockSpec((TM, LANES), lambda i: (bid(i) % nblk, 0))
        if first:
            srcs = [pl.BlockSpec((TM, d), lambda i: ((bid(i) // nblk) * (nblk - 1) + jnp.maximum(bid(i) % nblk - 1, 0), 0)),
                    pl.BlockSpec((TM, d), lambda i: (bid(i) // nblk, 0))]
        else:
            srcs = [pl.BlockSpec((TM, d), lambda i: (bid(i), 0)), pl.BlockSpec((TM, d // 2), lambda i: (bid(i), 0)),
                    pl.BlockSpec((TM, d // 2), lambda i: (bid(i) + t // TM, 0)),
                    pl.BlockSpec((TM, LANES), lambda i: (bid(i), 0)), modspec]
        return srcs + [modspec, table, table]

    if first:
        blk_args = tuple(src[1:]) + (mod, cos_t, sin_t)
    else:
        blk_args = (src[1], src[2], src[2], src[3], src[4], mod, cos_t, sin_t)
    outs = [(d, F32), (SSD_XBC, F32), None, (GROUP_W, F32), (LANES, F32),
            (N_HEADS * LANES, BF16), (LANES, BF16), (2 * LANES, BF16),
            (N_HEADS * LANES, BF16), (LANES, BF16), (2 * LANES, BF16)]
    shapes = [jax.ShapeDtypeStruct((t, o[0]), o[1]) if o else
              jax.ShapeDtypeStruct((S5_GROUPS, t // S5_Q, S5_BLK), F32) for o in outs]
    specs = [pl.BlockSpec((ROW_SUB * TM, o[0]), row) if o else
             pl.BlockSpec((S5_GROUPS, ROW_SUB * S5_TB, S5_BLK), lambda i: (0, i, 0)) for o in outs]
    return pl.pallas_call(
        functools.partial(_inproj_kernel, first=first, nblk=nblk),
        out_shape=shapes,
        grid=(nsteps,),
        in_specs=[sp for s in range(ROW_SUB) for sp in blk_specs(s)] + [
                  pl.BlockSpec((1, d), fix),
                  pl.BlockSpec((d, _C_END), fix),
                  pl.BlockSpec((1, N_HEADS * LANES), fix),
                  pl.BlockSpec((1, LANES), fix)],
        out_specs=specs,
        scratch_shapes=[pltpu.VMEM((GROUP_W // LANES, TM, LANES), F32)],
        compiler_params=_cp(("parallel",)),
        name="in_proj",
    )(*(blk_args * ROW_SUB), norm_g.reshape(1, d), w_packed,
      jnp.tile(qn_g, 2 * N_HEADS).reshape(1, -1), jnp.tile(kn_g, 2).reshape(1, -1))


def _rope_tables(lc, l):
    n_rows = l // GRID_W
    rows = np.repeat(np.arange(n_rows), GRID_W)
    cols = np.tile(np.arange(GRID_W), n_rows)
    inv = np.power(np.float32(ROPE_BASE), -np.arange(ROPE_FREQS, dtype=np.float32) / ROPE_FREQS)
    ang = np.stack([rows, cols], axis=-1).astype(np.float32)[..., None] * inv
    cos = np.cos(ang)
    sin = np.sin(ang)
    cos64 = np.stack([cos, cos], axis=2).reshape(l, HEAD_DIM)
    sin64 = np.stack([-sin, sin], axis=2).reshape(l, HEAD_DIM)
    cos64 = np.concatenate([np.ones((lc, HEAD_DIM), np.float32), cos64], axis=0)
    sin64 = np.concatenate([np.zeros((lc, HEAD_DIM), np.float32), sin64], axis=0)
    return (jnp.asarray(np.tile(cos64, (1, 2)), dtype=F32), jnp.asarray(np.tile(sin64, (1, 2)), dtype=F32))


def _merge_heads(o2, kvh):
    tq = o2.shape[0] // 2
    oa, ob = o2[:tq], o2[tq:]
    lane = lax.broadcasted_iota(jnp.int32, oa.shape, 1)
    if kvh == 0:
        return jnp.where(lane < HEAD_DIM, oa, pltpu.roll(ob, HEAD_DIM, 1))
    return jnp.where(lane < HEAD_DIM, pltpu.roll(oa, HEAD_DIM, 1), ob)


def _stack_q(q_ref, rows, kvh):
    return jnp.concatenate([q_ref[rows, (2 * kvh) * LANES:(2 * kvh + 1) * LANES],
                            q_ref[rows, (2 * kvh + 1) * LANES:(2 * kvh + 2) * LANES]], axis=0)


def _ga_attend(q_ref, k_ref, v_ref, o_ref, nkeys):
    k = k_ref[0:nkeys, :]
    for sub in range(GA_SUB):
        rows = slice(sub * GA_TQ, (sub + 1) * GA_TQ)
        scores = [_dot_nt(_stack_q(q_ref, rows, kvh), k) for kvh in range(2)]
        outs = []
        for kvh in range(2):
            s = scores[kvh]
            p = jnp.exp2((s - jnp.max(s, axis=1, keepdims=True)).astype(BF16))
            o2 = _dot(p, v_ref[0:nkeys, kvh * LANES:(kvh + 1) * LANES])
            outs.append(_merge_heads(o2 / pltpu.roll(o2, HEAD_DIM, 1), kvh))
        o_ref[rows, :] = jnp.concatenate(outs, axis=1)


def _attn_kernel(sink_ref, gq_ref, gk_ref, gv_ref, wq_ref, wk_ref, wv_ref, og_ref, ow_ref, *, lc):
    is_ctx = pl.program_id(1) < lc // TM

    @pl.when(is_ctx)
    def _():
        _ga_attend(gq_ref, gk_ref, gv_ref, og_ref, lc)
        _wa_attend(sink_ref, wq_ref, wk_ref, wv_ref, ow_ref, lc)

    @pl.when(jnp.logical_not(is_ctx))
    def _():
        _ga_attend(gq_ref, gk_ref, gv_ref, og_ref, gk_ref.shape[0])
        _wa_attend(sink_ref, wq_ref, wk_ref, wv_ref, ow_ref, lc)


def _attn(sink, gq, gk, gv, wq, wk, wv, nb, s_len, lc):
    t = gq.shape[0]
    nq = s_len // TM
    assert GA_SUB * GA_TQ == TM and WA_SUB * TQ == TM
    qspec = pl.BlockSpec((TM, N_HEADS * LANES), lambda b, j: (b * nq + j, 0))
    kspec = pl.BlockSpec((s_len, LANES), lambda b, j: (b, 0))
    vspec = pl.BlockSpec((s_len, 2 * LANES), lambda b, j: (b, 0))
    ospec = pl.BlockSpec((TM, GROUP_W), lambda b, j: (b * nq + j, 0))
    return pl.pallas_call(
        functools.partial(_attn_kernel, lc=lc),
        out_shape=[jax.ShapeDtypeStruct((t, GROUP_W), F32)] * 2,
        grid=(nb, nq),
        in_specs=[pl.BlockSpec(memory_space=pltpu.SMEM), qspec, kspec, vspec, qspec, kspec, vspec],
        out_specs=[ospec, ospec],
        compiler_params=_cp(("parallel", "arbitrary")),
        name="attention",
    )(sink, gq, gk, gv, wq, wk, wv)


WA_SUB = TM // TQ


def _wa_attend(sink_ref, q_ref, k_ref, v_ref, o_ref, lc):
    s_len = k_ref.shape[0]
    kc = k_ref[0:lc, :]
    row = lax.broadcasted_iota(jnp.int32, (2 * TQ, 1), 0)
    for sub in range(WA_SUB):
        rows = slice(sub * TQ, (sub + 1) * TQ)
        n = pl.program_id(1) * WA_SUB + sub - lc // TQ
        start = pl.multiple_of(jnp.clip(lc + (n - 1) * TQ, lc, s_len - 3 * TQ), TQ)
        kb = k_ref[pl.ds(start, 3 * TQ), :]
        qpos = n * TQ + lax.broadcasted_iota(jnp.int32, (TQ, 3 * TQ), 0)
        kpos = (start - lc) + lax.broadcasted_iota(jnp.int32, (TQ, 3 * TQ), 1)
        reach = jnp.where(n >= 0, WINDOW, -1)
        valid = jnp.abs(qpos - kpos) <= reach
        valid = jnp.concatenate([valid, valid], axis=0)
        outs = []
        for kvh in range(2):
            q2 = jnp.concatenate([q_ref[rows, (2 * kvh) * LANES:(2 * kvh + 1) * LANES],
                                  q_ref[rows, (2 * kvh + 1) * LANES:(2 * kvh + 2) * LANES]], axis=0)
            sc = _dot_nt(q2, kc)
            sb = jnp.where(valid, _dot_nt(q2, kb), NEG_INF)
            sink = jnp.where(row < TQ, sink_ref[2 * kvh], sink_ref[2 * kvh + 1]) * LOG2E
            m = jnp.maximum(jnp.maximum(jnp.max(sc, axis=1, keepdims=True), jnp.max(sb, axis=1, keepdims=True)), sink)
            pc = jnp.exp2((sc - m).astype(BF16))
            pb = jnp.exp2((sb - m).astype(BF16))
            vcols = slice(kvh * LANES, (kvh + 1) * LANES)
            o2 = _dot(pc, v_ref[0:lc, vcols]) + _dot(pb, v_ref[pl.ds(start, 3 * TQ), vcols])
            denom = pltpu.roll(o2, HEAD_DIM, 1) + jnp.exp2(sink - m)
            outs.append(_merge_heads(o2 / denom, kvh))
        o_ref[rows, :] = jnp.concatenate(outs, axis=1)


def _s5_chunk_index(t, rev, nc_ctx, nc_tot):
    if not rev:
        return t
    return jnp.where(t < nc_ctx, nc_ctx - 1 - t, nc_tot - 1 - (t - nc_ctx))


def _s5_kernel(u_ref, k_ref, p_ref, g_ref, ar_ref, ai_ref, dsk_ref, y_ref, s_scr, h_scr, m_scr, *, nb, nc_ctx, nc_tot):
    for d in range(2):
        ext = k_ref[d, 0]
        for s in range(S5_Q):
            lo = ((S5_Q - s) if d == 0 else (S5_Q - 1 - s)) * S5_CH
            win = pltpu.roll(ext, (2 * S5_BLK - lo) % (2 * S5_BLK), 1)[:, :S5_BLK]
            m_scr[d, s * S5_CH:(s + 1) * S5_CH, :] = win.astype(BF16)
    uf = u_ref[0]
    u = uf.astype(BF16)
    for d in range(2):
        for k in range(2):
            s_scr[d, k] = _dot(u, p_ref[d, k, 0])
    ar = [jnp.broadcast_to(ar_ref[d, 0], (nb, LANES)) for d in range(2)]
    ai = [[jnp.broadcast_to(ai_ref[d, k, 0], (nb, LANES)) for k in range(2)] for d in range(2)]

    def body(t, carry):
        out = []
        for d in range(2):
            h, hs = carry[d]
            rows = pl.ds(_s5_chunk_index(t, d == 1, nc_ctx, nc_tot), nb, stride=nc_tot)
            h_scr[d, rows, :] = h
            out.append((ar[d] * h + ai[d][0] * hs + s_scr[d, 0, rows, :],
                        ar[d] * hs + ai[d][1] * h + s_scr[d, 1, rows, :]))
        return tuple(out)

    zero = jnp.zeros((nb, LANES), F32)
    lax.fori_loop(0, nc_tot, body, ((zero, zero), (zero, zero)), unroll=2)
    y = uf * dsk_ref[0]
    for d in range(2):
        y = y + _dot(u, m_scr[d]) + _dot(h_scr[d].astype(BF16), g_ref[d, 0])
    y_ref[0] = y


S5_TB = TM // S5_Q
S5_GPS = LANES // S5_CH


def _s5_pack_kernel(lo_ref, hi_ref, o_ref):
    for s in range(S5_Q):
        rows = pl.ds(s, S5_TB, stride=S5_Q)
        halves = (lo_ref[rows, :], hi_ref[rows, :])
        dst = S5_CH * (s % S5_GPS)
        for g in range(S5_GROUPS):
            slab = halves[g // S5_GPS]
            src = S5_CH * (g % S5_GPS)
            moved = slab if src == dst else pltpu.roll(slab, (dst - src) % LANES, 1)
            o_ref[g, :, s * S5_CH:(s + 1) * S5_CH] = moved[:, dst:dst + S5_CH]


def _s5_unpack_kernel(y_ref, o_ref):
    lane_grp = lax.broadcasted_iota(jnp.int32, (S5_TB, LANES), 1) // S5_CH
    for s in range(S5_Q):
        src = S5_CH * (s % S5_GPS)
        for half in range(S5_GROUPS // S5_GPS):
            acc = None
            for gl in range(S5_GPS):
                slab = y_ref[half * S5_GPS + gl, :, (s // S5_GPS) * LANES:(s // S5_GPS + 1) * LANES]
                dst = S5_CH * gl
                moved = slab if src == dst else pltpu.roll(slab, (dst - src) % LANES, 1)
                acc = moved if acc is None else jnp.where(lane_grp == gl, moved, acc)
            o_ref[half, pl.ds(s, S5_TB, stride=S5_Q), :] = acc


def _s5_params(lam_re, lam_im, log_dt, b_re, b_im, c_re, c_im, d_skip):
    q = S5_Q
    dt = jnp.exp(log_dt)[..., None]
    lr, li = lam_re, lam_im
    mag = jnp.exp(lr * dt)
    a_re = mag * jnp.cos(li * dt)
    a_im = mag * jnp.sin(li * dt)
    den = lr * lr + li * li
    f_re = ((a_re - 1.0) * lr + a_im * li) / den
    f_im = (a_im * lr - (a_re - 1.0) * li) / den
    bb_re = f_re[..., None] * b_re - f_im[..., None] * b_im
    bb_im = f_re[..., None] * b_im + f_im[..., None] * b_re
    kk = jnp.arange(q + 1, dtype=F32)[:, None, None, None]
    pmag = jnp.exp(kk * (lr * dt))
    pw_re = pmag * jnp.cos(kk * (li * dt))
    pw_im = pmag * jnp.sin(kk * (li * dt))
    lw_re = pw_re[:q].transpose(1, 2, 0, 3)[:, :, :, None, :]
    lw_im = pw_im[:q].transpose(1, 2, 0, 3)[:, :, :, None, :]
    ck_re = c_re[:, :, None] * lw_re - c_im[:, :, None] * lw_im
    ck_im = c_re[:, :, None] * lw_im + c_im[:, :, None] * lw_re
    ck = jnp.concatenate([ck_re, -ck_im], axis=-1).reshape(2, S5_GROUPS, S5_BLK, 2 * S5_STATE)
    kern_t = jnp.einsum("dgmp,dgpc->dgcm", ck, jnp.concatenate([bb_re, bb_im], axis=2), precision=HI)
    kern_t = kern_t.reshape(2, S5_GROUPS, S5_CH, q, S5_CH)
    zeros = jnp.zeros_like(kern_t)
    bbt_re = bb_re.transpose(0, 1, 3, 2)[:, :, None]
    bbt_im = bb_im.transpose(0, 1, 3, 2)[:, :, None]
    ct_re = c_re.transpose(0, 1, 3, 2)[:, :, :, None, :]
    ct_im = c_im.transpose(0, 1, 3, 2)[:, :, :, None, :]
    ms, ps, gs = [], [], []
    for d in range(2):
        ext = (jnp.concatenate([zeros[d], kern_t[d]], axis=2) if d == 0
               else jnp.concatenate([kern_t[d, :, :, ::-1], zeros[d]], axis=2))
        ext = ext.reshape(S5_GROUPS, S5_CH, 2 * S5_BLK)
        ms.append(ext)
        pidx = (q - 1 - jnp.arange(q)) if d == 0 else jnp.arange(q)
        pr = pw_re[pidx, d].transpose(1, 0, 2)[:, :, None, :]
        pi = pw_im[pidx, d].transpose(1, 0, 2)[:, :, None, :]
        p_re = pr * bbt_re[d] - pi * bbt_im[d]
        p_im = pr * bbt_im[d] + pi * bbt_re[d]
        pd = jnp.stack([jnp.concatenate([p_re, p_im], axis=3), jnp.concatenate([p_im, p_re], axis=3)])
        ps.append(pd.reshape(2, S5_GROUPS, S5_BLK, 2 * S5_STATE))
        gidx = (jnp.arange(q) + 1) if d == 0 else (q - jnp.arange(q))
        gw_re = pw_re[gidx, d].transpose(1, 2, 0)[..., None]
        gw_im = pw_im[gidx, d].transpose(1, 2, 0)[..., None]
        g_re = ct_re[d] * gw_re - ct_im[d] * gw_im
        g_im = ct_re[d] * gw_im + ct_im[d] * gw_re
        gs.append(jnp.concatenate([g_re, -g_im], axis=1).reshape(S5_GROUPS, 2 * S5_STATE, S5_BLK))
    ar = jnp.concatenate([pw_re[q], pw_re[q]], axis=-1)[:, :, None, :]
    ai = jnp.stack([jnp.concatenate([-pw_im[q], pw_im[q]], axis=-1),
                    jnp.concatenate([pw_im[q], -pw_im[q]], axis=-1)], axis=1)[:, :, :, None, :]
    dsk = jnp.tile(d_skip.reshape(S5_GROUPS, 1, S5_CH), (1, 1, q))
    return (jnp.stack(ms), jnp.stack(ps).astype(BF16), jnp.stack(gs).astype(BF16),
            ar.astype(F32), ai.astype(F32), dsk.astype(F32))


def _s5(ug, params, nb, s_len, lc):
    m, p, g, ar, ai, dsk = params
    nc_tot = s_len // S5_Q
    nc_ctx = lc // S5_Q
    r = nb * nc_tot
    return pl.pallas_call(
        functools.partial(_s5_kernel, nb=nb, nc_ctx=nc_ctx, nc_tot=nc_tot),
        out_shape=jax.ShapeDtypeStruct((S5_GROUPS, r, S5_BLK), F32),
        grid=(S5_GROUPS,),
        in_specs=[pl.BlockSpec((1, r, S5_BLK), lambda gi: (gi, 0, 0)),
                  pl.BlockSpec((2, 1, S5_CH, 2 * S5_BLK), lambda gi: (0, gi, 0, 0)),
                  pl.BlockSpec((2, 2, 1, S5_BLK, 2 * S5_STATE), lambda gi: (0, 0, gi, 0, 0)),
                  pl.BlockSpec((2, 1, 2 * S5_STATE, S5_BLK), lambda gi: (0, gi, 0, 0)),
                  pl.BlockSpec((2, 1, 1, 2 * S5_STATE), lambda gi: (0, gi, 0, 0)),
                  pl.BlockSpec((2, 2, 1, 1, 2 * S5_STATE), lambda gi: (0, 0, gi, 0, 0)),
                  pl.BlockSpec((1, 1, S5_BLK), lambda gi: (gi, 0, 0))],
        out_specs=pl.BlockSpec((1, r, S5_BLK), lambda gi: (gi, 0, 0)),
        scratch_shapes=[pltpu.VMEM((2, 2, r, 2 * S5_STATE), F32), pltpu.VMEM((2, r, 2 * S5_STATE), F32),
                        pltpu.VMEM((2, S5_BLK, S5_BLK), BF16)],
        compiler_params=_cp(("parallel",)),
        name="s5_scan",
    )(ug, m, p, g, ar, ai, dsk)


CONV_ROWS = 2 * TM


def _conv_kernel(x_ref, prev_ref, next_ref, w_ref, b_ref, o_ref, *, s_len, lc):
    x = x_ref[...]
    rows = x.shape[0]
    ridx = lax.broadcasted_iota(jnp.int32, x.shape, 0)
    pos = (pl.program_id(0) * rows) % s_len + ridx
    pos = jnp.where(pos >= s_len, pos - s_len, pos)
    seg_first = jnp.logical_or(pos == 0, pos == lc)
    seg_last = jnp.logical_or(pos == lc - 1, pos == s_len - 1)
    xm = jnp.where(ridx == 0, prev_ref[SUBLANES - 1:SUBLANES, :], pltpu.roll(x, 1, 0))
    xp = jnp.where(ridx == rows - 1, next_ref[0:1, :], pltpu.roll(x, rows - 1, 0))
    xm = jnp.where(seg_first, 0.0, xm)
    xp = jnp.where(seg_last, 0.0, xp)
    y = xm * w_ref[0:1, :] + x * w_ref[1:2, :] + xp * w_ref[2:3, :] + b_ref[...]
    o_ref[...] = _silu(y)


def _conv(xbc, w, b, s_len, lc):
    t, c = xbc.shape
    per = CONV_ROWS // SUBLANES
    last = t // SUBLANES - 1
    return pl.pallas_call(
        functools.partial(_conv_kernel, s_len=s_len, lc=lc),
        out_shape=jax.ShapeDtypeStruct((t, c), F32),
        grid=(t // CONV_ROWS,),
        in_specs=[pl.BlockSpec((CONV_ROWS, c), lambda i: (i, 0)),
                  pl.BlockSpec((SUBLANES, c), lambda i: (jnp.maximum(i * per - 1, 0), 0)),
                  pl.BlockSpec((SUBLANES, c), lambda i: (jnp.minimum((i + 1) * per, last), 0)),
                  pl.BlockSpec((3, c), lambda i: (0, 0)),
                  pl.BlockSpec((1, c), lambda i: (0, 0))],
        out_specs=pl.BlockSpec((CONV_ROWS, c), lambda i: (i, 0)),
        compiler_params=_cp(("parallel",)),
        name="ssd_conv",
    )(xbc, xbc, xbc, w, b.reshape(1, c))


_X_B = GROUP_W
_X_C = GROUP_W + SSD_NGROUPS * SSD_STATE


def _ssd_kernel(xf_ref, dtf_ref, dttf_ref, xr_ref, dtr_ref, dttr_ref, bias_ref, a_ref, biast_ref, at_ref, dsk_ref,
                yf_ref, yr_ref, stf_ref, str_ref):
    @pl.when(pl.program_id(1) == 0)
    def _():
        stf_ref[...] = jnp.zeros_like(stf_ref)
        str_ref[...] = jnp.zeros_like(str_ref)

    par = (bias_ref[...], a_ref[...], biast_ref[...], at_ref[...], dsk_ref[...])
    for j in range(SSD_SUB):
        rf = slice(j * TQ, (j + 1) * TQ)
        yf_ref[rf, :] = _ssd_chunk_step(xf_ref[rf, :], dtf_ref[rf, :], dttf_ref[0, :, rf], par, stf_ref, False)
        rr = slice((SSD_SUB - 1 - j) * TQ, (SSD_SUB - j) * TQ)
        yr_ref[rr, :] = _ssd_chunk_step(xr_ref[rr, :], dtr_ref[rr, :], dttr_ref[0, :, rr], par, str_ref, True)


def _ssd_chunk_step(xc, dt_raw, dtt_raw, par, st_ref, rev):
    bias, a_vec, biast, at_vec, dsk = par
    base = SSD_HEADS if rev else 0
    x = xc[:, 0:GROUP_W]
    dt = _softplus(dt_raw + bias)
    a = dt * a_vec
    dtt = _softplus(dtt_raw + biast)
    at = dtt * at_vec
    ri = lax.broadcasted_iota(jnp.int32, (TQ, TQ), 0)
    ci = lax.broadcasted_iota(jnp.int32, (TQ, TQ), 1)
    causal = (ci >= ri) if rev else (ri >= ci)
    tri = jnp.where(causal, 1.0, 0.0)
    cum_c = _dot_hi(tri, a)
    cum_r = _dot_nt_hi(at, tri)
    edge = 0 if rev else TQ - 1
    tot = cum_c[edge:edge + 1, :]

    shape = (TQ, GROUP_W)
    xdt = x * _per_head_cols(dt, base, SSD_HEADS, shape)
    lane = lax.broadcasted_iota(jnp.int32, shape, 1)
    y = jnp.zeros(shape, F32)
    bmat = [xc[:, _X_B + g * SSD_STATE:_X_B + (g + 1) * SSD_STATE].astype(BF16) for g in range(SSD_NGROUPS)]
    cmat = [xc[:, _X_C + g * SSD_STATE:_X_C + (g + 1) * SSD_STATE].astype(BF16) for g in range(SSD_NGROUPS)]
    cb = [_dot_nt(cmat[g], bmat[g]) for g in range(SSD_NGROUPS)]
    for h in range(SSD_HEADS):
        col = base + h
        seg = jnp.where(causal, cum_c[:, col:col + 1] - cum_r[col:col + 1, :], NEG_INF)
        scores = cb[h // 2] * jnp.exp(seg)
        xh = jnp.where((lane >= h * HEAD_DIM) & (lane < (h + 1) * HEAD_DIM), xdt, 0.0)
        y = y + _dot(scores.astype(BF16), xh.astype(BF16))
    st = st_ref[...]
    yo = jnp.concatenate(
        [_dot_nt(cmat[g], st[g * SSD_STATE:(g + 1) * SSD_STATE].astype(BF16)) for g in range(SSD_NGROUPS)], axis=1)
    y = y + yo * _per_head_cols(jnp.exp(cum_c), base, SSD_HEADS, shape)
    if not rev:
        y = y + x * dsk
    xd = xdt * _per_head_cols(jnp.exp(tot - cum_c), base, SSD_HEADS, shape)
    xdt_t = xd.T.astype(BF16)
    decay = jnp.exp(tot)
    for g in range(SSD_NGROUPS):
        new = _dot(xdt_t[g * SSD_STATE:(g + 1) * SSD_STATE], bmat[g])
        for hh in range(2):
            h = 2 * g + hh
            r0 = h * HEAD_DIM
            st_ref[r0:r0 + HEAD_DIM, :] = (decay[:, base + h:base + h + 1] * st[r0:r0 + HEAD_DIM]
                                           + new[hh * HEAD_DIM:(hh + 1) * HEAD_DIM])
    return y


def _dot_nt_hi(a, b):
    return lax.dot_general(a, b, (((1,), (1,)), ((), ())), preferred_element_type=F32, precision=HI)


def _ssd_chunk(c, rev, nc_ctx, nc_tot):
    if not rev:
        return c
    return jnp.where(c < nc_ctx, nc_ctx - 1 - c, nc_tot - 1 - (c - nc_ctx))


SSD_SUB = TM // TQ


def _ssd_scan(xc, dt, dtt, bias, a, biast, at, dsk, nb, s_len, lc):
    t = xc.shape[0]
    nblk = s_len // TM
    nctx = lc // TM
    fix = lambda b, c: (0, 0)

    def rows(rev):
        return lambda b, c: (b * nblk + _ssd_chunk(c, rev, nctx, nblk), 0)

    def lanes(rev):
        return lambda b, c: (b, 0, _ssd_chunk(c, rev, nctx, nblk))

    def data_specs(rev):
        return [pl.BlockSpec((TM, SSD_XBC), rows(rev)), pl.BlockSpec((TM, LANES), rows(rev)),
                pl.BlockSpec((1, SUBLANES, TM), lanes(rev))]

    state = pltpu.VMEM((SSD_HEADS * HEAD_DIM, SSD_STATE), F32)
    return pl.pallas_call(
        _ssd_kernel,
        out_shape=[jax.ShapeDtypeStruct((t, GROUP_W), F32)] * 2,
        grid=(nb, nblk),
        in_specs=data_specs(False) + data_specs(True) + [
            pl.BlockSpec((1, LANES), fix), pl.BlockSpec((1, LANES), fix),
            pl.BlockSpec((SUBLANES, TQ), fix), pl.BlockSpec((SUBLANES, TQ), fix),
            pl.BlockSpec((1, GROUP_W), fix)],
        out_specs=[pl.BlockSpec((TM, GROUP_W), rows(False)), pl.BlockSpec((TM, GROUP_W), rows(True))],
        scratch_shapes=[state, state],
        compiler_params=_cp(("parallel", "arbitrary")),
        name="ssd_scan",
    )(xc, dt, dtt, xc, dt, dtt, bias, a, biast, at, dsk)


def _ssd(xbc, dt, conv_w, conv_b, dt_bias, a_log, d_skip, nb, s_len, lc):
    xc = _conv(xbc, conv_w, conv_b, s_len, lc)
    nd = 2 * SSD_HEADS
    dtt = dt[:, :nd].reshape(nb, s_len, nd).transpose(0, 2, 1)
    bias = jnp.pad(dt_bias.reshape(1, nd), ((0, 0), (0, LANES - nd)))
    a = jnp.pad(-jnp.exp(a_log).reshape(1, nd), ((0, 0), (0, LANES - nd)))
    biast = jnp.broadcast_to(dt_bias.reshape(nd, 1), (nd, TQ))
    at = jnp.broadcast_to(-jnp.exp(a_log).reshape(nd, 1), (nd, TQ))
    dsk = jnp.repeat(d_skip, HEAD_DIM).reshape(1, GROUP_W)
    return _ssd_scan(xc, dt, dtt, bias, a, biast, at, dsk, nb, s_len, lc)


def _outproj_kernel(x_ref, ys5_ref, oga_ref, y0_ref, y1_ref, z_ref, owa_ref, *refs):
    mods, shared = refs[:ROW_SUB], refs[ROW_SUB:]
    for s in range(ROW_SUB):
        rows = _row_views((x_ref, oga_ref, y0_ref, y1_ref, z_ref, owa_ref) + tuple(shared[-4:-1]), s)
        _outproj_block(rows[0], ys5_ref.at[:, pl.ds(s * S5_TB, S5_TB), :], *rows[1:6], mods[s], *shared[:-4],
                       *rows[6:], shared[-1])


def _outproj_block(x_ref, ys5_ref, oga_ref, y0_ref, y1_ref, z_ref, owa_ref, mod_ref, gluw_ref, glub_ref,
                   ng_ref, wout_ref, n2_ref, wr_ref, br_ref, xn_o, h2_o, route_o, y_scr):
    _s5_unpack_kernel(ys5_ref, y_scr)
    gl = _gelu_tanh(jnp.concatenate([y_scr[0], y_scr[1]], axis=1))
    a = gl * _sigmoid(_dot(gl.astype(BF16), gluw_ref[...]) + glub_ref[...])
    m = (y0_ref[...] + y1_ref[...]) * _silu(z_ref[...])
    m = m * lax.rsqrt(jnp.mean(m * m, axis=-1, keepdims=True) + EPS) * ng_ref[...]
    w = wout_ref
    mix = (_dot(a.astype(BF16), w[0:GROUP_W, :]) + _dot(oga_ref[...].astype(BF16), w[GROUP_W:2 * GROUP_W, :])
           + _dot(m.astype(BF16), w[2 * GROUP_W:3 * GROUP_W, :]) + _dot(owa_ref[...].astype(BF16), w[3 * GROUP_W:, :]))
    xn = x_ref[...] + mod_ref[0, 2:3, :] * mix
    xn_o[...] = xn
    h2 = xn * lax.rsqrt(jnp.mean(xn * xn, axis=-1, keepdims=True) + EPS) * n2_ref[...]
    h2 = h2 * (1.0 + mod_ref[0, 4:5, :]) + mod_ref[0, 3:4, :]
    h2_o[...] = _pack_bf16_pair(h2)
    h_hi = h2.astype(BF16)
    h_lo = (h2 - h_hi.astype(F32)).astype(BF16)
    logits = _dot(h_hi, wr_ref[0]) + (_dot(h_lo, wr_ref[0]) + _dot(h_hi, wr_ref[1])) + br_ref[...]
    lane = lax.broadcasted_iota(jnp.int32, logits.shape, 1).astype(F32)
    big = float(4 * LANES)
    lcoarse = jnp.where(lane < MOE_GROUPS, logits, NEG_INF)
    mx = jnp.max(lcoarse, axis=1, keepdims=True)
    den = jnp.sum(jnp.exp(lcoarse - mx), axis=1, keepdims=True)
    grp = jnp.min(jnp.where(lcoarse == mx, lane, big), axis=1, keepdims=True)
    pg = 1.0 / den
    lo = ROUTE_FINE0 + grp * MOE_PER_GROUP
    lf = jnp.where(lane >= lo, jnp.where(lane < lo + MOE_PER_GROUP, logits, NEG_INF), NEG_INF)
    v1 = jnp.max(lf, axis=1, keepdims=True)
    i1 = jnp.min(jnp.where(lf == v1, lane, big), axis=1, keepdims=True)
    lf2 = jnp.where(lane == i1, NEG_INF, lf)
    v2 = jnp.max(lf2, axis=1, keepdims=True)
    i2 = jnp.min(jnp.where(lf2 == v2, lane, big), axis=1, keepdims=True)
    e2 = jnp.exp(v2 - v1)
    w1 = pg / (1.0 + e2)
    w2 = w1 * e2
    route = jnp.where(lane == 0, i1 - ROUTE_FINE0,
                      jnp.where(lane == 1, i2 - ROUTE_FINE0,
                                jnp.where(lane == 2, w1, jnp.where(lane == 3, w2, 0.0))))
    route_o[...] = route


def _outproj(x, ys5, oga, y0, y1, z, owa, mod, glu_w, glu_b, ssd_norm_g, w_out, norm2_g, wr, br, nb, nblk):
    t, d = x.shape
    row = lambda i: (i, 0)
    fix = lambda i: (0, 0)
    step = ROW_SUB * TM
    gw = pl.BlockSpec((step, GROUP_W), row)
    wr_hi = wr.astype(BF16)
    mod_specs = [pl.BlockSpec((1, 6, d), lambda i, s=s: (_mod_row(ROW_SUB * i + s, nblk, nb), 0, 0))
                 for s in range(ROW_SUB)]
    return pl.pallas_call(
        _outproj_kernel,
        out_shape=[jax.ShapeDtypeStruct((t, d), F32), jax.ShapeDtypeStruct((t, d // 2), jnp.uint32),
                   jax.ShapeDtypeStruct((t, LANES), F32)],
        grid=(t // step,),
        in_specs=[pl.BlockSpec((step, d), row),
                  pl.BlockSpec((S5_GROUPS, ROW_SUB * S5_TB, S5_BLK), lambda i: (0, i, 0)),
                  gw, gw, gw, gw, gw] + mod_specs + [
                  pl.BlockSpec((GROUP_W, GROUP_W), fix),
                  pl.BlockSpec((1, GROUP_W), fix),
                  pl.BlockSpec((1, GROUP_W), fix),
                  pl.BlockSpec((d, d), fix),
                  pl.BlockSpec((1, d), fix),
                  pl.BlockSpec((2, d, LANES), lambda i: (0, 0, 0)),
                  pl.BlockSpec((1, LANES), fix)],
        out_specs=[pl.BlockSpec((step, d), row), pl.BlockSpec((step, d // 2), row), pl.BlockSpec((step, LANES), row)],
        scratch_shapes=[pltpu.VMEM((GROUP_W // LANES, TM, LANES), F32)],
        compiler_params=_cp(("parallel",)),
        name="out_proj_router",
    )(x, ys5, oga, y0, y1, z, owa, *([mod] * ROW_SUB), glu_w.astype(BF16), glu_b.reshape(1, -1), ssd_norm_g.reshape(1, -1),
      w_out.astype(BF16), norm2_g.reshape(1, -1), jnp.stack([wr_hi, (wr - wr_hi.astype(F32)).astype(BF16)]), br)


def _pack_router(coarse_w, coarse_b, fine_w, fine_b):
    def lanes(coarse, fine):
        gap = jnp.zeros(coarse.shape[:-1] + (ROUTE_FINE0 - MOE_GROUPS,), F32)
        tail = jnp.zeros(coarse.shape[:-1] + (LANES - ROUTE_FINE0 - N_EXPERTS,), F32)
        return jnp.concatenate([coarse, gap, fine, tail], axis=-1)

    return lanes(coarse_w, fine_w), lanes(coarse_b[None, :], fine_b[None, :])


def _gather_rows(src, idx):
    m = idx.shape[0]
    d = src.shape[1]
    workers = SC_CORES * SC_SUBCORES
    k = SC_FETCH_K
    nch = m // (workers * k)
    assert nch * workers * k == m
    mesh = plsc.VectorSubcoreMesh(core_axis_name="c", subcore_axis_name="s")

    @functools.partial(
        pl.kernel, mesh=mesh,
        out_type=jax.ShapeDtypeStruct((m, d), src.dtype),
        scratch_types=[pltpu.VMEM((nch, k), jnp.int32),
                       pltpu.VMEM((k, d), src.dtype),
                       pltpu.SemaphoreType.DMA],
    )
    def gather(src_hbm, idx_hbm, out_hbm, idx_v, rows_v, sem):
        wid = lax.axis_index("s") * SC_CORES + lax.axis_index("c")
        pltpu.sync_copy(idx_hbm.at[wid], idx_v)

        @pl.loop(0, nch)
        def _(j):
            off = pl.multiple_of((wid * nch + j) * k, k)
            pltpu.async_copy(src_hbm.at[idx_v.at[j]], rows_v, sem).wait()
            pltpu.sync_copy(rows_v, out_hbm.at[pl.ds(off, k)])

    return gather(src, idx.reshape(workers, nch, k))


def _scatter_rows(src, dst0, dst1, nrows):
    t, d = src.shape
    workers = SC_CORES * SC_SUBCORES
    nch = t // (workers * SC_GATHER_K)
    assert nch * workers * SC_GATHER_K == t
    mesh = plsc.VectorSubcoreMesh(core_axis_name="c", subcore_axis_name="s")

    @functools.partial(
        pl.kernel, mesh=mesh,
        out_type=jax.ShapeDtypeStruct((nrows, d), src.dtype),
        scratch_types=[pltpu.VMEM((nch, SC_GATHER_K), jnp.int32),
                       pltpu.VMEM((nch, SC_GATHER_K), jnp.int32),
                       pltpu.VMEM((SC_GATHER_K, d), src.dtype),
                       pltpu.SemaphoreType.DMA((2,))],
    )
    def scatter(src_hbm, d0_hbm, d1_hbm, out_hbm, i0_v, i1_v, rows_v, sem):
        wid = lax.axis_index("s") * SC_CORES + lax.axis_index("c")
        pltpu.sync_copy(d0_hbm.at[wid], i0_v)
        pltpu.sync_copy(d1_hbm.at[wid], i1_v)

        @pl.loop(0, nch)
        def _(j):
            off = pl.multiple_of((wid * nch + j) * SC_GATHER_K, SC_GATHER_K)
            pltpu.sync_copy(src_hbm.at[pl.ds(off, SC_GATHER_K)], rows_v)
            first = pltpu.async_copy(rows_v, out_hbm.at[i0_v.at[j]], sem.at[0])
            second = pltpu.async_copy(rows_v, out_hbm.at[i1_v.at[j]], sem.at[1])
            first.wait()
            second.wait()

    return scatter(src, dst0.reshape(workers, nch, SC_GATHER_K), dst1.reshape(workers, nch, SC_GATHER_K))


def _expert_kernel(be_ref, nused_ref, nvalid_ref, nxt_ref, slot_ref, x_ref, wg_hbm, wu_hbm, wd_hbm, o_ref,
                   wg_f, wu_f, wd_f, wg_s, wu_s, wd_s, sem, *, layer):
    i = pl.program_id(0)
    used = i < nused_ref[0]
    new_expert = jnp.logical_or(i == 0, be_ref[i] != be_ref[jnp.maximum(i - 1, 0)])

    def weight_copies(expert, slot):
        return [pltpu.make_async_copy(w.at[layer, expert], f.at[slot], sem.at[slot, j])
                for j, (w, f) in enumerate(((wg_hbm, wg_f), (wu_hbm, wu_f), (wd_hbm, wd_f)))]

    @pl.when(jnp.logical_and(used, new_expert))
    def _():
        slot = slot_ref[i]

        @pl.when(i == 0)
        def _():
            for c in weight_copies(be_ref[i], slot):
                c.start()

        for c in weight_copies(be_ref[i], slot):
            c.wait()
        wg_s[...] = wg_f[slot].astype(BF16)
        wu_s[...] = wu_f[slot].astype(BF16)
        wd_s[...] = wd_f[slot].astype(BF16)

        @pl.when(nxt_ref[i] >= 0)
        def _():
            for c in weight_copies(nxt_ref[i], 1 - slot):
                c.start()

    def swiglu(rows):
        row = rows.start + lax.broadcasted_iota(jnp.int32, (rows.stop - rows.start, x_ref.shape[1]), 0)
        lo, hi = _unpack_bf16_pair(jnp.where(row < nvalid_ref[i], x_ref[rows, :], jnp.uint32(0)))
        lo = lo.astype(BF16)
        hi = hi.astype(BF16)
        half = lo.shape[1]
        gate = _dot(lo, wg_s[0:half, :]) + _dot(hi, wg_s[half:, :])
        up = _dot(lo, wu_s[0:half, :]) + _dot(hi, wu_s[half:, :])
        o_ref[rows, :] = _pack_bf16_pair(_dot((_silu(gate) * up).astype(BF16), wd_s[...]))

    used = i < nused_ref[0]
    half_rows = MOE_TM // 2

    @pl.when(jnp.logical_and(used, nvalid_ref[i] > half_rows))
    def _():
        swiglu(slice(0, MOE_TM))

    @pl.when(jnp.logical_and(used, nvalid_ref[i] <= half_rows))
    def _():
        swiglu(slice(0, half_rows))
        o_ref[half_rows:, :] = jnp.zeros((MOE_TM - half_rows, o_ref.shape[1]), o_ref.dtype)

    @pl.when(jnp.logical_not(used))
    def _():
        o_ref[...] = jnp.zeros_like(o_ref)


def _experts(xs, blk_e, n_used, n_valid, nxt_e, slot, wg, wu, wd, layer):
    rows, dp = xs.shape
    d = 2 * dp
    nblocks = rows // MOE_TM
    de = wg.shape[3]
    blk = lambda i, *_: (i, 0)
    hbm = pl.BlockSpec(memory_space=pl.ANY)
    grid_spec = pltpu.PrefetchScalarGridSpec(
        num_scalar_prefetch=5,
        grid=(nblocks,),
        in_specs=[pl.BlockSpec((MOE_TM, dp), blk), hbm, hbm, hbm],
        out_specs=pl.BlockSpec((MOE_TM, dp), blk),
        scratch_shapes=[pltpu.VMEM((2, d, de), F32), pltpu.VMEM((2, d, de), F32), pltpu.VMEM((2, de, d), F32),
                        pltpu.VMEM((d, de), BF16), pltpu.VMEM((d, de), BF16), pltpu.VMEM((de, d), BF16),
                        pltpu.SemaphoreType.DMA((2, 3))],
    )
    return pl.pallas_call(
        functools.partial(_expert_kernel, layer=layer),
        out_shape=jax.ShapeDtypeStruct((rows, dp), jnp.uint32),
        grid_spec=grid_spec,
        compiler_params=_cp(("arbitrary",)),
        name="moe_experts",
    )(blk_e, n_used, n_valid, nxt_e, slot, xs, wg, wu, wd)


def _final_kernel(*refs):
    fg_ref, o_ref = refs[-2:]
    for s in range(ROW_SUB):
        y = _moe_residual(*refs[5 * s:5 * s + 5])
        o_ref[s * TM:(s + 1) * TM, :] = y * lax.rsqrt(jnp.mean(y * y, axis=-1, keepdims=True) + EPS) * fg_ref[...]


def _final(xn, rows2, route, mod, final_g, nb, nblk):
    t, d = xn.shape
    nlat = nblk - 1
    assert (nb * nlat) % ROW_SUB == 0

    def blk_specs(s):
        lat = lambda i: ROW_SUB * i + s
        src = lambda i: ((lat(i) // nlat) * nblk + 1 + lat(i) % nlat, 0)
        return [pl.BlockSpec((TM, d), src),
                pl.BlockSpec((TM, d // 2), src),
                pl.BlockSpec((TM, d // 2), lambda i: (src(i)[0] + t // TM, 0)),
                pl.BlockSpec((TM, LANES), src),
                pl.BlockSpec((1, 6, d), lambda i: (lat(i) // nlat, 0, 0))]

    return pl.pallas_call(
        _final_kernel,
        out_shape=jax.ShapeDtypeStruct((nb * nlat * TM, d), F32),
        grid=(nb * nlat // ROW_SUB,),
        in_specs=[sp for s in range(ROW_SUB) for sp in blk_specs(s)] + [pl.BlockSpec((1, d), lambda i: (0, 0))],
        out_specs=pl.BlockSpec((ROW_SUB * TM, d), lambda i: (i, 0)),
        compiler_params=_cp(("parallel",)),
        name="moe_combine_final",
    )(*((xn, rows2, rows2, route, mod) * ROW_SUB), final_g.reshape(1, d))


def _moe(h2, route, wg, wu, wd, layer):
    t, d = h2.shape
    n_slots = 2 * t
    experts = jnp.arange(N_EXPERTS, dtype=F32)[None, :]
    oh0 = (route[:, 0:1] == experts).astype(F32)
    oh1 = (route[:, 1:2] == experts).astype(F32)
    both = (oh0 + oh1).reshape(t // LANES, LANES, N_EXPERTS)
    tri = jnp.tril(jnp.ones((LANES, LANES), F32))
    intra = jnp.einsum("ij,bjk->bik", tri, both)
    blk_tot = intra[:, -1, :]
    blk_cum = jnp.cumsum(blk_tot, axis=0)
    earlier = (intra - both + (blk_cum - blk_tot)[:, None, :]).reshape(t, N_EXPERTS)
    counts = blk_cum[-1].astype(jnp.int32)
    pcounts = (counts + MOE_TM - 1) // MOE_TM * MOE_TM
    pends = jnp.cumsum(pcounts)
    pstarts = pends - pcounts
    base = pstarts.astype(F32)[None, :] + earlier
    dest0 = jnp.sum(oh0 * base, axis=1).astype(jnp.int32)
    dest1 = jnp.sum(oh1 * base, axis=1).astype(jnp.int32)
    nblocks = -(-n_slots // MOE_TM) + N_EXPERTS
    nrows = -(-nblocks * MOE_TM // GATHER_ROWS) * GATHER_ROWS
    nblocks = nrows // MOE_TM
    blk_start = jnp.arange(nblocks, dtype=jnp.int32) * MOE_TM
    blk_e = jnp.minimum(jnp.sum((pends[None, :] <= blk_start[:, None]).astype(jnp.int32), axis=1), N_EXPERTS - 1)
    n_used = (pends[-1] // MOE_TM).astype(jnp.int32).reshape(1)
    n_valid = jnp.clip((pstarts + counts)[blk_e] - blk_start, 0, MOE_TM).astype(jnp.int32)
    ids = jnp.arange(N_EXPERTS, dtype=jnp.int32)
    has = counts > 0
    later = lax.cummin(jnp.where(has, ids, N_EXPERTS)[::-1])[::-1]
    nxt = jnp.concatenate([later[1:], jnp.full((1,), N_EXPERTS, jnp.int32)])
    nxt = jnp.where(nxt >= N_EXPERTS, -1, nxt)
    slot = (jnp.cumsum(has.astype(jnp.int32)) - 1) % 2
    xs = _scatter_rows(h2, dest0, dest1, nrows)
    ys = _experts(xs, blk_e, n_used, n_valid, nxt[blk_e], slot[blk_e], wg, wu, wd, layer)
    return _gather_rows(ys, jnp.concatenate([dest0, dest1]))


def kernel(x, c, ctx, c_ctx, ada_w, ada_b, norm1_g, norm2_g, w_in, w_out, s5_lam_re, s5_lam_im, s5_log_dt, s5_b_re, s5_b_im, s5_c_re, s5_c_im, s5_d, s5_glu_w, s5_glu_b, ga_qn_g, ga_kn_g, ssd_conv_w, ssd_conv_b, ssd_dt_bias, ssd_a_log, ssd_d, ssd_norm_g, wa_sink, moe_coarse_w, moe_coarse_b, moe_fine_w, moe_fine_b, moe_w_gate, moe_w_up, moe_w_down, final_g):
    nb, l, d = x.shape
    lc = ctx.shape[1]
    depth = ada_w.shape[0]
    assert lc == TM and l % TM == 0 and nb <= SUBLANES - 1 and d == D_MODEL
    s_len = lc + l
    nblk = s_len // TM
    t = nb * s_len

    cc = jnp.zeros((SUBLANES, d), F32).at[:nb].set(c).at[nb].set(c_ctx)
    mods = _ada(cc, ada_w, ada_b).reshape(depth, SUBLANES, 6, d)
    cos_t, sin_t = _rope_tables(lc, l)
    w_packed = jax.vmap(_pack_w_in)(w_in)
    s5_tabs = jax.vmap(_s5_params)(s5_lam_re, s5_lam_im, s5_log_dt, s5_b_re, s5_b_im, s5_c_re, s5_c_im, s5_d)
    wrs, brs = jax.vmap(_pack_router)(moe_coarse_w, moe_coarse_b, moe_fine_w, moe_fine_b)

    src = ("first", x.reshape(nb * l, d), ctx.reshape(nb * lc, d))
    for i in range(depth):
        mod = mods[i]
        (xm, xbc, ug, z, dt, gaq, gak, gav, waq, wak, wav) = _inproj(
            src, mod, norm1_g[i], w_packed[i], cos_t, sin_t, ga_qn_g[i], ga_kn_g[i], nb, nblk)
        ys5 = _s5(ug, tuple(tab[i] for tab in s5_tabs), nb, s_len, lc)
        oga, owa = _attn(wa_sink[i], gaq, gak, gav, waq, wak, wav, nb, s_len, lc)
        y0, y1 = _ssd(xbc, dt, ssd_conv_w[i], ssd_conv_b[i], ssd_dt_bias[i], ssd_a_log[i], ssd_d[i], nb, s_len, lc)
        wr, br = wrs[i], brs[i]
        xn, h2, route = _outproj(xm, ys5, oga, y0, y1, z, owa, mod, s5_glu_w[i], s5_glu_b[i], ssd_norm_g[i],
                                 w_out[i], norm2_g[i], wr, br, nb, nblk)
        rows2 = _moe(h2, route, moe_w_gate, moe_w_up, moe_w_down, i)
        src = ("moe", xn, rows2, route, mod)
    return _final(xn, rows2, route, mod, final_g, nb, nblk).reshape(nb, l, d)
```

```python
import functools
import math

import jax
import jax.numpy as jnp
import numpy as np
from jax import lax
from jax.experimental import pallas as pl
from jax.experimental.pallas import tpu as pltpu
from jax.experimental.pallas import tpu_sc as plsc

F32 = jnp.float32
BF16 = jnp.bfloat16
HI = lax.Precision.HIGHEST

D_MODEL = 1024
GRID_W = 64
GROUP_W = 256
HEAD_DIM = 64
ROPE_FREQS = HEAD_DIM // 4
ROPE_BASE = 10000.0
EPS = 1e-6
S5_CH = 16
S5_GROUPS = GROUP_W // S5_CH
S5_STATE = 64
N_HEADS = 4
SSD_HEADS = 4
SSD_NGROUPS = 2
SSD_STATE = 128
SSD_XBC = GROUP_W + 2 * SSD_NGROUPS * SSD_STATE
WINDOW = 128
MOE_GROUPS = 4
MOE_PER_GROUP = 8
N_EXPERTS = 32
D_EXPERT = D_MODEL // 2

LANES = 128
SUBLANES = 8
TM = 256
TQ = 128
GA_TQ = 128
GA_SUB = 2
S5_Q = 32
S5_BLK = S5_Q * S5_CH
MOE_TM = 512
SC_CORES = 2
SC_SUBCORES = 16
SC_GATHER_K = 32
SC_FETCH_K = 64
GATHER_ROWS = SC_CORES * SC_SUBCORES * SC_GATHER_K
ROUTE_FINE0 = 32
VMEM_LIMIT = 56 * 1024 * 1024

NEG_INF = float("-inf")
LOG2E = math.log2(math.e)


def _cp(sem, vmem=VMEM_LIMIT):
    return pltpu.CompilerParams(dimension_semantics=sem, vmem_limit_bytes=vmem)


def _dot(a, b):
    return jnp.dot(a, b, preferred_element_type=F32)


def _dot_hi(a, b):
    return jnp.dot(a, b, preferred_element_type=F32, precision=HI)


def _dot_nt(a, b):
    return lax.dot_general(a, b, (((1,), (1,)), ((), ())), preferred_element_type=F32)


def _sigmoid(x):
    return 1.0 / (1.0 + jnp.exp(-x))


def _silu(x):
    return x * _sigmoid(x)


def _gelu_tanh(x):
    return 0.5 * x * (1.0 + jnp.tanh(math.sqrt(2.0 / math.pi) * (x + 0.044715 * (x * x * x))))


def _softplus(x):
    return jnp.maximum(x, 0.0) + jnp.log(1.0 + jnp.exp(-jnp.abs(x)))


_HI16 = 0xFFFF0000


def _pack_bf16_pair(x):
    n = x.shape[1] // 2
    bits = pltpu.bitcast(x.astype(BF16).astype(F32), jnp.uint32)
    return (bits[:, n:] & jnp.uint32(_HI16)) | (bits[:, :n] >> 16)


def _unpack_bf16_pair(w):
    return pltpu.bitcast(w << 16, F32), pltpu.bitcast(w & jnp.uint32(_HI16), F32)


def _per_head_cols(v, base, n_heads, shape):
    lane = lax.broadcasted_iota(jnp.int32, shape, 1)
    out = jnp.broadcast_to(v[:, base + n_heads - 1:base + n_heads], shape)
    for h in range(n_heads - 2, -1, -1):
        out = jnp.where(lane < (h + 1) * HEAD_DIM, v[:, base + h:base + h + 1], out)
    return out


def _ada_kernel(c_ref, w_ref, b_ref, o_ref):
    c = c_ref[...]
    o_ref[0] = _dot_hi(_silu(c), w_ref[0]) + b_ref[0]


def _ada(cc, ada_w, ada_b):
    depth, d, n = ada_w.shape
    tn = 1536
    return pl.pallas_call(
        _ada_kernel,
        out_shape=jax.ShapeDtypeStruct((depth, SUBLANES, n), F32),
        grid=(depth, n // tn),
        in_specs=[pl.BlockSpec((SUBLANES, d), lambda l, j: (0, 0)),
                  pl.BlockSpec((1, d, tn), lambda l, j: (l, 0, j)),
                  pl.BlockSpec((1, 1, tn), lambda l, j: (l, 0, j))],
        out_specs=pl.BlockSpec((1, SUBLANES, tn), lambda l, j: (l, 0, j)),
        compiler_params=_cp(("parallel", "parallel")),
        name="ada_mod",
    )(cc, ada_w, ada_b.reshape(depth, 1, n))


_C_XBC = 0
_C_U = _C_XBC + SSD_XBC
_C_Z = _C_U + GROUP_W
_C_DT = _C_Z + GROUP_W
_C_GAQ = _C_DT + LANES
_C_WAQ = _C_GAQ + N_HEADS * LANES
_C_GAK = _C_WAQ + N_HEADS * LANES
_C_GAV = _C_GAK + LANES
_C_WAK = _C_GAV + LANES
_C_WAV = _C_WAK + LANES
_C_END = _C_WAV + LANES


def _expand_q_cols(wq):
    zero = jnp.zeros((wq.shape[0], HEAD_DIM), wq.dtype)
    parts = []
    for h in range(N_HEADS):
        head = wq[:, h * HEAD_DIM:(h + 1) * HEAD_DIM]
        parts += [head, zero] if h // 2 == 0 else [zero, head]
    return jnp.concatenate(parts, axis=1)


def _pack_w_in(w_in):
    cuts = np.cumsum([256, 256, 128, 128, 256, SSD_XBC, 2 * SSD_HEADS, 256, 128, 128])[:-1]
    u, gaq, gak, gav, z, xbc, dt, waq, wak, wav = jnp.split(w_in, [int(c) for c in cuts], axis=1)
    dt = jnp.pad(dt, ((0, 0), (0, LANES - dt.shape[1])))
    w = jnp.concatenate([xbc, u, z, dt, _expand_q_cols(gaq), _expand_q_cols(waq), gak, gav, wak, wav], axis=1)
    return w.astype(BF16)


def _rope(x, cos, sins):
    w = x.shape[1]
    if w > LANES:
        cos = jnp.concatenate([cos] * (w // LANES), axis=1)
        sins = jnp.concatenate([sins] * (w // LANES), axis=1)
    lane = lax.broadcasted_iota(jnp.int32, x.shape, 1)
    up = pltpu.roll(x, w - ROPE_FREQS, 1)
    dn = pltpu.roll(x, ROPE_FREQS, 1)
    partner = jnp.where((lane & ROPE_FREQS) == 0, up, dn)
    return x * cos + partner * sins


def _v_with_ones(v):
    lo = lax.broadcasted_iota(jnp.int32, v.shape, 1) < HEAD_DIM
    return jnp.concatenate([jnp.where(lo, v, 1.0), jnp.where(lo, 1.0, v)], axis=1).astype(BF16)


def _moe_residual(xn_ref, r0_ref, r1_ref, route_ref, mod_ref):
    route = route_ref[...]
    r0 = jnp.concatenate(_unpack_bf16_pair(r0_ref[...]), axis=1)
    r1 = jnp.concatenate(_unpack_bf16_pair(r1_ref[...]), axis=1)
    return xn_ref[...] + mod_ref[0, 5:6, :] * (route[:, 2:3] * r0 + route[:, 3:4] * r1)


ROW_SUB = 2


def _row_views(refs, s):
    return [r.at[pl.ds(s * TM, TM), :] for r in refs]


def _inproj_kernel(*refs, first, nblk):
    n_blk_in = (2 if first else 5) + 3
    shared = refs[ROW_SUB * n_blk_in:]
    g_ref, w_ref, qn_ref, kn_ref = shared[:4]
    xm_o, xbc_o, ug_o = shared[4:7]
    rest_o = shared[7:-1]
    u_scr = shared[-1]
    for s in range(ROW_SUB):
        blk_refs = refs[s * n_blk_in:(s + 1) * n_blk_in]
        xm_v, xbc_v = _row_views((xm_o, xbc_o), s)
        ug_v = ug_o.at[:, pl.ds(s * S5_TB, S5_TB), :]
        _inproj_block(blk_refs, g_ref, w_ref, qn_ref, kn_ref, xm_v, xbc_v, ug_v, _row_views(rest_o, s), u_scr,
                      first, (pl.program_id(0) * ROW_SUB + s) % nblk == 0)


def _inproj_block(blk_refs, g_ref, w_ref, qn_ref, kn_ref, xm_o, xbc_o, ug_o, rest_o, u_scr, first, is_ctx):
    if first:
        lat_ref, ctx_ref = blk_refs[:2]
        x = jnp.where(is_ctx, ctx_ref[...], lat_ref[...])
    else:
        x = _moe_residual(*blk_refs[:5])
    mod_ref, cos_ref, sin_ref = blk_refs[-3:]
    z_o, dt_o, gaq_o, gak_o, gav_o, waq_o, wak_o, wav_o = rest_o
    xm_o[...] = x
    ms = jnp.mean(x * x, axis=-1, keepdims=True)
    xn = x * lax.rsqrt(ms + EPS) * g_ref[...]
    h = xn * (1.0 + mod_ref[0, 1:2, :]) + mod_ref[0, 0:1, :]
    hb = h.astype(BF16)

    def proj(lo, hi):
        return _dot(hb, w_ref[:, lo:hi])

    cos = cos_ref[...]
    sins = sin_ref[...]
    scale = LOG2E * HEAD_DIM ** -0.5
    q = proj(_C_GAQ, _C_WAQ)
    qs = q * q
    inv = jnp.concatenate(
        [jnp.broadcast_to(lax.rsqrt(jnp.sum(qs[:, s * LANES:(s + 1) * LANES], axis=1, keepdims=True)
                                    * (1.0 / HEAD_DIM) + EPS), (q.shape[0], LANES)) for s in range(N_HEADS)], axis=1)
    gaq_o[...] = (_rope(q * inv * qn_ref[...], cos, sins) * scale).astype(BF16)
    waq_o[...] = (_rope(proj(_C_WAQ, _C_GAK), cos, sins) * scale).astype(BF16)
    k = proj(_C_GAK, _C_GAV)
    ks = k * k
    lane = lax.broadcasted_iota(jnp.int32, k.shape, 1)
    lo = lane < HEAD_DIM
    ms0 = jnp.sum(jnp.where(lo, ks, 0.0), axis=1, keepdims=True)
    ms1 = jnp.sum(jnp.where(lo, 0.0, ks), axis=1, keepdims=True)
    kinv = lax.rsqrt(jnp.where(lo, ms0, ms1) * (1.0 / HEAD_DIM) + EPS)
    gak_o[...] = _rope(k * kinv * kn_ref[...], cos, sins).astype(BF16)
    gav_o[...] = _v_with_ones(proj(_C_GAV, _C_WAK))
    wak_o[...] = _rope(proj(_C_WAK, _C_WAV), cos, sins).astype(BF16)
    wav_o[...] = _v_with_ones(proj(_C_WAV, _C_END))
    xbc_o[...] = proj(_C_XBC, _C_U)
    u = proj(_C_U, _C_Z)
    u_scr[0] = u[:, :LANES]
    u_scr[1] = u[:, LANES:]
    _s5_pack_kernel(u_scr.at[0], u_scr.at[1], ug_o)
    z_o[...] = proj(_C_Z, _C_DT)
    dt_o[...] = proj(_C_DT, _C_GAQ)


def _mod_row(i, nblk, nb):
    return jnp.where(i % nblk == 0, nb, i // nblk)


def _inproj(src, mod, norm_g, w_packed, cos_t, sin_t, qn_g, kn_g, nb, nblk):
    first = src[0] == "first"
    d = src[1].shape[1]
    t = nb * nblk * TM
    row = lambda i: (i, 0)
    fix = lambda i: (0, 0)
    nsteps = t // (ROW_SUB * TM)
    assert nsteps * ROW_SUB * TM == t

    def blk_specs(s):
        bid = lambda i: ROW_SUB * i + s
        modspec = pl.BlockSpec((1, 6, d), lambda i: (_mod_row(bid(i), nblk, nb), 0, 0))
        table = pl.BlockSpec((TM, LANES), lambda i: (bid(i) % nblk, 0))
        if first:
            srcs = [pl.BlockSpec((TM, d), lambda i: ((bid(i) // nblk) * (nblk - 1) + jnp.maximum(bid(i) % nblk - 1, 0), 0)),
                    pl.BlockSpec((TM, d), lambda i: (bid(i) // nblk, 0))]
        else:
            srcs = [pl.BlockSpec((TM, d), lambda i: (bid(i), 0)), pl.BlockSpec((TM, d // 2), lambda i: (bid(i), 0)),
                    pl.BlockSpec((TM, d // 2), lambda i: (bid(i) + t // TM, 0)),
                    pl.BlockSpec((TM, LANES), lambda i: (bid(i), 0)), modspec]
        return srcs + [modspec, table, table]

    if first:
        blk_args = tuple(src[1:]) + (mod, cos_t, sin_t)
    else:
        blk_args = (src[1], src[2], src[2], src[3], src[4], mod, cos_t, sin_t)
    outs = [(d, F32), (SSD_XBC, F32), None, (GROUP_W, F32), (LANES, F32),
            (N_HEADS * LANES, BF16), (LANES, BF16), (2 * LANES, BF16),
            (N_HEADS * LANES, BF16), (LANES, BF16), (2 * LANES, BF16)]
    shapes = [jax.ShapeDtypeStruct((t, o[0]), o[1]) if o else
              jax.ShapeDtypeStruct((S5_GROUPS, t // S5_Q, S5_BLK), F32) for o in outs]
    specs = [pl.BlockSpec((ROW_SUB * TM, o[0]), row) if o else
             pl.BlockSpec((S5_GROUPS, ROW_SUB * S5_TB, S5_BLK), lambda i: (0, i, 0)) for o in outs]
    return pl.pallas_call(
        functools.partial(_inproj_kernel, first=first, nblk=nblk),
        out_shape=shapes,
        grid=(nsteps,),
        in_specs=[sp for s in range(ROW_SUB) for sp in blk_specs(s)] + [
                  pl.BlockSpec((1, d), fix),
                  pl.BlockSpec((d, _C_END), fix),
                  pl.BlockSpec((1, N_HEADS * LANES), fix),
                  pl.BlockSpec((1, LANES), fix)],
        out_specs=specs,
        scratch_shapes=[pltpu.VMEM((GROUP_W // LANES, TM, LANES), F32)],
        compiler_params=_cp(("parallel",)),
        name="in_proj",
    )(*(blk_args * ROW_SUB), norm_g.reshape(1, d), w_packed,
      jnp.tile(qn_g, 2 * N_HEADS).reshape(1, -1), jnp.tile(kn_g, 2).reshape(1, -1))


def _rope_tables(lc, l):
    n_rows = l // GRID_W
    rows = np.repeat(np.arange(n_rows), GRID_W)
    cols = np.tile(np.arange(GRID_W), n_rows)
    inv = np.power(np.float32(ROPE_BASE), -np.arange(ROPE_FREQS, dtype=np.float32) / ROPE_FREQS)
    ang = np.stack([rows, cols], axis=-1).astype(np.float32)[..., None] * inv
    cos = np.cos(ang)
    sin = np.sin(ang)
    cos64 = np.stack([cos, cos], axis=2).reshape(l, HEAD_DIM)
    sin64 = np.stack([-sin, sin], axis=2).reshape(l, HEAD_DIM)
    cos64 = np.concatenate([np.ones((lc, HEAD_DIM), np.float32), cos64], axis=0)
    sin64 = np.concatenate([np.zeros((lc, HEAD_DIM), np.float32), sin64], axis=0)
    return (jnp.asarray(np.tile(cos64, (1, 2)), dtype=F32), jnp.asarray(np.tile(sin64, (1, 2)), dtype=F32))


def _merge_heads(o2, kvh):
    tq = o2.shape[0] // 2
    oa, ob = o2[:tq], o2[tq:]
    lane = lax.broadcasted_iota(jnp.int32, oa.shape, 1)
    if kvh == 0:
        return jnp.where(lane < HEAD_DIM, oa, pltpu.roll(ob, HEAD_DIM, 1))
    return jnp.where(lane < HEAD_DIM, pltpu.roll(oa, HEAD_DIM, 1), ob)


def _stack_q(q_ref, rows, kvh):
    return jnp.concatenate([q_ref[rows, (2 * kvh) * LANES:(2 * kvh + 1) * LANES],
                            q_ref[rows, (2 * kvh + 1) * LANES:(2 * kvh + 2) * LANES]], axis=0)


def _ga_attend(q_ref, k_ref, v_ref, o_ref, nkeys):
    k = k_ref[0:nkeys, :]
    for sub in range(GA_SUB):
        rows = slice(sub * GA_TQ, (sub + 1) * GA_TQ)
        scores = [_dot_nt(_stack_q(q_ref, rows, kvh), k).astype(BF16) for kvh in range(2)]
        outs = []
        for kvh in range(2):
            s = scores[kvh]
            p = jnp.exp2(s - jnp.max(s, axis=1, keepdims=True))
            o2 = _dot(p, v_ref[0:nkeys, kvh * LANES:(kvh + 1) * LANES])
            outs.append(_merge_heads(o2 / pltpu.roll(o2, HEAD_DIM, 1), kvh))
        o_ref[rows, :] = jnp.concatenate(outs, axis=1)


def _attn_kernel(sink_ref, gq_ref, gk_ref, gv_ref, wq_ref, wk_ref, wv_ref, og_ref, ow_ref, *, lc):
    is_ctx = pl.program_id(1) < lc // TM

    @pl.when(is_ctx)
    def _():
        _ga_attend(gq_ref, gk_ref, gv_ref, og_ref, lc)
        _wa_attend(sink_ref, wq_ref, wk_ref, wv_ref, ow_ref, lc)

    @pl.when(jnp.logical_not(is_ctx))
    def _():
        _ga_attend(gq_ref, gk_ref, gv_ref, og_ref, gk_ref.shape[0])
        _wa_attend(sink_ref, wq_ref, wk_ref, wv_ref, ow_ref, lc)


def _attn(sink, gq, gk, gv, wq, wk, wv, nb, s_len, lc):
    t = gq.shape[0]
    nq = s_len // TM
    assert GA_SUB * GA_TQ == TM and WA_SUB * TQ == TM
    qspec = pl.BlockSpec((TM, N_HEADS * LANES), lambda b, j: (b * nq + j, 0))
    kspec = pl.BlockSpec((s_len, LANES), lambda b, j: (b, 0))
    vspec = pl.BlockSpec((s_len, 2 * LANES), lambda b, j: (b, 0))
    ospec = pl.BlockSpec((TM, GROUP_W), lambda b, j: (b * nq + j, 0))
    return pl.pallas_call(
        functools.partial(_attn_kernel, lc=lc),
        out_shape=[jax.ShapeDtypeStruct((t, GROUP_W), F32)] * 2,
        grid=(nb, nq),
        in_specs=[pl.BlockSpec(memory_space=pltpu.SMEM), qspec, kspec, vspec, qspec, kspec, vspec],
        out_specs=[ospec, ospec],
        compiler_params=_cp(("parallel", "arbitrary")),
        name="attention",
    )(sink, gq, gk, gv, wq, wk, wv)


WA_SUB = TM // TQ


def _wa_attend(sink_ref, q_ref, k_ref, v_ref, o_ref, lc):
    s_len = k_ref.shape[0]
    kc = k_ref[0:lc, :]
    row = lax.broadcasted_iota(jnp.int32, (2 * TQ, 1), 0)
    for sub in range(WA_SUB):
        rows = slice(sub * TQ, (sub + 1) * TQ)
        n = pl.program_id(1) * WA_SUB + sub - lc // TQ
        start = pl.multiple_of(jnp.clip(lc + (n - 1) * TQ, lc, s_len - 3 * TQ), TQ)
        kb = k_ref[pl.ds(start, 3 * TQ), :]
        qpos = n * TQ + lax.broadcasted_iota(jnp.int32, (TQ, 3 * TQ), 0)
        kpos = (start - lc) + lax.broadcasted_iota(jnp.int32, (TQ, 3 * TQ), 1)
        reach = jnp.where(n >= 0, WINDOW, -1)
        valid = jnp.abs(qpos - kpos) <= reach
        valid = jnp.concatenate([valid, valid], axis=0)
        outs = []
        for kvh in range(2):
            q2 = jnp.concatenate([q_ref[rows, (2 * kvh) * LANES:(2 * kvh + 1) * LANES],
                                  q_ref[rows, (2 * kvh + 1) * LANES:(2 * kvh + 2) * LANES]], axis=0)
            sc = _dot_nt(q2, kc)
            sb = jnp.where(valid, _dot_nt(q2, kb), NEG_INF)
            sink = jnp.where(row < TQ, sink_ref[2 * kvh], sink_ref[2 * kvh + 1]) * LOG2E
            m = jnp.maximum(jnp.maximum(jnp.max(sc, axis=1, keepdims=True), jnp.max(sb, axis=1, keepdims=True)), sink)
            pc = jnp.exp2((sc - m).astype(BF16))
            pb = jnp.exp2((sb - m).astype(BF16))
            vcols = slice(kvh * LANES, (kvh + 1) * LANES)
            o2 = _dot(pc, v_ref[0:lc, vcols]) + _dot(pb, v_ref[pl.ds(start, 3 * TQ), vcols])
            denom = pltpu.roll(o2, HEAD_DIM, 1) + jnp.exp2(sink - m)
            outs.append(_merge_heads(o2 / denom, kvh))
        o_ref[rows, :] = jnp.concatenate(outs, axis=1)


def _s5_chunk_index(t, rev, nc_ctx, nc_tot):
    if not rev:
        return t
    return jnp.where(t < nc_ctx, nc_ctx - 1 - t, nc_tot - 1 - (t - nc_ctx))


def _s5_kernel(u_ref, k_ref, p_ref, g_ref, ar_ref, ai_ref, dsk_ref, y_ref, s_scr, h_scr, m_scr, *, nb, nc_ctx, nc_tot):
    for d in range(2):
        ext = k_ref[d, 0]
        for s in range(S5_Q):
            lo = ((S5_Q - s) if d == 0 else (S5_Q - 1 - s)) * S5_CH
            win = pltpu.roll(ext, (2 * S5_BLK - lo) % (2 * S5_BLK), 1)[:, :S5_BLK]
            m_scr[d, s * S5_CH:(s + 1) * S5_CH, :] = win.astype(BF16)
    uf = u_ref[0]
    u = uf.astype(BF16)
    for d in range(2):
        for k in range(2):
            s_scr[d, k] = _dot(u, p_ref[d, k, 0])
    ar = [jnp.broadcast_to(ar_ref[d, 0], (nb, LANES)) for d in range(2)]
    ai = [[jnp.broadcast_to(ai_ref[d, k, 0], (nb, LANES)) for k in range(2)] for d in range(2)]

    def body(t, carry):
        out = []
        for d in range(2):
            h, hs = carry[d]
            rows = pl.ds(_s5_chunk_index(t, d == 1, nc_ctx, nc_tot), nb, stride=nc_tot)
            h_scr[d, rows, :] = h
            out.append((ar[d] * h + ai[d][0] * hs + s_scr[d, 0, rows, :],
                        ar[d] * hs + ai[d][1] * h + s_scr[d, 1, rows, :]))
        return tuple(out)

    zero = jnp.zeros((nb, LANES), F32)
    lax.fori_loop(0, nc_tot, body, ((zero, zero), (zero, zero)), unroll=2)
    y = uf * dsk_ref[0]
    for d in range(2):
        y = y + _dot(u, m_scr[d]) + _dot(h_scr[d].astype(BF16), g_ref[d, 0])
    y_ref[0] = y


S5_TB = TM // S5_Q
S5_GPS = LANES // S5_CH


def _s5_pack_kernel(lo_ref, hi_ref, o_ref):
    for s in range(S5_Q):
        rows = pl.ds(s, S5_TB, stride=S5_Q)
        halves = (lo_ref[rows, :], hi_ref[rows, :])
        dst = S5_CH * (s % S5_GPS)
        for g in range(S5_GROUPS):
            slab = halves[g // S5_GPS]
            src = S5_CH * (g % S5_GPS)
            moved = slab if src == dst else pltpu.roll(slab, (dst - src) % LANES, 1)
            o_ref[g, :, s * S5_CH:(s + 1) * S5_CH] = moved[:, dst:dst + S5_CH]


def _s5_unpack_kernel(y_ref, o_ref):
    lane_grp = lax.broadcasted_iota(jnp.int32, (S5_TB, LANES), 1) // S5_CH
    for s in range(S5_Q):
        src = S5_CH * (s % S5_GPS)
        for half in range(S5_GROUPS // S5_GPS):
            acc = None
            for gl in range(S5_GPS):
                slab = y_ref[half * S5_GPS + gl, :, (s // S5_GPS) * LANES:(s // S5_GPS + 1) * LANES]
                dst = S5_CH * gl
                moved = slab if src == dst else pltpu.roll(slab, (dst - src) % LANES, 1)
                acc = moved if acc is None else jnp.where(lane_grp == gl, moved, acc)
            o_ref[half, pl.ds(s, S5_TB, stride=S5_Q), :] = acc


def _s5_params(lam_re, lam_im, log_dt, b_re, b_im, c_re, c_im, d_skip):
    q = S5_Q
    dt = jnp.exp(log_dt)[..., None]
    lr, li = lam_re, lam_im
    mag = jnp.exp(lr * dt)
    a_re = mag * jnp.cos(li * dt)
    a_im = mag * jnp.sin(li * dt)
    den = lr * lr + li * li
    f_re = ((a_re - 1.0) * lr + a_im * li) / den
    f_im = (a_im * lr - (a_re - 1.0) * li) / den
    bb_re = f_re[..., None] * b_re - f_im[..., None] * b_im
    bb_im = f_re[..., None] * b_im + f_im[..., None] * b_re
    kk = jnp.arange(q + 1, dtype=F32)[:, None, None, None]
    pmag = jnp.exp(kk * (lr * dt))
    pw_re = pmag * jnp.cos(kk * (li * dt))
    pw_im = pmag * jnp.sin(kk * (li * dt))
    lw_re = pw_re[:q].transpose(1, 2, 0, 3)[:, :, :, None, :]
    lw_im = pw_im[:q].transpose(1, 2, 0, 3)[:, :, :, None, :]
    ck_re = c_re[:, :, None] * lw_re - c_im[:, :, None] * lw_im
    ck_im = c_re[:, :, None] * lw_im + c_im[:, :, None] * lw_re
    ck = jnp.concatenate([ck_re, -ck_im], axis=-1).reshape(2, S5_GROUPS, S5_BLK, 2 * S5_STATE)
    kern_t = jnp.einsum("dgmp,dgpc->dgcm", ck, jnp.concatenate([bb_re, bb_im], axis=2), precision=HI)
    kern_t = kern_t.reshape(2, S5_GROUPS, S5_CH, q, S5_CH)
    zeros = jnp.zeros_like(kern_t)
    bbt_re = bb_re.transpose(0, 1, 3, 2)[:, :, None]
    bbt_im = bb_im.transpose(0, 1, 3, 2)[:, :, None]
    ct_re = c_re.transpose(0, 1, 3, 2)[:, :, :, None, :]
    ct_im = c_im.transpose(0, 1, 3, 2)[:, :, :, None, :]
    ms, ps, gs = [], [], []
    for d in range(2):
        ext = (jnp.concatenate([zeros[d], kern_t[d]], axis=2) if d == 0
               else jnp.concatenate([kern_t[d, :, :, ::-1], zeros[d]], axis=2))
        ext = ext.reshape(S5_GROUPS, S5_CH, 2 * S5_BLK)
        ms.append(ext)
        pidx = (q - 1 - jnp.arange(q)) if d == 0 else jnp.arange(q)
        pr = pw_re[pidx, d].transpose(1, 0, 2)[:, :, None, :]
        pi = pw_im[pidx, d].transpose(1, 0, 2)[:, :, None, :]
        p_re = pr * bbt_re[d] - pi * bbt_im[d]
        p_im = pr * bbt_im[d] + pi * bbt_re[d]
        pd = jnp.stack([jnp.concatenate([p_re, p_im], axis=3), jnp.concatenate([p_im, p_re], axis=3)])
        ps.append(pd.reshape(2, S5_GROUPS, S5_BLK, 2 * S5_STATE))
        gidx = (jnp.arange(q) + 1) if d == 0 else (q - jnp.arange(q))
        gw_re = pw_re[gidx, d].transpose(1, 2, 0)[..., None]
        gw_im = pw_im[gidx, d].transpose(1, 2, 0)[..., None]
        g_re = ct_re[d] * gw_re - ct_im[d] * gw_im
        g_im = ct_re[d] * gw_im + ct_im[d] * gw_re
        gs.append(jnp.concatenate([g_re, -g_im], axis=1).reshape(S5_GROUPS, 2 * S5_STATE, S5_BLK))
    ar = jnp.concatenate([pw_re[q], pw_re[q]], axis=-1)[:, :, None, :]
    ai = jnp.stack([jnp.concatenate([-pw_im[q], pw_im[q]], axis=-1),
                    jnp.concatenate([pw_im[q], -pw_im[q]], axis=-1)], axis=1)[:, :, :, None, :]
    dsk = jnp.tile(d_skip.reshape(S5_GROUPS, 1, S5_CH), (1, 1, q))
    return (jnp.stack(ms), jnp.stack(ps).astype(BF16), jnp.stack(gs).astype(BF16),
            ar.astype(F32), ai.astype(F32), dsk.astype(F32))


def _s5(ug, params, nb, s_len, lc):
    m, p, g, ar, ai, dsk = params
    nc_tot = s_len // S5_Q
    nc_ctx = lc // S5_Q
    r = nb * nc_tot
    return pl.pallas_call(
        functools.partial(_s5_kernel, nb=nb, nc_ctx=nc_ctx, nc_tot=nc_tot),
        out_shape=jax.ShapeDtypeStruct((S5_GROUPS, r, S5_BLK), F32),
        grid=(S5_GROUPS,),
        in_specs=[pl.BlockSpec((1, r, S5_BLK), lambda gi: (gi, 0, 0)),
                  pl.BlockSpec((2, 1, S5_CH, 2 * S5_BLK), lambda gi: (0, gi, 0, 0)),
                  pl.BlockSpec((2, 2, 1, S5_BLK, 2 * S5_STATE), lambda gi: (0, 0, gi, 0, 0)),
                  pl.BlockSpec((2, 1, 2 * S5_STATE, S5_BLK), lambda gi: (0, gi, 0, 0)),
                  pl.BlockSpec((2, 1, 1, 2 * S5_STATE), lambda gi: (0, gi, 0, 0)),
                  pl.BlockSpec((2, 2, 1, 1, 2 * S5_STATE), lambda gi: (0, 0, gi, 0, 0)),
                  pl.BlockSpec((1, 1, S5_BLK), lambda gi: (gi, 0, 0))],
        out_specs=pl.BlockSpec((1, r, S5_BLK), lambda gi: (gi, 0, 0)),
        scratch_shapes=[pltpu.VMEM((2, 2, r, 2 * S5_STATE), F32), pltpu.VMEM((2, r, 2 * S5_STATE), F32),
                        pltpu.VMEM((2, S5_BLK, S5_BLK), BF16)],
        compiler_params=_cp(("parallel",)),
        name="s5_scan",
    )(ug, m, p, g, ar, ai, dsk)


CONV_ROWS = 2 * TM


def _conv_kernel(x_ref, prev_ref, next_ref, w_ref, b_ref, o_ref, *, s_len, lc):
    x = x_ref[...]
    rows = x.shape[0]
    ridx = lax.broadcasted_iota(jnp.int32, x.shape, 0)
    pos = (pl.program_id(0) * rows) % s_len + ridx
    pos = jnp.where(pos >= s_len, pos - s_len, pos)
    seg_first = jnp.logical_or(pos == 0, pos == lc)
    seg_last = jnp.logical_or(pos == lc - 1, pos == s_len - 1)
    xm = jnp.where(ridx == 0, prev_ref[SUBLANES - 1:SUBLANES, :], pltpu.roll(x, 1, 0))
    xp = jnp.where(ridx == rows - 1, next_ref[0:1, :], pltpu.roll(x, rows - 1, 0))
    xm = jnp.where(seg_first, 0.0, xm)
    xp = jnp.where(seg_last, 0.0, xp)
    y = xm * w_ref[0:1, :] + x * w_ref[1:2, :] + xp * w_ref[2:3, :] + b_ref[...]
    o_ref[...] = _silu(y)


def _conv(xbc, w, b, s_len, lc):
    t, c = xbc.shape
    per = CONV_ROWS // SUBLANES
    last = t // SUBLANES - 1
    return pl.pallas_call(
        functools.partial(_conv_kernel, s_len=s_len, lc=lc),
        out_shape=jax.ShapeDtypeStruct((t, c), F32),
        grid=(t // CONV_ROWS,),
        in_specs=[pl.BlockSpec((CONV_ROWS, c), lambda i: (i, 0)),
                  pl.BlockSpec((SUBLANES, c), lambda i: (jnp.maximum(i * per - 1, 0), 0)),
                  pl.BlockSpec((SUBLANES, c), lambda i: (jnp.minimum((i + 1) * per, last), 0)),
                  pl.BlockSpec((3, c), lambda i: (0, 0)),
                  pl.BlockSpec((1, c), lambda i: (0, 0))],
        out_specs=pl.BlockSpec((CONV_ROWS, c), lambda i: (i, 0)),
        compiler_params=_cp(("parallel",)),
        name="ssd_conv",
    )(xbc, xbc, xbc, w, b.reshape(1, c))


_X_B = GROUP_W
_X_C = GROUP_W + SSD_NGROUPS * SSD_STATE


def _ssd_kernel(xf_ref, dtf_ref, dttf_ref, xr_ref, dtr_ref, dttr_ref, bias_ref, a_ref, biast_ref, at_ref, dsk_ref,
                yf_ref, yr_ref, stf_ref, str_ref):
    @pl.when(pl.program_id(1) == 0)
    def _():
        stf_ref[...] = jnp.zeros_like(stf_ref)
        str_ref[...] = jnp.zeros_like(str_ref)

    par = (bias_ref[...], a_ref[...], biast_ref[...], at_ref[...], dsk_ref[...])
    for j in range(SSD_SUB):
        rf = slice(j * TQ, (j + 1) * TQ)
        yf_ref[rf, :] = _ssd_chunk_step(xf_ref[rf, :], dtf_ref[rf, :], dttf_ref[0, :, rf], par, stf_ref, False)
        rr = slice((SSD_SUB - 1 - j) * TQ, (SSD_SUB - j) * TQ)
        yr_ref[rr, :] = _ssd_chunk_step(xr_ref[rr, :], dtr_ref[rr, :], dttr_ref[0, :, rr], par, str_ref, True)


def _ssd_chunk_step(xc, dt_raw, dtt_raw, par, st_ref, rev):
    bias, a_vec, biast, at_vec, dsk = par
    base = SSD_HEADS if rev else 0
    x = xc[:, 0:GROUP_W]
    dt = _softplus(dt_raw + bias)
    a = dt * a_vec
    dtt = _softplus(dtt_raw + biast)
    at = dtt * at_vec
    ri = lax.broadcasted_iota(jnp.int32, (TQ, TQ), 0)
    ci = lax.broadcasted_iota(jnp.int32, (TQ, TQ), 1)
    causal = (ci >= ri) if rev else (ri >= ci)
    tri = jnp.where(causal, 1.0, 0.0)
    cum_c = _dot_hi(tri, a)
    cum_r = _dot_nt_hi(at, tri)
    edge = 0 if rev else TQ - 1
    tot = cum_c[edge:edge + 1, :]

    shape = (TQ, GROUP_W)
    xdt = x * _per_head_cols(dt, base, SSD_HEADS, shape)
    lane = lax.broadcasted_iota(jnp.int32, shape, 1)
    y = jnp.zeros(shape, F32)
    bmat = [xc[:, _X_B + g * SSD_STATE:_X_B + (g + 1) * SSD_STATE].astype(BF16) for g in range(SSD_NGROUPS)]
    cmat = [xc[:, _X_C + g * SSD_STATE:_X_C + (g + 1) * SSD_STATE].astype(BF16) for g in range(SSD_NGROUPS)]
    cb = [_dot_nt(cmat[g], bmat[g]) for g in range(SSD_NGROUPS)]
    for h in range(SSD_HEADS):
        col = base + h
        seg = jnp.where(causal, cum_c[:, col:col + 1] - cum_r[col:col + 1, :], NEG_INF)
        scores = cb[h // 2] * jnp.exp(seg)
        xh = jnp.where((lane >= h * HEAD_DIM) & (lane < (h + 1) * HEAD_DIM), xdt, 0.0)
        y = y + _dot(scores.astype(BF16), xh.astype(BF16))
    st = st_ref[...]
    yo = jnp.concatenate(
        [_dot_nt(cmat[g], st[g * SSD_STATE:(g + 1) * SSD_STATE].astype(BF16)) for g in range(SSD_NGROUPS)], axis=1)
    y = y + yo * _per_head_cols(jnp.exp(cum_c), base, SSD_HEADS, shape)
    if not rev:
        y = y + x * dsk
    xd = xdt * _per_head_cols(jnp.exp(tot - cum_c), base, SSD_HEADS, shape)
    xdt_t = xd.T.astype(BF16)
    decay = jnp.exp(tot)
    for g in range(SSD_NGROUPS):
        new = _dot(xdt_t[g * SSD_STATE:(g + 1) * SSD_STATE], bmat[g])
        for hh in range(2):
            h = 2 * g + hh
            r0 = h * HEAD_DIM
            st_ref[r0:r0 + HEAD_DIM, :] = (decay[:, base + h:base + h + 1] * st[r0:r0 + HEAD_DIM]
                                           + new[hh * HEAD_DIM:(hh + 1) * HEAD_DIM])
    return y


def _dot_nt_hi(a, b):
    return lax.dot_general(a, b, (((1,), (1,)), ((), ())), preferred_element_type=F32, precision=HI)


def _ssd_chunk(c, rev, nc_ctx, nc_tot):
    if not rev:
        return c
    return jnp.where(c < nc_ctx, nc_ctx - 1 - c, nc_tot - 1 - (c - nc_ctx))


SSD_SUB = TM // TQ


def _ssd_scan(xc, dt, dtt, bias, a, biast, at, dsk, nb, s_len, lc):
    t = xc.shape[0]
    nblk = s_len // TM
    nctx = lc // TM
    fix = lambda b, c: (0, 0)

    def rows(rev):
        return lambda b, c: (b * nblk + _ssd_chunk(c, rev, nctx, nblk), 0)

    def lanes(rev):
        return lambda b, c: (b, 0, _ssd_chunk(c, rev, nctx, nblk))

    def data_specs(rev):
        return [pl.BlockSpec((TM, SSD_XBC), rows(rev)), pl.BlockSpec((TM, LANES), rows(rev)),
                pl.BlockSpec((1, SUBLANES, TM), lanes(rev))]

    state = pltpu.VMEM((SSD_HEADS * HEAD_DIM, SSD_STATE), F32)
    return pl.pallas_call(
        _ssd_kernel,
        out_shape=[jax.ShapeDtypeStruct((t, GROUP_W), F32)] * 2,
        grid=(nb, nblk),
        in_specs=data_specs(False) + data_specs(True) + [
            pl.BlockSpec((1, LANES), fix), pl.BlockSpec((1, LANES), fix),
            pl.BlockSpec((SUBLANES, TQ), fix), pl.BlockSpec((SUBLANES, TQ), fix),
            pl.BlockSpec((1, GROUP_W), fix)],
        out_specs=[pl.BlockSpec((TM, GROUP_W), rows(False)), pl.BlockSpec((TM, GROUP_W), rows(True))],
        scratch_shapes=[state, state],
        compiler_params=_cp(("parallel", "arbitrary")),
        name="ssd_scan",
    )(xc, dt, dtt, xc, dt, dtt, bias, a, biast, at, dsk)


def _ssd(xbc, dt, conv_w, conv_b, dt_bias, a_log, d_skip, nb, s_len, lc):
    xc = _conv(xbc, conv_w, conv_b, s_len, lc)
    nd = 2 * SSD_HEADS
    dtt = dt[:, :nd].reshape(nb, s_len, nd).transpose(0, 2, 1)
    bias = jnp.pad(dt_bias.reshape(1, nd), ((0, 0), (0, LANES - nd)))
    a = jnp.pad(-jnp.exp(a_log).reshape(1, nd), ((0, 0), (0, LANES - nd)))
    biast = jnp.broadcast_to(dt_bias.reshape(nd, 1), (nd, TQ))
    at = jnp.broadcast_to(-jnp.exp(a_log).reshape(nd, 1), (nd, TQ))
    dsk = jnp.repeat(d_skip, HEAD_DIM).reshape(1, GROUP_W)
    return _ssd_scan(xc, dt, dtt, bias, a, biast, at, dsk, nb, s_len, lc)


def _outproj_kernel(x_ref, ys5_ref, oga_ref, y0_ref, y1_ref, z_ref, owa_ref, *refs):
    mods, shared = refs[:ROW_SUB], refs[ROW_SUB:]
    for s in range(ROW_SUB):
        rows = _row_views((x_ref, oga_ref, y0_ref, y1_ref, z_ref, owa_ref) + tuple(shared[-4:-1]), s)
        _outproj_block(rows[0], ys5_ref.at[:, pl.ds(s * S5_TB, S5_TB), :], *rows[1:6], mods[s], *shared[:-4],
                       *rows[6:], shared[-1])


def _outproj_block(x_ref, ys5_ref, oga_ref, y0_ref, y1_ref, z_ref, owa_ref, mod_ref, gluw_ref, glub_ref,
                   ng_ref, wout_ref, n2_ref, wr_ref, br_ref, xn_o, h2_o, route_o, y_scr):
    _s5_unpack_kernel(ys5_ref, y_scr)
    gl = _gelu_tanh(jnp.concatenate([y_scr[0], y_scr[1]], axis=1))
    a = gl * _sigmoid(_dot(gl.astype(BF16), gluw_ref[...]) + glub_ref[...])
    m = (y0_ref[...] + y1_ref[...]) * _silu(z_ref[...])
    m = m * lax.rsqrt(jnp.mean(m * m, axis=-1, keepdims=True) + EPS) * ng_ref[...]
    w = wout_ref
    mix = (_dot(a.astype(BF16), w[0:GROUP_W, :]) + _dot(oga_ref[...].astype(BF16), w[GROUP_W:2 * GROUP_W, :])
           + _dot(m.astype(BF16), w[2 * GROUP_W:3 * GROUP_W, :]) + _dot(owa_ref[...].astype(BF16), w[3 * GROUP_W:, :]))
    xn = x_ref[...] + mod_ref[0, 2:3, :] * mix
    xn_o[...] = xn
    h2 = xn * lax.rsqrt(jnp.mean(xn * xn, axis=-1, keepdims=True) + EPS) * n2_ref[...]
    h2 = h2 * (1.0 + mod_ref[0, 4:5, :]) + mod_ref[0, 3:4, :]
    h2_o[...] = _pack_bf16_pair(h2)
    h_hi = h2.astype(BF16)
    h_lo = (h2 - h_hi.astype(F32)).astype(BF16)
    logits = _dot(h_hi, wr_ref[0]) + (_dot(h_lo, wr_ref[0]) + _dot(h_hi, wr_ref[1])) + br_ref[...]
    lane = lax.broadcasted_iota(jnp.int32, logits.shape, 1).astype(F32)
    big = float(4 * LANES)
    lcoarse = jnp.where(lane < MOE_GROUPS, logits, NEG_INF)
    mx = jnp.max(lcoarse, axis=1, keepdims=True)
    den = jnp.sum(jnp.exp(lcoarse - mx), axis=1, keepdims=True)
    grp = jnp.min(jnp.where(lcoarse == mx, lane, big), axis=1, keepdims=True)
    pg = 1.0 / den
    lo = ROUTE_FINE0 + grp * MOE_PER_GROUP
    lf = jnp.where(lane >= lo, jnp.where(lane < lo + MOE_PER_GROUP, logits, NEG_INF), NEG_INF)
    v1 = jnp.max(lf, axis=1, keepdims=True)
    i1 = jnp.min(jnp.where(lf == v1, lane, big), axis=1, keepdims=True)
    lf2 = jnp.where(lane == i1, NEG_INF, lf)
    v2 = jnp.max(lf2, axis=1, keepdims=True)
    i2 = jnp.min(jnp.where(lf2 == v2, lane, big), axis=1, keepdims=True)
    e2 = jnp.exp(v2 - v1)
    w1 = pg / (1.0 + e2)
    w2 = w1 * e2
    route = jnp.where(lane == 0, i1 - ROUTE_FINE0,
                      jnp.where(lane == 1, i2 - ROUTE_FINE0,
                                jnp.where(lane == 2, w1, jnp.where(lane == 3, w2, 0.0))))
    route_o[...] = route


def _outproj(x, ys5, oga, y0, y1, z, owa, mod, glu_w, glu_b, ssd_norm_g, w_out, norm2_g, wr, br, nb, nblk):
    t, d = x.shape
    row = lambda i: (i, 0)
    fix = lambda i: (0, 0)
    step = ROW_SUB * TM
    gw = pl.BlockSpec((step, GROUP_W), row)
    wr_hi = wr.astype(BF16)
    mod_specs = [pl.BlockSpec((1, 6, d), lambda i, s=s: (_mod_row(ROW_SUB * i + s, nblk, nb), 0, 0))
                 for s in range(ROW_SUB)]
    return pl.pallas_call(
        _outproj_kernel,
        out_shape=[jax.ShapeDtypeStruct((t, d), F32), jax.ShapeDtypeStruct((t, d // 2), jnp.uint32),
                   jax.ShapeDtypeStruct((t, LANES), F32)],
        grid=(t // step,),
        in_specs=[pl.BlockSpec((step, d), row),
                  pl.BlockSpec((S5_GROUPS, ROW_SUB * S5_TB, S5_BLK), lambda i: (0, i, 0)),
                  gw, gw, gw, gw, gw] + mod_specs + [
                  pl.BlockSpec((GROUP_W, GROUP_W), fix),
                  pl.BlockSpec((1, GROUP_W), fix),
                  pl.BlockSpec((1, GROUP_W), fix),
                  pl.BlockSpec((d, d), fix),
                  pl.BlockSpec((1, d), fix),
                  pl.BlockSpec((2, d, LANES), lambda i: (0, 0, 0)),
                  pl.BlockSpec((1, LANES), fix)],
        out_specs=[pl.BlockSpec((step, d), row), pl.BlockSpec((step, d // 2), row), pl.BlockSpec((step, LANES), row)],
        scratch_shapes=[pltpu.VMEM((GROUP_W // LANES, TM, LANES), F32)],
        compiler_params=_cp(("parallel",)),
        name="out_proj_router",
    )(x, ys5, oga, y0, y1, z, owa, *([mod] * ROW_SUB), glu_w.astype(BF16), glu_b.reshape(1, -1), ssd_norm_g.reshape(1, -1),
      w_out.astype(BF16), norm2_g.reshape(1, -1), jnp.stack([wr_hi, (wr - wr_hi.astype(F32)).astype(BF16)]), br)


def _pack_router(coarse_w, coarse_b, fine_w, fine_b):
    def lanes(coarse, fine):
        gap = jnp.zeros(coarse.shape[:-1] + (ROUTE_FINE0 - MOE_GROUPS,), F32)
        tail = jnp.zeros(coarse.shape[:-1] + (LANES - ROUTE_FINE0 - N_EXPERTS,), F32)
        return jnp.concatenate([coarse, gap, fine, tail], axis=-1)

    return lanes(coarse_w, fine_w), lanes(coarse_b[None, :], fine_b[None, :])


def _gather_rows(src, idx):
    m = idx.shape[0]
    d = src.shape[1]
    workers = SC_CORES * SC_SUBCORES
    k = SC_FETCH_K
    nch = m // (workers * k)
    assert nch * workers * k == m
    mesh = plsc.VectorSubcoreMesh(core_axis_name="c", subcore_axis_name="s")

    @functools.partial(
        pl.kernel, mesh=mesh,
        out_type=jax.ShapeDtypeStruct((m, d), src.dtype),
        scratch_types=[pltpu.VMEM((nch, k), jnp.int32),
                       pltpu.VMEM((k, d), src.dtype),
                       pltpu.SemaphoreType.DMA],
    )
    def gather(src_hbm, idx_hbm, out_hbm, idx_v, rows_v, sem):
        wid = lax.axis_index("s") * SC_CORES + lax.axis_index("c")
        pltpu.sync_copy(idx_hbm.at[wid], idx_v)

        @pl.loop(0, nch)
        def _(j):
            off = pl.multiple_of((wid * nch + j) * k, k)
            pltpu.async_copy(src_hbm.at[idx_v.at[j]], rows_v, sem).wait()
            pltpu.sync_copy(rows_v, out_hbm.at[pl.ds(off, k)])

    return gather(src, idx.reshape(workers, nch, k))


def _scatter_rows(src, dst0, dst1, nrows):
    t, d = src.shape
    workers = SC_CORES * SC_SUBCORES
    nch = t // (workers * SC_GATHER_K)
    assert nch * workers * SC_GATHER_K == t
    mesh = plsc.VectorSubcoreMesh(core_axis_name="c", subcore_axis_name="s")

    @functools.partial(
        pl.kernel, mesh=mesh,
        out_type=jax.ShapeDtypeStruct((nrows, d), src.dtype),
        scratch_types=[pltpu.VMEM((nch, SC_GATHER_K), jnp.int32),
                       pltpu.VMEM((nch, SC_GATHER_K), jnp.int32),
                       pltpu.VMEM((SC_GATHER_K, d), src.dtype),
                       pltpu.SemaphoreType.DMA((2,))],
    )
    def scatter(src_hbm, d0_hbm, d1_hbm, out_hbm, i0_v, i1_v, rows_v, sem):
        wid = lax.axis_index("s") * SC_CORES + lax.axis_index("c")
        pltpu.sync_copy(d0_hbm.at[wid], i0_v)
        pltpu.sync_copy(d1_hbm.at[wid], i1_v)

        @pl.loop(0, nch)
        def _(j):
            off = pl.multiple_of((wid * nch + j) * SC_GATHER_K, SC_GATHER_K)
            pltpu.sync_copy(src_hbm.at[pl.ds(off, SC_GATHER_K)], rows_v)
            first = pltpu.async_copy(rows_v, out_hbm.at[i0_v.at[j]], sem.at[0])
            second = pltpu.async_copy(rows_v, out_hbm.at[i1_v.at[j]], sem.at[1])
            first.wait()
            second.wait()

    return scatter(src, dst0.reshape(workers, nch, SC_GATHER_K), dst1.reshape(workers, nch, SC_GATHER_K))


def _expert_kernel(be_ref, nused_ref, nvalid_ref, nxt_ref, slot_ref, x_ref, wg_hbm, wu_hbm, wd_hbm, o_ref,
                   wg_f, wu_f, wd_f, wg_s, wu_s, wd_s, sem, *, layer):
    i = pl.program_id(0)
    used = i < nused_ref[0]
    new_expert = jnp.logical_or(i == 0, be_ref[i] != be_ref[jnp.maximum(i - 1, 0)])

    def weight_copies(expert, slot):
        return [pltpu.make_async_copy(w.at[layer, expert], f.at[slot], sem.at[slot, j])
                for j, (w, f) in enumerate(((wg_hbm, wg_f), (wu_hbm, wu_f), (wd_hbm, wd_f)))]

    @pl.when(jnp.logical_and(used, new_expert))
    def _():
        slot = slot_ref[i]

        @pl.when(i == 0)
        def _():
            for c in weight_copies(be_ref[i], slot):
                c.start()

        for c in weight_copies(be_ref[i], slot):
            c.wait()
        wg_s[...] = wg_f[slot].astype(BF16)
        wu_s[...] = wu_f[slot].astype(BF16)
        wd_s[...] = wd_f[slot].astype(BF16)

        @pl.when(nxt_ref[i] >= 0)
        def _():
            for c in weight_copies(nxt_ref[i], 1 - slot):
                c.start()

    def swiglu(rows):
        row = rows.start + lax.broadcasted_iota(jnp.int32, (rows.stop - rows.start, x_ref.shape[1]), 0)
        lo, hi = _unpack_bf16_pair(jnp.where(row < nvalid_ref[i], x_ref[rows, :], jnp.uint32(0)))
        lo = lo.astype(BF16)
        hi = hi.astype(BF16)
        half = lo.shape[1]
        gate = _dot(lo, wg_s[0:half, :]) + _dot(hi, wg_s[half:, :])
        up = _dot(lo, wu_s[0:half, :]) + _dot(hi, wu_s[half:, :])
        o_ref[rows, :] = _pack_bf16_pair(_dot((_silu(gate) * up).astype(BF16), wd_s[...]))

    used = i < nused_ref[0]
    half_rows = MOE_TM // 2

    @pl.when(jnp.logical_and(used, nvalid_ref[i] > half_rows))
    def _():
        swiglu(slice(0, MOE_TM))

    @pl.when(jnp.logical_and(used, nvalid_ref[i] <= half_rows))
    def _():
        swiglu(slice(0, half_rows))
        o_ref[half_rows:, :] = jnp.zeros((MOE_TM - half_rows, o_ref.shape[1]), o_ref.dtype)

    @pl.when(jnp.logical_not(used))
    def _():
        o_ref[...] = jnp.zeros_like(o_ref)


def _experts(xs, blk_e, n_used, n_valid, nxt_e, slot, wg, wu, wd, layer):
    rows, dp = xs.shape
    d = 2 * dp
    nblocks = rows // MOE_TM
    de = wg.shape[3]
    blk = lambda i, *_: (i, 0)
    hbm = pl.BlockSpec(memory_space=pl.ANY)
    grid_spec = pltpu.PrefetchScalarGridSpec(
        num_scalar_prefetch=5,
        grid=(nblocks,),
        in_specs=[pl.BlockSpec((MOE_TM, dp), blk), hbm, hbm, hbm],
        out_specs=pl.BlockSpec((MOE_TM, dp), blk),
        scratch_shapes=[pltpu.VMEM((2, d, de), F32), pltpu.VMEM((2, d, de), F32), pltpu.VMEM((2, de, d), F32),
                        pltpu.VMEM((d, de), BF16), pltpu.VMEM((d, de), BF16), pltpu.VMEM((de, d), BF16),
                        pltpu.SemaphoreType.DMA((2, 3))],
    )
    return pl.pallas_call(
        functools.partial(_expert_kernel, layer=layer),
        out_shape=jax.ShapeDtypeStruct((rows, dp), jnp.uint32),
        grid_spec=grid_spec,
        compiler_params=_cp(("arbitrary",)),
        name="moe_experts",
    )(blk_e, n_used, n_valid, nxt_e, slot, xs, wg, wu, wd)


def _final_kernel(*refs):
    fg_ref, o_ref = refs[-2:]
    for s in range(ROW_SUB):
        y = _moe_residual(*refs[5 * s:5 * s + 5])
        o_ref[s * TM:(s + 1) * TM, :] = y * lax.rsqrt(jnp.mean(y * y, axis=-1, keepdims=True) + EPS) * fg_ref[...]


def _final(xn, rows2, route, mod, final_g, nb, nblk):
    t, d = xn.shape
    nlat = nblk - 1
    assert (nb * nlat) % ROW_SUB == 0

    def blk_specs(s):
        lat = lambda i: ROW_SUB * i + s
        src = lambda i: ((lat(i) // nlat) * nblk + 1 + lat(i) % nlat, 0)
        return [pl.BlockSpec((TM, d), src),
                pl.BlockSpec((TM, d // 2), src),
                pl.BlockSpec((TM, d // 2), lambda i: (src(i)[0] + t // TM, 0)),
                pl.BlockSpec((TM, LANES), src),
                pl.BlockSpec((1, 6, d), lambda i: (lat(i) // nlat, 0, 0))]

    return pl.pallas_call(
        _final_kernel,
        out_shape=jax.ShapeDtypeStruct((nb * nlat * TM, d), F32),
        grid=(nb * nlat // ROW_SUB,),
        in_specs=[sp for s in range(ROW_SUB) for sp in blk_specs(s)] + [pl.BlockSpec((1, d), lambda i: (0, 0))],
        out_specs=pl.BlockSpec((ROW_SUB * TM, d), lambda i: (i, 0)),
        compiler_params=_cp(("parallel",)),
        name="moe_combine_final",
    )(*((xn, rows2, rows2, route, mod) * ROW_SUB), final_g.reshape(1, d))


def _moe(h2, route, wg, wu, wd, layer):
    t, d = h2.shape
    n_slots = 2 * t
    experts = jnp.arange(N_EXPERTS, dtype=F32)[None, :]
    oh0 = (route[:, 0:1] == experts).astype(F32)
    oh1 = (route[:, 1:2] == experts).astype(F32)
    both = (oh0 + oh1).reshape(t // LANES, LANES, N_EXPERTS)
    tri = jnp.tril(jnp.ones((LANES, LANES), F32))
    intra = jnp.einsum("ij,bjk->bik", tri, both)
    blk_tot = intra[:, -1, :]
    blk_cum = jnp.cumsum(blk_tot, axis=0)
    earlier = (intra - both + (blk_cum - blk_tot)[:, None, :]).reshape(t, N_EXPERTS)
    counts = blk_cum[-1].astype(jnp.int32)
    pcounts = (counts + MOE_TM - 1) // MOE_TM * MOE_TM
    pends = jnp.cumsum(pcounts)
    pstarts = pends - pcounts
    base = pstarts.astype(F32)[None, :] + earlier
    dest0 = jnp.sum(oh0 * base, axis=1).astype(jnp.int32)
    dest1 = jnp.sum(oh1 * base, axis=1).astype(jnp.int32)
    nblocks = -(-n_slots // MOE_TM) + N_EXPERTS
    nrows = -(-nblocks * MOE_TM // GATHER_ROWS) * GATHER_ROWS
    nblocks = nrows // MOE_TM
    blk_start = jnp.arange(nblocks, dtype=jnp.int32) * MOE_TM
    blk_e = jnp.minimum(jnp.sum((pends[None, :] <= blk_start[:, None]).astype(jnp.int32), axis=1), N_EXPERTS - 1)
    n_used = (pends[-1] // MOE_TM).astype(jnp.int32).reshape(1)
    n_valid = jnp.clip((pstarts + counts)[blk_e] - blk_start, 0, MOE_TM).astype(jnp.int32)
    ids = jnp.arange(N_EXPERTS, dtype=jnp.int32)
    has = counts > 0
    later = lax.cummin(jnp.where(has, ids, N_EXPERTS)[::-1])[::-1]
    nxt = jnp.concatenate([later[1:], jnp.full((1,), N_EXPERTS, jnp.int32)])
    nxt = jnp.where(nxt >= N_EXPERTS, -1, nxt)
    slot = (jnp.cumsum(has.astype(jnp.int32)) - 1) % 2
    xs = _scatter_rows(h2, dest0, dest1, nrows)
    ys = _experts(xs, blk_e, n_used, n_valid, nxt[blk_e], slot[blk_e], wg, wu, wd, layer)
    return _gather_rows(ys, jnp.concatenate([dest0, dest1]))


def kernel(x, c, ctx, c_ctx, ada_w, ada_b, norm1_g, norm2_g, w_in, w_out, s5_lam_re, s5_lam_im, s5_log_dt, s5_b_re, s5_b_im, s5_c_re, s5_c_im, s5_d, s5_glu_w, s5_glu_b, ga_qn_g, ga_kn_g, ssd_conv_w, ssd_conv_b, ssd_dt_bias, ssd_a_log, ssd_d, ssd_norm_g, wa_sink, moe_coarse_w, moe_coarse_b, moe_fine_w, moe_fine_b, moe_w_gate, moe_w_up, moe_w_down, final_g):
    nb, l, d = x.shape
    lc = ctx.shape[1]
    depth = ada_w.shape[0]
    assert lc == TM and l % TM == 0 and nb <= SUBLANES - 1 and d == D_MODEL
    s_len = lc + l
    nblk = s_len // TM
    t = nb * s_len

    cc = jnp.zeros((SUBLANES, d), F32).at[:nb].set(c).at[nb].set(c_ctx)
    mods = _ada(cc, ada_w, ada_b).reshape(depth, SUBLANES, 6, d)
    cos_t, sin_t = _rope_tables(lc, l)
    w_packed = jax.vmap(_pack_w_in)(w_in)
    s5_tabs = jax.vmap(_s5_params)(s5_lam_re, s5_lam_im, s5_log_dt, s5_b_re, s5_b_im, s5_c_re, s5_c_im, s5_d)
    wrs, brs = jax.vmap(_pack_router)(moe_coarse_w, moe_coarse_b, moe_fine_w, moe_fine_b)

    src = ("first", x.reshape(nb * l, d), ctx.reshape(nb * lc, d))
    for i in range(depth):
        mod = mods[i]
        (xm, xbc, ug, z, dt, gaq, gak, gav, waq, wak, wav) = _inproj(
            src, mod, norm1_g[i], w_packed[i], cos_t, sin_t, ga_qn_g[i], ga_kn_g[i], nb, nblk)
        ys5 = _s5(ug, tuple(tab[i] for tab in s5_tabs), nb, s_len, lc)
        oga, owa = _attn(wa_sink[i], gaq, gak, gav, waq, wak, wav, nb, s_len, lc)
        y0, y1 = _ssd(xbc, dt, ssd_conv_w[i], ssd_conv_b[i], ssd_dt_bias[i], ssd_a_log[i], ssd_d[i], nb, s_len, lc)
        wr, br = wrs[i], brs[i]
        xn, h2, route = _outproj(xm, ys5, oga, y0, y1, z, owa, mod, s5_glu_w[i], s5_glu_b[i], ssd_norm_g[i],
                                 w_out[i], norm2_g[i], wr, br, nb, nblk)
        rows2 = _moe(h2, route, moe_w_gate, moe_w_up, moe_w_down, i)
        src = ("moe", xn, rows2, route, mod)
    return _final(xn, rows2, route, mod, final_g, nb, nblk).reshape(nb, l, d)
```

```python
import functools
import math

import jax
import jax.numpy as jnp
import numpy as np
from jax import lax
from jax.experimental import pallas as pl
from jax.experimental.pallas import tpu as pltpu
from jax.experimental.pallas import tpu_sc as plsc

F32 = jnp.float32
BF16 = jnp.bfloat16
HI = lax.Precision.HIGHEST

D_MODEL = 1024
GRID_W = 64
GROUP_W = 256
HEAD_DIM = 64
ROPE_FREQS = HEAD_DIM // 4
ROPE_BASE = 10000.0
EPS = 1e-6
S5_CH = 16
S5_GROUPS = GROUP_W // S5_CH
S5_STATE = 64
N_HEADS = 4
SSD_HEADS = 4
SSD_NGROUPS = 2
SSD_STATE = 128
SSD_XBC = GROUP_W + 2 * SSD_NGROUPS * SSD_STATE
WINDOW = 128
MOE_GROUPS = 4
MOE_PER_GROUP = 8
N_EXPERTS = 32
D_EXPERT = D_MODEL // 2

LANES = 128
SUBLANES = 8
TM = 256
TQ = 128
GA_TQ = 128
GA_SUB = 2
S5_Q = 32
S5_BLK = S5_Q * S5_CH
MOE_TM = 512
SC_CORES = 2
SC_SUBCORES = 16
SC_GATHER_K = 32
SC_FETCH_K = 128
GATHER_ROWS = SC_CORES * SC_SUBCORES * SC_GATHER_K
ROUTE_FINE0 = 32
VMEM_LIMIT = 56 * 1024 * 1024

NEG_INF = float("-inf")
LOG2E = math.log2(math.e)


def _cp(sem, vmem=VMEM_LIMIT):
    return pltpu.CompilerParams(dimension_semantics=sem, vmem_limit_bytes=vmem)


def _dot(a, b):
    return jnp.dot(a, b, preferred_element_type=F32)


def _dot_hi(a, b):
    return jnp.dot(a, b, preferred_element_type=F32, precision=HI)


def _dot_nt(a, b):
    return lax.dot_general(a, b, (((1,), (1,)), ((), ())), preferred_element_type=F32)


def _sigmoid(x):
    return 1.0 / (1.0 + jnp.exp(-x))


def _silu(x):
    return x * _sigmoid(x)


def _gelu_tanh(x):
    return 0.5 * x * (1.0 + jnp.tanh(math.sqrt(2.0 / math.pi) * (x + 0.044715 * (x * x * x))))


def _softplus(x):
    return jnp.maximum(x, 0.0) + jnp.log(1.0 + jnp.exp(-jnp.abs(x)))


_HI16 = 0xFFFF0000


def _pack_bf16_pair(x):
    n = x.shape[1] // 2
    bits = pltpu.bitcast(x.astype(BF16).astype(F32), jnp.uint32)
    return (bits[:, n:] & jnp.uint32(_HI16)) | (bits[:, :n] >> 16)


def _unpack_bf16_pair(w):
    return pltpu.bitcast(w << 16, F32), pltpu.bitcast(w & jnp.uint32(_HI16), F32)


def _per_head_cols(v, base, n_heads, shape):
    lane = lax.broadcasted_iota(jnp.int32, shape, 1)
    out = jnp.broadcast_to(v[:, base + n_heads - 1:base + n_heads], shape)
    for h in range(n_heads - 2, -1, -1):
        out = jnp.where(lane < (h + 1) * HEAD_DIM, v[:, base + h:base + h + 1], out)
    return out


def _ada_kernel(c_ref, w_ref, b_ref, o_ref):
    c = c_ref[...]
    o_ref[0] = _dot_hi(_silu(c), w_ref[0]) + b_ref[0]


def _ada(cc, ada_w, ada_b):
    depth, d, n = ada_w.shape
    tn = 1536
    return pl.pallas_call(
        _ada_kernel,
        out_shape=jax.ShapeDtypeStruct((depth, SUBLANES, n), F32),
        grid=(depth, n // tn),
        in_specs=[pl.BlockSpec((SUBLANES, d), lambda l, j: (0, 0)),
                  pl.BlockSpec((1, d, tn), lambda l, j: (l, 0, j)),
                  pl.BlockSpec((1, 1, tn), lambda l, j: (l, 0, j))],
        out_specs=pl.BlockSpec((1, SUBLANES, tn), lambda l, j: (l, 0, j)),
        compiler_params=_cp(("parallel", "parallel")),
        name="ada_mod",
    )(cc, ada_w, ada_b.reshape(depth, 1, n))


_C_XBC = 0
_C_U = _C_XBC + SSD_XBC
_C_Z = _C_U + GROUP_W
_C_DT = _C_Z + GROUP_W
_C_GAQ = _C_DT + LANES
_C_WAQ = _C_GAQ + N_HEADS * LANES
_C_GAK = _C_WAQ + N_HEADS * LANES
_C_GAV = _C_GAK + LANES
_C_WAK = _C_GAV + LANES
_C_WAV = _C_WAK + LANES
_C_END = _C_WAV + LANES


def _expand_q_cols(wq):
    zero = jnp.zeros((wq.shape[0], HEAD_DIM), wq.dtype)
    parts = []
    for h in range(N_HEADS):
        head = wq[:, h * HEAD_DIM:(h + 1) * HEAD_DIM]
        parts += [head, zero] if h // 2 == 0 else [zero, head]
    return jnp.concatenate(parts, axis=1)


def _pack_w_in(w_in):
    cuts = np.cumsum([256, 256, 128, 128, 256, SSD_XBC, 2 * SSD_HEADS, 256, 128, 128])[:-1]
    u, gaq, gak, gav, z, xbc, dt, waq, wak, wav = jnp.split(w_in, [int(c) for c in cuts], axis=1)
    dt = jnp.pad(dt, ((0, 0), (0, LANES - dt.shape[1])))
    w = jnp.concatenate([xbc, u, z, dt, _expand_q_cols(gaq), _expand_q_cols(waq), gak, gav, wak, wav], axis=1)
    return w.astype(BF16)


def _rope(x, cos, sins):
    w = x.shape[1]
    if w > LANES:
        cos = jnp.concatenate([cos] * (w // LANES), axis=1)
        sins = jnp.concatenate([sins] * (w // LANES), axis=1)
    lane = lax.broadcasted_iota(jnp.int32, x.shape, 1)
    up = pltpu.roll(x, w - ROPE_FREQS, 1)
    dn = pltpu.roll(x, ROPE_FREQS, 1)
    partner = jnp.where((lane & ROPE_FREQS) == 0, up, dn)
    return x * cos + partner * sins


def _v_with_ones(v):
    lo = lax.broadcasted_iota(jnp.int32, v.shape, 1) < HEAD_DIM
    return jnp.concatenate([jnp.where(lo, v, 1.0), jnp.where(lo, 1.0, v)], axis=1).astype(BF16)


def _moe_residual(xn_ref, r0_ref, r1_ref, route_ref, mod_ref):
    route = route_ref[...]
    r0 = jnp.concatenate(_unpack_bf16_pair(r0_ref[...]), axis=1)
    r1 = jnp.concatenate(_unpack_bf16_pair(r1_ref[...]), axis=1)
    return xn_ref[...] + mod_ref[0, 5:6, :] * (route[:, 2:3] * r0 + route[:, 3:4] * r1)


ROW_SUB = 2


def _row_views(refs, s):
    return [r.at[pl.ds(s * TM, TM), :] for r in refs]


def _inproj_kernel(*refs, first, nblk):
    n_blk_in = (2 if first else 5) + 3
    shared = refs[ROW_SUB * n_blk_in:]
    g_ref, w_ref, qn_ref, kn_ref = shared[:4]
    xm_o, xbc_o, ug_o = shared[4:7]
    rest_o = shared[7:-1]
    u_scr = shared[-1]
    for s in range(ROW_SUB):
        blk_refs = refs[s * n_blk_in:(s + 1) * n_blk_in]
        xm_v, xbc_v = _row_views((xm_o, xbc_o), s)
        ug_v = ug_o.at[:, pl.ds(s * S5_TB, S5_TB), :]
        _inproj_block(blk_refs, g_ref, w_ref, qn_ref, kn_ref, xm_v, xbc_v, ug_v, _row_views(rest_o, s), u_scr,
                      first, (pl.program_id(0) * ROW_SUB + s) % nblk == 0)


def _inproj_block(blk_refs, g_ref, w_ref, qn_ref, kn_ref, xm_o, xbc_o, ug_o, rest_o, u_scr, first, is_ctx):
    if first:
        lat_ref, ctx_ref = blk_refs[:2]
        x = jnp.where(is_ctx, ctx_ref[...], lat_ref[...])
    else:
        x = _moe_residual(*blk_refs[:5])
    mod_ref, cos_ref, sin_ref = blk_refs[-3:]
    z_o, dt_o, gaq_o, gak_o, gav_o, waq_o, wak_o, wav_o = rest_o
    xm_o[...] = x
    ms = jnp.mean(x * x, axis=-1, keepdims=True)
    xn = x * lax.rsqrt(ms + EPS) * g_ref[...]
    h = xn * (1.0 + mod_ref[0, 1:2, :]) + mod_ref[0, 0:1, :]
    hb = h.astype(BF16)

    def proj(lo, hi):
        return _dot(hb, w_ref[:, lo:hi])

    cos = cos_ref[...]
    sins = sin_ref[...]
    scale = LOG2E * HEAD_DIM ** -0.5
    q = proj(_C_GAQ, _C_WAQ)
    qs = q * q
    inv = jnp.concatenate(
        [jnp.broadcast_to(lax.rsqrt(jnp.sum(qs[:, s * LANES:(s + 1) * LANES], axis=1, keepdims=True)
                                    * (1.0 / HEAD_DIM) + EPS), (q.shape[0], LANES)) for s in range(N_HEADS)], axis=1)
    gaq_o[...] = (_rope(q * inv * qn_ref[...], cos, sins) * scale).astype(BF16)
    waq_o[...] = (_rope(proj(_C_WAQ, _C_GAK), cos, sins) * scale).astype(BF16)
    k = proj(_C_GAK, _C_GAV)
    ks = k * k
    lane = lax.broadcasted_iota(jnp.int32, k.shape, 1)
    lo = lane < HEAD_DIM
    ms0 = jnp.sum(jnp.where(lo, ks, 0.0), axis=1, keepdims=True)
    ms1 = jnp.sum(jnp.where(lo, 0.0, ks), axis=1, keepdims=True)
    kinv = lax.rsqrt(jnp.where(lo, ms0, ms1) * (1.0 / HEAD_DIM) + EPS)
    gak_o[...] = _rope(k * kinv * kn_ref[...], cos, sins).astype(BF16)
    gav_o[...] = _v_with_ones(proj(_C_GAV, _C_WAK))
    wak_o[...] = _rope(proj(_C_WAK, _C_WAV), cos, sins).astype(BF16)
    wav_o[...] = _v_with_ones(proj(_C_WAV, _C_END))
    xbc_o[...] = proj(_C_XBC, _C_U)
    u = proj(_C_U, _C_Z)
    u_scr[0] = u[:, :LANES]
    u_scr[1] = u[:, LANES:]
    _s5_pack_kernel(u_scr.at[0], u_scr.at[1], ug_o)
    z_o[...] = proj(_C_Z, _C_DT)
    dt_o[...] = proj(_C_DT, _C_GAQ)


def _mod_row(i, nblk, nb):
    return jnp.where(i % nblk == 0, nb, i // nblk)


def _inproj(src, mod, norm_g, w_packed, cos_t, sin_t, qn_g, kn_g, nb, nblk):
    first = src[0] == "first"
    d = src[1].shape[1]
    t = nb * nblk * TM
    row = lambda i: (i, 0)
    fix = lambda i: (0, 0)
    nsteps = t // (ROW_SUB * TM)
    assert nsteps * ROW_SUB * TM == t

    def blk_specs(s):
        bid = lambda i: ROW_SUB * i + s
        modspec = pl.BlockSpec((1, 6, d), lambda i: (_mod_row(bid(i), nblk, nb), 0, 0))
        table = pl.BlockSpec((TM, LANES), lambda i: (bid(i) % nblk, 0))
        if first:
            srcs = [pl.BlockSpec((TM, d), lambda i: ((bid(i) // nblk) * (nblk - 1) + jnp.maximum(bid(i) % nblk - 1, 0), 0)),
                    pl.BlockSpec((TM, d), lambda i: (bid(i) // nblk, 0))]
        else:
            srcs = [pl.BlockSpec((TM, d), lambda i: (bid(i), 0)), pl.BlockSpec((TM, d // 2), lambda i: (bid(i), 0)),
                    pl.BlockSpec((TM, d // 2), lambda i: (bid(i) + t // TM, 0)),
                    pl.BlockSpec((TM, LANES), lambda i: (bid(i), 0)), modspec]
        return srcs + [modspec, table, table]

    if first:
        blk_args = tuple(src[1:]) + (mod, cos_t, sin_t)
    else:
        blk_args = (src[1], src[2], src[2], src[3], src[4], mod, cos_t, sin_t)
    outs = [(d, F32), (SSD_XBC, F32), None, (GROUP_W, F32), (LANES, F32),
            (N_HEADS * LANES, BF16), (LANES, BF16), (2 * LANES, BF16),
            (N_HEADS * LANES, BF16), (LANES, BF16), (2 * LANES, BF16)]
    shapes = [jax.ShapeDtypeStruct((t, o[0]), o[1]) if o else
              jax.ShapeDtypeStruct((S5_GROUPS, t // S5_Q, S5_BLK), F32) for o in outs]
    specs = [pl.BlockSpec((ROW_SUB * TM, o[0]), row) if o else
             pl.BlockSpec((S5_GROUPS, ROW_SUB * S5_TB, S5_BLK), lambda i: (0, i, 0)) for o in outs]
    return pl.pallas_call(
        functools.partial(_inproj_kernel, first=first, nblk=nblk),
        out_shape=shapes,
        grid=(nsteps,),
        in_specs=[sp for s in range(ROW_SUB) for sp in blk_specs(s)] + [
                  pl.BlockSpec((1, d), fix),
                  pl.BlockSpec((d, _C_END), fix),
                  pl.BlockSpec((1, N_HEADS * LANES), fix),
                  pl.BlockSpec((1, LANES), fix)],
        out_specs=specs,
        scratch_shapes=[pltpu.VMEM((GROUP_W // LANES, TM, LANES), F32)],
        compiler_params=_cp(("parallel",)),
        name="in_proj",
    )(*(blk_args * ROW_SUB), norm_g.reshape(1, d), w_packed,
      jnp.tile(qn_g, 2 * N_HEADS).reshape(1, -1), jnp.tile(kn_g, 2).reshape(1, -1))


def _rope_tables(lc, l):
    n_rows = l // GRID_W
    rows = np.repeat(np.arange(n_rows), GRID_W)
    cols = np.tile(np.arange(GRID_W), n_rows)
    inv = np.power(np.float32(ROPE_BASE), -np.arange(ROPE_FREQS, dtype=np.float32) / ROPE_FREQS)
    ang = np.stack([rows, cols], axis=-1).astype(np.float32)[..., None] * inv
    cos = np.cos(ang)
    sin = np.sin(ang)
    cos64 = np.stack([cos, cos], axis=2).reshape(l, HEAD_DIM)
    sin64 = np.stack([-sin, sin], axis=2).reshape(l, HEAD_DIM)
    cos64 = np.concatenate([np.ones((lc, HEAD_DIM), np.float32), cos64], axis=0)
    sin64 = np.concatenate([np.zeros((lc, HEAD_DIM), np.float32), sin64], axis=0)
    return (jnp.asarray(np.tile(cos64, (1, 2)), dtype=F32), jnp.asarray(np.tile(sin64, (1, 2)), dtype=F32))


def _merge_heads(o2, kvh):
    tq = o2.shape[0] // 2
    oa, ob = o2[:tq], o2[tq:]
    lane = lax.broadcasted_iota(jnp.int32, oa.shape, 1)
    if kvh == 0:
        return jnp.where(lane < HEAD_DIM, oa, pltpu.roll(ob, HEAD_DIM, 1))
    return jnp.where(lane < HEAD_DIM, pltpu.roll(oa, HEAD_DIM, 1), ob)


def _stack_q(q_ref, rows, kvh):
    return jnp.concatenate([q_ref[rows, (2 * kvh) * LANES:(2 * kvh + 1) * LANES],
                            q_ref[rows, (2 * kvh + 1) * LANES:(2 * kvh + 2) * LANES]], axis=0)


def _ga_attend(q_ref, k_ref, v_ref, o_ref, nkeys):
    k = k_ref[0:nkeys, :]
    for sub in range(GA_SUB):
        rows = slice(sub * GA_TQ, (sub + 1) * GA_TQ)
        scores = [_dot_nt(_stack_q(q_ref, rows, kvh), k) for kvh in range(2)]
        outs = []
        for kvh in range(2):
            s = scores[kvh]
            p = jnp.exp2((s - jnp.max(s, axis=1, keepdims=True)).astype(BF16))
            o2 = _dot(p, v_ref[0:nkeys, kvh * LANES:(kvh + 1) * LANES])
            outs.append(_merge_heads(o2 / pltpu.roll(o2, HEAD_DIM, 1), kvh))
        o_ref[rows, :] = jnp.concatenate(outs, axis=1)


def _attn_kernel(sink_ref, gq_ref, gk_ref, gv_ref, wq_ref, wk_ref, wv_ref, og_ref, ow_ref, *, lc):
    is_ctx = pl.program_id(1) < lc // TM

    @pl.when(is_ctx)
    def _():
        _ga_attend(gq_ref, gk_ref, gv_ref, og_ref, lc)
        _wa_attend(sink_ref, wq_ref, wk_ref, wv_ref, ow_ref, lc)

    @pl.when(jnp.logical_not(is_ctx))
    def _():
        _ga_attend(gq_ref, gk_ref, gv_ref, og_ref, gk_ref.shape[0])
        _wa_attend(sink_ref, wq_ref, wk_ref, wv_ref, ow_ref, lc)


def _attn(sink, gq, gk, gv, wq, wk, wv, nb, s_len, lc):
    t = gq.shape[0]
    nq = s_len // TM
    assert GA_SUB * GA_TQ == TM and WA_SUB * TQ == TM
    qspec = pl.BlockSpec((TM, N_HEADS * LANES), lambda b, j: (b * nq + j, 0))
    kspec = pl.BlockSpec((s_len, LANES), lambda b, j: (b, 0))
    vspec = pl.BlockSpec((s_len, 2 * LANES), lambda b, j: (b, 0))
    ospec = pl.BlockSpec((TM, GROUP_W), lambda b, j: (b * nq + j, 0))
    return pl.pallas_call(
        functools.partial(_attn_kernel, lc=lc),
        out_shape=[jax.ShapeDtypeStruct((t, GROUP_W), F32)] * 2,
        grid=(nb, nq),
        in_specs=[pl.BlockSpec(memory_space=pltpu.SMEM), qspec, kspec, vspec, qspec, kspec, vspec],
        out_specs=[ospec, ospec],
        compiler_params=_cp(("parallel", "arbitrary")),
        name="attention",
    )(sink, gq, gk, gv, wq, wk, wv)


WA_SUB = TM // TQ


def _wa_attend(sink_ref, q_ref, k_ref, v_ref, o_ref, lc):
    s_len = k_ref.shape[0]
    kc = k_ref[0:lc, :]
    row = lax.broadcasted_iota(jnp.int32, (2 * TQ, 1), 0)
    for sub in range(WA_SUB):
        rows = slice(sub * TQ, (sub + 1) * TQ)
        n = pl.program_id(1) * WA_SUB + sub - lc // TQ
        start = pl.multiple_of(jnp.clip(lc + (n - 1) * TQ, lc, s_len - 3 * TQ), TQ)
        kb = k_ref[pl.ds(start, 3 * TQ), :]
        qpos = n * TQ + lax.broadcasted_iota(jnp.int32, (TQ, 3 * TQ), 0)
        kpos = (start - lc) + lax.broadcasted_iota(jnp.int32, (TQ, 3 * TQ), 1)
        reach = jnp.where(n >= 0, WINDOW, -1)
        valid = jnp.abs(qpos - kpos) <= reach
        valid = jnp.concatenate([valid, valid], axis=0)
        outs = []
        for kvh in range(2):
            q2 = jnp.concatenate([q_ref[rows, (2 * kvh) * LANES:(2 * kvh + 1) * LANES],
                                  q_ref[rows, (2 * kvh + 1) * LANES:(2 * kvh + 2) * LANES]], axis=0)
            sc = _dot_nt(q2, kc)
            sb = jnp.where(valid, _dot_nt(q2, kb), NEG_INF)
            sink = jnp.where(row < TQ, sink_ref[2 * kvh], sink_ref[2 * kvh + 1]) * LOG2E
            m = jnp.maximum(jnp.maximum(jnp.max(sc, axis=1, keepdims=True), jnp.max(sb, axis=1, keepdims=True)), sink)
            pc = jnp.exp2((sc - m).astype(BF16))
            pb = jnp.exp2((sb - m).astype(BF16))
            vcols = slice(kvh * LANES, (kvh + 1) * LANES)
            o2 = _dot(pc, v_ref[0:lc, vcols]) + _dot(pb, v_ref[pl.ds(start, 3 * TQ), vcols])
            denom = pltpu.roll(o2, HEAD_DIM, 1) + jnp.exp2(sink - m)
            outs.append(_merge_heads(o2 / denom, kvh))
        o_ref[rows, :] = jnp.concatenate(outs, axis=1)


def _s5_chunk_index(t, rev, nc_ctx, nc_tot):
    if not rev:
        return t
    return jnp.where(t < nc_ctx, nc_ctx - 1 - t, nc_tot - 1 - (t - nc_ctx))


def _s5_kernel(u_ref, k_ref, p_ref, g_ref, ar_ref, ai_ref, dsk_ref, y_ref, s_scr, h_scr, m_scr, *, nb, nc_ctx, nc_tot):
    for d in range(2):
        ext = k_ref[d, 0]
        for s in range(S5_Q):
            lo = ((S5_Q - s) if d == 0 else (S5_Q - 1 - s)) * S5_CH
            win = pltpu.roll(ext, (2 * S5_BLK - lo) % (2 * S5_BLK), 1)[:, :S5_BLK]
            m_scr[d, s * S5_CH:(s + 1) * S5_CH, :] = win.astype(BF16)
    uf = u_ref[0]
    u = uf.astype(BF16)
    for d in range(2):
        for k in range(2):
            s_scr[d, k] = _dot(u, p_ref[d, k, 0])
    ar = [jnp.broadcast_to(ar_ref[d, 0], (nb, LANES)) for d in range(2)]
    ai = [[jnp.broadcast_to(ai_ref[d, k, 0], (nb, LANES)) for k in range(2)] for d in range(2)]

    def body(t, carry):
        out = []
        for d in range(2):
            h, hs = carry[d]
            rows = pl.ds(_s5_chunk_index(t, d == 1, nc_ctx, nc_tot), nb, stride=nc_tot)
            h_scr[d, rows, :] = h
            out.append((ar[d] * h + ai[d][0] * hs + s_scr[d, 0, rows, :],
                        ar[d] * hs + ai[d][1] * h + s_scr[d, 1, rows, :]))
        return tuple(out)

    zero = jnp.zeros((nb, LANES), F32)
    lax.fori_loop(0, nc_tot, body, ((zero, zero), (zero, zero)), unroll=2)
    y = uf * dsk_ref[0]
    for d in range(2):
        y = y + _dot(u, m_scr[d]) + _dot(h_scr[d].astype(BF16), g_ref[d, 0])
    y_ref[0] = y


S5_TB = TM // S5_Q
S5_GPS = LANES // S5_CH


def _s5_pack_kernel(lo_ref, hi_ref, o_ref):
    for s in range(S5_Q):
        rows = pl.ds(s, S5_TB, stride=S5_Q)
        halves = (lo_ref[rows, :], hi_ref[rows, :])
        dst = S5_CH * (s % S5_GPS)
        for g in range(S5_GROUPS):
            slab = halves[g // S5_GPS]
            src = S5_CH * (g % S5_GPS)
            moved = slab if src == dst else pltpu.roll(slab, (dst - src) % LANES, 1)
            o_ref[g, :, s * S5_CH:(s + 1) * S5_CH] = moved[:, dst:dst + S5_CH]


def _s5_unpack_kernel(y_ref, o_ref):
    lane_grp = lax.broadcasted_iota(jnp.int32, (S5_TB, LANES), 1) // S5_CH
    for s in range(S5_Q):
        src = S5_CH * (s % S5_GPS)
        for half in range(S5_GROUPS // S5_GPS):
            acc = None
            for gl in range(S5_GPS):
                slab = y_ref[half * S5_GPS + gl, :, (s // S5_GPS) * LANES:(s // S5_GPS + 1) * LANES]
                dst = S5_CH * gl
                moved = slab if src == dst else pltpu.roll(slab, (dst - src) % LANES, 1)
                acc = moved if acc is None else jnp.where(lane_grp == gl, moved, acc)
            o_ref[half, pl.ds(s, S5_TB, stride=S5_Q), :] = acc


def _s5_params(lam_re, lam_im, log_dt, b_re, b_im, c_re, c_im, d_skip):
    q = S5_Q
    dt = jnp.exp(log_dt)[..., None]
    lr, li = lam_re, lam_im
    mag = jnp.exp(lr * dt)
    a_re = mag * jnp.cos(li * dt)
    a_im = mag * jnp.sin(li * dt)
    den = lr * lr + li * li
    f_re = ((a_re - 1.0) * lr + a_im * li) / den
    f_im = (a_im * lr - (a_re - 1.0) * li) / den
    bb_re = f_re[..., None] * b_re - f_im[..., None] * b_im
    bb_im = f_re[..., None] * b_im + f_im[..., None] * b_re
    kk = jnp.arange(q + 1, dtype=F32)[:, None, None, None]
    pmag = jnp.exp(kk * (lr * dt))
    pw_re = pmag * jnp.cos(kk * (li * dt))
    pw_im = pmag * jnp.sin(kk * (li * dt))
    lw_re = pw_re[:q].transpose(1, 2, 0, 3)[:, :, :, None, :]
    lw_im = pw_im[:q].transpose(1, 2, 0, 3)[:, :, :, None, :]
    ck_re = c_re[:, :, None] * lw_re - c_im[:, :, None] * lw_im
    ck_im = c_re[:, :, None] * lw_im + c_im[:, :, None] * lw_re
    ck = jnp.concatenate([ck_re, -ck_im], axis=-1).reshape(2, S5_GROUPS, S5_BLK, 2 * S5_STATE)
    kern_t = jnp.einsum("dgmp,dgpc->dgcm", ck, jnp.concatenate([bb_re, bb_im], axis=2), precision=HI)
    kern_t = kern_t.reshape(2, S5_GROUPS, S5_CH, q, S5_CH)
    zeros = jnp.zeros_like(kern_t)
    bbt_re = bb_re.transpose(0, 1, 3, 2)[:, :, None]
    bbt_im = bb_im.transpose(0, 1, 3, 2)[:, :, None]
    ct_re = c_re.transpose(0, 1, 3, 2)[:, :, :, None, :]
    ct_im = c_im.transpose(0, 1, 3, 2)[:, :, :, None, :]
    ms, ps, gs = [], [], []
    for d in range(2):
        ext = (jnp.concatenate([zeros[d], kern_t[d]], axis=2) if d == 0
               else jnp.concatenate([kern_t[d, :, :, ::-1], zeros[d]], axis=2))
        ext = ext.reshape(S5_GROUPS, S5_CH, 2 * S5_BLK)
        ms.append(ext)
        pidx = (q - 1 - jnp.arange(q)) if d == 0 else jnp.arange(q)
        pr = pw_re[pidx, d].transpose(1, 0, 2)[:, :, None, :]
        pi = pw_im[pidx, d].transpose(1, 0, 2)[:, :, None, :]
        p_re = pr * bbt_re[d] - pi * bbt_im[d]
        p_im = pr * bbt_im[d] + pi * bbt_re[d]
        pd = jnp.stack([jnp.concatenate([p_re, p_im], axis=3), jnp.concatenate([p_im, p_re], axis=3)])
        ps.append(pd.reshape(2, S5_GROUPS, S5_BLK, 2 * S5_STATE))
        gidx = (jnp.arange(q) + 1) if d == 0 else (q - jnp.arange(q))
        gw_re = pw_re[gidx, d].transpose(1, 2, 0)[..., None]
        gw_im = pw_im[gidx, d].transpose(1, 2, 0)[..., None]
        g_re = ct_re[d] * gw_re - ct_im[d] * gw_im
        g_im = ct_re[d] * gw_im + ct_im[d] * gw_re
        gs.append(jnp.concatenate([g_re, -g_im], axis=1).reshape(S5_GROUPS, 2 * S5_STATE, S5_BLK))
    ar = jnp.concatenate([pw_re[q], pw_re[q]], axis=-1)[:, :, None, :]
    ai = jnp.stack([jnp.concatenate([-pw_im[q], pw_im[q]], axis=-1),
                    jnp.concatenate([pw_im[q], -pw_im[q]], axis=-1)], axis=1)[:, :, :, None, :]
    dsk = jnp.tile(d_skip.reshape(S5_GROUPS, 1, S5_CH), (1, 1, q))
    return (jnp.stack(ms), jnp.stack(ps).astype(BF16), jnp.stack(gs).astype(BF16),
            ar.astype(F32), ai.astype(F32), dsk.astype(F32))


def _s5(ug, params, nb, s_len, lc):
    m, p, g, ar, ai, dsk = params
    nc_tot = s_len // S5_Q
    nc_ctx = lc // S5_Q
    r = nb * nc_tot
    return pl.pallas_call(
        functools.partial(_s5_kernel, nb=nb, nc_ctx=nc_ctx, nc_tot=nc_tot),
        out_shape=jax.ShapeDtypeStruct((S5_GROUPS, r, S5_BLK), F32),
        grid=(S5_GROUPS,),
        in_specs=[pl.BlockSpec((1, r, S5_BLK), lambda gi: (gi, 0, 0)),
                  pl.BlockSpec((2, 1, S5_CH, 2 * S5_BLK), lambda gi: (0, gi, 0, 0)),
                  pl.BlockSpec((2, 2, 1, S5_BLK, 2 * S5_STATE), lambda gi: (0, 0, gi, 0, 0)),
                  pl.BlockSpec((2, 1, 2 * S5_STATE, S5_BLK), lambda gi: (0, gi, 0, 0)),
                  pl.BlockSpec((2, 1, 1, 2 * S5_STATE), lambda gi: (0, gi, 0, 0)),
                  pl.BlockSpec((2, 2, 1, 1, 2 * S5_STATE), lambda gi: (0, 0, gi, 0, 0)),
                  pl.BlockSpec((1, 1, S5_BLK), lambda gi: (gi, 0, 0))],
        out_specs=pl.BlockSpec((1, r, S5_BLK), lambda gi: (gi, 0, 0)),
        scratch_shapes=[pltpu.VMEM((2, 2, r, 2 * S5_STATE), F32), pltpu.VMEM((2, r, 2 * S5_STATE), F32),
                        pltpu.VMEM((2, S5_BLK, S5_BLK), BF16)],
        compiler_params=_cp(("parallel",)),
        name="s5_scan",
    )(ug, m, p, g, ar, ai, dsk)


CONV_ROWS = 2 * TM


def _conv_kernel(x_ref, prev_ref, next_ref, w_ref, b_ref, o_ref, *, s_len, lc):
    x = x_ref[...]
    rows = x.shape[0]
    ridx = lax.broadcasted_iota(jnp.int32, x.shape, 0)
    pos = (pl.program_id(0) * rows) % s_len + ridx
    pos = jnp.where(pos >= s_len, pos - s_len, pos)
    seg_first = jnp.logical_or(pos == 0, pos == lc)
    seg_last = jnp.logical_or(pos == lc - 1, pos == s_len - 1)
    xm = jnp.where(ridx == 0, prev_ref[SUBLANES - 1:SUBLANES, :], pltpu.roll(x, 1, 0))
    xp = jnp.where(ridx == rows - 1, next_ref[0:1, :], pltpu.roll(x, rows - 1, 0))
    xm = jnp.where(seg_first, 0.0, xm)
    xp = jnp.where(seg_last, 0.0, xp)
    y = xm * w_ref[0:1, :] + x * w_ref[1:2, :] + xp * w_ref[2:3, :] + b_ref[...]
    o_ref[...] = _silu(y)


def _conv(xbc, w, b, s_len, lc):
    t, c = xbc.shape
    per = CONV_ROWS // SUBLANES
    last = t // SUBLANES - 1
    return pl.pallas_call(
        functools.partial(_conv_kernel, s_len=s_len, lc=lc),
        out_shape=jax.ShapeDtypeStruct((t, c), F32),
        grid=(t // CONV_ROWS,),
        in_specs=[pl.BlockSpec((CONV_ROWS, c), lambda i: (i, 0)),
                  pl.BlockSpec((SUBLANES, c), lambda i: (jnp.maximum(i * per - 1, 0), 0)),
                  pl.BlockSpec((SUBLANES, c), lambda i: (jnp.minimum((i + 1) * per, last), 0)),
                  pl.BlockSpec((3, c), lambda i: (0, 0)),
                  pl.BlockSpec((1, c), lambda i: (0, 0))],
        out_specs=pl.BlockSpec((CONV_ROWS, c), lambda i: (i, 0)),
        compiler_params=_cp(("parallel",)),
        name="ssd_conv",
    )(xbc, xbc, xbc, w, b.reshape(1, c))


_X_B = GROUP_W
_X_C = GROUP_W + SSD_NGROUPS * SSD_STATE


def _ssd_kernel(xf_ref, dtf_ref, dttf_ref, xr_ref, dtr_ref, dttr_ref, bias_ref, a_ref, biast_ref, at_ref, dsk_ref,
                yf_ref, yr_ref, stf_ref, str_ref):
    @pl.when(pl.program_id(1) == 0)
    def _():
        stf_ref[...] = jnp.zeros_like(stf_ref)
        str_ref[...] = jnp.zeros_like(str_ref)

    par = (bias_ref[...], a_ref[...], biast_ref[...], at_ref[...], dsk_ref[...])
    for j in range(SSD_SUB):
        rf = slice(j * TQ, (j + 1) * TQ)
        rr = slice((SSD_SUB - 1 - j) * TQ, (SSD_SUB - j) * TQ)
        for b in range(SSD_NB):
            yf_ref[b, rf, :] = _ssd_chunk_step(xf_ref[b, rf, :], dtf_ref[b, rf, :], dttf_ref[b, :, rf], par,
                                               stf_ref.at[b], False)
            yr_ref[b, rr, :] = _ssd_chunk_step(xr_ref[b, rr, :], dtr_ref[b, rr, :], dttr_ref[b, :, rr], par,
                                               str_ref.at[b], True)


def _ssd_chunk_step(xc, dt_raw, dtt_raw, par, st_ref, rev):
    bias, a_vec, biast, at_vec, dsk = par
    base = SSD_HEADS if rev else 0
    x = xc[:, 0:GROUP_W]
    dt = _softplus(dt_raw + bias)
    a = dt * a_vec
    dtt = _softplus(dtt_raw + biast)
    at = dtt * at_vec
    ri = lax.broadcasted_iota(jnp.int32, (TQ, TQ), 0)
    ci = lax.broadcasted_iota(jnp.int32, (TQ, TQ), 1)
    causal = (ci >= ri) if rev else (ri >= ci)
    tri = jnp.where(causal, 1.0, 0.0)
    cum_c = _dot_hi(tri, a)
    cum_r = _dot_nt_hi(at, tri)
    edge = 0 if rev else TQ - 1
    tot = cum_c[edge:edge + 1, :]

    shape = (TQ, GROUP_W)
    xdt = x * _per_head_cols(dt, base, SSD_HEADS, shape)
    lane = lax.broadcasted_iota(jnp.int32, shape, 1)
    y = jnp.zeros(shape, F32)
    bmat = [xc[:, _X_B + g * SSD_STATE:_X_B + (g + 1) * SSD_STATE].astype(BF16) for g in range(SSD_NGROUPS)]
    cmat = [xc[:, _X_C + g * SSD_STATE:_X_C + (g + 1) * SSD_STATE].astype(BF16) for g in range(SSD_NGROUPS)]
    cb = [_dot_nt(cmat[g], bmat[g]) for g in range(SSD_NGROUPS)]
    for h in range(SSD_HEADS):
        col = base + h
        seg = jnp.where(causal, cum_c[:, col:col + 1] - cum_r[col:col + 1, :], NEG_INF)
        scores = cb[h // 2] * jnp.exp(seg)
        xh = jnp.where((lane >= h * HEAD_DIM) & (lane < (h + 1) * HEAD_DIM), xdt, 0.0)
        y = y + _dot(scores.astype(BF16), xh.astype(BF16))
    st = st_ref[...]
    yo = jnp.concatenate(
        [_dot_nt(cmat[g], st[g * SSD_STATE:(g + 1) * SSD_STATE].astype(BF16)) for g in range(SSD_NGROUPS)], axis=1)
    y = y + yo * _per_head_cols(jnp.exp(cum_c), base, SSD_HEADS, shape)
    if not rev:
        y = y + x * dsk
    xd = xdt * _per_head_cols(jnp.exp(tot - cum_c), base, SSD_HEADS, shape)
    xdt_t = xd.T.astype(BF16)
    decay = jnp.exp(tot)
    for g in range(SSD_NGROUPS):
        new = _dot(xdt_t[g * SSD_STATE:(g + 1) * SSD_STATE], bmat[g])
        for hh in range(2):
            h = 2 * g + hh
            r0 = h * HEAD_DIM
            st_ref[r0:r0 + HEAD_DIM, :] = (decay[:, base + h:base + h + 1] * st[r0:r0 + HEAD_DIM]
                                           + new[hh * HEAD_DIM:(hh + 1) * HEAD_DIM])
    return y


def _dot_nt_hi(a, b):
    return lax.dot_general(a, b, (((1,), (1,)), ((), ())), preferred_element_type=F32, precision=HI)


def _ssd_chunk(c, rev, nc_ctx, nc_tot):
    if not rev:
        return c
    return jnp.where(c < nc_ctx, nc_ctx - 1 - c, nc_tot - 1 - (c - nc_ctx))


SSD_SUB = TM // TQ
SSD_NB = 2


def _ssd_scan(xc, dt, dtt, bias, a, biast, at, dsk, nb, s_len, lc):
    t = xc.shape[0]
    nblk = s_len // TM
    nctx = lc // TM
    assert nb % SSD_NB == 0
    fix = lambda b, c: (0, 0)
    xc3 = xc.reshape(nb, s_len, SSD_XBC)
    dt3 = dt.reshape(nb, s_len, LANES)

    def rows(rev):
        return lambda b, c: (b, _ssd_chunk(c, rev, nctx, nblk), 0)

    def lanes(rev):
        return lambda b, c: (b, 0, _ssd_chunk(c, rev, nctx, nblk))

    def data_specs(rev):
        return [pl.BlockSpec((SSD_NB, TM, SSD_XBC), rows(rev)), pl.BlockSpec((SSD_NB, TM, LANES), rows(rev)),
                pl.BlockSpec((SSD_NB, SUBLANES, TM), lanes(rev))]

    state = pltpu.VMEM((SSD_NB, SSD_HEADS * HEAD_DIM, SSD_STATE), F32)
    yf, yr = pl.pallas_call(
        _ssd_kernel,
        out_shape=[jax.ShapeDtypeStruct((nb, s_len, GROUP_W), F32)] * 2,
        grid=(nb // SSD_NB, nblk),
        in_specs=data_specs(False) + data_specs(True) + [
            pl.BlockSpec((1, LANES), fix), pl.BlockSpec((1, LANES), fix),
            pl.BlockSpec((SUBLANES, TQ), fix), pl.BlockSpec((SUBLANES, TQ), fix),
            pl.BlockSpec((1, GROUP_W), fix)],
        out_specs=[pl.BlockSpec((SSD_NB, TM, GROUP_W), rows(False)), pl.BlockSpec((SSD_NB, TM, GROUP_W), rows(True))],
        scratch_shapes=[state, state],
        compiler_params=_cp(("parallel", "arbitrary")),
        name="ssd_scan",
    )(xc3, dt3, dtt, xc3, dt3, dtt, bias, a, biast, at, dsk)
    return yf.reshape(t, GROUP_W), yr.reshape(t, GROUP_W)


def _ssd(xbc, dt, conv_w, conv_b, dt_bias, a_log, d_skip, nb, s_len, lc):
    xc = _conv(xbc, conv_w, conv_b, s_len, lc)
    nd = 2 * SSD_HEADS
    dtt = dt[:, :nd].reshape(nb, s_len, nd).transpose(0, 2, 1)
    bias = jnp.pad(dt_bias.reshape(1, nd), ((0, 0), (0, LANES - nd)))
    a = jnp.pad(-jnp.exp(a_log).reshape(1, nd), ((0, 0), (0, LANES - nd)))
    biast = jnp.broadcast_to(dt_bias.reshape(nd, 1), (nd, TQ))
    at = jnp.broadcast_to(-jnp.exp(a_log).reshape(nd, 1), (nd, TQ))
    dsk = jnp.repeat(d_skip, HEAD_DIM).reshape(1, GROUP_W)
    return _ssd_scan(xc, dt, dtt, bias, a, biast, at, dsk, nb, s_len, lc)


def _outproj_kernel(x_ref, ys5_ref, oga_ref, y0_ref, y1_ref, z_ref, owa_ref, *refs):
    mods, shared = refs[:ROW_SUB], refs[ROW_SUB:]
    for s in range(ROW_SUB):
        rows = _row_views((x_ref, oga_ref, y0_ref, y1_ref, z_ref, owa_ref) + tuple(shared[-4:-1]), s)
        _outproj_block(rows[0], ys5_ref.at[:, pl.ds(s * S5_TB, S5_TB), :], *rows[1:6], mods[s], *shared[:-4],
                       *rows[6:], shared[-1])


def _outproj_block(x_ref, ys5_ref, oga_ref, y0_ref, y1_ref, z_ref, owa_ref, mod_ref, gluw_ref, glub_ref,
                   ng_ref, wout_ref, n2_ref, wr_ref, br_ref, xn_o, h2_o, route_o, y_scr):
    _s5_unpack_kernel(ys5_ref, y_scr)
    gl = _gelu_tanh(jnp.concatenate([y_scr[0], y_scr[1]], axis=1))
    a = gl * _sigmoid(_dot(gl.astype(BF16), gluw_ref[...]) + glub_ref[...])
    m = (y0_ref[...] + y1_ref[...]) * _silu(z_ref[...])
    m = m * lax.rsqrt(jnp.mean(m * m, axis=-1, keepdims=True) + EPS) * ng_ref[...]
    w = wout_ref
    mix = (_dot(a.astype(BF16), w[0:GROUP_W, :]) + _dot(oga_ref[...].astype(BF16), w[GROUP_W:2 * GROUP_W, :])
           + _dot(m.astype(BF16), w[2 * GROUP_W:3 * GROUP_W, :]) + _dot(owa_ref[...].astype(BF16), w[3 * GROUP_W:, :]))
    xn = x_ref[...] + mod_ref[0, 2:3, :] * mix
    xn_o[...] = xn
    h2 = xn * lax.rsqrt(jnp.mean(xn * xn, axis=-1, keepdims=True) + EPS) * n2_ref[...]
    h2 = h2 * (1.0 + mod_ref[0, 4:5, :]) + mod_ref[0, 3:4, :]
    h2_o[...] = _pack_bf16_pair(h2)
    h_hi = h2.astype(BF16)
    h_lo = (h2 - h_hi.astype(F32)).astype(BF16)
    logits = _dot(h_hi, wr_ref[0]) + (_dot(h_lo, wr_ref[0]) + _dot(h_hi, wr_ref[1])) + br_ref[...]
    lane = lax.broadcasted_iota(jnp.int32, logits.shape, 1).astype(F32)
    big = float(4 * LANES)
    lcoarse = jnp.where(lane < MOE_GROUPS, logits, NEG_INF)
    mx = jnp.max(lcoarse, axis=1, keepdims=True)
    den = jnp.sum(jnp.exp(lcoarse - mx), axis=1, keepdims=True)
    grp = jnp.min(jnp.where(lcoarse == mx, lane, big), axis=1, keepdims=True)
    pg = 1.0 / den
    lo = ROUTE_FINE0 + grp * MOE_PER_GROUP
    lf = jnp.where(lane >= lo, jnp.where(lane < lo + MOE_PER_GROUP, logits, NEG_INF), NEG_INF)
    v1 = jnp.max(lf, axis=1, keepdims=True)
    i1 = jnp.min(jnp.where(lf == v1, lane, big), axis=1, keepdims=True)
    lf2 = jnp.where(lane == i1, NEG_INF, lf)
    v2 = jnp.max(lf2, axis=1, keepdims=True)
    i2 = jnp.min(jnp.where(lf2 == v2, lane, big), axis=1, keepdims=True)
    e2 = jnp.exp(v2 - v1)
    w1 = pg / (1.0 + e2)
    w2 = w1 * e2
    route = jnp.where(lane == 0, i1 - ROUTE_FINE0,
                      jnp.where(lane == 1, i2 - ROUTE_FINE0,
                                jnp.where(lane == 2, w1, jnp.where(lane == 3, w2, 0.0))))
    route_o[...] = route


def _outproj(x, ys5, oga, y0, y1, z, owa, mod, glu_w, glu_b, ssd_norm_g, w_out, norm2_g, wr, br, nb, nblk):
    t, d = x.shape
    row = lambda i: (i, 0)
    fix = lambda i: (0, 0)
    step = ROW_SUB * TM
    gw = pl.BlockSpec((step, GROUP_W), row)
    wr_hi = wr.astype(BF16)
    mod_specs = [pl.BlockSpec((1, 6, d), lambda i, s=s: (_mod_row(ROW_SUB * i + s, nblk, nb), 0, 0))
                 for s in range(ROW_SUB)]
    return pl.pallas_call(
        _outproj_kernel,
        out_shape=[jax.ShapeDtypeStruct((t, d), F32), jax.ShapeDtypeStruct((t, d // 2), jnp.uint32),
                   jax.ShapeDtypeStruct((t, LANES), F32)],
        grid=(t // step,),
        in_specs=[pl.BlockSpec((step, d), row),
                  pl.BlockSpec((S5_GROUPS, ROW_SUB * S5_TB, S5_BLK), lambda i: (0, i, 0)),
                  gw, gw, gw, gw, gw] + mod_specs + [
                  pl.BlockSpec((GROUP_W, GROUP_W), fix),
                  pl.BlockSpec((1, GROUP_W), fix),
                  pl.BlockSpec((1, GROUP_W), fix),
                  pl.BlockSpec((d, d), fix),
                  pl.BlockSpec((1, d), fix),
                  pl.BlockSpec((2, d, LANES), lambda i: (0, 0, 0)),
                  pl.BlockSpec((1, LANES), fix)],
        out_specs=[pl.BlockSpec((step, d), row), pl.BlockSpec((step, d // 2), row), pl.BlockSpec((step, LANES), row)],
        scratch_shapes=[pltpu.VMEM((GROUP_W // LANES, TM, LANES), F32)],
        compiler_params=_cp(("parallel",)),
        name="out_proj_router",
    )(x, ys5, oga, y0, y1, z, owa, *([mod] * ROW_SUB), glu_w.astype(BF16), glu_b.reshape(1, -1), ssd_norm_g.reshape(1, -1),
      w_out.astype(BF16), norm2_g.reshape(1, -1), jnp.stack([wr_hi, (wr - wr_hi.astype(F32)).astype(BF16)]), br)


def _pack_router(coarse_w, coarse_b, fine_w, fine_b):
    def lanes(coarse, fine):
        gap = jnp.zeros(coarse.shape[:-1] + (ROUTE_FINE0 - MOE_GROUPS,), F32)
        tail = jnp.zeros(coarse.shape[:-1] + (LANES - ROUTE_FINE0 - N_EXPERTS,), F32)
        return jnp.concatenate([coarse, gap, fine, tail], axis=-1)

    return lanes(coarse_w, fine_w), lanes(coarse_b[None, :], fine_b[None, :])


def _gather_rows(src, idx):
    m = idx.shape[0]
    d = src.shape[1]
    workers = SC_CORES * SC_SUBCORES
    k = SC_FETCH_K
    nch = m // (workers * k)
    assert nch * workers * k == m
    mesh = plsc.VectorSubcoreMesh(core_axis_name="c", subcore_axis_name="s")

    @functools.partial(
        pl.kernel, mesh=mesh,
        out_type=jax.ShapeDtypeStruct((m, d), src.dtype),
        scratch_types=[pltpu.VMEM((nch, k), jnp.int32),
                       pltpu.VMEM((k, d), src.dtype),
                       pltpu.SemaphoreType.DMA],
    )
    def gather(src_hbm, idx_hbm, out_hbm, idx_v, rows_v, sem):
        wid = lax.axis_index("s") * SC_CORES + lax.axis_index("c")
        pltpu.sync_copy(idx_hbm.at[wid], idx_v)

        @pl.loop(0, nch)
        def _(j):
            off = pl.multiple_of((wid * nch + j) * k, k)
            pltpu.async_copy(src_hbm.at[idx_v.at[j]], rows_v, sem).wait()
            pltpu.sync_copy(rows_v, out_hbm.at[pl.ds(off, k)])

    return gather(src, idx.reshape(workers, nch, k))


def _scatter_rows(src, dst0, dst1, nrows):
    t, d = src.shape
    workers = SC_CORES * SC_SUBCORES
    nch = t // (workers * SC_GATHER_K)
    assert nch * workers * SC_GATHER_K == t
    mesh = plsc.VectorSubcoreMesh(core_axis_name="c", subcore_axis_name="s")

    @functools.partial(
        pl.kernel, mesh=mesh,
        out_type=jax.ShapeDtypeStruct((nrows, d), src.dtype),
        scratch_types=[pltpu.VMEM((nch, SC_GATHER_K), jnp.int32),
                       pltpu.VMEM((nch, SC_GATHER_K), jnp.int32),
                       pltpu.VMEM((SC_GATHER_K, d), src.dtype),
                       pltpu.SemaphoreType.DMA((2,))],
    )
    def scatter(src_hbm, d0_hbm, d1_hbm, out_hbm, i0_v, i1_v, rows_v, sem):
        wid = lax.axis_index("s") * SC_CORES + lax.axis_index("c")
        pltpu.sync_copy(d0_hbm.at[wid], i0_v)
        pltpu.sync_copy(d1_hbm.at[wid], i1_v)

        @pl.loop(0, nch)
        def _(j):
            off = pl.multiple_of((wid * nch + j) * SC_GATHER_K, SC_GATHER_K)
            pltpu.sync_copy(src_hbm.at[pl.ds(off, SC_GATHER_K)], rows_v)
            first = pltpu.async_copy(rows_v, out_hbm.at[i0_v.at[j]], sem.at[0])
            second = pltpu.async_copy(rows_v, out_hbm.at[i1_v.at[j]], sem.at[1])
            first.wait()
            second.wait()

    return scatter(src, dst0.reshape(workers, nch, SC_GATHER_K), dst1.reshape(workers, nch, SC_GATHER_K))


def _expert_kernel(be_ref, nused_ref, nvalid_ref, nxt_ref, slot_ref, x_ref, wg_hbm, wu_hbm, wd_hbm, o_ref,
                   wg_f, wu_f, wd_f, wg_s, wu_s, wd_s, sem, *, layer):
    i = pl.program_id(0)
    used = i < nused_ref[0]
    new_expert = jnp.logical_or(i == 0, be_ref[i] != be_ref[jnp.maximum(i - 1, 0)])

    def weight_copies(expert, slot):
        return [pltpu.make_async_copy(w.at[layer, expert], f.at[slot], sem.at[slot, j])
                for j, (w, f) in enumerate(((wg_hbm, wg_f), (wu_hbm, wu_f), (wd_hbm, wd_f)))]

    @pl.when(jnp.logical_and(used, new_expert))
    def _():
        slot = slot_ref[i]

        @pl.when(i == 0)
        def _():
            for c in weight_copies(be_ref[i], slot):
                c.start()

        for c in weight_copies(be_ref[i], slot):
            c.wait()
        wg_s[...] = wg_f[slot].astype(BF16)
        wu_s[...] = wu_f[slot].astype(BF16)
        wd_s[...] = wd_f[slot].astype(BF16)

        @pl.when(nxt_ref[i] >= 0)
        def _():
            for c in weight_copies(nxt_ref[i], 1 - slot):
                c.start()

    def swiglu(rows):
        row = rows.start + lax.broadcasted_iota(jnp.int32, (rows.stop - rows.start, x_ref.shape[1]), 0)
        lo, hi = _unpack_bf16_pair(jnp.where(row < nvalid_ref[i], x_ref[rows, :], jnp.uint32(0)))
        lo = lo.astype(BF16)
        hi = hi.astype(BF16)
        half = lo.shape[1]
        gate = _dot(lo, wg_s[0:half, :]) + _dot(hi, wg_s[half:, :])
        up = _dot(lo, wu_s[0:half, :]) + _dot(hi, wu_s[half:, :])
        o_ref[rows, :] = _pack_bf16_pair(_dot((_silu(gate) * up).astype(BF16), wd_s[...]))

    used = i < nused_ref[0]
    half_rows = MOE_TM // 2

    @pl.when(jnp.logical_and(used, nvalid_ref[i] > half_rows))
    def _():
        swiglu(slice(0, MOE_TM))

    @pl.when(jnp.logical_and(used, nvalid_ref[i] <= half_rows))
    def _():
        swiglu(slice(0, half_rows))
        o_ref[half_rows:, :] = jnp.zeros((MOE_TM - half_rows, o_ref.shape[1]), o_ref.dtype)

    @pl.when(jnp.logical_not(used))
    def _():
        o_ref[...] = jnp.zeros_like(o_ref)


def _experts(xs, blk_e, n_used, n_valid, nxt_e, slot, wg, wu, wd, layer):
    rows, dp = xs.shape
    d = 2 * dp
    nblocks = rows // MOE_TM
    de = wg.shape[3]
    blk = lambda i, *_: (i, 0)
    hbm = pl.BlockSpec(memory_space=pl.ANY)
    grid_spec = pltpu.PrefetchScalarGridSpec(
        num_scalar_prefetch=5,
        grid=(nblocks,),
        in_specs=[pl.BlockSpec((MOE_TM, dp), blk), hbm, hbm, hbm],
        out_specs=pl.BlockSpec((MOE_TM, dp), blk),
        scratch_shapes=[pltpu.VMEM((2, d, de), F32), pltpu.VMEM((2, d, de), F32), pltpu.VMEM((2, de, d), F32),
                        pltpu.VMEM((d, de), BF16), pltpu.VMEM((d, de), BF16), pltpu.VMEM((de, d), BF16),
                        pltpu.SemaphoreType.DMA((2, 3))],
    )
    return pl.pallas_call(
        functools.partial(_expert_kernel, layer=layer),
        out_shape=jax.ShapeDtypeStruct((rows, dp), jnp.uint32),
        grid_spec=grid_spec,
        compiler_params=_cp(("arbitrary",)),
        name="moe_experts",
    )(blk_e, n_used, n_valid, nxt_e, slot, xs, wg, wu, wd)


def _final_kernel(*refs):
    fg_ref, o_ref = refs[-2:]
    for s in range(ROW_SUB):
        y = _moe_residual(*refs[5 * s:5 * s + 5])
        o_ref[s * TM:(s + 1) * TM, :] = y * lax.rsqrt(jnp.mean(y * y, axis=-1, keepdims=True) + EPS) * fg_ref[...]


def _final(xn, rows2, route, mod, final_g, nb, nblk):
    t, d = xn.shape
    nlat = nblk - 1
    assert (nb * nlat) % ROW_SUB == 0

    def blk_specs(s):
        lat = lambda i: ROW_SUB * i + s
        src = lambda i: ((lat(i) // nlat) * nblk + 1 + lat(i) % nlat, 0)
        return [pl.BlockSpec((TM, d), src),
                pl.BlockSpec((TM, d // 2), src),
                pl.BlockSpec((TM, d // 2), lambda i: (src(i)[0] + t // TM, 0)),
                pl.BlockSpec((TM, LANES), src),
                pl.BlockSpec((1, 6, d), lambda i: (lat(i) // nlat, 0, 0))]

    return pl.pallas_call(
        _final_kernel,
        out_shape=jax.ShapeDtypeStruct((nb * nlat * TM, d), F32),
        grid=(nb * nlat // ROW_SUB,),
        in_specs=[sp for s in range(ROW_SUB) for sp in blk_specs(s)] + [pl.BlockSpec((1, d), lambda i: (0, 0))],
        out_specs=pl.BlockSpec((ROW_SUB * TM, d), lambda i: (i, 0)),
        compiler_params=_cp(("parallel",)),
        name="moe_combine_final",
    )(*((xn, rows2, rows2, route, mod) * ROW_SUB), final_g.reshape(1, d))


def _moe(h2, route, wg, wu, wd, layer):
    t, d = h2.shape
    n_slots = 2 * t
    experts = jnp.arange(N_EXPERTS, dtype=F32)[None, :]
    oh0 = (route[:, 0:1] == experts).astype(F32)
    oh1 = (route[:, 1:2] == experts).astype(F32)
    both = (oh0 + oh1).reshape(t // LANES, LANES, N_EXPERTS)
    tri = jnp.tril(jnp.ones((LANES, LANES), F32))
    intra = jnp.einsum("ij,bjk->bik", tri, both)
    blk_tot = intra[:, -1, :]
    blk_cum = jnp.cumsum(blk_tot, axis=0)
    earlier = (intra - both + (blk_cum - blk_tot)[:, None, :]).reshape(t, N_EXPERTS)
    counts = blk_cum[-1].astype(jnp.int32)
    pcounts = (counts + MOE_TM - 1) // MOE_TM * MOE_TM
    pends = jnp.cumsum(pcounts)
    pstarts = pends - pcounts
    base = pstarts.astype(F32)[None, :] + earlier
    dest0 = jnp.sum(oh0 * base, axis=1).astype(jnp.int32)
    dest1 = jnp.sum(oh1 * base, axis=1).astype(jnp.int32)
    nblocks = -(-n_slots // MOE_TM) + N_EXPERTS
    nrows = -(-nblocks * MOE_TM // GATHER_ROWS) * GATHER_ROWS
    nblocks = nrows // MOE_TM
    blk_start = jnp.arange(nblocks, dtype=jnp.int32) * MOE_TM
    blk_e = jnp.minimum(jnp.sum((pends[None, :] <= blk_start[:, None]).astype(jnp.int32), axis=1), N_EXPERTS - 1)
    n_used = (pends[-1] // MOE_TM).astype(jnp.int32).reshape(1)
    n_valid = jnp.clip((pstarts + counts)[blk_e] - blk_start, 0, MOE_TM).astype(jnp.int32)
    ids = jnp.arange(N_EXPERTS, dtype=jnp.int32)
    has = counts > 0
    later = lax.cummin(jnp.where(has, ids, N_EXPERTS)[::-1])[::-1]
    nxt = jnp.concatenate([later[1:], jnp.full((1,), N_EXPERTS, jnp.int32)])
    nxt = jnp.where(nxt >= N_EXPERTS, -1, nxt)
    slot = (jnp.cumsum(has.astype(jnp.int32)) - 1) % 2
    xs = _scatter_rows(h2, dest0, dest1, nrows)
    ys = _experts(xs, blk_e, n_used, n_valid, nxt[blk_e], slot[blk_e], wg, wu, wd, layer)
    fetch = SC_CORES * SC_SUBCORES * SC_FETCH_K
    pad = jnp.zeros((-n_slots % fetch,), jnp.int32)
    return _gather_rows(ys, jnp.concatenate([dest0, dest1, pad]))


def kernel(x, c, ctx, c_ctx, ada_w, ada_b, norm1_g, norm2_g, w_in, w_out, s5_lam_re, s5_lam_im, s5_log_dt, s5_b_re, s5_b_im, s5_c_re, s5_c_im, s5_d, s5_glu_w, s5_glu_b, ga_qn_g, ga_kn_g, ssd_conv_w, ssd_conv_b, ssd_dt_bias, ssd_a_log, ssd_d, ssd_norm_g, wa_sink, moe_coarse_w, moe_coarse_b, moe_fine_w, moe_fine_b, moe_w_gate, moe_w_up, moe_w_down, final_g):
    nb, l, d = x.shape
    lc = ctx.shape[1]
    depth = ada_w.shape[0]
    assert lc == TM and l % TM == 0 and nb <= SUBLANES - 1 and d == D_MODEL
    s_len = lc + l
    nblk = s_len // TM
    t = nb * s_len

    cc = jnp.zeros((SUBLANES, d), F32).at[:nb].set(c).at[nb].set(c_ctx)
    mods = _ada(cc, ada_w, ada_b).reshape(depth, SUBLANES, 6, d)
    cos_t, sin_t = _rope_tables(lc, l)
    w_packed = jax.vmap(_pack_w_in)(w_in)
    s5_tabs = jax.vmap(_s5_params)(s5_lam_re, s5_lam_im, s5_log_dt, s5_b_re, s5_b_im, s5_c_re, s5_c_im, s5_d)
    wrs, brs = jax.vmap(_pack_router)(moe_coarse_w, moe_coarse_b, moe_fine_w, moe_fine_b)

    src = ("first", x.reshape(nb * l, d), ctx.reshape(nb * lc, d))
    for i in range(depth):
        mod = mods[i]
        (xm, xbc, ug, z, dt, gaq, gak, gav, waq, wak, wav) = _inproj(
            src, mod, norm1_g[i], w_packed[i], cos_t, sin_t, ga_qn_g[i], ga_kn_g[i], nb, nblk)
        ys5 = _s5(ug, tuple(tab[i] for tab in s5_tabs), nb, s_len, lc)
        oga, owa = _attn(wa_sink[i], gaq, gak, gav, waq, wak, wav, nb, s_len, lc)
        y0, y1 = _ssd(xbc, dt, ssd_conv_w[i], ssd_conv_b[i], ssd_dt_bias[i], ssd_a_log[i], ssd_d[i], nb, s_len, lc)
        wr, br = wrs[i], brs[i]
        xn, h2, route = _outproj(xm, ys5, oga, y0, y1, z, owa, mod, s5_glu_w[i], s5_glu_b[i], ssd_norm_g[i],
                                 w_out[i], norm2_g[i], wr, br, nb, nblk)
        rows2 = _moe(h2, route, moe_w_gate, moe_w_up, moe_w_down, i)
        src = ("moe", xn, rows2, route, mod)
    return _final(xn, rows2, route, mod, final_g, nb, nblk).reshape(nb, l, d)
```

```python
import functools
import math

import jax
import jax.numpy as jnp
import numpy as np
from jax import lax
from jax.experimental import pallas as pl
from jax.experimental.pallas import tpu as pltpu
from jax.experimental.pallas import tpu_sc as plsc

F32 = jnp.float32
BF16 = jnp.bfloat16
HI = lax.Precision.HIGHEST

D_MODEL = 1024
GRID_W = 64
GROUP_W = 256
HEAD_DIM = 64
ROPE_FREQS = HEAD_DIM // 4
ROPE_BASE = 10000.0
EPS = 1e-6
S5_CH = 16
S5_GROUPS = GROUP_W // S5_CH
S5_STATE = 64
N_HEADS = 4
SSD_HEADS = 4
SSD_NGROUPS = 2
SSD_STATE = 128
SSD_XBC = GROUP_W + 2 * SSD_NGROUPS * SSD_STATE
WINDOW = 128
MOE_GROUPS = 4
MOE_PER_GROUP = 8
N_EXPERTS = 32
D_EXPERT = D_MODEL // 2

LANES = 128
SUBLANES = 8
TM = 256
TQ = 128
GA_TQ = 128
GA_SUB = 2
S5_Q = 32
S5_BLK = S5_Q * S5_CH
MOE_TM = 512
SC_CORES = 2
SC_SUBCORES = 16
SC_GATHER_K = 32
SC_FETCH_K = 64
GATHER_ROWS = SC_CORES * SC_SUBCORES * SC_GATHER_K
ROUTE_FINE0 = 32
VMEM_LIMIT = 56 * 1024 * 1024

NEG_INF = float("-inf")
LOG2E = math.log2(math.e)


def _cp(sem, vmem=VMEM_LIMIT):
    return pltpu.CompilerParams(dimension_semantics=sem, vmem_limit_bytes=vmem)


def _dot(a, b):
    return jnp.dot(a, b, preferred_element_type=F32)


def _dot_hi(a, b):
    return jnp.dot(a, b, preferred_element_type=F32, precision=HI)


def _dot_nt(a, b):
    return lax.dot_general(a, b, (((1,), (1,)), ((), ())), preferred_element_type=F32)


def _sigmoid(x):
    return 1.0 / (1.0 + jnp.exp(-x))


def _silu(x):
    return x * _sigmoid(x)


def _gelu_tanh(x):
    return 0.5 * x * (1.0 + jnp.tanh(math.sqrt(2.0 / math.pi) * (x + 0.044715 * (x * x * x))))


def _softplus(x):
    return jnp.maximum(x, 0.0) + jnp.log(1.0 + jnp.exp(-jnp.abs(x)))


_HI16 = 0xFFFF0000


def _pack_bf16_pair(x):
    n = x.shape[1] // 2
    bits = pltpu.bitcast(x.astype(BF16).astype(F32), jnp.uint32)
    return (bits[:, n:] & jnp.uint32(_HI16)) | (bits[:, :n] >> 16)


def _unpack_bf16_pair(w):
    return pltpu.bitcast(w << 16, F32), pltpu.bitcast(w & jnp.uint32(_HI16), F32)


def _per_head_cols(v, base, n_heads, shape):
    lane = lax.broadcasted_iota(jnp.int32, shape, 1)
    out = jnp.broadcast_to(v[:, base + n_heads - 1:base + n_heads], shape)
    for h in range(n_heads - 2, -1, -1):
        out = jnp.where(lane < (h + 1) * HEAD_DIM, v[:, base + h:base + h + 1], out)
    return out


def _ada_kernel(c_ref, w_ref, b_ref, o_ref):
    c = c_ref[...]
    o_ref[0] = _dot_hi(_silu(c), w_ref[0]) + b_ref[0]


def _ada(cc, ada_w, ada_b):
    depth, d, n = ada_w.shape
    tn = 1536
    return pl.pallas_call(
        _ada_kernel,
        out_shape=jax.ShapeDtypeStruct((depth, SUBLANES, n), F32),
        grid=(depth, n // tn),
        in_specs=[pl.BlockSpec((SUBLANES, d), lambda l, j: (0, 0)),
                  pl.BlockSpec((1, d, tn), lambda l, j: (l, 0, j)),
                  pl.BlockSpec((1, 1, tn), lambda l, j: (l, 0, j))],
        out_specs=pl.BlockSpec((1, SUBLANES, tn), lambda l, j: (l, 0, j)),
        compiler_params=_cp(("parallel", "parallel")),
        name="ada_mod",
    )(cc, ada_w, ada_b.reshape(depth, 1, n))


_C_XBC = 0
_C_U = _C_XBC + SSD_XBC
_C_Z = _C_U + GROUP_W
_C_DT = _C_Z + GROUP_W
_C_GAQ = _C_DT + LANES
_C_WAQ = _C_GAQ + N_HEADS * LANES
_C_GAK = _C_WAQ + N_HEADS * LANES
_C_GAV = _C_GAK + LANES
_C_WAK = _C_GAV + LANES
_C_WAV = _C_WAK + LANES
_C_END = _C_WAV + LANES


def _expand_q_cols(wq):
    zero = jnp.zeros((wq.shape[0], HEAD_DIM), wq.dtype)
    parts = []
    for h in range(N_HEADS):
        head = wq[:, h * HEAD_DIM:(h + 1) * HEAD_DIM]
        parts += [head, zero] if h // 2 == 0 else [zero, head]
    return jnp.concatenate(parts, axis=1)


def _pack_w_in(w_in):
    cuts = np.cumsum([256, 256, 128, 128, 256, SSD_XBC, 2 * SSD_HEADS, 256, 128, 128])[:-1]
    u, gaq, gak, gav, z, xbc, dt, waq, wak, wav = jnp.split(w_in, [int(c) for c in cuts], axis=1)
    dt = jnp.pad(dt, ((0, 0), (0, LANES - dt.shape[1])))
    w = jnp.concatenate([xbc, u, z, dt, _expand_q_cols(gaq), _expand_q_cols(waq), gak, gav, wak, wav], axis=1)
    return w.astype(BF16)


def _rope(x, cos, sins):
    w = x.shape[1]
    if w > LANES:
        cos = jnp.concatenate([cos] * (w // LANES), axis=1)
        sins = jnp.concatenate([sins] * (w // LANES), axis=1)
    lane = lax.broadcasted_iota(jnp.int32, x.shape, 1)
    up = pltpu.roll(x, w - ROPE_FREQS, 1)
    dn = pltpu.roll(x, ROPE_FREQS, 1)
    partner = jnp.where((lane & ROPE_FREQS) == 0, up, dn)
    return x * cos + partner * sins


def _v_with_ones(v):
    lo = lax.broadcasted_iota(jnp.int32, v.shape, 1) < HEAD_DIM
    return jnp.concatenate([jnp.where(lo, v, 1.0), jnp.where(lo, 1.0, v)], axis=1).astype(BF16)


def _moe_residual(xn_ref, r0_ref, r1_ref, route_ref, mod_ref):
    route = route_ref[...]
    r0 = jnp.concatenate(_unpack_bf16_pair(r0_ref[...]), axis=1)
    r1 = jnp.concatenate(_unpack_bf16_pair(r1_ref[...]), axis=1)
    return xn_ref[...] + mod_ref[0, 5:6, :] * (route[:, 2:3] * r0 + route[:, 3:4] * r1)


ROW_SUB = 2


def _row_views(refs, s):
    return [r.at[pl.ds(s * TM, TM), :] for r in refs]


def _inproj_kernel(*refs, first, nblk):
    n_blk_in = (2 if first else 5) + 3
    shared = refs[ROW_SUB * n_blk_in:]
    g_ref, w_ref, qn_ref, kn_ref = shared[:4]
    xm_o, xbc_o, ug_o = shared[4:7]
    rest_o = shared[7:-1]
    u_scr = shared[-1]
    for s in range(ROW_SUB):
        blk_refs = refs[s * n_blk_in:(s + 1) * n_blk_in]
        xm_v, xbc_v = _row_views((xm_o, xbc_o), s)
        ug_v = ug_o.at[:, pl.ds(s * S5_TB, S5_TB), :]
        _inproj_block(blk_refs, g_ref, w_ref, qn_ref, kn_ref, xm_v, xbc_v, ug_v, _row_views(rest_o, s), u_scr,
                      first, (pl.program_id(0) * ROW_SUB + s) % nblk == 0)


def _inproj_block(blk_refs, g_ref, w_ref, qn_ref, kn_ref, xm_o, xbc_o, ug_o, rest_o, u_scr, first, is_ctx):
    if first:
        lat_ref, ctx_ref = blk_refs[:2]
        x = jnp.where(is_ctx, ctx_ref[...], lat_ref[...])
    else:
        x = _moe_residual(*blk_refs[:5])
    mod_ref, cos_ref, sin_ref = blk_refs[-3:]
    z_o, dt_o, gaq_o, gak_o, gav_o, waq_o, wak_o, wav_o = rest_o
    xm_o[...] = x
    ms = jnp.mean(x * x, axis=-1, keepdims=True)
    xn = x * lax.rsqrt(ms + EPS) * g_ref[...]
    h = xn * (1.0 + mod_ref[0, 1:2, :]) + mod_ref[0, 0:1, :]
    hb = h.astype(BF16)

    def proj(lo, hi):
        return _dot(hb, w_ref[:, lo:hi])

    cos = cos_ref[...]
    sins = sin_ref[...]
    scale = LOG2E * HEAD_DIM ** -0.5
    q = proj(_C_GAQ, _C_WAQ)
    qs = q * q
    inv = jnp.concatenate(
        [jnp.broadcast_to(lax.rsqrt(jnp.sum(qs[:, s * LANES:(s + 1) * LANES], axis=1, keepdims=True)
                                    * (1.0 / HEAD_DIM) + EPS), (q.shape[0], LANES)) for s in range(N_HEADS)], axis=1)
    gaq_o[...] = (_rope(q * inv * qn_ref[...], cos, sins) * scale).astype(BF16)
    waq_o[...] = (_rope(proj(_C_WAQ, _C_GAK), cos, sins) * scale).astype(BF16)
    k = proj(_C_GAK, _C_GAV)
    ks = k * k
    lane = lax.broadcasted_iota(jnp.int32, k.shape, 1)
    lo = lane < HEAD_DIM
    ms0 = jnp.sum(jnp.where(lo, ks, 0.0), axis=1, keepdims=True)
    ms1 = jnp.sum(jnp.where(lo, 0.0, ks), axis=1, keepdims=True)
    kinv = lax.rsqrt(jnp.where(lo, ms0, ms1) * (1.0 / HEAD_DIM) + EPS)
    gak_o[...] = _rope(k * kinv * kn_ref[...], cos, sins).astype(BF16)
    gav_o[...] = _v_with_ones(proj(_C_GAV, _C_WAK))
    wak_o[...] = _rope(proj(_C_WAK, _C_WAV), cos, sins).astype(BF16)
    wav_o[...] = _v_with_ones(proj(_C_WAV, _C_END))
    xbc_o[...] = proj(_C_XBC, _C_U)
    u = proj(_C_U, _C_Z)
    u_scr[0] = u[:, :LANES]
    u_scr[1] = u[:, LANES:]
    _s5_pack_kernel(u_scr.at[0], u_scr.at[1], ug_o)
    z_o[...] = proj(_C_Z, _C_DT)
    dt_o[...] = proj(_C_DT, _C_GAQ)


def _mod_row(i, nblk, nb):
    return jnp.where(i % nblk == 0, nb, i // nblk)


def _inproj(src, mod, norm_g, w_packed, cos_t, sin_t, qn_g, kn_g, nb, nblk):
    first = src[0] == "first"
    d = src[1].shape[1]
    t = nb * nblk * TM
    row = lambda i: (i, 0)
    fix = lambda i: (0, 0)
    nsteps = t // (ROW_SUB * TM)
    assert nsteps * ROW_SUB * TM == t

    def blk_specs(s):
        bid = lambda i: ROW_SUB * i + s
        modspec = pl.BlockSpec((1, 6, d), lambda i: (_mod_row(bid(i), nblk, nb), 0, 0))
        table = pl.BlockSpec((TM, LANES), lambda i: (bid(i) % nblk, 0))
        if first:
            srcs = [pl.BlockSpec((TM, d), lambda i: ((bid(i) // nblk) * (nblk - 1) + jnp.maximum(bid(i) % nblk - 1, 0), 0)),
                    pl.BlockSpec((TM, d), lambda i: (bid(i) // nblk, 0))]
        else:
            srcs = [pl.BlockSpec((TM, d), lambda i: (bid(i), 0)), pl.BlockSpec((TM, d // 2), lambda i: (bid(i), 0)),
                    pl.BlockSpec((TM, d // 2), lambda i: (bid(i) + t // TM, 0)),
                    pl.BlockSpec((TM, LANES), lambda i: (bid(i), 0)), modspec]
        return srcs + [modspec, table, table]

    if first:
        blk_args = tuple(src[1:]) + (mod, cos_t, sin_t)
    else:
        blk_args = (src[1], src[2], src[2], src[3], src[4], mod, cos_t, sin_t)
    outs = [(d, F32), (SSD_XBC, F32), None, (GROUP_W, F32), (LANES, F32),
            (N_HEADS * LANES, BF16), (LANES, BF16), (2 * LANES, BF16),
            (N_HEADS * LANES, BF16), (LANES, BF16), (2 * LANES, BF16)]
    shapes = [jax.ShapeDtypeStruct((t, o[0]), o[1]) if o else
              jax.ShapeDtypeStruct((S5_GROUPS, t // S5_Q, S5_BLK), F32) for o in outs]
    specs = [pl.BlockSpec((ROW_SUB * TM, o[0]), row) if o else
             pl.BlockSpec((S5_GROUPS, ROW_SUB * S5_TB, S5_BLK), lambda i: (0, i, 0)) for o in outs]
    return pl.pallas_call(
        functools.partial(_inproj_kernel, first=first, nblk=nblk),
        out_shape=shapes,
        grid=(nsteps,),
        in_specs=[sp for s in range(ROW_SUB) for sp in blk_specs(s)] + [
                  pl.BlockSpec((1, d), fix),
                  pl.BlockSpec((d, _C_END), fix),
                  pl.BlockSpec((1, N_HEADS * LANES), fix),
                  pl.BlockSpec((1, LANES), fix)],
        out_specs=specs,
        scratch_shapes=[pltpu.VMEM((GROUP_W // LANES, TM, LANES), F32)],
        compiler_params=_cp(("parallel",)),
        name="in_proj",
    )(*(blk_args * ROW_SUB), norm_g.reshape(1, d), w_packed,
      jnp.tile(qn_g, 2 * N_HEADS).reshape(1, -1), jnp.tile(kn_g, 2).reshape(1, -1))


def _rope_tables(lc, l):
    n_rows = l // GRID_W
    rows = np.repeat(np.arange(n_rows), GRID_W)
    cols = np.tile(np.arange(GRID_W), n_rows)
    inv = np.power(np.float32(ROPE_BASE), -np.arange(ROPE_FREQS, dtype=np.float32) / ROPE_FREQS)
    ang = np.stack([rows, cols], axis=-1).astype(np.float32)[..., None] * inv
    cos = np.cos(ang)
    sin = np.sin(ang)
    cos64 = np.stack([cos, cos], axis=2).reshape(l, HEAD_DIM)
    sin64 = np.stack([-sin, sin], axis=2).reshape(l, HEAD_DIM)
    cos64 = np.concatenate([np.ones((lc, HEAD_DIM), np.float32), cos64], axis=0)
    sin64 = np.concatenate([np.zeros((lc, HEAD_DIM), np.float32), sin64], axis=0)
    return (jnp.asarray(np.tile(cos64, (1, 2)), dtype=F32), jnp.asarray(np.tile(sin64, (1, 2)), dtype=F32))


def _merge_heads(o2, kvh):
    tq = o2.shape[0] // 2
    oa, ob = o2[:tq], o2[tq:]
    lane = lax.broadcasted_iota(jnp.int32, oa.shape, 1)
    if kvh == 0:
        return jnp.where(lane < HEAD_DIM, oa, pltpu.roll(ob, HEAD_DIM, 1))
    return jnp.where(lane < HEAD_DIM, pltpu.roll(oa, HEAD_DIM, 1), ob)


def _stack_q(q_ref, rows, kvh):
    return jnp.concatenate([q_ref[rows, (2 * kvh) * LANES:(2 * kvh + 1) * LANES],
                            q_ref[rows, (2 * kvh + 1) * LANES:(2 * kvh + 2) * LANES]], axis=0)


def _ga_attend(q_ref, k_ref, v_ref, o_ref, nkeys):
    k = k_ref[0:nkeys, :]
    for sub in range(GA_SUB):
        rows = slice(sub * GA_TQ, (sub + 1) * GA_TQ)
        scores = [_dot_nt(_stack_q(q_ref, rows, kvh), k) for kvh in range(2)]
        outs = []
        for kvh in range(2):
            s = scores[kvh]
            p = jnp.exp2((s - jnp.max(s, axis=1, keepdims=True)).astype(BF16))
            o2 = _dot(p, v_ref[0:nkeys, kvh * LANES:(kvh + 1) * LANES])
            outs.append(_merge_heads(o2 / pltpu.roll(o2, HEAD_DIM, 1), kvh))
        o_ref[rows, :] = jnp.concatenate(outs, axis=1)


def _attn_kernel(sink_ref, gq_ref, gk_ref, gv_ref, wq_ref, wk_ref, wv_ref, og_ref, ow_ref, *, lc):
    is_ctx = pl.program_id(1) < lc // TM

    @pl.when(is_ctx)
    def _():
        _ga_attend(gq_ref, gk_ref, gv_ref, og_ref, lc)
        _wa_attend(sink_ref, wq_ref, wk_ref, wv_ref, ow_ref, lc)

    @pl.when(jnp.logical_not(is_ctx))
    def _():
        _ga_attend(gq_ref, gk_ref, gv_ref, og_ref, gk_ref.shape[0])
        _wa_attend(sink_ref, wq_ref, wk_ref, wv_ref, ow_ref, lc)


def _attn(sink, gq, gk, gv, wq, wk, wv, nb, s_len, lc):
    t = gq.shape[0]
    nq = s_len // TM
    assert GA_SUB * GA_TQ == TM and WA_SUB * TQ == TM
    qspec = pl.BlockSpec((TM, N_HEADS * LANES), lambda b, j: (b * nq + j, 0))
    kspec = pl.BlockSpec((s_len, LANES), lambda b, j: (b, 0))
    vspec = pl.BlockSpec((s_len, 2 * LANES), lambda b, j: (b, 0))
    ospec = pl.BlockSpec((TM, GROUP_W), lambda b, j: (b * nq + j, 0))
    return pl.pallas_call(
        functools.partial(_attn_kernel, lc=lc),
        out_shape=[jax.ShapeDtypeStruct((t, GROUP_W), F32)] * 2,
        grid=(nb, nq),
        in_specs=[pl.BlockSpec(memory_space=pltpu.SMEM), qspec, kspec, vspec, qspec, kspec, vspec],
        out_specs=[ospec, ospec],
        compiler_params=_cp(("parallel", "arbitrary")),
        name="attention",
    )(sink, gq, gk, gv, wq, wk, wv)


WA_SUB = TM // TQ


def _wa_attend(sink_ref, q_ref, k_ref, v_ref, o_ref, lc):
    s_len = k_ref.shape[0]
    kc = k_ref[0:lc, :]
    row = lax.broadcasted_iota(jnp.int32, (2 * TQ, 1), 0)
    for sub in range(WA_SUB):
        rows = slice(sub * TQ, (sub + 1) * TQ)
        n = pl.program_id(1) * WA_SUB + sub - lc // TQ
        start = pl.multiple_of(jnp.clip(lc + (n - 1) * TQ, lc, s_len - 3 * TQ), TQ)
        kb = k_ref[pl.ds(start, 3 * TQ), :]
        qpos = n * TQ + lax.broadcasted_iota(jnp.int32, (TQ, 3 * TQ), 0)
        kpos = (start - lc) + lax.broadcasted_iota(jnp.int32, (TQ, 3 * TQ), 1)
        reach = jnp.where(n >= 0, WINDOW, -1)
        valid = jnp.abs(qpos - kpos) <= reach
        valid = jnp.concatenate([valid, valid], axis=0)
        outs = []
        for kvh in range(2):
            q2 = jnp.concatenate([q_ref[rows, (2 * kvh) * LANES:(2 * kvh + 1) * LANES],
                                  q_ref[rows, (2 * kvh + 1) * LANES:(2 * kvh + 2) * LANES]], axis=0)
            sc = _dot_nt(q2, kc)
            sb = jnp.where(valid, _dot_nt(q2, kb), NEG_INF)
            sink = jnp.where(row < TQ, sink_ref[2 * kvh], sink_ref[2 * kvh + 1]) * LOG2E
            m = jnp.maximum(jnp.maximum(jnp.max(sc, axis=1, keepdims=True), jnp.max(sb, axis=1, keepdims=True)), sink)
            pc = jnp.exp2((sc - m).astype(BF16))
            pb = jnp.exp2((sb - m).astype(BF16))
            vcols = slice(kvh * LANES, (kvh + 1) * LANES)
            o2 = _dot(pc, v_ref[0:lc, vcols]) + _dot(pb, v_ref[pl.ds(start, 3 * TQ), vcols])
            denom = pltpu.roll(o2, HEAD_DIM, 1) + jnp.exp2(sink - m)
            outs.append(_merge_heads(o2 / denom, kvh))
        o_ref[rows, :] = jnp.concatenate(outs, axis=1)


def _s5_chunk_index(t, rev, nc_ctx, nc_tot):
    if not rev:
        return t
    return jnp.where(t < nc_ctx, nc_ctx - 1 - t, nc_tot - 1 - (t - nc_ctx))


def _s5_kernel(u_ref, k_ref, p_ref, g_ref, ar_ref, ai_ref, dsk_ref, y_ref, s_scr, h_scr, m_scr, *, nb, nc_ctx, nc_tot):
    for d in range(2):
        ext = k_ref[d, 0]
        for s in range(S5_Q):
            lo = ((S5_Q - s) if d == 0 else (S5_Q - 1 - s)) * S5_CH
            win = pltpu.roll(ext, (2 * S5_BLK - lo) % (2 * S5_BLK), 1)[:, :S5_BLK]
            m_scr[d, s * S5_CH:(s + 1) * S5_CH, :] = win.astype(BF16)
    uf = u_ref[0]
    u = uf.astype(BF16)
    for d in range(2):
        for k in range(2):
            s_scr[d, k] = _dot(u, p_ref[d, k, 0])
    ar = [jnp.broadcast_to(ar_ref[d, 0], (nb, LANES)) for d in range(2)]
    ai = [[jnp.broadcast_to(ai_ref[d, k, 0], (nb, LANES)) for k in range(2)] for d in range(2)]

    def body(t, carry):
        out = []
        for d in range(2):
            h, hs = carry[d]
            rows = pl.ds(_s5_chunk_index(t, d == 1, nc_ctx, nc_tot), nb, stride=nc_tot)
            h_scr[d, rows, :] = h
            out.append((ar[d] * h + ai[d][0] * hs + s_scr[d, 0, rows, :],
                        ar[d] * hs + ai[d][1] * h + s_scr[d, 1, rows, :]))
        return tuple(out)

    zero = jnp.zeros((nb, LANES), F32)
    lax.fori_loop(0, nc_tot, body, ((zero, zero), (zero, zero)), unroll=2)
    y = uf * dsk_ref[0]
    for d in range(2):
        y = y + _dot(u, m_scr[d]) + _dot(h_scr[d].astype(BF16), g_ref[d, 0])
    y_ref[0] = y


S5_TB = TM // S5_Q
S5_GPS = LANES // S5_CH


def _s5_pack_kernel(lo_ref, hi_ref, o_ref):
    for s in range(S5_Q):
        rows = pl.ds(s, S5_TB, stride=S5_Q)
        halves = (lo_ref[rows, :], hi_ref[rows, :])
        dst = S5_CH * (s % S5_GPS)
        for g in range(S5_GROUPS):
            slab = halves[g // S5_GPS]
            src = S5_CH * (g % S5_GPS)
            moved = slab if src == dst else pltpu.roll(slab, (dst - src) % LANES, 1)
            o_ref[g, :, s * S5_CH:(s + 1) * S5_CH] = moved[:, dst:dst + S5_CH]


def _s5_unpack_kernel(y_ref, o_ref):
    lane_grp = lax.broadcasted_iota(jnp.int32, (S5_TB, LANES), 1) // S5_CH
    for s in range(S5_Q):
        src = S5_CH * (s % S5_GPS)
        for half in range(S5_GROUPS // S5_GPS):
            acc = None
            for gl in range(S5_GPS):
                slab = y_ref[half * S5_GPS + gl, :, (s // S5_GPS) * LANES:(s // S5_GPS + 1) * LANES]
                dst = S5_CH * gl
                moved = slab if src == dst else pltpu.roll(slab, (dst - src) % LANES, 1)
                acc = moved if acc is None else jnp.where(lane_grp == gl, moved, acc)
            o_ref[half, pl.ds(s, S5_TB, stride=S5_Q), :] = acc


def _s5_params(lam_re, lam_im, log_dt, b_re, b_im, c_re, c_im, d_skip):
    q = S5_Q
    dt = jnp.exp(log_dt)[..., None]
    lr, li = lam_re, lam_im
    mag = jnp.exp(lr * dt)
    a_re = mag * jnp.cos(li * dt)
    a_im = mag * jnp.sin(li * dt)
    den = lr * lr + li * li
    f_re = ((a_re - 1.0) * lr + a_im * li) / den
    f_im = (a_im * lr - (a_re - 1.0) * li) / den
    bb_re = f_re[..., None] * b_re - f_im[..., None] * b_im
    bb_im = f_re[..., None] * b_im + f_im[..., None] * b_re
    kk = jnp.arange(q + 1, dtype=F32)[:, None, None, None]
    pmag = jnp.exp(kk * (lr * dt))
    pw_re = pmag * jnp.cos(kk * (li * dt))
    pw_im = pmag * jnp.sin(kk * (li * dt))
    lw_re = pw_re[:q].transpose(1, 2, 0, 3)[:, :, :, None, :]
    lw_im = pw_im[:q].transpose(1, 2, 0, 3)[:, :, :, None, :]
    ck_re = c_re[:, :, None] * lw_re - c_im[:, :, None] * lw_im
    ck_im = c_re[:, :, None] * lw_im + c_im[:, :, None] * lw_re
    ck = jnp.concatenate([ck_re, -ck_im], axis=-1).reshape(2, S5_GROUPS, S5_BLK, 2 * S5_STATE)
    kern_t = jnp.einsum("dgmp,dgpc->dgcm", ck, jnp.concatenate([bb_re, bb_im], axis=2), precision=HI)
    kern_t = kern_t.reshape(2, S5_GROUPS, S5_CH, q, S5_CH)
    zeros = jnp.zeros_like(kern_t)
    bbt_re = bb_re.transpose(0, 1, 3, 2)[:, :, None]
    bbt_im = bb_im.transpose(0, 1, 3, 2)[:, :, None]
    ct_re = c_re.transpose(0, 1, 3, 2)[:, :, :, None, :]
    ct_im = c_im.transpose(0, 1, 3, 2)[:, :, :, None, :]
    ms, ps, gs = [], [], []
    for d in range(2):
        ext = (jnp.concatenate([zeros[d], kern_t[d]], axis=2) if d == 0
               else jnp.concatenate([kern_t[d, :, :, ::-1], zeros[d]], axis=2))
        ext = ext.reshape(S5_GROUPS, S5_CH, 2 * S5_BLK)
        ms.append(ext)
        pidx = (q - 1 - jnp.arange(q)) if d == 0 else jnp.arange(q)
        pr = pw_re[pidx, d].transpose(1, 0, 2)[:, :, None, :]
        pi = pw_im[pidx, d].transpose(1, 0, 2)[:, :, None, :]
        p_re = pr * bbt_re[d] - pi * bbt_im[d]
        p_im = pr * bbt_im[d] + pi * bbt_re[d]
        pd = jnp.stack([jnp.concatenate([p_re, p_im], axis=3), jnp.concatenate([p_im, p_re], axis=3)])
        ps.append(pd.reshape(2, S5_GROUPS, S5_BLK, 2 * S5_STATE))
        gidx = (jnp.arange(q) + 1) if d == 0 else (q - jnp.arange(q))
        gw_re = pw_re[gidx, d].transpose(1, 2, 0)[..., None]
        gw_im = pw_im[gidx, d].transpose(1, 2, 0)[..., None]
        g_re = ct_re[d] * gw_re - ct_im[d] * gw_im
        g_im = ct_re[d] * gw_im + ct_im[d] * gw_re
        gs.append(jnp.concatenate([g_re, -g_im], axis=1).reshape(S5_GROUPS, 2 * S5_STATE, S5_BLK))
    ar = jnp.concatenate([pw_re[q], pw_re[q]], axis=-1)[:, :, None, :]
    ai = jnp.stack([jnp.concatenate([-pw_im[q], pw_im[q]], axis=-1),
                    jnp.concatenate([pw_im[q], -pw_im[q]], axis=-1)], axis=1)[:, :, :, None, :]
    dsk = jnp.tile(d_skip.reshape(S5_GROUPS, 1, S5_CH), (1, 1, q))
    return (jnp.stack(ms), jnp.stack(ps).astype(BF16), jnp.stack(gs).astype(BF16),
            ar.astype(F32), ai.astype(F32), dsk.astype(F32))


def _s5(ug, params, nb, s_len, lc):
    m, p, g, ar, ai, dsk = params
    nc_tot = s_len // S5_Q
    nc_ctx = lc // S5_Q
    r = nb * nc_tot
    return pl.pallas_call(
        functools.partial(_s5_kernel, nb=nb, nc_ctx=nc_ctx, nc_tot=nc_tot),
        out_shape=jax.ShapeDtypeStruct((S5_GROUPS, r, S5_BLK), F32),
        grid=(S5_GROUPS,),
        in_specs=[pl.BlockSpec((1, r, S5_BLK), lambda gi: (gi, 0, 0)),
                  pl.BlockSpec((2, 1, S5_CH, 2 * S5_BLK), lambda gi: (0, gi, 0, 0)),
                  pl.BlockSpec((2, 2, 1, S5_BLK, 2 * S5_STATE), lambda gi: (0, 0, gi, 0, 0)),
                  pl.BlockSpec((2, 1, 2 * S5_STATE, S5_BLK), lambda gi: (0, gi, 0, 0)),
                  pl.BlockSpec((2, 1, 1, 2 * S5_STATE), lambda gi: (0, gi, 0, 0)),
                  pl.BlockSpec((2, 2, 1, 1, 2 * S5_STATE), lambda gi: (0, 0, gi, 0, 0)),
                  pl.BlockSpec((1, 1, S5_BLK), lambda gi: (gi, 0, 0))],
        out_specs=pl.BlockSpec((1, r, S5_BLK), lambda gi: (gi, 0, 0)),
        scratch_shapes=[pltpu.VMEM((2, 2, r, 2 * S5_STATE), F32), pltpu.VMEM((2, r, 2 * S5_STATE), F32),
                        pltpu.VMEM((2, S5_BLK, S5_BLK), BF16)],
        compiler_params=_cp(("parallel",)),
        name="s5_scan",
    )(ug, m, p, g, ar, ai, dsk)


CONV_ROWS = 4 * TM


def _conv_kernel(x_ref, prev_ref, next_ref, w_ref, b_ref, o_ref, *, s_len, lc):
    x = x_ref[...]
    rows = x.shape[0]
    ridx = lax.broadcasted_iota(jnp.int32, x.shape, 0)
    pos = (pl.program_id(0) * rows) % s_len + ridx
    pos = jnp.where(pos >= s_len, pos - s_len, pos)
    seg_first = jnp.logical_or(pos == 0, pos == lc)
    seg_last = jnp.logical_or(pos == lc - 1, pos == s_len - 1)
    xm = jnp.where(ridx == 0, prev_ref[SUBLANES - 1:SUBLANES, :], pltpu.roll(x, 1, 0))
    xp = jnp.where(ridx == rows - 1, next_ref[0:1, :], pltpu.roll(x, rows - 1, 0))
    xm = jnp.where(seg_first, 0.0, xm)
    xp = jnp.where(seg_last, 0.0, xp)
    y = xm * w_ref[0:1, :] + x * w_ref[1:2, :] + xp * w_ref[2:3, :] + b_ref[...]
    o_ref[...] = _silu(y)


def _conv(xbc, w, b, s_len, lc):
    t, c = xbc.shape
    rows = next(r for r in (CONV_ROWS, CONV_ROWS // 2, TM) if t % r == 0)
    per = rows // SUBLANES
    last = t // SUBLANES - 1
    return pl.pallas_call(
        functools.partial(_conv_kernel, s_len=s_len, lc=lc),
        out_shape=jax.ShapeDtypeStruct((t, c), F32),
        grid=(t // rows,),
        in_specs=[pl.BlockSpec((rows, c), lambda i: (i, 0)),
                  pl.BlockSpec((SUBLANES, c), lambda i: (jnp.maximum(i * per - 1, 0), 0)),
                  pl.BlockSpec((SUBLANES, c), lambda i: (jnp.minimum((i + 1) * per, last), 0)),
                  pl.BlockSpec((3, c), lambda i: (0, 0)),
                  pl.BlockSpec((1, c), lambda i: (0, 0))],
        out_specs=pl.BlockSpec((rows, c), lambda i: (i, 0)),
        compiler_params=_cp(("parallel",)),
        name="ssd_conv",
    )(xbc, xbc, xbc, w, b.reshape(1, c))


_X_B = GROUP_W
_X_C = GROUP_W + SSD_NGROUPS * SSD_STATE


def _ssd_kernel(xf_ref, dtf_ref, dttf_ref, xr_ref, dtr_ref, dttr_ref, bias_ref, a_ref, biast_ref, at_ref, dsk_ref,
                yf_ref, yr_ref, stf_ref, str_ref):
    @pl.when(pl.program_id(1) == 0)
    def _():
        stf_ref[...] = jnp.zeros_like(stf_ref)
        str_ref[...] = jnp.zeros_like(str_ref)

    par = (bias_ref[...], a_ref[...], biast_ref[...], at_ref[...], dsk_ref[...])
    for j in range(SSD_SUB):
        rf = slice(j * TQ, (j + 1) * TQ)
        rr = slice((SSD_SUB - 1 - j) * TQ, (SSD_SUB - j) * TQ)
        for b in range(xf_ref.shape[0]):
            yf_ref[b, rf, :] = _ssd_chunk_step(xf_ref[b, rf, :], dtf_ref[b, rf, :], dttf_ref[b, :, rf], par,
                                               stf_ref.at[b], False)
            yr_ref[b, rr, :] = _ssd_chunk_step(xr_ref[b, rr, :], dtr_ref[b, rr, :], dttr_ref[b, :, rr], par,
                                               str_ref.at[b], True)


def _ssd_chunk_step(xc, dt_raw, dtt_raw, par, st_ref, rev):
    bias, a_vec, biast, at_vec, dsk = par
    base = SSD_HEADS if rev else 0
    x = xc[:, 0:GROUP_W]
    dt = _softplus(dt_raw + bias)
    a = dt * a_vec
    dtt = _softplus(dtt_raw + biast)
    at = dtt * at_vec
    ri = lax.broadcasted_iota(jnp.int32, (TQ, TQ), 0)
    ci = lax.broadcasted_iota(jnp.int32, (TQ, TQ), 1)
    causal = (ci >= ri) if rev else (ri >= ci)
    tri = jnp.where(causal, 1.0, 0.0)
    cum_c = _dot_hi(tri, a)
    cum_r = _dot_nt_hi(at, tri)
    edge = 0 if rev else TQ - 1
    tot = cum_c[edge:edge + 1, :]

    shape = (TQ, GROUP_W)
    xdt = x * _per_head_cols(dt, base, SSD_HEADS, shape)
    lane = lax.broadcasted_iota(jnp.int32, shape, 1)
    y = jnp.zeros(shape, F32)
    bmat = [xc[:, _X_B + g * SSD_STATE:_X_B + (g + 1) * SSD_STATE].astype(BF16) for g in range(SSD_NGROUPS)]
    cmat = [xc[:, _X_C + g * SSD_STATE:_X_C + (g + 1) * SSD_STATE].astype(BF16) for g in range(SSD_NGROUPS)]
    cb = [_dot_nt(cmat[g], bmat[g]) for g in range(SSD_NGROUPS)]
    for h in range(SSD_HEADS):
        col = base + h
        seg = jnp.where(causal, cum_c[:, col:col + 1] - cum_r[col:col + 1, :], NEG_INF)
        scores = cb[h // 2] * jnp.exp(seg)
        xh = jnp.where((lane >= h * HEAD_DIM) & (lane < (h + 1) * HEAD_DIM), xdt, 0.0)
        y = y + _dot(scores.astype(BF16), xh.astype(BF16))
    st = st_ref[...]
    yo = jnp.concatenate(
        [_dot_nt(cmat[g], st[g * SSD_STATE:(g + 1) * SSD_STATE].astype(BF16)) for g in range(SSD_NGROUPS)], axis=1)
    y = y + yo * _per_head_cols(jnp.exp(cum_c), base, SSD_HEADS, shape)
    if not rev:
        y = y + x * dsk
    xd = xdt * _per_head_cols(jnp.exp(tot - cum_c), base, SSD_HEADS, shape)
    xdt_t = xd.T.astype(BF16)
    decay = jnp.exp(tot)
    for g in range(SSD_NGROUPS):
        new = _dot(xdt_t[g * SSD_STATE:(g + 1) * SSD_STATE], bmat[g])
        for hh in range(2):
            h = 2 * g + hh
            r0 = h * HEAD_DIM
            st_ref[r0:r0 + HEAD_DIM, :] = (decay[:, base + h:base + h + 1] * st[r0:r0 + HEAD_DIM]
                                           + new[hh * HEAD_DIM:(hh + 1) * HEAD_DIM])
    return y


def _dot_nt_hi(a, b):
    return lax.dot_general(a, b, (((1,), (1,)), ((), ())), preferred_element_type=F32, precision=HI)


def _ssd_chunk(c, rev, nc_ctx, nc_tot):
    if not rev:
        return c
    return jnp.where(c < nc_ctx, nc_ctx - 1 - c, nc_tot - 1 - (c - nc_ctx))


SSD_SUB = TM // TQ
SSD_NB = 4


def _ssd_scan(xc, dt, dtt, bias, a, biast, at, dsk, nb, s_len, lc):
    t = xc.shape[0]
    nblk = s_len // TM
    nctx = lc // TM
    nbs = math.gcd(nb, SSD_NB)
    fix = lambda b, c: (0, 0)
    xc3 = xc.reshape(nb, s_len, SSD_XBC)
    dt3 = dt.reshape(nb, s_len, LANES)

    def rows(rev):
        return lambda b, c: (b, _ssd_chunk(c, rev, nctx, nblk), 0)

    def lanes(rev):
        return lambda b, c: (b, 0, _ssd_chunk(c, rev, nctx, nblk))

    def data_specs(rev):
        return [pl.BlockSpec((nbs, TM, SSD_XBC), rows(rev)), pl.BlockSpec((nbs, TM, LANES), rows(rev)),
                pl.BlockSpec((nbs, SUBLANES, TM), lanes(rev))]

    state = pltpu.VMEM((nbs, SSD_HEADS * HEAD_DIM, SSD_STATE), F32)
    yf, yr = pl.pallas_call(
        _ssd_kernel,
        out_shape=[jax.ShapeDtypeStruct((nb, s_len, GROUP_W), F32)] * 2,
        grid=(nb // nbs, nblk),
        in_specs=data_specs(False) + data_specs(True) + [
            pl.BlockSpec((1, LANES), fix), pl.BlockSpec((1, LANES), fix),
            pl.BlockSpec((SUBLANES, TQ), fix), pl.BlockSpec((SUBLANES, TQ), fix),
            pl.BlockSpec((1, GROUP_W), fix)],
        out_specs=[pl.BlockSpec((nbs, TM, GROUP_W), rows(False)), pl.BlockSpec((nbs, TM, GROUP_W), rows(True))],
        scratch_shapes=[state, state],
        compiler_params=_cp(("parallel", "arbitrary")),
        name="ssd_scan",
    )(xc3, dt3, dtt, xc3, dt3, dtt, bias, a, biast, at, dsk)
    return yf.reshape(t, GROUP_W), yr.reshape(t, GROUP_W)


def _ssd(xbc, dt, conv_w, conv_b, dt_bias, a_log, d_skip, nb, s_len, lc):
    xc = _conv(xbc, conv_w, conv_b, s_len, lc)
    nd = 2 * SSD_HEADS
    dtt = dt[:, :nd].reshape(nb, s_len, nd).transpose(0, 2, 1)
    bias = jnp.pad(dt_bias.reshape(1, nd), ((0, 0), (0, LANES - nd)))
    a = jnp.pad(-jnp.exp(a_log).reshape(1, nd), ((0, 0), (0, LANES - nd)))
    biast = jnp.broadcast_to(dt_bias.reshape(nd, 1), (nd, TQ))
    at = jnp.broadcast_to(-jnp.exp(a_log).reshape(nd, 1), (nd, TQ))
    dsk = jnp.repeat(d_skip, HEAD_DIM).reshape(1, GROUP_W)
    return _ssd_scan(xc, dt, dtt, bias, a, biast, at, dsk, nb, s_len, lc)


def _outproj_kernel(x_ref, ys5_ref, oga_ref, y0_ref, y1_ref, z_ref, owa_ref, *refs):
    mods, shared = refs[:ROW_SUB], refs[ROW_SUB:]
    for s in range(ROW_SUB):
        rows = _row_views((x_ref, oga_ref, y0_ref, y1_ref, z_ref, owa_ref) + tuple(shared[-4:-1]), s)
        _outproj_block(rows[0], ys5_ref.at[:, pl.ds(s * S5_TB, S5_TB), :], *rows[1:6], mods[s], *shared[:-4],
                       *rows[6:], shared[-1])


def _outproj_block(x_ref, ys5_ref, oga_ref, y0_ref, y1_ref, z_ref, owa_ref, mod_ref, gluw_ref, glub_ref,
                   ng_ref, wout_ref, n2_ref, wr_ref, br_ref, xn_o, h2_o, route_o, y_scr):
    _s5_unpack_kernel(ys5_ref, y_scr)
    gl = _gelu_tanh(jnp.concatenate([y_scr[0], y_scr[1]], axis=1))
    a = gl * _sigmoid(_dot(gl.astype(BF16), gluw_ref[...]) + glub_ref[...])
    m = (y0_ref[...] + y1_ref[...]) * _silu(z_ref[...])
    m = m * lax.rsqrt(jnp.mean(m * m, axis=-1, keepdims=True) + EPS) * ng_ref[...]
    w = wout_ref
    mix = (_dot(a.astype(BF16), w[0:GROUP_W, :]) + _dot(oga_ref[...].astype(BF16), w[GROUP_W:2 * GROUP_W, :])
           + _dot(m.astype(BF16), w[2 * GROUP_W:3 * GROUP_W, :]) + _dot(owa_ref[...].astype(BF16), w[3 * GROUP_W:, :]))
    xn = x_ref[...] + mod_ref[0, 2:3, :] * mix
    xn_o[...] = xn
    h2 = xn * lax.rsqrt(jnp.mean(xn * xn, axis=-1, keepdims=True) + EPS) * n2_ref[...]
    h2 = h2 * (1.0 + mod_ref[0, 4:5, :]) + mod_ref[0, 3:4, :]
    h2_o[...] = _pack_bf16_pair(h2)
    h_hi = h2.astype(BF16)
    h_lo = (h2 - h_hi.astype(F32)).astype(BF16)
    logits = _dot(h_hi, wr_ref[0]) + (_dot(h_lo, wr_ref[0]) + _dot(h_hi, wr_ref[1])) + br_ref[...]
    lane = lax.broadcasted_iota(jnp.int32, logits.shape, 1).astype(F32)
    big = float(4 * LANES)
    lcoarse = jnp.where(lane < MOE_GROUPS, logits, NEG_INF)
    mx = jnp.max(lcoarse, axis=1, keepdims=True)
    den = jnp.sum(jnp.exp(lcoarse - mx), axis=1, keepdims=True)
    grp = jnp.min(jnp.where(lcoarse == mx, lane, big), axis=1, keepdims=True)
    pg = 1.0 / den
    lo = ROUTE_FINE0 + grp * MOE_PER_GROUP
    lf = jnp.where(lane >= lo, jnp.where(lane < lo + MOE_PER_GROUP, logits, NEG_INF), NEG_INF)
    v1 = jnp.max(lf, axis=1, keepdims=True)
    i1 = jnp.min(jnp.where(lf == v1, lane, big), axis=1, keepdims=True)
    lf2 = jnp.where(lane == i1, NEG_INF, lf)
    v2 = jnp.max(lf2, axis=1, keepdims=True)
    i2 = jnp.min(jnp.where(lf2 == v2, lane, big), axis=1, keepdims=True)
    e2 = jnp.exp(v2 - v1)
    w1 = pg / (1.0 + e2)
    w2 = w1 * e2
    route = jnp.where(lane == 0, i1 - ROUTE_FINE0,
                      jnp.where(lane == 1, i2 - ROUTE_FINE0,
                                jnp.where(lane == 2, w1, jnp.where(lane == 3, w2, 0.0))))
    route_o[...] = route


def _outproj(x, ys5, oga, y0, y1, z, owa, mod, glu_w, glu_b, ssd_norm_g, w_out, norm2_g, wr, br, nb, nblk):
    t, d = x.shape
    row = lambda i: (i, 0)
    fix = lambda i: (0, 0)
    step = ROW_SUB * TM
    gw = pl.BlockSpec((step, GROUP_W), row)
    wr_hi = wr.astype(BF16)
    mod_specs = [pl.BlockSpec((1, 6, d), lambda i, s=s: (_mod_row(ROW_SUB * i + s, nblk, nb), 0, 0))
                 for s in range(ROW_SUB)]
    return pl.pallas_call(
        _outproj_kernel,
        out_shape=[jax.ShapeDtypeStruct((t, d), F32), jax.ShapeDtypeStruct((t, d // 2), jnp.uint32),
                   jax.ShapeDtypeStruct((t, LANES), F32)],
        grid=(t // step,),
        in_specs=[pl.BlockSpec((step, d), row),
                  pl.BlockSpec((S5_GROUPS, ROW_SUB * S5_TB, S5_BLK), lambda i: (0, i, 0)),
                  gw, gw, gw, gw, gw] + mod_specs + [
                  pl.BlockSpec((GROUP_W, GROUP_W), fix),
                  pl.BlockSpec((1, GROUP_W), fix),
                  pl.BlockSpec((1, GROUP_W), fix),
                  pl.BlockSpec((d, d), fix),
                  pl.BlockSpec((1, d), fix),
                  pl.BlockSpec((2, d, LANES), lambda i: (0, 0, 0)),
                  pl.BlockSpec((1, LANES), fix)],
        out_specs=[pl.BlockSpec((step, d), row), pl.BlockSpec((step, d // 2), row), pl.BlockSpec((step, LANES), row)],
        scratch_shapes=[pltpu.VMEM((GROUP_W // LANES, TM, LANES), F32)],
        compiler_params=_cp(("parallel",)),
        name="out_proj_router",
    )(x, ys5, oga, y0, y1, z, owa, *([mod] * ROW_SUB), glu_w.astype(BF16), glu_b.reshape(1, -1), ssd_norm_g.reshape(1, -1),
      w_out.astype(BF16), norm2_g.reshape(1, -1), jnp.stack([wr_hi, (wr - wr_hi.astype(F32)).astype(BF16)]), br)


def _pack_router(coarse_w, coarse_b, fine_w, fine_b):
    def lanes(coarse, fine):
        gap = jnp.zeros(coarse.shape[:-1] + (ROUTE_FINE0 - MOE_GROUPS,), F32)
        tail = jnp.zeros(coarse.shape[:-1] + (LANES - ROUTE_FINE0 - N_EXPERTS,), F32)
        return jnp.concatenate([coarse, gap, fine, tail], axis=-1)

    return lanes(coarse_w, fine_w), lanes(coarse_b[None, :], fine_b[None, :])


def _gather_rows(src, idx):
    m = idx.shape[0]
    d = src.shape[1]
    workers = SC_CORES * SC_SUBCORES
    k = SC_FETCH_K
    nch = m // (workers * k)
    assert nch * workers * k == m
    mesh = plsc.VectorSubcoreMesh(core_axis_name="c", subcore_axis_name="s")

    @functools.partial(
        pl.kernel, mesh=mesh,
        out_type=jax.ShapeDtypeStruct((m, d), src.dtype),
        scratch_types=[pltpu.VMEM((nch, k), jnp.int32),
                       pltpu.VMEM((k, d), src.dtype),
                       pltpu.SemaphoreType.DMA],
    )
    def gather(src_hbm, idx_hbm, out_hbm, idx_v, rows_v, sem):
        wid = lax.axis_index("s") * SC_CORES + lax.axis_index("c")
        pltpu.sync_copy(idx_hbm.at[wid], idx_v)

        @pl.loop(0, nch)
        def _(j):
            off = pl.multiple_of((wid * nch + j) * k, k)
            pltpu.async_copy(src_hbm.at[idx_v.at[j]], rows_v, sem).wait()
            pltpu.sync_copy(rows_v, out_hbm.at[pl.ds(off, k)])

    return gather(src, idx.reshape(workers, nch, k))


def _scatter_rows(src, dst0, dst1, nrows):
    t, d = src.shape
    workers = SC_CORES * SC_SUBCORES
    nch = t // (workers * SC_GATHER_K)
    assert nch * workers * SC_GATHER_K == t
    mesh = plsc.VectorSubcoreMesh(core_axis_name="c", subcore_axis_name="s")

    @functools.partial(
        pl.kernel, mesh=mesh,
        out_type=jax.ShapeDtypeStruct((nrows, d), src.dtype),
        scratch_types=[pltpu.VMEM((nch, SC_GATHER_K), jnp.int32),
                       pltpu.VMEM((nch, SC_GATHER_K), jnp.int32),
                       pltpu.VMEM((SC_GATHER_K, d), src.dtype),
                       pltpu.SemaphoreType.DMA((2,))],
    )
    def scatter(src_hbm, d0_hbm, d1_hbm, out_hbm, i0_v, i1_v, rows_v, sem):
        wid = lax.axis_index("s") * SC_CORES + lax.axis_index("c")
        pltpu.sync_copy(d0_hbm.at[wid], i0_v)
        pltpu.sync_copy(d1_hbm.at[wid], i1_v)

        @pl.loop(0, nch)
        def _(j):
            off = pl.multiple_of((wid * nch + j) * SC_GATHER_K, SC_GATHER_K)
            pltpu.sync_copy(src_hbm.at[pl.ds(off, SC_GATHER_K)], rows_v)
            first = pltpu.async_copy(rows_v, out_hbm.at[i0_v.at[j]], sem.at[0])
            second = pltpu.async_copy(rows_v, out_hbm.at[i1_v.at[j]], sem.at[1])
            first.wait()
            second.wait()

    return scatter(src, dst0.reshape(workers, nch, SC_GATHER_K), dst1.reshape(workers, nch, SC_GATHER_K))


def _expert_kernel(be_ref, nused_ref, nvalid_ref, nxt_ref, slot_ref, x_ref, wg_hbm, wu_hbm, wd_hbm, o_ref,
                   wg_f, wu_f, wd_f, wg_s, wu_s, wd_s, sem, *, layer):
    i = pl.program_id(0)
    used = i < nused_ref[0]
    new_expert = jnp.logical_or(i == 0, be_ref[i] != be_ref[jnp.maximum(i - 1, 0)])

    def weight_copies(expert, slot):
        return [pltpu.make_async_copy(w.at[layer, expert], f.at[slot], sem.at[slot, j])
                for j, (w, f) in enumerate(((wg_hbm, wg_f), (wu_hbm, wu_f), (wd_hbm, wd_f)))]

    @pl.when(jnp.logical_and(used, new_expert))
    def _():
        slot = slot_ref[i]

        @pl.when(i == 0)
        def _():
            for c in weight_copies(be_ref[i], slot):
                c.start()

        for c in weight_copies(be_ref[i], slot):
            c.wait()
        wg_s[...] = wg_f[slot].astype(BF16)
        wu_s[...] = wu_f[slot].astype(BF16)
        wd_s[...] = wd_f[slot].astype(BF16)

        @pl.when(nxt_ref[i] >= 0)
        def _():
            for c in weight_copies(nxt_ref[i], 1 - slot):
                c.start()

    def swiglu(rows):
        row = rows.start + lax.broadcasted_iota(jnp.int32, (rows.stop - rows.start, x_ref.shape[1]), 0)
        lo, hi = _unpack_bf16_pair(jnp.where(row < nvalid_ref[i], x_ref[rows, :], jnp.uint32(0)))
        lo = lo.astype(BF16)
        hi = hi.astype(BF16)
        half = lo.shape[1]
        gate = _dot(lo, wg_s[0:half, :]) + _dot(hi, wg_s[half:, :])
        up = _dot(lo, wu_s[0:half, :]) + _dot(hi, wu_s[half:, :])
        o_ref[rows, :] = _pack_bf16_pair(_dot((_silu(gate) * up).astype(BF16), wd_s[...]))

    used = i < nused_ref[0]
    half_rows = MOE_TM // 2

    @pl.when(jnp.logical_and(used, nvalid_ref[i] > half_rows))
    def _():
        swiglu(slice(0, MOE_TM))

    @pl.when(jnp.logical_and(used, nvalid_ref[i] <= half_rows))
    def _():
        swiglu(slice(0, half_rows))
        o_ref[half_rows:, :] = jnp.zeros((MOE_TM - half_rows, o_ref.shape[1]), o_ref.dtype)

    @pl.when(jnp.logical_not(used))
    def _():
        o_ref[...] = jnp.zeros_like(o_ref)


def _experts(xs, blk_e, n_used, n_valid, nxt_e, slot, wg, wu, wd, layer):
    rows, dp = xs.shape
    d = 2 * dp
    nblocks = rows // MOE_TM
    de = wg.shape[3]
    blk = lambda i, *_: (i, 0)
    hbm = pl.BlockSpec(memory_space=pl.ANY)
    grid_spec = pltpu.PrefetchScalarGridSpec(
        num_scalar_prefetch=5,
        grid=(nblocks,),
        in_specs=[pl.BlockSpec((MOE_TM, dp), blk), hbm, hbm, hbm],
        out_specs=pl.BlockSpec((MOE_TM, dp), blk),
        scratch_shapes=[pltpu.VMEM((2, d, de), F32), pltpu.VMEM((2, d, de), F32), pltpu.VMEM((2, de, d), F32),
                        pltpu.VMEM((d, de), BF16), pltpu.VMEM((d, de), BF16), pltpu.VMEM((de, d), BF16),
                        pltpu.SemaphoreType.DMA((2, 3))],
    )
    return pl.pallas_call(
        functools.partial(_expert_kernel, layer=layer),
        out_shape=jax.ShapeDtypeStruct((rows, dp), jnp.uint32),
        grid_spec=grid_spec,
        compiler_params=_cp(("arbitrary",)),
        name="moe_experts",
    )(blk_e, n_used, n_valid, nxt_e, slot, xs, wg, wu, wd)


def _final_kernel(*refs):
    fg_ref, o_ref = refs[-2:]
    for s in range(ROW_SUB):
        y = _moe_residual(*refs[5 * s:5 * s + 5])
        o_ref[s * TM:(s + 1) * TM, :] = y * lax.rsqrt(jnp.mean(y * y, axis=-1, keepdims=True) + EPS) * fg_ref[...]


def _final(xn, rows2, route, mod, final_g, nb, nblk):
    t, d = xn.shape
    nlat = nblk - 1
    assert (nb * nlat) % ROW_SUB == 0

    def blk_specs(s):
        lat = lambda i: ROW_SUB * i + s
        src = lambda i: ((lat(i) // nlat) * nblk + 1 + lat(i) % nlat, 0)
        return [pl.BlockSpec((TM, d), src),
                pl.BlockSpec((TM, d // 2), src),
                pl.BlockSpec((TM, d // 2), lambda i: (src(i)[0] + t // TM, 0)),
                pl.BlockSpec((TM, LANES), src),
                pl.BlockSpec((1, 6, d), lambda i: (lat(i) // nlat, 0, 0))]

    return pl.pallas_call(
        _final_kernel,
        out_shape=jax.ShapeDtypeStruct((nb * nlat * TM, d), F32),
        grid=(nb * nlat // ROW_SUB,),
        in_specs=[sp for s in range(ROW_SUB) for sp in blk_specs(s)] + [pl.BlockSpec((1, d), lambda i: (0, 0))],
        out_specs=pl.BlockSpec((ROW_SUB * TM, d), lambda i: (i, 0)),
        compiler_params=_cp(("parallel",)),
        name="moe_combine_final",
    )(*((xn, rows2, rows2, route, mod) * ROW_SUB), final_g.reshape(1, d))


def _moe(h2, route, wg, wu, wd, layer):
    t, d = h2.shape
    n_slots = 2 * t
    experts = jnp.arange(N_EXPERTS, dtype=F32)[None, :]
    oh0 = (route[:, 0:1] == experts).astype(F32)
    oh1 = (route[:, 1:2] == experts).astype(F32)
    both = (oh0 + oh1).reshape(t // LANES, LANES, N_EXPERTS)
    tri = jnp.tril(jnp.ones((LANES, LANES), F32))
    intra = jnp.einsum("ij,bjk->bik", tri, both)
    blk_tot = intra[:, -1, :]
    blk_cum = jnp.cumsum(blk_tot, axis=0)
    earlier = (intra - both + (blk_cum - blk_tot)[:, None, :]).reshape(t, N_EXPERTS)
    counts = blk_cum[-1].astype(jnp.int32)
    pcounts = (counts + MOE_TM - 1) // MOE_TM * MOE_TM
    pends = jnp.cumsum(pcounts)
    pstarts = pends - pcounts
    base = pstarts.astype(F32)[None, :] + earlier
    dest0 = jnp.sum(oh0 * base, axis=1).astype(jnp.int32)
    dest1 = jnp.sum(oh1 * base, axis=1).astype(jnp.int32)
    nblocks = -(-n_slots // MOE_TM) + N_EXPERTS
    nrows = -(-nblocks * MOE_TM // GATHER_ROWS) * GATHER_ROWS
    nblocks = nrows // MOE_TM
    blk_start = jnp.arange(nblocks, dtype=jnp.int32) * MOE_TM
    blk_e = jnp.minimum(jnp.sum((pends[None, :] <= blk_start[:, None]).astype(jnp.int32), axis=1), N_EXPERTS - 1)
    n_used = (pends[-1] // MOE_TM).astype(jnp.int32).reshape(1)
    n_valid = jnp.clip((pstarts + counts)[blk_e] - blk_start, 0, MOE_TM).astype(jnp.int32)
    ids = jnp.arange(N_EXPERTS, dtype=jnp.int32)
    has = counts > 0
    later = lax.cummin(jnp.where(has, ids, N_EXPERTS)[::-1])[::-1]
    nxt = jnp.concatenate([later[1:], jnp.full((1,), N_EXPERTS, jnp.int32)])
    nxt = jnp.where(nxt >= N_EXPERTS, -1, nxt)
    slot = (jnp.cumsum(has.astype(jnp.int32)) - 1) % 2
    xs = _scatter_rows(h2, dest0, dest1, nrows)
    ys = _experts(xs, blk_e, n_used, n_valid, nxt[blk_e], slot[blk_e], wg, wu, wd, layer)
    return _gather_rows(ys, jnp.concatenate([dest0, dest1]))


def kernel(x, c, ctx, c_ctx, ada_w, ada_b, norm1_g, norm2_g, w_in, w_out, s5_lam_re, s5_lam_im, s5_log_dt, s5_b_re, s5_b_im, s5_c_re, s5_c_im, s5_d, s5_glu_w, s5_glu_b, ga_qn_g, ga_kn_g, ssd_conv_w, ssd_conv_b, ssd_dt_bias, ssd_a_log, ssd_d, ssd_norm_g, wa_sink, moe_coarse_w, moe_coarse_b, moe_fine_w, moe_fine_b, moe_w_gate, moe_w_up, moe_w_down, final_g):
    nb, l, d = x.shape
    lc = ctx.shape[1]
    depth = ada_w.shape[0]
    assert lc == TM and l % TM == 0 and nb <= SUBLANES - 1 and d == D_MODEL
    s_len = lc + l
    nblk = s_len // TM
    t = nb * s_len

    cc = jnp.zeros((SUBLANES, d), F32).at[:nb].set(c).at[nb].set(c_ctx)
    mods = _ada(cc, ada_w, ada_b).reshape(depth, SUBLANES, 6, d)
    cos_t, sin_t = _rope_tables(lc, l)
    w_packed = jax.vmap(_pack_w_in)(w_in)
    s5_tabs = jax.vmap(_s5_params)(s5_lam_re, s5_lam_im, s5_log_dt, s5_b_re, s5_b_im, s5_c_re, s5_c_im, s5_d)
    wrs, brs = jax.vmap(_pack_router)(moe_coarse_w, moe_coarse_b, moe_fine_w, moe_fine_b)

    src = ("first", x.reshape(nb * l, d), ctx.reshape(nb * lc, d))
    for i in range(depth):
        mod = mods[i]
        (xm, xbc, ug, z, dt, gaq, gak, gav, waq, wak, wav) = _inproj(
            src, mod, norm1_g[i], w_packed[i], cos_t, sin_t, ga_qn_g[i], ga_kn_g[i], nb, nblk)
        ys5 = _s5(ug, tuple(tab[i] for tab in s5_tabs), nb, s_len, lc)
        oga, owa = _attn(wa_sink[i], gaq, gak, gav, waq, wak, wav, nb, s_len, lc)
        y0, y1 = _ssd(xbc, dt, ssd_conv_w[i], ssd_conv_b[i], ssd_dt_bias[i], ssd_a_log[i], ssd_d[i], nb, s_len, lc)
        wr, br = wrs[i], brs[i]
        xn, h2, route = _outproj(xm, ys5, oga, y0, y1, z, owa, mod, s5_glu_w[i], s5_glu_b[i], ssd_norm_g[i],
                                 w_out[i], norm2_g[i], wr, br, nb, nblk)
        rows2 = _moe(h2, route, moe_w_gate, moe_w_up, moe_w_down, i)
        src = ("moe", xn, rows2, route, mod)
    return _final(xn, rows2, route, mod, final_g, nb, nblk).reshape(nb, l, d)
```

```python
import functools
import math

import jax
import jax.numpy as jnp
import numpy as np
from jax import lax
from jax.experimental import pallas as pl
from jax.experimental.pallas import tpu as pltpu
from jax.experimental.pallas import tpu_sc as plsc

F32 = jnp.float32
BF16 = jnp.bfloat16
HI = lax.Precision.HIGHEST

D_MODEL = 1024
GRID_W = 64
GROUP_W = 256
HEAD_DIM = 64
ROPE_FREQS = HEAD_DIM // 4
ROPE_BASE = 10000.0
EPS = 1e-6
S5_CH = 16
S5_GROUPS = GROUP_W // S5_CH
S5_STATE = 64
N_HEADS = 4
SSD_HEADS = 4
SSD_NGROUPS = 2
SSD_STATE = 128
SSD_XBC = GROUP_W + 2 * SSD_NGROUPS * SSD_STATE
WINDOW = 128
MOE_GROUPS = 4
MOE_PER_GROUP = 8
N_EXPERTS = 32

LANES = 128
SUBLANES = 8
TM = 256
TQ = 128
GA_TQ = 128
GA_SUB = 2
S5_Q = 32
S5_BLK = S5_Q * S5_CH
MOE_TM = 512
SC_CORES = 2
SC_SUBCORES = 16
SC_GATHER_K = 32
SC_FETCH_K = 64
ROUTE_FINE0 = 32
VMEM_LIMIT = 56 * 1024 * 1024

NEG_INF = float("-inf")
LOG2E = math.log2(math.e)


def _cp(sem, vmem=VMEM_LIMIT):
    return pltpu.CompilerParams(dimension_semantics=sem, vmem_limit_bytes=vmem)


def _dot(a, b):
    return jnp.dot(a, b, preferred_element_type=F32)


def _dot_hi(a, b):
    return jnp.dot(a, b, preferred_element_type=F32, precision=HI)


def _dot_nt(a, b):
    return lax.dot_general(a, b, (((1,), (1,)), ((), ())), preferred_element_type=F32)


def _sigmoid(x):
    return 1.0 / (1.0 + jnp.exp(-x))


def _silu(x):
    return x * _sigmoid(x)


def _gelu_tanh(x):
    return 0.5 * x * (1.0 + jnp.tanh(math.sqrt(2.0 / math.pi) * (x + 0.044715 * (x * x * x))))


def _softplus(x):
    return jnp.maximum(x, 0.0) + jnp.log(1.0 + jnp.exp(-jnp.abs(x)))


_HI16 = 0xFFFF0000


def _pack_bf16_pair(x):
    n = x.shape[1] // 2
    bits = pltpu.bitcast(x.astype(BF16).astype(F32), jnp.uint32)
    return (bits[:, n:] & jnp.uint32(_HI16)) | (bits[:, :n] >> 16)


def _unpack_bf16_pair(w):
    return pltpu.bitcast(w << 16, F32), pltpu.bitcast(w & jnp.uint32(_HI16), F32)


def _per_head_cols(v, base, n_heads, shape):
    lane = lax.broadcasted_iota(jnp.int32, shape, 1)
    out = jnp.broadcast_to(v[:, base + n_heads - 1:base + n_heads], shape)
    for h in range(n_heads - 2, -1, -1):
        out = jnp.where(lane < (h + 1) * HEAD_DIM, v[:, base + h:base + h + 1], out)
    return out


def _ada_kernel(c_ref, w_ref, b_ref, o_ref):
    c = c_ref[...]
    o_ref[0] = _dot_hi(_silu(c), w_ref[0]) + b_ref[0]


def _ada(cc, ada_w, ada_b):
    depth, d, n = ada_w.shape
    tn = 1536
    return pl.pallas_call(
        _ada_kernel,
        out_shape=jax.ShapeDtypeStruct((depth, SUBLANES, n), F32),
        grid=(depth, n // tn),
        in_specs=[pl.BlockSpec((SUBLANES, d), lambda l, j: (0, 0)),
                  pl.BlockSpec((1, d, tn), lambda l, j: (l, 0, j)),
                  pl.BlockSpec((1, 1, tn), lambda l, j: (l, 0, j))],
        out_specs=pl.BlockSpec((1, SUBLANES, tn), lambda l, j: (l, 0, j)),
        compiler_params=_cp(("parallel", "parallel")),
        name="ada_mod",
    )(cc, ada_w, ada_b.reshape(depth, 1, n))


_C_XBC = 0
_C_U = _C_XBC + SSD_XBC
_C_Z = _C_U + GROUP_W
_C_DT = _C_Z + GROUP_W
_C_GAQ = _C_DT + LANES
_C_WAQ = _C_GAQ + N_HEADS * LANES
_C_GAK = _C_WAQ + N_HEADS * LANES
_C_GAV = _C_GAK + LANES
_C_WAK = _C_GAV + LANES
_C_WAV = _C_WAK + LANES
_C_END = _C_WAV + LANES


def _expand_q_cols(wq):
    zero = jnp.zeros((wq.shape[0], HEAD_DIM), wq.dtype)
    parts = []
    for h in range(N_HEADS):
        head = wq[:, h * HEAD_DIM:(h + 1) * HEAD_DIM]
        parts += [head, zero] if h // 2 == 0 else [zero, head]
    return jnp.concatenate(parts, axis=1)


def _pack_w_in(w_in):
    cuts = np.cumsum([256, 256, 128, 128, 256, SSD_XBC, 2 * SSD_HEADS, 256, 128, 128])[:-1]
    u, gaq, gak, gav, z, xbc, dt, waq, wak, wav = jnp.split(w_in, [int(c) for c in cuts], axis=1)
    dt = jnp.pad(dt, ((0, 0), (0, LANES - dt.shape[1])))
    w = jnp.concatenate([xbc, u, z, dt, _expand_q_cols(gaq), _expand_q_cols(waq), gak, gav, wak, wav], axis=1)
    return w.astype(BF16)


def _rope(x, cos, sins):
    w = x.shape[1]
    if w > LANES:
        cos = jnp.concatenate([cos] * (w // LANES), axis=1)
        sins = jnp.concatenate([sins] * (w // LANES), axis=1)
    lane = lax.broadcasted_iota(jnp.int32, x.shape, 1)
    up = pltpu.roll(x, w - ROPE_FREQS, 1)
    dn = pltpu.roll(x, ROPE_FREQS, 1)
    partner = jnp.where((lane & ROPE_FREQS) == 0, up, dn)
    return x * cos + partner * sins


def _v_with_ones(v):
    lo = lax.broadcasted_iota(jnp.int32, v.shape, 1) < HEAD_DIM
    return jnp.concatenate([jnp.where(lo, v, 1.0), jnp.where(lo, 1.0, v)], axis=1).astype(BF16)


def _moe_residual(xn_ref, r0_ref, r1_ref, route_ref, mod_ref):
    route = route_ref[...]
    r0 = jnp.concatenate(_unpack_bf16_pair(r0_ref[...]), axis=1)
    r1 = jnp.concatenate(_unpack_bf16_pair(r1_ref[...]), axis=1)
    return xn_ref[...] + mod_ref[0, 5:6, :] * (route[:, 2:3] * r0 + route[:, 3:4] * r1)


ROW_SUB = 4


def _row_views(refs, s):
    return [r.at[pl.ds(s * TM, TM), :] for r in refs]


def _inproj_kernel(*refs, first, nblk):
    n_blk_in = (2 if first else 5) + 3
    shared = refs[ROW_SUB * n_blk_in:]
    g_ref, w_ref, qn_ref, kn_ref = shared[:4]
    xm_o, xbc_o, ug_o = shared[4:7]
    rest_o = shared[7:-1]
    u_scr = shared[-1]
    for s in range(ROW_SUB):
        blk_refs = refs[s * n_blk_in:(s + 1) * n_blk_in]
        xm_v, xbc_v = _row_views((xm_o, xbc_o), s)
        ug_v = ug_o.at[:, pl.ds(s * S5_TB, S5_TB), :]
        _inproj_block(blk_refs, g_ref, w_ref, qn_ref, kn_ref, xm_v, xbc_v, ug_v, _row_views(rest_o, s), u_scr,
                      first, (pl.program_id(0) * ROW_SUB + s) % nblk == 0)


def _inproj_block(blk_refs, g_ref, w_ref, qn_ref, kn_ref, xm_o, xbc_o, ug_o, rest_o, u_scr, first, is_ctx):
    if first:
        lat_ref, ctx_ref = blk_refs[:2]
        x = jnp.where(is_ctx, ctx_ref[...], lat_ref[...])
    else:
        x = _moe_residual(*blk_refs[:5])
    mod_ref, cos_ref, sin_ref = blk_refs[-3:]
    z_o, dt_o, gaq_o, gak_o, gav_o, waq_o, wak_o, wav_o = rest_o
    xm_o[...] = x
    ms = jnp.mean(x * x, axis=-1, keepdims=True)
    xn = x * lax.rsqrt(ms + EPS) * g_ref[...]
    h = xn * (1.0 + mod_ref[0, 1:2, :]) + mod_ref[0, 0:1, :]
    hb = h.astype(BF16)

    def proj(lo, hi):
        return _dot(hb, w_ref[:, lo:hi])

    cos = cos_ref[...]
    sins = sin_ref[...]
    scale = LOG2E * HEAD_DIM ** -0.5
    q = proj(_C_GAQ, _C_WAQ)
    qs = q * q
    inv = jnp.concatenate(
        [jnp.broadcast_to(lax.rsqrt(jnp.sum(qs[:, s * LANES:(s + 1) * LANES], axis=1, keepdims=True)
                                    * (1.0 / HEAD_DIM) + EPS), (q.shape[0], LANES)) for s in range(N_HEADS)], axis=1)
    gaq_o[...] = (_rope(q * inv * qn_ref[...], cos, sins) * scale).astype(BF16)
    waq_o[...] = (_rope(proj(_C_WAQ, _C_GAK), cos, sins) * scale).astype(BF16)
    k = proj(_C_GAK, _C_GAV)
    ks = k * k
    lane = lax.broadcasted_iota(jnp.int32, k.shape, 1)
    lo = lane < HEAD_DIM
    ms0 = jnp.sum(jnp.where(lo, ks, 0.0), axis=1, keepdims=True)
    ms1 = jnp.sum(jnp.where(lo, 0.0, ks), axis=1, keepdims=True)
    kinv = lax.rsqrt(jnp.where(lo, ms0, ms1) * (1.0 / HEAD_DIM) + EPS)
    gak_o[...] = _rope(k * kinv * kn_ref[...], cos, sins).astype(BF16)
    gav_o[...] = _v_with_ones(proj(_C_GAV, _C_WAK))
    wak_o[...] = _rope(proj(_C_WAK, _C_WAV), cos, sins).astype(BF16)
    wav_o[...] = _v_with_ones(proj(_C_WAV, _C_END))
    xbc_o[...] = proj(_C_XBC, _C_U)
    u = proj(_C_U, _C_Z)
    u_scr[0] = u[:, :LANES]
    u_scr[1] = u[:, LANES:]
    _s5_pack(u_scr.at[0], u_scr.at[1], ug_o)
    z_o[...] = proj(_C_Z, _C_DT)
    dt_o[...] = proj(_C_DT, _C_GAQ)


def _mod_row(i, nblk, nb):
    return jnp.where(i % nblk == 0, nb, i // nblk)


def _inproj(src, mod, norm_g, w_packed, cos_t, sin_t, qn_g, kn_g, nb, nblk):
    first = src[0] == "first"
    d = src[1].shape[1]
    t = nb * nblk * TM
    row = lambda i: (i, 0)
    fix = lambda i: (0, 0)
    nsteps = t // (ROW_SUB * TM)
    assert nsteps * ROW_SUB * TM == t

    def blk_specs(s):
        bid = lambda i: ROW_SUB * i + s
        modspec = pl.BlockSpec((1, 6, d), lambda i: (_mod_row(bid(i), nblk, nb), 0, 0))
        table = pl.BlockSpec((TM, LANES), lambda i: (bid(i) % nblk, 0))
        if first:
            srcs = [pl.BlockSpec((TM, d), lambda i: ((bid(i) // nblk) * (nblk - 1) + jnp.maximum(bid(i) % nblk - 1, 0), 0)),
                    pl.BlockSpec((TM, d), lambda i: (bid(i) // nblk, 0))]
        else:
            srcs = [pl.BlockSpec((TM, d), lambda i: (bid(i), 0)), pl.BlockSpec((TM, d // 2), lambda i: (bid(i), 0)),
                    pl.BlockSpec((TM, d // 2), lambda i: (bid(i) + t // TM, 0)),
                    pl.BlockSpec((TM, LANES), lambda i: (bid(i), 0)), modspec]
        return srcs + [modspec, table, table]

    if first:
        blk_args = tuple(src[1:]) + (mod, cos_t, sin_t)
    else:
        blk_args = (src[1], src[2], src[2], src[3], src[4], mod, cos_t, sin_t)
    outs = [(d, F32), (SSD_XBC, F32), None, (GROUP_W, F32), (LANES, F32),
            (N_HEADS * LANES, BF16), (LANES, BF16), (2 * LANES, BF16),
            (N_HEADS * LANES, BF16), (LANES, BF16), (2 * LANES, BF16)]
    shapes = [jax.ShapeDtypeStruct((t, o[0]), o[1]) if o else
              jax.ShapeDtypeStruct((S5_GROUPS, t // S5_Q, S5_BLK), F32) for o in outs]
    specs = [pl.BlockSpec((ROW_SUB * TM, o[0]), row) if o else
             pl.BlockSpec((S5_GROUPS, ROW_SUB * S5_TB, S5_BLK), lambda i: (0, i, 0)) for o in outs]
    return pl.pallas_call(
        functools.partial(_inproj_kernel, first=first, nblk=nblk),
        out_shape=shapes,
        grid=(nsteps,),
        in_specs=[sp for s in range(ROW_SUB) for sp in blk_specs(s)] + [
                  pl.BlockSpec((1, d), fix),
                  pl.BlockSpec((d, _C_END), fix),
                  pl.BlockSpec((1, N_HEADS * LANES), fix),
                  pl.BlockSpec((1, LANES), fix)],
        out_specs=specs,
        scratch_shapes=[pltpu.VMEM((GROUP_W // LANES, TM, LANES), F32)],
        compiler_params=_cp(("parallel",)),
        name="in_proj",
    )(*(blk_args * ROW_SUB), norm_g.reshape(1, d), w_packed,
      jnp.tile(qn_g, 2 * N_HEADS).reshape(1, -1), jnp.tile(kn_g, 2).reshape(1, -1))


def _rope_tables(lc, l):
    n_rows = l // GRID_W
    rows = np.repeat(np.arange(n_rows), GRID_W)
    cols = np.tile(np.arange(GRID_W), n_rows)
    inv = np.power(np.float32(ROPE_BASE), -np.arange(ROPE_FREQS, dtype=np.float32) / ROPE_FREQS)
    ang = np.stack([rows, cols], axis=-1).astype(np.float32)[..., None] * inv
    cos = np.cos(ang)
    sin = np.sin(ang)
    cos64 = np.stack([cos, cos], axis=2).reshape(l, HEAD_DIM)
    sin64 = np.stack([-sin, sin], axis=2).reshape(l, HEAD_DIM)
    cos64 = np.concatenate([np.ones((lc, HEAD_DIM), np.float32), cos64], axis=0)
    sin64 = np.concatenate([np.zeros((lc, HEAD_DIM), np.float32), sin64], axis=0)
    return (jnp.asarray(np.tile(cos64, (1, 2)), dtype=F32), jnp.asarray(np.tile(sin64, (1, 2)), dtype=F32))


def _merge_heads(o2, kvh):
    tq = o2.shape[0] // 2
    oa, ob = o2[:tq], o2[tq:]
    lane = lax.broadcasted_iota(jnp.int32, oa.shape, 1)
    if kvh == 0:
        return jnp.where(lane < HEAD_DIM, oa, pltpu.roll(ob, HEAD_DIM, 1))
    return jnp.where(lane < HEAD_DIM, pltpu.roll(oa, HEAD_DIM, 1), ob)


def _stack_q(q_ref, rows, kvh):
    return jnp.concatenate([q_ref[rows, (2 * kvh) * LANES:(2 * kvh + 1) * LANES],
                            q_ref[rows, (2 * kvh + 1) * LANES:(2 * kvh + 2) * LANES]], axis=0)


def _ga_attend(q_ref, k_ref, v_ref, o_ref, nkeys):
    k = k_ref[0:nkeys, :]
    for sub in range(GA_SUB):
        rows = slice(sub * GA_TQ, (sub + 1) * GA_TQ)
        scores = [_dot_nt(_stack_q(q_ref, rows, kvh), k) for kvh in range(2)]
        outs = []
        for kvh in range(2):
            s = scores[kvh]
            p = jnp.exp2((s - jnp.max(s, axis=1, keepdims=True)).astype(BF16))
            o2 = _dot(p, v_ref[0:nkeys, kvh * LANES:(kvh + 1) * LANES])
            outs.append(_merge_heads(o2 / pltpu.roll(o2, HEAD_DIM, 1), kvh))
        o_ref[rows, :] = jnp.concatenate(outs, axis=1)


def _attn_kernel(sink_ref, gq_ref, gk_ref, gv_ref, wq_ref, wk_ref, wv_ref, og_ref, ow_ref, *, lc):
    is_ctx = pl.program_id(1) < lc // TM

    @pl.when(is_ctx)
    def _():
        _ga_attend(gq_ref, gk_ref, gv_ref, og_ref, lc)
        _wa_attend(sink_ref, wq_ref, wk_ref, wv_ref, ow_ref, lc)

    @pl.when(jnp.logical_not(is_ctx))
    def _():
        _ga_attend(gq_ref, gk_ref, gv_ref, og_ref, gk_ref.shape[0])
        _wa_attend(sink_ref, wq_ref, wk_ref, wv_ref, ow_ref, lc)


def _attn(sink, gq, gk, gv, wq, wk, wv, nb, s_len, lc):
    t = gq.shape[0]
    nq = s_len // TM
    assert GA_SUB * GA_TQ == TM and WA_SUB * TQ == TM
    qspec = pl.BlockSpec((TM, N_HEADS * LANES), lambda b, j: (b * nq + j, 0))
    kspec = pl.BlockSpec((s_len, LANES), lambda b, j: (b, 0))
    vspec = pl.BlockSpec((s_len, 2 * LANES), lambda b, j: (b, 0))
    ospec = pl.BlockSpec((TM, GROUP_W), lambda b, j: (b * nq + j, 0))
    return pl.pallas_call(
        functools.partial(_attn_kernel, lc=lc),
        out_shape=[jax.ShapeDtypeStruct((t, GROUP_W), F32)] * 2,
        grid=(nb, nq),
        in_specs=[pl.BlockSpec(memory_space=pltpu.SMEM), qspec, kspec, vspec, qspec, kspec, vspec],
        out_specs=[ospec, ospec],
        compiler_params=_cp(("parallel", "arbitrary")),
        name="attention",
    )(sink, gq, gk, gv, wq, wk, wv)


WA_SUB = TM // TQ


def _wa_attend(sink_ref, q_ref, k_ref, v_ref, o_ref, lc):
    s_len = k_ref.shape[0]
    kc = k_ref[0:lc, :]
    row = lax.broadcasted_iota(jnp.int32, (2 * TQ, 1), 0)
    for sub in range(WA_SUB):
        rows = slice(sub * TQ, (sub + 1) * TQ)
        n = pl.program_id(1) * WA_SUB + sub - lc // TQ
        start = pl.multiple_of(jnp.clip(lc + (n - 1) * TQ, lc, s_len - 3 * TQ), TQ)
        kb = k_ref[pl.ds(start, 3 * TQ), :]
        qpos = n * TQ + lax.broadcasted_iota(jnp.int32, (TQ, 3 * TQ), 0)
        kpos = (start - lc) + lax.broadcasted_iota(jnp.int32, (TQ, 3 * TQ), 1)
        reach = jnp.where(n >= 0, WINDOW, -1)
        valid = jnp.abs(qpos - kpos) <= reach
        valid = jnp.concatenate([valid, valid], axis=0)
        outs = []
        for kvh in range(2):
            q2 = jnp.concatenate([q_ref[rows, (2 * kvh) * LANES:(2 * kvh + 1) * LANES],
                                  q_ref[rows, (2 * kvh + 1) * LANES:(2 * kvh + 2) * LANES]], axis=0)
            sc = _dot_nt(q2, kc)
            sb = jnp.where(valid, _dot_nt(q2, kb), NEG_INF)
            sink = jnp.where(row < TQ, sink_ref[2 * kvh], sink_ref[2 * kvh + 1]) * LOG2E
            m = jnp.maximum(jnp.maximum(jnp.max(sc, axis=1, keepdims=True), jnp.max(sb, axis=1, keepdims=True)), sink)
            pc = jnp.exp2((sc - m).astype(BF16))
            pb = jnp.exp2((sb - m).astype(BF16))
            vcols = slice(kvh * LANES, (kvh + 1) * LANES)
            o2 = _dot(pc, v_ref[0:lc, vcols]) + _dot(pb, v_ref[pl.ds(start, 3 * TQ), vcols])
            denom = pltpu.roll(o2, HEAD_DIM, 1) + jnp.exp2(sink - m)
            outs.append(_merge_heads(o2 / denom, kvh))
        o_ref[rows, :] = jnp.concatenate(outs, axis=1)


def _s5_chunk_index(t, rev, nc_ctx, nc_tot):
    if not rev:
        return t
    return jnp.where(t < nc_ctx, nc_ctx - 1 - t, nc_tot - 1 - (t - nc_ctx))


def _s5_kernel(u_ref, k_ref, p_ref, g_ref, ar_ref, ai_ref, dsk_ref, y_ref, s_scr, h_scr, m_scr, *, nb, nc_ctx, nc_tot):
    for d in range(2):
        ext = k_ref[d, 0]
        for s in range(S5_Q):
            lo = ((S5_Q - s) if d == 0 else (S5_Q - 1 - s)) * S5_CH
            win = pltpu.roll(ext, (2 * S5_BLK - lo) % (2 * S5_BLK), 1)[:, :S5_BLK]
            m_scr[d, s * S5_CH:(s + 1) * S5_CH, :] = win.astype(BF16)
    uf = u_ref[0]
    u = uf.astype(BF16)
    for d in range(2):
        for k in range(2):
            s_scr[d, k] = _dot(u, p_ref[d, k, 0])
    ar = [jnp.broadcast_to(ar_ref[d, 0], (nb, LANES)) for d in range(2)]
    ai = [[jnp.broadcast_to(ai_ref[d, k, 0], (nb, LANES)) for k in range(2)] for d in range(2)]

    def body(t, carry):
        out = []
        for d in range(2):
            h, hs = carry[d]
            rows = pl.ds(_s5_chunk_index(t, d == 1, nc_ctx, nc_tot), nb, stride=nc_tot)
            h_scr[d, rows, :] = h
            out.append((ar[d] * h + ai[d][0] * hs + s_scr[d, 0, rows, :],
                        ar[d] * hs + ai[d][1] * h + s_scr[d, 1, rows, :]))
        return tuple(out)

    zero = jnp.zeros((nb, LANES), F32)
    lax.fori_loop(0, nc_tot, body, ((zero, zero), (zero, zero)), unroll=2)
    y = uf * dsk_ref[0]
    for d in range(2):
        y = y + _dot(u, m_scr[d]) + _dot(h_scr[d].astype(BF16), g_ref[d, 0])
    y_ref[0] = y


S5_TB = TM // S5_Q
S5_GPS = LANES // S5_CH


def _s5_pack(lo_ref, hi_ref, o_ref):
    for s in range(S5_Q):
        rows = pl.ds(s, S5_TB, stride=S5_Q)
        halves = (lo_ref[rows, :], hi_ref[rows, :])
        dst = S5_CH * (s % S5_GPS)
        for g in range(S5_GROUPS):
            slab = halves[g // S5_GPS]
            src = S5_CH * (g % S5_GPS)
            moved = slab if src == dst else pltpu.roll(slab, (dst - src) % LANES, 1)
            o_ref[g, :, s * S5_CH:(s + 1) * S5_CH] = moved[:, dst:dst + S5_CH]


def _s5_unpack(y_ref, o_ref):
    lane_grp = lax.broadcasted_iota(jnp.int32, (S5_TB, LANES), 1) // S5_CH
    for s in range(S5_Q):
        src = S5_CH * (s % S5_GPS)
        for half in range(S5_GROUPS // S5_GPS):
            acc = None
            for gl in range(S5_GPS):
                slab = y_ref[half * S5_GPS + gl, :, (s // S5_GPS) * LANES:(s // S5_GPS + 1) * LANES]
                dst = S5_CH * gl
                moved = slab if src == dst else pltpu.roll(slab, (dst - src) % LANES, 1)
                acc = moved if acc is None else jnp.where(lane_grp == gl, moved, acc)
            o_ref[half, pl.ds(s, S5_TB, stride=S5_Q), :] = acc


def _s5_params(lam_re, lam_im, log_dt, b_re, b_im, c_re, c_im, d_skip):
    q = S5_Q
    dt = jnp.exp(log_dt)[..., None]
    lr, li = lam_re, lam_im
    mag = jnp.exp(lr * dt)
    a_re = mag * jnp.cos(li * dt)
    a_im = mag * jnp.sin(li * dt)
    den = lr * lr + li * li
    f_re = ((a_re - 1.0) * lr + a_im * li) / den
    f_im = (a_im * lr - (a_re - 1.0) * li) / den
    bb_re = f_re[..., None] * b_re - f_im[..., None] * b_im
    bb_im = f_re[..., None] * b_im + f_im[..., None] * b_re
    kk = jnp.arange(q + 1, dtype=F32)[:, None, None, None]
    pmag = jnp.exp(kk * (lr * dt))
    pw_re = pmag * jnp.cos(kk * (li * dt))
    pw_im = pmag * jnp.sin(kk * (li * dt))
    lw_re = pw_re[:q].transpose(1, 2, 0, 3)[:, :, :, None, :]
    lw_im = pw_im[:q].transpose(1, 2, 0, 3)[:, :, :, None, :]
    ck_re = c_re[:, :, None] * lw_re - c_im[:, :, None] * lw_im
    ck_im = c_re[:, :, None] * lw_im + c_im[:, :, None] * lw_re
    ck = jnp.concatenate([ck_re, -ck_im], axis=-1).reshape(2, S5_GROUPS, S5_BLK, 2 * S5_STATE)
    kern_t = jnp.einsum("dgmp,dgpc->dgcm", ck, jnp.concatenate([bb_re, bb_im], axis=2), precision=HI)
    kern_t = kern_t.reshape(2, S5_GROUPS, S5_CH, q, S5_CH)
    zeros = jnp.zeros_like(kern_t)
    bbt_re = bb_re.transpose(0, 1, 3, 2)[:, :, None]
    bbt_im = bb_im.transpose(0, 1, 3, 2)[:, :, None]
    ct_re = c_re.transpose(0, 1, 3, 2)[:, :, :, None, :]
    ct_im = c_im.transpose(0, 1, 3, 2)[:, :, :, None, :]
    ms, ps, gs = [], [], []
    for d in range(2):
        ext = (jnp.concatenate([zeros[d], kern_t[d]], axis=2) if d == 0
               else jnp.concatenate([kern_t[d, :, :, ::-1], zeros[d]], axis=2))
        ext = ext.reshape(S5_GROUPS, S5_CH, 2 * S5_BLK)
        ms.append(ext)
        pidx = (q - 1 - jnp.arange(q)) if d == 0 else jnp.arange(q)
        pr = pw_re[pidx, d].transpose(1, 0, 2)[:, :, None, :]
        pi = pw_im[pidx, d].transpose(1, 0, 2)[:, :, None, :]
        p_re = pr * bbt_re[d] - pi * bbt_im[d]
        p_im = pr * bbt_im[d] + pi * bbt_re[d]
        pd = jnp.stack([jnp.concatenate([p_re, p_im], axis=3), jnp.concatenate([p_im, p_re], axis=3)])
        ps.append(pd.reshape(2, S5_GROUPS, S5_BLK, 2 * S5_STATE))
        gidx = (jnp.arange(q) + 1) if d == 0 else (q - jnp.arange(q))
        gw_re = pw_re[gidx, d].transpose(1, 2, 0)[..., None]
        gw_im = pw_im[gidx, d].transpose(1, 2, 0)[..., None]
        g_re = ct_re[d] * gw_re - ct_im[d] * gw_im
        g_im = ct_re[d] * gw_im + ct_im[d] * gw_re
        gs.append(jnp.concatenate([g_re, -g_im], axis=1).reshape(S5_GROUPS, 2 * S5_STATE, S5_BLK))
    ar = jnp.concatenate([pw_re[q], pw_re[q]], axis=-1)[:, :, None, :]
    ai = jnp.stack([jnp.concatenate([-pw_im[q], pw_im[q]], axis=-1),
                    jnp.concatenate([pw_im[q], -pw_im[q]], axis=-1)], axis=1)[:, :, :, None, :]
    dsk = jnp.tile(d_skip.reshape(S5_GROUPS, 1, S5_CH), (1, 1, q))
    return (jnp.stack(ms), jnp.stack(ps).astype(BF16), jnp.stack(gs).astype(BF16),
            ar.astype(F32), ai.astype(F32), dsk.astype(F32))


def _s5(ug, params, nb, s_len, lc):
    m, p, g, ar, ai, dsk = params
    nc_tot = s_len // S5_Q
    nc_ctx = lc // S5_Q
    r = nb * nc_tot
    return pl.pallas_call(
        functools.partial(_s5_kernel, nb=nb, nc_ctx=nc_ctx, nc_tot=nc_tot),
        out_shape=jax.ShapeDtypeStruct((S5_GROUPS, r, S5_BLK), F32),
        grid=(S5_GROUPS,),
        in_specs=[pl.BlockSpec((1, r, S5_BLK), lambda gi: (gi, 0, 0)),
                  pl.BlockSpec((2, 1, S5_CH, 2 * S5_BLK), lambda gi: (0, gi, 0, 0)),
                  pl.BlockSpec((2, 2, 1, S5_BLK, 2 * S5_STATE), lambda gi: (0, 0, gi, 0, 0)),
                  pl.BlockSpec((2, 1, 2 * S5_STATE, S5_BLK), lambda gi: (0, gi, 0, 0)),
                  pl.BlockSpec((2, 1, 1, 2 * S5_STATE), lambda gi: (0, gi, 0, 0)),
                  pl.BlockSpec((2, 2, 1, 1, 2 * S5_STATE), lambda gi: (0, 0, gi, 0, 0)),
                  pl.BlockSpec((1, 1, S5_BLK), lambda gi: (gi, 0, 0))],
        out_specs=pl.BlockSpec((1, r, S5_BLK), lambda gi: (gi, 0, 0)),
        scratch_shapes=[pltpu.VMEM((2, 2, r, 2 * S5_STATE), F32), pltpu.VMEM((2, r, 2 * S5_STATE), F32),
                        pltpu.VMEM((2, S5_BLK, S5_BLK), BF16)],
        compiler_params=_cp(("parallel",)),
        name="s5_scan",
    )(ug, m, p, g, ar, ai, dsk)


CONV_ROWS = 4 * TM


def _conv_kernel(x_ref, prev_ref, next_ref, w_ref, b_ref, o_ref, *, s_len, lc):
    x = x_ref[...]
    rows = x.shape[0]
    ridx = lax.broadcasted_iota(jnp.int32, x.shape, 0)
    pos = (pl.program_id(0) * rows) % s_len + ridx
    pos = jnp.where(pos >= s_len, pos - s_len, pos)
    seg_first = jnp.logical_or(pos == 0, pos == lc)
    seg_last = jnp.logical_or(pos == lc - 1, pos == s_len - 1)
    xm = jnp.where(ridx == 0, prev_ref[SUBLANES - 1:SUBLANES, :], pltpu.roll(x, 1, 0))
    xp = jnp.where(ridx == rows - 1, next_ref[0:1, :], pltpu.roll(x, rows - 1, 0))
    xm = jnp.where(seg_first, 0.0, xm)
    xp = jnp.where(seg_last, 0.0, xp)
    y = xm * w_ref[0:1, :] + x * w_ref[1:2, :] + xp * w_ref[2:3, :] + b_ref[...]
    o_ref[...] = _silu(y)


def _conv(xbc, w, b, s_len, lc):
    t, c = xbc.shape
    rows = next(r for r in (CONV_ROWS, CONV_ROWS // 2, TM) if t % r == 0)
    per = rows // SUBLANES
    last = t // SUBLANES - 1
    return pl.pallas_call(
        functools.partial(_conv_kernel, s_len=s_len, lc=lc),
        out_shape=jax.ShapeDtypeStruct((t, c), F32),
        grid=(t // rows,),
        in_specs=[pl.BlockSpec((rows, c), lambda i: (i, 0)),
                  pl.BlockSpec((SUBLANES, c), lambda i: (jnp.maximum(i * per - 1, 0), 0)),
                  pl.BlockSpec((SUBLANES, c), lambda i: (jnp.minimum((i + 1) * per, last), 0)),
                  pl.BlockSpec((3, c), lambda i: (0, 0)),
                  pl.BlockSpec((1, c), lambda i: (0, 0))],
        out_specs=pl.BlockSpec((rows, c), lambda i: (i, 0)),
        compiler_params=_cp(("parallel",)),
        name="ssd_conv",
    )(xbc, xbc, xbc, w, b.reshape(1, c))


_X_B = GROUP_W
_X_C = GROUP_W + SSD_NGROUPS * SSD_STATE


def _ssd_kernel(xf_ref, dtf_ref, dttf_ref, xr_ref, dtr_ref, dttr_ref, bias_ref, a_ref, biast_ref, at_ref, dsk_ref,
                yf_ref, yr_ref, stf_ref, str_ref):
    @pl.when(pl.program_id(1) == 0)
    def _():
        stf_ref[...] = jnp.zeros_like(stf_ref)
        str_ref[...] = jnp.zeros_like(str_ref)

    par = (bias_ref[...], a_ref[...], biast_ref[...], at_ref[...], dsk_ref[...])
    for j in range(SSD_SUB):
        rf = slice(j * TQ, (j + 1) * TQ)
        rr = slice((SSD_SUB - 1 - j) * TQ, (SSD_SUB - j) * TQ)
        for b in range(xf_ref.shape[0]):
            yf_ref[b, rf, :] = _ssd_chunk_step(xf_ref[b, rf, :], dtf_ref[b, rf, :], dttf_ref[b, :, rf], par,
                                               stf_ref.at[b], False)
            yr_ref[b, rr, :] = _ssd_chunk_step(xr_ref[b, rr, :], dtr_ref[b, rr, :], dttr_ref[b, :, rr], par,
                                               str_ref.at[b], True)


def _ssd_chunk_step(xc, dt_raw, dtt_raw, par, st_ref, rev):
    bias, a_vec, biast, at_vec, dsk = par
    base = SSD_HEADS if rev else 0
    x = xc[:, 0:GROUP_W]
    dt = _softplus(dt_raw + bias)
    a = dt * a_vec
    dtt = _softplus(dtt_raw + biast)
    at = dtt * at_vec
    ri = lax.broadcasted_iota(jnp.int32, (TQ, TQ), 0)
    ci = lax.broadcasted_iota(jnp.int32, (TQ, TQ), 1)
    causal = (ci >= ri) if rev else (ri >= ci)
    tri = jnp.where(causal, 1.0, 0.0)
    cum_c = _dot_hi(tri, a)
    cum_r = _dot_nt_hi(at, tri)
    edge = 0 if rev else TQ - 1
    tot = cum_c[edge:edge + 1, :]

    shape = (TQ, GROUP_W)
    xdt = x * _per_head_cols(dt, base, SSD_HEADS, shape)
    lane = lax.broadcasted_iota(jnp.int32, shape, 1)
    y = jnp.zeros(shape, F32)
    bmat = [xc[:, _X_B + g * SSD_STATE:_X_B + (g + 1) * SSD_STATE].astype(BF16) for g in range(SSD_NGROUPS)]
    cmat = [xc[:, _X_C + g * SSD_STATE:_X_C + (g + 1) * SSD_STATE].astype(BF16) for g in range(SSD_NGROUPS)]
    cb = [_dot_nt(cmat[g], bmat[g]) for g in range(SSD_NGROUPS)]
    for h in range(SSD_HEADS):
        col = base + h
        seg = jnp.where(causal, cum_c[:, col:col + 1] - cum_r[col:col + 1, :], NEG_INF)
        scores = cb[h // 2] * jnp.exp(seg)
        xh = jnp.where((lane >= h * HEAD_DIM) & (lane < (h + 1) * HEAD_DIM), xdt, 0.0)
        y = y + _dot(scores.astype(BF16), xh.astype(BF16))
    st = st_ref[...]
    yo = jnp.concatenate(
        [_dot_nt(cmat[g], st[g * SSD_STATE:(g + 1) * SSD_STATE].astype(BF16)) for g in range(SSD_NGROUPS)], axis=1)
    y = y + yo * _per_head_cols(jnp.exp(cum_c), base, SSD_HEADS, shape)
    if not rev:
        y = y + x * dsk
    xd = xdt * _per_head_cols(jnp.exp(tot - cum_c), base, SSD_HEADS, shape)
    xdt_t = xd.T.astype(BF16)
    decay = jnp.exp(tot)
    for g in range(SSD_NGROUPS):
        new = _dot(xdt_t[g * SSD_STATE:(g + 1) * SSD_STATE], bmat[g])
        for hh in range(2):
            h = 2 * g + hh
            r0 = h * HEAD_DIM
            st_ref[r0:r0 + HEAD_DIM, :] = (decay[:, base + h:base + h + 1] * st[r0:r0 + HEAD_DIM]
                                           + new[hh * HEAD_DIM:(hh + 1) * HEAD_DIM])
    return y


def _dot_nt_hi(a, b):
    return lax.dot_general(a, b, (((1,), (1,)), ((), ())), preferred_element_type=F32, precision=HI)


def _ssd_chunk(c, rev, nc_ctx, nc_tot):
    if not rev:
        return c
    return jnp.where(c < nc_ctx, nc_ctx - 1 - c, nc_tot - 1 - (c - nc_ctx))


SSD_SUB = TM // TQ
SSD_NB = 4


def _ssd_scan(xc, dt, dtt, bias, a, biast, at, dsk, nb, s_len, lc):
    t = xc.shape[0]
    nblk = s_len // TM
    nctx = lc // TM
    nbs = math.gcd(nb, SSD_NB)
    fix = lambda b, c: (0, 0)
    xc3 = xc.reshape(nb, s_len, SSD_XBC)
    dt3 = dt.reshape(nb, s_len, LANES)

    def rows(rev):
        return lambda b, c: (b, _ssd_chunk(c, rev, nctx, nblk), 0)

    def lanes(rev):
        return lambda b, c: (b, 0, _ssd_chunk(c, rev, nctx, nblk))

    def data_specs(rev):
        return [pl.BlockSpec((nbs, TM, SSD_XBC), rows(rev)), pl.BlockSpec((nbs, TM, LANES), rows(rev)),
                pl.BlockSpec((nbs, SUBLANES, TM), lanes(rev))]

    state = pltpu.VMEM((nbs, SSD_HEADS * HEAD_DIM, SSD_STATE), F32)
    yf, yr = pl.pallas_call(
        _ssd_kernel,
        out_shape=[jax.ShapeDtypeStruct((nb, s_len, GROUP_W), F32)] * 2,
        grid=(nb // nbs, nblk),
        in_specs=data_specs(False) + data_specs(True) + [
            pl.BlockSpec((1, LANES), fix), pl.BlockSpec((1, LANES), fix),
            pl.BlockSpec((SUBLANES, TQ), fix), pl.BlockSpec((SUBLANES, TQ), fix),
            pl.BlockSpec((1, GROUP_W), fix)],
        out_specs=[pl.BlockSpec((nbs, TM, GROUP_W), rows(False)), pl.BlockSpec((nbs, TM, GROUP_W), rows(True))],
        scratch_shapes=[state, state],
        compiler_params=_cp(("parallel", "arbitrary")),
        name="ssd_scan",
    )(xc3, dt3, dtt, xc3, dt3, dtt, bias, a, biast, at, dsk)
    return yf.reshape(t, GROUP_W), yr.reshape(t, GROUP_W)


def _ssd(xbc, dt, conv_w, conv_b, dt_bias, a_log, d_skip, nb, s_len, lc):
    xc = _conv(xbc, conv_w, conv_b, s_len, lc)
    nd = 2 * SSD_HEADS
    dtt = dt[:, :nd].reshape(nb, s_len, nd).transpose(0, 2, 1)
    bias = jnp.pad(dt_bias.reshape(1, nd), ((0, 0), (0, LANES - nd)))
    a = jnp.pad(-jnp.exp(a_log).reshape(1, nd), ((0, 0), (0, LANES - nd)))
    biast = jnp.broadcast_to(dt_bias.reshape(nd, 1), (nd, TQ))
    at = jnp.broadcast_to(-jnp.exp(a_log).reshape(nd, 1), (nd, TQ))
    dsk = jnp.repeat(d_skip, HEAD_DIM).reshape(1, GROUP_W)
    return _ssd_scan(xc, dt, dtt, bias, a, biast, at, dsk, nb, s_len, lc)


def _outproj_kernel(x_ref, ys5_ref, oga_ref, y0_ref, y1_ref, z_ref, owa_ref, *refs):
    mods, shared = refs[:ROW_SUB], refs[ROW_SUB:]
    for s in range(ROW_SUB):
        rows = _row_views((x_ref, oga_ref, y0_ref, y1_ref, z_ref, owa_ref) + tuple(shared[-4:-1]), s)
        _outproj_block(rows[0], ys5_ref.at[:, pl.ds(s * S5_TB, S5_TB), :], *rows[1:6], mods[s], *shared[:-4],
                       *rows[6:], shared[-1])


def _outproj_block(x_ref, ys5_ref, oga_ref, y0_ref, y1_ref, z_ref, owa_ref, mod_ref, gluw_ref, glub_ref,
                   ng_ref, wout_ref, n2_ref, wr_ref, br_ref, xn_o, h2_o, route_o, y_scr):
    _s5_unpack(ys5_ref, y_scr)
    gl = _gelu_tanh(jnp.concatenate([y_scr[0], y_scr[1]], axis=1))
    a = gl * _sigmoid(_dot(gl.astype(BF16), gluw_ref[...]) + glub_ref[...])
    m = (y0_ref[...] + y1_ref[...]) * _silu(z_ref[...])
    m = m * lax.rsqrt(jnp.mean(m * m, axis=-1, keepdims=True) + EPS) * ng_ref[...]
    w = wout_ref
    mix = (_dot(a.astype(BF16), w[0:GROUP_W, :]) + _dot(oga_ref[...].astype(BF16), w[GROUP_W:2 * GROUP_W, :])
           + _dot(m.astype(BF16), w[2 * GROUP_W:3 * GROUP_W, :]) + _dot(owa_ref[...].astype(BF16), w[3 * GROUP_W:, :]))
    xn = x_ref[...] + mod_ref[0, 2:3, :] * mix
    xn_o[...] = xn
    h2 = xn * lax.rsqrt(jnp.mean(xn * xn, axis=-1, keepdims=True) + EPS) * n2_ref[...]
    h2 = h2 * (1.0 + mod_ref[0, 4:5, :]) + mod_ref[0, 3:4, :]
    h2_o[...] = _pack_bf16_pair(h2)
    h_hi = h2.astype(BF16)
    h_lo = (h2 - h_hi.astype(F32)).astype(BF16)
    logits = _dot(h_hi, wr_ref[0]) + (_dot(h_lo, wr_ref[0]) + _dot(h_hi, wr_ref[1])) + br_ref[...]
    lane = lax.broadcasted_iota(jnp.int32, logits.shape, 1).astype(F32)
    big = float(4 * LANES)
    lcoarse = jnp.where(lane < MOE_GROUPS, logits, NEG_INF)
    mx = jnp.max(lcoarse, axis=1, keepdims=True)
    den = jnp.sum(jnp.exp(lcoarse - mx), axis=1, keepdims=True)
    grp = jnp.min(jnp.where(lcoarse == mx, lane, big), axis=1, keepdims=True)
    pg = 1.0 / den
    lo = ROUTE_FINE0 + grp * MOE_PER_GROUP
    lf = jnp.where(lane >= lo, jnp.where(lane < lo + MOE_PER_GROUP, logits, NEG_INF), NEG_INF)
    v1 = jnp.max(lf, axis=1, keepdims=True)
    i1 = jnp.min(jnp.where(lf == v1, lane, big), axis=1, keepdims=True)
    lf2 = jnp.where(lane == i1, NEG_INF, lf)
    v2 = jnp.max(lf2, axis=1, keepdims=True)
    i2 = jnp.min(jnp.where(lf2 == v2, lane, big), axis=1, keepdims=True)
    e2 = jnp.exp(v2 - v1)
    w1 = pg / (1.0 + e2)
    w2 = w1 * e2
    route = jnp.where(lane == 0, i1 - ROUTE_FINE0,
                      jnp.where(lane == 1, i2 - ROUTE_FINE0,
                                jnp.where(lane == 2, w1, jnp.where(lane == 3, w2, 0.0))))
    route_o[...] = route


def _outproj(x, ys5, oga, y0, y1, z, owa, mod, glu_w, glu_b, ssd_norm_g, w_out, norm2_g, wr, br, nb, nblk):
    t, d = x.shape
    row = lambda i: (i, 0)
    fix = lambda i: (0, 0)
    step = ROW_SUB * TM
    assert t % step == 0
    gw = pl.BlockSpec((step, GROUP_W), row)
    wr_hi = wr.astype(BF16)
    mod_specs = [pl.BlockSpec((1, 6, d), lambda i, s=s: (_mod_row(ROW_SUB * i + s, nblk, nb), 0, 0))
                 for s in range(ROW_SUB)]
    return pl.pallas_call(
        _outproj_kernel,
        out_shape=[jax.ShapeDtypeStruct((t, d), F32), jax.ShapeDtypeStruct((t, d // 2), jnp.uint32),
                   jax.ShapeDtypeStruct((t, LANES), F32)],
        grid=(t // step,),
        in_specs=[pl.BlockSpec((step, d), row),
                  pl.BlockSpec((S5_GROUPS, ROW_SUB * S5_TB, S5_BLK), lambda i: (0, i, 0)),
                  gw, gw, gw, gw, gw] + mod_specs + [
                  pl.BlockSpec((GROUP_W, GROUP_W), fix),
                  pl.BlockSpec((1, GROUP_W), fix),
                  pl.BlockSpec((1, GROUP_W), fix),
                  pl.BlockSpec((d, d), fix),
                  pl.BlockSpec((1, d), fix),
                  pl.BlockSpec((2, d, LANES), lambda i: (0, 0, 0)),
                  pl.BlockSpec((1, LANES), fix)],
        out_specs=[pl.BlockSpec((step, d), row), pl.BlockSpec((step, d // 2), row), pl.BlockSpec((step, LANES), row)],
        scratch_shapes=[pltpu.VMEM((GROUP_W // LANES, TM, LANES), F32)],
        compiler_params=_cp(("parallel",)),
        name="out_proj_router",
    )(x, ys5, oga, y0, y1, z, owa, *([mod] * ROW_SUB), glu_w.astype(BF16), glu_b.reshape(1, -1), ssd_norm_g.reshape(1, -1),
      w_out.astype(BF16), norm2_g.reshape(1, -1), jnp.stack([wr_hi, (wr - wr_hi.astype(F32)).astype(BF16)]), br)


def _pack_router(coarse_w, coarse_b, fine_w, fine_b):
    def lanes(coarse, fine):
        gap = jnp.zeros(coarse.shape[:-1] + (ROUTE_FINE0 - MOE_GROUPS,), F32)
        tail = jnp.zeros(coarse.shape[:-1] + (LANES - ROUTE_FINE0 - N_EXPERTS,), F32)
        return jnp.concatenate([coarse, gap, fine, tail], axis=-1)

    return lanes(coarse_w, fine_w), lanes(coarse_b[None, :], fine_b[None, :])


def _gather_rows(src, idx):
    m = idx.shape[0]
    d = src.shape[1]
    workers = SC_CORES * SC_SUBCORES
    k = SC_FETCH_K
    nch = m // (workers * k)
    assert nch * workers * k == m
    mesh = plsc.VectorSubcoreMesh(core_axis_name="c", subcore_axis_name="s")

    @functools.partial(
        pl.kernel, mesh=mesh,
        out_type=jax.ShapeDtypeStruct((m, d), src.dtype),
        scratch_types=[pltpu.VMEM((nch, k), jnp.int32),
                       pltpu.VMEM((k, d), src.dtype),
                       pltpu.SemaphoreType.DMA],
    )
    def gather(src_hbm, idx_hbm, out_hbm, idx_v, rows_v, sem):
        wid = lax.axis_index("s") * SC_CORES + lax.axis_index("c")
        pltpu.sync_copy(idx_hbm.at[wid], idx_v)

        @pl.loop(0, nch)
        def _(j):
            off = pl.multiple_of((wid * nch + j) * k, k)
            pltpu.async_copy(src_hbm.at[idx_v.at[j]], rows_v, sem).wait()
            pltpu.sync_copy(rows_v, out_hbm.at[pl.ds(off, k)])

    return gather(src, idx.reshape(workers, nch, k))


def _scatter_rows(src, dst0, dst1, nrows):
    t, d = src.shape
    workers = SC_CORES * SC_SUBCORES
    nch = t // (workers * SC_GATHER_K)
    assert nch * workers * SC_GATHER_K == t
    mesh = plsc.VectorSubcoreMesh(core_axis_name="c", subcore_axis_name="s")

    @functools.partial(
        pl.kernel, mesh=mesh,
        out_type=jax.ShapeDtypeStruct((nrows, d), src.dtype),
        scratch_types=[pltpu.VMEM((nch, SC_GATHER_K), jnp.int32),
                       pltpu.VMEM((nch, SC_GATHER_K), jnp.int32),
                       pltpu.VMEM((SC_GATHER_K, d), src.dtype),
                       pltpu.SemaphoreType.DMA((2,))],
    )
    def scatter(src_hbm, d0_hbm, d1_hbm, out_hbm, i0_v, i1_v, rows_v, sem):
        wid = lax.axis_index("s") * SC_CORES + lax.axis_index("c")
        pltpu.sync_copy(d0_hbm.at[wid], i0_v)
        pltpu.sync_copy(d1_hbm.at[wid], i1_v)

        @pl.loop(0, nch)
        def _(j):
            off = pl.multiple_of((wid * nch + j) * SC_GATHER_K, SC_GATHER_K)
            pltpu.sync_copy(src_hbm.at[pl.ds(off, SC_GATHER_K)], rows_v)
            first = pltpu.async_copy(rows_v, out_hbm.at[i0_v.at[j]], sem.at[0])
            second = pltpu.async_copy(rows_v, out_hbm.at[i1_v.at[j]], sem.at[1])
            first.wait()
            second.wait()

    return scatter(src, dst0.reshape(workers, nch, SC_GATHER_K), dst1.reshape(workers, nch, SC_GATHER_K))


def _expert_kernel(be_ref, nused_ref, nvalid_ref, nxt_ref, slot_ref, x_ref, wg_hbm, wu_hbm, wd_hbm, o_ref,
                   wg_f, wu_f, wd_f, wg_s, wu_s, wd_s, sem, *, layer):
    i = pl.program_id(0)
    used = i < nused_ref[0]
    new_expert = jnp.logical_or(i == 0, be_ref[i] != be_ref[jnp.maximum(i - 1, 0)])

    def weight_copies(expert, slot):
        return [pltpu.make_async_copy(w.at[layer, expert], f.at[slot], sem.at[slot, j])
                for j, (w, f) in enumerate(((wg_hbm, wg_f), (wu_hbm, wu_f), (wd_hbm, wd_f)))]

    @pl.when(jnp.logical_and(used, new_expert))
    def _():
        slot = slot_ref[i]

        @pl.when(i == 0)
        def _():
            for c in weight_copies(be_ref[i], slot):
                c.start()

        for c in weight_copies(be_ref[i], slot):
            c.wait()
        wg_s[...] = wg_f[slot].astype(BF16)
        wu_s[...] = wu_f[slot].astype(BF16)
        wd_s[...] = wd_f[slot].astype(BF16)

        @pl.when(nxt_ref[i] >= 0)
        def _():
            for c in weight_copies(nxt_ref[i], 1 - slot):
                c.start()

    def swiglu(rows):
        row = rows.start + lax.broadcasted_iota(jnp.int32, (rows.stop - rows.start, x_ref.shape[1]), 0)
        lo, hi = _unpack_bf16_pair(jnp.where(row < nvalid_ref[i], x_ref[rows, :], jnp.uint32(0)))
        lo = lo.astype(BF16)
        hi = hi.astype(BF16)
        half = lo.shape[1]
        gate = _dot(lo, wg_s[0:half, :]) + _dot(hi, wg_s[half:, :])
        up = _dot(lo, wu_s[0:half, :]) + _dot(hi, wu_s[half:, :])
        o_ref[rows, :] = _pack_bf16_pair(_dot((_silu(gate) * up).astype(BF16), wd_s[...]))

    used = i < nused_ref[0]
    half_rows = MOE_TM // 2

    @pl.when(jnp.logical_and(used, nvalid_ref[i] > half_rows))
    def _():
        swiglu(slice(0, MOE_TM))

    @pl.when(jnp.logical_and(used, nvalid_ref[i] <= half_rows))
    def _():
        swiglu(slice(0, half_rows))
        o_ref[half_rows:, :] = jnp.zeros((MOE_TM - half_rows, o_ref.shape[1]), o_ref.dtype)

    @pl.when(jnp.logical_not(used))
    def _():
        o_ref[...] = jnp.zeros_like(o_ref)


def _experts(xs, blk_e, n_used, n_valid, nxt_e, slot, wg, wu, wd, layer):
    rows, dp = xs.shape
    d = 2 * dp
    nblocks = rows // MOE_TM
    de = wg.shape[3]
    blk = lambda i, *_: (i, 0)
    hbm = pl.BlockSpec(memory_space=pl.ANY)
    grid_spec = pltpu.PrefetchScalarGridSpec(
        num_scalar_prefetch=5,
        grid=(nblocks,),
        in_specs=[pl.BlockSpec((MOE_TM, dp), blk), hbm, hbm, hbm],
        out_specs=pl.BlockSpec((MOE_TM, dp), blk),
        scratch_shapes=[pltpu.VMEM((2, d, de), F32), pltpu.VMEM((2, d, de), F32), pltpu.VMEM((2, de, d), F32),
                        pltpu.VMEM((d, de), BF16), pltpu.VMEM((d, de), BF16), pltpu.VMEM((de, d), BF16),
                        pltpu.SemaphoreType.DMA((2, 3))],
    )
    return pl.pallas_call(
        functools.partial(_expert_kernel, layer=layer),
        out_shape=jax.ShapeDtypeStruct((rows, dp), jnp.uint32),
        grid_spec=grid_spec,
        compiler_params=_cp(("arbitrary",)),
        name="moe_experts",
    )(blk_e, n_used, n_valid, nxt_e, slot, xs, wg, wu, wd)


def _final_kernel(*refs):
    fg_ref, o_ref = refs[-2:]
    for s in range(ROW_SUB):
        y = _moe_residual(*refs[5 * s:5 * s + 5])
        o_ref[s * TM:(s + 1) * TM, :] = y * lax.rsqrt(jnp.mean(y * y, axis=-1, keepdims=True) + EPS) * fg_ref[...]


def _final(xn, rows2, route, mod, final_g, nb, nblk):
    t, d = xn.shape
    nlat = nblk - 1
    assert (nb * nlat) % ROW_SUB == 0

    def blk_specs(s):
        lat = lambda i: ROW_SUB * i + s
        src = lambda i: ((lat(i) // nlat) * nblk + 1 + lat(i) % nlat, 0)
        return [pl.BlockSpec((TM, d), src),
                pl.BlockSpec((TM, d // 2), src),
                pl.BlockSpec((TM, d // 2), lambda i: (src(i)[0] + t // TM, 0)),
                pl.BlockSpec((TM, LANES), src),
                pl.BlockSpec((1, 6, d), lambda i: (lat(i) // nlat, 0, 0))]

    return pl.pallas_call(
        _final_kernel,
        out_shape=jax.ShapeDtypeStruct((nb * nlat * TM, d), F32),
        grid=(nb * nlat // ROW_SUB,),
        in_specs=[sp for s in range(ROW_SUB) for sp in blk_specs(s)] + [pl.BlockSpec((1, d), lambda i: (0, 0))],
        out_specs=pl.BlockSpec((ROW_SUB * TM, d), lambda i: (i, 0)),
        compiler_params=_cp(("parallel",)),
        name="moe_combine_final",
    )(*((xn, rows2, rows2, route, mod) * ROW_SUB), final_g.reshape(1, d))


def _moe(h2, route, wg, wu, wd, layer):
    t, d = h2.shape
    n_slots = 2 * t
    experts = jnp.arange(N_EXPERTS, dtype=F32)[None, :]
    oh0 = (route[:, 0:1] == experts).astype(F32)
    oh1 = (route[:, 1:2] == experts).astype(F32)
    both = (oh0 + oh1).reshape(t // LANES, LANES, N_EXPERTS)
    tri = jnp.tril(jnp.ones((LANES, LANES), F32))
    intra = jnp.einsum("ij,bjk->bik", tri, both)
    blk_tot = intra[:, -1, :]
    blk_cum = jnp.cumsum(blk_tot, axis=0)
    earlier = (intra - both + (blk_cum - blk_tot)[:, None, :]).reshape(t, N_EXPERTS)
    counts = blk_cum[-1].astype(jnp.int32)
    pcounts = (counts + MOE_TM - 1) // MOE_TM * MOE_TM
    pends = jnp.cumsum(pcounts)
    pstarts = pends - pcounts
    base = pstarts.astype(F32)[None, :] + earlier
    dest0 = jnp.sum(oh0 * base, axis=1).astype(jnp.int32)
    dest1 = jnp.sum(oh1 * base, axis=1).astype(jnp.int32)
    nblocks = -(-n_slots // MOE_TM) + N_EXPERTS
    nrows = nblocks * MOE_TM
    blk_start = jnp.arange(nblocks, dtype=jnp.int32) * MOE_TM
    blk_e = jnp.minimum(jnp.sum((pends[None, :] <= blk_start[:, None]).astype(jnp.int32), axis=1), N_EXPERTS - 1)
    n_used = (pends[-1] // MOE_TM).astype(jnp.int32).reshape(1)
    n_valid = jnp.clip((pstarts + counts)[blk_e] - blk_start, 0, MOE_TM).astype(jnp.int32)
    ids = jnp.arange(N_EXPERTS, dtype=jnp.int32)
    has = counts > 0
    later = lax.cummin(jnp.where(has, ids, N_EXPERTS)[::-1])[::-1]
    nxt = jnp.concatenate([later[1:], jnp.full((1,), N_EXPERTS, jnp.int32)])
    nxt = jnp.where(nxt >= N_EXPERTS, -1, nxt)
    slot = (jnp.cumsum(has.astype(jnp.int32)) - 1) % 2
    xs = _scatter_rows(h2, dest0, dest1, nrows)
    ys = _experts(xs, blk_e, n_used, n_valid, nxt[blk_e], slot[blk_e], wg, wu, wd, layer)
    return _gather_rows(ys, jnp.concatenate([dest0, dest1]))


def kernel(x, c, ctx, c_ctx, ada_w, ada_b, norm1_g, norm2_g, w_in, w_out, s5_lam_re, s5_lam_im, s5_log_dt, s5_b_re, s5_b_im, s5_c_re, s5_c_im, s5_d, s5_glu_w, s5_glu_b, ga_qn_g, ga_kn_g, ssd_conv_w, ssd_conv_b, ssd_dt_bias, ssd_a_log, ssd_d, ssd_norm_g, wa_sink, moe_coarse_w, moe_coarse_b, moe_fine_w, moe_fine_b, moe_w_gate, moe_w_up, moe_w_down, final_g):
    nb, l, d = x.shape
    lc = ctx.shape[1]
    depth = ada_w.shape[0]
    assert lc == TM and l % TM == 0 and nb <= SUBLANES - 1 and d == D_MODEL
    s_len = lc + l
    nblk = s_len // TM
    t = nb * s_len

    cc = jnp.zeros((SUBLANES, d), F32).at[:nb].set(c).at[nb].set(c_ctx)
    mods = _ada(cc, ada_w, ada_b).reshape(depth, SUBLANES, 6, d)
    cos_t, sin_t = _rope_tables(lc, l)
    w_packed = jax.vmap(_pack_w_in)(w_in)
    s5_tabs = jax.vmap(_s5_params)(s5_lam_re, s5_lam_im, s5_log_dt, s5_b_re, s5_b_im, s5_c_re, s5_c_im, s5_d)
    wrs, brs = jax.vmap(_pack_router)(moe_coarse_w, moe_coarse_b, moe_fine_w, moe_fine_b)

    src = ("first", x.reshape(nb * l, d), ctx.reshape(nb * lc, d))
    for i in range(depth):
        mod = mods[i]
        (xm, xbc, ug, z, dt, gaq, gak, gav, waq, wak, wav) = _inproj(
            src, mod, norm1_g[i], w_packed[i], cos_t, sin_t, ga_qn_g[i], ga_kn_g[i], nb, nblk)
        ys5 = _s5(ug, tuple(tab[i] for tab in s5_tabs), nb, s_len, lc)
        oga, owa = _attn(wa_sink[i], gaq, gak, gav, waq, wak, wav, nb, s_len, lc)
        y0, y1 = _ssd(xbc, dt, ssd_conv_w[i], ssd_conv_b[i], ssd_dt_bias[i], ssd_a_log[i], ssd_d[i], nb, s_len, lc)
        wr, br = wrs[i], brs[i]
        xn, h2, route = _outproj(xm, ys5, oga, y0, y1, z, owa, mod, s5_glu_w[i], s5_glu_b[i], ssd_norm_g[i],
                                 w_out[i], norm2_g[i], wr, br, nb, nblk)
        rows2 = _moe(h2, route, moe_w_gate, moe_w_up, moe_w_down, i)
        src = ("moe", xn, rows2, route, mod)
    return _final(xn, rows2, route, mod, final_g, nb, nblk).reshape(nb, l, d)
```

```python
import functools
import math

import jax
import jax.numpy as jnp
import numpy as np
from jax import lax
from jax.experimental import pallas as pl
from jax.experimental.pallas import tpu as pltpu
from jax.experimental.pallas import tpu_sc as plsc

F32 = jnp.float32
BF16 = jnp.bfloat16
HI = lax.Precision.HIGHEST

D_MODEL = 1024
GRID_W = 64
GROUP_W = 256
HEAD_DIM = 64
ROPE_FREQS = HEAD_DIM // 4
ROPE_BASE = 10000.0
EPS = 1e-6
S5_CH = 16
S5_GROUPS = GROUP_W // S5_CH
S5_STATE = 64
N_HEADS = 4
SSD_HEADS = 4
SSD_NGROUPS = 2
SSD_STATE = 128
SSD_XBC = GROUP_W + 2 * SSD_NGROUPS * SSD_STATE
WINDOW = 128
MOE_GROUPS = 4
MOE_PER_GROUP = 8
N_EXPERTS = 32

LANES = 128
SUBLANES = 8
TM = 256
TQ = 128
GA_TQ = 128
GA_SUB = 2
S5_Q = 32
S5_BLK = S5_Q * S5_CH
MOE_TM = 512
SC_CORES = 2
SC_SUBCORES = 16
SC_GATHER_K = 32
SC_FETCH_K = 64
ROUTE_FINE0 = 32
VMEM_LIMIT = 56 * 1024 * 1024

NEG_INF = float("-inf")
LOG2E = math.log2(math.e)


def _cp(sem, vmem=VMEM_LIMIT):
    return pltpu.CompilerParams(dimension_semantics=sem, vmem_limit_bytes=vmem)


def _dot(a, b):
    return jnp.dot(a, b, preferred_element_type=F32)


def _dot_hi(a, b):
    return jnp.dot(a, b, preferred_element_type=F32, precision=HI)


def _dot_nt(a, b):
    return lax.dot_general(a, b, (((1,), (1,)), ((), ())), preferred_element_type=F32)


def _sigmoid(x):
    return 1.0 / (1.0 + jnp.exp(-x))


def _silu(x):
    return x * _sigmoid(x)


def _gelu_tanh(x):
    return 0.5 * x * (1.0 + jnp.tanh(math.sqrt(2.0 / math.pi) * (x + 0.044715 * (x * x * x))))


def _softplus(x):
    return jnp.maximum(x, 0.0) + jnp.log(1.0 + jnp.exp(-jnp.abs(x)))


_HI16 = 0xFFFF0000


def _pack_bf16_pair(x):
    n = x.shape[1] // 2
    bits = pltpu.bitcast(x.astype(BF16).astype(F32), jnp.uint32)
    return (bits[:, n:] & jnp.uint32(_HI16)) | (bits[:, :n] >> 16)


def _unpack_bf16_pair(w):
    return pltpu.bitcast(w << 16, F32), pltpu.bitcast(w & jnp.uint32(_HI16), F32)


def _per_head_cols(v, base, n_heads, shape):
    lane = lax.broadcasted_iota(jnp.int32, shape, 1)
    out = jnp.broadcast_to(v[:, base + n_heads - 1:base + n_heads], shape)
    for h in range(n_heads - 2, -1, -1):
        out = jnp.where(lane < (h + 1) * HEAD_DIM, v[:, base + h:base + h + 1], out)
    return out


def _ada_kernel(c_ref, w_ref, b_ref, o_ref):
    c = c_ref[...]
    o_ref[0] = _dot_hi(_silu(c), w_ref[0]) + b_ref[0]


def _ada(cc, ada_w, ada_b):
    depth, d, n = ada_w.shape
    tn = 1536
    return pl.pallas_call(
        _ada_kernel,
        out_shape=jax.ShapeDtypeStruct((depth, SUBLANES, n), F32),
        grid=(depth, n // tn),
        in_specs=[pl.BlockSpec((SUBLANES, d), lambda l, j: (0, 0)),
                  pl.BlockSpec((1, d, tn), lambda l, j: (l, 0, j)),
                  pl.BlockSpec((1, 1, tn), lambda l, j: (l, 0, j))],
        out_specs=pl.BlockSpec((1, SUBLANES, tn), lambda l, j: (l, 0, j)),
        compiler_params=_cp(("parallel", "parallel")),
        name="ada_mod",
    )(cc, ada_w, ada_b.reshape(depth, 1, n))


_C_XBC = 0
_C_U = _C_XBC + SSD_XBC
_C_Z = _C_U + GROUP_W
_C_DT = _C_Z + GROUP_W
_C_GAQ = _C_DT + LANES
_C_WAQ = _C_GAQ + N_HEADS * LANES
_C_GAK = _C_WAQ + N_HEADS * LANES
_C_GAV = _C_GAK + LANES
_C_WAK = _C_GAV + LANES
_C_WAV = _C_WAK + LANES
_C_END = _C_WAV + LANES


def _expand_q_cols(wq):
    zero = jnp.zeros((wq.shape[0], HEAD_DIM), wq.dtype)
    parts = []
    for h in range(N_HEADS):
        head = wq[:, h * HEAD_DIM:(h + 1) * HEAD_DIM]
        parts += [head, zero] if h // 2 == 0 else [zero, head]
    return jnp.concatenate(parts, axis=1)


def _pack_w_in(w_in):
    cuts = np.cumsum([256, 256, 128, 128, 256, SSD_XBC, 2 * SSD_HEADS, 256, 128, 128])[:-1]
    u, gaq, gak, gav, z, xbc, dt, waq, wak, wav = jnp.split(w_in, [int(c) for c in cuts], axis=1)
    dt = jnp.pad(dt, ((0, 0), (0, LANES - dt.shape[1])))
    w = jnp.concatenate([xbc, u, z, dt, _expand_q_cols(gaq), _expand_q_cols(waq), gak, gav, wak, wav], axis=1)
    return w.astype(BF16)


def _rope(x, cos, sins):
    w = x.shape[1]
    if w > LANES:
        cos = jnp.concatenate([cos] * (w // LANES), axis=1)
        sins = jnp.concatenate([sins] * (w // LANES), axis=1)
    lane = lax.broadcasted_iota(jnp.int32, x.shape, 1)
    up = pltpu.roll(x, w - ROPE_FREQS, 1)
    dn = pltpu.roll(x, ROPE_FREQS, 1)
    partner = jnp.where((lane & ROPE_FREQS) == 0, up, dn)
    return x * cos + partner * sins


def _v_with_ones(v):
    lo = lax.broadcasted_iota(jnp.int32, v.shape, 1) < HEAD_DIM
    return jnp.concatenate([jnp.where(lo, v, 1.0), jnp.where(lo, 1.0, v)], axis=1).astype(BF16)


def _moe_residual(xn_ref, r0_ref, r1_ref, route_ref, mod_ref):
    route = route_ref[...]
    r0 = jnp.concatenate(_unpack_bf16_pair(r0_ref[...]), axis=1)
    r1 = jnp.concatenate(_unpack_bf16_pair(r1_ref[...]), axis=1)
    return xn_ref[...] + mod_ref[0, 5:6, :] * (route[:, 2:3] * r0 + route[:, 3:4] * r1)


ROW_SUB = 4


def _row_views(refs, s):
    return [r.at[pl.ds(s * TM, TM), :] for r in refs]


def _inproj_kernel(*refs, first, nblk):
    n_blk_in = (2 if first else 5) + 3
    shared = refs[ROW_SUB * n_blk_in:]
    g_ref, w_ref, qn_ref, kn_ref = shared[:4]
    xm_o, xbc_o, ug_o = shared[4:7]
    rest_o = shared[7:-1]
    u_scr = shared[-1]
    for s in range(ROW_SUB):
        blk_refs = refs[s * n_blk_in:(s + 1) * n_blk_in]
        xm_v, xbc_v = _row_views((xm_o, xbc_o), s)
        ug_v = ug_o.at[:, pl.ds(s * S5_TB, S5_TB), :]
        rest_v = [r.at[:, pl.ds(s * TM, TM)] if i in _K_OUT else r.at[pl.ds(s * TM, TM), :]
                  for i, r in enumerate(rest_o)]
        _inproj_block(blk_refs, g_ref, w_ref, qn_ref, kn_ref, xm_v, xbc_v, ug_v, rest_v, u_scr,
                      first, (pl.program_id(0) * ROW_SUB + s) % nblk == 0)


_K_OUT = (3, 6)


def _inproj_block(blk_refs, g_ref, w_ref, qn_ref, kn_ref, xm_o, xbc_o, ug_o, rest_o, u_scr, first, is_ctx):
    if first:
        lat_ref, ctx_ref = blk_refs[:2]
        x = jnp.where(is_ctx, ctx_ref[...], lat_ref[...])
    else:
        x = _moe_residual(*blk_refs[:5])
    mod_ref, cos_ref, sin_ref = blk_refs[-3:]
    z_o, dt_o, gaq_o, gak_o, gav_o, waq_o, wak_o, wav_o = rest_o
    xm_o[...] = x
    ms = jnp.mean(x * x, axis=-1, keepdims=True)
    xn = x * lax.rsqrt(ms + EPS) * g_ref[...]
    h = xn * (1.0 + mod_ref[0, 1:2, :]) + mod_ref[0, 0:1, :]
    hb = h.astype(BF16)

    def proj(lo, hi):
        return _dot(hb, w_ref[:, lo:hi])

    cos = cos_ref[...]
    sins = sin_ref[...]
    scale = LOG2E * HEAD_DIM ** -0.5
    q = proj(_C_GAQ, _C_WAQ)
    qs = q * q
    inv = jnp.concatenate(
        [jnp.broadcast_to(lax.rsqrt(jnp.sum(qs[:, s * LANES:(s + 1) * LANES], axis=1, keepdims=True)
                                    * (1.0 / HEAD_DIM) + EPS), (q.shape[0], LANES)) for s in range(N_HEADS)], axis=1)
    gaq_o[...] = (_rope(q * inv * qn_ref[...], cos, sins) * scale).astype(BF16)
    waq_o[...] = (_rope(proj(_C_WAQ, _C_GAK), cos, sins) * scale).astype(BF16)
    k = proj(_C_GAK, _C_GAV)
    ks = k * k
    lane = lax.broadcasted_iota(jnp.int32, k.shape, 1)
    lo = lane < HEAD_DIM
    ms0 = jnp.sum(jnp.where(lo, ks, 0.0), axis=1, keepdims=True)
    ms1 = jnp.sum(jnp.where(lo, 0.0, ks), axis=1, keepdims=True)
    kinv = lax.rsqrt(jnp.where(lo, ms0, ms1) * (1.0 / HEAD_DIM) + EPS)
    gak_o[...] = _rope(k * kinv * kn_ref[...], cos, sins).T.astype(BF16)
    gav_o[...] = _v_with_ones(proj(_C_GAV, _C_WAK))
    wak_o[...] = _rope(proj(_C_WAK, _C_WAV), cos, sins).T.astype(BF16)
    wav_o[...] = _v_with_ones(proj(_C_WAV, _C_END))
    xbc_o[...] = proj(_C_XBC, _C_U)
    u = proj(_C_U, _C_Z)
    u_scr[0] = u[:, :LANES]
    u_scr[1] = u[:, LANES:]
    _s5_pack(u_scr.at[0], u_scr.at[1], ug_o)
    z_o[...] = proj(_C_Z, _C_DT)
    dt_o[...] = proj(_C_DT, _C_GAQ)


def _mod_row(i, nblk, nb):
    return jnp.where(i % nblk == 0, nb, i // nblk)


def _inproj(src, mod, norm_g, w_packed, cos_t, sin_t, qn_g, kn_g, nb, nblk):
    first = src[0] == "first"
    d = src[1].shape[1]
    t = nb * nblk * TM
    row = lambda i: (i, 0)
    fix = lambda i: (0, 0)
    nsteps = t // (ROW_SUB * TM)
    assert nsteps * ROW_SUB * TM == t

    def blk_specs(s):
        bid = lambda i: ROW_SUB * i + s
        modspec = pl.BlockSpec((1, 6, d), lambda i: (_mod_row(bid(i), nblk, nb), 0, 0))
        table = pl.BlockSpec((TM, LANES), lambda i: (bid(i) % nblk, 0))
        if first:
            srcs = [pl.BlockSpec((TM, d), lambda i: ((bid(i) // nblk) * (nblk - 1) + jnp.maximum(bid(i) % nblk - 1, 0), 0)),
                    pl.BlockSpec((TM, d), lambda i: (bid(i) // nblk, 0))]
        else:
            srcs = [pl.BlockSpec((TM, d), lambda i: (bid(i), 0)), pl.BlockSpec((TM, d // 2), lambda i: (bid(i), 0)),
                    pl.BlockSpec((TM, d // 2), lambda i: (bid(i) + t // TM, 0)),
                    pl.BlockSpec((TM, LANES), lambda i: (bid(i), 0)), modspec]
        return srcs + [modspec, table, table]

    if first:
        blk_args = tuple(src[1:]) + (mod, cos_t, sin_t)
    else:
        blk_args = (src[1], src[2], src[2], src[3], src[4], mod, cos_t, sin_t)
    outs = [(d, F32), (SSD_XBC, F32), None, (GROUP_W, F32), (LANES, F32),
            (N_HEADS * LANES, BF16), (LANES, BF16), (2 * LANES, BF16),
            (N_HEADS * LANES, BF16), (LANES, BF16), (2 * LANES, BF16)]
    shapes = [jax.ShapeDtypeStruct((t, o[0]), o[1]) if o else
              jax.ShapeDtypeStruct((S5_GROUPS, t // S5_Q, S5_BLK), F32) for o in outs]
    specs = [pl.BlockSpec((ROW_SUB * TM, o[0]), row) if o else
             pl.BlockSpec((S5_GROUPS, ROW_SUB * S5_TB, S5_BLK), lambda i: (0, i, 0)) for o in outs]
    for ki in _K_OUT:
        shapes[3 + ki] = jax.ShapeDtypeStruct((LANES, t), BF16)
        specs[3 + ki] = pl.BlockSpec((LANES, ROW_SUB * TM), lambda i: (0, i))
    return pl.pallas_call(
        functools.partial(_inproj_kernel, first=first, nblk=nblk),
        out_shape=shapes,
        grid=(nsteps,),
        in_specs=[sp for s in range(ROW_SUB) for sp in blk_specs(s)] + [
                  pl.BlockSpec((1, d), fix),
                  pl.BlockSpec((d, _C_END), fix),
                  pl.BlockSpec((1, N_HEADS * LANES), fix),
                  pl.BlockSpec((1, LANES), fix)],
        out_specs=specs,
        scratch_shapes=[pltpu.VMEM((GROUP_W // LANES, TM, LANES), F32)],
        compiler_params=_cp(("parallel",)),
        name="in_proj",
    )(*(blk_args * ROW_SUB), norm_g.reshape(1, d), w_packed,
      jnp.tile(qn_g, 2 * N_HEADS).reshape(1, -1), jnp.tile(kn_g, 2).reshape(1, -1))


def _rope_tables(lc, l):
    n_rows = l // GRID_W
    rows = np.repeat(np.arange(n_rows), GRID_W)
    cols = np.tile(np.arange(GRID_W), n_rows)
    inv = np.power(np.float32(ROPE_BASE), -np.arange(ROPE_FREQS, dtype=np.float32) / ROPE_FREQS)
    ang = np.stack([rows, cols], axis=-1).astype(np.float32)[..., None] * inv
    cos = np.cos(ang)
    sin = np.sin(ang)
    cos64 = np.stack([cos, cos], axis=2).reshape(l, HEAD_DIM)
    sin64 = np.stack([-sin, sin], axis=2).reshape(l, HEAD_DIM)
    cos64 = np.concatenate([np.ones((lc, HEAD_DIM), np.float32), cos64], axis=0)
    sin64 = np.concatenate([np.zeros((lc, HEAD_DIM), np.float32), sin64], axis=0)
    return (jnp.asarray(np.tile(cos64, (1, 2)), dtype=F32), jnp.asarray(np.tile(sin64, (1, 2)), dtype=F32))


def _merge_heads(o2, kvh):
    tq = o2.shape[0] // 2
    oa, ob = o2[:tq], o2[tq:]
    lane = lax.broadcasted_iota(jnp.int32, oa.shape, 1)
    if kvh == 0:
        return jnp.where(lane < HEAD_DIM, oa, pltpu.roll(ob, HEAD_DIM, 1))
    return jnp.where(lane < HEAD_DIM, pltpu.roll(oa, HEAD_DIM, 1), ob)


def _stack_q(q_ref, rows, kvh):
    return jnp.concatenate([q_ref[rows, (2 * kvh) * LANES:(2 * kvh + 1) * LANES],
                            q_ref[rows, (2 * kvh + 1) * LANES:(2 * kvh + 2) * LANES]], axis=0)


def _ga_attend(q_ref, k_ref, v_ref, o_ref, nkeys):
    kt = k_ref[:, 0:nkeys]
    for sub in range(GA_SUB):
        rows = slice(sub * GA_TQ, (sub + 1) * GA_TQ)
        scores = [_dot(_stack_q(q_ref, rows, kvh), kt) for kvh in range(2)]
        outs = []
        for kvh in range(2):
            s = scores[kvh]
            p = jnp.exp2((s - jnp.max(s, axis=1, keepdims=True)).astype(BF16))
            o2 = _dot(p, v_ref[0:nkeys, kvh * LANES:(kvh + 1) * LANES])
            outs.append(_merge_heads(o2 / pltpu.roll(o2, HEAD_DIM, 1), kvh))
        o_ref[rows, :] = jnp.concatenate(outs, axis=1)


def _attn_kernel(sink_ref, gq_ref, gk_ref, gv_ref, wq_ref, wk_ref, wv_ref, og_ref, ow_ref, *, lc):
    is_ctx = pl.program_id(1) < lc // TM

    @pl.when(is_ctx)
    def _():
        _ga_attend(gq_ref, gk_ref, gv_ref, og_ref, lc)
        _wa_attend(sink_ref, wq_ref, wk_ref, wv_ref, ow_ref, lc)

    @pl.when(jnp.logical_not(is_ctx))
    def _():
        _ga_attend(gq_ref, gk_ref, gv_ref, og_ref, gk_ref.shape[1])
        _wa_attend(sink_ref, wq_ref, wk_ref, wv_ref, ow_ref, lc)


def _attn(sink, gq, gk, gv, wq, wk, wv, nb, s_len, lc):
    t = gq.shape[0]
    nq = s_len // TM
    assert GA_SUB * GA_TQ == TM and WA_SUB * TQ == TM
    qspec = pl.BlockSpec((TM, N_HEADS * LANES), lambda b, j: (b * nq + j, 0))
    kspec = pl.BlockSpec((LANES, s_len), lambda b, j: (0, b))
    vspec = pl.BlockSpec((s_len, 2 * LANES), lambda b, j: (b, 0))
    ospec = pl.BlockSpec((TM, GROUP_W), lambda b, j: (b * nq + j, 0))
    return pl.pallas_call(
        functools.partial(_attn_kernel, lc=lc),
        out_shape=[jax.ShapeDtypeStruct((t, GROUP_W), F32)] * 2,
        grid=(nb, nq),
        in_specs=[pl.BlockSpec(memory_space=pltpu.SMEM), qspec, kspec, vspec, qspec, kspec, vspec],
        out_specs=[ospec, ospec],
        compiler_params=_cp(("parallel", "arbitrary")),
        name="attention",
    )(sink, gq, gk, gv, wq, wk, wv)


WA_SUB = TM // TQ


def _wa_attend(sink_ref, q_ref, k_ref, v_ref, o_ref, lc):
    s_len = k_ref.shape[1]
    kc = k_ref[:, 0:lc]
    row = lax.broadcasted_iota(jnp.int32, (2 * TQ, 1), 0)
    for sub in range(WA_SUB):
        rows = slice(sub * TQ, (sub + 1) * TQ)
        n = pl.program_id(1) * WA_SUB + sub - lc // TQ
        start = pl.multiple_of(jnp.clip(lc + (n - 1) * TQ, lc, s_len - 3 * TQ), TQ)
        kb = k_ref[:, pl.ds(start, 3 * TQ)]
        qpos = n * TQ + lax.broadcasted_iota(jnp.int32, (TQ, 3 * TQ), 0)
        kpos = (start - lc) + lax.broadcasted_iota(jnp.int32, (TQ, 3 * TQ), 1)
        reach = jnp.where(n >= 0, WINDOW, -1)
        valid = jnp.abs(qpos - kpos) <= reach
        valid = jnp.concatenate([valid, valid], axis=0)
        outs = []
        for kvh in range(2):
            q2 = jnp.concatenate([q_ref[rows, (2 * kvh) * LANES:(2 * kvh + 1) * LANES],
                                  q_ref[rows, (2 * kvh + 1) * LANES:(2 * kvh + 2) * LANES]], axis=0)
            sc = _dot(q2, kc)
            sb = jnp.where(valid, _dot(q2, kb), NEG_INF)
            sink = jnp.where(row < TQ, sink_ref[2 * kvh], sink_ref[2 * kvh + 1]) * LOG2E
            m = jnp.maximum(jnp.maximum(jnp.max(sc, axis=1, keepdims=True), jnp.max(sb, axis=1, keepdims=True)), sink)
            pc = jnp.exp2((sc - m).astype(BF16))
            pb = jnp.exp2((sb - m).astype(BF16))
            vcols = slice(kvh * LANES, (kvh + 1) * LANES)
            o2 = _dot(pc, v_ref[0:lc, vcols]) + _dot(pb, v_ref[pl.ds(start, 3 * TQ), vcols])
            denom = pltpu.roll(o2, HEAD_DIM, 1) + jnp.exp2(sink - m)
            outs.append(_merge_heads(o2 / denom, kvh))
        o_ref[rows, :] = jnp.concatenate(outs, axis=1)


def _s5_chunk_index(t, rev, nc_ctx, nc_tot):
    if not rev:
        return t
    return jnp.where(t < nc_ctx, nc_ctx - 1 - t, nc_tot - 1 - (t - nc_ctx))


def _s5_kernel(u_ref, k_ref, p_ref, g_ref, ar_ref, ai_ref, dsk_ref, y_ref, s_scr, h_scr, m_scr, *, nb, nc_ctx, nc_tot):
    for d in range(2):
        ext = k_ref[d, 0]
        for s in range(S5_Q):
            lo = ((S5_Q - s) if d == 0 else (S5_Q - 1 - s)) * S5_CH
            win = pltpu.roll(ext, (2 * S5_BLK - lo) % (2 * S5_BLK), 1)[:, :S5_BLK]
            m_scr[d, s * S5_CH:(s + 1) * S5_CH, :] = win.astype(BF16)
    uf = u_ref[0]
    u = uf.astype(BF16)
    for d in range(2):
        for k in range(2):
            s_scr[d, k] = _dot(u, p_ref[d, k, 0])
    ar = [jnp.broadcast_to(ar_ref[d, 0], (nb, LANES)) for d in range(2)]
    ai = [[jnp.broadcast_to(ai_ref[d, k, 0], (nb, LANES)) for k in range(2)] for d in range(2)]

    def body(t, carry):
        out = []
        for d in range(2):
            h, hs = carry[d]
            rows = pl.ds(_s5_chunk_index(t, d == 1, nc_ctx, nc_tot), nb, stride=nc_tot)
            h_scr[d, rows, :] = h
            out.append((ar[d] * h + ai[d][0] * hs + s_scr[d, 0, rows, :],
                        ar[d] * hs + ai[d][1] * h + s_scr[d, 1, rows, :]))
        return tuple(out)

    zero = jnp.zeros((nb, LANES), F32)
    lax.fori_loop(0, nc_tot, body, ((zero, zero), (zero, zero)), unroll=2)
    y = uf * dsk_ref[0]
    for d in range(2):
        y = y + _dot(u, m_scr[d]) + _dot(h_scr[d].astype(BF16), g_ref[d, 0])
    y_ref[0] = y


S5_TB = TM // S5_Q
S5_GPS = LANES // S5_CH


def _s5_pack(lo_ref, hi_ref, o_ref):
    for s in range(S5_Q):
        rows = pl.ds(s, S5_TB, stride=S5_Q)
        halves = (lo_ref[rows, :], hi_ref[rows, :])
        dst = S5_CH * (s % S5_GPS)
        for g in range(S5_GROUPS):
            slab = halves[g // S5_GPS]
            src = S5_CH * (g % S5_GPS)
            moved = slab if src == dst else pltpu.roll(slab, (dst - src) % LANES, 1)
            o_ref[g, :, s * S5_CH:(s + 1) * S5_CH] = moved[:, dst:dst + S5_CH]


def _s5_unpack(y_ref, o_ref):
    lane_grp = lax.broadcasted_iota(jnp.int32, (S5_TB, LANES), 1) // S5_CH
    for s in range(S5_Q):
        src = S5_CH * (s % S5_GPS)
        for half in range(S5_GROUPS // S5_GPS):
            acc = None
            for gl in range(S5_GPS):
                slab = y_ref[half * S5_GPS + gl, :, (s // S5_GPS) * LANES:(s // S5_GPS + 1) * LANES]
                dst = S5_CH * gl
                moved = slab if src == dst else pltpu.roll(slab, (dst - src) % LANES, 1)
                acc = moved if acc is None else jnp.where(lane_grp == gl, moved, acc)
            o_ref[half, pl.ds(s, S5_TB, stride=S5_Q), :] = acc


def _s5_params(lam_re, lam_im, log_dt, b_re, b_im, c_re, c_im, d_skip):
    q = S5_Q
    dt = jnp.exp(log_dt)[..., None]
    lr, li = lam_re, lam_im
    mag = jnp.exp(lr * dt)
    a_re = mag * jnp.cos(li * dt)
    a_im = mag * jnp.sin(li * dt)
    den = lr * lr + li * li
    f_re = ((a_re - 1.0) * lr + a_im * li) / den
    f_im = (a_im * lr - (a_re - 1.0) * li) / den
    bb_re = f_re[..., None] * b_re - f_im[..., None] * b_im
    bb_im = f_re[..., None] * b_im + f_im[..., None] * b_re
    kk = jnp.arange(q + 1, dtype=F32)[:, None, None, None]
    pmag = jnp.exp(kk * (lr * dt))
    pw_re = pmag * jnp.cos(kk * (li * dt))
    pw_im = pmag * jnp.sin(kk * (li * dt))
    lw_re = pw_re[:q].transpose(1, 2, 0, 3)[:, :, :, None, :]
    lw_im = pw_im[:q].transpose(1, 2, 0, 3)[:, :, :, None, :]
    ck_re = c_re[:, :, None] * lw_re - c_im[:, :, None] * lw_im
    ck_im = c_re[:, :, None] * lw_im + c_im[:, :, None] * lw_re
    ck = jnp.concatenate([ck_re, -ck_im], axis=-1).reshape(2, S5_GROUPS, S5_BLK, 2 * S5_STATE)
    kern_t = jnp.einsum("dgmp,dgpc->dgcm", ck, jnp.concatenate([bb_re, bb_im], axis=2), precision=HI)
    kern_t = kern_t.reshape(2, S5_GROUPS, S5_CH, q, S5_CH)
    zeros = jnp.zeros_like(kern_t)
    bbt_re = bb_re.transpose(0, 1, 3, 2)[:, :, None]
    bbt_im = bb_im.transpose(0, 1, 3, 2)[:, :, None]
    ct_re = c_re.transpose(0, 1, 3, 2)[:, :, :, None, :]
    ct_im = c_im.transpose(0, 1, 3, 2)[:, :, :, None, :]
    ms, ps, gs = [], [], []
    for d in range(2):
        ext = (jnp.concatenate([zeros[d], kern_t[d]], axis=2) if d == 0
               else jnp.concatenate([kern_t[d, :, :, ::-1], zeros[d]], axis=2))
        ext = ext.reshape(S5_GROUPS, S5_CH, 2 * S5_BLK)
        ms.append(ext)
        pidx = (q - 1 - jnp.arange(q)) if d == 0 else jnp.arange(q)
        pr = pw_re[pidx, d].transpose(1, 0, 2)[:, :, None, :]
        pi = pw_im[pidx, d].transpose(1, 0, 2)[:, :, None, :]
        p_re = pr * bbt_re[d] - pi * bbt_im[d]
        p_im = pr * bbt_im[d] + pi * bbt_re[d]
        pd = jnp.stack([jnp.concatenate([p_re, p_im], axis=3), jnp.concatenate([p_im, p_re], axis=3)])
        ps.append(pd.reshape(2, S5_GROUPS, S5_BLK, 2 * S5_STATE))
        gidx = (jnp.arange(q) + 1) if d == 0 else (q - jnp.arange(q))
        gw_re = pw_re[gidx, d].transpose(1, 2, 0)[..., None]
        gw_im = pw_im[gidx, d].transpose(1, 2, 0)[..., None]
        g_re = ct_re[d] * gw_re - ct_im[d] * gw_im
        g_im = ct_re[d] * gw_im + ct_im[d] * gw_re
        gs.append(jnp.concatenate([g_re, -g_im], axis=1).reshape(S5_GROUPS, 2 * S5_STATE, S5_BLK))
    ar = jnp.concatenate([pw_re[q], pw_re[q]], axis=-1)[:, :, None, :]
    ai = jnp.stack([jnp.concatenate([-pw_im[q], pw_im[q]], axis=-1),
                    jnp.concatenate([pw_im[q], -pw_im[q]], axis=-1)], axis=1)[:, :, :, None, :]
    dsk = jnp.tile(d_skip.reshape(S5_GROUPS, 1, S5_CH), (1, 1, q))
    return (jnp.stack(ms), jnp.stack(ps).astype(BF16), jnp.stack(gs).astype(BF16),
            ar.astype(F32), ai.astype(F32), dsk.astype(F32))


def _s5(ug, params, nb, s_len, lc):
    m, p, g, ar, ai, dsk = params
    nc_tot = s_len // S5_Q
    nc_ctx = lc // S5_Q
    r = nb * nc_tot
    return pl.pallas_call(
        functools.partial(_s5_kernel, nb=nb, nc_ctx=nc_ctx, nc_tot=nc_tot),
        out_shape=jax.ShapeDtypeStruct((S5_GROUPS, r, S5_BLK), F32),
        grid=(S5_GROUPS,),
        in_specs=[pl.BlockSpec((1, r, S5_BLK), lambda gi: (gi, 0, 0)),
                  pl.BlockSpec((2, 1, S5_CH, 2 * S5_BLK), lambda gi: (0, gi, 0, 0)),
                  pl.BlockSpec((2, 2, 1, S5_BLK, 2 * S5_STATE), lambda gi: (0, 0, gi, 0, 0)),
                  pl.BlockSpec((2, 1, 2 * S5_STATE, S5_BLK), lambda gi: (0, gi, 0, 0)),
                  pl.BlockSpec((2, 1, 1, 2 * S5_STATE), lambda gi: (0, gi, 0, 0)),
                  pl.BlockSpec((2, 2, 1, 1, 2 * S5_STATE), lambda gi: (0, 0, gi, 0, 0)),
                  pl.BlockSpec((1, 1, S5_BLK), lambda gi: (gi, 0, 0))],
        out_specs=pl.BlockSpec((1, r, S5_BLK), lambda gi: (gi, 0, 0)),
        scratch_shapes=[pltpu.VMEM((2, 2, r, 2 * S5_STATE), F32), pltpu.VMEM((2, r, 2 * S5_STATE), F32),
                        pltpu.VMEM((2, S5_BLK, S5_BLK), BF16)],
        compiler_params=_cp(("parallel",)),
        name="s5_scan",
    )(ug, m, p, g, ar, ai, dsk)


CONV_ROWS = 4 * TM


def _conv_kernel(x_ref, prev_ref, next_ref, w_ref, b_ref, o_ref, *, s_len, lc):
    x = x_ref[...]
    rows = x.shape[0]
    ridx = lax.broadcasted_iota(jnp.int32, x.shape, 0)
    pos = (pl.program_id(0) * rows) % s_len + ridx
    pos = jnp.where(pos >= s_len, pos - s_len, pos)
    seg_first = jnp.logical_or(pos == 0, pos == lc)
    seg_last = jnp.logical_or(pos == lc - 1, pos == s_len - 1)
    xm = jnp.where(ridx == 0, prev_ref[SUBLANES - 1:SUBLANES, :], pltpu.roll(x, 1, 0))
    xp = jnp.where(ridx == rows - 1, next_ref[0:1, :], pltpu.roll(x, rows - 1, 0))
    xm = jnp.where(seg_first, 0.0, xm)
    xp = jnp.where(seg_last, 0.0, xp)
    y = xm * w_ref[0:1, :] + x * w_ref[1:2, :] + xp * w_ref[2:3, :] + b_ref[...]
    o_ref[...] = _silu(y)


def _conv(xbc, w, b, s_len, lc):
    t, c = xbc.shape
    rows = next(r for r in (CONV_ROWS, CONV_ROWS // 2, TM) if t % r == 0)
    per = rows // SUBLANES
    last = t // SUBLANES - 1
    return pl.pallas_call(
        functools.partial(_conv_kernel, s_len=s_len, lc=lc),
        out_shape=jax.ShapeDtypeStruct((t, c), F32),
        grid=(t // rows,),
        in_specs=[pl.BlockSpec((rows, c), lambda i: (i, 0)),
                  pl.BlockSpec((SUBLANES, c), lambda i: (jnp.maximum(i * per - 1, 0), 0)),
                  pl.BlockSpec((SUBLANES, c), lambda i: (jnp.minimum((i + 1) * per, last), 0)),
                  pl.BlockSpec((3, c), lambda i: (0, 0)),
                  pl.BlockSpec((1, c), lambda i: (0, 0))],
        out_specs=pl.BlockSpec((rows, c), lambda i: (i, 0)),
        compiler_params=_cp(("parallel",)),
        name="ssd_conv",
    )(xbc, xbc, xbc, w, b.reshape(1, c))


_X_B = GROUP_W
_X_C = GROUP_W + SSD_NGROUPS * SSD_STATE


def _ssd_kernel(xf_ref, dtf_ref, dttf_ref, xr_ref, dtr_ref, dttr_ref, bias_ref, a_ref, biast_ref, at_ref, dsk_ref,
                yf_ref, yr_ref, stf_ref, str_ref):
    @pl.when(pl.program_id(1) == 0)
    def _():
        stf_ref[...] = jnp.zeros_like(stf_ref)
        str_ref[...] = jnp.zeros_like(str_ref)

    par = (bias_ref[...], a_ref[...], biast_ref[...], at_ref[...], dsk_ref[...])
    for j in range(SSD_SUB):
        rf = slice(j * TQ, (j + 1) * TQ)
        rr = slice((SSD_SUB - 1 - j) * TQ, (SSD_SUB - j) * TQ)
        for b in range(xf_ref.shape[0]):
            yf_ref[b, rf, :] = _ssd_chunk_step(xf_ref[b, rf, :], dtf_ref[b, rf, :], dttf_ref[b, :, rf], par,
                                               stf_ref.at[b], False)
            yr_ref[b, rr, :] = _ssd_chunk_step(xr_ref[b, rr, :], dtr_ref[b, rr, :], dttr_ref[b, :, rr], par,
                                               str_ref.at[b], True)


def _ssd_chunk_step(xc, dt_raw, dtt_raw, par, st_ref, rev):
    bias, a_vec, biast, at_vec, dsk = par
    base = SSD_HEADS if rev else 0
    x = xc[:, 0:GROUP_W]
    dt = _softplus(dt_raw + bias)
    a = dt * a_vec
    dtt = _softplus(dtt_raw + biast)
    at = dtt * at_vec
    ri = lax.broadcasted_iota(jnp.int32, (TQ, TQ), 0)
    ci = lax.broadcasted_iota(jnp.int32, (TQ, TQ), 1)
    causal = (ci >= ri) if rev else (ri >= ci)
    tri = jnp.where(causal, 1.0, 0.0)
    cum_c = _dot_hi(tri, a)
    cum_r = _dot_nt_hi(at, tri)
    edge = 0 if rev else TQ - 1
    tot = cum_c[edge:edge + 1, :]

    shape = (TQ, GROUP_W)
    xdt = x * _per_head_cols(dt, base, SSD_HEADS, shape)
    lane = lax.broadcasted_iota(jnp.int32, shape, 1)
    y = jnp.zeros(shape, F32)
    bmat = [xc[:, _X_B + g * SSD_STATE:_X_B + (g + 1) * SSD_STATE].astype(BF16) for g in range(SSD_NGROUPS)]
    cmat = [xc[:, _X_C + g * SSD_STATE:_X_C + (g + 1) * SSD_STATE].astype(BF16) for g in range(SSD_NGROUPS)]
    cb = [_dot_nt(cmat[g], bmat[g]) for g in range(SSD_NGROUPS)]
    for h in range(SSD_HEADS):
        col = base + h
        seg = jnp.where(causal, cum_c[:, col:col + 1] - cum_r[col:col + 1, :], NEG_INF)
        scores = cb[h // 2] * jnp.exp(seg)
        xh = jnp.where((lane >= h * HEAD_DIM) & (lane < (h + 1) * HEAD_DIM), xdt, 0.0)
        y = y + _dot(scores.astype(BF16), xh.astype(BF16))
    st = st_ref[...]
    yo = jnp.concatenate(
        [_dot_nt(cmat[g], st[g * SSD_STATE:(g + 1) * SSD_STATE].astype(BF16)) for g in range(SSD_NGROUPS)], axis=1)
    y = y + yo * _per_head_cols(jnp.exp(cum_c), base, SSD_HEADS, shape)
    if not rev:
        y = y + x * dsk
    xd = xdt * _per_head_cols(jnp.exp(tot - cum_c), base, SSD_HEADS, shape)
    xdt_t = xd.T.astype(BF16)
    decay = jnp.exp(tot)
    for g in range(SSD_NGROUPS):
        new = _dot(xdt_t[g * SSD_STATE:(g + 1) * SSD_STATE], bmat[g])
        for hh in range(2):
            h = 2 * g + hh
            r0 = h * HEAD_DIM
            st_ref[r0:r0 + HEAD_DIM, :] = (decay[:, base + h:base + h + 1] * st[r0:r0 + HEAD_DIM]
                                           + new[hh * HEAD_DIM:(hh + 1) * HEAD_DIM])
    return y


def _dot_nt_hi(a, b):
    return lax.dot_general(a, b, (((1,), (1,)), ((), ())), preferred_element_type=F32, precision=HI)


def _ssd_chunk(c, rev, nc_ctx, nc_tot):
    if not rev:
        return c
    return jnp.where(c < nc_ctx, nc_ctx - 1 - c, nc_tot - 1 - (c - nc_ctx))


SSD_SUB = TM // TQ
SSD_NB = 4


def _ssd_scan(xc, dt, dtt, bias, a, biast, at, dsk, nb, s_len, lc):
    t = xc.shape[0]
    nblk = s_len // TM
    nctx = lc // TM
    nbs = math.gcd(nb, SSD_NB)
    fix = lambda b, c: (0, 0)
    xc3 = xc.reshape(nb, s_len, SSD_XBC)
    dt3 = dt.reshape(nb, s_len, LANES)

    def rows(rev):
        return lambda b, c: (b, _ssd_chunk(c, rev, nctx, nblk), 0)

    def lanes(rev):
        return lambda b, c: (b, 0, _ssd_chunk(c, rev, nctx, nblk))

    def data_specs(rev):
        return [pl.BlockSpec((nbs, TM, SSD_XBC), rows(rev)), pl.BlockSpec((nbs, TM, LANES), rows(rev)),
                pl.BlockSpec((nbs, SUBLANES, TM), lanes(rev))]

    state = pltpu.VMEM((nbs, SSD_HEADS * HEAD_DIM, SSD_STATE), F32)
    yf, yr = pl.pallas_call(
        _ssd_kernel,
        out_shape=[jax.ShapeDtypeStruct((nb, s_len, GROUP_W), F32)] * 2,
        grid=(nb // nbs, nblk),
        in_specs=data_specs(False) + data_specs(True) + [
            pl.BlockSpec((1, LANES), fix), pl.BlockSpec((1, LANES), fix),
            pl.BlockSpec((SUBLANES, TQ), fix), pl.BlockSpec((SUBLANES, TQ), fix),
            pl.BlockSpec((1, GROUP_W), fix)],
        out_specs=[pl.BlockSpec((nbs, TM, GROUP_W), rows(False)), pl.BlockSpec((nbs, TM, GROUP_W), rows(True))],
        scratch_shapes=[state, state],
        compiler_params=_cp(("parallel", "arbitrary")),
        name="ssd_scan",
    )(xc3, dt3, dtt, xc3, dt3, dtt, bias, a, biast, at, dsk)
    return yf.reshape(t, GROUP_W), yr.reshape(t, GROUP_W)


def _ssd(xbc, dt, conv_w, conv_b, dt_bias, a_log, d_skip, nb, s_len, lc):
    xc = _conv(xbc, conv_w, conv_b, s_len, lc)
    nd = 2 * SSD_HEADS
    dtt = dt[:, :nd].reshape(nb, s_len, nd).transpose(0, 2, 1)
    bias = jnp.pad(dt_bias.reshape(1, nd), ((0, 0), (0, LANES - nd)))
    a = jnp.pad(-jnp.exp(a_log).reshape(1, nd), ((0, 0), (0, LANES - nd)))
    biast = jnp.broadcast_to(dt_bias.reshape(nd, 1), (nd, TQ))
    at = jnp.broadcast_to(-jnp.exp(a_log).reshape(nd, 1), (nd, TQ))
    dsk = jnp.repeat(d_skip, HEAD_DIM).reshape(1, GROUP_W)
    return _ssd_scan(xc, dt, dtt, bias, a, biast, at, dsk, nb, s_len, lc)


def _outproj_kernel(x_ref, ys5_ref, oga_ref, y0_ref, y1_ref, z_ref, owa_ref, *refs):
    mods, shared = refs[:ROW_SUB], refs[ROW_SUB:]
    for s in range(ROW_SUB):
        rows = _row_views((x_ref, oga_ref, y0_ref, y1_ref, z_ref, owa_ref) + tuple(shared[-4:-1]), s)
        _outproj_block(rows[0], ys5_ref.at[:, pl.ds(s * S5_TB, S5_TB), :], *rows[1:6], mods[s], *shared[:-4],
                       *rows[6:], shared[-1])


def _outproj_block(x_ref, ys5_ref, oga_ref, y0_ref, y1_ref, z_ref, owa_ref, mod_ref, gluw_ref, glub_ref,
                   ng_ref, wout_ref, n2_ref, wr_ref, br_ref, xn_o, h2_o, route_o, y_scr):
    _s5_unpack(ys5_ref, y_scr)
    gl = _gelu_tanh(jnp.concatenate([y_scr[0], y_scr[1]], axis=1))
    a = gl * _sigmoid(_dot(gl.astype(BF16), gluw_ref[...]) + glub_ref[...])
    m = (y0_ref[...] + y1_ref[...]) * _silu(z_ref[...])
    m = m * lax.rsqrt(jnp.mean(m * m, axis=-1, keepdims=True) + EPS) * ng_ref[...]
    w = wout_ref
    mix = (_dot(a.astype(BF16), w[0:GROUP_W, :]) + _dot(oga_ref[...].astype(BF16), w[GROUP_W:2 * GROUP_W, :])
           + _dot(m.astype(BF16), w[2 * GROUP_W:3 * GROUP_W, :]) + _dot(owa_ref[...].astype(BF16), w[3 * GROUP_W:, :]))
    xn = x_ref[...] + mod_ref[0, 2:3, :] * mix
    xn_o[...] = xn
    h2 = xn * lax.rsqrt(jnp.mean(xn * xn, axis=-1, keepdims=True) + EPS) * n2_ref[...]
    h2 = h2 * (1.0 + mod_ref[0, 4:5, :]) + mod_ref[0, 3:4, :]
    h2_o[...] = _pack_bf16_pair(h2)
    h_hi = h2.astype(BF16)
    h_lo = (h2 - h_hi.astype(F32)).astype(BF16)
    logits = _dot(h_hi, wr_ref[0]) + (_dot(h_lo, wr_ref[0]) + _dot(h_hi, wr_ref[1])) + br_ref[...]
    lane = lax.broadcasted_iota(jnp.int32, logits.shape, 1).astype(F32)
    big = float(4 * LANES)
    lcoarse = jnp.where(lane < MOE_GROUPS, logits, NEG_INF)
    mx = jnp.max(lcoarse, axis=1, keepdims=True)
    den = jnp.sum(jnp.exp(lcoarse - mx), axis=1, keepdims=True)
    grp = jnp.min(jnp.where(lcoarse == mx, lane, big), axis=1, keepdims=True)
    pg = 1.0 / den
    lo = ROUTE_FINE0 + grp * MOE_PER_GROUP
    lf = jnp.where(lane >= lo, jnp.where(lane < lo + MOE_PER_GROUP, logits, NEG_INF), NEG_INF)
    v1 = jnp.max(lf, axis=1, keepdims=True)
    i1 = jnp.min(jnp.where(lf == v1, lane, big), axis=1, keepdims=True)
    lf2 = jnp.where(lane == i1, NEG_INF, lf)
    v2 = jnp.max(lf2, axis=1, keepdims=True)
    i2 = jnp.min(jnp.where(lf2 == v2, lane, big), axis=1, keepdims=True)
    e2 = jnp.exp(v2 - v1)
    w1 = pg / (1.0 + e2)
    w2 = w1 * e2
    route = jnp.where(lane == 0, i1 - ROUTE_FINE0,
                      jnp.where(lane == 1, i2 - ROUTE_FINE0,
                                jnp.where(lane == 2, w1, jnp.where(lane == 3, w2, 0.0))))
    route_o[...] = route


def _outproj(x, ys5, oga, y0, y1, z, owa, mod, glu_w, glu_b, ssd_norm_g, w_out, norm2_g, wr, br, nb, nblk):
    t, d = x.shape
    row = lambda i: (i, 0)
    fix = lambda i: (0, 0)
    step = ROW_SUB * TM
    assert t % step == 0
    gw = pl.BlockSpec((step, GROUP_W), row)
    wr_hi = wr.astype(BF16)
    mod_specs = [pl.BlockSpec((1, 6, d), lambda i, s=s: (_mod_row(ROW_SUB * i + s, nblk, nb), 0, 0))
                 for s in range(ROW_SUB)]
    return pl.pallas_call(
        _outproj_kernel,
        out_shape=[jax.ShapeDtypeStruct((t, d), F32), jax.ShapeDtypeStruct((t, d // 2), jnp.uint32),
                   jax.ShapeDtypeStruct((t, LANES), F32)],
        grid=(t // step,),
        in_specs=[pl.BlockSpec((step, d), row),
                  pl.BlockSpec((S5_GROUPS, ROW_SUB * S5_TB, S5_BLK), lambda i: (0, i, 0)),
                  gw, gw, gw, gw, gw] + mod_specs + [
                  pl.BlockSpec((GROUP_W, GROUP_W), fix),
                  pl.BlockSpec((1, GROUP_W), fix),
                  pl.BlockSpec((1, GROUP_W), fix),
                  pl.BlockSpec((d, d), fix),
                  pl.BlockSpec((1, d), fix),
                  pl.BlockSpec((2, d, LANES), lambda i: (0, 0, 0)),
                  pl.BlockSpec((1, LANES), fix)],
        out_specs=[pl.BlockSpec((step, d), row), pl.BlockSpec((step, d // 2), row), pl.BlockSpec((step, LANES), row)],
        scratch_shapes=[pltpu.VMEM((GROUP_W // LANES, TM, LANES), F32)],
        compiler_params=_cp(("parallel",)),
        name="out_proj_router",
    )(x, ys5, oga, y0, y1, z, owa, *([mod] * ROW_SUB), glu_w.astype(BF16), glu_b.reshape(1, -1), ssd_norm_g.reshape(1, -1),
      w_out.astype(BF16), norm2_g.reshape(1, -1), jnp.stack([wr_hi, (wr - wr_hi.astype(F32)).astype(BF16)]), br)


def _pack_router(coarse_w, coarse_b, fine_w, fine_b):
    def lanes(coarse, fine):
        gap = jnp.zeros(coarse.shape[:-1] + (ROUTE_FINE0 - MOE_GROUPS,), F32)
        tail = jnp.zeros(coarse.shape[:-1] + (LANES - ROUTE_FINE0 - N_EXPERTS,), F32)
        return jnp.concatenate([coarse, gap, fine, tail], axis=-1)

    return lanes(coarse_w, fine_w), lanes(coarse_b[None, :], fine_b[None, :])


def _gather_rows(src, idx):
    m = idx.shape[0]
    d = src.shape[1]
    workers = SC_CORES * SC_SUBCORES
    k = SC_FETCH_K
    nch = m // (workers * k)
    assert nch * workers * k == m
    mesh = plsc.VectorSubcoreMesh(core_axis_name="c", subcore_axis_name="s")

    @functools.partial(
        pl.kernel, mesh=mesh,
        out_type=jax.ShapeDtypeStruct((m, d), src.dtype),
        scratch_types=[pltpu.VMEM((nch, k), jnp.int32),
                       pltpu.VMEM((k, d), src.dtype),
                       pltpu.SemaphoreType.DMA],
    )
    def gather(src_hbm, idx_hbm, out_hbm, idx_v, rows_v, sem):
        wid = lax.axis_index("s") * SC_CORES + lax.axis_index("c")
        pltpu.sync_copy(idx_hbm.at[wid], idx_v)

        @pl.loop(0, nch)
        def _(j):
            off = pl.multiple_of((wid * nch + j) * k, k)
            pltpu.async_copy(src_hbm.at[idx_v.at[j]], rows_v, sem).wait()
            pltpu.sync_copy(rows_v, out_hbm.at[pl.ds(off, k)])

    return gather(src, idx.reshape(workers, nch, k))


def _scatter_rows(src, dst0, dst1, nrows):
    t, d = src.shape
    workers = SC_CORES * SC_SUBCORES
    nch = t // (workers * SC_GATHER_K)
    assert nch * workers * SC_GATHER_K == t
    mesh = plsc.VectorSubcoreMesh(core_axis_name="c", subcore_axis_name="s")

    @functools.partial(
        pl.kernel, mesh=mesh,
        out_type=jax.ShapeDtypeStruct((nrows, d), src.dtype),
        scratch_types=[pltpu.VMEM((nch, SC_GATHER_K), jnp.int32),
                       pltpu.VMEM((nch, SC_GATHER_K), jnp.int32),
                       pltpu.VMEM((SC_GATHER_K, d), src.dtype),
                       pltpu.SemaphoreType.DMA((2,))],
    )
    def scatter(src_hbm, d0_hbm, d1_hbm, out_hbm, i0_v, i1_v, rows_v, sem):
        wid = lax.axis_index("s") * SC_CORES + lax.axis_index("c")
        pltpu.sync_copy(d0_hbm.at[wid], i0_v)
        pltpu.sync_copy(d1_hbm.at[wid], i1_v)

        @pl.loop(0, nch)
        def _(j):
            off = pl.multiple_of((wid * nch + j) * SC_GATHER_K, SC_GATHER_K)
            pltpu.sync_copy(src_hbm.at[pl.ds(off, SC_GATHER_K)], rows_v)
            first = pltpu.async_copy(rows_v, out_hbm.at[i0_v.at[j]], sem.at[0])
            second = pltpu.async_copy(rows_v, out_hbm.at[i1_v.at[j]], sem.at[1])
            first.wait()
            second.wait()

    return scatter(src, dst0.reshape(workers, nch, SC_GATHER_K), dst1.reshape(workers, nch, SC_GATHER_K))


def _expert_kernel(be_ref, nused_ref, nvalid_ref, nxt_ref, slot_ref, x_ref, wg_hbm, wu_hbm, wd_hbm, o_ref,
                   wg_f, wu_f, wd_f, wg_s, wu_s, wd_s, sem, *, layer):
    i = pl.program_id(0)
    used = i < nused_ref[0]
    new_expert = jnp.logical_or(i == 0, be_ref[i] != be_ref[jnp.maximum(i - 1, 0)])

    def weight_copies(expert, slot):
        return [pltpu.make_async_copy(w.at[layer, expert], f.at[slot], sem.at[slot, j])
                for j, (w, f) in enumerate(((wg_hbm, wg_f), (wu_hbm, wu_f), (wd_hbm, wd_f)))]

    @pl.when(jnp.logical_and(used, new_expert))
    def _():
        slot = slot_ref[i]

        @pl.when(i == 0)
        def _():
            for c in weight_copies(be_ref[i], slot):
                c.start()

        for c in weight_copies(be_ref[i], slot):
            c.wait()
        wg_s[...] = wg_f[slot].astype(BF16)
        wu_s[...] = wu_f[slot].astype(BF16)
        wd_s[...] = wd_f[slot].astype(BF16)

        @pl.when(nxt_ref[i] >= 0)
        def _():
            for c in weight_copies(nxt_ref[i], 1 - slot):
                c.start()

    def swiglu(rows):
        row = rows.start + lax.broadcasted_iota(jnp.int32, (rows.stop - rows.start, x_ref.shape[1]), 0)
        lo, hi = _unpack_bf16_pair(jnp.where(row < nvalid_ref[i], x_ref[rows, :], jnp.uint32(0)))
        lo = lo.astype(BF16)
        hi = hi.astype(BF16)
        half = lo.shape[1]
        gate = _dot(lo, wg_s[0:half, :]) + _dot(hi, wg_s[half:, :])
        up = _dot(lo, wu_s[0:half, :]) + _dot(hi, wu_s[half:, :])
        o_ref[rows, :] = _pack_bf16_pair(_dot((_silu(gate) * up).astype(BF16), wd_s[...]))

    used = i < nused_ref[0]
    half_rows = MOE_TM // 2

    @pl.when(jnp.logical_and(used, nvalid_ref[i] > half_rows))
    def _():
        swiglu(slice(0, MOE_TM))

    @pl.when(jnp.logical_and(used, nvalid_ref[i] <= half_rows))
    def _():
        swiglu(slice(0, half_rows))
        o_ref[half_rows:, :] = jnp.zeros((MOE_TM - half_rows, o_ref.shape[1]), o_ref.dtype)

    @pl.when(jnp.logical_not(used))
    def _():
        o_ref[...] = jnp.zeros_like(o_ref)


def _experts(xs, blk_e, n_used, n_valid, nxt_e, slot, wg, wu, wd, layer):
    rows, dp = xs.shape
    d = 2 * dp
    nblocks = rows // MOE_TM
    de = wg.shape[3]
    blk = lambda i, *_: (i, 0)
    hbm = pl.BlockSpec(memory_space=pl.ANY)
    grid_spec = pltpu.PrefetchScalarGridSpec(
        num_scalar_prefetch=5,
        grid=(nblocks,),
        in_specs=[pl.BlockSpec((MOE_TM, dp), blk), hbm, hbm, hbm],
        out_specs=pl.BlockSpec((MOE_TM, dp), blk),
        scratch_shapes=[pltpu.VMEM((2, d, de), F32), pltpu.VMEM((2, d, de), F32), pltpu.VMEM((2, de, d), F32),
                        pltpu.VMEM((d, de), BF16), pltpu.VMEM((d, de), BF16), pltpu.VMEM((de, d), BF16),
                        pltpu.SemaphoreType.DMA((2, 3))],
    )
    return pl.pallas_call(
        functools.partial(_expert_kernel, layer=layer),
        out_shape=jax.ShapeDtypeStruct((rows, dp), jnp.uint32),
        grid_spec=grid_spec,
        compiler_params=_cp(("arbitrary",)),
        name="moe_experts",
    )(blk_e, n_used, n_valid, nxt_e, slot, xs, wg, wu, wd)


def _final_kernel(*refs):
    fg_ref, o_ref = refs[-2:]
    for s in range(ROW_SUB):
        y = _moe_residual(*refs[5 * s:5 * s + 5])
        o_ref[s * TM:(s + 1) * TM, :] = y * lax.rsqrt(jnp.mean(y * y, axis=-1, keepdims=True) + EPS) * fg_ref[...]


def _final(xn, rows2, route, mod, final_g, nb, nblk):
    t, d = xn.shape
    nlat = nblk - 1
    assert (nb * nlat) % ROW_SUB == 0

    def blk_specs(s):
        lat = lambda i: ROW_SUB * i + s
        src = lambda i: ((lat(i) // nlat) * nblk + 1 + lat(i) % nlat, 0)
        return [pl.BlockSpec((TM, d), src),
                pl.BlockSpec((TM, d // 2), src),
                pl.BlockSpec((TM, d // 2), lambda i: (src(i)[0] + t // TM, 0)),
                pl.BlockSpec((TM, LANES), src),
                pl.BlockSpec((1, 6, d), lambda i: (lat(i) // nlat, 0, 0))]

    return pl.pallas_call(
        _final_kernel,
        out_shape=jax.ShapeDtypeStruct((nb * nlat * TM, d), F32),
        grid=(nb * nlat // ROW_SUB,),
        in_specs=[sp for s in range(ROW_SUB) for sp in blk_specs(s)] + [pl.BlockSpec((1, d), lambda i: (0, 0))],
        out_specs=pl.BlockSpec((ROW_SUB * TM, d), lambda i: (i, 0)),
        compiler_params=_cp(("parallel",)),
        name="moe_combine_final",
    )(*((xn, rows2, rows2, route, mod) * ROW_SUB), final_g.reshape(1, d))


def _moe(h2, route, wg, wu, wd, layer):
    t, d = h2.shape
    n_slots = 2 * t
    experts = jnp.arange(N_EXPERTS, dtype=F32)[None, :]
    oh0 = (route[:, 0:1] == experts).astype(F32)
    oh1 = (route[:, 1:2] == experts).astype(F32)
    both = (oh0 + oh1).reshape(t // LANES, LANES, N_EXPERTS)
    tri = jnp.tril(jnp.ones((LANES, LANES), F32))
    intra = jnp.einsum("ij,bjk->bik", tri, both)
    blk_tot = intra[:, -1, :]
    blk_cum = jnp.cumsum(blk_tot, axis=0)
    earlier = (intra - both + (blk_cum - blk_tot)[:, None, :]).reshape(t, N_EXPERTS)
    counts = blk_cum[-1].astype(jnp.int32)
    pcounts = (counts + MOE_TM - 1) // MOE_TM * MOE_TM
    pends = jnp.cumsum(pcounts)
    pstarts = pends - pcounts
    base = pstarts.astype(F32)[None, :] + earlier
    dest0 = jnp.sum(oh0 * base, axis=1).astype(jnp.int32)
    dest1 = jnp.sum(oh1 * base, axis=1).astype(jnp.int32)
    nblocks = -(-n_slots // MOE_TM) + N_EXPERTS
    nrows = nblocks * MOE_TM
    blk_start = jnp.arange(nblocks, dtype=jnp.int32) * MOE_TM
    blk_e = jnp.minimum(jnp.sum((pends[None, :] <= blk_start[:, None]).astype(jnp.int32), axis=1), N_EXPERTS - 1)
    n_used = (pends[-1] // MOE_TM).astype(jnp.int32).reshape(1)
    n_valid = jnp.clip((pstarts + counts)[blk_e] - blk_start, 0, MOE_TM).astype(jnp.int32)
    ids = jnp.arange(N_EXPERTS, dtype=jnp.int32)
    has = counts > 0
    later = lax.cummin(jnp.where(has, ids, N_EXPERTS)[::-1])[::-1]
    nxt = jnp.concatenate([later[1:], jnp.full((1,), N_EXPERTS, jnp.int32)])
    nxt = jnp.where(nxt >= N_EXPERTS, -1, nxt)
    slot = (jnp.cumsum(has.astype(jnp.int32)) - 1) % 2
    xs = _scatter_rows(h2, dest0, dest1, nrows)
    ys = _experts(xs, blk_e, n_used, n_valid, nxt[blk_e], slot[blk_e], wg, wu, wd, layer)
    return _gather_rows(ys, jnp.concatenate([dest0, dest1]))


def kernel(x, c, ctx, c_ctx, ada_w, ada_b, norm1_g, norm2_g, w_in, w_out, s5_lam_re, s5_lam_im, s5_log_dt, s5_b_re, s5_b_im, s5_c_re, s5_c_im, s5_d, s5_glu_w, s5_glu_b, ga_qn_g, ga_kn_g, ssd_conv_w, ssd_conv_b, ssd_dt_bias, ssd_a_log, ssd_d, ssd_norm_g, wa_sink, moe_coarse_w, moe_coarse_b, moe_fine_w, moe_fine_b, moe_w_gate, moe_w_up, moe_w_down, final_g):
    nb, l, d = x.shape
    lc = ctx.shape[1]
    depth = ada_w.shape[0]
    assert lc == TM and l % TM == 0 and nb <= SUBLANES - 1 and d == D_MODEL
    s_len = lc + l
    nblk = s_len // TM
    t = nb * s_len

    cc = jnp.zeros((SUBLANES, d), F32).at[:nb].set(c).at[nb].set(c_ctx)
    mods = _ada(cc, ada_w, ada_b).reshape(depth, SUBLANES, 6, d)
    cos_t, sin_t = _rope_tables(lc, l)
    w_packed = jax.vmap(_pack_w_in)(w_in)
    s5_tabs = jax.vmap(_s5_params)(s5_lam_re, s5_lam_im, s5_log_dt, s5_b_re, s5_b_im, s5_c_re, s5_c_im, s5_d)
    wrs, brs = jax.vmap(_pack_router)(moe_coarse_w, moe_coarse_b, moe_fine_w, moe_fine_b)

    src = ("first", x.reshape(nb * l, d), ctx.reshape(nb * lc, d))
    for i in range(depth):
        mod = mods[i]
        (xm, xbc, ug, z, dt, gaq, gak, gav, waq, wak, wav) = _inproj(
            src, mod, norm1_g[i], w_packed[i], cos_t, sin_t, ga_qn_g[i], ga_kn_g[i], nb, nblk)
        ys5 = _s5(ug, tuple(tab[i] for tab in s5_tabs), nb, s_len, lc)
        oga, owa = _attn(wa_sink[i], gaq, gak, gav, waq, wak, wav, nb, s_len, lc)
        y0, y1 = _ssd(xbc, dt, ssd_conv_w[i], ssd_conv_b[i], ssd_dt_bias[i], ssd_a_log[i], ssd_d[i], nb, s_len, lc)
        wr, br = wrs[i], brs[i]
        xn, h2, route = _outproj(xm, ys5, oga, y0, y1, z, owa, mod, s5_glu_w[i], s5_glu_b[i], ssd_norm_g[i],
                                 w_out[i], norm2_g[i], wr, br, nb, nblk)
        rows2 = _moe(h2, route, moe_w_gate, moe_w_up, moe_w_down, i)
        src = ("moe", xn, rows2, route, mod)
    return _final(xn, rows2, route, mod, final_g, nb, nblk).reshape(nb, l, d)
```

```python
import functools
import math

import jax
import jax.numpy as jnp
import numpy as np
from jax import lax
from jax.experimental import pallas as pl
from jax.experimental.pallas import tpu as pltpu
from jax.experimental.pallas import tpu_sc as plsc

F32 = jnp.float32
BF16 = jnp.bfloat16
HI = lax.Precision.HIGHEST

D_MODEL = 1024
GRID_W = 64
GROUP_W = 256
HEAD_DIM = 64
ROPE_FREQS = HEAD_DIM // 4
ROPE_BASE = 10000.0
EPS = 1e-6
S5_CH = 16
S5_GROUPS = GROUP_W // S5_CH
S5_STATE = 64
N_HEADS = 4
SSD_HEADS = 4
SSD_NGROUPS = 2
SSD_STATE = 128
SSD_XBC = GROUP_W + 2 * SSD_NGROUPS * SSD_STATE
WINDOW = 128
MOE_GROUPS = 4
MOE_PER_GROUP = 8
N_EXPERTS = 32

LANES = 128
SUBLANES = 8
TM = 256
TQ = 128
GA_TQ = 128
GA_SUB = 2
S5_Q = 32
S5_BLK = S5_Q * S5_CH
MOE_TM = 512
SC_CORES = 2
SC_SUBCORES = 16
SC_GATHER_K = 32
SC_FETCH_K = 64
ROUTE_FINE0 = 32
VMEM_LIMIT = 56 * 1024 * 1024

NEG_INF = float("-inf")
LOG2E = math.log2(math.e)


def _cp(sem, vmem=VMEM_LIMIT):
    return pltpu.CompilerParams(dimension_semantics=sem, vmem_limit_bytes=vmem)


def _dot(a, b):
    return jnp.dot(a, b, preferred_element_type=F32)


def _dot_hi(a, b):
    return jnp.dot(a, b, preferred_element_type=F32, precision=HI)


def _dot_nt(a, b):
    return lax.dot_general(a, b, (((1,), (1,)), ((), ())), preferred_element_type=F32)


def _sigmoid(x):
    return 1.0 / (1.0 + jnp.exp(-x))


def _silu(x):
    return x * _sigmoid(x)


def _gelu_tanh(x):
    return 0.5 * x * (1.0 + jnp.tanh(math.sqrt(2.0 / math.pi) * (x + 0.044715 * (x * x * x))))


def _softplus(x):
    return jnp.maximum(x, 0.0) + jnp.log(1.0 + jnp.exp(-jnp.abs(x)))


_HI16 = 0xFFFF0000


def _pack_bf16_pair(x):
    n = x.shape[1] // 2
    bits = pltpu.bitcast(x.astype(BF16).astype(F32), jnp.uint32)
    return (bits[:, n:] & jnp.uint32(_HI16)) | (bits[:, :n] >> 16)


def _unpack_bf16_pair(w):
    return pltpu.bitcast(w << 16, F32), pltpu.bitcast(w & jnp.uint32(_HI16), F32)


def _per_head_cols(v, base, n_heads, shape):
    lane = lax.broadcasted_iota(jnp.int32, shape, 1)
    out = jnp.broadcast_to(v[:, base + n_heads - 1:base + n_heads], shape)
    for h in range(n_heads - 2, -1, -1):
        out = jnp.where(lane < (h + 1) * HEAD_DIM, v[:, base + h:base + h + 1], out)
    return out


def _ada_kernel(c_ref, w_ref, b_ref, o_ref):
    c = c_ref[...]
    o_ref[0] = _dot_hi(_silu(c), w_ref[0]) + b_ref[0]


def _ada(cc, ada_w, ada_b):
    depth, d, n = ada_w.shape
    tn = 1536
    return pl.pallas_call(
        _ada_kernel,
        out_shape=jax.ShapeDtypeStruct((depth, SUBLANES, n), F32),
        grid=(depth, n // tn),
        in_specs=[pl.BlockSpec((SUBLANES, d), lambda l, j: (0, 0)),
                  pl.BlockSpec((1, d, tn), lambda l, j: (l, 0, j)),
                  pl.BlockSpec((1, 1, tn), lambda l, j: (l, 0, j))],
        out_specs=pl.BlockSpec((1, SUBLANES, tn), lambda l, j: (l, 0, j)),
        compiler_params=_cp(("parallel", "parallel")),
        name="ada_mod",
    )(cc, ada_w, ada_b.reshape(depth, 1, n))


_C_XBC = 0
_C_U = _C_XBC + SSD_XBC
_C_Z = _C_U + GROUP_W
_C_DT = _C_Z + GROUP_W
_C_GAQ = _C_DT + LANES
_C_WAQ = _C_GAQ + N_HEADS * LANES
_C_GAK = _C_WAQ + N_HEADS * LANES
_C_GAV = _C_GAK + LANES
_C_WAK = _C_GAV + LANES
_C_WAV = _C_WAK + LANES
_C_END = _C_WAV + LANES


def _expand_q_cols(wq):
    zero = jnp.zeros((wq.shape[0], HEAD_DIM), wq.dtype)
    parts = []
    for h in range(N_HEADS):
        head = wq[:, h * HEAD_DIM:(h + 1) * HEAD_DIM]
        parts += [head, zero] if h // 2 == 0 else [zero, head]
    return jnp.concatenate(parts, axis=1)


def _pack_w_in(w_in):
    cuts = np.cumsum([256, 256, 128, 128, 256, SSD_XBC, 2 * SSD_HEADS, 256, 128, 128])[:-1]
    u, gaq, gak, gav, z, xbc, dt, waq, wak, wav = jnp.split(w_in, [int(c) for c in cuts], axis=1)
    dt = jnp.pad(dt, ((0, 0), (0, LANES - dt.shape[1])))
    w = jnp.concatenate([xbc, u, z, dt, _expand_q_cols(gaq), _expand_q_cols(waq), gak, gav, wak, wav], axis=1)
    return w.astype(BF16)


def _rope(x, cos, sins):
    w = x.shape[1]
    if w > LANES:
        cos = jnp.concatenate([cos] * (w // LANES), axis=1)
        sins = jnp.concatenate([sins] * (w // LANES), axis=1)
    lane = lax.broadcasted_iota(jnp.int32, x.shape, 1)
    up = pltpu.roll(x, w - ROPE_FREQS, 1)
    dn = pltpu.roll(x, ROPE_FREQS, 1)
    partner = jnp.where((lane & ROPE_FREQS) == 0, up, dn)
    return x * cos + partner * sins


def _v_with_ones(v):
    lo = lax.broadcasted_iota(jnp.int32, v.shape, 1) < HEAD_DIM
    return jnp.concatenate([jnp.where(lo, v, 1.0), jnp.where(lo, 1.0, v)], axis=1).astype(BF16)


def _moe_residual(xn_ref, r0_ref, r1_ref, route_ref, mod_ref):
    route = route_ref[...]
    r0 = jnp.concatenate(_unpack_bf16_pair(r0_ref[...]), axis=1)
    r1 = jnp.concatenate(_unpack_bf16_pair(r1_ref[...]), axis=1)
    return xn_ref[...] + mod_ref[0, 5:6, :] * (route[:, 2:3] * r0 + route[:, 3:4] * r1)


ROW_SUB = 4


def _row_views(refs, s):
    return [r.at[pl.ds(s * TM, TM), :] for r in refs]


def _inproj_kernel(*refs, first, nblk):
    n_blk_in = (2 if first else 5) + 3
    shared = refs[ROW_SUB * n_blk_in:]
    g_ref, w_ref, qn_ref, kn_ref = shared[:4]
    xm_o, xbc_o, ug_o = shared[4:7]
    rest_o = shared[7:-1]
    u_scr = shared[-1]
    for s in range(ROW_SUB):
        blk_refs = refs[s * n_blk_in:(s + 1) * n_blk_in]
        xm_v, xbc_v = _row_views((xm_o, xbc_o), s)
        ug_v = ug_o.at[:, pl.ds(s * S5_TB, S5_TB), :]
        _inproj_block(blk_refs, g_ref, w_ref, qn_ref, kn_ref, xm_v, xbc_v, ug_v, _row_views(rest_o, s), u_scr,
                      first, (pl.program_id(0) * ROW_SUB + s) % nblk == 0)


def _inproj_block(blk_refs, g_ref, w_ref, qn_ref, kn_ref, xm_o, xbc_o, ug_o, rest_o, u_scr, first, is_ctx):
    if first:
        lat_ref, ctx_ref = blk_refs[:2]
        x = jnp.where(is_ctx, ctx_ref[...], lat_ref[...])
    else:
        x = _moe_residual(*blk_refs[:5])
    mod_ref, cos_ref, sin_ref = blk_refs[-3:]
    z_o, dt_o, gaq_o, gak_o, gav_o, waq_o, wak_o, wav_o = rest_o
    xm_o[...] = x
    ms = jnp.mean(x * x, axis=-1, keepdims=True)
    xn = x * lax.rsqrt(ms + EPS) * g_ref[...]
    h = xn * (1.0 + mod_ref[0, 1:2, :]) + mod_ref[0, 0:1, :]
    hb = h.astype(BF16)

    def proj(lo, hi):
        return _dot(hb, w_ref[:, lo:hi])

    cos = cos_ref[...]
    sins = sin_ref[...]
    scale = LOG2E * HEAD_DIM ** -0.5
    q = proj(_C_GAQ, _C_WAQ)
    qs = q * q
    inv = jnp.concatenate(
        [jnp.broadcast_to(lax.rsqrt(jnp.sum(qs[:, s * LANES:(s + 1) * LANES], axis=1, keepdims=True)
                                    * (1.0 / HEAD_DIM) + EPS), (q.shape[0], LANES)) for s in range(N_HEADS)], axis=1)
    gaq_o[...] = (_rope(q * inv * qn_ref[...], cos, sins) * scale).astype(BF16)
    waq_o[...] = (_rope(proj(_C_WAQ, _C_GAK), cos, sins) * scale).astype(BF16)
    k = proj(_C_GAK, _C_GAV)
    ks = k * k
    lane = lax.broadcasted_iota(jnp.int32, k.shape, 1)
    lo = lane < HEAD_DIM
    ms0 = jnp.sum(jnp.where(lo, ks, 0.0), axis=1, keepdims=True)
    ms1 = jnp.sum(jnp.where(lo, 0.0, ks), axis=1, keepdims=True)
    kinv = lax.rsqrt(jnp.where(lo, ms0, ms1) * (1.0 / HEAD_DIM) + EPS)
    gak_o[...] = _rope(k * kinv * kn_ref[...], cos, sins).astype(BF16)
    gav_o[...] = _v_with_ones(proj(_C_GAV, _C_WAK))
    wak_o[...] = _rope(proj(_C_WAK, _C_WAV), cos, sins).astype(BF16)
    wav_o[...] = _v_with_ones(proj(_C_WAV, _C_END))
    xbc_o[...] = proj(_C_XBC, _C_U)
    u = proj(_C_U, _C_Z)
    u_scr[0] = u[:, :LANES]
    u_scr[1] = u[:, LANES:]
    _s5_pack(u_scr.at[0], u_scr.at[1], ug_o)
    z_o[...] = proj(_C_Z, _C_DT)
    dt_o[...] = proj(_C_DT, _C_GAQ)


def _mod_row(i, nblk, nb):
    return jnp.where(i % nblk == 0, nb, i // nblk)


def _inproj(src, mod, norm_g, w_packed, cos_t, sin_t, qn_g, kn_g, nb, nblk):
    first = src[0] == "first"
    d = src[1].shape[1]
    t = nb * nblk * TM
    row = lambda i: (i, 0)
    fix = lambda i: (0, 0)
    nsteps = t // (ROW_SUB * TM)
    assert nsteps * ROW_SUB * TM == t

    def blk_specs(s):
        bid = lambda i: ROW_SUB * i + s
        modspec = pl.BlockSpec((1, 6, d), lambda i: (_mod_row(bid(i), nblk, nb), 0, 0))
        table = pl.BlockSpec((TM, LANES), lambda i: (bid(i) % nblk, 0))
        if first:
            srcs = [pl.BlockSpec((TM, d), lambda i: ((bid(i) // nblk) * (nblk - 1) + jnp.maximum(bid(i) % nblk - 1, 0), 0)),
                    pl.BlockSpec((TM, d), lambda i: (bid(i) // nblk, 0))]
        else:
            srcs = [pl.BlockSpec((TM, d), lambda i: (bid(i), 0)), pl.BlockSpec((TM, d // 2), lambda i: (bid(i), 0)),
                    pl.BlockSpec((TM, d // 2), lambda i: (bid(i) + t // TM, 0)),
                    pl.BlockSpec((TM, LANES), lambda i: (bid(i), 0)), modspec]
        return srcs + [modspec, table, table]

    if first:
        blk_args = tuple(src[1:]) + (mod, cos_t, sin_t)
    else:
        blk_args = (src[1], src[2], src[2], src[3], src[4], mod, cos_t, sin_t)
    outs = [(d, F32), (SSD_XBC, F32), None, (GROUP_W, F32), (LANES, F32),
            (N_HEADS * LANES, BF16), (LANES, BF16), (2 * LANES, BF16),
            (N_HEADS * LANES, BF16), (LANES, BF16), (2 * LANES, BF16)]
    shapes = [jax.ShapeDtypeStruct((t, o[0]), o[1]) if o else
              jax.ShapeDtypeStruct((S5_GROUPS, t // S5_Q, S5_BLK), F32) for o in outs]
    specs = [pl.BlockSpec((ROW_SUB * TM, o[0]), row) if o else
             pl.BlockSpec((S5_GROUPS, ROW_SUB * S5_TB, S5_BLK), lambda i: (0, i, 0)) for o in outs]
    return pl.pallas_call(
        functools.partial(_inproj_kernel, first=first, nblk=nblk),
        out_shape=shapes,
        grid=(nsteps,),
        in_specs=[sp for s in range(ROW_SUB) for sp in blk_specs(s)] + [
                  pl.BlockSpec((1, d), fix),
                  pl.BlockSpec((d, _C_END), fix),
                  pl.BlockSpec((1, N_HEADS * LANES), fix),
                  pl.BlockSpec((1, LANES), fix)],
        out_specs=specs,
        scratch_shapes=[pltpu.VMEM((GROUP_W // LANES, TM, LANES), F32)],
        compiler_params=_cp(("parallel",)),
        name="in_proj",
    )(*(blk_args * ROW_SUB), norm_g.reshape(1, d), w_packed,
      jnp.tile(qn_g, 2 * N_HEADS).reshape(1, -1), jnp.tile(kn_g, 2).reshape(1, -1))


def _rope_tables(lc, l):
    n_rows = l // GRID_W
    rows = np.repeat(np.arange(n_rows), GRID_W)
    cols = np.tile(np.arange(GRID_W), n_rows)
    inv = np.power(np.float32(ROPE_BASE), -np.arange(ROPE_FREQS, dtype=np.float32) / ROPE_FREQS)
    ang = np.stack([rows, cols], axis=-1).astype(np.float32)[..., None] * inv
    cos = np.cos(ang)
    sin = np.sin(ang)
    cos64 = np.stack([cos, cos], axis=2).reshape(l, HEAD_DIM)
    sin64 = np.stack([-sin, sin], axis=2).reshape(l, HEAD_DIM)
    cos64 = np.concatenate([np.ones((lc, HEAD_DIM), np.float32), cos64], axis=0)
    sin64 = np.concatenate([np.zeros((lc, HEAD_DIM), np.float32), sin64], axis=0)
    return (jnp.asarray(np.tile(cos64, (1, 2)), dtype=F32), jnp.asarray(np.tile(sin64, (1, 2)), dtype=F32))


def _merge_heads(o2, kvh):
    tq = o2.shape[0] // 2
    oa, ob = o2[:tq], o2[tq:]
    lane = lax.broadcasted_iota(jnp.int32, oa.shape, 1)
    if kvh == 0:
        return jnp.where(lane < HEAD_DIM, oa, pltpu.roll(ob, HEAD_DIM, 1))
    return jnp.where(lane < HEAD_DIM, pltpu.roll(oa, HEAD_DIM, 1), ob)


def _stack_q(q_ref, rows, kvh):
    return jnp.concatenate([q_ref[rows, (2 * kvh) * LANES:(2 * kvh + 1) * LANES],
                            q_ref[rows, (2 * kvh + 1) * LANES:(2 * kvh + 2) * LANES]], axis=0)


def _ga_attend(q_ref, k_ref, v_ref, o_ref, nkeys):
    k = k_ref[0:nkeys, :]
    for sub in range(GA_SUB):
        rows = slice(sub * GA_TQ, (sub + 1) * GA_TQ)
        scores = [_dot_nt(_stack_q(q_ref, rows, kvh), k) for kvh in range(2)]
        outs = []
        for kvh in range(2):
            s = scores[kvh]
            p = jnp.exp2((s - jnp.max(s, axis=1, keepdims=True)).astype(BF16))
            o2 = _dot(p, v_ref[0:nkeys, kvh * LANES:(kvh + 1) * LANES])
            outs.append(_merge_heads(o2 / pltpu.roll(o2, HEAD_DIM, 1), kvh))
        o_ref[rows, :] = jnp.concatenate(outs, axis=1)


def _attn_kernel(sink_ref, gq_ref, gk_ref, gv_ref, wq_ref, wk_ref, wv_ref, og_ref, ow_ref, *, lc):
    is_ctx = pl.program_id(1) < lc // TM

    @pl.when(is_ctx)
    def _():
        _ga_attend(gq_ref, gk_ref, gv_ref, og_ref, lc)
        _wa_attend(sink_ref, wq_ref, wk_ref, wv_ref, ow_ref, lc)

    @pl.when(jnp.logical_not(is_ctx))
    def _():
        _ga_attend(gq_ref, gk_ref, gv_ref, og_ref, gk_ref.shape[0])
        _wa_attend(sink_ref, wq_ref, wk_ref, wv_ref, ow_ref, lc)


def _attn(sink, gq, gk, gv, wq, wk, wv, nb, s_len, lc):
    t = gq.shape[0]
    nq = s_len // TM
    assert GA_SUB * GA_TQ == TM and WA_SUB * TQ == TM
    qspec = pl.BlockSpec((TM, N_HEADS * LANES), lambda b, j: (b * nq + j, 0))
    kspec = pl.BlockSpec((s_len, LANES), lambda b, j: (b, 0))
    vspec = pl.BlockSpec((s_len, 2 * LANES), lambda b, j: (b, 0))
    ospec = pl.BlockSpec((TM, GROUP_W), lambda b, j: (b * nq + j, 0))
    return pl.pallas_call(
        functools.partial(_attn_kernel, lc=lc),
        out_shape=[jax.ShapeDtypeStruct((t, GROUP_W), F32)] * 2,
        grid=(nb, nq),
        in_specs=[pl.BlockSpec(memory_space=pltpu.SMEM), qspec, kspec, vspec, qspec, kspec, vspec],
        out_specs=[ospec, ospec],
        compiler_params=_cp(("parallel", "arbitrary")),
        name="attention",
    )(sink, gq, gk, gv, wq, wk, wv)


WA_SUB = TM // TQ


def _wa_attend(sink_ref, q_ref, k_ref, v_ref, o_ref, lc):
    s_len = k_ref.shape[0]
    kc = k_ref[0:lc, :]
    row = lax.broadcasted_iota(jnp.int32, (2 * TQ, 1), 0)
    for sub in range(WA_SUB):
        rows = slice(sub * TQ, (sub + 1) * TQ)
        n = pl.program_id(1) * WA_SUB + sub - lc // TQ
        start = pl.multiple_of(jnp.clip(lc + (n - 1) * TQ, lc, s_len - 3 * TQ), TQ)
        kb = k_ref[pl.ds(start, 3 * TQ), :]
        qpos = n * TQ + lax.broadcasted_iota(jnp.int32, (TQ, 3 * TQ), 0)
        kpos = (start - lc) + lax.broadcasted_iota(jnp.int32, (TQ, 3 * TQ), 1)
        reach = jnp.where(n >= 0, WINDOW, -1)
        valid = jnp.abs(qpos - kpos) <= reach
        valid = jnp.concatenate([valid, valid], axis=0)
        outs = []
        for kvh in range(2):
            q2 = jnp.concatenate([q_ref[rows, (2 * kvh) * LANES:(2 * kvh + 1) * LANES],
                                  q_ref[rows, (2 * kvh + 1) * LANES:(2 * kvh + 2) * LANES]], axis=0)
            sc = _dot_nt(q2, kc)
            sb = jnp.where(valid, _dot_nt(q2, kb), NEG_INF)
            sink = jnp.where(row < TQ, sink_ref[2 * kvh], sink_ref[2 * kvh + 1]) * LOG2E
            m = jnp.maximum(jnp.maximum(jnp.max(sc, axis=1, keepdims=True), jnp.max(sb, axis=1, keepdims=True)), sink)
            pc = jnp.exp2((sc - m).astype(BF16))
            pb = jnp.exp2((sb - m).astype(BF16))
            vcols = slice(kvh * LANES, (kvh + 1) * LANES)
            o2 = _dot(pc, v_ref[0:lc, vcols]) + _dot(pb, v_ref[pl.ds(start, 3 * TQ), vcols])
            denom = pltpu.roll(o2, HEAD_DIM, 1) + jnp.exp2(sink - m)
            outs.append(_merge_heads(o2 / denom, kvh))
        o_ref[rows, :] = jnp.concatenate(outs, axis=1)


def _s5_chunk_index(t, rev, nc_ctx, nc_tot):
    if not rev:
        return t
    return jnp.where(t < nc_ctx, nc_ctx - 1 - t, nc_tot - 1 - (t - nc_ctx))


def _s5_kernel(u_ref, k_ref, p_ref, g_ref, ar_ref, ai_ref, dsk_ref, y_ref, s_scr, h_scr, m_scr, *, nb, nc_ctx, nc_tot):
    for d in range(2):
        ext = k_ref[d, 0]
        for s in range(S5_Q):
            lo = ((S5_Q - s) if d == 0 else (S5_Q - 1 - s)) * S5_CH
            win = pltpu.roll(ext, (2 * S5_BLK - lo) % (2 * S5_BLK), 1)[:, :S5_BLK]
            m_scr[d, s * S5_CH:(s + 1) * S5_CH, :] = win.astype(BF16)
    uf = u_ref[0]
    u = uf.astype(BF16)
    for d in range(2):
        for k in range(2):
            s_scr[d, k] = _dot(u, p_ref[d, k, 0])
    ar = [jnp.broadcast_to(ar_ref[d, 0], (nb, LANES)) for d in range(2)]
    ai = [[jnp.broadcast_to(ai_ref[d, k, 0], (nb, LANES)) for k in range(2)] for d in range(2)]

    def body(t, carry):
        out = []
        for d in range(2):
            h, hs = carry[d]
            rows = pl.ds(_s5_chunk_index(t, d == 1, nc_ctx, nc_tot), nb, stride=nc_tot)
            h_scr[d, rows, :] = h
            out.append((ar[d] * h + ai[d][0] * hs + s_scr[d, 0, rows, :],
                        ar[d] * hs + ai[d][1] * h + s_scr[d, 1, rows, :]))
        return tuple(out)

    zero = jnp.zeros((nb, LANES), F32)
    lax.fori_loop(0, nc_tot, body, ((zero, zero), (zero, zero)), unroll=2)
    y = uf * dsk_ref[0]
    for d in range(2):
        y = y + _dot(u, m_scr[d]) + _dot(h_scr[d].astype(BF16), g_ref[d, 0])
    y_ref[0] = y


S5_TB = TM // S5_Q
S5_GPS = LANES // S5_CH


def _s5_pack(lo_ref, hi_ref, o_ref):
    for s in range(S5_Q):
        rows = pl.ds(s, S5_TB, stride=S5_Q)
        halves = (lo_ref[rows, :], hi_ref[rows, :])
        dst = S5_CH * (s % S5_GPS)
        for g in range(S5_GROUPS):
            slab = halves[g // S5_GPS]
            src = S5_CH * (g % S5_GPS)
            moved = slab if src == dst else pltpu.roll(slab, (dst - src) % LANES, 1)
            o_ref[g, :, s * S5_CH:(s + 1) * S5_CH] = moved[:, dst:dst + S5_CH]


def _s5_unpack(y_ref, o_ref):
    lane_grp = lax.broadcasted_iota(jnp.int32, (S5_TB, LANES), 1) // S5_CH
    for s in range(S5_Q):
        src = S5_CH * (s % S5_GPS)
        for half in range(S5_GROUPS // S5_GPS):
            acc = None
            for gl in range(S5_GPS):
                slab = y_ref[half * S5_GPS + gl, :, (s // S5_GPS) * LANES:(s // S5_GPS + 1) * LANES]
                dst = S5_CH * gl
                moved = slab if src == dst else pltpu.roll(slab, (dst - src) % LANES, 1)
                acc = moved if acc is None else jnp.where(lane_grp == gl, moved, acc)
            o_ref[half, pl.ds(s, S5_TB, stride=S5_Q), :] = acc


def _s5_params(lam_re, lam_im, log_dt, b_re, b_im, c_re, c_im, d_skip):
    q = S5_Q
    dt = jnp.exp(log_dt)[..., None]
    lr, li = lam_re, lam_im
    mag = jnp.exp(lr * dt)
    a_re = mag * jnp.cos(li * dt)
    a_im = mag * jnp.sin(li * dt)
    den = lr * lr + li * li
    f_re = ((a_re - 1.0) * lr + a_im * li) / den
    f_im = (a_im * lr - (a_re - 1.0) * li) / den
    bb_re = f_re[..., None] * b_re - f_im[..., None] * b_im
    bb_im = f_re[..., None] * b_im + f_im[..., None] * b_re
    kk = jnp.arange(q + 1, dtype=F32)[:, None, None, None]
    pmag = jnp.exp(kk * (lr * dt))
    pw_re = pmag * jnp.cos(kk * (li * dt))
    pw_im = pmag * jnp.sin(kk * (li * dt))
    lw_re = pw_re[:q].transpose(1, 2, 0, 3)[:, :, :, None, :]
    lw_im = pw_im[:q].transpose(1, 2, 0, 3)[:, :, :, None, :]
    ck_re = c_re[:, :, None] * lw_re - c_im[:, :, None] * lw_im
    ck_im = c_re[:, :, None] * lw_im + c_im[:, :, None] * lw_re
    ck = jnp.concatenate([ck_re, -ck_im], axis=-1).reshape(2, S5_GROUPS, S5_BLK, 2 * S5_STATE)
    kern_t = jnp.einsum("dgmp,dgpc->dgcm", ck, jnp.concatenate([bb_re, bb_im], axis=2), precision=HI)
    kern_t = kern_t.reshape(2, S5_GROUPS, S5_CH, q, S5_CH)
    zeros = jnp.zeros_like(kern_t)
    bbt_re = bb_re.transpose(0, 1, 3, 2)[:, :, None]
    bbt_im = bb_im.transpose(0, 1, 3, 2)[:, :, None]
    ct_re = c_re.transpose(0, 1, 3, 2)[:, :, :, None, :]
    ct_im = c_im.transpose(0, 1, 3, 2)[:, :, :, None, :]
    ms, ps, gs = [], [], []
    for d in range(2):
        ext = (jnp.concatenate([zeros[d], kern_t[d]], axis=2) if d == 0
               else jnp.concatenate([kern_t[d, :, :, ::-1], zeros[d]], axis=2))
        ext = ext.reshape(S5_GROUPS, S5_CH, 2 * S5_BLK)
        ms.append(ext)
        pidx = (q - 1 - jnp.arange(q)) if d == 0 else jnp.arange(q)
        pr = pw_re[pidx, d].transpose(1, 0, 2)[:, :, None, :]
        pi = pw_im[pidx, d].transpose(1, 0, 2)[:, :, None, :]
        p_re = pr * bbt_re[d] - pi * bbt_im[d]
        p_im = pr * bbt_im[d] + pi * bbt_re[d]
        pd = jnp.stack([jnp.concatenate([p_re, p_im], axis=3), jnp.concatenate([p_im, p_re], axis=3)])
        ps.append(pd.reshape(2, S5_GROUPS, S5_BLK, 2 * S5_STATE))
        gidx = (jnp.arange(q) + 1) if d == 0 else (q - jnp.arange(q))
        gw_re = pw_re[gidx, d].transpose(1, 2, 0)[..., None]
        gw_im = pw_im[gidx, d].transpose(1, 2, 0)[..., None]
        g_re = ct_re[d] * gw_re - ct_im[d] * gw_im
        g_im = ct_re[d] * gw_im + ct_im[d] * gw_re
        gs.append(jnp.concatenate([g_re, -g_im], axis=1).reshape(S5_GROUPS, 2 * S5_STATE, S5_BLK))
    ar = jnp.concatenate([pw_re[q], pw_re[q]], axis=-1)[:, :, None, :]
    ai = jnp.stack([jnp.concatenate([-pw_im[q], pw_im[q]], axis=-1),
                    jnp.concatenate([pw_im[q], -pw_im[q]], axis=-1)], axis=1)[:, :, :, None, :]
    dsk = jnp.tile(d_skip.reshape(S5_GROUPS, 1, S5_CH), (1, 1, q))
    return (jnp.stack(ms), jnp.stack(ps).astype(BF16), jnp.stack(gs).astype(BF16),
            ar.astype(F32), ai.astype(F32), dsk.astype(F32))


def _s5(ug, params, nb, s_len, lc):
    m, p, g, ar, ai, dsk = params
    nc_tot = s_len // S5_Q
    nc_ctx = lc // S5_Q
    r = nb * nc_tot
    return pl.pallas_call(
        functools.partial(_s5_kernel, nb=nb, nc_ctx=nc_ctx, nc_tot=nc_tot),
        out_shape=jax.ShapeDtypeStruct((S5_GROUPS, r, S5_BLK), F32),
        grid=(S5_GROUPS,),
        in_specs=[pl.BlockSpec((1, r, S5_BLK), lambda gi: (gi, 0, 0)),
                  pl.BlockSpec((2, 1, S5_CH, 2 * S5_BLK), lambda gi: (0, gi, 0, 0)),
                  pl.BlockSpec((2, 2, 1, S5_BLK, 2 * S5_STATE), lambda gi: (0, 0, gi, 0, 0)),
                  pl.BlockSpec((2, 1, 2 * S5_STATE, S5_BLK), lambda gi: (0, gi, 0, 0)),
                  pl.BlockSpec((2, 1, 1, 2 * S5_STATE), lambda gi: (0, gi, 0, 0)),
                  pl.BlockSpec((2, 2, 1, 1, 2 * S5_STATE), lambda gi: (0, 0, gi, 0, 0)),
                  pl.BlockSpec((1, 1, S5_BLK), lambda gi: (gi, 0, 0))],
        out_specs=pl.BlockSpec((1, r, S5_BLK), lambda gi: (gi, 0, 0)),
        scratch_shapes=[pltpu.VMEM((2, 2, r, 2 * S5_STATE), F32), pltpu.VMEM((2, r, 2 * S5_STATE), F32),
                        pltpu.VMEM((2, S5_BLK, S5_BLK), BF16)],
        compiler_params=_cp(("parallel",)),
        name="s5_scan",
    )(ug, m, p, g, ar, ai, dsk)


CONV_ROWS = 4 * TM


def _conv_kernel(x_ref, prev_ref, next_ref, w_ref, b_ref, o_ref, *, s_len, lc):
    x = x_ref[...]
    rows = x.shape[0]
    ridx = lax.broadcasted_iota(jnp.int32, x.shape, 0)
    pos = (pl.program_id(0) * rows) % s_len + ridx
    pos = jnp.where(pos >= s_len, pos - s_len, pos)
    seg_first = jnp.logical_or(pos == 0, pos == lc)
    seg_last = jnp.logical_or(pos == lc - 1, pos == s_len - 1)
    xm = jnp.where(ridx == 0, prev_ref[SUBLANES - 1:SUBLANES, :], pltpu.roll(x, 1, 0))
    xp = jnp.where(ridx == rows - 1, next_ref[0:1, :], pltpu.roll(x, rows - 1, 0))
    xm = jnp.where(seg_first, 0.0, xm)
    xp = jnp.where(seg_last, 0.0, xp)
    y = xm * w_ref[0:1, :] + x * w_ref[1:2, :] + xp * w_ref[2:3, :] + b_ref[...]
    o_ref[...] = _silu(y)


def _conv(xbc, w, b, s_len, lc):
    t, c = xbc.shape
    rows = next(r for r in (CONV_ROWS, CONV_ROWS // 2, TM) if t % r == 0)
    per = rows // SUBLANES
    last = t // SUBLANES - 1
    return pl.pallas_call(
        functools.partial(_conv_kernel, s_len=s_len, lc=lc),
        out_shape=jax.ShapeDtypeStruct((t, c), F32),
        grid=(t // rows,),
        in_specs=[pl.BlockSpec((rows, c), lambda i: (i, 0)),
                  pl.BlockSpec((SUBLANES, c), lambda i: (jnp.maximum(i * per - 1, 0), 0)),
                  pl.BlockSpec((SUBLANES, c), lambda i: (jnp.minimum((i + 1) * per, last), 0)),
                  pl.BlockSpec((3, c), lambda i: (0, 0)),
                  pl.BlockSpec((1, c), lambda i: (0, 0))],
        out_specs=pl.BlockSpec((rows, c), lambda i: (i, 0)),
        compiler_params=_cp(("parallel",)),
        name="ssd_conv",
    )(xbc, xbc, xbc, w, b.reshape(1, c))


_X_B = GROUP_W
_X_C = GROUP_W + SSD_NGROUPS * SSD_STATE


def _ssd_kernel(xf_ref, dtf_ref, dttf_ref, xr_ref, dtr_ref, dttr_ref, bias_ref, a_ref, biast_ref, at_ref, dsk_ref,
                yf_ref, yr_ref, stf_ref, str_ref):
    @pl.when(pl.program_id(1) == 0)
    def _():
        stf_ref[...] = jnp.zeros_like(stf_ref)
        str_ref[...] = jnp.zeros_like(str_ref)

    par = (bias_ref[...], a_ref[...], biast_ref[...], at_ref[...], dsk_ref[...])
    for j in range(SSD_SUB):
        rf = slice(j * TQ, (j + 1) * TQ)
        rr = slice((SSD_SUB - 1 - j) * TQ, (SSD_SUB - j) * TQ)
        for b in range(xf_ref.shape[0]):
            yf_ref[b, rf, :] = _ssd_chunk_step(xf_ref[b, rf, :], dtf_ref[b, rf, :], dttf_ref[b, :, rf], par,
                                               stf_ref.at[b], False)
            yr_ref[b, rr, :] = _ssd_chunk_step(xr_ref[b, rr, :], dtr_ref[b, rr, :], dttr_ref[b, :, rr], par,
                                               str_ref.at[b], True)


def _ssd_chunk_step(xc, dt_raw, dtt_raw, par, st_ref, rev):
    bias, a_vec, biast, at_vec, dsk = par
    base = SSD_HEADS if rev else 0
    x = xc[:, 0:GROUP_W]
    dt = _softplus(dt_raw + bias)
    a = dt * a_vec
    dtt = _softplus(dtt_raw + biast)
    at = dtt * at_vec
    ri = lax.broadcasted_iota(jnp.int32, (TQ, TQ), 0)
    ci = lax.broadcasted_iota(jnp.int32, (TQ, TQ), 1)
    causal = (ci >= ri) if rev else (ri >= ci)
    tri = jnp.where(causal, 1.0, 0.0)
    cum_c = _dot_hi(tri, a)
    cum_r = _dot_nt_hi(at, tri)
    edge = 0 if rev else TQ - 1
    tot = cum_c[edge:edge + 1, :]

    shape = (TQ, GROUP_W)
    xdt = x * _per_head_cols(dt, base, SSD_HEADS, shape)
    lane = lax.broadcasted_iota(jnp.int32, shape, 1)
    y = jnp.zeros(shape, F32)
    bmat = [xc[:, _X_B + g * SSD_STATE:_X_B + (g + 1) * SSD_STATE].astype(BF16) for g in range(SSD_NGROUPS)]
    cmat = [xc[:, _X_C + g * SSD_STATE:_X_C + (g + 1) * SSD_STATE].astype(BF16) for g in range(SSD_NGROUPS)]
    cb = [_dot_nt(cmat[g], bmat[g]) for g in range(SSD_NGROUPS)]
    for h in range(SSD_HEADS):
        col = base + h
        seg = jnp.where(causal, cum_c[:, col:col + 1] - cum_r[col:col + 1, :], NEG_INF)
        scores = cb[h // 2] * jnp.exp(seg)
        xh = jnp.where((lane >= h * HEAD_DIM) & (lane < (h + 1) * HEAD_DIM), xdt, 0.0)
        y = y + _dot(scores.astype(BF16), xh.astype(BF16))
    st = st_ref[...]
    yo = jnp.concatenate(
        [_dot_nt(cmat[g], st[g * SSD_STATE:(g + 1) * SSD_STATE].astype(BF16)) for g in range(SSD_NGROUPS)], axis=1)
    y = y + yo * _per_head_cols(jnp.exp(cum_c), base, SSD_HEADS, shape)
    if not rev:
        y = y + x * dsk
    xd = xdt * _per_head_cols(jnp.exp(tot - cum_c), base, SSD_HEADS, shape)
    xdt_t = xd.T.astype(BF16)
    decay = jnp.exp(tot)
    for g in range(SSD_NGROUPS):
        new = _dot(xdt_t[g * SSD_STATE:(g + 1) * SSD_STATE], bmat[g])
        for hh in range(2):
            h = 2 * g + hh
            r0 = h * HEAD_DIM
            st_ref[r0:r0 + HEAD_DIM, :] = (decay[:, base + h:base + h + 1] * st[r0:r0 + HEAD_DIM]
                                           + new[hh * HEAD_DIM:(hh + 1) * HEAD_DIM])
    return y


def _dot_nt_hi(a, b):
    return lax.dot_general(a, b, (((1,), (1,)), ((), ())), preferred_element_type=F32, precision=HI)


def _ssd_chunk(c, rev, nc_ctx, nc_tot):
    if not rev:
        return c
    return jnp.where(c < nc_ctx, nc_ctx - 1 - c, nc_tot - 1 - (c - nc_ctx))


SSD_SUB = TM // TQ
SSD_NB = 4


def _ssd_scan(xc, dt, dtt, bias, a, biast, at, dsk, nb, s_len, lc):
    t = xc.shape[0]
    nblk = s_len // TM
    nctx = lc // TM
    nbs = math.gcd(nb, SSD_NB)
    fix = lambda b, c: (0, 0)
    xc3 = xc.reshape(nb, s_len, SSD_XBC)
    dt3 = dt.reshape(nb, s_len, LANES)

    def rows(rev):
        return lambda b, c: (b, _ssd_chunk(c, rev, nctx, nblk), 0)

    def lanes(rev):
        return lambda b, c: (b, 0, _ssd_chunk(c, rev, nctx, nblk))

    def data_specs(rev):
        return [pl.BlockSpec((nbs, TM, SSD_XBC), rows(rev)), pl.BlockSpec((nbs, TM, LANES), rows(rev)),
                pl.BlockSpec((nbs, SUBLANES, TM), lanes(rev))]

    state = pltpu.VMEM((nbs, SSD_HEADS * HEAD_DIM, SSD_STATE), F32)
    yf, yr = pl.pallas_call(
        _ssd_kernel,
        out_shape=[jax.ShapeDtypeStruct((nb, s_len, GROUP_W), F32)] * 2,
        grid=(nb // nbs, nblk),
        in_specs=data_specs(False) + data_specs(True) + [
            pl.BlockSpec((1, LANES), fix), pl.BlockSpec((1, LANES), fix),
            pl.BlockSpec((SUBLANES, TQ), fix), pl.BlockSpec((SUBLANES, TQ), fix),
            pl.BlockSpec((1, GROUP_W), fix)],
        out_specs=[pl.BlockSpec((nbs, TM, GROUP_W), rows(False)), pl.BlockSpec((nbs, TM, GROUP_W), rows(True))],
        scratch_shapes=[state, state],
        compiler_params=_cp(("parallel", "arbitrary")),
        name="ssd_scan",
    )(xc3, dt3, dtt, xc3, dt3, dtt, bias, a, biast, at, dsk)
    return yf.reshape(t, GROUP_W), yr.reshape(t, GROUP_W)


def _ssd(xbc, dt, conv_w, conv_b, dt_bias, a_log, d_skip, nb, s_len, lc):
    xc = _conv(xbc, conv_w, conv_b, s_len, lc)
    nd = 2 * SSD_HEADS
    dtt = dt[:, :nd].reshape(nb, s_len, nd).transpose(0, 2, 1)
    bias = jnp.pad(dt_bias.reshape(1, nd), ((0, 0), (0, LANES - nd)))
    a = jnp.pad(-jnp.exp(a_log).reshape(1, nd), ((0, 0), (0, LANES - nd)))
    biast = jnp.broadcast_to(dt_bias.reshape(nd, 1), (nd, TQ))
    at = jnp.broadcast_to(-jnp.exp(a_log).reshape(nd, 1), (nd, TQ))
    dsk = jnp.repeat(d_skip, HEAD_DIM).reshape(1, GROUP_W)
    return _ssd_scan(xc, dt, dtt, bias, a, biast, at, dsk, nb, s_len, lc)


def _outproj_kernel(x_ref, ys5_ref, oga_ref, y0_ref, y1_ref, z_ref, owa_ref, *refs):
    mods, shared = refs[:ROW_SUB], refs[ROW_SUB:]
    for s in range(ROW_SUB):
        rows = _row_views((x_ref, oga_ref, y0_ref, y1_ref, z_ref, owa_ref) + tuple(shared[-4:-1]), s)
        _outproj_block(rows[0], ys5_ref.at[:, pl.ds(s * S5_TB, S5_TB), :], *rows[1:6], mods[s], *shared[:-4],
                       *rows[6:], shared[-1])


def _outproj_block(x_ref, ys5_ref, oga_ref, y0_ref, y1_ref, z_ref, owa_ref, mod_ref, gluw_ref, glub_ref,
                   ng_ref, wout_ref, n2_ref, wr_ref, br_ref, xn_o, h2_o, route_o, y_scr):
    _s5_unpack(ys5_ref, y_scr)
    gl = _gelu_tanh(jnp.concatenate([y_scr[0], y_scr[1]], axis=1))
    a = gl * _sigmoid(_dot(gl.astype(BF16), gluw_ref[...]) + glub_ref[...])
    m = (y0_ref[...] + y1_ref[...]) * _silu(z_ref[...])
    m = m * lax.rsqrt(jnp.mean(m * m, axis=-1, keepdims=True) + EPS) * ng_ref[...]
    w = wout_ref
    mix = (_dot(a.astype(BF16), w[0:GROUP_W, :]) + _dot(oga_ref[...].astype(BF16), w[GROUP_W:2 * GROUP_W, :])
           + _dot(m.astype(BF16), w[2 * GROUP_W:3 * GROUP_W, :]) + _dot(owa_ref[...].astype(BF16), w[3 * GROUP_W:, :]))
    xn = x_ref[...] + mod_ref[0, 2:3, :] * mix
    xn_o[...] = xn
    h2 = xn * lax.rsqrt(jnp.mean(xn * xn, axis=-1, keepdims=True) + EPS) * n2_ref[...]
    h2 = h2 * (1.0 + mod_ref[0, 4:5, :]) + mod_ref[0, 3:4, :]
    h2_o[...] = _pack_bf16_pair(h2)
    h_hi = h2.astype(BF16)
    h_lo = (h2 - h_hi.astype(F32)).astype(BF16)
    logits = _dot(h_hi, wr_ref[0]) + (_dot(h_lo, wr_ref[0]) + _dot(h_hi, wr_ref[1])) + br_ref[...]
    lane = lax.broadcasted_iota(jnp.int32, logits.shape, 1).astype(F32)
    big = float(4 * LANES)
    lcoarse = jnp.where(lane < MOE_GROUPS, logits, NEG_INF)
    mx = jnp.max(lcoarse, axis=1, keepdims=True)
    den = jnp.sum(jnp.exp(lcoarse - mx), axis=1, keepdims=True)
    grp = jnp.min(jnp.where(lcoarse == mx, lane, big), axis=1, keepdims=True)
    pg = 1.0 / den
    lo = ROUTE_FINE0 + grp * MOE_PER_GROUP
    lf = jnp.where(lane >= lo, jnp.where(lane < lo + MOE_PER_GROUP, logits, NEG_INF), NEG_INF)
    v1 = jnp.max(lf, axis=1, keepdims=True)
    i1 = jnp.min(jnp.where(lf == v1, lane, big), axis=1, keepdims=True)
    lf2 = jnp.where(lane == i1, NEG_INF, lf)
    v2 = jnp.max(lf2, axis=1, keepdims=True)
    i2 = jnp.min(jnp.where(lf2 == v2, lane, big), axis=1, keepdims=True)
    e2 = jnp.exp(v2 - v1)
    w1 = pg / (1.0 + e2)
    w2 = w1 * e2
    route = jnp.where(lane == 0, i1 - ROUTE_FINE0,
                      jnp.where(lane == 1, i2 - ROUTE_FINE0,
                                jnp.where(lane == 2, w1, jnp.where(lane == 3, w2, 0.0))))
    route_o[...] = route


def _outproj(x, ys5, oga, y0, y1, z, owa, mod, glu_w, glu_b, ssd_norm_g, w_out, norm2_g, wr, br, nb, nblk):
    t, d = x.shape
    row = lambda i: (i, 0)
    fix = lambda i: (0, 0)
    step = ROW_SUB * TM
    assert t % step == 0
    gw = pl.BlockSpec((step, GROUP_W), row)
    wr_hi = wr.astype(BF16)
    mod_specs = [pl.BlockSpec((1, 6, d), lambda i, s=s: (_mod_row(ROW_SUB * i + s, nblk, nb), 0, 0))
                 for s in range(ROW_SUB)]
    return pl.pallas_call(
        _outproj_kernel,
        out_shape=[jax.ShapeDtypeStruct((t, d), F32), jax.ShapeDtypeStruct((t, d // 2), jnp.uint32),
                   jax.ShapeDtypeStruct((t, LANES), F32)],
        grid=(t // step,),
        in_specs=[pl.BlockSpec((step, d), row),
                  pl.BlockSpec((S5_GROUPS, ROW_SUB * S5_TB, S5_BLK), lambda i: (0, i, 0)),
                  gw, gw, gw, gw, gw] + mod_specs + [
                  pl.BlockSpec((GROUP_W, GROUP_W), fix),
                  pl.BlockSpec((1, GROUP_W), fix),
                  pl.BlockSpec((1, GROUP_W), fix),
                  pl.BlockSpec((d, d), fix),
                  pl.BlockSpec((1, d), fix),
                  pl.BlockSpec((2, d, LANES), lambda i: (0, 0, 0)),
                  pl.BlockSpec((1, LANES), fix)],
        out_specs=[pl.BlockSpec((step, d), row), pl.BlockSpec((step, d // 2), row), pl.BlockSpec((step, LANES), row)],
        scratch_shapes=[pltpu.VMEM((GROUP_W // LANES, TM, LANES), F32)],
        compiler_params=_cp(("parallel",)),
        name="out_proj_router",
    )(x, ys5, oga, y0, y1, z, owa, *([mod] * ROW_SUB), glu_w.astype(BF16), glu_b.reshape(1, -1), ssd_norm_g.reshape(1, -1),
      w_out.astype(BF16), norm2_g.reshape(1, -1), jnp.stack([wr_hi, (wr - wr_hi.astype(F32)).astype(BF16)]), br)


def _pack_router(coarse_w, coarse_b, fine_w, fine_b):
    def lanes(coarse, fine):
        gap = jnp.zeros(coarse.shape[:-1] + (ROUTE_FINE0 - MOE_GROUPS,), F32)
        tail = jnp.zeros(coarse.shape[:-1] + (LANES - ROUTE_FINE0 - N_EXPERTS,), F32)
        return jnp.concatenate([coarse, gap, fine, tail], axis=-1)

    return lanes(coarse_w, fine_w), lanes(coarse_b[None, :], fine_b[None, :])


def _gather_rows(src, idx):
    m = idx.shape[0]
    d = src.shape[1]
    workers = SC_CORES * SC_SUBCORES
    k = SC_FETCH_K
    nch = m // (workers * k)
    assert nch * workers * k == m
    paired = nch - nch % 2
    mesh = plsc.VectorSubcoreMesh(core_axis_name="c", subcore_axis_name="s")

    @functools.partial(
        pl.kernel, mesh=mesh,
        out_type=jax.ShapeDtypeStruct((m, d), src.dtype),
        scratch_types=[pltpu.VMEM((nch, k), jnp.int32),
                       pltpu.VMEM((2, k, d), src.dtype),
                       pltpu.SemaphoreType.DMA((2,)),
                       pltpu.SemaphoreType.DMA((2,))],
    )
    def gather(src_hbm, idx_hbm, out_hbm, idx_v, rows_v, gsem, wsem):
        wid = lax.axis_index("s") * SC_CORES + lax.axis_index("c")
        pltpu.sync_copy(idx_hbm.at[wid], idx_v)

        def fetch(j, b):
            return pltpu.make_async_copy(src_hbm.at[idx_v.at[j]], rows_v.at[b], gsem.at[b])

        def flush(j, b):
            off = pl.multiple_of((wid * nch + j) * k, k)
            return pltpu.make_async_copy(rows_v.at[b], out_hbm.at[pl.ds(off, k)], wsem.at[b])

        fetch(0, 0).start()

        @pl.loop(0, paired, step=2)
        def _(j):
            for b in range(2):
                cur = j + b
                fetch(cur, b).wait()

                @pl.when(cur >= 1)
                def _():
                    flush(cur - 1, 1 - b).wait()

                @pl.when(cur + 1 < nch)
                def _():
                    fetch(cur + 1, 1 - b).start()

                flush(cur, b).start()

        if nch % 2:
            fetch(nch - 1, 0).wait()
            if nch > 1:
                flush(nch - 2, 1).wait()
            flush(nch - 1, 0).start()
            flush(nch - 1, 0).wait()
        else:
            flush(nch - 1, 1).wait()

    return gather(src, idx.reshape(workers, nch, k))


def _scatter_rows(src, dst0, dst1, nrows):
    t, d = src.shape
    workers = SC_CORES * SC_SUBCORES
    nch = t // (workers * SC_GATHER_K)
    assert nch * workers * SC_GATHER_K == t
    mesh = plsc.VectorSubcoreMesh(core_axis_name="c", subcore_axis_name="s")

    @functools.partial(
        pl.kernel, mesh=mesh,
        out_type=jax.ShapeDtypeStruct((nrows, d), src.dtype),
        scratch_types=[pltpu.VMEM((nch, SC_GATHER_K), jnp.int32),
                       pltpu.VMEM((nch, SC_GATHER_K), jnp.int32),
                       pltpu.VMEM((SC_GATHER_K, d), src.dtype),
                       pltpu.SemaphoreType.DMA((2,))],
    )
    def scatter(src_hbm, d0_hbm, d1_hbm, out_hbm, i0_v, i1_v, rows_v, sem):
        wid = lax.axis_index("s") * SC_CORES + lax.axis_index("c")
        pltpu.sync_copy(d0_hbm.at[wid], i0_v)
        pltpu.sync_copy(d1_hbm.at[wid], i1_v)

        @pl.loop(0, nch)
        def _(j):
            off = pl.multiple_of((wid * nch + j) * SC_GATHER_K, SC_GATHER_K)
            pltpu.sync_copy(src_hbm.at[pl.ds(off, SC_GATHER_K)], rows_v)
            first = pltpu.async_copy(rows_v, out_hbm.at[i0_v.at[j]], sem.at[0])
            second = pltpu.async_copy(rows_v, out_hbm.at[i1_v.at[j]], sem.at[1])
            first.wait()
            second.wait()

    return scatter(src, dst0.reshape(workers, nch, SC_GATHER_K), dst1.reshape(workers, nch, SC_GATHER_K))


def _expert_kernel(be_ref, nused_ref, nvalid_ref, nxt_ref, slot_ref, x_ref, wg_hbm, wu_hbm, wd_hbm, o_ref,
                   wg_f, wu_f, wd_f, wg_s, wu_s, wd_s, sem, *, layer):
    i = pl.program_id(0)
    used = i < nused_ref[0]
    new_expert = jnp.logical_or(i == 0, be_ref[i] != be_ref[jnp.maximum(i - 1, 0)])

    def weight_copies(expert, slot):
        return [pltpu.make_async_copy(w.at[layer, expert], f.at[slot], sem.at[slot, j])
                for j, (w, f) in enumerate(((wg_hbm, wg_f), (wu_hbm, wu_f), (wd_hbm, wd_f)))]

    @pl.when(jnp.logical_and(used, new_expert))
    def _():
        slot = slot_ref[i]

        @pl.when(i == 0)
        def _():
            for c in weight_copies(be_ref[i], slot):
                c.start()

        for c in weight_copies(be_ref[i], slot):
            c.wait()
        wg_s[...] = wg_f[slot].astype(BF16)
        wu_s[...] = wu_f[slot].astype(BF16)
        wd_s[...] = wd_f[slot].astype(BF16)

        @pl.when(nxt_ref[i] >= 0)
        def _():
            for c in weight_copies(nxt_ref[i], 1 - slot):
                c.start()

    def swiglu(rows):
        row = rows.start + lax.broadcasted_iota(jnp.int32, (rows.stop - rows.start, x_ref.shape[1]), 0)
        lo, hi = _unpack_bf16_pair(jnp.where(row < nvalid_ref[i], x_ref[rows, :], jnp.uint32(0)))
        lo = lo.astype(BF16)
        hi = hi.astype(BF16)
        half = lo.shape[1]
        gate = _dot(lo, wg_s[0:half, :]) + _dot(hi, wg_s[half:, :])
        up = _dot(lo, wu_s[0:half, :]) + _dot(hi, wu_s[half:, :])
        o_ref[rows, :] = _pack_bf16_pair(_dot((_silu(gate) * up).astype(BF16), wd_s[...]))

    used = i < nused_ref[0]
    half_rows = MOE_TM // 2

    @pl.when(jnp.logical_and(used, nvalid_ref[i] > half_rows))
    def _():
        swiglu(slice(0, MOE_TM))

    @pl.when(jnp.logical_and(used, nvalid_ref[i] <= half_rows))
    def _():
        swiglu(slice(0, half_rows))
        o_ref[half_rows:, :] = jnp.zeros((MOE_TM - half_rows, o_ref.shape[1]), o_ref.dtype)

    @pl.when(jnp.logical_not(used))
    def _():
        o_ref[...] = jnp.zeros_like(o_ref)


def _experts(xs, blk_e, n_used, n_valid, nxt_e, slot, wg, wu, wd, layer):
    rows, dp = xs.shape
    d = 2 * dp
    nblocks = rows // MOE_TM
    de = wg.shape[3]
    blk = lambda i, *_: (i, 0)
    hbm = pl.BlockSpec(memory_space=pl.ANY)
    grid_spec = pltpu.PrefetchScalarGridSpec(
        num_scalar_prefetch=5,
        grid=(nblocks,),
        in_specs=[pl.BlockSpec((MOE_TM, dp), blk), hbm, hbm, hbm],
        out_specs=pl.BlockSpec((MOE_TM, dp), blk),
        scratch_shapes=[pltpu.VMEM((2, d, de), F32), pltpu.VMEM((2, d, de), F32), pltpu.VMEM((2, de, d), F32),
                        pltpu.VMEM((d, de), BF16), pltpu.VMEM((d, de), BF16), pltpu.VMEM((de, d), BF16),
                        pltpu.SemaphoreType.DMA((2, 3))],
    )
    return pl.pallas_call(
        functools.partial(_expert_kernel, layer=layer),
        out_shape=jax.ShapeDtypeStruct((rows, dp), jnp.uint32),
        grid_spec=grid_spec,
        compiler_params=_cp(("arbitrary",)),
        name="moe_experts",
    )(blk_e, n_used, n_valid, nxt_e, slot, xs, wg, wu, wd)


def _final_kernel(*refs):
    fg_ref, o_ref = refs[-2:]
    for s in range(ROW_SUB):
        y = _moe_residual(*refs[5 * s:5 * s + 5])
        o_ref[s * TM:(s + 1) * TM, :] = y * lax.rsqrt(jnp.mean(y * y, axis=-1, keepdims=True) + EPS) * fg_ref[...]


def _final(xn, rows2, route, mod, final_g, nb, nblk):
    t, d = xn.shape
    nlat = nblk - 1
    assert (nb * nlat) % ROW_SUB == 0

    def blk_specs(s):
        lat = lambda i: ROW_SUB * i + s
        src = lambda i: ((lat(i) // nlat) * nblk + 1 + lat(i) % nlat, 0)
        return [pl.BlockSpec((TM, d), src),
                pl.BlockSpec((TM, d // 2), src),
                pl.BlockSpec((TM, d // 2), lambda i: (src(i)[0] + t // TM, 0)),
                pl.BlockSpec((TM, LANES), src),
                pl.BlockSpec((1, 6, d), lambda i: (lat(i) // nlat, 0, 0))]

    return pl.pallas_call(
        _final_kernel,
        out_shape=jax.ShapeDtypeStruct((nb * nlat * TM, d), F32),
        grid=(nb * nlat // ROW_SUB,),
        in_specs=[sp for s in range(ROW_SUB) for sp in blk_specs(s)] + [pl.BlockSpec((1, d), lambda i: (0, 0))],
        out_specs=pl.BlockSpec((ROW_SUB * TM, d), lambda i: (i, 0)),
        compiler_params=_cp(("parallel",)),
        name="moe_combine_final",
    )(*((xn, rows2, rows2, route, mod) * ROW_SUB), final_g.reshape(1, d))


def _moe(h2, route, wg, wu, wd, layer):
    t, d = h2.shape
    n_slots = 2 * t
    experts = jnp.arange(N_EXPERTS, dtype=F32)[None, :]
    oh0 = (route[:, 0:1] == experts).astype(F32)
    oh1 = (route[:, 1:2] == experts).astype(F32)
    both = (oh0 + oh1).reshape(t // LANES, LANES, N_EXPERTS)
    tri = jnp.tril(jnp.ones((LANES, LANES), F32))
    intra = jnp.einsum("ij,bjk->bik", tri, both)
    blk_tot = intra[:, -1, :]
    blk_cum = jnp.cumsum(blk_tot, axis=0)
    earlier = (intra - both + (blk_cum - blk_tot)[:, None, :]).reshape(t, N_EXPERTS)
    counts = blk_cum[-1].astype(jnp.int32)
    pcounts = (counts + MOE_TM - 1) // MOE_TM * MOE_TM
    pends = jnp.cumsum(pcounts)
    pstarts = pends - pcounts
    base = pstarts.astype(F32)[None, :] + earlier
    dest0 = jnp.sum(oh0 * base, axis=1).astype(jnp.int32)
    dest1 = jnp.sum(oh1 * base, axis=1).astype(jnp.int32)
    nblocks = -(-n_slots // MOE_TM) + N_EXPERTS
    nrows = nblocks * MOE_TM
    blk_start = jnp.arange(nblocks, dtype=jnp.int32) * MOE_TM
    blk_e = jnp.minimum(jnp.sum((pends[None, :] <= blk_start[:, None]).astype(jnp.int32), axis=1), N_EXPERTS - 1)
    n_used = (pends[-1] // MOE_TM).astype(jnp.int32).reshape(1)
    n_valid = jnp.clip((pstarts + counts)[blk_e] - blk_start, 0, MOE_TM).astype(jnp.int32)
    ids = jnp.arange(N_EXPERTS, dtype=jnp.int32)
    has = counts > 0
    later = lax.cummin(jnp.where(has, ids, N_EXPERTS)[::-1])[::-1]
    nxt = jnp.concatenate([later[1:], jnp.full((1,), N_EXPERTS, jnp.int32)])
    nxt = jnp.where(nxt >= N_EXPERTS, -1, nxt)
    slot = (jnp.cumsum(has.astype(jnp.int32)) - 1) % 2
    xs = _scatter_rows(h2, dest0, dest1, nrows)
    ys = _experts(xs, blk_e, n_used, n_valid, nxt[blk_e], slot[blk_e], wg, wu, wd, layer)
    return _gather_rows(ys, jnp.concatenate([dest0, dest1]))


def kernel(x, c, ctx, c_ctx, ada_w, ada_b, norm1_g, norm2_g, w_in, w_out, s5_lam_re, s5_lam_im, s5_log_dt, s5_b_re, s5_b_im, s5_c_re, s5_c_im, s5_d, s5_glu_w, s5_glu_b, ga_qn_g, ga_kn_g, ssd_conv_w, ssd_conv_b, ssd_dt_bias, ssd_a_log, ssd_d, ssd_norm_g, wa_sink, moe_coarse_w, moe_coarse_b, moe_fine_w, moe_fine_b, moe_w_gate, moe_w_up, moe_w_down, final_g):
    nb, l, d = x.shape
    lc = ctx.shape[1]
    depth = ada_w.shape[0]
    assert lc == TM and l % TM == 0 and nb <= SUBLANES - 1 and d == D_MODEL
    s_len = lc + l
    nblk = s_len // TM
    t = nb * s_len

    cc = jnp.zeros((SUBLANES, d), F32).at[:nb].set(c).at[nb].set(c_ctx)
    mods = _ada(cc, ada_w, ada_b).reshape(depth, SUBLANES, 6, d)
    cos_t, sin_t = _rope_tables(lc, l)
    w_packed = jax.vmap(_pack_w_in)(w_in)
    s5_tabs = jax.vmap(_s5_params)(s5_lam_re, s5_lam_im, s5_log_dt, s5_b_re, s5_b_im, s5_c_re, s5_c_im, s5_d)
    wrs, brs = jax.vmap(_pack_router)(moe_coarse_w, moe_coarse_b, moe_fine_w, moe_fine_b)

    src = ("first", x.reshape(nb * l, d), ctx.reshape(nb * lc, d))
    for i in range(depth):
        mod = mods[i]
        (xm, xbc, ug, z, dt, gaq, gak, gav, waq, wak, wav) = _inproj(
            src, mod, norm1_g[i], w_packed[i], cos_t, sin_t, ga_qn_g[i], ga_kn_g[i], nb, nblk)
        ys5 = _s5(ug, tuple(tab[i] for tab in s5_tabs), nb, s_len, lc)
        oga, owa = _attn(wa_sink[i], gaq, gak, gav, waq, wak, wav, nb, s_len, lc)
        y0, y1 = _ssd(xbc, dt, ssd_conv_w[i], ssd_conv_b[i], ssd_dt_bias[i], ssd_a_log[i], ssd_d[i], nb, s_len, lc)
        wr, br = wrs[i], brs[i]
        xn, h2, route = _outproj(xm, ys5, oga, y0, y1, z, owa, mod, s5_glu_w[i], s5_glu_b[i], ssd_norm_g[i],
                                 w_out[i], norm2_g[i], wr, br, nb, nblk)
        rows2 = _moe(h2, route, moe_w_gate, moe_w_up, moe_w_down, i)
        src = ("moe", xn, rows2, route, mod)
    return _final(xn, rows2, route, mod, final_g, nb, nblk).reshape(nb, l, d)
```

```python
import functools
import math

import jax
import jax.numpy as jnp
import numpy as np
from jax import lax
from jax.experimental import pallas as pl
from jax.experimental.pallas import tpu as pltpu
from jax.experimental.pallas import tpu_sc as plsc

F32 = jnp.float32
BF16 = jnp.bfloat16
HI = lax.Precision.HIGHEST

D_MODEL = 1024
GRID_W = 64
GROUP_W = 256
HEAD_DIM = 64
ROPE_FREQS = HEAD_DIM // 4
ROPE_BASE = 10000.0
EPS = 1e-6
S5_CH = 16
S5_GROUPS = GROUP_W // S5_CH
S5_STATE = 64
N_HEADS = 4
SSD_HEADS = 4
SSD_NGROUPS = 2
SSD_STATE = 128
SSD_XBC = GROUP_W + 2 * SSD_NGROUPS * SSD_STATE
WINDOW = 128
MOE_GROUPS = 4
MOE_PER_GROUP = 8
N_EXPERTS = 32

LANES = 128
SUBLANES = 8
TM = 256
TQ = 128
GA_TQ = 128
GA_SUB = 2
S5_Q = 32
S5_BLK = S5_Q * S5_CH
MOE_TM = 512
SC_CORES = 2
SC_SUBCORES = 16
SC_GATHER_K = 32
SC_FETCH_K = 64
ROUTE_FINE0 = 32
VMEM_LIMIT = 56 * 1024 * 1024

NEG_INF = float("-inf")
LOG2E = math.log2(math.e)


def _cp(sem, vmem=VMEM_LIMIT):
    return pltpu.CompilerParams(dimension_semantics=sem, vmem_limit_bytes=vmem)


def _dot(a, b):
    return jnp.dot(a, b, preferred_element_type=F32)


def _dot_hi(a, b):
    return jnp.dot(a, b, preferred_element_type=F32, precision=HI)


def _dot_nt(a, b):
    return lax.dot_general(a, b, (((1,), (1,)), ((), ())), preferred_element_type=F32)


def _sigmoid(x):
    return 1.0 / (1.0 + jnp.exp(-x))


def _silu(x):
    return x * _sigmoid(x)


def _gelu_tanh(x):
    return 0.5 * x * (1.0 + jnp.tanh(math.sqrt(2.0 / math.pi) * (x + 0.044715 * (x * x * x))))


def _softplus(x):
    return jnp.maximum(x, 0.0) + jnp.log(1.0 + jnp.exp(-jnp.abs(x)))


_HI16 = 0xFFFF0000


def _pack_bf16_pair(x):
    n = x.shape[1] // 2
    bits = pltpu.bitcast(x.astype(BF16).astype(F32), jnp.uint32)
    return (bits[:, n:] & jnp.uint32(_HI16)) | (bits[:, :n] >> 16)


def _unpack_bf16_pair(w):
    return pltpu.bitcast(w << 16, F32), pltpu.bitcast(w & jnp.uint32(_HI16), F32)


def _per_head_cols(v, base, n_heads, shape):
    lane = lax.broadcasted_iota(jnp.int32, shape, 1)
    out = jnp.broadcast_to(v[:, base + n_heads - 1:base + n_heads], shape)
    for h in range(n_heads - 2, -1, -1):
        out = jnp.where(lane < (h + 1) * HEAD_DIM, v[:, base + h:base + h + 1], out)
    return out


def _ada_kernel(c_ref, w_ref, b_ref, o_ref):
    c = c_ref[...]
    o_ref[0] = _dot_hi(_silu(c), w_ref[0]) + b_ref[0]


def _ada(cc, ada_w, ada_b):
    depth, d, n = ada_w.shape
    tn = 1536
    return pl.pallas_call(
        _ada_kernel,
        out_shape=jax.ShapeDtypeStruct((depth, SUBLANES, n), F32),
        grid=(depth, n // tn),
        in_specs=[pl.BlockSpec((SUBLANES, d), lambda l, j: (0, 0)),
                  pl.BlockSpec((1, d, tn), lambda l, j: (l, 0, j)),
                  pl.BlockSpec((1, 1, tn), lambda l, j: (l, 0, j))],
        out_specs=pl.BlockSpec((1, SUBLANES, tn), lambda l, j: (l, 0, j)),
        compiler_params=_cp(("parallel", "parallel")),
        name="ada_mod",
    )(cc, ada_w, ada_b.reshape(depth, 1, n))


_C_XBC = 0
_C_U = _C_XBC + SSD_XBC
_C_Z = _C_U + GROUP_W
_C_DT = _C_Z + GROUP_W
_C_GAQ = _C_DT + LANES
_C_WAQ = _C_GAQ + N_HEADS * LANES
_C_GAK = _C_WAQ + N_HEADS * LANES
_C_GAV = _C_GAK + LANES
_C_WAK = _C_GAV + LANES
_C_WAV = _C_WAK + LANES
_C_END = _C_WAV + LANES


def _expand_q_cols(wq):
    zero = jnp.zeros((wq.shape[0], HEAD_DIM), wq.dtype)
    parts = []
    for h in range(N_HEADS):
        head = wq[:, h * HEAD_DIM:(h + 1) * HEAD_DIM]
        parts += [head, zero] if h // 2 == 0 else [zero, head]
    return jnp.concatenate(parts, axis=1)


def _pack_w_in(w_in):
    cuts = np.cumsum([256, 256, 128, 128, 256, SSD_XBC, 2 * SSD_HEADS, 256, 128, 128])[:-1]
    u, gaq, gak, gav, z, xbc, dt, waq, wak, wav = jnp.split(w_in, [int(c) for c in cuts], axis=1)
    dt = jnp.pad(dt, ((0, 0), (0, LANES - dt.shape[1])))
    w = jnp.concatenate([xbc, u, z, dt, _expand_q_cols(gaq), _expand_q_cols(waq), gak, gav, wak, wav], axis=1)
    return w.astype(BF16)


def _rope(x, cos, sins):
    w = x.shape[1]
    if w > LANES:
        cos = jnp.concatenate([cos] * (w // LANES), axis=1)
        sins = jnp.concatenate([sins] * (w // LANES), axis=1)
    lane = lax.broadcasted_iota(jnp.int32, x.shape, 1)
    up = pltpu.roll(x, w - ROPE_FREQS, 1)
    dn = pltpu.roll(x, ROPE_FREQS, 1)
    partner = jnp.where((lane & ROPE_FREQS) == 0, up, dn)
    return x * cos + partner * sins


def _v_with_ones(v):
    lo = lax.broadcasted_iota(jnp.int32, v.shape, 1) < HEAD_DIM
    return jnp.concatenate([jnp.where(lo, v, 1.0), jnp.where(lo, 1.0, v)], axis=1).astype(BF16)


def _moe_residual(xn_ref, r0_ref, r1_ref, route_ref, mod_ref):
    route = route_ref[...]
    r0 = jnp.concatenate(_unpack_bf16_pair(r0_ref[...]), axis=1)
    r1 = jnp.concatenate(_unpack_bf16_pair(r1_ref[...]), axis=1)
    return xn_ref[...] + mod_ref[0, 5:6, :] * (route[:, 2:3] * r0 + route[:, 3:4] * r1)


ROW_SUB = 4


def _row_views(refs, s):
    return [r.at[pl.ds(s * TM, TM), :] for r in refs]


def _inproj_kernel(*refs, first, nblk):
    n_blk_in = (2 if first else 5) + 3
    shared = refs[ROW_SUB * n_blk_in:]
    g_ref, w_ref, qn_ref, kn_ref = shared[:4]
    xm_o, xbc_o, ug_o = shared[4:7]
    rest_o = shared[7:-1]
    u_scr = shared[-1]
    for s in range(ROW_SUB):
        blk_refs = refs[s * n_blk_in:(s + 1) * n_blk_in]
        xm_v, xbc_v = _row_views((xm_o, xbc_o), s)
        ug_v = ug_o.at[:, pl.ds(s * S5_TB, S5_TB), :]
        _inproj_block(blk_refs, g_ref, w_ref, qn_ref, kn_ref, xm_v, xbc_v, ug_v, _row_views(rest_o, s), u_scr,
                      first, (pl.program_id(0) * ROW_SUB + s) % nblk == 0)


def _inproj_block(blk_refs, g_ref, w_ref, qn_ref, kn_ref, xm_o, xbc_o, ug_o, rest_o, u_scr, first, is_ctx):
    if first:
        lat_ref, ctx_ref = blk_refs[:2]
        x = jnp.where(is_ctx, ctx_ref[...], lat_ref[...])
    else:
        x = _moe_residual(*blk_refs[:5])
    mod_ref, cos_ref, sin_ref = blk_refs[-3:]
    z_o, dt_o, gaq_o, gak_o, gav_o, waq_o, wak_o, wav_o = rest_o
    xm_o[...] = x
    ms = jnp.mean(x * x, axis=-1, keepdims=True)
    xn = x * lax.rsqrt(ms + EPS) * g_ref[...]
    h = xn * (1.0 + mod_ref[0, 1:2, :]) + mod_ref[0, 0:1, :]
    hb = h.astype(BF16)

    def proj(lo, hi):
        return _dot(hb, w_ref[:, lo:hi])

    cos = cos_ref[...]
    sins = sin_ref[...]
    scale = LOG2E * HEAD_DIM ** -0.5
    q = proj(_C_GAQ, _C_WAQ)
    qs = q * q
    inv = jnp.concatenate(
        [jnp.broadcast_to(lax.rsqrt(jnp.sum(qs[:, s * LANES:(s + 1) * LANES], axis=1, keepdims=True)
                                    * (1.0 / HEAD_DIM) + EPS), (q.shape[0], LANES)) for s in range(N_HEADS)], axis=1)
    gaq_o[...] = (_rope(q * inv * qn_ref[...], cos, sins) * scale).astype(BF16)
    waq_o[...] = (_rope(proj(_C_WAQ, _C_GAK), cos, sins) * scale).astype(BF16)
    k = proj(_C_GAK, _C_GAV)
    ks = k * k
    lane = lax.broadcasted_iota(jnp.int32, k.shape, 1)
    lo = lane < HEAD_DIM
    ms0 = jnp.sum(jnp.where(lo, ks, 0.0), axis=1, keepdims=True)
    ms1 = jnp.sum(jnp.where(lo, 0.0, ks), axis=1, keepdims=True)
    kinv = lax.rsqrt(jnp.where(lo, ms0, ms1) * (1.0 / HEAD_DIM) + EPS)
    gak_o[...] = _rope(k * kinv * kn_ref[...], cos, sins).astype(BF16)
    gav_o[...] = _v_with_ones(proj(_C_GAV, _C_WAK))
    wak_o[...] = _rope(proj(_C_WAK, _C_WAV), cos, sins).astype(BF16)
    wav_o[...] = _v_with_ones(proj(_C_WAV, _C_END))
    xbc_o[...] = proj(_C_XBC, _C_U)
    u = proj(_C_U, _C_Z)
    u_scr[0] = u[:, :LANES]
    u_scr[1] = u[:, LANES:]
    _s5_pack(u_scr.at[0], u_scr.at[1], ug_o)
    z_o[...] = proj(_C_Z, _C_DT)
    dt_o[...] = proj(_C_DT, _C_GAQ)


def _mod_row(i, nblk, nb):
    return jnp.where(i % nblk == 0, nb, i // nblk)


def _inproj(src, mod, norm_g, w_packed, cos_t, sin_t, qn_g, kn_g, nb, nblk):
    first = src[0] == "first"
    d = src[1].shape[1]
    t = nb * nblk * TM
    row = lambda i: (i, 0)
    fix = lambda i: (0, 0)
    nsteps = t // (ROW_SUB * TM)
    assert nsteps * ROW_SUB * TM == t

    def blk_specs(s):
        bid = lambda i: ROW_SUB * i + s
        modspec = pl.BlockSpec((1, 6, d), lambda i: (_mod_row(bid(i), nblk, nb), 0, 0))
        table = pl.BlockSpec((TM, LANES), lambda i: (bid(i) % nblk, 0))
        if first:
            srcs = [pl.BlockSpec((TM, d), lambda i: ((bid(i) // nblk) * (nblk - 1) + jnp.maximum(bid(i) % nblk - 1, 0), 0)),
                    pl.BlockSpec((TM, d), lambda i: (bid(i) // nblk, 0))]
        else:
            srcs = [pl.BlockSpec((TM, d), lambda i: (bid(i), 0)), pl.BlockSpec((TM, d // 2), lambda i: (bid(i), 0)),
                    pl.BlockSpec((TM, d // 2), lambda i: (bid(i) + t // TM, 0)),
                    pl.BlockSpec((TM, LANES), lambda i: (bid(i), 0)), modspec]
        return srcs + [modspec, table, table]

    if first:
        blk_args = tuple(src[1:]) + (mod, cos_t, sin_t)
    else:
        blk_args = (src[1], src[2], src[2], src[3], src[4], mod, cos_t, sin_t)
    outs = [(d, F32), (SSD_XBC, F32), None, (GROUP_W, F32), (LANES, F32),
            (N_HEADS * LANES, BF16), (LANES, BF16), (2 * LANES, BF16),
            (N_HEADS * LANES, BF16), (LANES, BF16), (2 * LANES, BF16)]
    shapes = [jax.ShapeDtypeStruct((t, o[0]), o[1]) if o else
              jax.ShapeDtypeStruct((S5_GROUPS, t // S5_Q, S5_BLK), F32) for o in outs]
    specs = [pl.BlockSpec((ROW_SUB * TM, o[0]), row) if o else
             pl.BlockSpec((S5_GROUPS, ROW_SUB * S5_TB, S5_BLK), lambda i: (0, i, 0)) for o in outs]
    return pl.pallas_call(
        functools.partial(_inproj_kernel, first=first, nblk=nblk),
        out_shape=shapes,
        grid=(nsteps,),
        in_specs=[sp for s in range(ROW_SUB) for sp in blk_specs(s)] + [
                  pl.BlockSpec((1, d), fix),
                  pl.BlockSpec((d, _C_END), fix),
                  pl.BlockSpec((1, N_HEADS * LANES), fix),
                  pl.BlockSpec((1, LANES), fix)],
        out_specs=specs,
        scratch_shapes=[pltpu.VMEM((GROUP_W // LANES, TM, LANES), F32)],
        compiler_params=_cp(("parallel",)),
        name="in_proj",
    )(*(blk_args * ROW_SUB), norm_g.reshape(1, d), w_packed,
      jnp.tile(qn_g, 2 * N_HEADS).reshape(1, -1), jnp.tile(kn_g, 2).reshape(1, -1))


def _rope_tables(lc, l):
    n_rows = l // GRID_W
    rows = np.repeat(np.arange(n_rows), GRID_W)
    cols = np.tile(np.arange(GRID_W), n_rows)
    inv = np.power(np.float32(ROPE_BASE), -np.arange(ROPE_FREQS, dtype=np.float32) / ROPE_FREQS)
    ang = np.stack([rows, cols], axis=-1).astype(np.float32)[..., None] * inv
    cos = np.cos(ang)
    sin = np.sin(ang)
    cos64 = np.stack([cos, cos], axis=2).reshape(l, HEAD_DIM)
    sin64 = np.stack([-sin, sin], axis=2).reshape(l, HEAD_DIM)
    cos64 = np.concatenate([np.ones((lc, HEAD_DIM), np.float32), cos64], axis=0)
    sin64 = np.concatenate([np.zeros((lc, HEAD_DIM), np.float32), sin64], axis=0)
    return (jnp.asarray(np.tile(cos64, (1, 2)), dtype=F32), jnp.asarray(np.tile(sin64, (1, 2)), dtype=F32))


def _merge_heads(o2, kvh):
    tq = o2.shape[0] // 2
    oa, ob = o2[:tq], o2[tq:]
    lane = lax.broadcasted_iota(jnp.int32, oa.shape, 1)
    if kvh == 0:
        return jnp.where(lane < HEAD_DIM, oa, pltpu.roll(ob, HEAD_DIM, 1))
    return jnp.where(lane < HEAD_DIM, pltpu.roll(oa, HEAD_DIM, 1), ob)


def _stack_q(q_ref, rows, kvh):
    return jnp.concatenate([q_ref[rows, (2 * kvh) * LANES:(2 * kvh + 1) * LANES],
                            q_ref[rows, (2 * kvh + 1) * LANES:(2 * kvh + 2) * LANES]], axis=0)


def _ga_attend(q_ref, k_ref, v_ref, o_ref, nkeys):
    k = k_ref[0:nkeys, :]
    for sub in range(GA_SUB):
        rows = slice(sub * GA_TQ, (sub + 1) * GA_TQ)
        scores = [_dot_nt(_stack_q(q_ref, rows, kvh), k) for kvh in range(2)]
        outs = []
        for kvh in range(2):
            s = scores[kvh]
            p = jnp.exp2((s - jnp.max(s, axis=1, keepdims=True)).astype(BF16))
            o2 = _dot(p, v_ref[0:nkeys, kvh * LANES:(kvh + 1) * LANES])
            outs.append(_merge_heads(o2 / pltpu.roll(o2, HEAD_DIM, 1), kvh))
        o_ref[rows, :] = jnp.concatenate(outs, axis=1)


def _attn_kernel(sink_ref, gq_ref, gk_ref, gv_ref, wq_ref, wk_ref, wv_ref, og_ref, ow_ref, *, lc):
    is_ctx = pl.program_id(1) < lc // TM

    @pl.when(is_ctx)
    def _():
        _ga_attend(gq_ref, gk_ref, gv_ref, og_ref, lc)
        _wa_attend(sink_ref, wq_ref, wk_ref, wv_ref, ow_ref, lc)

    @pl.when(jnp.logical_not(is_ctx))
    def _():
        _ga_attend(gq_ref, gk_ref, gv_ref, og_ref, gk_ref.shape[0])
        _wa_attend(sink_ref, wq_ref, wk_ref, wv_ref, ow_ref, lc)


def _attn(sink, gq, gk, gv, wq, wk, wv, nb, s_len, lc):
    t = gq.shape[0]
    nq = s_len // TM
    assert GA_SUB * GA_TQ == TM and WA_SUB * TQ == TM
    qspec = pl.BlockSpec((TM, N_HEADS * LANES), lambda b, j: (b * nq + j, 0))
    kspec = pl.BlockSpec((s_len, LANES), lambda b, j: (b, 0))
    vspec = pl.BlockSpec((s_len, 2 * LANES), lambda b, j: (b, 0))
    ospec = pl.BlockSpec((TM, GROUP_W), lambda b, j: (b * nq + j, 0))
    return pl.pallas_call(
        functools.partial(_attn_kernel, lc=lc),
        out_shape=[jax.ShapeDtypeStruct((t, GROUP_W), F32)] * 2,
        grid=(nb, nq),
        in_specs=[pl.BlockSpec(memory_space=pltpu.SMEM), qspec, kspec, vspec, qspec, kspec, vspec],
        out_specs=[ospec, ospec],
        compiler_params=_cp(("parallel", "arbitrary")),
        name="attention",
    )(sink, gq, gk, gv, wq, wk, wv)


WA_SUB = TM // TQ


def _wa_attend(sink_ref, q_ref, k_ref, v_ref, o_ref, lc):
    s_len = k_ref.shape[0]
    kc = k_ref[0:lc, :]
    row = lax.broadcasted_iota(jnp.int32, (2 * TQ, 1), 0)
    for sub in range(WA_SUB):
        rows = slice(sub * TQ, (sub + 1) * TQ)
        n = pl.program_id(1) * WA_SUB + sub - lc // TQ
        start = pl.multiple_of(jnp.clip(lc + (n - 1) * TQ, lc, s_len - 3 * TQ), TQ)
        kb = k_ref[pl.ds(start, 3 * TQ), :]
        qpos = n * TQ + lax.broadcasted_iota(jnp.int32, (TQ, 3 * TQ), 0)
        kpos = (start - lc) + lax.broadcasted_iota(jnp.int32, (TQ, 3 * TQ), 1)
        reach = jnp.where(n >= 0, WINDOW, -1)
        valid = jnp.abs(qpos - kpos) <= reach
        valid = jnp.concatenate([valid, valid], axis=0)
        outs = []
        for kvh in range(2):
            q2 = jnp.concatenate([q_ref[rows, (2 * kvh) * LANES:(2 * kvh + 1) * LANES],
                                  q_ref[rows, (2 * kvh + 1) * LANES:(2 * kvh + 2) * LANES]], axis=0)
            sc = _dot_nt(q2, kc)
            sb = jnp.where(valid, _dot_nt(q2, kb), NEG_INF)
            sink = jnp.where(row < TQ, sink_ref[2 * kvh], sink_ref[2 * kvh + 1]) * LOG2E
            m = jnp.maximum(jnp.maximum(jnp.max(sc, axis=1, keepdims=True), jnp.max(sb, axis=1, keepdims=True)), sink)
            pc = jnp.exp2((sc - m).astype(BF16))
            pb = jnp.exp2((sb - m).astype(BF16))
            vcols = slice(kvh * LANES, (kvh + 1) * LANES)
            o2 = _dot(pc, v_ref[0:lc, vcols]) + _dot(pb, v_ref[pl.ds(start, 3 * TQ), vcols])
            denom = pltpu.roll(o2, HEAD_DIM, 1) + jnp.exp2(sink - m)
            outs.append(_merge_heads(o2 / denom, kvh))
        o_ref[rows, :] = jnp.concatenate(outs, axis=1)


def _s5_chunk_index(t, rev, nc_ctx, nc_tot):
    if not rev:
        return t
    return jnp.where(t < nc_ctx, nc_ctx - 1 - t, nc_tot - 1 - (t - nc_ctx))


def _s5_kernel(u_ref, k_ref, p_ref, g_ref, ar_ref, ai_ref, dsk_ref, y_ref, s_scr, h_scr, m_scr, *, nb, nc_ctx, nc_tot):
    for d in range(2):
        ext = k_ref[d, 0]
        for s in range(S5_Q):
            lo = ((S5_Q - s) if d == 0 else (S5_Q - 1 - s)) * S5_CH
            win = pltpu.roll(ext, (2 * S5_BLK - lo) % (2 * S5_BLK), 1)[:, :S5_BLK]
            m_scr[d, s * S5_CH:(s + 1) * S5_CH, :] = win.astype(BF16)
    uf = u_ref[0]
    u = uf.astype(BF16)
    for d in range(2):
        for k in range(2):
            s_scr[d, k] = _dot(u, p_ref[d, k, 0])
    ar = [jnp.broadcast_to(ar_ref[d, 0], (nb, LANES)) for d in range(2)]
    ai = [[jnp.broadcast_to(ai_ref[d, k, 0], (nb, LANES)) for k in range(2)] for d in range(2)]

    def body(t, carry):
        out = []
        for d in range(2):
            h, hs = carry[d]
            rows = pl.ds(_s5_chunk_index(t, d == 1, nc_ctx, nc_tot), nb, stride=nc_tot)
            h_scr[d, rows, :] = h
            out.append((ar[d] * h + ai[d][0] * hs + s_scr[d, 0, rows, :],
                        ar[d] * hs + ai[d][1] * h + s_scr[d, 1, rows, :]))
        return tuple(out)

    zero = jnp.zeros((nb, LANES), F32)
    lax.fori_loop(0, nc_tot, body, ((zero, zero), (zero, zero)), unroll=2)
    y = uf * dsk_ref[0]
    for d in range(2):
        y = y + _dot(u, m_scr[d]) + _dot(h_scr[d].astype(BF16), g_ref[d, 0])
    y_ref[0] = y


S5_TB = TM // S5_Q
S5_GPS = LANES // S5_CH


def _s5_pack(lo_ref, hi_ref, o_ref):
    for s in range(S5_Q):
        rows = pl.ds(s, S5_TB, stride=S5_Q)
        halves = (lo_ref[rows, :], hi_ref[rows, :])
        dst = S5_CH * (s % S5_GPS)
        for g in range(S5_GROUPS):
            slab = halves[g // S5_GPS]
            src = S5_CH * (g % S5_GPS)
            moved = slab if src == dst else pltpu.roll(slab, (dst - src) % LANES, 1)
            o_ref[g, :, s * S5_CH:(s + 1) * S5_CH] = moved[:, dst:dst + S5_CH]


def _s5_unpack(y_ref, o_ref):
    lane_grp = lax.broadcasted_iota(jnp.int32, (S5_TB, LANES), 1) // S5_CH
    for s in range(S5_Q):
        src = S5_CH * (s % S5_GPS)
        for half in range(S5_GROUPS // S5_GPS):
            acc = None
            for gl in range(S5_GPS):
                slab = y_ref[half * S5_GPS + gl, :, (s // S5_GPS) * LANES:(s // S5_GPS + 1) * LANES]
                dst = S5_CH * gl
                moved = slab if src == dst else pltpu.roll(slab, (dst - src) % LANES, 1)
                acc = moved if acc is None else jnp.where(lane_grp == gl, moved, acc)
            o_ref[half, pl.ds(s, S5_TB, stride=S5_Q), :] = acc


def _s5_params(lam_re, lam_im, log_dt, b_re, b_im, c_re, c_im, d_skip):
    q = S5_Q
    dt = jnp.exp(log_dt)[..., None]
    lr, li = lam_re, lam_im
    mag = jnp.exp(lr * dt)
    a_re = mag * jnp.cos(li * dt)
    a_im = mag * jnp.sin(li * dt)
    den = lr * lr + li * li
    f_re = ((a_re - 1.0) * lr + a_im * li) / den
    f_im = (a_im * lr - (a_re - 1.0) * li) / den
    bb_re = f_re[..., None] * b_re - f_im[..., None] * b_im
    bb_im = f_re[..., None] * b_im + f_im[..., None] * b_re
    kk = jnp.arange(q + 1, dtype=F32)[:, None, None, None]
    pmag = jnp.exp(kk * (lr * dt))
    pw_re = pmag * jnp.cos(kk * (li * dt))
    pw_im = pmag * jnp.sin(kk * (li * dt))
    lw_re = pw_re[:q].transpose(1, 2, 0, 3)[:, :, :, None, :]
    lw_im = pw_im[:q].transpose(1, 2, 0, 3)[:, :, :, None, :]
    ck_re = c_re[:, :, None] * lw_re - c_im[:, :, None] * lw_im
    ck_im = c_re[:, :, None] * lw_im + c_im[:, :, None] * lw_re
    ck = jnp.concatenate([ck_re, -ck_im], axis=-1).reshape(2, S5_GROUPS, S5_BLK, 2 * S5_STATE)
    kern_t = jnp.einsum("dgmp,dgpc->dgcm", ck, jnp.concatenate([bb_re, bb_im], axis=2), precision=HI)
    kern_t = kern_t.reshape(2, S5_GROUPS, S5_CH, q, S5_CH)
    zeros = jnp.zeros_like(kern_t)
    bbt_re = bb_re.transpose(0, 1, 3, 2)[:, :, None]
    bbt_im = bb_im.transpose(0, 1, 3, 2)[:, :, None]
    ct_re = c_re.transpose(0, 1, 3, 2)[:, :, :, None, :]
    ct_im = c_im.transpose(0, 1, 3, 2)[:, :, :, None, :]
    ms, ps, gs = [], [], []
    for d in range(2):
        ext = (jnp.concatenate([zeros[d], kern_t[d]], axis=2) if d == 0
               else jnp.concatenate([kern_t[d, :, :, ::-1], zeros[d]], axis=2))
        ext = ext.reshape(S5_GROUPS, S5_CH, 2 * S5_BLK)
        ms.append(ext)
        pidx = (q - 1 - jnp.arange(q)) if d == 0 else jnp.arange(q)
        pr = pw_re[pidx, d].transpose(1, 0, 2)[:, :, None, :]
        pi = pw_im[pidx, d].transpose(1, 0, 2)[:, :, None, :]
        p_re = pr * bbt_re[d] - pi * bbt_im[d]
        p_im = pr * bbt_im[d] + pi * bbt_re[d]
        pd = jnp.stack([jnp.concatenate([p_re, p_im], axis=3), jnp.concatenate([p_im, p_re], axis=3)])
        ps.append(pd.reshape(2, S5_GROUPS, S5_BLK, 2 * S5_STATE))
        gidx = (jnp.arange(q) + 1) if d == 0 else (q - jnp.arange(q))
        gw_re = pw_re[gidx, d].transpose(1, 2, 0)[..., None]
        gw_im = pw_im[gidx, d].transpose(1, 2, 0)[..., None]
        g_re = ct_re[d] * gw_re - ct_im[d] * gw_im
        g_im = ct_re[d] * gw_im + ct_im[d] * gw_re
        gs.append(jnp.concatenate([g_re, -g_im], axis=1).reshape(S5_GROUPS, 2 * S5_STATE, S5_BLK))
    ar = jnp.concatenate([pw_re[q], pw_re[q]], axis=-1)[:, :, None, :]
    ai = jnp.stack([jnp.concatenate([-pw_im[q], pw_im[q]], axis=-1),
                    jnp.concatenate([pw_im[q], -pw_im[q]], axis=-1)], axis=1)[:, :, :, None, :]
    dsk = jnp.tile(d_skip.reshape(S5_GROUPS, 1, S5_CH), (1, 1, q))
    return (jnp.stack(ms), jnp.stack(ps).astype(BF16), jnp.stack(gs).astype(BF16),
            ar.astype(F32), ai.astype(F32), dsk.astype(F32))


def _s5(ug, params, nb, s_len, lc):
    m, p, g, ar, ai, dsk = params
    nc_tot = s_len // S5_Q
    nc_ctx = lc // S5_Q
    r = nb * nc_tot
    return pl.pallas_call(
        functools.partial(_s5_kernel, nb=nb, nc_ctx=nc_ctx, nc_tot=nc_tot),
        out_shape=jax.ShapeDtypeStruct((S5_GROUPS, r, S5_BLK), F32),
        grid=(S5_GROUPS,),
        in_specs=[pl.BlockSpec((1, r, S5_BLK), lambda gi: (gi, 0, 0)),
                  pl.BlockSpec((2, 1, S5_CH, 2 * S5_BLK), lambda gi: (0, gi, 0, 0)),
                  pl.BlockSpec((2, 2, 1, S5_BLK, 2 * S5_STATE), lambda gi: (0, 0, gi, 0, 0)),
                  pl.BlockSpec((2, 1, 2 * S5_STATE, S5_BLK), lambda gi: (0, gi, 0, 0)),
                  pl.BlockSpec((2, 1, 1, 2 * S5_STATE), lambda gi: (0, gi, 0, 0)),
                  pl.BlockSpec((2, 2, 1, 1, 2 * S5_STATE), lambda gi: (0, 0, gi, 0, 0)),
                  pl.BlockSpec((1, 1, S5_BLK), lambda gi: (gi, 0, 0))],
        out_specs=pl.BlockSpec((1, r, S5_BLK), lambda gi: (gi, 0, 0)),
        scratch_shapes=[pltpu.VMEM((2, 2, r, 2 * S5_STATE), F32), pltpu.VMEM((2, r, 2 * S5_STATE), F32),
                        pltpu.VMEM((2, S5_BLK, S5_BLK), BF16)],
        compiler_params=_cp(("parallel",)),
        name="s5_scan",
    )(ug, m, p, g, ar, ai, dsk)


CONV_ROWS = 4 * TM


def _conv_kernel(x_ref, prev_ref, next_ref, w_ref, b_ref, o_ref, *, s_len, lc):
    x = x_ref[...]
    rows = x.shape[0]
    ridx = lax.broadcasted_iota(jnp.int32, x.shape, 0)
    pos = (pl.program_id(0) * rows) % s_len + ridx
    pos = jnp.where(pos >= s_len, pos - s_len, pos)
    seg_first = jnp.logical_or(pos == 0, pos == lc)
    seg_last = jnp.logical_or(pos == lc - 1, pos == s_len - 1)
    xm = jnp.where(ridx == 0, prev_ref[SUBLANES - 1:SUBLANES, :], pltpu.roll(x, 1, 0))
    xp = jnp.where(ridx == rows - 1, next_ref[0:1, :], pltpu.roll(x, rows - 1, 0))
    xm = jnp.where(seg_first, 0.0, xm)
    xp = jnp.where(seg_last, 0.0, xp)
    y = xm * w_ref[0:1, :] + x * w_ref[1:2, :] + xp * w_ref[2:3, :] + b_ref[...]
    o_ref[...] = _silu(y)


def _conv(xbc, w, b, s_len, lc):
    t, c = xbc.shape
    rows = next(r for r in (CONV_ROWS, CONV_ROWS // 2, TM) if t % r == 0)
    per = rows // SUBLANES
    last = t // SUBLANES - 1
    return pl.pallas_call(
        functools.partial(_conv_kernel, s_len=s_len, lc=lc),
        out_shape=jax.ShapeDtypeStruct((t, c), F32),
        grid=(t // rows,),
        in_specs=[pl.BlockSpec((rows, c), lambda i: (i, 0)),
                  pl.BlockSpec((SUBLANES, c), lambda i: (jnp.maximum(i * per - 1, 0), 0)),
                  pl.BlockSpec((SUBLANES, c), lambda i: (jnp.minimum((i + 1) * per, last), 0)),
                  pl.BlockSpec((3, c), lambda i: (0, 0)),
                  pl.BlockSpec((1, c), lambda i: (0, 0))],
        out_specs=pl.BlockSpec((rows, c), lambda i: (i, 0)),
        compiler_params=_cp(("parallel",)),
        name="ssd_conv",
    )(xbc, xbc, xbc, w, b.reshape(1, c))


_X_B = GROUP_W
_X_C = GROUP_W + SSD_NGROUPS * SSD_STATE


def _ssd_kernel(xf_ref, dtf_ref, dttf_ref, xr_ref, dtr_ref, dttr_ref, bias_ref, a_ref, biast_ref, at_ref, dsk_ref,
                yf_ref, yr_ref, stf_ref, str_ref):
    @pl.when(pl.program_id(1) == 0)
    def _():
        stf_ref[...] = jnp.zeros_like(stf_ref)
        str_ref[...] = jnp.zeros_like(str_ref)

    par = (bias_ref[...], a_ref[...], biast_ref[...], at_ref[...], dsk_ref[...])
    for j in range(SSD_SUB):
        rf = slice(j * TQ, (j + 1) * TQ)
        rr = slice((SSD_SUB - 1 - j) * TQ, (SSD_SUB - j) * TQ)
        for b in range(xf_ref.shape[0]):
            yf_ref[b, rf, :] = _ssd_chunk_step(xf_ref[b, rf, :], dtf_ref[b, rf, :], dttf_ref[b, :, rf], par,
                                               stf_ref.at[b], False)
            yr_ref[b, rr, :] = _ssd_chunk_step(xr_ref[b, rr, :], dtr_ref[b, rr, :], dttr_ref[b, :, rr], par,
                                               str_ref.at[b], True)


def _ssd_chunk_step(xc, dt_raw, dtt_raw, par, st_ref, rev):
    bias, a_vec, biast, at_vec, dsk = par
    base = SSD_HEADS if rev else 0
    x = xc[:, 0:GROUP_W]
    dt = _softplus(dt_raw + bias)
    a = dt * a_vec
    dtt = _softplus(dtt_raw + biast)
    at = dtt * at_vec
    ri = lax.broadcasted_iota(jnp.int32, (TQ, TQ), 0)
    ci = lax.broadcasted_iota(jnp.int32, (TQ, TQ), 1)
    causal = (ci >= ri) if rev else (ri >= ci)
    tri = jnp.where(causal, 1.0, 0.0)
    cum_c = _dot_hi(tri, a)
    cum_r = _dot_nt_hi(at, tri)
    edge = 0 if rev else TQ - 1
    tot = cum_c[edge:edge + 1, :]

    shape = (TQ, GROUP_W)
    xdt = x * _per_head_cols(dt, base, SSD_HEADS, shape)
    lane = lax.broadcasted_iota(jnp.int32, shape, 1)
    y = jnp.zeros(shape, F32)
    bmat = [xc[:, _X_B + g * SSD_STATE:_X_B + (g + 1) * SSD_STATE].astype(BF16) for g in range(SSD_NGROUPS)]
    cmat = [xc[:, _X_C + g * SSD_STATE:_X_C + (g + 1) * SSD_STATE].astype(BF16) for g in range(SSD_NGROUPS)]
    cb = [_dot_nt(cmat[g], bmat[g]) for g in range(SSD_NGROUPS)]
    for h in range(SSD_HEADS):
        col = base + h
        seg = jnp.where(causal, cum_c[:, col:col + 1] - cum_r[col:col + 1, :], NEG_INF)
        scores = cb[h // 2] * jnp.exp(seg)
        xh = jnp.where((lane >= h * HEAD_DIM) & (lane < (h + 1) * HEAD_DIM), xdt, 0.0)
        y = y + _dot(scores.astype(BF16), xh.astype(BF16))
    st = st_ref[...]
    yo = jnp.concatenate(
        [_dot_nt(cmat[g], st[g * SSD_STATE:(g + 1) * SSD_STATE].astype(BF16)) for g in range(SSD_NGROUPS)], axis=1)
    y = y + yo * _per_head_cols(jnp.exp(cum_c), base, SSD_HEADS, shape)
    if not rev:
        y = y + x * dsk
    xd = xdt * _per_head_cols(jnp.exp(tot - cum_c), base, SSD_HEADS, shape)
    xdt_t = xd.T.astype(BF16)
    decay = jnp.exp(tot)
    for g in range(SSD_NGROUPS):
        new = _dot(xdt_t[g * SSD_STATE:(g + 1) * SSD_STATE], bmat[g])
        for hh in range(2):
            h = 2 * g + hh
            r0 = h * HEAD_DIM
            st_ref[r0:r0 + HEAD_DIM, :] = (decay[:, base + h:base + h + 1] * st[r0:r0 + HEAD_DIM]
                                           + new[hh * HEAD_DIM:(hh + 1) * HEAD_DIM])
    return y


def _dot_nt_hi(a, b):
    return lax.dot_general(a, b, (((1,), (1,)), ((), ())), preferred_element_type=F32, precision=HI)


def _ssd_chunk(c, rev, nc_ctx, nc_tot):
    if not rev:
        return c
    return jnp.where(c < nc_ctx, nc_ctx - 1 - c, nc_tot - 1 - (c - nc_ctx))


SSD_SUB = TM // TQ
SSD_NB = 4


def _ssd_scan(xc, dt, dtt, bias, a, biast, at, dsk, nb, s_len, lc):
    t = xc.shape[0]
    nblk = s_len // TM
    nctx = lc // TM
    nbs = math.gcd(nb, SSD_NB)
    fix = lambda b, c: (0, 0)
    xc3 = xc.reshape(nb, s_len, SSD_XBC)
    dt3 = dt.reshape(nb, s_len, LANES)

    def rows(rev):
        return lambda b, c: (b, _ssd_chunk(c, rev, nctx, nblk), 0)

    def lanes(rev):
        return lambda b, c: (b, 0, _ssd_chunk(c, rev, nctx, nblk))

    def data_specs(rev):
        return [pl.BlockSpec((nbs, TM, SSD_XBC), rows(rev)), pl.BlockSpec((nbs, TM, LANES), rows(rev)),
                pl.BlockSpec((nbs, SUBLANES, TM), lanes(rev))]

    state = pltpu.VMEM((nbs, SSD_HEADS * HEAD_DIM, SSD_STATE), F32)
    yf, yr = pl.pallas_call(
        _ssd_kernel,
        out_shape=[jax.ShapeDtypeStruct((nb, s_len, GROUP_W), F32)] * 2,
        grid=(nb // nbs, nblk),
        in_specs=data_specs(False) + data_specs(True) + [
            pl.BlockSpec((1, LANES), fix), pl.BlockSpec((1, LANES), fix),
            pl.BlockSpec((SUBLANES, TQ), fix), pl.BlockSpec((SUBLANES, TQ), fix),
            pl.BlockSpec((1, GROUP_W), fix)],
        out_specs=[pl.BlockSpec((nbs, TM, GROUP_W), rows(False)), pl.BlockSpec((nbs, TM, GROUP_W), rows(True))],
        scratch_shapes=[state, state],
        compiler_params=_cp(("parallel", "arbitrary")),
        name="ssd_scan",
    )(xc3, dt3, dtt, xc3, dt3, dtt, bias, a, biast, at, dsk)
    return yf.reshape(t, GROUP_W), yr.reshape(t, GROUP_W)


def _ssd(xbc, dt, conv_w, conv_b, dt_bias, a_log, d_skip, nb, s_len, lc):
    xc = _conv(xbc, conv_w, conv_b, s_len, lc)
    nd = 2 * SSD_HEADS
    dtt = dt[:, :nd].reshape(nb, s_len, nd).transpose(0, 2, 1)
    bias = jnp.pad(dt_bias.reshape(1, nd), ((0, 0), (0, LANES - nd)))
    a = jnp.pad(-jnp.exp(a_log).reshape(1, nd), ((0, 0), (0, LANES - nd)))
    biast = jnp.broadcast_to(dt_bias.reshape(nd, 1), (nd, TQ))
    at = jnp.broadcast_to(-jnp.exp(a_log).reshape(nd, 1), (nd, TQ))
    dsk = jnp.repeat(d_skip, HEAD_DIM).reshape(1, GROUP_W)
    return _ssd_scan(xc, dt, dtt, bias, a, biast, at, dsk, nb, s_len, lc)


def _outproj_kernel(x_ref, ys5_ref, oga_ref, y0_ref, y1_ref, z_ref, owa_ref, *refs):
    mods, shared = refs[:ROW_SUB], refs[ROW_SUB:]
    for s in range(ROW_SUB):
        rows = _row_views((x_ref, oga_ref, y0_ref, y1_ref, z_ref, owa_ref) + tuple(shared[-4:-1]), s)
        _outproj_block(rows[0], ys5_ref.at[:, pl.ds(s * S5_TB, S5_TB), :], *rows[1:6], mods[s], *shared[:-4],
                       *rows[6:], shared[-1])


def _outproj_block(x_ref, ys5_ref, oga_ref, y0_ref, y1_ref, z_ref, owa_ref, mod_ref, gluw_ref, glub_ref,
                   ng_ref, wout_ref, n2_ref, wr_ref, br_ref, xn_o, h2_o, route_o, y_scr):
    _s5_unpack(ys5_ref, y_scr)
    gl = _gelu_tanh(jnp.concatenate([y_scr[0], y_scr[1]], axis=1))
    a = gl * _sigmoid(_dot(gl.astype(BF16), gluw_ref[...]) + glub_ref[...])
    m = (y0_ref[...] + y1_ref[...]) * _silu(z_ref[...])
    m = m * lax.rsqrt(jnp.mean(m * m, axis=-1, keepdims=True) + EPS) * ng_ref[...]
    w = wout_ref
    mix = (_dot(a.astype(BF16), w[0:GROUP_W, :]) + _dot(oga_ref[...].astype(BF16), w[GROUP_W:2 * GROUP_W, :])
           + _dot(m.astype(BF16), w[2 * GROUP_W:3 * GROUP_W, :]) + _dot(owa_ref[...].astype(BF16), w[3 * GROUP_W:, :]))
    xn = x_ref[...] + mod_ref[0, 2:3, :] * mix
    xn_o[...] = xn
    h2 = xn * lax.rsqrt(jnp.mean(xn * xn, axis=-1, keepdims=True) + EPS) * n2_ref[...]
    h2 = h2 * (1.0 + mod_ref[0, 4:5, :]) + mod_ref[0, 3:4, :]
    h2_o[...] = _pack_bf16_pair(h2)
    h_hi = h2.astype(BF16)
    h_lo = (h2 - h_hi.astype(F32)).astype(BF16)
    logits = _dot(h_hi, wr_ref[0]) + (_dot(h_lo, wr_ref[0]) + _dot(h_hi, wr_ref[1])) + br_ref[...]
    lane = lax.broadcasted_iota(jnp.int32, logits.shape, 1).astype(F32)
    big = float(4 * LANES)
    lcoarse = jnp.where(lane < MOE_GROUPS, logits, NEG_INF)
    mx = jnp.max(lcoarse, axis=1, keepdims=True)
    den = jnp.sum(jnp.exp(lcoarse - mx), axis=1, keepdims=True)
    grp = jnp.min(jnp.where(lcoarse == mx, lane, big), axis=1, keepdims=True)
    pg = 1.0 / den
    lo = ROUTE_FINE0 + grp * MOE_PER_GROUP
    lf = jnp.where(lane >= lo, jnp.where(lane < lo + MOE_PER_GROUP, logits, NEG_INF), NEG_INF)
    v1 = jnp.max(lf, axis=1, keepdims=True)
    i1 = jnp.min(jnp.where(lf == v1, lane, big), axis=1, keepdims=True)
    lf2 = jnp.where(lane == i1, NEG_INF, lf)
    v2 = jnp.max(lf2, axis=1, keepdims=True)
    i2 = jnp.min(jnp.where(lf2 == v2, lane, big), axis=1, keepdims=True)
    e2 = jnp.exp(v2 - v1)
    w1 = pg / (1.0 + e2)
    w2 = w1 * e2
    route = jnp.where(lane == 0, i1 - ROUTE_FINE0,
                      jnp.where(lane == 1, i2 - ROUTE_FINE0,
                                jnp.where(lane == 2, w1, jnp.where(lane == 3, w2, 0.0))))
    route_o[...] = route


def _outproj(x, ys5, oga, y0, y1, z, owa, mod, glu_w, glu_b, ssd_norm_g, w_out, norm2_g, wr, br, nb, nblk):
    t, d = x.shape
    row = lambda i: (i, 0)
    fix = lambda i: (0, 0)
    step = ROW_SUB * TM
    assert t % step == 0
    gw = pl.BlockSpec((step, GROUP_W), row)
    wr_hi = wr.astype(BF16)
    mod_specs = [pl.BlockSpec((1, 6, d), lambda i, s=s: (_mod_row(ROW_SUB * i + s, nblk, nb), 0, 0))
                 for s in range(ROW_SUB)]
    return pl.pallas_call(
        _outproj_kernel,
        out_shape=[jax.ShapeDtypeStruct((t, d), F32), jax.ShapeDtypeStruct((t, d // 2), jnp.uint32),
                   jax.ShapeDtypeStruct((t, LANES), F32)],
        grid=(t // step,),
        in_specs=[pl.BlockSpec((step, d), row),
                  pl.BlockSpec((S5_GROUPS, ROW_SUB * S5_TB, S5_BLK), lambda i: (0, i, 0)),
                  gw, gw, gw, gw, gw] + mod_specs + [
                  pl.BlockSpec((GROUP_W, GROUP_W), fix),
                  pl.BlockSpec((1, GROUP_W), fix),
                  pl.BlockSpec((1, GROUP_W), fix),
                  pl.BlockSpec((d, d), fix),
                  pl.BlockSpec((1, d), fix),
                  pl.BlockSpec((2, d, LANES), lambda i: (0, 0, 0)),
                  pl.BlockSpec((1, LANES), fix)],
        out_specs=[pl.BlockSpec((step, d), row), pl.BlockSpec((step, d // 2), row), pl.BlockSpec((step, LANES), row)],
        scratch_shapes=[pltpu.VMEM((GROUP_W // LANES, TM, LANES), F32)],
        compiler_params=_cp(("parallel",)),
        name="out_proj_router",
    )(x, ys5, oga, y0, y1, z, owa, *([mod] * ROW_SUB), glu_w.astype(BF16), glu_b.reshape(1, -1), ssd_norm_g.reshape(1, -1),
      w_out.astype(BF16), norm2_g.reshape(1, -1), jnp.stack([wr_hi, (wr - wr_hi.astype(F32)).astype(BF16)]), br)


def _pack_router(coarse_w, coarse_b, fine_w, fine_b):
    def lanes(coarse, fine):
        gap = jnp.zeros(coarse.shape[:-1] + (ROUTE_FINE0 - MOE_GROUPS,), F32)
        tail = jnp.zeros(coarse.shape[:-1] + (LANES - ROUTE_FINE0 - N_EXPERTS,), F32)
        return jnp.concatenate([coarse, gap, fine, tail], axis=-1)

    return lanes(coarse_w, fine_w), lanes(coarse_b[None, :], fine_b[None, :])


def _gather_rows(src, idx):
    m = idx.shape[0]
    d = src.shape[1]
    workers = SC_CORES * SC_SUBCORES
    k = SC_FETCH_K
    nch = m // (workers * k)
    assert nch * workers * k == m
    mesh = plsc.VectorSubcoreMesh(core_axis_name="c", subcore_axis_name="s")

    @functools.partial(
        pl.kernel, mesh=mesh,
        out_type=jax.ShapeDtypeStruct((m, d), src.dtype),
        scratch_types=[pltpu.VMEM((nch, k), jnp.int32),
                       pltpu.VMEM((k, d), src.dtype),
                       pltpu.SemaphoreType.DMA],
    )
    def gather(src_hbm, idx_hbm, out_hbm, idx_v, rows_v, sem):
        wid = lax.axis_index("s") * SC_CORES + lax.axis_index("c")
        pltpu.sync_copy(idx_hbm.at[wid], idx_v)

        @pl.loop(0, nch)
        def _(j):
            off = pl.multiple_of((wid * nch + j) * k, k)
            pltpu.async_copy(src_hbm.at[idx_v.at[j]], rows_v, sem).wait()
            pltpu.sync_copy(rows_v, out_hbm.at[pl.ds(off, k)])

    return gather(src, idx.reshape(workers, nch, k))


def _scatter_rows(src, dst0, dst1, nrows):
    t, d = src.shape
    workers = SC_CORES * SC_SUBCORES
    nch = t // (workers * SC_GATHER_K)
    assert nch * workers * SC_GATHER_K == t
    mesh = plsc.VectorSubcoreMesh(core_axis_name="c", subcore_axis_name="s")

    @functools.partial(
        pl.kernel, mesh=mesh,
        out_type=jax.ShapeDtypeStruct((nrows, d), src.dtype),
        scratch_types=[pltpu.VMEM((nch, SC_GATHER_K), jnp.int32),
                       pltpu.VMEM((nch, SC_GATHER_K), jnp.int32),
                       pltpu.VMEM((SC_GATHER_K, d), src.dtype),
                       pltpu.SemaphoreType.DMA((2,))],
    )
    def scatter(src_hbm, d0_hbm, d1_hbm, out_hbm, i0_v, i1_v, rows_v, sem):
        wid = lax.axis_index("s") * SC_CORES + lax.axis_index("c")
        pltpu.sync_copy(d0_hbm.at[wid], i0_v)
        pltpu.sync_copy(d1_hbm.at[wid], i1_v)

        @pl.loop(0, nch)
        def _(j):
            off = pl.multiple_of((wid * nch + j) * SC_GATHER_K, SC_GATHER_K)
            pltpu.sync_copy(src_hbm.at[pl.ds(off, SC_GATHER_K)], rows_v)
            first = pltpu.async_copy(rows_v, out_hbm.at[i0_v.at[j]], sem.at[0])
            second = pltpu.async_copy(rows_v, out_hbm.at[i1_v.at[j]], sem.at[1])
            first.wait()
            second.wait()

    return scatter(src, dst0.reshape(workers, nch, SC_GATHER_K), dst1.reshape(workers, nch, SC_GATHER_K))


def _expert_kernel(be_ref, nused_ref, nvalid_ref, nxt_ref, slot_ref, x_ref, wg_hbm, wu_hbm, wd_hbm, o_ref,
                   wg_f, wu_f, wd_f, wg_s, wu_s, wd_s, sem, *, layer):
    i = pl.program_id(0)
    used = i < nused_ref[0]
    new_expert = jnp.logical_or(i == 0, be_ref[i] != be_ref[jnp.maximum(i - 1, 0)])

    def weight_copies(expert, slot):
        return [pltpu.make_async_copy(w.at[layer, expert], f.at[slot], sem.at[slot, j])
                for j, (w, f) in enumerate(((wg_hbm, wg_f), (wu_hbm, wu_f), (wd_hbm, wd_f)))]

    @pl.when(jnp.logical_and(used, new_expert))
    def _():
        slot = slot_ref[i]

        @pl.when(i == 0)
        def _():
            for c in weight_copies(be_ref[i], slot):
                c.start()

        for c in weight_copies(be_ref[i], slot):
            c.wait()
        wg_s[...] = wg_f[slot].astype(BF16)
        wu_s[...] = wu_f[slot].astype(BF16)
        wd_s[...] = wd_f[slot].astype(BF16)

        @pl.when(nxt_ref[i] >= 0)
        def _():
            for c in weight_copies(nxt_ref[i], 1 - slot):
                c.start()

    def swiglu(rows):
        row = rows.start + lax.broadcasted_iota(jnp.int32, (rows.stop - rows.start, x_ref.shape[1]), 0)
        lo, hi = _unpack_bf16_pair(jnp.where(row < nvalid_ref[i], x_ref[rows, :], jnp.uint32(0)))
        lo = lo.astype(BF16)
        hi = hi.astype(BF16)
        half = lo.shape[1]
        gate = _dot(lo, wg_s[0:half, :]) + _dot(hi, wg_s[half:, :])
        up = _dot(lo, wu_s[0:half, :]) + _dot(hi, wu_s[half:, :])
        o_ref[rows, :] = _pack_bf16_pair(_dot((_silu(gate) * up).astype(BF16), wd_s[...]))

    used = i < nused_ref[0]
    half_rows = MOE_TM // 2

    @pl.when(jnp.logical_and(used, nvalid_ref[i] > half_rows))
    def _():
        swiglu(slice(0, MOE_TM))

    @pl.when(jnp.logical_and(used, nvalid_ref[i] <= half_rows))
    def _():
        swiglu(slice(0, half_rows))
        o_ref[half_rows:, :] = jnp.zeros((MOE_TM - half_rows, o_ref.shape[1]), o_ref.dtype)


def _experts(xs, blk_e, n_used, n_valid, nxt_e, slot, wg, wu, wd, layer):
    rows, dp = xs.shape
    d = 2 * dp
    nblocks = rows // MOE_TM
    de = wg.shape[3]
    blk = lambda i, be, nu, *_: (jnp.minimum(i, nu[0] - 1), 0)
    hbm = pl.BlockSpec(memory_space=pl.ANY)
    grid_spec = pltpu.PrefetchScalarGridSpec(
        num_scalar_prefetch=5,
        grid=(nblocks,),
        in_specs=[pl.BlockSpec((MOE_TM, dp), blk), hbm, hbm, hbm],
        out_specs=pl.BlockSpec((MOE_TM, dp), blk),
        scratch_shapes=[pltpu.VMEM((2, d, de), F32), pltpu.VMEM((2, d, de), F32), pltpu.VMEM((2, de, d), F32),
                        pltpu.VMEM((d, de), BF16), pltpu.VMEM((d, de), BF16), pltpu.VMEM((de, d), BF16),
                        pltpu.SemaphoreType.DMA((2, 3))],
    )
    return pl.pallas_call(
        functools.partial(_expert_kernel, layer=layer),
        out_shape=jax.ShapeDtypeStruct((rows, dp), jnp.uint32),
        grid_spec=grid_spec,
        compiler_params=_cp(("arbitrary",)),
        name="moe_experts",
    )(blk_e, n_used, n_valid, nxt_e, slot, xs, wg, wu, wd)


def _final_kernel(*refs):
    fg_ref, o_ref = refs[-2:]
    for s in range(ROW_SUB):
        y = _moe_residual(*refs[5 * s:5 * s + 5])
        o_ref[s * TM:(s + 1) * TM, :] = y * lax.rsqrt(jnp.mean(y * y, axis=-1, keepdims=True) + EPS) * fg_ref[...]


def _final(xn, rows2, route, mod, final_g, nb, nblk):
    t, d = xn.shape
    nlat = nblk - 1
    assert (nb * nlat) % ROW_SUB == 0

    def blk_specs(s):
        lat = lambda i: ROW_SUB * i + s
        src = lambda i: ((lat(i) // nlat) * nblk + 1 + lat(i) % nlat, 0)
        return [pl.BlockSpec((TM, d), src),
                pl.BlockSpec((TM, d // 2), src),
                pl.BlockSpec((TM, d // 2), lambda i: (src(i)[0] + t // TM, 0)),
                pl.BlockSpec((TM, LANES), src),
                pl.BlockSpec((1, 6, d), lambda i: (lat(i) // nlat, 0, 0))]

    return pl.pallas_call(
        _final_kernel,
        out_shape=jax.ShapeDtypeStruct((nb * nlat * TM, d), F32),
        grid=(nb * nlat // ROW_SUB,),
        in_specs=[sp for s in range(ROW_SUB) for sp in blk_specs(s)] + [pl.BlockSpec((1, d), lambda i: (0, 0))],
        out_specs=pl.BlockSpec((ROW_SUB * TM, d), lambda i: (i, 0)),
        compiler_params=_cp(("parallel",)),
        name="moe_combine_final",
    )(*((xn, rows2, rows2, route, mod) * ROW_SUB), final_g.reshape(1, d))


def _moe(h2, route, wg, wu, wd, layer):
    t, d = h2.shape
    n_slots = 2 * t
    experts = jnp.arange(N_EXPERTS, dtype=F32)[None, :]
    oh0 = (route[:, 0:1] == experts).astype(F32)
    oh1 = (route[:, 1:2] == experts).astype(F32)
    both = (oh0 + oh1).reshape(t // LANES, LANES, N_EXPERTS)
    tri = jnp.tril(jnp.ones((LANES, LANES), F32))
    intra = jnp.einsum("ij,bjk->bik", tri, both)
    blk_tot = intra[:, -1, :]
    blk_cum = jnp.cumsum(blk_tot, axis=0)
    earlier = (intra - both + (blk_cum - blk_tot)[:, None, :]).reshape(t, N_EXPERTS)
    counts = blk_cum[-1].astype(jnp.int32)
    pcounts = (counts + MOE_TM - 1) // MOE_TM * MOE_TM
    pends = jnp.cumsum(pcounts)
    pstarts = pends - pcounts
    base = pstarts.astype(F32)[None, :] + earlier
    dest0 = jnp.sum(oh0 * base, axis=1).astype(jnp.int32)
    dest1 = jnp.sum(oh1 * base, axis=1).astype(jnp.int32)
    nblocks = -(-n_slots // MOE_TM) + N_EXPERTS
    nrows = nblocks * MOE_TM
    blk_start = jnp.arange(nblocks, dtype=jnp.int32) * MOE_TM
    blk_e = jnp.minimum(jnp.sum((pends[None, :] <= blk_start[:, None]).astype(jnp.int32), axis=1), N_EXPERTS - 1)
    n_used = (pends[-1] // MOE_TM).astype(jnp.int32).reshape(1)
    n_valid = jnp.clip((pstarts + counts)[blk_e] - blk_start, 0, MOE_TM).astype(jnp.int32)
    ids = jnp.arange(N_EXPERTS, dtype=jnp.int32)
    has = counts > 0
    later = lax.cummin(jnp.where(has, ids, N_EXPERTS)[::-1])[::-1]
    nxt = jnp.concatenate([later[1:], jnp.full((1,), N_EXPERTS, jnp.int32)])
    nxt = jnp.where(nxt >= N_EXPERTS, -1, nxt)
    slot = (jnp.cumsum(has.astype(jnp.int32)) - 1) % 2
    xs = _scatter_rows(h2, dest0, dest1, nrows)
    ys = _experts(xs, blk_e, n_used, n_valid, nxt[blk_e], slot[blk_e], wg, wu, wd, layer)
    return _gather_rows(ys, jnp.concatenate([dest0, dest1]))


def kernel(x, c, ctx, c_ctx, ada_w, ada_b, norm1_g, norm2_g, w_in, w_out, s5_lam_re, s5_lam_im, s5_log_dt, s5_b_re, s5_b_im, s5_c_re, s5_c_im, s5_d, s5_glu_w, s5_glu_b, ga_qn_g, ga_kn_g, ssd_conv_w, ssd_conv_b, ssd_dt_bias, ssd_a_log, ssd_d, ssd_norm_g, wa_sink, moe_coarse_w, moe_coarse_b, moe_fine_w, moe_fine_b, moe_w_gate, moe_w_up, moe_w_down, final_g):
    nb, l, d = x.shape
    lc = ctx.shape[1]
    depth = ada_w.shape[0]
    assert lc == TM and l % TM == 0 and nb <= SUBLANES - 1 and d == D_MODEL
    s_len = lc + l
    nblk = s_len // TM
    t = nb * s_len

    cc = jnp.zeros((SUBLANES, d), F32).at[:nb].set(c).at[nb].set(c_ctx)
    mods = _ada(cc, ada_w, ada_b).reshape(depth, SUBLANES, 6, d)
    cos_t, sin_t = _rope_tables(lc, l)
    w_packed = jax.vmap(_pack_w_in)(w_in)
    s5_tabs = jax.vmap(_s5_params)(s5_lam_re, s5_lam_im, s5_log_dt, s5_b_re, s5_b_im, s5_c_re, s5_c_im, s5_d)
    wrs, brs = jax.vmap(_pack_router)(moe_coarse_w, moe_coarse_b, moe_fine_w, moe_fine_b)

    src = ("first", x.reshape(nb * l, d), ctx.reshape(nb * lc, d))
    for i in range(depth):
        mod = mods[i]
        (xm, xbc, ug, z, dt, gaq, gak, gav, waq, wak, wav) = _inproj(
            src, mod, norm1_g[i], w_packed[i], cos_t, sin_t, ga_qn_g[i], ga_kn_g[i], nb, nblk)
        ys5 = _s5(ug, tuple(tab[i] for tab in s5_tabs), nb, s_len, lc)
        oga, owa = _attn(wa_sink[i], gaq, gak, gav, waq, wak, wav, nb, s_len, lc)
        y0, y1 = _ssd(xbc, dt, ssd_conv_w[i], ssd_conv_b[i], ssd_dt_bias[i], ssd_a_log[i], ssd_d[i], nb, s_len, lc)
        wr, br = wrs[i], brs[i]
        xn, h2, route = _outproj(xm, ys5, oga, y0, y1, z, owa, mod, s5_glu_w[i], s5_glu_b[i], ssd_norm_g[i],
                                 w_out[i], norm2_g[i], wr, br, nb, nblk)
        rows2 = _moe(h2, route, moe_w_gate, moe_w_up, moe_w_down, i)
        src = ("moe", xn, rows2, route, mod)
    return _final(xn, rows2, route, mod, final_g, nb, nblk).reshape(nb, l, d)
```

```python
import functools
import math

import jax
import jax.numpy as jnp
import numpy as np
from jax import lax
from jax.experimental import pallas as pl
from jax.experimental.pallas import tpu as pltpu
from jax.experimental.pallas import tpu_sc as plsc

F32 = jnp.float32
BF16 = jnp.bfloat16
HI = lax.Precision.HIGHEST

D_MODEL = 1024
GRID_W = 64
GROUP_W = 256
HEAD_DIM = 64
ROPE_FREQS = HEAD_DIM // 4
ROPE_BASE = 10000.0
EPS = 1e-6
S5_CH = 16
S5_GROUPS = GROUP_W // S5_CH
S5_STATE = 64
N_HEADS = 4
SSD_HEADS = 4
SSD_NGROUPS = 2
SSD_STATE = 128
SSD_XBC = GROUP_W + 2 * SSD_NGROUPS * SSD_STATE
WINDOW = 128
MOE_GROUPS = 4
MOE_PER_GROUP = 8
N_EXPERTS = 32

LANES = 128
SUBLANES = 8
TM = 256
TQ = 128
GA_TQ = 128
GA_SUB = 2
S5_Q = 32
S5_BLK = S5_Q * S5_CH
MOE_TM = 512
SC_CORES = 2
SC_SUBCORES = 16
SC_GATHER_K = 32
SC_FETCH_K = 64
ROUTE_FINE0 = 32
VMEM_LIMIT = 56 * 1024 * 1024

NEG_INF = float("-inf")
LOG2E = math.log2(math.e)


def _cp(sem, vmem=VMEM_LIMIT):
    return pltpu.CompilerParams(dimension_semantics=sem, vmem_limit_bytes=vmem)


def _dot(a, b):
    return jnp.dot(a, b, preferred_element_type=F32)


def _dot_hi(a, b):
    return jnp.dot(a, b, preferred_element_type=F32, precision=HI)


def _dot_nt(a, b):
    return lax.dot_general(a, b, (((1,), (1,)), ((), ())), preferred_element_type=F32)


def _sigmoid(x):
    return 1.0 / (1.0 + jnp.exp(-x))


def _silu(x):
    return x * _sigmoid(x)


def _gelu_tanh(x):
    return 0.5 * x * (1.0 + jnp.tanh(math.sqrt(2.0 / math.pi) * (x + 0.044715 * (x * x * x))))


def _softplus(x):
    return jnp.maximum(x, 0.0) + jnp.log(1.0 + jnp.exp(-jnp.abs(x)))


_HI16 = 0xFFFF0000


def _pack_bf16_pair(x):
    n = x.shape[1] // 2
    bits = pltpu.bitcast(x.astype(BF16).astype(F32), jnp.uint32)
    return (bits[:, n:] & jnp.uint32(_HI16)) | (bits[:, :n] >> 16)


def _unpack_bf16_pair(w):
    return pltpu.bitcast(w << 16, F32), pltpu.bitcast(w & jnp.uint32(_HI16), F32)


def _per_head_cols(v, base, n_heads, shape):
    lane = lax.broadcasted_iota(jnp.int32, shape, 1)
    out = jnp.broadcast_to(v[:, base + n_heads - 1:base + n_heads], shape)
    for h in range(n_heads - 2, -1, -1):
        out = jnp.where(lane < (h + 1) * HEAD_DIM, v[:, base + h:base + h + 1], out)
    return out


def _ada_kernel(c_ref, w_ref, b_ref, o_ref):
    c = c_ref[...]
    o_ref[0] = _dot_hi(_silu(c), w_ref[0]) + b_ref[0]


def _ada(cc, ada_w, ada_b):
    depth, d, n = ada_w.shape
    tn = 1536
    return pl.pallas_call(
        _ada_kernel,
        out_shape=jax.ShapeDtypeStruct((depth, SUBLANES, n), F32),
        grid=(depth, n // tn),
        in_specs=[pl.BlockSpec((SUBLANES, d), lambda l, j: (0, 0)),
                  pl.BlockSpec((1, d, tn), lambda l, j: (l, 0, j)),
                  pl.BlockSpec((1, 1, tn), lambda l, j: (l, 0, j))],
        out_specs=pl.BlockSpec((1, SUBLANES, tn), lambda l, j: (l, 0, j)),
        compiler_params=_cp(("parallel", "parallel")),
        name="ada_mod",
    )(cc, ada_w, ada_b.reshape(depth, 1, n))


_C_XBC = 0
_C_U = _C_XBC + SSD_XBC
_C_Z = _C_U + GROUP_W
_C_DT = _C_Z + GROUP_W
_C_GAQ = _C_DT + LANES
_C_WAQ = _C_GAQ + N_HEADS * LANES
_C_GAK = _C_WAQ + N_HEADS * LANES
_C_GAV = _C_GAK + LANES
_C_WAK = _C_GAV + LANES
_C_WAV = _C_WAK + LANES
_C_END = _C_WAV + LANES


def _expand_q_cols(wq):
    zero = jnp.zeros((wq.shape[0], HEAD_DIM), wq.dtype)
    parts = []
    for h in range(N_HEADS):
        head = wq[:, h * HEAD_DIM:(h + 1) * HEAD_DIM]
        parts += [head, zero] if h // 2 == 0 else [zero, head]
    return jnp.concatenate(parts, axis=1)


def _pack_w_in(w_in):
    cuts = np.cumsum([256, 256, 128, 128, 256, SSD_XBC, 2 * SSD_HEADS, 256, 128, 128])[:-1]
    u, gaq, gak, gav, z, xbc, dt, waq, wak, wav = jnp.split(w_in, [int(c) for c in cuts], axis=1)
    dt = jnp.pad(dt, ((0, 0), (0, LANES - dt.shape[1])))
    w = jnp.concatenate([xbc, u, z, dt, _expand_q_cols(gaq), _expand_q_cols(waq), gak, gav, wak, wav], axis=1)
    return w.astype(BF16)


def _rope(x, cos, sins):
    w = x.shape[1]
    if w > LANES:
        cos = jnp.concatenate([cos] * (w // LANES), axis=1)
        sins = jnp.concatenate([sins] * (w // LANES), axis=1)
    lane = lax.broadcasted_iota(jnp.int32, x.shape, 1)
    up = pltpu.roll(x, w - ROPE_FREQS, 1)
    dn = pltpu.roll(x, ROPE_FREQS, 1)
    partner = jnp.where((lane & ROPE_FREQS) == 0, up, dn)
    return x * cos + partner * sins


def _v_with_ones(v):
    lo = lax.broadcasted_iota(jnp.int32, v.shape, 1) < HEAD_DIM
    return jnp.concatenate([jnp.where(lo, v, 1.0), jnp.where(lo, 1.0, v)], axis=1).astype(BF16)


def _moe_residual(xn_ref, r0_ref, r1_ref, route_ref, mod_ref):
    route = route_ref[...]
    r0 = jnp.concatenate(_unpack_bf16_pair(r0_ref[...]), axis=1)
    r1 = jnp.concatenate(_unpack_bf16_pair(r1_ref[...]), axis=1)
    return xn_ref[...] + mod_ref[0, 5:6, :] * (route[:, 2:3] * r0 + route[:, 3:4] * r1)


ROW_SUB = 4


def _row_views(refs, s):
    return [r.at[pl.ds(s * TM, TM), :] for r in refs]


def _inproj_kernel(*refs, first, nblk):
    n_blk_in = (2 if first else 5) + 3
    shared = refs[ROW_SUB * n_blk_in:]
    g_ref, w_ref, qn_ref, kn_ref = shared[:4]
    xm_o, xbc_o, ug_o = shared[4:7]
    rest_o = shared[7:-1]
    u_scr = shared[-1]
    for s in range(ROW_SUB):
        blk_refs = refs[s * n_blk_in:(s + 1) * n_blk_in]
        xm_v, xbc_v = _row_views((xm_o, xbc_o), s)
        ug_v = ug_o.at[:, pl.ds(s * S5_TB, S5_TB), :]
        _inproj_block(blk_refs, g_ref, w_ref, qn_ref, kn_ref, xm_v, xbc_v, ug_v, _row_views(rest_o, s), u_scr,
                      first, (pl.program_id(0) * ROW_SUB + s) % nblk == 0)


def _inproj_block(blk_refs, g_ref, w_ref, qn_ref, kn_ref, xm_o, xbc_o, ug_o, rest_o, u_scr, first, is_ctx):
    if first:
        lat_ref, ctx_ref = blk_refs[:2]
        x = jnp.where(is_ctx, ctx_ref[...], lat_ref[...])
    else:
        x = _moe_residual(*blk_refs[:5])
    mod_ref, cos_ref, sin_ref = blk_refs[-3:]
    z_o, dt_o, gaq_o, gak_o, gav_o, waq_o, wak_o, wav_o = rest_o
    xm_o[...] = x
    ms = jnp.mean(x * x, axis=-1, keepdims=True)
    xn = x * lax.rsqrt(ms + EPS) * g_ref[...]
    h = xn * (1.0 + mod_ref[0, 1:2, :]) + mod_ref[0, 0:1, :]
    hb = h.astype(BF16)

    def proj(lo, hi):
        return _dot(hb, w_ref[:, lo:hi])

    cos = cos_ref[...]
    sins = sin_ref[...]
    scale = LOG2E * HEAD_DIM ** -0.5
    q = proj(_C_GAQ, _C_WAQ)
    qs = q * q
    inv = jnp.concatenate(
        [jnp.broadcast_to(lax.rsqrt(jnp.sum(qs[:, s * LANES:(s + 1) * LANES], axis=1, keepdims=True)
                                    * (1.0 / HEAD_DIM) + EPS), (q.shape[0], LANES)) for s in range(N_HEADS)], axis=1)
    gaq_o[...] = (_rope(q * inv * qn_ref[...], cos, sins) * scale).astype(BF16)
    waq_o[...] = (_rope(proj(_C_WAQ, _C_GAK), cos, sins) * scale).astype(BF16)
    k = proj(_C_GAK, _C_GAV)
    ks = k * k
    lane = lax.broadcasted_iota(jnp.int32, k.shape, 1)
    lo = lane < HEAD_DIM
    ms0 = jnp.sum(jnp.where(lo, ks, 0.0), axis=1, keepdims=True)
    ms1 = jnp.sum(jnp.where(lo, 0.0, ks), axis=1, keepdims=True)
    kinv = lax.rsqrt(jnp.where(lo, ms0, ms1) * (1.0 / HEAD_DIM) + EPS)
    gak_o[...] = _rope(k * kinv * kn_ref[...], cos, sins).astype(BF16)
    gav_o[...] = _v_with_ones(proj(_C_GAV, _C_WAK))
    wak_o[...] = _rope(proj(_C_WAK, _C_WAV), cos, sins).astype(BF16)
    wav_o[...] = _v_with_ones(proj(_C_WAV, _C_END))
    xbc_o[...] = proj(_C_XBC, _C_U)
    u = proj(_C_U, _C_Z)
    u_scr[0] = u[:, :LANES]
    u_scr[1] = u[:, LANES:]
    _s5_pack(u_scr.at[0], u_scr.at[1], ug_o)
    z_o[...] = proj(_C_Z, _C_DT)
    dt_o[...] = proj(_C_DT, _C_GAQ)


def _mod_row(i, nblk, nb):
    return jnp.where(i % nblk == 0, nb, i // nblk)


def _inproj(src, mod, norm_g, w_packed, cos_t, sin_t, qn_g, kn_g, nb, nblk):
    first = src[0] == "first"
    d = src[1].shape[1]
    t = nb * nblk * TM
    row = lambda i: (i, 0)
    fix = lambda i: (0, 0)
    nsteps = t // (ROW_SUB * TM)
    assert nsteps * ROW_SUB * TM == t

    def blk_specs(s):
        bid = lambda i: ROW_SUB * i + s
        modspec = pl.BlockSpec((1, 6, d), lambda i: (_mod_row(bid(i), nblk, nb), 0, 0))
        table = pl.BlockSpec((TM, LANES), lambda i: (bid(i) % nblk, 0))
        if first:
            srcs = [pl.BlockSpec((TM, d), lambda i: ((bid(i) // nblk) * (nblk - 1) + jnp.maximum(bid(i) % nblk - 1, 0), 0)),
                    pl.BlockSpec((TM, d), lambda i: (bid(i) // nblk, 0))]
        else:
            srcs = [pl.BlockSpec((TM, d), lambda i: (bid(i), 0)), pl.BlockSpec((TM, d // 2), lambda i: (bid(i), 0)),
                    pl.BlockSpec((TM, d // 2), lambda i: (bid(i) + t // TM, 0)),
                    pl.BlockSpec((TM, LANES), lambda i: (bid(i), 0)), modspec]
        return srcs + [modspec, table, table]

    if first:
        blk_args = tuple(src[1:]) + (mod, cos_t, sin_t)
    else:
        blk_args = (src[1], src[2], src[2], src[3], src[4], mod, cos_t, sin_t)
    outs = [(d, F32), (SSD_XBC, F32), None, (GROUP_W, F32), (LANES, F32),
            (N_HEADS * LANES, BF16), (LANES, BF16), (2 * LANES, BF16),
            (N_HEADS * LANES, BF16), (LANES, BF16), (2 * LANES, BF16)]
    shapes = [jax.ShapeDtypeStruct((t, o[0]), o[1]) if o else
              jax.ShapeDtypeStruct((S5_GROUPS, t // S5_Q, S5_BLK), F32) for o in outs]
    specs = [pl.BlockSpec((ROW_SUB * TM, o[0]), row) if o else
             pl.BlockSpec((S5_GROUPS, ROW_SUB * S5_TB, S5_BLK), lambda i: (0, i, 0)) for o in outs]
    return pl.pallas_call(
        functools.partial(_inproj_kernel, first=first, nblk=nblk),
        out_shape=shapes,
        grid=(nsteps,),
        in_specs=[sp for s in range(ROW_SUB) for sp in blk_specs(s)] + [
                  pl.BlockSpec((1, d), fix),
                  pl.BlockSpec((d, _C_END), fix),
                  pl.BlockSpec((1, N_HEADS * LANES), fix),
                  pl.BlockSpec((1, LANES), fix)],
        out_specs=specs,
        scratch_shapes=[pltpu.VMEM((GROUP_W // LANES, TM, LANES), F32)],
        compiler_params=_cp(("parallel",)),
        name="in_proj",
    )(*(blk_args * ROW_SUB), norm_g.reshape(1, d), w_packed,
      jnp.tile(qn_g, 2 * N_HEADS).reshape(1, -1), jnp.tile(kn_g, 2).reshape(1, -1))


def _rope_tables(lc, l):
    n_rows = l // GRID_W
    rows = np.repeat(np.arange(n_rows), GRID_W)
    cols = np.tile(np.arange(GRID_W), n_rows)
    inv = np.power(np.float32(ROPE_BASE), -np.arange(ROPE_FREQS, dtype=np.float32) / ROPE_FREQS)
    ang = np.stack([rows, cols], axis=-1).astype(np.float32)[..., None] * inv
    cos = np.cos(ang)
    sin = np.sin(ang)
    cos64 = np.stack([cos, cos], axis=2).reshape(l, HEAD_DIM)
    sin64 = np.stack([-sin, sin], axis=2).reshape(l, HEAD_DIM)
    cos64 = np.concatenate([np.ones((lc, HEAD_DIM), np.float32), cos64], axis=0)
    sin64 = np.concatenate([np.zeros((lc, HEAD_DIM), np.float32), sin64], axis=0)
    return (jnp.asarray(np.tile(cos64, (1, 2)), dtype=F32), jnp.asarray(np.tile(sin64, (1, 2)), dtype=F32))


def _merge_heads(o2, kvh):
    tq = o2.shape[0] // 2
    oa, ob = o2[:tq], o2[tq:]
    lane = lax.broadcasted_iota(jnp.int32, oa.shape, 1)
    if kvh == 0:
        return jnp.where(lane < HEAD_DIM, oa, pltpu.roll(ob, HEAD_DIM, 1))
    return jnp.where(lane < HEAD_DIM, pltpu.roll(oa, HEAD_DIM, 1), ob)


def _stack_q(q_ref, rows, kvh):
    return jnp.concatenate([q_ref[rows, (2 * kvh) * LANES:(2 * kvh + 1) * LANES],
                            q_ref[rows, (2 * kvh + 1) * LANES:(2 * kvh + 2) * LANES]], axis=0)


def _ga_attend(q_ref, k_ref, v_ref, o_ref, nkeys):
    k = k_ref[0:nkeys, :]
    for sub in range(GA_SUB):
        rows = slice(sub * GA_TQ, (sub + 1) * GA_TQ)
        scores = [_dot_nt(_stack_q(q_ref, rows, kvh), k) for kvh in range(2)]
        outs = []
        for kvh in range(2):
            s = scores[kvh]
            p = jnp.exp2((s - jnp.max(s, axis=1, keepdims=True)).astype(BF16))
            o2 = _dot(p, v_ref[0:nkeys, kvh * LANES:(kvh + 1) * LANES])
            outs.append(_merge_heads(o2 / pltpu.roll(o2, HEAD_DIM, 1), kvh))
        o_ref[rows, :] = jnp.concatenate(outs, axis=1)


def _attn_kernel(sink_ref, gq_ref, gk_ref, gv_ref, wq_ref, wk_ref, wv_ref, og_ref, ow_ref, *, lc):
    is_ctx = pl.program_id(1) < lc // TM

    @pl.when(is_ctx)
    def _():
        _ga_attend(gq_ref, gk_ref, gv_ref, og_ref, lc)
        _wa_attend(sink_ref, wq_ref, wk_ref, wv_ref, ow_ref, lc)

    @pl.when(jnp.logical_not(is_ctx))
    def _():
        _ga_attend(gq_ref, gk_ref, gv_ref, og_ref, gk_ref.shape[0])
        _wa_attend(sink_ref, wq_ref, wk_ref, wv_ref, ow_ref, lc)


def _attn(sink, gq, gk, gv, wq, wk, wv, nb, s_len, lc):
    t = gq.shape[0]
    nq = s_len // TM
    assert GA_SUB * GA_TQ == TM and WA_SUB * TQ == TM
    qspec = pl.BlockSpec((TM, N_HEADS * LANES), lambda b, j: (b * nq + j, 0))
    kspec = pl.BlockSpec((s_len, LANES), lambda b, j: (b, 0))
    vspec = pl.BlockSpec((s_len, 2 * LANES), lambda b, j: (b, 0))
    ospec = pl.BlockSpec((TM, GROUP_W), lambda b, j: (b * nq + j, 0))
    return pl.pallas_call(
        functools.partial(_attn_kernel, lc=lc),
        out_shape=[jax.ShapeDtypeStruct((t, GROUP_W), F32)] * 2,
        grid=(nb, nq),
        in_specs=[pl.BlockSpec(memory_space=pltpu.SMEM), qspec, kspec, vspec, qspec, kspec, vspec],
        out_specs=[ospec, ospec],
        compiler_params=_cp(("parallel", "arbitrary")),
        name="attention",
    )(sink, gq, gk, gv, wq, wk, wv)


WA_SUB = TM // TQ


def _wa_attend(sink_ref, q_ref, k_ref, v_ref, o_ref, lc):
    s_len = k_ref.shape[0]
    kc = k_ref[0:lc, :]
    row = lax.broadcasted_iota(jnp.int32, (2 * TQ, 1), 0)
    for sub in range(WA_SUB):
        rows = slice(sub * TQ, (sub + 1) * TQ)
        n = pl.program_id(1) * WA_SUB + sub - lc // TQ
        start = pl.multiple_of(jnp.clip(lc + (n - 1) * TQ, lc, s_len - 3 * TQ), TQ)
        kb = k_ref[pl.ds(start, 3 * TQ), :]
        qpos = n * TQ + lax.broadcasted_iota(jnp.int32, (TQ, 3 * TQ), 0)
        kpos = (start - lc) + lax.broadcasted_iota(jnp.int32, (TQ, 3 * TQ), 1)
        reach = jnp.where(n >= 0, WINDOW, -1)
        valid = jnp.abs(qpos - kpos) <= reach
        valid = jnp.concatenate([valid, valid], axis=0)
        outs = []
        for kvh in range(2):
            q2 = jnp.concatenate([q_ref[rows, (2 * kvh) * LANES:(2 * kvh + 1) * LANES],
                                  q_ref[rows, (2 * kvh + 1) * LANES:(2 * kvh + 2) * LANES]], axis=0)
            sc = _dot_nt(q2, kc)
            sb = jnp.where(valid, _dot_nt(q2, kb), NEG_INF)
            sink = jnp.where(row < TQ, sink_ref[2 * kvh], sink_ref[2 * kvh + 1]) * LOG2E
            m = jnp.maximum(jnp.maximum(jnp.max(sc, axis=1, keepdims=True), jnp.max(sb, axis=1, keepdims=True)), sink)
            pc = jnp.exp2((sc - m).astype(BF16))
            pb = jnp.exp2((sb - m).astype(BF16))
            vcols = slice(kvh * LANES, (kvh + 1) * LANES)
            o2 = _dot(pc, v_ref[0:lc, vcols]) + _dot(pb, v_ref[pl.ds(start, 3 * TQ), vcols])
            denom = pltpu.roll(o2, HEAD_DIM, 1) + jnp.exp2(sink - m)
            outs.append(_merge_heads(o2 / denom, kvh))
        o_ref[rows, :] = jnp.concatenate(outs, axis=1)


def _s5_chunk_index(t, rev, nc_ctx, nc_tot):
    if not rev:
        return t
    return jnp.where(t < nc_ctx, nc_ctx - 1 - t, nc_tot - 1 - (t - nc_ctx))


def _s5_kernel(u_ref, k_ref, p_ref, g_ref, ar_ref, ai_ref, dsk_ref, y_ref, s_scr, h_scr, m_scr, *, nb, nc_ctx, nc_tot):
    for d in range(2):
        ext = k_ref[d, 0]
        for s in range(S5_Q):
            lo = ((S5_Q - s) if d == 0 else (S5_Q - 1 - s)) * S5_CH
            win = pltpu.roll(ext, (2 * S5_BLK - lo) % (2 * S5_BLK), 1)[:, :S5_BLK]
            m_scr[d, s * S5_CH:(s + 1) * S5_CH, :] = win.astype(BF16)
    uf = u_ref[0]
    u = uf.astype(BF16)
    for d in range(2):
        for k in range(2):
            s_scr[d, k] = _dot(u, p_ref[d, k, 0])
    ar = [jnp.broadcast_to(ar_ref[d, 0], (nb, LANES)) for d in range(2)]
    ai = [[jnp.broadcast_to(ai_ref[d, k, 0], (nb, LANES)) for k in range(2)] for d in range(2)]

    def body(t, carry):
        out = []
        for d in range(2):
            h, hs = carry[d]
            rows = pl.ds(_s5_chunk_index(t, d == 1, nc_ctx, nc_tot), nb, stride=nc_tot)
            h_scr[d, rows, :] = h
            out.append((ar[d] * h + ai[d][0] * hs + s_scr[d, 0, rows, :],
                        ar[d] * hs + ai[d][1] * h + s_scr[d, 1, rows, :]))
        return tuple(out)

    zero = jnp.zeros((nb, LANES), F32)
    lax.fori_loop(0, nc_tot, body, ((zero, zero), (zero, zero)), unroll=2)
    y = uf * dsk_ref[0]
    for d in range(2):
        y = y + _dot(u, m_scr[d]) + _dot(h_scr[d].astype(BF16), g_ref[d, 0])
    y_ref[0] = y


S5_TB = TM // S5_Q
S5_GPS = LANES // S5_CH


def _s5_pack(lo_ref, hi_ref, o_ref):
    for s in range(S5_Q):
        rows = pl.ds(s, S5_TB, stride=S5_Q)
        halves = (lo_ref[rows, :], hi_ref[rows, :])
        dst = S5_CH * (s % S5_GPS)
        for g in range(S5_GROUPS):
            slab = halves[g // S5_GPS]
            src = S5_CH * (g % S5_GPS)
            moved = slab if src == dst else pltpu.roll(slab, (dst - src) % LANES, 1)
            o_ref[g, :, s * S5_CH:(s + 1) * S5_CH] = moved[:, dst:dst + S5_CH]


def _s5_unpack(y_ref, o_ref):
    lane_grp = lax.broadcasted_iota(jnp.int32, (S5_TB, LANES), 1) // S5_CH
    for s in range(S5_Q):
        src = S5_CH * (s % S5_GPS)
        for half in range(S5_GROUPS // S5_GPS):
            acc = None
            for gl in range(S5_GPS):
                slab = y_ref[half * S5_GPS + gl, :, (s // S5_GPS) * LANES:(s // S5_GPS + 1) * LANES]
                dst = S5_CH * gl
                moved = slab if src == dst else pltpu.roll(slab, (dst - src) % LANES, 1)
                acc = moved if acc is None else jnp.where(lane_grp == gl, moved, acc)
            o_ref[half, pl.ds(s, S5_TB, stride=S5_Q), :] = acc


def _s5_params(lam_re, lam_im, log_dt, b_re, b_im, c_re, c_im, d_skip):
    q = S5_Q
    dt = jnp.exp(log_dt)[..., None]
    lr, li = lam_re, lam_im
    mag = jnp.exp(lr * dt)
    a_re = mag * jnp.cos(li * dt)
    a_im = mag * jnp.sin(li * dt)
    den = lr * lr + li * li
    f_re = ((a_re - 1.0) * lr + a_im * li) / den
    f_im = (a_im * lr - (a_re - 1.0) * li) / den
    bb_re = f_re[..., None] * b_re - f_im[..., None] * b_im
    bb_im = f_re[..., None] * b_im + f_im[..., None] * b_re
    kk = jnp.arange(q + 1, dtype=F32)[:, None, None, None]
    pmag = jnp.exp(kk * (lr * dt))
    pw_re = pmag * jnp.cos(kk * (li * dt))
    pw_im = pmag * jnp.sin(kk * (li * dt))
    lw_re = pw_re[:q].transpose(1, 2, 0, 3)[:, :, :, None, :]
    lw_im = pw_im[:q].transpose(1, 2, 0, 3)[:, :, :, None, :]
    ck_re = c_re[:, :, None] * lw_re - c_im[:, :, None] * lw_im
    ck_im = c_re[:, :, None] * lw_im + c_im[:, :, None] * lw_re
    ck = jnp.concatenate([ck_re, -ck_im], axis=-1).reshape(2, S5_GROUPS, S5_BLK, 2 * S5_STATE)
    kern_t = jnp.einsum("dgmp,dgpc->dgcm", ck, jnp.concatenate([bb_re, bb_im], axis=2), precision=HI)
    kern_t = kern_t.reshape(2, S5_GROUPS, S5_CH, q, S5_CH)
    zeros = jnp.zeros_like(kern_t)
    bbt_re = bb_re.transpose(0, 1, 3, 2)[:, :, None]
    bbt_im = bb_im.transpose(0, 1, 3, 2)[:, :, None]
    ct_re = c_re.transpose(0, 1, 3, 2)[:, :, :, None, :]
    ct_im = c_im.transpose(0, 1, 3, 2)[:, :, :, None, :]
    ms, ps, gs = [], [], []
    for d in range(2):
        ext = (jnp.concatenate([zeros[d], kern_t[d]], axis=2) if d == 0
               else jnp.concatenate([kern_t[d, :, :, ::-1], zeros[d]], axis=2))
        ext = ext.reshape(S5_GROUPS, S5_CH, 2 * S5_BLK)
        ms.append(ext)
        pidx = (q - 1 - jnp.arange(q)) if d == 0 else jnp.arange(q)
        pr = pw_re[pidx, d].transpose(1, 0, 2)[:, :, None, :]
        pi = pw_im[pidx, d].transpose(1, 0, 2)[:, :, None, :]
        p_re = pr * bbt_re[d] - pi * bbt_im[d]
        p_im = pr * bbt_im[d] + pi * bbt_re[d]
        pd = jnp.stack([jnp.concatenate([p_re, p_im], axis=3), jnp.concatenate([p_im, p_re], axis=3)])
        ps.append(pd.reshape(2, S5_GROUPS, S5_BLK, 2 * S5_STATE))
        gidx = (jnp.arange(q) + 1) if d == 0 else (q - jnp.arange(q))
        gw_re = pw_re[gidx, d].transpose(1, 2, 0)[..., None]
        gw_im = pw_im[gidx, d].transpose(1, 2, 0)[..., None]
        g_re = ct_re[d] * gw_re - ct_im[d] * gw_im
        g_im = ct_re[d] * gw_im + ct_im[d] * gw_re
        gs.append(jnp.concatenate([g_re, -g_im], axis=1).reshape(S5_GROUPS, 2 * S5_STATE, S5_BLK))
    ar = jnp.concatenate([pw_re[q], pw_re[q]], axis=-1)[:, :, None, :]
    ai = jnp.stack([jnp.concatenate([-pw_im[q], pw_im[q]], axis=-1),
                    jnp.concatenate([pw_im[q], -pw_im[q]], axis=-1)], axis=1)[:, :, :, None, :]
    dsk = jnp.tile(d_skip.reshape(S5_GROUPS, 1, S5_CH), (1, 1, q))
    return (jnp.stack(ms), jnp.stack(ps).astype(BF16), jnp.stack(gs).astype(BF16),
            ar.astype(F32), ai.astype(F32), dsk.astype(F32))


def _s5(ug, params, nb, s_len, lc):
    m, p, g, ar, ai, dsk = params
    nc_tot = s_len // S5_Q
    nc_ctx = lc // S5_Q
    r = nb * nc_tot
    return pl.pallas_call(
        functools.partial(_s5_kernel, nb=nb, nc_ctx=nc_ctx, nc_tot=nc_tot),
        out_shape=jax.ShapeDtypeStruct((S5_GROUPS, r, S5_BLK), F32),
        grid=(S5_GROUPS,),
        in_specs=[pl.BlockSpec((1, r, S5_BLK), lambda gi: (gi, 0, 0)),
                  pl.BlockSpec((2, 1, S5_CH, 2 * S5_BLK), lambda gi: (0, gi, 0, 0)),
                  pl.BlockSpec((2, 2, 1, S5_BLK, 2 * S5_STATE), lambda gi: (0, 0, gi, 0, 0)),
                  pl.BlockSpec((2, 1, 2 * S5_STATE, S5_BLK), lambda gi: (0, gi, 0, 0)),
                  pl.BlockSpec((2, 1, 1, 2 * S5_STATE), lambda gi: (0, gi, 0, 0)),
                  pl.BlockSpec((2, 2, 1, 1, 2 * S5_STATE), lambda gi: (0, 0, gi, 0, 0)),
                  pl.BlockSpec((1, 1, S5_BLK), lambda gi: (gi, 0, 0))],
        out_specs=pl.BlockSpec((1, r, S5_BLK), lambda gi: (gi, 0, 0)),
        scratch_shapes=[pltpu.VMEM((2, 2, r, 2 * S5_STATE), F32), pltpu.VMEM((2, r, 2 * S5_STATE), F32),
                        pltpu.VMEM((2, S5_BLK, S5_BLK), BF16)],
        compiler_params=_cp(("parallel",)),
        name="s5_scan",
    )(ug, m, p, g, ar, ai, dsk)


CONV_ROWS = 4 * TM


def _conv_kernel(x_ref, prev_ref, next_ref, w_ref, b_ref, o_ref, *, s_len, lc):
    x = x_ref[...]
    rows = x.shape[0]
    ridx = lax.broadcasted_iota(jnp.int32, x.shape, 0)
    pos = (pl.program_id(0) * rows) % s_len + ridx
    pos = jnp.where(pos >= s_len, pos - s_len, pos)
    seg_first = jnp.logical_or(pos == 0, pos == lc)
    seg_last = jnp.logical_or(pos == lc - 1, pos == s_len - 1)
    xm = jnp.where(ridx == 0, prev_ref[SUBLANES - 1:SUBLANES, :], pltpu.roll(x, 1, 0))
    xp = jnp.where(ridx == rows - 1, next_ref[0:1, :], pltpu.roll(x, rows - 1, 0))
    xm = jnp.where(seg_first, 0.0, xm)
    xp = jnp.where(seg_last, 0.0, xp)
    y = xm * w_ref[0:1, :] + x * w_ref[1:2, :] + xp * w_ref[2:3, :] + b_ref[...]
    o_ref[...] = _silu(y)


def _conv(xbc, w, b, s_len, lc):
    t, c = xbc.shape
    rows = next(r for r in (CONV_ROWS, CONV_ROWS // 2, TM) if t % r == 0)
    per = rows // SUBLANES
    last = t // SUBLANES - 1
    return pl.pallas_call(
        functools.partial(_conv_kernel, s_len=s_len, lc=lc),
        out_shape=jax.ShapeDtypeStruct((t, c), F32),
        grid=(t // rows,),
        in_specs=[pl.BlockSpec((rows, c), lambda i: (i, 0)),
                  pl.BlockSpec((SUBLANES, c), lambda i: (jnp.maximum(i * per - 1, 0), 0)),
                  pl.BlockSpec((SUBLANES, c), lambda i: (jnp.minimum((i + 1) * per, last), 0)),
                  pl.BlockSpec((3, c), lambda i: (0, 0)),
                  pl.BlockSpec((1, c), lambda i: (0, 0))],
        out_specs=pl.BlockSpec((rows, c), lambda i: (i, 0)),
        compiler_params=_cp(("parallel",)),
        name="ssd_conv",
    )(xbc, xbc, xbc, w, b.reshape(1, c))


_X_B = GROUP_W
_X_C = GROUP_W + SSD_NGROUPS * SSD_STATE


def _ssd_kernel(xf_ref, dtf_ref, dttf_ref, xr_ref, dtr_ref, dttr_ref, bias_ref, a_ref, biast_ref, at_ref, dsk_ref,
                yf_ref, yr_ref, stf_ref, str_ref):
    @pl.when(pl.program_id(1) == 0)
    def _():
        stf_ref[...] = jnp.zeros_like(stf_ref)
        str_ref[...] = jnp.zeros_like(str_ref)

    par = (bias_ref[...], a_ref[...], biast_ref[...], at_ref[...], dsk_ref[...])
    for j in range(SSD_SUB):
        rf = slice(j * TQ, (j + 1) * TQ)
        rr = slice((SSD_SUB - 1 - j) * TQ, (SSD_SUB - j) * TQ)
        for b in range(xf_ref.shape[0]):
            yf_ref[b, rf, :] = _ssd_chunk_step(xf_ref[b, rf, :], dtf_ref[b, rf, :], dttf_ref[b, :, rf], par,
                                               stf_ref.at[b], False)
            yr_ref[b, rr, :] = _ssd_chunk_step(xr_ref[b, rr, :], dtr_ref[b, rr, :], dttr_ref[b, :, rr], par,
                                               str_ref.at[b], True)


def _ssd_chunk_step(xc, dt_raw, dtt_raw, par, st_ref, rev):
    bias, a_vec, biast, at_vec, dsk = par
    base = SSD_HEADS if rev else 0
    x = xc[:, 0:GROUP_W]
    dt = _softplus(dt_raw + bias)
    a = dt * a_vec
    dtt = _softplus(dtt_raw + biast)
    at = dtt * at_vec
    ri = lax.broadcasted_iota(jnp.int32, (TQ, TQ), 0)
    ci = lax.broadcasted_iota(jnp.int32, (TQ, TQ), 1)
    causal = (ci >= ri) if rev else (ri >= ci)
    tri = jnp.where(causal, 1.0, 0.0).astype(BF16)
    cum_c = sum(_dot(tri, piece) for piece in _bf16_pieces(a))
    cum_r = sum(_dot_nt(piece, tri) for piece in _bf16_pieces(at))
    edge = 0 if rev else TQ - 1
    tot = cum_c[edge:edge + 1, :]

    shape = (TQ, GROUP_W)
    xdt = x * _per_head_cols(dt, base, SSD_HEADS, shape)
    lane = lax.broadcasted_iota(jnp.int32, shape, 1)
    y = jnp.zeros(shape, F32)
    bmat = [xc[:, _X_B + g * SSD_STATE:_X_B + (g + 1) * SSD_STATE].astype(BF16) for g in range(SSD_NGROUPS)]
    cmat = [xc[:, _X_C + g * SSD_STATE:_X_C + (g + 1) * SSD_STATE].astype(BF16) for g in range(SSD_NGROUPS)]
    cb = [_dot_nt(cmat[g], bmat[g]) for g in range(SSD_NGROUPS)]
    for h in range(SSD_HEADS):
        col = base + h
        seg = jnp.where(causal, cum_c[:, col:col + 1] - cum_r[col:col + 1, :], NEG_INF)
        scores = cb[h // 2] * jnp.exp(seg)
        xh = jnp.where((lane >= h * HEAD_DIM) & (lane < (h + 1) * HEAD_DIM), xdt, 0.0)
        y = y + _dot(scores.astype(BF16), xh.astype(BF16))
    st = st_ref[...]
    yo = jnp.concatenate(
        [_dot_nt(cmat[g], st[g * SSD_STATE:(g + 1) * SSD_STATE].astype(BF16)) for g in range(SSD_NGROUPS)], axis=1)
    y = y + yo * _per_head_cols(jnp.exp(cum_c), base, SSD_HEADS, shape)
    if not rev:
        y = y + x * dsk
    xd = xdt * _per_head_cols(jnp.exp(tot - cum_c), base, SSD_HEADS, shape)
    xdt_t = xd.T.astype(BF16)
    decay = jnp.exp(tot)
    for g in range(SSD_NGROUPS):
        new = _dot(xdt_t[g * SSD_STATE:(g + 1) * SSD_STATE], bmat[g])
        for hh in range(2):
            h = 2 * g + hh
            r0 = h * HEAD_DIM
            st_ref[r0:r0 + HEAD_DIM, :] = (decay[:, base + h:base + h + 1] * st[r0:r0 + HEAD_DIM]
                                           + new[hh * HEAD_DIM:(hh + 1) * HEAD_DIM])
    return y


def _bf16_pieces(x):
    hi = x.astype(BF16)
    rest = x - hi.astype(F32)
    mid = rest.astype(BF16)
    return hi, mid, (rest - mid.astype(F32)).astype(BF16)


def _ssd_chunk(c, rev, nc_ctx, nc_tot):
    if not rev:
        return c
    return jnp.where(c < nc_ctx, nc_ctx - 1 - c, nc_tot - 1 - (c - nc_ctx))


SSD_SUB = TM // TQ
SSD_NB = 4


def _ssd_scan(xc, dt, dtt, bias, a, biast, at, dsk, nb, s_len, lc):
    t = xc.shape[0]
    nblk = s_len // TM
    nctx = lc // TM
    nbs = math.gcd(nb, SSD_NB)
    fix = lambda b, c: (0, 0)
    xc3 = xc.reshape(nb, s_len, SSD_XBC)
    dt3 = dt.reshape(nb, s_len, LANES)

    def rows(rev):
        return lambda b, c: (b, _ssd_chunk(c, rev, nctx, nblk), 0)

    def lanes(rev):
        return lambda b, c: (b, 0, _ssd_chunk(c, rev, nctx, nblk))

    def data_specs(rev):
        return [pl.BlockSpec((nbs, TM, SSD_XBC), rows(rev)), pl.BlockSpec((nbs, TM, LANES), rows(rev)),
                pl.BlockSpec((nbs, SUBLANES, TM), lanes(rev))]

    state = pltpu.VMEM((nbs, SSD_HEADS * HEAD_DIM, SSD_STATE), F32)
    yf, yr = pl.pallas_call(
        _ssd_kernel,
        out_shape=[jax.ShapeDtypeStruct((nb, s_len, GROUP_W), F32)] * 2,
        grid=(nb // nbs, nblk),
        in_specs=data_specs(False) + data_specs(True) + [
            pl.BlockSpec((1, LANES), fix), pl.BlockSpec((1, LANES), fix),
            pl.BlockSpec((SUBLANES, TQ), fix), pl.BlockSpec((SUBLANES, TQ), fix),
            pl.BlockSpec((1, GROUP_W), fix)],
        out_specs=[pl.BlockSpec((nbs, TM, GROUP_W), rows(False)), pl.BlockSpec((nbs, TM, GROUP_W), rows(True))],
        scratch_shapes=[state, state],
        compiler_params=_cp(("parallel", "arbitrary")),
        name="ssd_scan",
    )(xc3, dt3, dtt, xc3, dt3, dtt, bias, a, biast, at, dsk)
    return yf.reshape(t, GROUP_W), yr.reshape(t, GROUP_W)


def _ssd(xbc, dt, conv_w, conv_b, dt_bias, a_log, d_skip, nb, s_len, lc):
    xc = _conv(xbc, conv_w, conv_b, s_len, lc)
    nd = 2 * SSD_HEADS
    dtt = dt[:, :nd].reshape(nb, s_len, nd).transpose(0, 2, 1)
    bias = jnp.pad(dt_bias.reshape(1, nd), ((0, 0), (0, LANES - nd)))
    a = jnp.pad(-jnp.exp(a_log).reshape(1, nd), ((0, 0), (0, LANES - nd)))
    biast = jnp.broadcast_to(dt_bias.reshape(nd, 1), (nd, TQ))
    at = jnp.broadcast_to(-jnp.exp(a_log).reshape(nd, 1), (nd, TQ))
    dsk = jnp.repeat(d_skip, HEAD_DIM).reshape(1, GROUP_W)
    return _ssd_scan(xc, dt, dtt, bias, a, biast, at, dsk, nb, s_len, lc)


def _outproj_kernel(x_ref, ys5_ref, oga_ref, y0_ref, y1_ref, z_ref, owa_ref, *refs):
    mods, shared = refs[:ROW_SUB], refs[ROW_SUB:]
    for s in range(ROW_SUB):
        rows = _row_views((x_ref, oga_ref, y0_ref, y1_ref, z_ref, owa_ref) + tuple(shared[-4:-1]), s)
        _outproj_block(rows[0], ys5_ref.at[:, pl.ds(s * S5_TB, S5_TB), :], *rows[1:6], mods[s], *shared[:-4],
                       *rows[6:], shared[-1])


def _outproj_block(x_ref, ys5_ref, oga_ref, y0_ref, y1_ref, z_ref, owa_ref, mod_ref, gluw_ref, glub_ref,
                   ng_ref, wout_ref, n2_ref, wr_ref, br_ref, xn_o, h2_o, route_o, y_scr):
    _s5_unpack(ys5_ref, y_scr)
    gl = _gelu_tanh(jnp.concatenate([y_scr[0], y_scr[1]], axis=1))
    a = gl * _sigmoid(_dot(gl.astype(BF16), gluw_ref[...]) + glub_ref[...])
    m = (y0_ref[...] + y1_ref[...]) * _silu(z_ref[...])
    m = m * lax.rsqrt(jnp.mean(m * m, axis=-1, keepdims=True) + EPS) * ng_ref[...]
    w = wout_ref
    mix = (_dot(a.astype(BF16), w[0:GROUP_W, :]) + _dot(oga_ref[...].astype(BF16), w[GROUP_W:2 * GROUP_W, :])
           + _dot(m.astype(BF16), w[2 * GROUP_W:3 * GROUP_W, :]) + _dot(owa_ref[...].astype(BF16), w[3 * GROUP_W:, :]))
    xn = x_ref[...] + mod_ref[0, 2:3, :] * mix
    xn_o[...] = xn
    h2 = xn * lax.rsqrt(jnp.mean(xn * xn, axis=-1, keepdims=True) + EPS) * n2_ref[...]
    h2 = h2 * (1.0 + mod_ref[0, 4:5, :]) + mod_ref[0, 3:4, :]
    h2_o[...] = _pack_bf16_pair(h2)
    h_hi = h2.astype(BF16)
    h_lo = (h2 - h_hi.astype(F32)).astype(BF16)
    logits = _dot(h_hi, wr_ref[0]) + (_dot(h_lo, wr_ref[0]) + _dot(h_hi, wr_ref[1])) + br_ref[...]
    lane = lax.broadcasted_iota(jnp.int32, logits.shape, 1).astype(F32)
    big = float(4 * LANES)
    lcoarse = jnp.where(lane < MOE_GROUPS, logits, NEG_INF)
    mx = jnp.max(lcoarse, axis=1, keepdims=True)
    den = jnp.sum(jnp.exp(lcoarse - mx), axis=1, keepdims=True)
    grp = jnp.min(jnp.where(lcoarse == mx, lane, big), axis=1, keepdims=True)
    pg = 1.0 / den
    lo = ROUTE_FINE0 + grp * MOE_PER_GROUP
    lf = jnp.where(lane >= lo, jnp.where(lane < lo + MOE_PER_GROUP, logits, NEG_INF), NEG_INF)
    v1 = jnp.max(lf, axis=1, keepdims=True)
    i1 = jnp.min(jnp.where(lf == v1, lane, big), axis=1, keepdims=True)
    lf2 = jnp.where(lane == i1, NEG_INF, lf)
    v2 = jnp.max(lf2, axis=1, keepdims=True)
    i2 = jnp.min(jnp.where(lf2 == v2, lane, big), axis=1, keepdims=True)
    e2 = jnp.exp(v2 - v1)
    w1 = pg / (1.0 + e2)
    w2 = w1 * e2
    route = jnp.where(lane == 0, i1 - ROUTE_FINE0,
                      jnp.where(lane == 1, i2 - ROUTE_FINE0,
                                jnp.where(lane == 2, w1, jnp.where(lane == 3, w2, 0.0))))
    route_o[...] = route


def _outproj(x, ys5, oga, y0, y1, z, owa, mod, glu_w, glu_b, ssd_norm_g, w_out, norm2_g, wr, br, nb, nblk):
    t, d = x.shape
    row = lambda i: (i, 0)
    fix = lambda i: (0, 0)
    step = ROW_SUB * TM
    assert t % step == 0
    gw = pl.BlockSpec((step, GROUP_W), row)
    wr_hi = wr.astype(BF16)
    mod_specs = [pl.BlockSpec((1, 6, d), lambda i, s=s: (_mod_row(ROW_SUB * i + s, nblk, nb), 0, 0))
                 for s in range(ROW_SUB)]
    return pl.pallas_call(
        _outproj_kernel,
        out_shape=[jax.ShapeDtypeStruct((t, d), F32), jax.ShapeDtypeStruct((t, d // 2), jnp.uint32),
                   jax.ShapeDtypeStruct((t, LANES), F32)],
        grid=(t // step,),
        in_specs=[pl.BlockSpec((step, d), row),
                  pl.BlockSpec((S5_GROUPS, ROW_SUB * S5_TB, S5_BLK), lambda i: (0, i, 0)),
                  gw, gw, gw, gw, gw] + mod_specs + [
                  pl.BlockSpec((GROUP_W, GROUP_W), fix),
                  pl.BlockSpec((1, GROUP_W), fix),
                  pl.BlockSpec((1, GROUP_W), fix),
                  pl.BlockSpec((d, d), fix),
                  pl.BlockSpec((1, d), fix),
                  pl.BlockSpec((2, d, LANES), lambda i: (0, 0, 0)),
                  pl.BlockSpec((1, LANES), fix)],
        out_specs=[pl.BlockSpec((step, d), row), pl.BlockSpec((step, d // 2), row), pl.BlockSpec((step, LANES), row)],
        scratch_shapes=[pltpu.VMEM((GROUP_W // LANES, TM, LANES), F32)],
        compiler_params=_cp(("parallel",)),
        name="out_proj_router",
    )(x, ys5, oga, y0, y1, z, owa, *([mod] * ROW_SUB), glu_w.astype(BF16), glu_b.reshape(1, -1), ssd_norm_g.reshape(1, -1),
      w_out.astype(BF16), norm2_g.reshape(1, -1), jnp.stack([wr_hi, (wr - wr_hi.astype(F32)).astype(BF16)]), br)


def _pack_router(coarse_w, coarse_b, fine_w, fine_b):
    def lanes(coarse, fine):
        gap = jnp.zeros(coarse.shape[:-1] + (ROUTE_FINE0 - MOE_GROUPS,), F32)
        tail = jnp.zeros(coarse.shape[:-1] + (LANES - ROUTE_FINE0 - N_EXPERTS,), F32)
        return jnp.concatenate([coarse, gap, fine, tail], axis=-1)

    return lanes(coarse_w, fine_w), lanes(coarse_b[None, :], fine_b[None, :])


def _gather_rows(src, idx):
    m = idx.shape[0]
    d = src.shape[1]
    workers = SC_CORES * SC_SUBCORES
    k = SC_FETCH_K
    nch = m // (workers * k)
    assert nch * workers * k == m
    mesh = plsc.VectorSubcoreMesh(core_axis_name="c", subcore_axis_name="s")

    @functools.partial(
        pl.kernel, mesh=mesh,
        out_type=jax.ShapeDtypeStruct((m, d), src.dtype),
        scratch_types=[pltpu.VMEM((nch, k), jnp.int32),
                       pltpu.VMEM((k, d), src.dtype),
                       pltpu.SemaphoreType.DMA],
    )
    def gather(src_hbm, idx_hbm, out_hbm, idx_v, rows_v, sem):
        wid = lax.axis_index("s") * SC_CORES + lax.axis_index("c")
        pltpu.sync_copy(idx_hbm.at[wid], idx_v)

        @pl.loop(0, nch)
        def _(j):
            off = pl.multiple_of((wid * nch + j) * k, k)
            pltpu.async_copy(src_hbm.at[idx_v.at[j]], rows_v, sem).wait()
            pltpu.sync_copy(rows_v, out_hbm.at[pl.ds(off, k)])

    return gather(src, idx.reshape(workers, nch, k))


def _scatter_rows(src, dst0, dst1, nrows):
    t, d = src.shape
    workers = SC_CORES * SC_SUBCORES
    nch = t // (workers * SC_GATHER_K)
    assert nch * workers * SC_GATHER_K == t
    mesh = plsc.VectorSubcoreMesh(core_axis_name="c", subcore_axis_name="s")

    @functools.partial(
        pl.kernel, mesh=mesh,
        out_type=jax.ShapeDtypeStruct((nrows, d), src.dtype),
        scratch_types=[pltpu.VMEM((nch, SC_GATHER_K), jnp.int32),
                       pltpu.VMEM((nch, SC_GATHER_K), jnp.int32),
                       pltpu.VMEM((SC_GATHER_K, d), src.dtype),
                       pltpu.SemaphoreType.DMA((2,))],
    )
    def scatter(src_hbm, d0_hbm, d1_hbm, out_hbm, i0_v, i1_v, rows_v, sem):
        wid = lax.axis_index("s") * SC_CORES + lax.axis_index("c")
        pltpu.sync_copy(d0_hbm.at[wid], i0_v)
        pltpu.sync_copy(d1_hbm.at[wid], i1_v)

        @pl.loop(0, nch)
        def _(j):
            off = pl.multiple_of((wid * nch + j) * SC_GATHER_K, SC_GATHER_K)
            pltpu.sync_copy(src_hbm.at[pl.ds(off, SC_GATHER_K)], rows_v)
            first = pltpu.async_copy(rows_v, out_hbm.at[i0_v.at[j]], sem.at[0])
            second = pltpu.async_copy(rows_v, out_hbm.at[i1_v.at[j]], sem.at[1])
            first.wait()
            second.wait()

    return scatter(src, dst0.reshape(workers, nch, SC_GATHER_K), dst1.reshape(workers, nch, SC_GATHER_K))


def _expert_kernel(be_ref, nused_ref, nvalid_ref, nxt_ref, slot_ref, x_ref, wg_hbm, wu_hbm, wd_hbm, o_ref,
                   wg_f, wu_f, wd_f, wg_s, wu_s, wd_s, sem, *, layer):
    i = pl.program_id(0)
    used = i < nused_ref[0]
    new_expert = jnp.logical_or(i == 0, be_ref[i] != be_ref[jnp.maximum(i - 1, 0)])

    def weight_copies(expert, slot):
        return [pltpu.make_async_copy(w.at[layer, expert], f.at[slot], sem.at[slot, j])
                for j, (w, f) in enumerate(((wg_hbm, wg_f), (wu_hbm, wu_f), (wd_hbm, wd_f)))]

    @pl.when(jnp.logical_and(used, new_expert))
    def _():
        slot = slot_ref[i]

        @pl.when(i == 0)
        def _():
            for c in weight_copies(be_ref[i], slot):
                c.start()

        for c in weight_copies(be_ref[i], slot):
            c.wait()
        wg_s[...] = wg_f[slot].astype(BF16)
        wu_s[...] = wu_f[slot].astype(BF16)
        wd_s[...] = wd_f[slot].astype(BF16)

        @pl.when(nxt_ref[i] >= 0)
        def _():
            for c in weight_copies(nxt_ref[i], 1 - slot):
                c.start()

    def swiglu(rows):
        row = rows.start + lax.broadcasted_iota(jnp.int32, (rows.stop - rows.start, x_ref.shape[1]), 0)
        lo, hi = _unpack_bf16_pair(jnp.where(row < nvalid_ref[i], x_ref[rows, :], jnp.uint32(0)))
        lo = lo.astype(BF16)
        hi = hi.astype(BF16)
        half = lo.shape[1]
        gate = _dot(lo, wg_s[0:half, :]) + _dot(hi, wg_s[half:, :])
        up = _dot(lo, wu_s[0:half, :]) + _dot(hi, wu_s[half:, :])
        o_ref[rows, :] = _pack_bf16_pair(_dot((_silu(gate) * up).astype(BF16), wd_s[...]))

    used = i < nused_ref[0]
    half_rows = MOE_TM // 2

    @pl.when(jnp.logical_and(used, nvalid_ref[i] > half_rows))
    def _():
        swiglu(slice(0, MOE_TM))

    @pl.when(jnp.logical_and(used, nvalid_ref[i] <= half_rows))
    def _():
        swiglu(slice(0, half_rows))
        o_ref[half_rows:, :] = jnp.zeros((MOE_TM - half_rows, o_ref.shape[1]), o_ref.dtype)


def _experts(xs, blk_e, n_used, n_valid, nxt_e, slot, wg, wu, wd, layer):
    rows, dp = xs.shape
    d = 2 * dp
    nblocks = rows // MOE_TM
    de = wg.shape[3]
    blk = lambda i, be, nu, *_: (jnp.minimum(i, nu[0] - 1), 0)
    hbm = pl.BlockSpec(memory_space=pl.ANY)
    grid_spec = pltpu.PrefetchScalarGridSpec(
        num_scalar_prefetch=5,
        grid=(nblocks,),
        in_specs=[pl.BlockSpec((MOE_TM, dp), blk), hbm, hbm, hbm],
        out_specs=pl.BlockSpec((MOE_TM, dp), blk),
        scratch_shapes=[pltpu.VMEM((2, d, de), F32), pltpu.VMEM((2, d, de), F32), pltpu.VMEM((2, de, d), F32),
                        pltpu.VMEM((d, de), BF16), pltpu.VMEM((d, de), BF16), pltpu.VMEM((de, d), BF16),
                        pltpu.SemaphoreType.DMA((2, 3))],
    )
    return pl.pallas_call(
        functools.partial(_expert_kernel, layer=layer),
        out_shape=jax.ShapeDtypeStruct((rows, dp), jnp.uint32),
        grid_spec=grid_spec,
        compiler_params=_cp(("arbitrary",)),
        name="moe_experts",
    )(blk_e, n_used, n_valid, nxt_e, slot, xs, wg, wu, wd)


def _final_kernel(*refs):
    fg_ref, o_ref = refs[-2:]
    for s in range(ROW_SUB):
        y = _moe_residual(*refs[5 * s:5 * s + 5])
        o_ref[s * TM:(s + 1) * TM, :] = y * lax.rsqrt(jnp.mean(y * y, axis=-1, keepdims=True) + EPS) * fg_ref[...]


def _final(xn, rows2, route, mod, final_g, nb, nblk):
    t, d = xn.shape
    nlat = nblk - 1
    assert (nb * nlat) % ROW_SUB == 0

    def blk_specs(s):
        lat = lambda i: ROW_SUB * i + s
        src = lambda i: ((lat(i) // nlat) * nblk + 1 + lat(i) % nlat, 0)
        return [pl.BlockSpec((TM, d), src),
                pl.BlockSpec((TM, d // 2), src),
                pl.BlockSpec((TM, d // 2), lambda i: (src(i)[0] + t // TM, 0)),
                pl.BlockSpec((TM, LANES), src),
                pl.BlockSpec((1, 6, d), lambda i: (lat(i) // nlat, 0, 0))]

    return pl.pallas_call(
        _final_kernel,
        out_shape=jax.ShapeDtypeStruct((nb * nlat * TM, d), F32),
        grid=(nb * nlat // ROW_SUB,),
        in_specs=[sp for s in range(ROW_SUB) for sp in blk_specs(s)] + [pl.BlockSpec((1, d), lambda i: (0, 0))],
        out_specs=pl.BlockSpec((ROW_SUB * TM, d), lambda i: (i, 0)),
        compiler_params=_cp(("parallel",)),
        name="moe_combine_final",
    )(*((xn, rows2, rows2, route, mod) * ROW_SUB), final_g.reshape(1, d))


def _moe(h2, route, wg, wu, wd, layer):
    t, d = h2.shape
    n_slots = 2 * t
    experts = jnp.arange(N_EXPERTS, dtype=F32)[None, :]
    oh0 = (route[:, 0:1] == experts).astype(F32)
    oh1 = (route[:, 1:2] == experts).astype(F32)
    both = (oh0 + oh1).reshape(t // LANES, LANES, N_EXPERTS)
    tri = jnp.tril(jnp.ones((LANES, LANES), F32))
    intra = jnp.einsum("ij,bjk->bik", tri, both)
    blk_tot = intra[:, -1, :]
    blk_cum = jnp.cumsum(blk_tot, axis=0)
    earlier = (intra - both + (blk_cum - blk_tot)[:, None, :]).reshape(t, N_EXPERTS)
    counts = blk_cum[-1].astype(jnp.int32)
    pcounts = (counts + MOE_TM - 1) // MOE_TM * MOE_TM
    pends = jnp.cumsum(pcounts)
    pstarts = pends - pcounts
    base = pstarts.astype(F32)[None, :] + earlier
    dest0 = jnp.sum(oh0 * base, axis=1).astype(jnp.int32)
    dest1 = jnp.sum(oh1 * base, axis=1).astype(jnp.int32)
    nblocks = -(-n_slots // MOE_TM) + N_EXPERTS
    nrows = nblocks * MOE_TM
    blk_start = jnp.arange(nblocks, dtype=jnp.int32) * MOE_TM
    blk_e = jnp.minimum(jnp.sum((pends[None, :] <= blk_start[:, None]).astype(jnp.int32), axis=1), N_EXPERTS - 1)
    n_used = (pends[-1] // MOE_TM).astype(jnp.int32).reshape(1)
    n_valid = jnp.clip((pstarts + counts)[blk_e] - blk_start, 0, MOE_TM).astype(jnp.int32)
    ids = jnp.arange(N_EXPERTS, dtype=jnp.int32)
    has = counts > 0
    later = lax.cummin(jnp.where(has, ids, N_EXPERTS)[::-1])[::-1]
    nxt = jnp.concatenate([later[1:], jnp.full((1,), N_EXPERTS, jnp.int32)])
    nxt = jnp.where(nxt >= N_EXPERTS, -1, nxt)
    slot = (jnp.cumsum(has.astype(jnp.int32)) - 1) % 2
    xs = _scatter_rows(h2, dest0, dest1, nrows)
    ys = _experts(xs, blk_e, n_used, n_valid, nxt[blk_e], slot[blk_e], wg, wu, wd, layer)
    return _gather_rows(ys, jnp.concatenate([dest0, dest1]))


def kernel(x, c, ctx, c_ctx, ada_w, ada_b, norm1_g, norm2_g, w_in, w_out, s5_lam_re, s5_lam_im, s5_log_dt, s5_b_re, s5_b_im, s5_c_re, s5_c_im, s5_d, s5_glu_w, s5_glu_b, ga_qn_g, ga_kn_g, ssd_conv_w, ssd_conv_b, ssd_dt_bias, ssd_a_log, ssd_d, ssd_norm_g, wa_sink, moe_coarse_w, moe_coarse_b, moe_fine_w, moe_fine_b, moe_w_gate, moe_w_up, moe_w_down, final_g):
    nb, l, d = x.shape
    lc = ctx.shape[1]
    depth = ada_w.shape[0]
    assert lc == TM and l % TM == 0 and nb <= SUBLANES - 1 and d == D_MODEL
    s_len = lc + l
    nblk = s_len // TM
    t = nb * s_len

    cc = jnp.zeros((SUBLANES, d), F32).at[:nb].set(c).at[nb].set(c_ctx)
    mods = _ada(cc, ada_w, ada_b).reshape(depth, SUBLANES, 6, d)
    cos_t, sin_t = _rope_tables(lc, l)
    w_packed = jax.vmap(_pack_w_in)(w_in)
    s5_tabs = jax.vmap(_s5_params)(s5_lam_re, s5_lam_im, s5_log_dt, s5_b_re, s5_b_im, s5_c_re, s5_c_im, s5_d)
    wrs, brs = jax.vmap(_pack_router)(moe_coarse_w, moe_coarse_b, moe_fine_w, moe_fine_b)

    src = ("first", x.reshape(nb * l, d), ctx.reshape(nb * lc, d))
    for i in range(depth):
        mod = mods[i]
        (xm, xbc, ug, z, dt, gaq, gak, gav, waq, wak, wav) = _inproj(
            src, mod, norm1_g[i], w_packed[i], cos_t, sin_t, ga_qn_g[i], ga_kn_g[i], nb, nblk)
        ys5 = _s5(ug, tuple(tab[i] for tab in s5_tabs), nb, s_len, lc)
        oga, owa = _attn(wa_sink[i], gaq, gak, gav, waq, wak, wav, nb, s_len, lc)
        y0, y1 = _ssd(xbc, dt, ssd_conv_w[i], ssd_conv_b[i], ssd_dt_bias[i], ssd_a_log[i], ssd_d[i], nb, s_len, lc)
        wr, br = wrs[i], brs[i]
        xn, h2, route = _outproj(xm, ys5, oga, y0, y1, z, owa, mod, s5_glu_w[i], s5_glu_b[i], ssd_norm_g[i],
                                 w_out[i], norm2_g[i], wr, br, nb, nblk)
        rows2 = _moe(h2, route, moe_w_gate, moe_w_up, moe_w_down, i)
        src = ("moe", xn, rows2, route, mod)
    return _final(xn, rows2, route, mod, final_g, nb, nblk).reshape(nb, l, d)
```

```python
import functools
import math

import jax
import jax.numpy as jnp
import numpy as np
from jax import lax
from jax.experimental import pallas as pl
from jax.experimental.pallas import tpu as pltpu
from jax.experimental.pallas import tpu_sc as plsc

F32 = jnp.float32
BF16 = jnp.bfloat16
HI = lax.Precision.HIGHEST

D_MODEL = 1024
GRID_W = 64
GROUP_W = 256
HEAD_DIM = 64
ROPE_FREQS = HEAD_DIM // 4
ROPE_BASE = 10000.0
EPS = 1e-6
S5_CH = 16
S5_GROUPS = GROUP_W // S5_CH
S5_STATE = 64
N_HEADS = 4
SSD_HEADS = 4
SSD_NGROUPS = 2
SSD_STATE = 128
SSD_XBC = GROUP_W + 2 * SSD_NGROUPS * SSD_STATE
WINDOW = 128
MOE_GROUPS = 4
MOE_PER_GROUP = 8
N_EXPERTS = 32

LANES = 128
SUBLANES = 8
TM = 256
TQ = 128
GA_TQ = 128
GA_SUB = 2
S5_Q = 32
S5_BLK = S5_Q * S5_CH
MOE_TM = 512
SC_CORES = 2
SC_SUBCORES = 16
SC_GATHER_K = 32
SC_FETCH_K = 64
ROUTE_FINE0 = 32
VMEM_LIMIT = 56 * 1024 * 1024

NEG_INF = float("-inf")
LOG2E = math.log2(math.e)


def _cp(sem, vmem=VMEM_LIMIT):
    return pltpu.CompilerParams(dimension_semantics=sem, vmem_limit_bytes=vmem)


def _dot(a, b):
    return jnp.dot(a, b, preferred_element_type=F32)


def _dot_hi(a, b):
    return jnp.dot(a, b, preferred_element_type=F32, precision=HI)


def _dot_nt(a, b):
    return lax.dot_general(a, b, (((1,), (1,)), ((), ())), preferred_element_type=F32)


def _sigmoid(x):
    return 1.0 / (1.0 + jnp.exp(-x))


def _silu(x):
    return x * _sigmoid(x)


def _gelu_tanh(x):
    return 0.5 * x * (1.0 + jnp.tanh(math.sqrt(2.0 / math.pi) * (x + 0.044715 * (x * x * x))))


def _softplus(x):
    return jnp.maximum(x, 0.0) + jnp.log(1.0 + jnp.exp(-jnp.abs(x)))


_HI16 = 0xFFFF0000


def _pack_bf16_pair(x):
    n = x.shape[1] // 2
    bits = pltpu.bitcast(x.astype(BF16).astype(F32), jnp.uint32)
    return (bits[:, n:] & jnp.uint32(_HI16)) | (bits[:, :n] >> 16)


def _unpack_bf16_pair(w):
    return pltpu.bitcast(w << 16, F32), pltpu.bitcast(w & jnp.uint32(_HI16), F32)


def _per_head_cols(v, base, n_heads, shape):
    lane = lax.broadcasted_iota(jnp.int32, shape, 1)
    out = jnp.broadcast_to(v[:, base + n_heads - 1:base + n_heads], shape)
    for h in range(n_heads - 2, -1, -1):
        out = jnp.where(lane < (h + 1) * HEAD_DIM, v[:, base + h:base + h + 1], out)
    return out


def _ada_kernel(c_ref, w_ref, b_ref, o_ref):
    c = c_ref[...]
    o_ref[0] = _dot_hi(_silu(c), w_ref[0]) + b_ref[0]


def _ada(cc, ada_w, ada_b):
    depth, d, n = ada_w.shape
    tn = 1536
    return pl.pallas_call(
        _ada_kernel,
        out_shape=jax.ShapeDtypeStruct((depth, SUBLANES, n), F32),
        grid=(depth, n // tn),
        in_specs=[pl.BlockSpec((SUBLANES, d), lambda l, j: (0, 0)),
                  pl.BlockSpec((1, d, tn), lambda l, j: (l, 0, j)),
                  pl.BlockSpec((1, 1, tn), lambda l, j: (l, 0, j))],
        out_specs=pl.BlockSpec((1, SUBLANES, tn), lambda l, j: (l, 0, j)),
        compiler_params=_cp(("parallel", "parallel")),
        name="ada_mod",
    )(cc, ada_w, ada_b.reshape(depth, 1, n))


_C_XBC = 0
_C_U = _C_XBC + SSD_XBC
_C_Z = _C_U + GROUP_W
_C_DT = _C_Z + GROUP_W
_C_GAQ = _C_DT + LANES
_C_WAQ = _C_GAQ + N_HEADS * LANES
_C_GAK = _C_WAQ + N_HEADS * LANES
_C_GAV = _C_GAK + LANES
_C_WAK = _C_GAV + LANES
_C_WAV = _C_WAK + LANES
_C_END = _C_WAV + LANES


def _expand_q_cols(wq):
    zero = jnp.zeros((wq.shape[0], HEAD_DIM), wq.dtype)
    parts = []
    for h in range(N_HEADS):
        head = wq[:, h * HEAD_DIM:(h + 1) * HEAD_DIM]
        parts += [head, zero] if h // 2 == 0 else [zero, head]
    return jnp.concatenate(parts, axis=1)


def _pack_w_in(w_in):
    cuts = np.cumsum([256, 256, 128, 128, 256, SSD_XBC, 2 * SSD_HEADS, 256, 128, 128])[:-1]
    u, gaq, gak, gav, z, xbc, dt, waq, wak, wav = jnp.split(w_in, [int(c) for c in cuts], axis=1)
    dt = jnp.pad(dt, ((0, 0), (0, LANES - dt.shape[1])))
    w = jnp.concatenate([xbc, u, z, dt, _expand_q_cols(gaq), _expand_q_cols(waq), gak, gav, wak, wav], axis=1)
    return w.astype(BF16)


def _rope(x, cos, sins):
    w = x.shape[1]
    if w > LANES:
        cos = jnp.concatenate([cos] * (w // LANES), axis=1)
        sins = jnp.concatenate([sins] * (w // LANES), axis=1)
    lane = lax.broadcasted_iota(jnp.int32, x.shape, 1)
    up = pltpu.roll(x, w - ROPE_FREQS, 1)
    dn = pltpu.roll(x, ROPE_FREQS, 1)
    partner = jnp.where((lane & ROPE_FREQS) == 0, up, dn)
    return x * cos + partner * sins


def _v_with_ones(v):
    lo = lax.broadcasted_iota(jnp.int32, v.shape, 1) < HEAD_DIM
    return jnp.concatenate([jnp.where(lo, v, 1.0), jnp.where(lo, 1.0, v)], axis=1).astype(BF16)


def _moe_residual(xn_ref, r0_ref, r1_ref, route_ref, mod_ref):
    route = route_ref[...]
    r0 = jnp.concatenate(_unpack_bf16_pair(r0_ref[...]), axis=1)
    r1 = jnp.concatenate(_unpack_bf16_pair(r1_ref[...]), axis=1)
    return xn_ref[...] + mod_ref[0, 5:6, :] * (route[:, 2:3] * r0 + route[:, 3:4] * r1)


ROW_SUB = 4


def _row_views(refs, s):
    return [r.at[pl.ds(s * TM, TM), :] for r in refs]


def _inproj_kernel(*refs, first, nblk):
    n_blk_in = (2 if first else 5) + 3
    shared = refs[ROW_SUB * n_blk_in:]
    g_ref, w_ref, qn_ref, kn_ref = shared[:4]
    xm_o, xbc_o, ug_o = shared[4:7]
    rest_o = shared[7:-1]
    u_scr = shared[-1]
    for s in range(ROW_SUB):
        blk_refs = refs[s * n_blk_in:(s + 1) * n_blk_in]
        xm_v, xbc_v = _row_views((xm_o, xbc_o), s)
        ug_v = ug_o.at[:, pl.ds(s * S5_TB, S5_TB), :]
        _inproj_block(blk_refs, g_ref, w_ref, qn_ref, kn_ref, xm_v, xbc_v, ug_v, _row_views(rest_o, s), u_scr,
                      first, (pl.program_id(0) * ROW_SUB + s) % nblk == 0)


def _inproj_block(blk_refs, g_ref, w_ref, qn_ref, kn_ref, xm_o, xbc_o, ug_o, rest_o, u_scr, first, is_ctx):
    if first:
        lat_ref, ctx_ref = blk_refs[:2]
        x = jnp.where(is_ctx, ctx_ref[...], lat_ref[...])
    else:
        x = _moe_residual(*blk_refs[:5])
    mod_ref, cos_ref, sin_ref = blk_refs[-3:]
    z_o, dt_o, gaq_o, gak_o, gav_o, waq_o, wak_o, wav_o = rest_o
    xm_o[...] = x
    ms = jnp.mean(x * x, axis=-1, keepdims=True)
    xn = x * lax.rsqrt(ms + EPS) * g_ref[...]
    h = xn * (1.0 + mod_ref[0, 1:2, :]) + mod_ref[0, 0:1, :]
    hb = h.astype(BF16)

    def proj(lo, hi):
        return _dot(hb, w_ref[:, lo:hi])

    cos = cos_ref[...]
    sins = sin_ref[...]
    scale = LOG2E * HEAD_DIM ** -0.5
    q = proj(_C_GAQ, _C_WAQ)
    qs = q * q
    inv = jnp.concatenate(
        [jnp.broadcast_to(lax.rsqrt(jnp.sum(qs[:, s * LANES:(s + 1) * LANES], axis=1, keepdims=True)
                                    * (1.0 / HEAD_DIM) + EPS), (q.shape[0], LANES)) for s in range(N_HEADS)], axis=1)
    gaq_o[...] = (_rope(q * inv * qn_ref[...], cos, sins) * scale).astype(BF16)
    waq_o[...] = (_rope(proj(_C_WAQ, _C_GAK), cos, sins) * scale).astype(BF16)
    k = proj(_C_GAK, _C_GAV)
    ks = k * k
    lane = lax.broadcasted_iota(jnp.int32, k.shape, 1)
    lo = lane < HEAD_DIM
    ms0 = jnp.sum(jnp.where(lo, ks, 0.0), axis=1, keepdims=True)
    ms1 = jnp.sum(jnp.where(lo, 0.0, ks), axis=1, keepdims=True)
    kinv = lax.rsqrt(jnp.where(lo, ms0, ms1) * (1.0 / HEAD_DIM) + EPS)
    gak_o[...] = _rope(k * kinv * kn_ref[...], cos, sins).astype(BF16)
    gav_o[...] = _v_with_ones(proj(_C_GAV, _C_WAK))
    wak_o[...] = _rope(proj(_C_WAK, _C_WAV), cos, sins).astype(BF16)
    wav_o[...] = _v_with_ones(proj(_C_WAV, _C_END))
    xbc_o[...] = proj(_C_XBC, _C_U)
    u = proj(_C_U, _C_Z)
    u_scr[0] = u[:, :LANES]
    u_scr[1] = u[:, LANES:]
    _s5_pack(u_scr.at[0], u_scr.at[1], ug_o)
    z_o[...] = proj(_C_Z, _C_DT)
    dt_o[...] = proj(_C_DT, _C_GAQ)


def _mod_row(i, nblk, nb):
    return jnp.where(i % nblk == 0, nb, i // nblk)


def _inproj(src, mod, norm_g, w_packed, cos_t, sin_t, qn_g, kn_g, nb, nblk):
    first = src[0] == "first"
    d = src[1].shape[1]
    t = nb * nblk * TM
    row = lambda i: (i, 0)
    fix = lambda i: (0, 0)
    nsteps = t // (ROW_SUB * TM)
    assert nsteps * ROW_SUB * TM == t

    def blk_specs(s):
        bid = lambda i: ROW_SUB * i + s
        modspec = pl.BlockSpec((1, 6, d), lambda i: (_mod_row(bid(i), nblk, nb), 0, 0))
        table = pl.BlockSpec((TM, LANES), lambda i: (bid(i) % nblk, 0))
        if first:
            srcs = [pl.BlockSpec((TM, d), lambda i: ((bid(i) // nblk) * (nblk - 1) + jnp.maximum(bid(i) % nblk - 1, 0), 0)),
                    pl.BlockSpec((TM, d), lambda i: (bid(i) // nblk, 0))]
        else:
            srcs = [pl.BlockSpec((TM, d), lambda i: (bid(i), 0)), pl.BlockSpec((TM, d // 2), lambda i: (bid(i), 0)),
                    pl.BlockSpec((TM, d // 2), lambda i: (bid(i) + t // TM, 0)),
                    pl.BlockSpec((TM, LANES), lambda i: (bid(i), 0)), modspec]
        return srcs + [modspec, table, table]

    if first:
        blk_args = tuple(src[1:]) + (mod, cos_t, sin_t)
    else:
        blk_args = (src[1], src[2], src[2], src[3], src[4], mod, cos_t, sin_t)
    outs = [(d, F32), (SSD_XBC, F32), None, (GROUP_W, F32), (LANES, F32),
            (N_HEADS * LANES, BF16), (LANES, BF16), (2 * LANES, BF16),
            (N_HEADS * LANES, BF16), (LANES, BF16), (2 * LANES, BF16)]
    shapes = [jax.ShapeDtypeStruct((t, o[0]), o[1]) if o else
              jax.ShapeDtypeStruct((S5_GROUPS, t // S5_Q, S5_BLK), F32) for o in outs]
    specs = [pl.BlockSpec((ROW_SUB * TM, o[0]), row) if o else
             pl.BlockSpec((S5_GROUPS, ROW_SUB * S5_TB, S5_BLK), lambda i: (0, i, 0)) for o in outs]
    return pl.pallas_call(
        functools.partial(_inproj_kernel, first=first, nblk=nblk),
        out_shape=shapes,
        grid=(nsteps,),
        in_specs=[sp for s in range(ROW_SUB) for sp in blk_specs(s)] + [
                  pl.BlockSpec((1, d), fix),
                  pl.BlockSpec((d, _C_END), fix),
                  pl.BlockSpec((1, N_HEADS * LANES), fix),
                  pl.BlockSpec((1, LANES), fix)],
        out_specs=specs,
        scratch_shapes=[pltpu.VMEM((GROUP_W // LANES, TM, LANES), F32)],
        compiler_params=_cp(("parallel",)),
        name="in_proj",
    )(*(blk_args * ROW_SUB), norm_g.reshape(1, d), w_packed,
      jnp.tile(qn_g, 2 * N_HEADS).reshape(1, -1), jnp.tile(kn_g, 2).reshape(1, -1))


def _rope_tables(lc, l):
    n_rows = l // GRID_W
    rows = np.repeat(np.arange(n_rows), GRID_W)
    cols = np.tile(np.arange(GRID_W), n_rows)
    inv = np.power(np.float32(ROPE_BASE), -np.arange(ROPE_FREQS, dtype=np.float32) / ROPE_FREQS)
    ang = np.stack([rows, cols], axis=-1).astype(np.float32)[..., None] * inv
    cos = np.cos(ang)
    sin = np.sin(ang)
    cos64 = np.stack([cos, cos], axis=2).reshape(l, HEAD_DIM)
    sin64 = np.stack([-sin, sin], axis=2).reshape(l, HEAD_DIM)
    cos64 = np.concatenate([np.ones((lc, HEAD_DIM), np.float32), cos64], axis=0)
    sin64 = np.concatenate([np.zeros((lc, HEAD_DIM), np.float32), sin64], axis=0)
    return (jnp.asarray(np.tile(cos64, (1, 2)), dtype=F32), jnp.asarray(np.tile(sin64, (1, 2)), dtype=F32))


def _merge_heads(o2, kvh):
    tq = o2.shape[0] // 2
    oa, ob = o2[:tq], o2[tq:]
    lane = lax.broadcasted_iota(jnp.int32, oa.shape, 1)
    if kvh == 0:
        return jnp.where(lane < HEAD_DIM, oa, pltpu.roll(ob, HEAD_DIM, 1))
    return jnp.where(lane < HEAD_DIM, pltpu.roll(oa, HEAD_DIM, 1), ob)


def _stack_q(q_ref, rows, kvh):
    return jnp.concatenate([q_ref[rows, (2 * kvh) * LANES:(2 * kvh + 1) * LANES],
                            q_ref[rows, (2 * kvh + 1) * LANES:(2 * kvh + 2) * LANES]], axis=0)


def _ga_attend(q_ref, k_ref, v_ref, o_ref, nkeys):
    k = k_ref[0:nkeys, :]
    for sub in range(GA_SUB):
        rows = slice(sub * GA_TQ, (sub + 1) * GA_TQ)
        scores = [_dot_nt(_stack_q(q_ref, rows, kvh), k) for kvh in range(2)]
        outs = []
        for kvh in range(2):
            s = scores[kvh]
            p = jnp.exp2((s - jnp.max(s, axis=1, keepdims=True)).astype(BF16))
            o2 = _dot(p, v_ref[0:nkeys, kvh * LANES:(kvh + 1) * LANES])
            outs.append(_merge_heads(o2 / pltpu.roll(o2, HEAD_DIM, 1), kvh))
        o_ref[rows, :] = jnp.concatenate(outs, axis=1)


def _attn_kernel(sink_ref, gq_ref, gk_ref, gv_ref, wq_ref, wk_ref, wv_ref, og_ref, ow_ref, *, lc):
    is_ctx = pl.program_id(1) < lc // TM

    @pl.when(is_ctx)
    def _():
        _ga_attend(gq_ref, gk_ref, gv_ref, og_ref, lc)
        _wa_attend(sink_ref, wq_ref, wk_ref, wv_ref, ow_ref, lc)

    @pl.when(jnp.logical_not(is_ctx))
    def _():
        _ga_attend(gq_ref, gk_ref, gv_ref, og_ref, gk_ref.shape[0])
        _wa_attend(sink_ref, wq_ref, wk_ref, wv_ref, ow_ref, lc)


def _attn(sink, gq, gk, gv, wq, wk, wv, nb, s_len, lc):
    t = gq.shape[0]
    nq = s_len // TM
    assert GA_SUB * GA_TQ == TM and WA_SUB * TQ == TM
    qspec = pl.BlockSpec((TM, N_HEADS * LANES), lambda b, j: (b * nq + j, 0))
    kspec = pl.BlockSpec((s_len, LANES), lambda b, j: (b, 0))
    vspec = pl.BlockSpec((s_len, 2 * LANES), lambda b, j: (b, 0))
    ospec = pl.BlockSpec((TM, GROUP_W), lambda b, j: (b * nq + j, 0))
    return pl.pallas_call(
        functools.partial(_attn_kernel, lc=lc),
        out_shape=[jax.ShapeDtypeStruct((t, GROUP_W), F32)] * 2,
        grid=(nb, nq),
        in_specs=[pl.BlockSpec(memory_space=pltpu.SMEM), qspec, kspec, vspec, qspec, kspec, vspec],
        out_specs=[ospec, ospec],
        compiler_params=_cp(("parallel", "arbitrary")),
        name="attention",
    )(sink, gq, gk, gv, wq, wk, wv)


WA_SUB = TM // TQ


def _wa_attend(sink_ref, q_ref, k_ref, v_ref, o_ref, lc):
    s_len = k_ref.shape[0]
    kc = k_ref[0:lc, :]
    row = lax.broadcasted_iota(jnp.int32, (2 * TQ, 1), 0)
    for sub in range(WA_SUB):
        rows = slice(sub * TQ, (sub + 1) * TQ)
        n = pl.program_id(1) * WA_SUB + sub - lc // TQ
        start = pl.multiple_of(jnp.clip(lc + (n - 1) * TQ, lc, s_len - 3 * TQ), TQ)
        kb = k_ref[pl.ds(start, 3 * TQ), :]
        qpos = n * TQ + lax.broadcasted_iota(jnp.int32, (TQ, 3 * TQ), 0)
        kpos = (start - lc) + lax.broadcasted_iota(jnp.int32, (TQ, 3 * TQ), 1)
        reach = jnp.where(n >= 0, WINDOW, -1)
        valid = jnp.abs(qpos - kpos) <= reach
        valid = jnp.concatenate([valid, valid], axis=0)
        outs = []
        for kvh in range(2):
            q2 = jnp.concatenate([q_ref[rows, (2 * kvh) * LANES:(2 * kvh + 1) * LANES],
                                  q_ref[rows, (2 * kvh + 1) * LANES:(2 * kvh + 2) * LANES]], axis=0)
            sc = _dot_nt(q2, kc)
            sb = jnp.where(valid, _dot_nt(q2, kb), NEG_INF)
            sink = jnp.where(row < TQ, sink_ref[2 * kvh], sink_ref[2 * kvh + 1]) * LOG2E
            m = jnp.maximum(jnp.maximum(jnp.max(sc, axis=1, keepdims=True), jnp.max(sb, axis=1, keepdims=True)), sink)
            pc = jnp.exp2((sc - m).astype(BF16))
            pb = jnp.exp2((sb - m).astype(BF16))
            vcols = slice(kvh * LANES, (kvh + 1) * LANES)
            o2 = _dot(pc, v_ref[0:lc, vcols]) + _dot(pb, v_ref[pl.ds(start, 3 * TQ), vcols])
            denom = pltpu.roll(o2, HEAD_DIM, 1) + jnp.exp2(sink - m)
            outs.append(_merge_heads(o2 / denom, kvh))
        o_ref[rows, :] = jnp.concatenate(outs, axis=1)


def _s5_chunk_index(t, rev, nc_ctx, nc_tot):
    if not rev:
        return t
    return jnp.where(t < nc_ctx, nc_ctx - 1 - t, nc_tot - 1 - (t - nc_ctx))


def _s5_kernel(u_ref, k_ref, p_ref, g_ref, ar_ref, ai_ref, dsk_ref, y_ref, s_scr, h_scr, m_scr, *, nb, nc_ctx, nc_tot):
    for d in range(2):
        ext = k_ref[d, 0]
        for s in range(S5_Q):
            lo = ((S5_Q - s) if d == 0 else (S5_Q - 1 - s)) * S5_CH
            win = pltpu.roll(ext, (2 * S5_BLK - lo) % (2 * S5_BLK), 1)[:, :S5_BLK]
            m_scr[d, s * S5_CH:(s + 1) * S5_CH, :] = win.astype(BF16)
    uf = u_ref[0]
    u = uf.astype(BF16)
    for d in range(2):
        for k in range(2):
            s_scr[d, k] = _dot(u, p_ref[d, k, 0])
    ar = [jnp.broadcast_to(ar_ref[d, 0], (nb, LANES)) for d in range(2)]
    ai = [[jnp.broadcast_to(ai_ref[d, k, 0], (nb, LANES)) for k in range(2)] for d in range(2)]

    def body(t, carry):
        out = []
        for d in range(2):
            h, hs = carry[d]
            rows = pl.ds(_s5_chunk_index(t, d == 1, nc_ctx, nc_tot), nb, stride=nc_tot)
            h_scr[d, rows, :] = h
            out.append((ar[d] * h + ai[d][0] * hs + s_scr[d, 0, rows, :],
                        ar[d] * hs + ai[d][1] * h + s_scr[d, 1, rows, :]))
        return tuple(out)

    zero = jnp.zeros((nb, LANES), F32)
    lax.fori_loop(0, nc_tot, body, ((zero, zero), (zero, zero)), unroll=2)
    y = uf * dsk_ref[0]
    for d in range(2):
        y = y + _dot(u, m_scr[d]) + _dot(h_scr[d].astype(BF16), g_ref[d, 0])
    y_ref[0] = y


S5_TB = TM // S5_Q
S5_GPS = LANES // S5_CH


def _s5_pack(lo_ref, hi_ref, o_ref):
    for s in range(S5_Q):
        rows = pl.ds(s, S5_TB, stride=S5_Q)
        halves = (lo_ref[rows, :], hi_ref[rows, :])
        dst = S5_CH * (s % S5_GPS)
        for g in range(S5_GROUPS):
            slab = halves[g // S5_GPS]
            src = S5_CH * (g % S5_GPS)
            moved = slab if src == dst else pltpu.roll(slab, (dst - src) % LANES, 1)
            o_ref[g, :, s * S5_CH:(s + 1) * S5_CH] = moved[:, dst:dst + S5_CH]


def _s5_unpack(y_ref, o_ref):
    lane_grp = lax.broadcasted_iota(jnp.int32, (S5_TB, LANES), 1) // S5_CH
    for s in range(S5_Q):
        src = S5_CH * (s % S5_GPS)
        for half in range(S5_GROUPS // S5_GPS):
            acc = None
            for gl in range(S5_GPS):
                slab = y_ref[half * S5_GPS + gl, :, (s // S5_GPS) * LANES:(s // S5_GPS + 1) * LANES]
                dst = S5_CH * gl
                moved = slab if src == dst else pltpu.roll(slab, (dst - src) % LANES, 1)
                acc = moved if acc is None else jnp.where(lane_grp == gl, moved, acc)
            o_ref[half, pl.ds(s, S5_TB, stride=S5_Q), :] = acc


def _s5_params(lam_re, lam_im, log_dt, b_re, b_im, c_re, c_im, d_skip):
    q = S5_Q
    dt = jnp.exp(log_dt)[..., None]
    lr, li = lam_re, lam_im
    mag = jnp.exp(lr * dt)
    a_re = mag * jnp.cos(li * dt)
    a_im = mag * jnp.sin(li * dt)
    den = lr * lr + li * li
    f_re = ((a_re - 1.0) * lr + a_im * li) / den
    f_im = (a_im * lr - (a_re - 1.0) * li) / den
    bb_re = f_re[..., None] * b_re - f_im[..., None] * b_im
    bb_im = f_re[..., None] * b_im + f_im[..., None] * b_re
    kk = jnp.arange(q + 1, dtype=F32)[:, None, None, None]
    pmag = jnp.exp(kk * (lr * dt))
    pw_re = pmag * jnp.cos(kk * (li * dt))
    pw_im = pmag * jnp.sin(kk * (li * dt))
    lw_re = pw_re[:q].transpose(1, 2, 0, 3)[:, :, :, None, :]
    lw_im = pw_im[:q].transpose(1, 2, 0, 3)[:, :, :, None, :]
    ck_re = c_re[:, :, None] * lw_re - c_im[:, :, None] * lw_im
    ck_im = c_re[:, :, None] * lw_im + c_im[:, :, None] * lw_re
    ck = jnp.concatenate([ck_re, -ck_im], axis=-1).reshape(2, S5_GROUPS, S5_BLK, 2 * S5_STATE)
    kern_t = jnp.einsum("dgmp,dgpc->dgcm", ck, jnp.concatenate([bb_re, bb_im], axis=2), precision=HI)
    kern_t = kern_t.reshape(2, S5_GROUPS, S5_CH, q, S5_CH)
    zeros = jnp.zeros_like(kern_t)
    bbt_re = bb_re.transpose(0, 1, 3, 2)[:, :, None]
    bbt_im = bb_im.transpose(0, 1, 3, 2)[:, :, None]
    ct_re = c_re.transpose(0, 1, 3, 2)[:, :, :, None, :]
    ct_im = c_im.transpose(0, 1, 3, 2)[:, :, :, None, :]
    ms, ps, gs = [], [], []
    for d in range(2):
        ext = (jnp.concatenate([zeros[d], kern_t[d]], axis=2) if d == 0
               else jnp.concatenate([kern_t[d, :, :, ::-1], zeros[d]], axis=2))
        ext = ext.reshape(S5_GROUPS, S5_CH, 2 * S5_BLK)
        ms.append(ext)
        pidx = (q - 1 - jnp.arange(q)) if d == 0 else jnp.arange(q)
        pr = pw_re[pidx, d].transpose(1, 0, 2)[:, :, None, :]
        pi = pw_im[pidx, d].transpose(1, 0, 2)[:, :, None, :]
        p_re = pr * bbt_re[d] - pi * bbt_im[d]
        p_im = pr * bbt_im[d] + pi * bbt_re[d]
        pd = jnp.stack([jnp.concatenate([p_re, p_im], axis=3), jnp.concatenate([p_im, p_re], axis=3)])
        ps.append(pd.reshape(2, S5_GROUPS, S5_BLK, 2 * S5_STATE))
        gidx = (jnp.arange(q) + 1) if d == 0 else (q - jnp.arange(q))
        gw_re = pw_re[gidx, d].transpose(1, 2, 0)[..., None]
        gw_im = pw_im[gidx, d].transpose(1, 2, 0)[..., None]
        g_re = ct_re[d] * gw_re - ct_im[d] * gw_im
        g_im = ct_re[d] * gw_im + ct_im[d] * gw_re
        gs.append(jnp.concatenate([g_re, -g_im], axis=1).reshape(S5_GROUPS, 2 * S5_STATE, S5_BLK))
    ar = jnp.concatenate([pw_re[q], pw_re[q]], axis=-1)[:, :, None, :]
    ai = jnp.stack([jnp.concatenate([-pw_im[q], pw_im[q]], axis=-1),
                    jnp.concatenate([pw_im[q], -pw_im[q]], axis=-1)], axis=1)[:, :, :, None, :]
    dsk = jnp.tile(d_skip.reshape(S5_GROUPS, 1, S5_CH), (1, 1, q))
    return (jnp.stack(ms), jnp.stack(ps).astype(BF16), jnp.stack(gs).astype(BF16),
            ar.astype(F32), ai.astype(F32), dsk.astype(F32))


def _s5(ug, params, nb, s_len, lc):
    m, p, g, ar, ai, dsk = params
    nc_tot = s_len // S5_Q
    nc_ctx = lc // S5_Q
    r = nb * nc_tot
    return pl.pallas_call(
        functools.partial(_s5_kernel, nb=nb, nc_ctx=nc_ctx, nc_tot=nc_tot),
        out_shape=jax.ShapeDtypeStruct((S5_GROUPS, r, S5_BLK), F32),
        grid=(S5_GROUPS,),
        in_specs=[pl.BlockSpec((1, r, S5_BLK), lambda gi: (gi, 0, 0)),
                  pl.BlockSpec((2, 1, S5_CH, 2 * S5_BLK), lambda gi: (0, gi, 0, 0)),
                  pl.BlockSpec((2, 2, 1, S5_BLK, 2 * S5_STATE), lambda gi: (0, 0, gi, 0, 0)),
                  pl.BlockSpec((2, 1, 2 * S5_STATE, S5_BLK), lambda gi: (0, gi, 0, 0)),
                  pl.BlockSpec((2, 1, 1, 2 * S5_STATE), lambda gi: (0, gi, 0, 0)),
                  pl.BlockSpec((2, 2, 1, 1, 2 * S5_STATE), lambda gi: (0, 0, gi, 0, 0)),
                  pl.BlockSpec((1, 1, S5_BLK), lambda gi: (gi, 0, 0))],
        out_specs=pl.BlockSpec((1, r, S5_BLK), lambda gi: (gi, 0, 0)),
        scratch_shapes=[pltpu.VMEM((2, 2, r, 2 * S5_STATE), F32), pltpu.VMEM((2, r, 2 * S5_STATE), F32),
                        pltpu.VMEM((2, S5_BLK, S5_BLK), BF16)],
        compiler_params=_cp(("parallel",)),
        name="s5_scan",
    )(ug, m, p, g, ar, ai, dsk)


CONV_ROWS = 4 * TM


def _conv_kernel(x_ref, prev_ref, next_ref, w_ref, b_ref, x_o, bc_o, *, s_len, lc):
    x = x_ref[...]
    rows = x.shape[0]
    ridx = lax.broadcasted_iota(jnp.int32, x.shape, 0)
    pos = (pl.program_id(0) * rows) % s_len + ridx
    pos = jnp.where(pos >= s_len, pos - s_len, pos)
    seg_first = jnp.logical_or(pos == 0, pos == lc)
    seg_last = jnp.logical_or(pos == lc - 1, pos == s_len - 1)
    xm = jnp.where(ridx == 0, prev_ref[SUBLANES - 1:SUBLANES, :], pltpu.roll(x, 1, 0))
    xp = jnp.where(ridx == rows - 1, next_ref[0:1, :], pltpu.roll(x, rows - 1, 0))
    xm = jnp.where(seg_first, 0.0, xm)
    xp = jnp.where(seg_last, 0.0, xp)
    y = _silu(xm * w_ref[0:1, :] + x * w_ref[1:2, :] + xp * w_ref[2:3, :] + b_ref[...])
    x_o[...] = y[:, :GROUP_W]
    bc_o[...] = y[:, GROUP_W:].astype(BF16)


def _conv(xbc, w, b, s_len, lc):
    t, c = xbc.shape
    rows = next(r for r in (CONV_ROWS, CONV_ROWS // 2, TM) if t % r == 0)
    per = rows // SUBLANES
    last = t // SUBLANES - 1
    return pl.pallas_call(
        functools.partial(_conv_kernel, s_len=s_len, lc=lc),
        out_shape=[jax.ShapeDtypeStruct((t, GROUP_W), F32), jax.ShapeDtypeStruct((t, c - GROUP_W), BF16)],
        grid=(t // rows,),
        in_specs=[pl.BlockSpec((rows, c), lambda i: (i, 0)),
                  pl.BlockSpec((SUBLANES, c), lambda i: (jnp.maximum(i * per - 1, 0), 0)),
                  pl.BlockSpec((SUBLANES, c), lambda i: (jnp.minimum((i + 1) * per, last), 0)),
                  pl.BlockSpec((3, c), lambda i: (0, 0)),
                  pl.BlockSpec((1, c), lambda i: (0, 0))],
        out_specs=[pl.BlockSpec((rows, GROUP_W), lambda i: (i, 0)), pl.BlockSpec((rows, c - GROUP_W), lambda i: (i, 0))],
        compiler_params=_cp(("parallel",)),
        name="ssd_conv",
    )(xbc, xbc, xbc, w, b.reshape(1, c))


_X_B = 0
_X_C = SSD_NGROUPS * SSD_STATE


def _ssd_kernel(xf_ref, bcf_ref, dtf_ref, dttf_ref, xr_ref, bcr_ref, dtr_ref, dttr_ref,
                bias_ref, a_ref, biast_ref, at_ref, dsk_ref, yf_ref, yr_ref, stf_ref, str_ref):
    @pl.when(pl.program_id(1) == 0)
    def _():
        stf_ref[...] = jnp.zeros_like(stf_ref)
        str_ref[...] = jnp.zeros_like(str_ref)

    par = (bias_ref[...], a_ref[...], biast_ref[...], at_ref[...], dsk_ref[...])
    for j in range(SSD_SUB):
        rf = slice(j * TQ, (j + 1) * TQ)
        rr = slice((SSD_SUB - 1 - j) * TQ, (SSD_SUB - j) * TQ)
        for b in range(xf_ref.shape[0]):
            yf_ref[b, rf, :] = _ssd_chunk_step(xf_ref[b, rf, :], bcf_ref[b, rf, :], dtf_ref[b, rf, :],
                                               dttf_ref[b, :, rf], par, stf_ref.at[b], False)
            yr_ref[b, rr, :] = _ssd_chunk_step(xr_ref[b, rr, :], bcr_ref[b, rr, :], dtr_ref[b, rr, :],
                                               dttr_ref[b, :, rr], par, str_ref.at[b], True)


def _ssd_chunk_step(x, xc, dt_raw, dtt_raw, par, st_ref, rev):
    bias, a_vec, biast, at_vec, dsk = par
    base = SSD_HEADS if rev else 0
    dt = _softplus(dt_raw + bias)
    a = dt * a_vec
    dtt = _softplus(dtt_raw + biast)
    at = dtt * at_vec
    ri = lax.broadcasted_iota(jnp.int32, (TQ, TQ), 0)
    ci = lax.broadcasted_iota(jnp.int32, (TQ, TQ), 1)
    causal = (ci >= ri) if rev else (ri >= ci)
    tri = jnp.where(causal, 1.0, 0.0).astype(BF16)
    cum_c = sum(_dot(tri, piece) for piece in _bf16_pieces(a))
    cum_r = sum(_dot_nt(piece, tri) for piece in _bf16_pieces(at))
    edge = 0 if rev else TQ - 1
    tot = cum_c[edge:edge + 1, :]

    shape = (TQ, GROUP_W)
    xdt = x * _per_head_cols(dt, base, SSD_HEADS, shape)
    lane = lax.broadcasted_iota(jnp.int32, shape, 1)
    y = jnp.zeros(shape, F32)
    bmat = [xc[:, _X_B + g * SSD_STATE:_X_B + (g + 1) * SSD_STATE] for g in range(SSD_NGROUPS)]
    cmat = [xc[:, _X_C + g * SSD_STATE:_X_C + (g + 1) * SSD_STATE] for g in range(SSD_NGROUPS)]
    cb = [_dot_nt(cmat[g], bmat[g]) for g in range(SSD_NGROUPS)]
    for h in range(SSD_HEADS):
        col = base + h
        seg = jnp.where(causal, cum_c[:, col:col + 1] - cum_r[col:col + 1, :], NEG_INF)
        scores = cb[h // 2] * jnp.exp(seg)
        xh = jnp.where((lane >= h * HEAD_DIM) & (lane < (h + 1) * HEAD_DIM), xdt, 0.0)
        y = y + _dot(scores.astype(BF16), xh.astype(BF16))
    st = st_ref[...]
    yo = jnp.concatenate(
        [_dot_nt(cmat[g], st[g * SSD_STATE:(g + 1) * SSD_STATE].astype(BF16)) for g in range(SSD_NGROUPS)], axis=1)
    y = y + yo * _per_head_cols(jnp.exp(cum_c), base, SSD_HEADS, shape)
    if not rev:
        y = y + x * dsk
    xd = xdt * _per_head_cols(jnp.exp(tot - cum_c), base, SSD_HEADS, shape)
    xdt_t = xd.T.astype(BF16)
    decay = jnp.exp(tot)
    for g in range(SSD_NGROUPS):
        new = _dot(xdt_t[g * SSD_STATE:(g + 1) * SSD_STATE], bmat[g])
        for hh in range(2):
            h = 2 * g + hh
            r0 = h * HEAD_DIM
            st_ref[r0:r0 + HEAD_DIM, :] = (decay[:, base + h:base + h + 1] * st[r0:r0 + HEAD_DIM]
                                           + new[hh * HEAD_DIM:(hh + 1) * HEAD_DIM])
    return y


def _bf16_pieces(x):
    hi = x.astype(BF16)
    rest = x - hi.astype(F32)
    mid = rest.astype(BF16)
    return hi, mid, (rest - mid.astype(F32)).astype(BF16)


def _ssd_chunk(c, rev, nc_ctx, nc_tot):
    if not rev:
        return c
    return jnp.where(c < nc_ctx, nc_ctx - 1 - c, nc_tot - 1 - (c - nc_ctx))


SSD_SUB = TM // TQ
SSD_NB = 4


def _ssd_scan(xc, dt, dtt, bias, a, biast, at, dsk, nb, s_len, lc):
    x, bc = xc
    t = x.shape[0]
    nblk = s_len // TM
    nctx = lc // TM
    nbs = math.gcd(nb, SSD_NB)
    fix = lambda b, c: (0, 0)
    x3 = x.reshape(nb, s_len, GROUP_W)
    bc3 = bc.reshape(nb, s_len, SSD_XBC - GROUP_W)
    dt3 = dt.reshape(nb, s_len, LANES)

    def rows(rev):
        return lambda b, c: (b, _ssd_chunk(c, rev, nctx, nblk), 0)

    def lanes(rev):
        return lambda b, c: (b, 0, _ssd_chunk(c, rev, nctx, nblk))

    def data_specs(rev):
        return [pl.BlockSpec((nbs, TM, GROUP_W), rows(rev)), pl.BlockSpec((nbs, TM, SSD_XBC - GROUP_W), rows(rev)),
                pl.BlockSpec((nbs, TM, LANES), rows(rev)), pl.BlockSpec((nbs, SUBLANES, TM), lanes(rev))]

    state = pltpu.VMEM((nbs, SSD_HEADS * HEAD_DIM, SSD_STATE), F32)
    yf, yr = pl.pallas_call(
        _ssd_kernel,
        out_shape=[jax.ShapeDtypeStruct((nb, s_len, GROUP_W), F32)] * 2,
        grid=(nb // nbs, nblk),
        in_specs=data_specs(False) + data_specs(True) + [
            pl.BlockSpec((1, LANES), fix), pl.BlockSpec((1, LANES), fix),
            pl.BlockSpec((SUBLANES, TQ), fix), pl.BlockSpec((SUBLANES, TQ), fix),
            pl.BlockSpec((1, GROUP_W), fix)],
        out_specs=[pl.BlockSpec((nbs, TM, GROUP_W), rows(False)), pl.BlockSpec((nbs, TM, GROUP_W), rows(True))],
        scratch_shapes=[state, state],
        compiler_params=_cp(("parallel", "arbitrary")),
        name="ssd_scan",
    )(x3, bc3, dt3, dtt, x3, bc3, dt3, dtt, bias, a, biast, at, dsk)
    return yf.reshape(t, GROUP_W), yr.reshape(t, GROUP_W)


def _ssd(xbc, dt, conv_w, conv_b, dt_bias, a_log, d_skip, nb, s_len, lc):
    xc = _conv(xbc, conv_w, conv_b, s_len, lc)
    nd = 2 * SSD_HEADS
    dtt = dt[:, :nd].reshape(nb, s_len, nd).transpose(0, 2, 1)
    bias = jnp.pad(dt_bias.reshape(1, nd), ((0, 0), (0, LANES - nd)))
    a = jnp.pad(-jnp.exp(a_log).reshape(1, nd), ((0, 0), (0, LANES - nd)))
    biast = jnp.broadcast_to(dt_bias.reshape(nd, 1), (nd, TQ))
    at = jnp.broadcast_to(-jnp.exp(a_log).reshape(nd, 1), (nd, TQ))
    dsk = jnp.repeat(d_skip, HEAD_DIM).reshape(1, GROUP_W)
    return _ssd_scan(xc, dt, dtt, bias, a, biast, at, dsk, nb, s_len, lc)


def _outproj_kernel(x_ref, ys5_ref, oga_ref, y0_ref, y1_ref, z_ref, owa_ref, *refs):
    mods, shared = refs[:ROW_SUB], refs[ROW_SUB:]
    for s in range(ROW_SUB):
        rows = _row_views((x_ref, oga_ref, y0_ref, y1_ref, z_ref, owa_ref) + tuple(shared[-4:-1]), s)
        _outproj_block(rows[0], ys5_ref.at[:, pl.ds(s * S5_TB, S5_TB), :], *rows[1:6], mods[s], *shared[:-4],
                       *rows[6:], shared[-1])


def _outproj_block(x_ref, ys5_ref, oga_ref, y0_ref, y1_ref, z_ref, owa_ref, mod_ref, gluw_ref, glub_ref,
                   ng_ref, wout_ref, n2_ref, wr_ref, br_ref, xn_o, h2_o, route_o, y_scr):
    _s5_unpack(ys5_ref, y_scr)
    gl = _gelu_tanh(jnp.concatenate([y_scr[0], y_scr[1]], axis=1))
    a = gl * _sigmoid(_dot(gl.astype(BF16), gluw_ref[...]) + glub_ref[...])
    m = (y0_ref[...] + y1_ref[...]) * _silu(z_ref[...])
    m = m * lax.rsqrt(jnp.mean(m * m, axis=-1, keepdims=True) + EPS) * ng_ref[...]
    w = wout_ref
    mix = (_dot(a.astype(BF16), w[0:GROUP_W, :]) + _dot(oga_ref[...].astype(BF16), w[GROUP_W:2 * GROUP_W, :])
           + _dot(m.astype(BF16), w[2 * GROUP_W:3 * GROUP_W, :]) + _dot(owa_ref[...].astype(BF16), w[3 * GROUP_W:, :]))
    xn = x_ref[...] + mod_ref[0, 2:3, :] * mix
    xn_o[...] = xn
    h2 = xn * lax.rsqrt(jnp.mean(xn * xn, axis=-1, keepdims=True) + EPS) * n2_ref[...]
    h2 = h2 * (1.0 + mod_ref[0, 4:5, :]) + mod_ref[0, 3:4, :]
    h2_o[...] = _pack_bf16_pair(h2)
    h_hi = h2.astype(BF16)
    h_lo = (h2 - h_hi.astype(F32)).astype(BF16)
    logits = _dot(h_hi, wr_ref[0]) + (_dot(h_lo, wr_ref[0]) + _dot(h_hi, wr_ref[1])) + br_ref[...]
    lane = lax.broadcasted_iota(jnp.int32, logits.shape, 1).astype(F32)
    big = float(4 * LANES)
    lcoarse = jnp.where(lane < MOE_GROUPS, logits, NEG_INF)
    mx = jnp.max(lcoarse, axis=1, keepdims=True)
    den = jnp.sum(jnp.exp(lcoarse - mx), axis=1, keepdims=True)
    grp = jnp.min(jnp.where(lcoarse == mx, lane, big), axis=1, keepdims=True)
    pg = 1.0 / den
    lo = ROUTE_FINE0 + grp * MOE_PER_GROUP
    lf = jnp.where(lane >= lo, jnp.where(lane < lo + MOE_PER_GROUP, logits, NEG_INF), NEG_INF)
    v1 = jnp.max(lf, axis=1, keepdims=True)
    i1 = jnp.min(jnp.where(lf == v1, lane, big), axis=1, keepdims=True)
    lf2 = jnp.where(lane == i1, NEG_INF, lf)
    v2 = jnp.max(lf2, axis=1, keepdims=True)
    i2 = jnp.min(jnp.where(lf2 == v2, lane, big), axis=1, keepdims=True)
    e2 = jnp.exp(v2 - v1)
    w1 = pg / (1.0 + e2)
    w2 = w1 * e2
    route = jnp.where(lane == 0, i1 - ROUTE_FINE0,
                      jnp.where(lane == 1, i2 - ROUTE_FINE0,
                                jnp.where(lane == 2, w1, jnp.where(lane == 3, w2, 0.0))))
    route_o[...] = route


def _outproj(x, ys5, oga, y0, y1, z, owa, mod, glu_w, glu_b, ssd_norm_g, w_out, norm2_g, wr, br, nb, nblk):
    t, d = x.shape
    row = lambda i: (i, 0)
    fix = lambda i: (0, 0)
    step = ROW_SUB * TM
    assert t % step == 0
    gw = pl.BlockSpec((step, GROUP_W), row)
    wr_hi = wr.astype(BF16)
    mod_specs = [pl.BlockSpec((1, 6, d), lambda i, s=s: (_mod_row(ROW_SUB * i + s, nblk, nb), 0, 0))
                 for s in range(ROW_SUB)]
    return pl.pallas_call(
        _outproj_kernel,
        out_shape=[jax.ShapeDtypeStruct((t, d), F32), jax.ShapeDtypeStruct((t, d // 2), jnp.uint32),
                   jax.ShapeDtypeStruct((t, LANES), F32)],
        grid=(t // step,),
        in_specs=[pl.BlockSpec((step, d), row),
                  pl.BlockSpec((S5_GROUPS, ROW_SUB * S5_TB, S5_BLK), lambda i: (0, i, 0)),
                  gw, gw, gw, gw, gw] + mod_specs + [
                  pl.BlockSpec((GROUP_W, GROUP_W), fix),
                  pl.BlockSpec((1, GROUP_W), fix),
                  pl.BlockSpec((1, GROUP_W), fix),
                  pl.BlockSpec((d, d), fix),
                  pl.BlockSpec((1, d), fix),
                  pl.BlockSpec((2, d, LANES), lambda i: (0, 0, 0)),
                  pl.BlockSpec((1, LANES), fix)],
        out_specs=[pl.BlockSpec((step, d), row), pl.BlockSpec((step, d // 2), row), pl.BlockSpec((step, LANES), row)],
        scratch_shapes=[pltpu.VMEM((GROUP_W // LANES, TM, LANES), F32)],
        compiler_params=_cp(("parallel",)),
        name="out_proj_router",
    )(x, ys5, oga, y0, y1, z, owa, *([mod] * ROW_SUB), glu_w.astype(BF16), glu_b.reshape(1, -1), ssd_norm_g.reshape(1, -1),
      w_out.astype(BF16), norm2_g.reshape(1, -1), jnp.stack([wr_hi, (wr - wr_hi.astype(F32)).astype(BF16)]), br)


def _pack_router(coarse_w, coarse_b, fine_w, fine_b):
    def lanes(coarse, fine):
        gap = jnp.zeros(coarse.shape[:-1] + (ROUTE_FINE0 - MOE_GROUPS,), F32)
        tail = jnp.zeros(coarse.shape[:-1] + (LANES - ROUTE_FINE0 - N_EXPERTS,), F32)
        return jnp.concatenate([coarse, gap, fine, tail], axis=-1)

    return lanes(coarse_w, fine_w), lanes(coarse_b[None, :], fine_b[None, :])


def _gather_rows(src, idx):
    m = idx.shape[0]
    d = src.shape[1]
    workers = SC_CORES * SC_SUBCORES
    k = SC_FETCH_K
    nch = m // (workers * k)
    assert nch * workers * k == m
    mesh = plsc.VectorSubcoreMesh(core_axis_name="c", subcore_axis_name="s")

    @functools.partial(
        pl.kernel, mesh=mesh,
        out_type=jax.ShapeDtypeStruct((m, d), src.dtype),
        scratch_types=[pltpu.VMEM((nch, k), jnp.int32),
                       pltpu.VMEM((k, d), src.dtype),
                       pltpu.SemaphoreType.DMA],
    )
    def gather(src_hbm, idx_hbm, out_hbm, idx_v, rows_v, sem):
        wid = lax.axis_index("s") * SC_CORES + lax.axis_index("c")
        pltpu.sync_copy(idx_hbm.at[wid], idx_v)

        @pl.loop(0, nch)
        def _(j):
            off = pl.multiple_of((wid * nch + j) * k, k)
            pltpu.async_copy(src_hbm.at[idx_v.at[j]], rows_v, sem).wait()
            pltpu.sync_copy(rows_v, out_hbm.at[pl.ds(off, k)])

    return gather(src, idx.reshape(workers, nch, k))


def _scatter_rows(src, dst0, dst1, nrows):
    t, d = src.shape
    workers = SC_CORES * SC_SUBCORES
    nch = t // (workers * SC_GATHER_K)
    assert nch * workers * SC_GATHER_K == t
    mesh = plsc.VectorSubcoreMesh(core_axis_name="c", subcore_axis_name="s")

    @functools.partial(
        pl.kernel, mesh=mesh,
        out_type=jax.ShapeDtypeStruct((nrows, d), src.dtype),
        scratch_types=[pltpu.VMEM((nch, SC_GATHER_K), jnp.int32),
                       pltpu.VMEM((nch, SC_GATHER_K), jnp.int32),
                       pltpu.VMEM((SC_GATHER_K, d), src.dtype),
                       pltpu.SemaphoreType.DMA((2,))],
    )
    def scatter(src_hbm, d0_hbm, d1_hbm, out_hbm, i0_v, i1_v, rows_v, sem):
        wid = lax.axis_index("s") * SC_CORES + lax.axis_index("c")
        pltpu.sync_copy(d0_hbm.at[wid], i0_v)
        pltpu.sync_copy(d1_hbm.at[wid], i1_v)

        @pl.loop(0, nch)
        def _(j):
            off = pl.multiple_of((wid * nch + j) * SC_GATHER_K, SC_GATHER_K)
            pltpu.sync_copy(src_hbm.at[pl.ds(off, SC_GATHER_K)], rows_v)
            first = pltpu.async_copy(rows_v, out_hbm.at[i0_v.at[j]], sem.at[0])
            second = pltpu.async_copy(rows_v, out_hbm.at[i1_v.at[j]], sem.at[1])
            first.wait()
            second.wait()

    return scatter(src, dst0.reshape(workers, nch, SC_GATHER_K), dst1.reshape(workers, nch, SC_GATHER_K))


def _expert_kernel(be_ref, nused_ref, nvalid_ref, nxt_ref, slot_ref, x_ref, wg_hbm, wu_hbm, wd_hbm, o_ref,
                   wg_f, wu_f, wd_f, wg_s, wu_s, wd_s, sem, *, layer):
    i = pl.program_id(0)
    used = i < nused_ref[0]
    new_expert = jnp.logical_or(i == 0, be_ref[i] != be_ref[jnp.maximum(i - 1, 0)])

    def weight_copies(expert, slot):
        return [pltpu.make_async_copy(w.at[layer, expert], f.at[slot], sem.at[slot, j])
                for j, (w, f) in enumerate(((wg_hbm, wg_f), (wu_hbm, wu_f), (wd_hbm, wd_f)))]

    @pl.when(jnp.logical_and(used, new_expert))
    def _():
        slot = slot_ref[i]

        @pl.when(i == 0)
        def _():
            for c in weight_copies(be_ref[i], slot):
                c.start()

        for c in weight_copies(be_ref[i], slot):
            c.wait()
        wg_s[...] = wg_f[slot].astype(BF16)
        wu_s[...] = wu_f[slot].astype(BF16)
        wd_s[...] = wd_f[slot].astype(BF16)

        @pl.when(nxt_ref[i] >= 0)
        def _():
            for c in weight_copies(nxt_ref[i], 1 - slot):
                c.start()

    def swiglu(rows):
        row = rows.start + lax.broadcasted_iota(jnp.int32, (rows.stop - rows.start, x_ref.shape[1]), 0)
        lo, hi = _unpack_bf16_pair(jnp.where(row < nvalid_ref[i], x_ref[rows, :], jnp.uint32(0)))
        lo = lo.astype(BF16)
        hi = hi.astype(BF16)
        half = lo.shape[1]
        gate = _dot(lo, wg_s[0:half, :]) + _dot(hi, wg_s[half:, :])
        up = _dot(lo, wu_s[0:half, :]) + _dot(hi, wu_s[half:, :])
        o_ref[rows, :] = _pack_bf16_pair(_dot((_silu(gate) * up).astype(BF16), wd_s[...]))

    used = i < nused_ref[0]
    half_rows = MOE_TM // 2

    @pl.when(jnp.logical_and(used, nvalid_ref[i] > half_rows))
    def _():
        swiglu(slice(0, MOE_TM))

    @pl.when(jnp.logical_and(used, nvalid_ref[i] <= half_rows))
    def _():
        swiglu(slice(0, half_rows))
        o_ref[half_rows:, :] = jnp.zeros((MOE_TM - half_rows, o_ref.shape[1]), o_ref.dtype)


def _experts(xs, blk_e, n_used, n_valid, nxt_e, slot, wg, wu, wd, layer):
    rows, dp = xs.shape
    d = 2 * dp
    nblocks = rows // MOE_TM
    de = wg.shape[3]
    blk = lambda i, be, nu, *_: (jnp.minimum(i, nu[0] - 1), 0)
    hbm = pl.BlockSpec(memory_space=pl.ANY)
    grid_spec = pltpu.PrefetchScalarGridSpec(
        num_scalar_prefetch=5,
        grid=(nblocks,),
        in_specs=[pl.BlockSpec((MOE_TM, dp), blk), hbm, hbm, hbm],
        out_specs=pl.BlockSpec((MOE_TM, dp), blk),
        scratch_shapes=[pltpu.VMEM((2, d, de), F32), pltpu.VMEM((2, d, de), F32), pltpu.VMEM((2, de, d), F32),
                        pltpu.VMEM((d, de), BF16), pltpu.VMEM((d, de), BF16), pltpu.VMEM((de, d), BF16),
                        pltpu.SemaphoreType.DMA((2, 3))],
    )
    return pl.pallas_call(
        functools.partial(_expert_kernel, layer=layer),
        out_shape=jax.ShapeDtypeStruct((rows, dp), jnp.uint32),
        grid_spec=grid_spec,
        compiler_params=_cp(("arbitrary",)),
        name="moe_experts",
    )(blk_e, n_used, n_valid, nxt_e, slot, xs, wg, wu, wd)


def _final_kernel(*refs):
    fg_ref, o_ref = refs[-2:]
    for s in range(ROW_SUB):
        y = _moe_residual(*refs[5 * s:5 * s + 5])
        o_ref[s * TM:(s + 1) * TM, :] = y * lax.rsqrt(jnp.mean(y * y, axis=-1, keepdims=True) + EPS) * fg_ref[...]


def _final(xn, rows2, route, mod, final_g, nb, nblk):
    t, d = xn.shape
    nlat = nblk - 1
    assert (nb * nlat) % ROW_SUB == 0

    def blk_specs(s):
        lat = lambda i: ROW_SUB * i + s
        src = lambda i: ((lat(i) // nlat) * nblk + 1 + lat(i) % nlat, 0)
        return [pl.BlockSpec((TM, d), src),
                pl.BlockSpec((TM, d // 2), src),
                pl.BlockSpec((TM, d // 2), lambda i: (src(i)[0] + t // TM, 0)),
                pl.BlockSpec((TM, LANES), src),
                pl.BlockSpec((1, 6, d), lambda i: (lat(i) // nlat, 0, 0))]

    return pl.pallas_call(
        _final_kernel,
        out_shape=jax.ShapeDtypeStruct((nb * nlat * TM, d), F32),
        grid=(nb * nlat // ROW_SUB,),
        in_specs=[sp for s in range(ROW_SUB) for sp in blk_specs(s)] + [pl.BlockSpec((1, d), lambda i: (0, 0))],
        out_specs=pl.BlockSpec((ROW_SUB * TM, d), lambda i: (i, 0)),
        compiler_params=_cp(("parallel",)),
        name="moe_combine_final",
    )(*((xn, rows2, rows2, route, mod) * ROW_SUB), final_g.reshape(1, d))


def _moe(h2, route, wg, wu, wd, layer):
    t, d = h2.shape
    n_slots = 2 * t
    experts = jnp.arange(N_EXPERTS, dtype=F32)[None, :]
    oh0 = (route[:, 0:1] == experts).astype(F32)
    oh1 = (route[:, 1:2] == experts).astype(F32)
    both = (oh0 + oh1).reshape(t // LANES, LANES, N_EXPERTS)
    tri = jnp.tril(jnp.ones((LANES, LANES), F32))
    intra = jnp.einsum("ij,bjk->bik", tri, both)
    blk_tot = intra[:, -1, :]
    blk_cum = jnp.cumsum(blk_tot, axis=0)
    earlier = (intra - both + (blk_cum - blk_tot)[:, None, :]).reshape(t, N_EXPERTS)
    counts = blk_cum[-1].astype(jnp.int32)
    pcounts = (counts + MOE_TM - 1) // MOE_TM * MOE_TM
    pends = jnp.cumsum(pcounts)
    pstarts = pends - pcounts
    base = pstarts.astype(F32)[None, :] + earlier
    dest0 = jnp.sum(oh0 * base, axis=1).astype(jnp.int32)
    dest1 = jnp.sum(oh1 * base, axis=1).astype(jnp.int32)
    nblocks = -(-n_slots // MOE_TM) + N_EXPERTS
    nrows = nblocks * MOE_TM
    blk_start = jnp.arange(nblocks, dtype=jnp.int32) * MOE_TM
    blk_e = jnp.minimum(jnp.sum((pends[None, :] <= blk_start[:, None]).astype(jnp.int32), axis=1), N_EXPERTS - 1)
    n_used = (pends[-1] // MOE_TM).astype(jnp.int32).reshape(1)
    n_valid = jnp.clip((pstarts + counts)[blk_e] - blk_start, 0, MOE_TM).astype(jnp.int32)
    ids = jnp.arange(N_EXPERTS, dtype=jnp.int32)
    has = counts > 0
    later = lax.cummin(jnp.where(has, ids, N_EXPERTS)[::-1])[::-1]
    nxt = jnp.concatenate([later[1:], jnp.full((1,), N_EXPERTS, jnp.int32)])
    nxt = jnp.where(nxt >= N_EXPERTS, -1, nxt)
    slot = (jnp.cumsum(has.astype(jnp.int32)) - 1) % 2
    xs = _scatter_rows(h2, dest0, dest1, nrows)
    ys = _experts(xs, blk_e, n_used, n_valid, nxt[blk_e], slot[blk_e], wg, wu, wd, layer)
    return _gather_rows(ys, jnp.concatenate([dest0, dest1]))


def kernel(x, c, ctx, c_ctx, ada_w, ada_b, norm1_g, norm2_g, w_in, w_out, s5_lam_re, s5_lam_im, s5_log_dt, s5_b_re, s5_b_im, s5_c_re, s5_c_im, s5_d, s5_glu_w, s5_glu_b, ga_qn_g, ga_kn_g, ssd_conv_w, ssd_conv_b, ssd_dt_bias, ssd_a_log, ssd_d, ssd_norm_g, wa_sink, moe_coarse_w, moe_coarse_b, moe_fine_w, moe_fine_b, moe_w_gate, moe_w_up, moe_w_down, final_g):
    nb, l, d = x.shape
    lc = ctx.shape[1]
    depth = ada_w.shape[0]
    assert lc == TM and l % TM == 0 and nb <= SUBLANES - 1 and d == D_MODEL
    s_len = lc + l
    nblk = s_len // TM
    t = nb * s_len

    cc = jnp.zeros((SUBLANES, d), F32).at[:nb].set(c).at[nb].set(c_ctx)
    mods = _ada(cc, ada_w, ada_b).reshape(depth, SUBLANES, 6, d)
    cos_t, sin_t = _rope_tables(lc, l)
    w_packed = jax.vmap(_pack_w_in)(w_in)
    s5_tabs = jax.vmap(_s5_params)(s5_lam_re, s5_lam_im, s5_log_dt, s5_b_re, s5_b_im, s5_c_re, s5_c_im, s5_d)
    wrs, brs = jax.vmap(_pack_router)(moe_coarse_w, moe_coarse_b, moe_fine_w, moe_fine_b)

    src = ("first", x.reshape(nb * l, d), ctx.reshape(nb * lc, d))
    for i in range(depth):
        mod = mods[i]
        (xm, xbc, ug, z, dt, gaq, gak, gav, waq, wak, wav) = _inproj(
            src, mod, norm1_g[i], w_packed[i], cos_t, sin_t, ga_qn_g[i], ga_kn_g[i], nb, nblk)
        ys5 = _s5(ug, tuple(tab[i] for tab in s5_tabs), nb, s_len, lc)
        oga, owa = _attn(wa_sink[i], gaq, gak, gav, waq, wak, wav, nb, s_len, lc)
        y0, y1 = _ssd(xbc, dt, ssd_conv_w[i], ssd_conv_b[i], ssd_dt_bias[i], ssd_a_log[i], ssd_d[i], nb, s_len, lc)
        wr, br = wrs[i], brs[i]
        xn, h2, route = _outproj(xm, ys5, oga, y0, y1, z, owa, mod, s5_glu_w[i], s5_glu_b[i], ssd_norm_g[i],
                                 w_out[i], norm2_g[i], wr, br, nb, nblk)
        rows2 = _moe(h2, route, moe_w_gate, moe_w_up, moe_w_down, i)
        src = ("moe", xn, rows2, route, mod)
    return _final(xn, rows2, route, mod, final_g, nb, nblk).reshape(nb, l, d)
```

```python
import functools
import math

import jax
import jax.numpy as jnp
import numpy as np
from jax import lax
from jax.experimental import pallas as pl
from jax.experimental.pallas import tpu as pltpu
from jax.experimental.pallas import tpu_sc as plsc

F32 = jnp.float32
BF16 = jnp.bfloat16
HI = lax.Precision.HIGHEST

D_MODEL = 1024
GRID_W = 64
GROUP_W = 256
HEAD_DIM = 64
ROPE_FREQS = HEAD_DIM // 4
ROPE_BASE = 10000.0
EPS = 1e-6
S5_CH = 16
S5_GROUPS = GROUP_W // S5_CH
S5_STATE = 64
N_HEADS = 4
SSD_HEADS = 4
SSD_NGROUPS = 2
SSD_STATE = 128
SSD_XBC = GROUP_W + 2 * SSD_NGROUPS * SSD_STATE
WINDOW = 128
MOE_GROUPS = 4
MOE_PER_GROUP = 8
N_EXPERTS = 32

LANES = 128
SUBLANES = 8
TM = 256
TQ = 128
GA_TQ = 128
GA_SUB = 2
S5_Q = 32
S5_BLK = S5_Q * S5_CH
MOE_TM = 512
SC_CORES = 2
SC_SUBCORES = 16
SC_GATHER_K = 32
SC_FETCH_K = 64
ROUTE_FINE0 = 32
VMEM_LIMIT = 56 * 1024 * 1024

NEG_INF = float("-inf")
LOG2E = math.log2(math.e)


def _cp(sem, vmem=VMEM_LIMIT):
    return pltpu.CompilerParams(dimension_semantics=sem, vmem_limit_bytes=vmem)


def _dot(a, b):
    return jnp.dot(a, b, preferred_element_type=F32)


def _dot_hi(a, b):
    return jnp.dot(a, b, preferred_element_type=F32, precision=HI)


def _dot_nt(a, b):
    return lax.dot_general(a, b, (((1,), (1,)), ((), ())), preferred_element_type=F32)


def _sigmoid(x):
    return 1.0 / (1.0 + jnp.exp(-x))


def _silu(x):
    return x * _sigmoid(x)


def _gelu_tanh(x):
    return 0.5 * x * (1.0 + jnp.tanh(math.sqrt(2.0 / math.pi) * (x + 0.044715 * (x * x * x))))


def _softplus(x):
    return jnp.maximum(x, 0.0) + jnp.log(1.0 + jnp.exp(-jnp.abs(x)))


_HI16 = 0xFFFF0000


def _pack_bf16_pair(x):
    n = x.shape[1] // 2
    bits = pltpu.bitcast(x.astype(BF16).astype(F32), jnp.uint32)
    return (bits[:, n:] & jnp.uint32(_HI16)) | (bits[:, :n] >> 16)


def _unpack_bf16_pair(w):
    return pltpu.bitcast(w << 16, F32), pltpu.bitcast(w & jnp.uint32(_HI16), F32)


def _per_head_cols(v, base, n_heads, shape):
    lane = lax.broadcasted_iota(jnp.int32, shape, 1)
    out = jnp.broadcast_to(v[:, base + n_heads - 1:base + n_heads], shape)
    for h in range(n_heads - 2, -1, -1):
        out = jnp.where(lane < (h + 1) * HEAD_DIM, v[:, base + h:base + h + 1], out)
    return out


def _ada_kernel(c_ref, w_ref, b_ref, o_ref):
    c = c_ref[...]
    o_ref[0] = _dot_hi(_silu(c), w_ref[0]) + b_ref[0]


def _ada(cc, ada_w, ada_b):
    depth, d, n = ada_w.shape
    tn = 1536
    return pl.pallas_call(
        _ada_kernel,
        out_shape=jax.ShapeDtypeStruct((depth, SUBLANES, n), F32),
        grid=(depth, n // tn),
        in_specs=[pl.BlockSpec((SUBLANES, d), lambda l, j: (0, 0)),
                  pl.BlockSpec((1, d, tn), lambda l, j: (l, 0, j)),
                  pl.BlockSpec((1, 1, tn), lambda l, j: (l, 0, j))],
        out_specs=pl.BlockSpec((1, SUBLANES, tn), lambda l, j: (l, 0, j)),
        compiler_params=_cp(("parallel", "parallel")),
        name="ada_mod",
    )(cc, ada_w, ada_b.reshape(depth, 1, n))


_C_XBC = 0
_C_U = _C_XBC + SSD_XBC
_C_Z = _C_U + GROUP_W
_C_DT = _C_Z + GROUP_W
_C_GAQ = _C_DT + LANES
_C_WAQ = _C_GAQ + N_HEADS * LANES
_C_GAK = _C_WAQ + N_HEADS * LANES
_C_GAV = _C_GAK + LANES
_C_WAK = _C_GAV + LANES
_C_WAV = _C_WAK + LANES
_C_END = _C_WAV + LANES


def _expand_q_cols(wq):
    zero = jnp.zeros((wq.shape[0], HEAD_DIM), wq.dtype)
    parts = []
    for h in range(N_HEADS):
        head = wq[:, h * HEAD_DIM:(h + 1) * HEAD_DIM]
        parts += [head, zero] if h // 2 == 0 else [zero, head]
    return jnp.concatenate(parts, axis=1)


def _pack_w_in(w_in):
    cuts = np.cumsum([256, 256, 128, 128, 256, SSD_XBC, 2 * SSD_HEADS, 256, 128, 128])[:-1]
    u, gaq, gak, gav, z, xbc, dt, waq, wak, wav = jnp.split(w_in, [int(c) for c in cuts], axis=1)
    dt = jnp.pad(dt, ((0, 0), (0, LANES - dt.shape[1])))
    w = jnp.concatenate([xbc, u, z, dt, _expand_q_cols(gaq), _expand_q_cols(waq), gak, gav, wak, wav], axis=1)
    return w.astype(BF16)


def _rope(x, cos, sins):
    w = x.shape[1]
    if w > LANES:
        cos = jnp.concatenate([cos] * (w // LANES), axis=1)
        sins = jnp.concatenate([sins] * (w // LANES), axis=1)
    lane = lax.broadcasted_iota(jnp.int32, x.shape, 1)
    up = pltpu.roll(x, w - ROPE_FREQS, 1)
    dn = pltpu.roll(x, ROPE_FREQS, 1)
    partner = jnp.where((lane & ROPE_FREQS) == 0, up, dn)
    return x * cos + partner * sins


def _v_with_ones(v):
    lo = lax.broadcasted_iota(jnp.int32, v.shape, 1) < HEAD_DIM
    return jnp.concatenate([jnp.where(lo, v, 1.0), jnp.where(lo, 1.0, v)], axis=1).astype(BF16)


def _moe_residual(xn_ref, r0_ref, r1_ref, route_ref, mod_ref):
    route = route_ref[...]
    r0 = jnp.concatenate(_unpack_bf16_pair(r0_ref[...]), axis=1)
    r1 = jnp.concatenate(_unpack_bf16_pair(r1_ref[...]), axis=1)
    return xn_ref[...] + mod_ref[0, 5:6, :] * (route[:, 2:3] * r0 + route[:, 3:4] * r1)


ROW_SUB = 4


def _row_views(refs, s):
    return [r.at[pl.ds(s * TM, TM), :] for r in refs]


def _inproj_kernel(*refs, first, nblk):
    n_blk_in = (2 if first else 5) + 3
    shared = refs[ROW_SUB * n_blk_in:]
    g_ref, w_ref, qn_ref, kn_ref = shared[:4]
    xm_o, xbc_o, ug_o = shared[4:7]
    rest_o = shared[7:-1]
    u_scr = shared[-1]
    for s in range(ROW_SUB):
        blk_refs = refs[s * n_blk_in:(s + 1) * n_blk_in]
        xm_v, xbc_v = _row_views((xm_o, xbc_o), s)
        ug_v = ug_o.at[:, pl.ds(s * S5_TB, S5_TB), :]
        _inproj_block(blk_refs, g_ref, w_ref, qn_ref, kn_ref, xm_v, xbc_v, ug_v, _row_views(rest_o, s), u_scr,
                      first, (pl.program_id(0) * ROW_SUB + s) % nblk == 0)


def _inproj_block(blk_refs, g_ref, w_ref, qn_ref, kn_ref, xm_o, xbc_o, ug_o, rest_o, u_scr, first, is_ctx):
    if first:
        lat_ref, ctx_ref = blk_refs[:2]
        x = jnp.where(is_ctx, ctx_ref[...], lat_ref[...])
    else:
        x = _moe_residual(*blk_refs[:5])
    mod_ref, cos_ref, sin_ref = blk_refs[-3:]
    z_o, dt_o, gaq_o, gak_o, gav_o, waq_o, wak_o, wav_o = rest_o
    xm_o[...] = x
    ms = jnp.mean(x * x, axis=-1, keepdims=True)
    xn = x * lax.rsqrt(ms + EPS) * g_ref[...]
    h = xn * (1.0 + mod_ref[0, 1:2, :]) + mod_ref[0, 0:1, :]
    hb = h.astype(BF16)

    def proj(lo, hi):
        return _dot(hb, w_ref[:, lo:hi])

    cos = cos_ref[...]
    sins = sin_ref[...]
    scale = LOG2E * HEAD_DIM ** -0.5
    q = proj(_C_GAQ, _C_WAQ)
    qs = q * q
    inv = jnp.concatenate(
        [jnp.broadcast_to(lax.rsqrt(jnp.sum(qs[:, s * LANES:(s + 1) * LANES], axis=1, keepdims=True)
                                    * (1.0 / HEAD_DIM) + EPS), (q.shape[0], LANES)) for s in range(N_HEADS)], axis=1)
    gaq_o[...] = (_rope(q * inv * qn_ref[...], cos, sins) * scale).astype(BF16)
    waq_o[...] = (_rope(proj(_C_WAQ, _C_GAK), cos, sins) * scale).astype(BF16)
    k = proj(_C_GAK, _C_GAV)
    ks = k * k
    lane = lax.broadcasted_iota(jnp.int32, k.shape, 1)
    lo = lane < HEAD_DIM
    ms0 = jnp.sum(jnp.where(lo, ks, 0.0), axis=1, keepdims=True)
    ms1 = jnp.sum(jnp.where(lo, 0.0, ks), axis=1, keepdims=True)
    kinv = lax.rsqrt(jnp.where(lo, ms0, ms1) * (1.0 / HEAD_DIM) + EPS)
    gak_o[...] = _rope(k * kinv * kn_ref[...], cos, sins).astype(BF16)
    gav_o[...] = _v_with_ones(proj(_C_GAV, _C_WAK))
    wak_o[...] = _rope(proj(_C_WAK, _C_WAV), cos, sins).astype(BF16)
    wav_o[...] = _v_with_ones(proj(_C_WAV, _C_END))
    xbc_o[...] = proj(_C_XBC, _C_U)
    u = proj(_C_U, _C_Z)
    u_scr[0] = u[:, :LANES]
    u_scr[1] = u[:, LANES:]
    _s5_pack(u_scr.at[0], u_scr.at[1], ug_o)
    z_o[...] = proj(_C_Z, _C_DT)
    dt_o[...] = proj(_C_DT, _C_GAQ)


def _mod_row(i, nblk, nb):
    return jnp.where(i % nblk == 0, nb, i // nblk)


def _inproj(src, mod, norm_g, w_packed, cos_t, sin_t, qn_g, kn_g, nb, nblk):
    first = src[0] == "first"
    d = src[1].shape[1]
    t = nb * nblk * TM
    row = lambda i: (i, 0)
    fix = lambda i: (0, 0)
    nsteps = t // (ROW_SUB * TM)
    assert nsteps * ROW_SUB * TM == t

    def blk_specs(s):
        bid = lambda i: ROW_SUB * i + s
        modspec = pl.BlockSpec((1, 6, d), lambda i: (_mod_row(bid(i), nblk, nb), 0, 0))
        table = pl.BlockSpec((TM, LANES), lambda i: (bid(i) % nblk, 0))
        if first:
            srcs = [pl.BlockSpec((TM, d), lambda i: ((bid(i) // nblk) * (nblk - 1) + jnp.maximum(bid(i) % nblk - 1, 0), 0)),
                    pl.BlockSpec((TM, d), lambda i: (bid(i) // nblk, 0))]
        else:
            srcs = [pl.BlockSpec((TM, d), lambda i: (bid(i), 0)), pl.BlockSpec((TM, d // 2), lambda i: (bid(i), 0)),
                    pl.BlockSpec((TM, d // 2), lambda i: (bid(i) + t // TM, 0)),
                    pl.BlockSpec((TM, LANES), lambda i: (bid(i), 0)), modspec]
        return srcs + [modspec, table, table]

    if first:
        blk_args = tuple(src[1:]) + (mod, cos_t, sin_t)
    else:
        blk_args = (src[1], src[2], src[2], src[3], src[4], mod, cos_t, sin_t)
    outs = [(d, F32), (SSD_XBC, F32), None, (GROUP_W, F32), (LANES, F32),
            (N_HEADS * LANES, BF16), (LANES, BF16), (2 * LANES, BF16),
            (N_HEADS * LANES, BF16), (LANES, BF16), (2 * LANES, BF16)]
    shapes = [jax.ShapeDtypeStruct((t, o[0]), o[1]) if o else
              jax.ShapeDtypeStruct((S5_GROUPS, t // S5_Q, S5_BLK), F32) for o in outs]
    specs = [pl.BlockSpec((ROW_SUB * TM, o[0]), row) if o else
             pl.BlockSpec((S5_GROUPS, ROW_SUB * S5_TB, S5_BLK), lambda i: (0, i, 0)) for o in outs]
    return pl.pallas_call(
        functools.partial(_inproj_kernel, first=first, nblk=nblk),
        out_shape=shapes,
        grid=(nsteps,),
        in_specs=[sp for s in range(ROW_SUB) for sp in blk_specs(s)] + [
                  pl.BlockSpec((1, d), fix),
                  pl.BlockSpec((d, _C_END), fix),
                  pl.BlockSpec((1, N_HEADS * LANES), fix),
                  pl.BlockSpec((1, LANES), fix)],
        out_specs=specs,
        scratch_shapes=[pltpu.VMEM((GROUP_W // LANES, TM, LANES), F32)],
        compiler_params=_cp(("parallel",)),
        name="in_proj",
    )(*(blk_args * ROW_SUB), norm_g.reshape(1, d), w_packed,
      jnp.tile(qn_g, 2 * N_HEADS).reshape(1, -1), jnp.tile(kn_g, 2).reshape(1, -1))


def _rope_tables(lc, l):
    n_rows = l // GRID_W
    rows = np.repeat(np.arange(n_rows), GRID_W)
    cols = np.tile(np.arange(GRID_W), n_rows)
    inv = np.power(np.float32(ROPE_BASE), -np.arange(ROPE_FREQS, dtype=np.float32) / ROPE_FREQS)
    ang = np.stack([rows, cols], axis=-1).astype(np.float32)[..., None] * inv
    cos = np.cos(ang)
    sin = np.sin(ang)
    cos64 = np.stack([cos, cos], axis=2).reshape(l, HEAD_DIM)
    sin64 = np.stack([-sin, sin], axis=2).reshape(l, HEAD_DIM)
    cos64 = np.concatenate([np.ones((lc, HEAD_DIM), np.float32), cos64], axis=0)
    sin64 = np.concatenate([np.zeros((lc, HEAD_DIM), np.float32), sin64], axis=0)
    return (jnp.asarray(np.tile(cos64, (1, 2)), dtype=F32), jnp.asarray(np.tile(sin64, (1, 2)), dtype=F32))


def _merge_heads(o2, kvh):
    tq = o2.shape[0] // 2
    oa, ob = o2[:tq], o2[tq:]
    lane = lax.broadcasted_iota(jnp.int32, oa.shape, 1)
    if kvh == 0:
        return jnp.where(lane < HEAD_DIM, oa, pltpu.roll(ob, HEAD_DIM, 1))
    return jnp.where(lane < HEAD_DIM, pltpu.roll(oa, HEAD_DIM, 1), ob)


def _stack_q(q_ref, rows, kvh):
    return jnp.concatenate([q_ref[rows, (2 * kvh) * LANES:(2 * kvh + 1) * LANES],
                            q_ref[rows, (2 * kvh + 1) * LANES:(2 * kvh + 2) * LANES]], axis=0)


def _ga_attend(q_ref, k_ref, v_ref, o_ref, nkeys):
    k = k_ref[0:nkeys, :]
    for sub in range(GA_SUB):
        rows = slice(sub * GA_TQ, (sub + 1) * GA_TQ)
        scores = [_dot_nt(_stack_q(q_ref, rows, kvh), k) for kvh in range(2)]
        outs = []
        for kvh in range(2):
            s = scores[kvh]
            p = jnp.exp2((s - jnp.max(s, axis=1, keepdims=True)).astype(BF16))
            o2 = _dot(p, v_ref[0:nkeys, kvh * LANES:(kvh + 1) * LANES])
            outs.append(_merge_heads(o2 / pltpu.roll(o2, HEAD_DIM, 1), kvh))
        o_ref[rows, :] = jnp.concatenate(outs, axis=1).astype(o_ref.dtype)


def _attn_kernel(sink_ref, gq_ref, gk_ref, gv_ref, wq_ref, wk_ref, wv_ref, og_ref, ow_ref, *, lc):
    is_ctx = pl.program_id(1) < lc // TM

    @pl.when(is_ctx)
    def _():
        _ga_attend(gq_ref, gk_ref, gv_ref, og_ref, lc)
        _wa_attend(sink_ref, wq_ref, wk_ref, wv_ref, ow_ref, lc)

    @pl.when(jnp.logical_not(is_ctx))
    def _():
        _ga_attend(gq_ref, gk_ref, gv_ref, og_ref, gk_ref.shape[0])
        _wa_attend(sink_ref, wq_ref, wk_ref, wv_ref, ow_ref, lc)


def _attn(sink, gq, gk, gv, wq, wk, wv, nb, s_len, lc):
    t = gq.shape[0]
    nq = s_len // TM
    assert GA_SUB * GA_TQ == TM and WA_SUB * TQ == TM
    qspec = pl.BlockSpec((TM, N_HEADS * LANES), lambda b, j: (b * nq + j, 0))
    kspec = pl.BlockSpec((s_len, LANES), lambda b, j: (b, 0))
    vspec = pl.BlockSpec((s_len, 2 * LANES), lambda b, j: (b, 0))
    ospec = pl.BlockSpec((TM, GROUP_W), lambda b, j: (b * nq + j, 0))
    return pl.pallas_call(
        functools.partial(_attn_kernel, lc=lc),
        out_shape=[jax.ShapeDtypeStruct((t, GROUP_W), BF16)] * 2,
        grid=(nb, nq),
        in_specs=[pl.BlockSpec(memory_space=pltpu.SMEM), qspec, kspec, vspec, qspec, kspec, vspec],
        out_specs=[ospec, ospec],
        compiler_params=_cp(("parallel", "arbitrary")),
        name="attention",
    )(sink, gq, gk, gv, wq, wk, wv)


WA_SUB = TM // TQ


def _wa_attend(sink_ref, q_ref, k_ref, v_ref, o_ref, lc):
    s_len = k_ref.shape[0]
    kc = k_ref[0:lc, :]
    row = lax.broadcasted_iota(jnp.int32, (2 * TQ, 1), 0)
    for sub in range(WA_SUB):
        rows = slice(sub * TQ, (sub + 1) * TQ)
        n = pl.program_id(1) * WA_SUB + sub - lc // TQ
        start = pl.multiple_of(jnp.clip(lc + (n - 1) * TQ, lc, s_len - 3 * TQ), TQ)
        kb = k_ref[pl.ds(start, 3 * TQ), :]
        qpos = n * TQ + lax.broadcasted_iota(jnp.int32, (TQ, 3 * TQ), 0)
        kpos = (start - lc) + lax.broadcasted_iota(jnp.int32, (TQ, 3 * TQ), 1)
        reach = jnp.where(n >= 0, WINDOW, -1)
        valid = jnp.abs(qpos - kpos) <= reach
        valid = jnp.concatenate([valid, valid], axis=0)
        outs = []
        for kvh in range(2):
            q2 = jnp.concatenate([q_ref[rows, (2 * kvh) * LANES:(2 * kvh + 1) * LANES],
                                  q_ref[rows, (2 * kvh + 1) * LANES:(2 * kvh + 2) * LANES]], axis=0)
            sc = _dot_nt(q2, kc)
            sb = jnp.where(valid, _dot_nt(q2, kb), NEG_INF)
            sink = jnp.where(row < TQ, sink_ref[2 * kvh], sink_ref[2 * kvh + 1]) * LOG2E
            m = jnp.maximum(jnp.maximum(jnp.max(sc, axis=1, keepdims=True), jnp.max(sb, axis=1, keepdims=True)), sink)
            pc = jnp.exp2((sc - m).astype(BF16))
            pb = jnp.exp2((sb - m).astype(BF16))
            vcols = slice(kvh * LANES, (kvh + 1) * LANES)
            o2 = _dot(pc, v_ref[0:lc, vcols]) + _dot(pb, v_ref[pl.ds(start, 3 * TQ), vcols])
            denom = pltpu.roll(o2, HEAD_DIM, 1) + jnp.exp2(sink - m)
            outs.append(_merge_heads(o2 / denom, kvh))
        o_ref[rows, :] = jnp.concatenate(outs, axis=1).astype(o_ref.dtype)


def _s5_chunk_index(t, rev, nc_ctx, nc_tot):
    if not rev:
        return t
    return jnp.where(t < nc_ctx, nc_ctx - 1 - t, nc_tot - 1 - (t - nc_ctx))


def _s5_kernel(u_ref, k_ref, p_ref, g_ref, ar_ref, ai_ref, dsk_ref, y_ref, s_scr, h_scr, m_scr, *, nb, nc_ctx, nc_tot):
    for d in range(2):
        ext = k_ref[d, 0]
        for s in range(S5_Q):
            lo = ((S5_Q - s) if d == 0 else (S5_Q - 1 - s)) * S5_CH
            win = pltpu.roll(ext, (2 * S5_BLK - lo) % (2 * S5_BLK), 1)[:, :S5_BLK]
            m_scr[d, s * S5_CH:(s + 1) * S5_CH, :] = win.astype(BF16)
    uf = u_ref[0]
    u = uf.astype(BF16)
    for d in range(2):
        for k in range(2):
            s_scr[d, k] = _dot(u, p_ref[d, k, 0])
    ar = [jnp.broadcast_to(ar_ref[d, 0], (nb, LANES)) for d in range(2)]
    ai = [[jnp.broadcast_to(ai_ref[d, k, 0], (nb, LANES)) for k in range(2)] for d in range(2)]

    def body(t, carry):
        out = []
        for d in range(2):
            h, hs = carry[d]
            rows = pl.ds(_s5_chunk_index(t, d == 1, nc_ctx, nc_tot), nb, stride=nc_tot)
            h_scr[d, rows, :] = h
            out.append((ar[d] * h + ai[d][0] * hs + s_scr[d, 0, rows, :],
                        ar[d] * hs + ai[d][1] * h + s_scr[d, 1, rows, :]))
        return tuple(out)

    zero = jnp.zeros((nb, LANES), F32)
    lax.fori_loop(0, nc_tot, body, ((zero, zero), (zero, zero)), unroll=2)
    y = uf * dsk_ref[0]
    for d in range(2):
        y = y + _dot(u, m_scr[d]) + _dot(h_scr[d].astype(BF16), g_ref[d, 0])
    y_ref[0] = y


S5_TB = TM // S5_Q
S5_GPS = LANES // S5_CH


def _s5_pack(lo_ref, hi_ref, o_ref):
    for s in range(S5_Q):
        rows = pl.ds(s, S5_TB, stride=S5_Q)
        halves = (lo_ref[rows, :], hi_ref[rows, :])
        dst = S5_CH * (s % S5_GPS)
        for g in range(S5_GROUPS):
            slab = halves[g // S5_GPS]
            src = S5_CH * (g % S5_GPS)
            moved = slab if src == dst else pltpu.roll(slab, (dst - src) % LANES, 1)
            o_ref[g, :, s * S5_CH:(s + 1) * S5_CH] = moved[:, dst:dst + S5_CH]


def _s5_unpack(y_ref, o_ref):
    lane_grp = lax.broadcasted_iota(jnp.int32, (S5_TB, LANES), 1) // S5_CH
    for s in range(S5_Q):
        src = S5_CH * (s % S5_GPS)
        for half in range(S5_GROUPS // S5_GPS):
            acc = None
            for gl in range(S5_GPS):
                slab = y_ref[half * S5_GPS + gl, :, (s // S5_GPS) * LANES:(s // S5_GPS + 1) * LANES]
                dst = S5_CH * gl
                moved = slab if src == dst else pltpu.roll(slab, (dst - src) % LANES, 1)
                acc = moved if acc is None else jnp.where(lane_grp == gl, moved, acc)
            o_ref[half, pl.ds(s, S5_TB, stride=S5_Q), :] = acc


def _s5_params(lam_re, lam_im, log_dt, b_re, b_im, c_re, c_im, d_skip):
    q = S5_Q
    dt = jnp.exp(log_dt)[..., None]
    lr, li = lam_re, lam_im
    mag = jnp.exp(lr * dt)
    a_re = mag * jnp.cos(li * dt)
    a_im = mag * jnp.sin(li * dt)
    den = lr * lr + li * li
    f_re = ((a_re - 1.0) * lr + a_im * li) / den
    f_im = (a_im * lr - (a_re - 1.0) * li) / den
    bb_re = f_re[..., None] * b_re - f_im[..., None] * b_im
    bb_im = f_re[..., None] * b_im + f_im[..., None] * b_re
    kk = jnp.arange(q + 1, dtype=F32)[:, None, None, None]
    pmag = jnp.exp(kk * (lr * dt))
    pw_re = pmag * jnp.cos(kk * (li * dt))
    pw_im = pmag * jnp.sin(kk * (li * dt))
    lw_re = pw_re[:q].transpose(1, 2, 0, 3)[:, :, :, None, :]
    lw_im = pw_im[:q].transpose(1, 2, 0, 3)[:, :, :, None, :]
    ck_re = c_re[:, :, None] * lw_re - c_im[:, :, None] * lw_im
    ck_im = c_re[:, :, None] * lw_im + c_im[:, :, None] * lw_re
    ck = jnp.concatenate([ck_re, -ck_im], axis=-1).reshape(2, S5_GROUPS, S5_BLK, 2 * S5_STATE)
    kern_t = jnp.einsum("dgmp,dgpc->dgcm", ck, jnp.concatenate([bb_re, bb_im], axis=2), precision=HI)
    kern_t = kern_t.reshape(2, S5_GROUPS, S5_CH, q, S5_CH)
    zeros = jnp.zeros_like(kern_t)
    bbt_re = bb_re.transpose(0, 1, 3, 2)[:, :, None]
    bbt_im = bb_im.transpose(0, 1, 3, 2)[:, :, None]
    ct_re = c_re.transpose(0, 1, 3, 2)[:, :, :, None, :]
    ct_im = c_im.transpose(0, 1, 3, 2)[:, :, :, None, :]
    ms, ps, gs = [], [], []
    for d in range(2):
        ext = (jnp.concatenate([zeros[d], kern_t[d]], axis=2) if d == 0
               else jnp.concatenate([kern_t[d, :, :, ::-1], zeros[d]], axis=2))
        ext = ext.reshape(S5_GROUPS, S5_CH, 2 * S5_BLK)
        ms.append(ext)
        pidx = (q - 1 - jnp.arange(q)) if d == 0 else jnp.arange(q)
        pr = pw_re[pidx, d].transpose(1, 0, 2)[:, :, None, :]
        pi = pw_im[pidx, d].transpose(1, 0, 2)[:, :, None, :]
        p_re = pr * bbt_re[d] - pi * bbt_im[d]
        p_im = pr * bbt_im[d] + pi * bbt_re[d]
        pd = jnp.stack([jnp.concatenate([p_re, p_im], axis=3), jnp.concatenate([p_im, p_re], axis=3)])
        ps.append(pd.reshape(2, S5_GROUPS, S5_BLK, 2 * S5_STATE))
        gidx = (jnp.arange(q) + 1) if d == 0 else (q - jnp.arange(q))
        gw_re = pw_re[gidx, d].transpose(1, 2, 0)[..., None]
        gw_im = pw_im[gidx, d].transpose(1, 2, 0)[..., None]
        g_re = ct_re[d] * gw_re - ct_im[d] * gw_im
        g_im = ct_re[d] * gw_im + ct_im[d] * gw_re
        gs.append(jnp.concatenate([g_re, -g_im], axis=1).reshape(S5_GROUPS, 2 * S5_STATE, S5_BLK))
    ar = jnp.concatenate([pw_re[q], pw_re[q]], axis=-1)[:, :, None, :]
    ai = jnp.stack([jnp.concatenate([-pw_im[q], pw_im[q]], axis=-1),
                    jnp.concatenate([pw_im[q], -pw_im[q]], axis=-1)], axis=1)[:, :, :, None, :]
    dsk = jnp.tile(d_skip.reshape(S5_GROUPS, 1, S5_CH), (1, 1, q))
    return (jnp.stack(ms), jnp.stack(ps).astype(BF16), jnp.stack(gs).astype(BF16),
            ar.astype(F32), ai.astype(F32), dsk.astype(F32))


def _s5(ug, params, nb, s_len, lc):
    m, p, g, ar, ai, dsk = params
    nc_tot = s_len // S5_Q
    nc_ctx = lc // S5_Q
    r = nb * nc_tot
    return pl.pallas_call(
        functools.partial(_s5_kernel, nb=nb, nc_ctx=nc_ctx, nc_tot=nc_tot),
        out_shape=jax.ShapeDtypeStruct((S5_GROUPS, r, S5_BLK), F32),
        grid=(S5_GROUPS,),
        in_specs=[pl.BlockSpec((1, r, S5_BLK), lambda gi: (gi, 0, 0)),
                  pl.BlockSpec((2, 1, S5_CH, 2 * S5_BLK), lambda gi: (0, gi, 0, 0)),
                  pl.BlockSpec((2, 2, 1, S5_BLK, 2 * S5_STATE), lambda gi: (0, 0, gi, 0, 0)),
                  pl.BlockSpec((2, 1, 2 * S5_STATE, S5_BLK), lambda gi: (0, gi, 0, 0)),
                  pl.BlockSpec((2, 1, 1, 2 * S5_STATE), lambda gi: (0, gi, 0, 0)),
                  pl.BlockSpec((2, 2, 1, 1, 2 * S5_STATE), lambda gi: (0, 0, gi, 0, 0)),
                  pl.BlockSpec((1, 1, S5_BLK), lambda gi: (gi, 0, 0))],
        out_specs=pl.BlockSpec((1, r, S5_BLK), lambda gi: (gi, 0, 0)),
        scratch_shapes=[pltpu.VMEM((2, 2, r, 2 * S5_STATE), F32), pltpu.VMEM((2, r, 2 * S5_STATE), F32),
                        pltpu.VMEM((2, S5_BLK, S5_BLK), BF16)],
        compiler_params=_cp(("parallel",)),
        name="s5_scan",
    )(ug, m, p, g, ar, ai, dsk)


CONV_ROWS = 4 * TM


def _conv_kernel(x_ref, prev_ref, next_ref, w_ref, b_ref, x_o, bc_o, *, s_len, lc):
    x = x_ref[...]
    rows = x.shape[0]
    ridx = lax.broadcasted_iota(jnp.int32, x.shape, 0)
    pos = (pl.program_id(0) * rows) % s_len + ridx
    pos = jnp.where(pos >= s_len, pos - s_len, pos)
    seg_first = jnp.logical_or(pos == 0, pos == lc)
    seg_last = jnp.logical_or(pos == lc - 1, pos == s_len - 1)
    xm = jnp.where(ridx == 0, prev_ref[SUBLANES - 1:SUBLANES, :], pltpu.roll(x, 1, 0))
    xp = jnp.where(ridx == rows - 1, next_ref[0:1, :], pltpu.roll(x, rows - 1, 0))
    xm = jnp.where(seg_first, 0.0, xm)
    xp = jnp.where(seg_last, 0.0, xp)
    y = _silu(xm * w_ref[0:1, :] + x * w_ref[1:2, :] + xp * w_ref[2:3, :] + b_ref[...])
    x_o[...] = y[:, :GROUP_W]
    bc_o[...] = y[:, GROUP_W:].astype(BF16)


def _conv(xbc, w, b, s_len, lc):
    t, c = xbc.shape
    rows = next(r for r in (CONV_ROWS, CONV_ROWS // 2, TM) if t % r == 0)
    per = rows // SUBLANES
    last = t // SUBLANES - 1
    return pl.pallas_call(
        functools.partial(_conv_kernel, s_len=s_len, lc=lc),
        out_shape=[jax.ShapeDtypeStruct((t, GROUP_W), F32), jax.ShapeDtypeStruct((t, c - GROUP_W), BF16)],
        grid=(t // rows,),
        in_specs=[pl.BlockSpec((rows, c), lambda i: (i, 0)),
                  pl.BlockSpec((SUBLANES, c), lambda i: (jnp.maximum(i * per - 1, 0), 0)),
                  pl.BlockSpec((SUBLANES, c), lambda i: (jnp.minimum((i + 1) * per, last), 0)),
                  pl.BlockSpec((3, c), lambda i: (0, 0)),
                  pl.BlockSpec((1, c), lambda i: (0, 0))],
        out_specs=[pl.BlockSpec((rows, GROUP_W), lambda i: (i, 0)), pl.BlockSpec((rows, c - GROUP_W), lambda i: (i, 0))],
        compiler_params=_cp(("parallel",)),
        name="ssd_conv",
    )(xbc, xbc, xbc, w, b.reshape(1, c))


_X_B = 0
_X_C = SSD_NGROUPS * SSD_STATE


def _ssd_kernel(xf_ref, bcf_ref, dtf_ref, dttf_ref, xr_ref, bcr_ref, dtr_ref, dttr_ref,
                bias_ref, a_ref, biast_ref, at_ref, dsk_ref, yf_ref, yr_ref, stf_ref, str_ref):
    @pl.when(pl.program_id(1) == 0)
    def _():
        stf_ref[...] = jnp.zeros_like(stf_ref)
        str_ref[...] = jnp.zeros_like(str_ref)

    par = (bias_ref[...], a_ref[...], biast_ref[...], at_ref[...], dsk_ref[...])
    for j in range(SSD_SUB):
        rf = slice(j * TQ, (j + 1) * TQ)
        rr = slice((SSD_SUB - 1 - j) * TQ, (SSD_SUB - j) * TQ)
        for b in range(xf_ref.shape[0]):
            yf_ref[b, rf, :] = _ssd_chunk_step(xf_ref[b, rf, :], bcf_ref[b, rf, :], dtf_ref[b, rf, :],
                                               dttf_ref[b, :, rf], par, stf_ref.at[b], False)
            yr_ref[b, rr, :] = _ssd_chunk_step(xr_ref[b, rr, :], bcr_ref[b, rr, :], dtr_ref[b, rr, :],
                                               dttr_ref[b, :, rr], par, str_ref.at[b], True)


def _ssd_chunk_step(x, xc, dt_raw, dtt_raw, par, st_ref, rev):
    bias, a_vec, biast, at_vec, dsk = par
    base = SSD_HEADS if rev else 0
    dt = _softplus(dt_raw + bias)
    a = dt * a_vec
    dtt = _softplus(dtt_raw + biast)
    at = dtt * at_vec
    ri = lax.broadcasted_iota(jnp.int32, (TQ, TQ), 0)
    ci = lax.broadcasted_iota(jnp.int32, (TQ, TQ), 1)
    causal = (ci >= ri) if rev else (ri >= ci)
    tri = jnp.where(causal, 1.0, 0.0).astype(BF16)
    cum_c = sum(_dot(tri, piece) for piece in _bf16_pieces(a))
    cum_r = sum(_dot_nt(piece, tri) for piece in _bf16_pieces(at))
    edge = 0 if rev else TQ - 1
    tot = cum_c[edge:edge + 1, :]

    shape = (TQ, GROUP_W)
    xdt = x * _per_head_cols(dt, base, SSD_HEADS, shape)
    lane = lax.broadcasted_iota(jnp.int32, shape, 1)
    y = jnp.zeros(shape, F32)
    bmat = [xc[:, _X_B + g * SSD_STATE:_X_B + (g + 1) * SSD_STATE] for g in range(SSD_NGROUPS)]
    cmat = [xc[:, _X_C + g * SSD_STATE:_X_C + (g + 1) * SSD_STATE] for g in range(SSD_NGROUPS)]
    cb = [_dot_nt(cmat[g], bmat[g]) for g in range(SSD_NGROUPS)]
    for h in range(SSD_HEADS):
        col = base + h
        seg = jnp.where(causal, cum_c[:, col:col + 1] - cum_r[col:col + 1, :], NEG_INF)
        scores = cb[h // 2] * jnp.exp(seg)
        xh = jnp.where((lane >= h * HEAD_DIM) & (lane < (h + 1) * HEAD_DIM), xdt, 0.0)
        y = y + _dot(scores.astype(BF16), xh.astype(BF16))
    st = st_ref[...]
    yo = jnp.concatenate(
        [_dot_nt(cmat[g], st[g * SSD_STATE:(g + 1) * SSD_STATE].astype(BF16)) for g in range(SSD_NGROUPS)], axis=1)
    y = y + yo * _per_head_cols(jnp.exp(cum_c), base, SSD_HEADS, shape)
    if not rev:
        y = y + x * dsk
    xd = xdt * _per_head_cols(jnp.exp(tot - cum_c), base, SSD_HEADS, shape)
    xdt_t = xd.T.astype(BF16)
    decay = jnp.exp(tot)
    for g in range(SSD_NGROUPS):
        new = _dot(xdt_t[g * SSD_STATE:(g + 1) * SSD_STATE], bmat[g])
        for hh in range(2):
            h = 2 * g + hh
            r0 = h * HEAD_DIM
            st_ref[r0:r0 + HEAD_DIM, :] = (decay[:, base + h:base + h + 1] * st[r0:r0 + HEAD_DIM]
                                           + new[hh * HEAD_DIM:(hh + 1) * HEAD_DIM])
    return y


def _bf16_pieces(x):
    hi = x.astype(BF16)
    rest = x - hi.astype(F32)
    mid = rest.astype(BF16)
    return hi, mid, (rest - mid.astype(F32)).astype(BF16)


def _ssd_chunk(c, rev, nc_ctx, nc_tot):
    if not rev:
        return c
    return jnp.where(c < nc_ctx, nc_ctx - 1 - c, nc_tot - 1 - (c - nc_ctx))


SSD_SUB = TM // TQ
SSD_NB = 4


def _ssd_scan(xc, dt, dtt, bias, a, biast, at, dsk, nb, s_len, lc):
    x, bc = xc
    t = x.shape[0]
    nblk = s_len // TM
    nctx = lc // TM
    nbs = math.gcd(nb, SSD_NB)
    fix = lambda b, c: (0, 0)
    x3 = x.reshape(nb, s_len, GROUP_W)
    bc3 = bc.reshape(nb, s_len, SSD_XBC - GROUP_W)
    dt3 = dt.reshape(nb, s_len, LANES)

    def rows(rev):
        return lambda b, c: (b, _ssd_chunk(c, rev, nctx, nblk), 0)

    def lanes(rev):
        return lambda b, c: (b, 0, _ssd_chunk(c, rev, nctx, nblk))

    def data_specs(rev):
        return [pl.BlockSpec((nbs, TM, GROUP_W), rows(rev)), pl.BlockSpec((nbs, TM, SSD_XBC - GROUP_W), rows(rev)),
                pl.BlockSpec((nbs, TM, LANES), rows(rev)), pl.BlockSpec((nbs, SUBLANES, TM), lanes(rev))]

    state = pltpu.VMEM((nbs, SSD_HEADS * HEAD_DIM, SSD_STATE), F32)
    yf, yr = pl.pallas_call(
        _ssd_kernel,
        out_shape=[jax.ShapeDtypeStruct((nb, s_len, GROUP_W), F32)] * 2,
        grid=(nb // nbs, nblk),
        in_specs=data_specs(False) + data_specs(True) + [
            pl.BlockSpec((1, LANES), fix), pl.BlockSpec((1, LANES), fix),
            pl.BlockSpec((SUBLANES, TQ), fix), pl.BlockSpec((SUBLANES, TQ), fix),
            pl.BlockSpec((1, GROUP_W), fix)],
        out_specs=[pl.BlockSpec((nbs, TM, GROUP_W), rows(False)), pl.BlockSpec((nbs, TM, GROUP_W), rows(True))],
        scratch_shapes=[state, state],
        compiler_params=_cp(("parallel", "arbitrary")),
        name="ssd_scan",
    )(x3, bc3, dt3, dtt, x3, bc3, dt3, dtt, bias, a, biast, at, dsk)
    return yf.reshape(t, GROUP_W), yr.reshape(t, GROUP_W)


def _ssd(xbc, dt, conv_w, conv_b, dt_bias, a_log, d_skip, nb, s_len, lc):
    xc = _conv(xbc, conv_w, conv_b, s_len, lc)
    nd = 2 * SSD_HEADS
    dtt = dt[:, :nd].reshape(nb, s_len, nd).transpose(0, 2, 1)
    bias = jnp.pad(dt_bias.reshape(1, nd), ((0, 0), (0, LANES - nd)))
    a = jnp.pad(-jnp.exp(a_log).reshape(1, nd), ((0, 0), (0, LANES - nd)))
    biast = jnp.broadcast_to(dt_bias.reshape(nd, 1), (nd, TQ))
    at = jnp.broadcast_to(-jnp.exp(a_log).reshape(nd, 1), (nd, TQ))
    dsk = jnp.repeat(d_skip, HEAD_DIM).reshape(1, GROUP_W)
    return _ssd_scan(xc, dt, dtt, bias, a, biast, at, dsk, nb, s_len, lc)


def _outproj_kernel(x_ref, ys5_ref, oga_ref, y0_ref, y1_ref, z_ref, owa_ref, *refs):
    mods, shared = refs[:ROW_SUB], refs[ROW_SUB:]
    for s in range(ROW_SUB):
        rows = _row_views((x_ref, oga_ref, y0_ref, y1_ref, z_ref, owa_ref) + tuple(shared[-4:-1]), s)
        _outproj_block(rows[0], ys5_ref.at[:, pl.ds(s * S5_TB, S5_TB), :], *rows[1:6], mods[s], *shared[:-4],
                       *rows[6:], shared[-1])


def _outproj_block(x_ref, ys5_ref, oga_ref, y0_ref, y1_ref, z_ref, owa_ref, mod_ref, gluw_ref, glub_ref,
                   ng_ref, wout_ref, n2_ref, wr_ref, br_ref, xn_o, h2_o, route_o, y_scr):
    _s5_unpack(ys5_ref, y_scr)
    gl = _gelu_tanh(jnp.concatenate([y_scr[0], y_scr[1]], axis=1))
    a = gl * _sigmoid(_dot(gl.astype(BF16), gluw_ref[...]) + glub_ref[...])
    m = (y0_ref[...] + y1_ref[...]) * _silu(z_ref[...])
    m = m * lax.rsqrt(jnp.mean(m * m, axis=-1, keepdims=True) + EPS) * ng_ref[...]
    w = wout_ref
    mix = (_dot(a.astype(BF16), w[0:GROUP_W, :]) + _dot(oga_ref[...].astype(BF16), w[GROUP_W:2 * GROUP_W, :])
           + _dot(m.astype(BF16), w[2 * GROUP_W:3 * GROUP_W, :]) + _dot(owa_ref[...].astype(BF16), w[3 * GROUP_W:, :]))
    xn = x_ref[...] + mod_ref[0, 2:3, :] * mix
    xn_o[...] = xn
    h2 = xn * lax.rsqrt(jnp.mean(xn * xn, axis=-1, keepdims=True) + EPS) * n2_ref[...]
    h2 = h2 * (1.0 + mod_ref[0, 4:5, :]) + mod_ref[0, 3:4, :]
    h2_o[...] = _pack_bf16_pair(h2)
    h_hi = h2.astype(BF16)
    h_lo = (h2 - h_hi.astype(F32)).astype(BF16)
    logits = _dot(h_hi, wr_ref[0]) + (_dot(h_lo, wr_ref[0]) + _dot(h_hi, wr_ref[1])) + br_ref[...]
    lane = lax.broadcasted_iota(jnp.int32, logits.shape, 1).astype(F32)
    big = float(4 * LANES)
    lcoarse = jnp.where(lane < MOE_GROUPS, logits, NEG_INF)
    mx = jnp.max(lcoarse, axis=1, keepdims=True)
    den = jnp.sum(jnp.exp(lcoarse - mx), axis=1, keepdims=True)
    grp = jnp.min(jnp.where(lcoarse == mx, lane, big), axis=1, keepdims=True)
    pg = 1.0 / den
    lo = ROUTE_FINE0 + grp * MOE_PER_GROUP
    lf = jnp.where(lane >= lo, jnp.where(lane < lo + MOE_PER_GROUP, logits, NEG_INF), NEG_INF)
    v1 = jnp.max(lf, axis=1, keepdims=True)
    i1 = jnp.min(jnp.where(lf == v1, lane, big), axis=1, keepdims=True)
    lf2 = jnp.where(lane == i1, NEG_INF, lf)
    v2 = jnp.max(lf2, axis=1, keepdims=True)
    i2 = jnp.min(jnp.where(lf2 == v2, lane, big), axis=1, keepdims=True)
    e2 = jnp.exp(v2 - v1)
    w1 = pg / (1.0 + e2)
    w2 = w1 * e2
    route = jnp.where(lane == 0, i1 - ROUTE_FINE0,
                      jnp.where(lane == 1, i2 - ROUTE_FINE0,
                                jnp.where(lane == 2, w1, jnp.where(lane == 3, w2, 0.0))))
    route_o[...] = route


def _outproj(x, ys5, oga, y0, y1, z, owa, mod, glu_w, glu_b, ssd_norm_g, w_out, norm2_g, wr, br, nb, nblk):
    t, d = x.shape
    row = lambda i: (i, 0)
    fix = lambda i: (0, 0)
    step = ROW_SUB * TM
    assert t % step == 0
    gw = pl.BlockSpec((step, GROUP_W), row)
    wr_hi = wr.astype(BF16)
    mod_specs = [pl.BlockSpec((1, 6, d), lambda i, s=s: (_mod_row(ROW_SUB * i + s, nblk, nb), 0, 0))
                 for s in range(ROW_SUB)]
    return pl.pallas_call(
        _outproj_kernel,
        out_shape=[jax.ShapeDtypeStruct((t, d), F32), jax.ShapeDtypeStruct((t, d // 2), jnp.uint32),
                   jax.ShapeDtypeStruct((t, LANES), F32)],
        grid=(t // step,),
        in_specs=[pl.BlockSpec((step, d), row),
                  pl.BlockSpec((S5_GROUPS, ROW_SUB * S5_TB, S5_BLK), lambda i: (0, i, 0)),
                  gw, gw, gw, gw, gw] + mod_specs + [
                  pl.BlockSpec((GROUP_W, GROUP_W), fix),
                  pl.BlockSpec((1, GROUP_W), fix),
                  pl.BlockSpec((1, GROUP_W), fix),
                  pl.BlockSpec((d, d), fix),
                  pl.BlockSpec((1, d), fix),
                  pl.BlockSpec((2, d, LANES), lambda i: (0, 0, 0)),
                  pl.BlockSpec((1, LANES), fix)],
        out_specs=[pl.BlockSpec((step, d), row), pl.BlockSpec((step, d // 2), row), pl.BlockSpec((step, LANES), row)],
        scratch_shapes=[pltpu.VMEM((GROUP_W // LANES, TM, LANES), F32)],
        compiler_params=_cp(("parallel",)),
        name="out_proj_router",
    )(x, ys5, oga, y0, y1, z, owa, *([mod] * ROW_SUB), glu_w.astype(BF16), glu_b.reshape(1, -1), ssd_norm_g.reshape(1, -1),
      w_out.astype(BF16), norm2_g.reshape(1, -1), jnp.stack([wr_hi, (wr - wr_hi.astype(F32)).astype(BF16)]), br)


def _pack_router(coarse_w, coarse_b, fine_w, fine_b):
    def lanes(coarse, fine):
        gap = jnp.zeros(coarse.shape[:-1] + (ROUTE_FINE0 - MOE_GROUPS,), F32)
        tail = jnp.zeros(coarse.shape[:-1] + (LANES - ROUTE_FINE0 - N_EXPERTS,), F32)
        return jnp.concatenate([coarse, gap, fine, tail], axis=-1)

    return lanes(coarse_w, fine_w), lanes(coarse_b[None, :], fine_b[None, :])


def _gather_rows(src, idx):
    m = idx.shape[0]
    d = src.shape[1]
    workers = SC_CORES * SC_SUBCORES
    k = SC_FETCH_K
    nch = m // (workers * k)
    assert nch * workers * k == m
    mesh = plsc.VectorSubcoreMesh(core_axis_name="c", subcore_axis_name="s")

    @functools.partial(
        pl.kernel, mesh=mesh,
        out_type=jax.ShapeDtypeStruct((m, d), src.dtype),
        scratch_types=[pltpu.VMEM((nch, k), jnp.int32),
                       pltpu.VMEM((k, d), src.dtype),
                       pltpu.SemaphoreType.DMA],
    )
    def gather(src_hbm, idx_hbm, out_hbm, idx_v, rows_v, sem):
        wid = lax.axis_index("s") * SC_CORES + lax.axis_index("c")
        pltpu.sync_copy(idx_hbm.at[wid], idx_v)

        @pl.loop(0, nch)
        def _(j):
            off = pl.multiple_of((wid * nch + j) * k, k)
            pltpu.async_copy(src_hbm.at[idx_v.at[j]], rows_v, sem).wait()
            pltpu.sync_copy(rows_v, out_hbm.at[pl.ds(off, k)])

    return gather(src, idx.reshape(workers, nch, k))


def _scatter_rows(src, dst0, dst1, nrows):
    t, d = src.shape
    workers = SC_CORES * SC_SUBCORES
    nch = t // (workers * SC_GATHER_K)
    assert nch * workers * SC_GATHER_K == t
    mesh = plsc.VectorSubcoreMesh(core_axis_name="c", subcore_axis_name="s")

    @functools.partial(
        pl.kernel, mesh=mesh,
        out_type=jax.ShapeDtypeStruct((nrows, d), src.dtype),
        scratch_types=[pltpu.VMEM((nch, SC_GATHER_K), jnp.int32),
                       pltpu.VMEM((nch, SC_GATHER_K), jnp.int32),
                       pltpu.VMEM((SC_GATHER_K, d), src.dtype),
                       pltpu.SemaphoreType.DMA((2,))],
    )
    def scatter(src_hbm, d0_hbm, d1_hbm, out_hbm, i0_v, i1_v, rows_v, sem):
        wid = lax.axis_index("s") * SC_CORES + lax.axis_index("c")
        pltpu.sync_copy(d0_hbm.at[wid], i0_v)
        pltpu.sync_copy(d1_hbm.at[wid], i1_v)

        @pl.loop(0, nch)
        def _(j):
            off = pl.multiple_of((wid * nch + j) * SC_GATHER_K, SC_GATHER_K)
            pltpu.sync_copy(src_hbm.at[pl.ds(off, SC_GATHER_K)], rows_v)
            first = pltpu.async_copy(rows_v, out_hbm.at[i0_v.at[j]], sem.at[0])
            second = pltpu.async_copy(rows_v, out_hbm.at[i1_v.at[j]], sem.at[1])
            first.wait()
            second.wait()

    return scatter(src, dst0.reshape(workers, nch, SC_GATHER_K), dst1.reshape(workers, nch, SC_GATHER_K))


def _expert_kernel(be_ref, nused_ref, nvalid_ref, nxt_ref, slot_ref, x_ref, wg_hbm, wu_hbm, wd_hbm, o_ref,
                   wg_f, wu_f, wd_f, wg_s, wu_s, wd_s, sem, *, layer):
    i = pl.program_id(0)
    used = i < nused_ref[0]
    new_expert = jnp.logical_or(i == 0, be_ref[i] != be_ref[jnp.maximum(i - 1, 0)])

    def weight_copies(expert, slot):
        return [pltpu.make_async_copy(w.at[layer, expert], f.at[slot], sem.at[slot, j])
                for j, (w, f) in enumerate(((wg_hbm, wg_f), (wu_hbm, wu_f), (wd_hbm, wd_f)))]

    @pl.when(jnp.logical_and(used, new_expert))
    def _():
        slot = slot_ref[i]

        @pl.when(i == 0)
        def _():
            for c in weight_copies(be_ref[i], slot):
                c.start()

        for c in weight_copies(be_ref[i], slot):
            c.wait()
        wg_s[...] = wg_f[slot].astype(BF16)
        wu_s[...] = wu_f[slot].astype(BF16)
        wd_s[...] = wd_f[slot].astype(BF16)

        @pl.when(nxt_ref[i] >= 0)
        def _():
            for c in weight_copies(nxt_ref[i], 1 - slot):
                c.start()

    def swiglu(rows):
        row = rows.start + lax.broadcasted_iota(jnp.int32, (rows.stop - rows.start, x_ref.shape[1]), 0)
        lo, hi = _unpack_bf16_pair(jnp.where(row < nvalid_ref[i], x_ref[rows, :], jnp.uint32(0)))
        lo = lo.astype(BF16)
        hi = hi.astype(BF16)
        half = lo.shape[1]
        gate = _dot(lo, wg_s[0:half, :]) + _dot(hi, wg_s[half:, :])
        up = _dot(lo, wu_s[0:half, :]) + _dot(hi, wu_s[half:, :])
        o_ref[rows, :] = _pack_bf16_pair(_dot((_silu(gate) * up).astype(BF16), wd_s[...]))

    used = i < nused_ref[0]
    half_rows = MOE_TM // 2

    @pl.when(jnp.logical_and(used, nvalid_ref[i] > half_rows))
    def _():
        swiglu(slice(0, MOE_TM))

    @pl.when(jnp.logical_and(used, nvalid_ref[i] <= half_rows))
    def _():
        swiglu(slice(0, half_rows))
        o_ref[half_rows:, :] = jnp.zeros((MOE_TM - half_rows, o_ref.shape[1]), o_ref.dtype)


def _experts(xs, blk_e, n_used, n_valid, nxt_e, slot, wg, wu, wd, layer):
    rows, dp = xs.shape
    d = 2 * dp
    nblocks = rows // MOE_TM
    de = wg.shape[3]
    blk = lambda i, be, nu, *_: (jnp.minimum(i, nu[0] - 1), 0)
    hbm = pl.BlockSpec(memory_space=pl.ANY)
    grid_spec = pltpu.PrefetchScalarGridSpec(
        num_scalar_prefetch=5,
        grid=(nblocks,),
        in_specs=[pl.BlockSpec((MOE_TM, dp), blk), hbm, hbm, hbm],
        out_specs=pl.BlockSpec((MOE_TM, dp), blk),
        scratch_shapes=[pltpu.VMEM((2, d, de), F32), pltpu.VMEM((2, d, de), F32), pltpu.VMEM((2, de, d), F32),
                        pltpu.VMEM((d, de), BF16), pltpu.VMEM((d, de), BF16), pltpu.VMEM((de, d), BF16),
                        pltpu.SemaphoreType.DMA((2, 3))],
    )
    return pl.pallas_call(
        functools.partial(_expert_kernel, layer=layer),
        out_shape=jax.ShapeDtypeStruct((rows, dp), jnp.uint32),
        grid_spec=grid_spec,
        compiler_params=_cp(("arbitrary",)),
        name="moe_experts",
    )(blk_e, n_used, n_valid, nxt_e, slot, xs, wg, wu, wd)


def _final_kernel(*refs):
    fg_ref, o_ref = refs[-2:]
    for s in range(ROW_SUB):
        y = _moe_residual(*refs[5 * s:5 * s + 5])
        o_ref[s * TM:(s + 1) * TM, :] = y * lax.rsqrt(jnp.mean(y * y, axis=-1, keepdims=True) + EPS) * fg_ref[...]


def _final(xn, rows2, route, mod, final_g, nb, nblk):
    t, d = xn.shape
    nlat = nblk - 1
    assert (nb * nlat) % ROW_SUB == 0

    def blk_specs(s):
        lat = lambda i: ROW_SUB * i + s
        src = lambda i: ((lat(i) // nlat) * nblk + 1 + lat(i) % nlat, 0)
        return [pl.BlockSpec((TM, d), src),
                pl.BlockSpec((TM, d // 2), src),
                pl.BlockSpec((TM, d // 2), lambda i: (src(i)[0] + t // TM, 0)),
                pl.BlockSpec((TM, LANES), src),
                pl.BlockSpec((1, 6, d), lambda i: (lat(i) // nlat, 0, 0))]

    return pl.pallas_call(
        _final_kernel,
        out_shape=jax.ShapeDtypeStruct((nb * nlat * TM, d), F32),
        grid=(nb * nlat // ROW_SUB,),
        in_specs=[sp for s in range(ROW_SUB) for sp in blk_specs(s)] + [pl.BlockSpec((1, d), lambda i: (0, 0))],
        out_specs=pl.BlockSpec((ROW_SUB * TM, d), lambda i: (i, 0)),
        compiler_params=_cp(("parallel",)),
        name="moe_combine_final",
    )(*((xn, rows2, rows2, route, mod) * ROW_SUB), final_g.reshape(1, d))


def _moe(h2, route, wg, wu, wd, layer):
    t, d = h2.shape
    n_slots = 2 * t
    experts = jnp.arange(N_EXPERTS, dtype=F32)[None, :]
    oh0 = (route[:, 0:1] == experts).astype(F32)
    oh1 = (route[:, 1:2] == experts).astype(F32)
    both = (oh0 + oh1).reshape(t // LANES, LANES, N_EXPERTS)
    tri = jnp.tril(jnp.ones((LANES, LANES), F32))
    intra = jnp.einsum("ij,bjk->bik", tri, both)
    blk_tot = intra[:, -1, :]
    blk_cum = jnp.cumsum(blk_tot, axis=0)
    earlier = (intra - both + (blk_cum - blk_tot)[:, None, :]).reshape(t, N_EXPERTS)
    counts = blk_cum[-1].astype(jnp.int32)
    pcounts = (counts + MOE_TM - 1) // MOE_TM * MOE_TM
    pends = jnp.cumsum(pcounts)
    pstarts = pends - pcounts
    base = pstarts.astype(F32)[None, :] + earlier
    dest0 = jnp.sum(oh0 * base, axis=1).astype(jnp.int32)
    dest1 = jnp.sum(oh1 * base, axis=1).astype(jnp.int32)
    nblocks = -(-n_slots // MOE_TM) + N_EXPERTS
    nrows = nblocks * MOE_TM
    blk_start = jnp.arange(nblocks, dtype=jnp.int32) * MOE_TM
    blk_e = jnp.minimum(jnp.sum((pends[None, :] <= blk_start[:, None]).astype(jnp.int32), axis=1), N_EXPERTS - 1)
    n_used = (pends[-1] // MOE_TM).astype(jnp.int32).reshape(1)
    n_valid = jnp.clip((pstarts + counts)[blk_e] - blk_start, 0, MOE_TM).astype(jnp.int32)
    ids = jnp.arange(N_EXPERTS, dtype=jnp.int32)
    has = counts > 0
    later = lax.cummin(jnp.where(has, ids, N_EXPERTS)[::-1])[::-1]
    nxt = jnp.concatenate([later[1:], jnp.full((1,), N_EXPERTS, jnp.int32)])
    nxt = jnp.where(nxt >= N_EXPERTS, -1, nxt)
    slot = (jnp.cumsum(has.astype(jnp.int32)) - 1) % 2
    xs = _scatter_rows(h2, dest0, dest1, nrows)
    ys = _experts(xs, blk_e, n_used, n_valid, nxt[blk_e], slot[blk_e], wg, wu, wd, layer)
    return _gather_rows(ys, jnp.concatenate([dest0, dest1]))


def kernel(x, c, ctx, c_ctx, ada_w, ada_b, norm1_g, norm2_g, w_in, w_out, s5_lam_re, s5_lam_im, s5_log_dt, s5_b_re, s5_b_im, s5_c_re, s5_c_im, s5_d, s5_glu_w, s5_glu_b, ga_qn_g, ga_kn_g, ssd_conv_w, ssd_conv_b, ssd_dt_bias, ssd_a_log, ssd_d, ssd_norm_g, wa_sink, moe_coarse_w, moe_coarse_b, moe_fine_w, moe_fine_b, moe_w_gate, moe_w_up, moe_w_down, final_g):
    nb, l, d = x.shape
    lc = ctx.shape[1]
    depth = ada_w.shape[0]
    assert lc == TM and l % TM == 0 and nb <= SUBLANES - 1 and d == D_MODEL
    s_len = lc + l
    nblk = s_len // TM
    t = nb * s_len

    cc = jnp.zeros((SUBLANES, d), F32).at[:nb].set(c).at[nb].set(c_ctx)
    mods = _ada(cc, ada_w, ada_b).reshape(depth, SUBLANES, 6, d)
    cos_t, sin_t = _rope_tables(lc, l)
    w_packed = jax.vmap(_pack_w_in)(w_in)
    s5_tabs = jax.vmap(_s5_params)(s5_lam_re, s5_lam_im, s5_log_dt, s5_b_re, s5_b_im, s5_c_re, s5_c_im, s5_d)
    wrs, brs = jax.vmap(_pack_router)(moe_coarse_w, moe_coarse_b, moe_fine_w, moe_fine_b)

    src = ("first", x.reshape(nb * l, d), ctx.reshape(nb * lc, d))
    for i in range(depth):
        mod = mods[i]
        (xm, xbc, ug, z, dt, gaq, gak, gav, waq, wak, wav) = _inproj(
            src, mod, norm1_g[i], w_packed[i], cos_t, sin_t, ga_qn_g[i], ga_kn_g[i], nb, nblk)
        ys5 = _s5(ug, tuple(tab[i] for tab in s5_tabs), nb, s_len, lc)
        oga, owa = _attn(wa_sink[i], gaq, gak, gav, waq, wak, wav, nb, s_len, lc)
        y0, y1 = _ssd(xbc, dt, ssd_conv_w[i], ssd_conv_b[i], ssd_dt_bias[i], ssd_a_log[i], ssd_d[i], nb, s_len, lc)
        wr, br = wrs[i], brs[i]
        xn, h2, route = _outproj(xm, ys5, oga, y0, y1, z, owa, mod, s5_glu_w[i], s5_glu_b[i], ssd_norm_g[i],
                                 w_out[i], norm2_g[i], wr, br, nb, nblk)
        rows2 = _moe(h2, route, moe_w_gate, moe_w_up, moe_w_down, i)
        src = ("moe", xn, rows2, route, mod)
    return _final(xn, rows2, route, mod, final_g, nb, nblk).reshape(nb, l, d)
```
